```python
import jax
import jax.numpy as jnp
from jax import lax
import numpy as np

D_MODEL = 1024
BATCH = 4
SEQ = 4096
DEPTH = 1

N_HEADS = 8
HEAD_DIM = 64
ATTN_WIDTH = N_HEADS * HEAD_DIM
MOBA_BLOCK = 256
MOBA_TOPK = 3
Q_CHUNK = 32
SGU_CHUNK = 128
SGU_GROUPS = 8
SGU_WIDTH = D_MODEL // 2
SGU_GROUP_DIM = SGU_WIDTH // SGU_GROUPS
N_BRANCHES = 2
IN_WIDTH = 3 * ATTN_WIDTH + 2 * SGU_WIDTH + N_BRANCHES * D_MODEL
N_EXPERTS = 32
EXPERT_TOPK = 4
D_EXPERT = D_MODEL
SWIGLU_LIMIT = 7.0
SWIGLU_ALPHA = 1.702
ROW_BLOCK = 128
NORM_EPS = 1e-5

kernel_name = "hybrid_moba_gmlp_moe_layer"


def rms_norm(x, g):
    xf = x.astype(jnp.float32)
    y = xf * lax.rsqrt(jnp.mean(xf * xf, axis=-1, keepdims=True) + NORM_EPS)
    return (y * g.astype(jnp.float32)).astype(x.dtype)


def layer_norm(x, g, b):
    xf = x.astype(jnp.float32)
    mu = jnp.mean(xf, axis=-1, keepdims=True)
    var = jnp.mean(jnp.square(xf - mu), axis=-1, keepdims=True)
    y = (xf - mu) * lax.rsqrt(var + NORM_EPS) * g.astype(jnp.float32) + b.astype(jnp.float32)
    return y.astype(x.dtype)


def alibi_slopes(n_heads):
    return 2.0 ** (-8.0 * jnp.arange(1, n_heads + 1, dtype=jnp.float32) / n_heads)


def moba_attention(q, k, v):
    b, s, h, dh = q.shape
    nb = -(-s // MOBA_BLOCK)
    s_pad = nb * MOBA_BLOCK
    pad = [(0, 0), (0, s_pad - s), (0, 0), (0, 0)]
    q, k, v = [jnp.pad(t, pad).transpose(0, 2, 1, 3) for t in (q, k, v)]
    k_blk = k.reshape(b, h, nb, MOBA_BLOCK, dh)
    v_blk = v.reshape(b, h, nb, MOBA_BLOCK, dh)
    k_mean = jnp.mean(k_blk.astype(jnp.float32), axis=3)
    pos = jnp.arange(s_pad, dtype=jnp.int32)
    q_blk_id = pos // MOBA_BLOCK
    gate = jnp.einsum('bhtd,bhnd->bhtn', q.astype(jnp.float32), k_mean)
    past = jnp.arange(nb, dtype=jnp.int32)[None, :] < q_blk_id[:, None]
    gate = jnp.where(past, gate, -jnp.inf)
    n_sel = min(MOBA_TOPK, nb)
    _, sel_idx = lax.top_k(gate, n_sel)
    sel_valid = jnp.arange(n_sel, dtype=jnp.int32)[None, :] < q_blk_id[:, None]

    n_chunks = s_pad // Q_CHUNK
    slopes = alibi_slopes(h)
    scale = dh ** -0.5
    q_c = q.reshape(b, h, n_chunks, Q_CHUNK, dh).transpose(2, 0, 1, 3, 4)
    idx_c = sel_idx.reshape(b, h, n_chunks, Q_CHUNK, n_sel).transpose(2, 0, 1, 3, 4)
    valid_c = sel_valid.reshape(n_chunks, Q_CHUNK, n_sel)
    b_ix = jnp.arange(b)[:, None, None, None]
    h_ix = jnp.arange(h)[None, :, None, None]
    offs = jnp.arange(MOBA_BLOCK, dtype=jnp.int32)

    def one_chunk(args):
        c, qc, ic, vc = args
        t = c * Q_CHUNK + jnp.arange(Q_CHUNK, dtype=jnp.int32)
        own = (c * Q_CHUNK) // MOBA_BLOCK
        k_sel = k_blk[b_ix, h_ix, ic]
        v_sel = v_blk[b_ix, h_ix, ic]
        k_own = lax.dynamic_index_in_dim(k_blk, own, axis=2, keepdims=False)
        v_own = lax.dynamic_index_in_dim(v_blk, own, axis=2, keepdims=False)
        s_sel = jnp.einsum('bhqd,bhqnld->bhqnl', qc, k_sel, preferred_element_type=jnp.float32) * scale
        s_own = jnp.einsum('bhqd,bhld->bhql', qc, k_own, preferred_element_type=jnp.float32) * scale
        sel_pos = ic[..., None] * MOBA_BLOCK + offs
        own_pos = own * MOBA_BLOCK + offs
        s_sel = s_sel - slopes[:, None, None, None] * (t[:, None, None] - sel_pos).astype(jnp.float32)
        s_sel = jnp.where(vc[None, None, :, :, None], s_sel, -jnp.inf)
        s_own = s_own - slopes[:, None, None] * (t[:, None] - own_pos[None, :]).astype(jnp.float32)
        s_own = jnp.where(own_pos[None, :] <= t[:, None], s_own, -jnp.inf)
        scores = jnp.concatenate([s_sel.reshape(b, h, Q_CHUNK, n_sel * MOBA_BLOCK), s_own], axis=-1)
        p = jax.nn.softmax(scores, axis=-1).astype(v.dtype)
        p_sel = p[..., :n_sel * MOBA_BLOCK].reshape(b, h, Q_CHUNK, n_sel, MOBA_BLOCK)
        p_own = p[..., n_sel * MOBA_BLOCK:]
        return (jnp.einsum('bhqnl,bhqnld->bhqd', p_sel, v_sel)
                + jnp.einsum('bhql,bhld->bhqd', p_own, v_own))

    out = lax.map(one_chunk, (jnp.arange(n_chunks, dtype=jnp.int32), q_c, idx_c, valid_c))
    out = out.transpose(1, 0, 3, 2, 4).reshape(b, s_pad, h * dh)
    return out[:, :s]


def spatial_gating(z, ln_g, ln_b, w_s, b_s):
    u, vv = jnp.split(z, 2, axis=-1)
    vv = layer_norm(vv, ln_g, ln_b)
    b, s, _ = vv.shape
    nc = s // SGU_CHUNK
    vv = vv.reshape(b, nc, SGU_CHUNK, SGU_GROUPS, SGU_GROUP_DIM)
    causal = jnp.tril(jnp.ones((SGU_CHUNK, SGU_CHUNK), dtype=bool))
    w = jnp.where(causal[None], w_s, jnp.zeros_like(w_s))
    mixed = jnp.einsum('gts,bcsgd->bctgd', w, vv) + b_s.T[None, None, :, :, None]
    return u * mixed.reshape(b, s, SGU_WIDTH)


def clamped_swiglu(gu):
    gate, up = gu[..., 0::2], gu[..., 1::2]
    gate = jnp.minimum(gate, SWIGLU_LIMIT)
    up = jnp.clip(up, -SWIGLU_LIMIT, SWIGLU_LIMIT)
    return gate * jax.nn.sigmoid(SWIGLU_ALPHA * gate) * (up + 1.0)


def moe_ffn(x, w_router, b_router, w_gate_up, b_gate_up, w_down, b_down):
    b, s, d = x.shape
    n_tok = b * s
    xt = x.reshape(n_tok, d)
    logits = (xt @ w_router + b_router).astype(jnp.float32)
    top_val, top_idx = lax.top_k(logits, EXPERT_TOPK)
    top_w = jax.nn.softmax(top_val, axis=-1)
    n_assign = n_tok * EXPERT_TOPK
    flat_e = top_idx.reshape(n_assign).astype(jnp.int32)
    flat_tok = jnp.repeat(jnp.arange(n_tok, dtype=jnp.int32), EXPERT_TOPK)
    flat_w = top_w.reshape(n_assign)
    order = jnp.argsort(flat_e)
    e_sorted = flat_e[order]
    counts = jnp.bincount(flat_e, length=N_EXPERTS).astype(jnp.int32)
    padded = (counts + ROW_BLOCK - 1) // ROW_BLOCK * ROW_BLOCK
    pad_end = jnp.cumsum(padded)
    pad_start = pad_end - padded
    grp_start = jnp.cumsum(counts) - counts
    dest = pad_start[e_sorted] + jnp.arange(n_assign, dtype=jnp.int32) - grp_start[e_sorted]
    n_blocks = -(-(n_assign + N_EXPERTS * (ROW_BLOCK - 1)) // ROW_BLOCK)
    n_rows = n_blocks * ROW_BLOCK
    row_tok = jnp.zeros((n_rows,), jnp.int32).at[dest].set(flat_tok[order])
    row_w = jnp.zeros((n_rows,), jnp.float32).at[dest].set(flat_w[order])
    block_start = jnp.arange(n_blocks, dtype=jnp.int32) * ROW_BLOCK
    block_expert = jnp.minimum(jnp.searchsorted(pad_end, block_start, side='right'), N_EXPERTS - 1)
    x_rows = xt[row_tok].reshape(n_blocks, ROW_BLOCK, d)

    def expert_block(args):
        e, xb = args
        hid = clamped_swiglu(xb @ w_gate_up[e] + b_gate_up[e])
        return hid @ w_down[e] + b_down[e]

    y_rows = lax.map(expert_block, (block_expert, x_rows)).reshape(n_rows, d)
    out = jax.ops.segment_sum(y_rows.astype(jnp.float32) * row_w[:, None], row_tok, num_segments=n_tok)
    return out.astype(x.dtype).reshape(b, s, d)


def setup_inputs(seed: int = 0) -> dict:
    key = jax.random.key(seed)
    ks = jax.random.split(key, 20)
    f32 = jnp.float32
    nrm = lambda k, shape, fan_in: jax.random.normal(k, shape, f32) * (fan_in ** -0.5)
    L = DEPTH
    return {
        "x": jax.random.normal(ks[0], (BATCH, SEQ, D_MODEL), f32),
        "mix_norm_g": 1.0 + 0.05 * jax.random.normal(ks[1], (L, D_MODEL), f32),
        "w_in": nrm(ks[2], (L, D_MODEL, IN_WIDTH), D_MODEL),
        "w_attn_out": nrm(ks[3], (L, ATTN_WIDTH, D_MODEL), ATTN_WIDTH),
        "sgu_ln_g": 1.0 + 0.05 * jax.random.normal(ks[4], (L, SGU_WIDTH), f32),
        "sgu_ln_b": 0.01 * jax.random.normal(ks[5], (L, SGU_WIDTH), f32),
        "w_spatial": nrm(ks[6], (L, SGU_GROUPS, SGU_CHUNK, SGU_CHUNK), SGU_CHUNK),
        "b_spatial": 1.0 + 0.01 * jax.random.normal(ks[7], (L, SGU_GROUPS, SGU_CHUNK), f32),
        "w_sgu_out": nrm(ks[8], (L, SGU_WIDTH, D_MODEL), SGU_WIDTH),
        "w_mix_out": nrm(ks[9], (L, D_MODEL, D_MODEL), D_MODEL),
        "ffn_norm_g": 1.0 + 0.05 * jax.random.normal(ks[10], (L, D_MODEL), f32),
        "w_router": nrm(ks[11], (L, D_MODEL, N_EXPERTS), D_MODEL),
        "b_router": 0.01 * jax.random.normal(ks[12], (L, N_EXPERTS), f32),
        "w_gate_up": nrm(ks[13], (L, N_EXPERTS, D_MODEL, 2 * D_EXPERT), D_MODEL),
        "b_gate_up": 0.01 * jax.random.normal(ks[14], (L, N_EXPERTS, 2 * D_EXPERT), f32),
        "w_down": nrm(ks[15], (L, N_EXPERTS, D_EXPERT, D_MODEL), D_EXPERT),
        "b_down": 0.01 * jax.random.normal(ks[16], (L, N_EXPERTS, D_MODEL), f32),
        "final_norm_g": 1.0 + 0.05 * jax.random.normal(ks[17], (D_MODEL,), f32),
    }


def reference(x, mix_norm_g, w_in, w_attn_out, sgu_ln_g, sgu_ln_b, w_spatial, b_spatial, w_sgu_out,
              w_mix_out, ffn_norm_g, w_router, b_router, w_gate_up, b_gate_up, w_down, b_down,
              final_norm_g):
    b, s, _ = x.shape
    h = x
    splits = [ATTN_WIDTH, 2 * ATTN_WIDTH, 3 * ATTN_WIDTH, 3 * ATTN_WIDTH + 2 * SGU_WIDTH]
    for layer in range(DEPTH):
        xn = rms_norm(h, mix_norm_g[layer])
        proj = xn @ w_in[layer]
        q, k, v, z, gates = jnp.split(proj, splits, axis=-1)
        q = q.reshape(b, s, N_HEADS, HEAD_DIM)
        k = k.reshape(b, s, N_HEADS, HEAD_DIM)
        v = v.reshape(b, s, N_HEADS, HEAD_DIM)
        y_attn = moba_attention(q, k, v) @ w_attn_out[layer]
        y_sgu = spatial_gating(jax.nn.gelu(z, approximate=False), sgu_ln_g[layer], sgu_ln_b[layer],
                               w_spatial[layer], b_spatial[layer]) @ w_sgu_out[layer]
        g_attn, g_sgu = jnp.split(jax.nn.sigmoid(gates), 2, axis=-1)
        h = h + (g_attn * y_attn + g_sgu * y_sgu) @ w_mix_out[layer]
        h = h + moe_ffn(rms_norm(h, ffn_norm_g[layer]), w_router[layer], b_router[layer],
                        w_gate_up[layer], b_gate_up[layer], w_down[layer], b_down[layer])
    return rms_norm(h, final_norm_g)
```

```python
import functools

import jax
import jax.numpy as jnp
from jax import lax
from jax.experimental import pallas as pl
from jax.experimental.pallas import tpu as pltpu

F32 = jnp.float32
BF16 = jnp.bfloat16
NEG_INF = float("-inf")

N_HEADS = 8
HEAD_DIM = 64
ATTN_WIDTH = N_HEADS * HEAD_DIM
MOBA_BLOCK = 256
MOBA_TOPK = 3
SGU_CHUNK = 128
SGU_GROUPS = 8
N_EXPERTS = 32
EXPERT_TOPK = 4
SWIGLU_LIMIT = 7.0
SWIGLU_ALPHA = 1.702
NORM_EPS = 1e-5

LANES = 128
HEADS_PER_LANE_TILE = LANES // HEAD_DIM
PROJ_ROWS = 256
EXPERT_ROWS = 256
DISPATCH_TOKENS = 512
COMBINE_TOKENS = 128
VMEM_LIMIT = 48 * 1024 * 1024


def _sigmoid(x):
    return 1.0 / (1.0 + jnp.exp(-x))


def _gelu_exact(x):
    return 0.5 * x * (1.0 + lax.erf(x * (0.5 ** 0.5)))


def _in_proj_kernel(x_ref, g_ref, w_ref, lng_ref, lnb_ref,
                    q_ref, k_ref, v_ref, kmean_ref, u_ref, vvn_ref, ga_ref, gs_ref):
    x = x_ref[...]
    xn = x * lax.rsqrt(jnp.mean(x * x, axis=-1, keepdims=True) + NORM_EPS) * g_ref[...]
    xb = xn.astype(BF16)

    def proj(lo, hi):
        return jnp.dot(xb, w_ref[:, lo:hi], preferred_element_type=F32)

    a = ATTN_WIDTH
    sw = u_ref.shape[1]
    d = ga_ref.shape[1]
    q_ref[...] = proj(0, a)
    kf = proj(a, 2 * a)
    k_ref[...] = kf.astype(BF16)
    for j in range(kf.shape[0] // MOBA_BLOCK):
        kmean_ref[j] = jnp.mean(kf[j * MOBA_BLOCK:(j + 1) * MOBA_BLOCK], axis=0, keepdims=True)
    v_ref[...] = proj(2 * a, 3 * a).astype(BF16)
    z0 = 3 * a
    u_ref[...] = _gelu_exact(proj(z0, z0 + sw))
    zv = _gelu_exact(proj(z0 + sw, z0 + 2 * sw))
    mu = jnp.mean(zv, axis=-1, keepdims=True)
    zc = zv - mu
    var = jnp.mean(zc * zc, axis=-1, keepdims=True)
    vvn_ref[...] = zc * lax.rsqrt(var + NORM_EPS) * lng_ref[...] + lnb_ref[...]
    g0 = z0 + 2 * sw
    ga_ref[...] = _sigmoid(proj(g0, g0 + d))
    gs_ref[...] = _sigmoid(proj(g0 + d, g0 + 2 * d))


def _in_proj(x2, g, w_bf, lng, lnb, sgu_width):
    t, d = x2.shape
    tm = PROJ_ROWS
    n_in = w_bf.shape[1]
    row = lambda w: pl.BlockSpec((tm, w), lambda i: (i, 0))
    const = lambda shape: pl.BlockSpec(shape, lambda i: (0,) * len(shape))
    out_shape = (
        jax.ShapeDtypeStruct((t, ATTN_WIDTH), F32),
        jax.ShapeDtypeStruct((t, ATTN_WIDTH), BF16),
        jax.ShapeDtypeStruct((t, ATTN_WIDTH), BF16),
        jax.ShapeDtypeStruct((t // MOBA_BLOCK, 1, ATTN_WIDTH), F32),
        jax.ShapeDtypeStruct((t, sgu_width), F32),
        jax.ShapeDtypeStruct((t, sgu_width), F32),
        jax.ShapeDtypeStruct((t, d), F32),
        jax.ShapeDtypeStruct((t, d), F32),
    )
    out_specs = (
        row(ATTN_WIDTH), row(ATTN_WIDTH), row(ATTN_WIDTH),
        pl.BlockSpec((tm // MOBA_BLOCK, 1, ATTN_WIDTH), lambda i: (i, 0, 0)),
        row(sgu_width), row(sgu_width), row(d), row(d),
    )
    return pl.pallas_call(
        _in_proj_kernel,
        grid=(t // tm,),
        in_specs=[row(d), const((1, d)), const((d, n_in)), const((1, sgu_width)), const((1, sgu_width))],
        out_specs=out_specs,
        out_shape=out_shape,
        compiler_params=pltpu.CompilerParams(dimension_semantics=("arbitrary",), vmem_limit_bytes=VMEM_LIMIT),
        name="in_proj",
    )(x2, g, w_bf, lng, lnb)


def _moba_kernel(slopes_ref, q_ref, k_ref, v_ref, kmean_ref, o_ref):
    pair = pl.program_id(1)
    i = pl.program_id(2)
    nb = kmean_ref.shape[0]
    blk = MOBA_BLOCK
    q = q_ref[...]
    kmean = kmean_ref[...]
    lane = lax.broadcasted_iota(jnp.int32, (1, LANES), 1)
    bid = lax.broadcasted_iota(jnp.int32, (1, nb), 1)
    r_minus_c = (lax.broadcasted_iota(jnp.int32, (blk, blk), 0)
                 - lax.broadcasted_iota(jnp.int32, (blk, blk), 1)).astype(F32)
    scale = HEAD_DIM ** -0.5
    contract_last = (((1,), (1,)), ((), ()))
    out = jnp.zeros((blk, LANES), F32)
    own = pl.multiple_of(i * blk, blk)
    for hh in range(HEADS_PER_LANE_TILE):
        hmask = (lane >= HEAD_DIM * hh) & (lane < HEAD_DIM * (hh + 1))
        qh = jnp.where(hmask, q, 0.0)
        gate = lax.dot_general(qh, kmean, contract_last, precision=lax.Precision.HIGHEST,
                               preferred_element_type=F32)
        gate = jnp.where(bid < i, gate, NEG_INF)
        selbias = jnp.full((blk, nb), NEG_INF, F32)
        for kk in range(MOBA_TOPK):
            gmax = jnp.max(gate, axis=1, keepdims=True)
            first = jnp.min(jnp.where(gate == gmax, bid, nb), axis=1, keepdims=True)
            valid = (i > kk).astype(jnp.int32)
            first = first * valid + (valid - 1)
            hit = bid == first
            selbias = jnp.where(hit, 0.0, selbias)
            gate = jnp.where(hit, NEG_INF, gate)
        slope = slopes_ref[pair * HEADS_PER_LANE_TILE + hh]
        qb = (qh * scale).astype(BF16)

        kb = k_ref[pl.ds(own, blk), :]
        vb = v_ref[pl.ds(own, blk), :]
        s = lax.dot_general(qb, kb, contract_last, preferred_element_type=F32) - slope * r_minus_c
        s = jnp.where(r_minus_c >= 0.0, s, NEG_INF)
        m0 = jnp.max(s, axis=1, keepdims=True)
        p0 = jnp.exp(s - m0)
        l0 = jnp.sum(p0, axis=1, keepdims=True)
        acc0 = jnp.dot(p0.astype(BF16), vb, preferred_element_type=F32)

        def body(j, carry, qb=qb, selbias=selbias, slope=slope):
            m, l, acc = carry
            start = pl.multiple_of(j * blk, blk)
            kb = k_ref[pl.ds(start, blk), :]
            vb = v_ref[pl.ds(start, blk), :]
            dist = r_minus_c + ((i - j) * blk).astype(F32)
            s = lax.dot_general(qb, kb, contract_last, preferred_element_type=F32) - slope * dist
            rowb = jnp.max(jnp.where(bid == j, selbias, NEG_INF), axis=1, keepdims=True)
            s = s + rowb
            m_new = jnp.maximum(m, jnp.max(s, axis=1, keepdims=True))
            alpha = jnp.exp(m - m_new)
            pj = jnp.exp(s - m_new)
            l = alpha * l + jnp.sum(pj, axis=1, keepdims=True)
            acc = alpha * acc + jnp.dot(pj.astype(BF16), vb, preferred_element_type=F32)
            return m_new, l, acc

        _, l_fin, acc = lax.fori_loop(0, i, body, (m0, l0, acc0))
        out = jnp.where(hmask, acc / l_fin, out)
    o_ref[...] = out


def _moba(q, k, v, kmean, slopes):
    b, s, a = q.shape
    nb = s // MOBA_BLOCK
    n_pairs = a // LANES
    grid_spec = pltpu.PrefetchScalarGridSpec(
        num_scalar_prefetch=1,
        grid=(b, n_pairs, nb),
        in_specs=[
            pl.BlockSpec((None, MOBA_BLOCK, LANES), lambda bi, p, i, sl: (bi, i, p)),
            pl.BlockSpec((None, s, LANES), lambda bi, p, i, sl: (bi, 0, p)),
            pl.BlockSpec((None, s, LANES), lambda bi, p, i, sl: (bi, 0, p)),
            pl.BlockSpec((None, nb, LANES), lambda bi, p, i, sl: (bi, 0, p)),
        ],
        out_specs=pl.BlockSpec((None, MOBA_BLOCK, LANES), lambda bi, p, i, sl: (bi, i, p)),
    )
    return pl.pallas_call(
        _moba_kernel,
        grid_spec=grid_spec,
        out_shape=jax.ShapeDtypeStruct((b, s, a), F32),
        compiler_params=pltpu.CompilerParams(
            dimension_semantics=("arbitrary", "arbitrary", "arbitrary"), vmem_limit_bytes=VMEM_LIMIT),
        name="moba",
    )(slopes, q, k, v, kmean)


def _mix_kernel(attn_ref, u_ref, vvn_ref, ga_ref, gs_ref, x_ref,
                wao_ref, wso_ref, wmo_ref, wsp_ref, bsp_ref, fg_ref, wr_ref, br_ref,
                h1_ref, xn2_ref, ri_ref, rw_ref, cnt_ref, carry_ref):
    step = pl.program_id(0)
    tm = x_ref.shape[0]

    @pl.when(step == 0)
    def _():
        carry_ref[...] = jnp.zeros_like(carry_ref)

    y_attn = jnp.dot(attn_ref[...].astype(BF16), wao_ref[...], preferred_element_type=F32)

    ch = SGU_CHUNK
    lane = lax.broadcasted_iota(jnp.int32, (1, LANES), 1)
    tril = lax.broadcasted_iota(jnp.int32, (ch, ch), 0) >= lax.broadcasted_iota(jnp.int32, (ch, ch), 1)
    gdim = vvn_ref.shape[1] // SGU_GROUPS
    groups_per_tile = LANES // gdim
    w_causal = [jnp.where(tril, wsp_ref[g], 0.0).astype(BF16) for g in range(SGU_GROUPS)]
    rows = []
    for c in range(tm // ch):
        cols = []
        for ct in range(vvn_ref.shape[1] // LANES):
            vp = vvn_ref[c * ch:(c + 1) * ch, ct * LANES:(ct + 1) * LANES]
            acc = jnp.zeros((ch, LANES), F32)
            for gg in range(groups_per_tile):
                gmask = (lane >= gdim * gg) & (lane < gdim * (gg + 1))
                vm = jnp.where(gmask, vp, 0.0).astype(BF16)
                acc = acc + jnp.dot(w_causal[ct * groups_per_tile + gg], vm, preferred_element_type=F32)
            cols.append(acc)
        rows.append(jnp.concatenate(cols, axis=1) + bsp_ref[...])
    mixed = jnp.concatenate(rows, axis=0)
    sgu = u_ref[...] * mixed
    y_sgu = jnp.dot(sgu.astype(BF16), wso_ref[...], preferred_element_type=F32)

    merged = ga_ref[...] * y_attn + gs_ref[...] * y_sgu
    h1 = x_ref[...] + jnp.dot(merged.astype(BF16), wmo_ref[...], preferred_element_type=F32)
    h1_ref[...] = h1
    xn2 = h1 * lax.rsqrt(jnp.mean(h1 * h1, axis=-1, keepdims=True) + NORM_EPS) * fg_ref[...]
    xn2_ref[...] = xn2

    logits = jnp.dot(xn2, wr_ref[...], precision=lax.Precision.HIGHEST,
                     preferred_element_type=F32) + br_ref[...]
    work = jnp.where(lane < N_EXPERTS, logits, NEG_INF)
    vals, idxs = [], []
    for _ in range(EXPERT_TOPK):
        vmax = jnp.max(work, axis=1, keepdims=True)
        first = jnp.min(jnp.where(work == vmax, lane, LANES), axis=1, keepdims=True)
        vals.append(vmax)
        idxs.append(first)
        work = jnp.where(lane == first, NEG_INF, work)
    exps = [jnp.exp(v - vals[0]) for v in vals]
    denom = exps[0]
    for e in exps[1:]:
        denom = denom + e
    chosen = jnp.zeros((tm, LANES), F32)
    for first in idxs:
        chosen = jnp.where(lane == first, 1.0, chosen)

    strict_lower = (lax.broadcasted_iota(jnp.int32, (tm, tm), 0)
                    > lax.broadcasted_iota(jnp.int32, (tm, tm), 1))
    before = jnp.dot(jnp.where(strict_lower, 1.0, 0.0).astype(BF16), chosen.astype(BF16),
                     preferred_element_type=F32) + carry_ref[0:1, :]
    ri = jnp.zeros((tm, LANES), jnp.int32)
    rw = jnp.zeros((tm, LANES), F32)
    for kk in range(EXPERT_TOPK):
        rank = jnp.sum(jnp.where(lane == idxs[kk], before, 0.0), axis=1, keepdims=True)
        ri = jnp.where(lane == kk, idxs[kk], ri)
        ri = jnp.where(lane == EXPERT_TOPK + kk, rank.astype(jnp.int32), ri)
        rw = jnp.where(lane == kk, exps[kk] / denom, rw)
    ri_ref[...] = ri
    rw_ref[...] = rw
    total = carry_ref[0:1, :] + jnp.sum(chosen, axis=0, keepdims=True)
    carry_ref[...] = jnp.broadcast_to(total, carry_ref.shape)
    cnt_ref[...] = jnp.broadcast_to(total, cnt_ref.shape)


def _mix(attn, u, vvn, ga, gs, x2, wao, wso, wmo, wsp, bsp_full, fg, wr_pad, br_pad):
    t, d = x2.shape
    tm = PROJ_ROWS
    row = lambda w: pl.BlockSpec((tm, w), lambda i: (i, 0))
    const = lambda shape: pl.BlockSpec(shape, lambda i: (0,) * len(shape))
    ins = (attn, u, vvn, ga, gs, x2, wao, wso, wmo, wsp, bsp_full, fg, wr_pad, br_pad)
    in_specs = [row(a.shape[1]) for a in ins[:6]] + [const(a.shape) for a in ins[6:]]
    out_shape = (
        jax.ShapeDtypeStruct((t, d), F32),
        jax.ShapeDtypeStruct((t, d), F32),
        jax.ShapeDtypeStruct((t, LANES), jnp.int32),
        jax.ShapeDtypeStruct((t, LANES), F32),
        jax.ShapeDtypeStruct((8, LANES), F32),
    )
    out_specs = (row(d), row(d), row(LANES), row(LANES), const((8, LANES)))
    return pl.pallas_call(
        _mix_kernel,
        grid=(t // tm,),
        in_specs=in_specs,
        out_specs=out_specs,
        out_shape=out_shape,
        scratch_shapes=[pltpu.VMEM((8, LANES), F32)],
        compiler_params=pltpu.CompilerParams(dimension_semantics=("arbitrary",), vmem_limit_bytes=VMEM_LIMIT),
        name="mix_route",
    )(*ins)


def _dispatch_kernel(start_ref, padlo_ref, padhi_ref, nused_ref, eid_ref, rank_ref, x_hbm, xs_hbm,
                     zeros_ref, sem, zsem):
    c = pl.program_id(0)
    n_assign = eid_ref.shape[0]
    tile = zeros_ref.shape[0]

    def pad_copy(r):
        return pltpu.make_async_copy(zeros_ref.at[pl.ds(0, 1)], xs_hbm.at[pl.ds(r, 1)], zsem)

    def tail_copy(tl):
        return pltpu.make_async_copy(zeros_ref, xs_hbm.at[pl.ds(pl.multiple_of(tl * tile, tile), tile)], zsem)

    @pl.when(c == 0)
    def _():
        zeros_ref[...] = jnp.zeros_like(zeros_ref)

        def tail_tile(tl, carry):
            tail_copy(tl).start()
            tail_copy(tl).wait()
            return carry

        lax.fori_loop(nused_ref[0], xs_hbm.shape[0] // tile, tail_tile, 0)

        def per_expert(e, carry):
            lo = padlo_ref[e]
            hi = padhi_ref[e]

            def start_row(r, cr):
                pad_copy(r).start()
                return cr

            def wait_row(r, cr):
                pad_copy(r).wait()
                return cr

            lax.fori_loop(lo, hi, start_row, 0)
            lax.fori_loop(lo, hi, wait_row, 0)
            return carry

        lax.fori_loop(0, N_EXPERTS, per_expert, 0)

    def row_copy(a):
        tok = c * (n_assign // EXPERT_TOPK) + lax.shift_right_logical(a, 2)
        dst = start_ref[eid_ref[a]] + rank_ref[a]
        return pltpu.make_async_copy(x_hbm.at[pl.ds(tok, 1)], xs_hbm.at[pl.ds(dst, 1)], sem)

    def start_row(a, cr):
        row_copy(a).start()
        return cr

    def wait_row(a, cr):
        row_copy(a).wait()
        return cr

    lax.fori_loop(0, n_assign, start_row, 0)
    lax.fori_loop(0, n_assign, wait_row, 0)


def _dispatch(starts, padlo, padhi, n_used, eid, rank, xn2, n_rows):
    t, d = xn2.shape
    n_assign = DISPATCH_TOKENS * EXPERT_TOPK
    smem_blk = pl.BlockSpec((n_assign,), lambda c, *_: (c,), memory_space=pltpu.SMEM)
    grid_spec = pltpu.PrefetchScalarGridSpec(
        num_scalar_prefetch=4,
        grid=(t // DISPATCH_TOKENS,),
        in_specs=[smem_blk, smem_blk, pl.BlockSpec(memory_space=pl.ANY)],
        out_specs=pl.BlockSpec(memory_space=pl.ANY),
        scratch_shapes=[pltpu.VMEM((EXPERT_ROWS, d), F32), pltpu.SemaphoreType.DMA(()),
                        pltpu.SemaphoreType.DMA(())],
    )
    return pl.pallas_call(
        _dispatch_kernel,
        grid_spec=grid_spec,
        out_shape=jax.ShapeDtypeStruct((n_rows, d), F32),
        compiler_params=pltpu.CompilerParams(dimension_semantics=("arbitrary",)),
        name="dispatch",
    )(starts, padlo, padhi, n_used, eid, rank, xn2)


def _expert_kernel(be_ref, nused_ref, x_ref, wg_ref, wu_ref, bg_ref, bu_ref, wd_ref, bd_ref, y_ref):
    @pl.when(pl.program_id(0) < nused_ref[0])
    def _():
        xb = x_ref[...].astype(BF16)
        gate = jnp.dot(xb, wg_ref[...], preferred_element_type=F32) + bg_ref[...]
        up = jnp.dot(xb, wu_ref[...], preferred_element_type=F32) + bu_ref[...]
        gate = jnp.minimum(gate, SWIGLU_LIMIT)
        up = jnp.clip(up, -SWIGLU_LIMIT, SWIGLU_LIMIT)
        hid = gate * _sigmoid(SWIGLU_ALPHA * gate) * (up + 1.0)
        y_ref[...] = jnp.dot(hid.astype(BF16), wd_ref[...], preferred_element_type=F32) + bd_ref[...]

    @pl.when(pl.program_id(0) >= nused_ref[0])
    def _():
        y_ref[...] = jnp.zeros_like(y_ref)


def _experts(block_expert, n_used, xs, wg, wu, bg, bu, wd, bd):
    n_rows, d = xs.shape
    de = wg.shape[2]
    n_tiles = n_rows // EXPERT_ROWS
    tile_idx = lambda t, be, nu: (jnp.minimum(t, nu[0] - 1), 0)
    per_expert = lambda t, be, nu: (be[t], 0, 0)
    grid_spec = pltpu.PrefetchScalarGridSpec(
        num_scalar_prefetch=2,
        grid=(n_tiles,),
        in_specs=[
            pl.BlockSpec((EXPERT_ROWS, d), tile_idx),
            pl.BlockSpec((None, d, de), per_expert),
            pl.BlockSpec((None, d, de), per_expert),
            pl.BlockSpec((None, 1, de), per_expert),
            pl.BlockSpec((None, 1, de), per_expert),
            pl.BlockSpec((None, de, d), per_expert),
            pl.BlockSpec((None, 1, d), per_expert),
        ],
        out_specs=pl.BlockSpec((EXPERT_ROWS, d), lambda t, be, nu: (t, 0)),
    )
    return pl.pallas_call(
        _expert_kernel,
        grid_spec=grid_spec,
        out_shape=jax.ShapeDtypeStruct((n_rows, d), F32),
        compiler_params=pltpu.CompilerParams(dimension_semantics=("arbitrary",), vmem_limit_bytes=VMEM_LIMIT),
        name="experts",
    )(block_expert, n_used, xs, wg, wu, bg, bu, wd, bd)


def _combine_kernel(start_ref, eid_ref, rank_ref, rw_ref, h1_ref, fg_ref, ys_hbm, o_ref, buf_ref, sem):
    n_assign = eid_ref.shape[0]

    def row_copy(a):
        tok = lax.shift_right_logical(a, 2)
        slot = lax.bitwise_and(a, EXPERT_TOPK - 1)
        src = start_ref[eid_ref[a]] + rank_ref[a]
        return pltpu.make_async_copy(ys_hbm.at[pl.ds(src, 1)], buf_ref.at[slot, pl.ds(tok, 1)], sem)

    def start_row(a, cr):
        row_copy(a).start()
        return cr

    def wait_row(a, cr):
        row_copy(a).wait()
        return cr

    lax.fori_loop(0, n_assign, start_row, 0)
    lax.fori_loop(0, n_assign, wait_row, 0)

    rw = rw_ref[...]
    h = h1_ref[...]
    for kk in range(EXPERT_TOPK):
        h = h + rw[:, kk:kk + 1] * buf_ref[kk]
    o_ref[...] = h * lax.rsqrt(jnp.mean(h * h, axis=-1, keepdims=True) + NORM_EPS) * fg_ref[...]


def _combine(starts, eid, rank, rw, h1, fg, ys):
    t, d = h1.shape
    tt = COMBINE_TOKENS
    n_assign = tt * EXPERT_TOPK
    smem_blk = pl.BlockSpec((n_assign,), lambda c, *_: (c,), memory_space=pltpu.SMEM)
    grid_spec = pltpu.PrefetchScalarGridSpec(
        num_scalar_prefetch=1,
        grid=(t // tt,),
        in_specs=[
            smem_blk, smem_blk,
            pl.BlockSpec((tt, LANES), lambda c, *_: (c, 0)),
            pl.BlockSpec((tt, d), lambda c, *_: (c, 0)),
            pl.BlockSpec((1, d), lambda c, *_: (0, 0)),
            pl.BlockSpec(memory_space=pl.ANY),
        ],
        out_specs=pl.BlockSpec((tt, d), lambda c, *_: (c, 0)),
        scratch_shapes=[pltpu.VMEM((EXPERT_TOPK, tt, d), F32), pltpu.SemaphoreType.DMA(())],
    )
    return pl.pallas_call(
        _combine_kernel,
        grid_spec=grid_spec,
        out_shape=jax.ShapeDtypeStruct((t, d), F32),
        compiler_params=pltpu.CompilerParams(dimension_semantics=("arbitrary",), vmem_limit_bytes=VMEM_LIMIT),
        name="combine",
    )(starts, eid, rank, rw, h1, fg, ys)


def _layer(h, mix_norm_g, w_in, w_attn_out, sgu_ln_g, sgu_ln_b, w_spatial, b_spatial, w_sgu_out,
           w_mix_out, ffn_norm_g, w_router, b_router, w_gate_up, b_gate_up, w_down, b_down, out_g):
    b, s, d = h.shape
    t = b * s
    sgu_width = sgu_ln_g.shape[0]
    x2 = h.reshape(t, d)

    q, k, v, kmean, u, vvn, ga, gs = _in_proj(
        x2, mix_norm_g.reshape(1, d), w_in.astype(BF16), sgu_ln_g.reshape(1, sgu_width),
        sgu_ln_b.reshape(1, sgu_width), sgu_width)

    slopes = 2.0 ** (-8.0 * jnp.arange(1, N_HEADS + 1, dtype=F32) / N_HEADS)
    nb = s // MOBA_BLOCK
    attn = _moba(q.reshape(b, s, ATTN_WIDTH), k.reshape(b, s, ATTN_WIDTH), v.reshape(b, s, ATTN_WIDTH),
                 kmean.reshape(b, nb, ATTN_WIDTH), slopes).reshape(t, ATTN_WIDTH)

    gdim = sgu_width // SGU_GROUPS
    bsp_full = jnp.repeat(b_spatial.T, gdim, axis=1)
    wr_pad = jnp.zeros((d, LANES), F32).at[:, :N_EXPERTS].set(w_router)
    br_pad = jnp.zeros((1, LANES), F32).at[0, :N_EXPERTS].set(b_router)
    h1, xn2, ri, rw, cnt = _mix(
        attn, u, vvn, ga, gs, x2, w_attn_out.astype(BF16), w_sgu_out.astype(BF16), w_mix_out.astype(BF16),
        w_spatial, bsp_full, ffn_norm_g.reshape(1, d), wr_pad, br_pad)

    n_assign = t * EXPERT_TOPK
    n_tiles = -(-(n_assign + N_EXPERTS * (EXPERT_ROWS - 1)) // EXPERT_ROWS)
    n_rows = n_tiles * EXPERT_ROWS
    counts = cnt[0, :N_EXPERTS].astype(jnp.int32)
    padded = (counts + EXPERT_ROWS - 1) // EXPERT_ROWS * EXPERT_ROWS
    pad_end = jnp.cumsum(padded)
    starts = pad_end - padded
    block_expert = jnp.minimum(
        jnp.searchsorted(pad_end, jnp.arange(n_tiles, dtype=jnp.int32) * EXPERT_ROWS, side="right"),
        N_EXPERTS - 1).astype(jnp.int32)
    n_used = (pad_end[-1:] // EXPERT_ROWS).astype(jnp.int32)
    eid = ri[:, :EXPERT_TOPK].reshape(n_assign)
    rank = ri[:, EXPERT_TOPK:2 * EXPERT_TOPK].reshape(n_assign)

    xs = _dispatch(starts, starts + counts, pad_end, n_used, eid, rank, xn2, n_rows)

    de = w_down.shape[1]
    wg = w_gate_up[:, :, 0::2].astype(BF16)
    wu = w_gate_up[:, :, 1::2].astype(BF16)
    bg = b_gate_up[:, 0::2].reshape(N_EXPERTS, 1, de)
    bu = b_gate_up[:, 1::2].reshape(N_EXPERTS, 1, de)
    ys = _experts(block_expert, n_used, xs, wg, wu, bg, bu, w_down.astype(BF16), b_down.reshape(N_EXPERTS, 1, d))

    out = _combine(starts, eid, rank, rw, h1, out_g.reshape(1, d), ys)
    return out.reshape(b, s, d)


def kernel(x, mix_norm_g, w_in, w_attn_out, sgu_ln_g, sgu_ln_b, w_spatial, b_spatial, w_sgu_out, w_mix_out,
           ffn_norm_g, w_router, b_router, w_gate_up, b_gate_up, w_down, b_down, final_norm_g):
    depth = w_in.shape[0]
    assert depth == 1, "the final RMSNorm is fused into the single layer's combine step"
    return _layer(x, mix_norm_g[0], w_in[0], w_attn_out[0], sgu_ln_g[0], sgu_ln_b[0], w_spatial[0],
                  b_spatial[0], w_sgu_out[0], w_mix_out[0], ffn_norm_g[0], w_router[0], b_router[0],
                  w_gate_up[0], b_gate_up[0], w_down[0], b_down[0], final_norm_g)
```

```python
import jax
import jax.numpy as jnp
from jax import lax
from jax.experimental import pallas as pl
from jax.experimental.pallas import tpu as pltpu

F32 = jnp.float32
BF16 = jnp.bfloat16
NEG_INF = float("-inf")

N_HEADS = 8
HEAD_DIM = 64
ATTN_WIDTH = N_HEADS * HEAD_DIM
MOBA_BLOCK = 256
MOBA_TOPK = 3
SGU_CHUNK = 128
SGU_GROUPS = 8
N_EXPERTS = 32
EXPERT_TOPK = 4
SWIGLU_LIMIT = 7.0
SWIGLU_ALPHA = 1.702
NORM_EPS = 1e-5

LANES = 128
SUBLANES = 8
HEADS_PER_LANE_TILE = LANES // HEAD_DIM
PROJ_ROWS = 256
EXPERT_ROWS = 256
DISPATCH_TOKENS = 512
COMBINE_TOKENS = 128
VMEM_LIMIT = 48 * 1024 * 1024
EXPERT_VMEM_LIMIT = 56 * 1024 * 1024


def _sigmoid(x):
    return 1.0 / (1.0 + jnp.exp(-x))


def _store_token_rows(ref, value, base=0):
    n = value.shape[0]
    for c in range(value.shape[1] // LANES):
        ref[pl.ds(base + c, n, stride=SUBLANES), :] = value[:, c * LANES:(c + 1) * LANES]


def _load_token_rows(ref, n, base=0):
    return jnp.concatenate([ref[pl.ds(base + c, n, stride=SUBLANES), :] for c in range(SUBLANES)], axis=1)


def _token_row(ref, r):
    return ref.at[pl.ds(pl.multiple_of(r * SUBLANES, SUBLANES), SUBLANES)]


def _gelu_exact(x):
    return 0.5 * x * (1.0 + lax.erf(x * (0.5 ** 0.5)))


def _in_proj_kernel(x_ref, g_ref, w_ref, lng_ref, lnb_ref,
                    q_ref, k_ref, v_ref, kmean_ref, u_ref, vvn_ref, ga_ref, gs_ref):
    x = x_ref[...]
    xn = x * lax.rsqrt(jnp.mean(x * x, axis=-1, keepdims=True) + NORM_EPS) * g_ref[...]
    xb = xn.astype(BF16)

    def proj(lo, hi):
        return jnp.dot(xb, w_ref[:, lo:hi], preferred_element_type=F32)

    a = ATTN_WIDTH
    sw = u_ref.shape[1]
    d = ga_ref.shape[1]
    q_ref[...] = proj(0, a)
    kf = proj(a, 2 * a)
    k_ref[...] = kf.astype(BF16)
    for j in range(kf.shape[0] // MOBA_BLOCK):
        kmean_ref[j] = jnp.mean(kf[j * MOBA_BLOCK:(j + 1) * MOBA_BLOCK], axis=0, keepdims=True)
    v_ref[...] = proj(2 * a, 3 * a).astype(BF16)
    z0 = 3 * a
    u_ref[...] = _gelu_exact(proj(z0, z0 + sw))
    zv = _gelu_exact(proj(z0 + sw, z0 + 2 * sw))
    mu = jnp.mean(zv, axis=-1, keepdims=True)
    zc = zv - mu
    var = jnp.mean(zc * zc, axis=-1, keepdims=True)
    vvn_ref[...] = zc * lax.rsqrt(var + NORM_EPS) * lng_ref[...] + lnb_ref[...]
    g0 = z0 + 2 * sw
    ga_ref[...] = _sigmoid(proj(g0, g0 + d))
    gs_ref[...] = _sigmoid(proj(g0 + d, g0 + 2 * d))


def _in_proj(x2, g, w_bf, lng, lnb, sgu_width):
    t, d = x2.shape
    tm = PROJ_ROWS
    n_in = w_bf.shape[1]
    row = lambda w: pl.BlockSpec((tm, w), lambda i: (i, 0))
    const = lambda shape: pl.BlockSpec(shape, lambda i: (0,) * len(shape))
    out_shape = (
        jax.ShapeDtypeStruct((t, ATTN_WIDTH), F32),
        jax.ShapeDtypeStruct((t, ATTN_WIDTH), BF16),
        jax.ShapeDtypeStruct((t, ATTN_WIDTH), BF16),
        jax.ShapeDtypeStruct((t // MOBA_BLOCK, 1, ATTN_WIDTH), F32),
        jax.ShapeDtypeStruct((t, sgu_width), F32),
        jax.ShapeDtypeStruct((t, sgu_width), F32),
        jax.ShapeDtypeStruct((t, d), F32),
        jax.ShapeDtypeStruct((t, d), F32),
    )
    out_specs = (
        row(ATTN_WIDTH), row(ATTN_WIDTH), row(ATTN_WIDTH),
        pl.BlockSpec((tm // MOBA_BLOCK, 1, ATTN_WIDTH), lambda i: (i, 0, 0)),
        row(sgu_width), row(sgu_width), row(d), row(d),
    )
    return pl.pallas_call(
        _in_proj_kernel,
        grid=(t // tm,),
        in_specs=[row(d), const((1, d)), const((d, n_in)), const((1, sgu_width)), const((1, sgu_width))],
        out_specs=out_specs,
        out_shape=out_shape,
        compiler_params=pltpu.CompilerParams(dimension_semantics=("arbitrary",), vmem_limit_bytes=VMEM_LIMIT),
        name="in_proj",
    )(x2, g, w_bf, lng, lnb)


def _moba_kernel(slopes_ref, q_ref, k_ref, v_ref, kmean_ref, o_ref):
    pair = pl.program_id(1)
    i = pl.program_id(2)
    nb = kmean_ref.shape[0]
    blk = MOBA_BLOCK
    q = q_ref[...]
    kmean = kmean_ref[...]
    lane = lax.broadcasted_iota(jnp.int32, (1, LANES), 1)
    bid = lax.broadcasted_iota(jnp.int32, (1, nb), 1)
    r_minus_c = (lax.broadcasted_iota(jnp.int32, (blk, blk), 0)
                 - lax.broadcasted_iota(jnp.int32, (blk, blk), 1)).astype(F32)
    scale = HEAD_DIM ** -0.5
    contract_last = (((1,), (1,)), ((), ()))
    out = jnp.zeros((blk, LANES), F32)
    own = pl.multiple_of(i * blk, blk)
    for hh in range(HEADS_PER_LANE_TILE):
        hmask = (lane >= HEAD_DIM * hh) & (lane < HEAD_DIM * (hh + 1))
        qh = jnp.where(hmask, q, 0.0)
        gate = lax.dot_general(qh, kmean, contract_last, precision=lax.Precision.HIGHEST,
                               preferred_element_type=F32)
        gate = jnp.where(bid < i, gate, NEG_INF)
        selbias = jnp.full((blk, nb), NEG_INF, F32)
        for kk in range(MOBA_TOPK):
            gmax = jnp.max(gate, axis=1, keepdims=True)
            first = jnp.min(jnp.where(gate == gmax, bid, nb), axis=1, keepdims=True)
            valid = (i > kk).astype(jnp.int32)
            first = first * valid + (valid - 1)
            hit = bid == first
            selbias = jnp.where(hit, 0.0, selbias)
            gate = jnp.where(hit, NEG_INF, gate)
        slope = slopes_ref[pair * HEADS_PER_LANE_TILE + hh]
        qb = (qh * scale).astype(BF16)

        kb = k_ref[pl.ds(own, blk), :]
        vb = v_ref[pl.ds(own, blk), :]
        s = lax.dot_general(qb, kb, contract_last, preferred_element_type=F32) - slope * r_minus_c
        s = jnp.where(r_minus_c >= 0.0, s, NEG_INF)
        m0 = jnp.max(s, axis=1, keepdims=True)
        p0 = jnp.exp(s - m0)
        l0 = jnp.sum(p0, axis=1, keepdims=True)
        acc0 = jnp.dot(p0.astype(BF16), vb, preferred_element_type=F32)

        def body(j, carry, qb=qb, selbias=selbias, slope=slope):
            m, l, acc = carry
            start = pl.multiple_of(j * blk, blk)
            kb = k_ref[pl.ds(start, blk), :]
            vb = v_ref[pl.ds(start, blk), :]
            dist = r_minus_c + ((i - j) * blk).astype(F32)
            s = lax.dot_general(qb, kb, contract_last, preferred_element_type=F32) - slope * dist
            rowb = jnp.max(jnp.where(bid == j, selbias, NEG_INF), axis=1, keepdims=True)
            s = s + rowb
            m_new = jnp.maximum(m, jnp.max(s, axis=1, keepdims=True))
            alpha = jnp.exp(m - m_new)
            pj = jnp.exp(s - m_new)
            l = alpha * l + jnp.sum(pj, axis=1, keepdims=True)
            acc = alpha * acc + jnp.dot(pj.astype(BF16), vb, preferred_element_type=F32)
            return m_new, l, acc

        _, l_fin, acc = lax.fori_loop(0, i, body, (m0, l0, acc0))
        out = jnp.where(hmask, acc / l_fin, out)
    o_ref[...] = out


def _moba(q, k, v, kmean, slopes):
    b, s, a = q.shape
    nb = s // MOBA_BLOCK
    n_pairs = a // LANES
    grid_spec = pltpu.PrefetchScalarGridSpec(
        num_scalar_prefetch=1,
        grid=(b, n_pairs, nb),
        in_specs=[
            pl.BlockSpec((None, MOBA_BLOCK, LANES), lambda bi, p, i, sl: (bi, i, p)),
            pl.BlockSpec((None, s, LANES), lambda bi, p, i, sl: (bi, 0, p)),
            pl.BlockSpec((None, s, LANES), lambda bi, p, i, sl: (bi, 0, p)),
            pl.BlockSpec((None, nb, LANES), lambda bi, p, i, sl: (bi, 0, p)),
        ],
        out_specs=pl.BlockSpec((None, MOBA_BLOCK, LANES), lambda bi, p, i, sl: (bi, i, p)),
    )
    return pl.pallas_call(
        _moba_kernel,
        grid_spec=grid_spec,
        out_shape=jax.ShapeDtypeStruct((b, s, a), F32),
        compiler_params=pltpu.CompilerParams(
            dimension_semantics=("arbitrary", "arbitrary", "arbitrary"), vmem_limit_bytes=VMEM_LIMIT),
        name="moba",
    )(slopes, q, k, v, kmean)


def _mix_kernel(attn_ref, u_ref, vvn_ref, ga_ref, gs_ref, x_ref,
                wao_ref, wso_ref, wmo_ref, wsp_ref, bsp_ref, fg_ref, wr_ref, br_ref,
                h1_ref, xn2_ref, ri_ref, rw_ref, cnt_ref, carry_ref):
    step = pl.program_id(0)
    tm = x_ref.shape[0]

    @pl.when(step == 0)
    def _():
        carry_ref[...] = jnp.zeros_like(carry_ref)

    y_attn = jnp.dot(attn_ref[...].astype(BF16), wao_ref[...], preferred_element_type=F32)

    ch = SGU_CHUNK
    lane = lax.broadcasted_iota(jnp.int32, (1, LANES), 1)
    tril = lax.broadcasted_iota(jnp.int32, (ch, ch), 0) >= lax.broadcasted_iota(jnp.int32, (ch, ch), 1)
    gdim = vvn_ref.shape[1] // SGU_GROUPS
    groups_per_tile = LANES // gdim
    w_causal = [jnp.where(tril, wsp_ref[g], 0.0).astype(BF16) for g in range(SGU_GROUPS)]
    rows = []
    for c in range(tm // ch):
        cols = []
        for ct in range(vvn_ref.shape[1] // LANES):
            vp = vvn_ref[c * ch:(c + 1) * ch, ct * LANES:(ct + 1) * LANES]
            acc = jnp.zeros((ch, LANES), F32)
            for gg in range(groups_per_tile):
                gmask = (lane >= gdim * gg) & (lane < gdim * (gg + 1))
                vm = jnp.where(gmask, vp, 0.0).astype(BF16)
                acc = acc + jnp.dot(w_causal[ct * groups_per_tile + gg], vm, preferred_element_type=F32)
            cols.append(acc)
        rows.append(jnp.concatenate(cols, axis=1) + bsp_ref[...])
    mixed = jnp.concatenate(rows, axis=0)
    sgu = u_ref[...] * mixed
    y_sgu = jnp.dot(sgu.astype(BF16), wso_ref[...], preferred_element_type=F32)

    merged = ga_ref[...] * y_attn + gs_ref[...] * y_sgu
    h1 = x_ref[...] + jnp.dot(merged.astype(BF16), wmo_ref[...], preferred_element_type=F32)
    h1_ref[...] = h1
    xn2 = h1 * lax.rsqrt(jnp.mean(h1 * h1, axis=-1, keepdims=True) + NORM_EPS) * fg_ref[...]
    _store_token_rows(xn2_ref, xn2)

    logits = jnp.dot(xn2, wr_ref[...], precision=lax.Precision.HIGHEST,
                     preferred_element_type=F32) + br_ref[...]
    work = jnp.where(lane < N_EXPERTS, logits, NEG_INF)
    vals, idxs = [], []
    for _ in range(EXPERT_TOPK):
        vmax = jnp.max(work, axis=1, keepdims=True)
        first = jnp.min(jnp.where(work == vmax, lane, LANES), axis=1, keepdims=True)
        vals.append(vmax)
        idxs.append(first)
        work = jnp.where(lane == first, NEG_INF, work)
    exps = [jnp.exp(v - vals[0]) for v in vals]
    denom = exps[0]
    for e in exps[1:]:
        denom = denom + e
    chosen = jnp.zeros((tm, LANES), F32)
    for first in idxs:
        chosen = jnp.where(lane == first, 1.0, chosen)

    strict_lower = (lax.broadcasted_iota(jnp.int32, (tm, tm), 0)
                    > lax.broadcasted_iota(jnp.int32, (tm, tm), 1))
    before = jnp.dot(jnp.where(strict_lower, 1.0, 0.0).astype(BF16), chosen.astype(BF16),
                     preferred_element_type=F32) + carry_ref[0:1, :]
    ri = jnp.zeros((tm, LANES), jnp.int32)
    rw = jnp.zeros((tm, LANES), F32)
    for kk in range(EXPERT_TOPK):
        rank = jnp.sum(jnp.where(lane == idxs[kk], before, 0.0), axis=1, keepdims=True)
        ri = jnp.where(lane == kk, idxs[kk], ri)
        ri = jnp.where(lane == EXPERT_TOPK + kk, rank.astype(jnp.int32), ri)
        rw = jnp.where(lane == kk, exps[kk] / denom, rw)
    ri_ref[...] = ri
    rw_ref[...] = rw
    total = carry_ref[0:1, :] + jnp.sum(chosen, axis=0, keepdims=True)
    carry_ref[...] = jnp.broadcast_to(total, carry_ref.shape)
    cnt_ref[...] = jnp.broadcast_to(total, cnt_ref.shape)


def _mix(attn, u, vvn, ga, gs, x2, wao, wso, wmo, wsp, bsp_full, fg, wr_pad, br_pad):
    t, d = x2.shape
    tm = PROJ_ROWS
    row = lambda w: pl.BlockSpec((tm, w), lambda i: (i, 0))
    const = lambda shape: pl.BlockSpec(shape, lambda i: (0,) * len(shape))
    ins = (attn, u, vvn, ga, gs, x2, wao, wso, wmo, wsp, bsp_full, fg, wr_pad, br_pad)
    in_specs = [row(a.shape[1]) for a in ins[:6]] + [const(a.shape) for a in ins[6:]]
    out_shape = (
        jax.ShapeDtypeStruct((t, d), F32),
        jax.ShapeDtypeStruct((t * SUBLANES, LANES), F32),
        jax.ShapeDtypeStruct((t, LANES), jnp.int32),
        jax.ShapeDtypeStruct((t, LANES), F32),
        jax.ShapeDtypeStruct((8, LANES), F32),
    )
    assert d == SUBLANES * LANES, "a token row is stored as one (8, 128) tile"
    token_rows = pl.BlockSpec((tm * SUBLANES, LANES), lambda i: (i, 0))
    out_specs = (row(d), token_rows, row(LANES), row(LANES), const((8, LANES)))
    return pl.pallas_call(
        _mix_kernel,
        grid=(t // tm,),
        in_specs=in_specs,
        out_specs=out_specs,
        out_shape=out_shape,
        scratch_shapes=[pltpu.VMEM((8, LANES), F32)],
        compiler_params=pltpu.CompilerParams(dimension_semantics=("arbitrary",), vmem_limit_bytes=VMEM_LIMIT),
        name="mix_route",
    )(*ins)


def _dispatch_kernel(start_ref, padlo_ref, padhi_ref, nused_ref, eid_ref, rank_ref, x_ref, xs_hbm,
                     zeros_ref, sem, zsem):
    c = pl.program_id(0)
    n_assign = eid_ref.shape[0]
    tile_rows = zeros_ref.shape[0]

    def pad_copy(r):
        return pltpu.make_async_copy(zeros_ref.at[pl.ds(0, SUBLANES)], _token_row(xs_hbm, r), zsem)

    def tail_copy(tl):
        dst = xs_hbm.at[pl.ds(pl.multiple_of(tl * tile_rows, tile_rows), tile_rows)]
        return pltpu.make_async_copy(zeros_ref, dst, zsem)

    @pl.when(c == 0)
    def _():
        zeros_ref[...] = jnp.zeros_like(zeros_ref)

        def tail_tile(tl, carry):
            tail_copy(tl).start()
            tail_copy(tl).wait()
            return carry

        lax.fori_loop(nused_ref[0], xs_hbm.shape[0] // tile_rows, tail_tile, 0)

        def per_expert(e, carry):
            lo = padlo_ref[e]
            hi = padhi_ref[e]

            def start_row(r, cr):
                pad_copy(r).start()
                return cr

            def wait_row(r, cr):
                pad_copy(r).wait()
                return cr

            lax.fori_loop(lo, hi, start_row, 0)
            lax.fori_loop(lo, hi, wait_row, 0)
            return carry

        lax.fori_loop(0, N_EXPERTS, per_expert, 0)

    def row_copy(a):
        tok = lax.shift_right_logical(a, 2)
        dst = start_ref[eid_ref[a]] + rank_ref[a]
        return pltpu.make_async_copy(_token_row(x_ref, tok), _token_row(xs_hbm, dst), sem)

    def start_row(a, cr):
        row_copy(a).start()
        return cr

    def wait_row(a, cr):
        row_copy(a).wait()
        return cr

    lax.fori_loop(0, n_assign, start_row, 0)
    lax.fori_loop(0, n_assign, wait_row, 0)


def _dispatch(starts, padlo, padhi, n_used, eid, rank, xn2_rows, n_rows):
    t = xn2_rows.shape[0] // SUBLANES
    n_assign = DISPATCH_TOKENS * EXPERT_TOPK
    smem_blk = pl.BlockSpec((n_assign,), lambda c, *_: (c,), memory_space=pltpu.SMEM)
    grid_spec = pltpu.PrefetchScalarGridSpec(
        num_scalar_prefetch=4,
        grid=(t // DISPATCH_TOKENS,),
        in_specs=[smem_blk, smem_blk, pl.BlockSpec((DISPATCH_TOKENS * SUBLANES, LANES), lambda c, *_: (c, 0))],
        out_specs=pl.BlockSpec(memory_space=pl.ANY),
        scratch_shapes=[pltpu.VMEM((EXPERT_ROWS * SUBLANES, LANES), F32), pltpu.SemaphoreType.DMA(()),
                        pltpu.SemaphoreType.DMA(())],
    )
    return pl.pallas_call(
        _dispatch_kernel,
        grid_spec=grid_spec,
        out_shape=jax.ShapeDtypeStruct((n_rows * SUBLANES, LANES), F32),
        compiler_params=pltpu.CompilerParams(dimension_semantics=("arbitrary",)),
        name="dispatch",
    )(starts, padlo, padhi, n_used, eid, rank, xn2_rows)


def _expert_kernel(be_ref, nused_ref, x_ref, wgu_ref, bg_ref, bu_ref, wd_ref, bd_ref, y_ref, wgu_bf, wd_bf):
    t = pl.program_id(0)
    rows = y_ref.shape[0] // SUBLANES
    de = wd_ref.shape[0]
    grp = 2 * LANES
    active = t < nused_ref[0]
    fresh = jnp.logical_and(
        active, jnp.logical_or(t == 0, be_ref[t] != be_ref[jnp.maximum(t - 1, 0)]))

    @pl.when(fresh)
    def _():
        src = lax.broadcasted_iota(jnp.int32, (grp, grp), 0)
        dst = lax.broadcasted_iota(jnp.int32, (grp, grp), 1)
        wanted = jnp.where(dst < LANES, 2 * dst, 2 * (dst - LANES) + 1)
        perm = jnp.where(src == wanted, 1.0, 0.0).astype(BF16)
        for g in range(wgu_ref.shape[1] // grp):
            cols = slice(g * grp, (g + 1) * grp)
            wgu_bf[:, cols] = jnp.dot(wgu_ref[:, cols].astype(BF16), perm,
                                      preferred_element_type=F32).astype(BF16)
        wd_bf[...] = wd_ref[...].astype(BF16)

    @pl.when(active)
    def _():
        xb = _load_token_rows(x_ref, rows).astype(BF16)
        gu = jnp.dot(xb, wgu_bf[...], preferred_element_type=F32)
        hid = []
        for g in range(de // LANES):
            cols = slice(g * LANES, (g + 1) * LANES)
            gate = gu[:, g * grp:g * grp + LANES] + bg_ref[:, cols]
            up = gu[:, g * grp + LANES:(g + 1) * grp] + bu_ref[:, cols]
            gate = jnp.minimum(gate, SWIGLU_LIMIT)
            up = jnp.clip(up, -SWIGLU_LIMIT, SWIGLU_LIMIT)
            hid.append((gate * _sigmoid(SWIGLU_ALPHA * gate) * (up + 1.0)).astype(BF16))
        y = jnp.dot(jnp.concatenate(hid, axis=1), wd_bf[...], preferred_element_type=F32) + bd_ref[...]
        _store_token_rows(y_ref, y)

    @pl.when(jnp.logical_not(active))
    def _():
        y_ref[...] = jnp.zeros_like(y_ref)


def _experts(block_expert, n_used, xs_rows, wgu, bg, bu, wd, bd):
    n_rows = xs_rows.shape[0] // SUBLANES
    _, d, de2 = wgu.shape
    de = wd.shape[1]
    n_tiles = n_rows // EXPERT_ROWS
    tile_rows = EXPERT_ROWS * SUBLANES
    per_expert = lambda t, be, nu: (be[t], 0, 0)
    grid_spec = pltpu.PrefetchScalarGridSpec(
        num_scalar_prefetch=2,
        grid=(n_tiles,),
        in_specs=[
            pl.BlockSpec((tile_rows, LANES), lambda t, be, nu: (jnp.minimum(t, nu[0] - 1), 0)),
            pl.BlockSpec((None, d, de2), per_expert),
            pl.BlockSpec((None, 1, de), per_expert),
            pl.BlockSpec((None, 1, de), per_expert),
            pl.BlockSpec((None, de, d), per_expert),
            pl.BlockSpec((None, 1, d), per_expert),
        ],
        out_specs=pl.BlockSpec((tile_rows, LANES), lambda t, be, nu: (t, 0)),
        scratch_shapes=[pltpu.VMEM((d, de2), BF16), pltpu.VMEM((de, d), BF16)],
    )
    return pl.pallas_call(
        _expert_kernel,
        grid_spec=grid_spec,
        out_shape=jax.ShapeDtypeStruct((n_rows * SUBLANES, LANES), F32),
        compiler_params=pltpu.CompilerParams(
            dimension_semantics=("arbitrary",), vmem_limit_bytes=EXPERT_VMEM_LIMIT),
        name="experts",
    )(block_expert, n_used, xs_rows, wgu, bg, bu, wd, bd)


def _combine_kernel(start_ref, eid_ref, rank_ref, eid_next_ref, rank_next_ref, rw_ref, h1_ref, fg_ref, ys_hbm,
                    o_ref, buf_ref, sems):
    step = pl.program_id(0)
    n_steps = pl.num_programs(0)
    tt = h1_ref.shape[0]
    n_assign = eid_ref.shape[0]
    slot = lax.rem(step, 2)

    def gather(e_ref, r_ref, into, wait):
        def one(a, cr):
            tok = lax.shift_right_logical(a, 2)
            choice = lax.bitwise_and(a, EXPERT_TOPK - 1)
            src = start_ref[e_ref[a]] + r_ref[a]
            cp = pltpu.make_async_copy(_token_row(ys_hbm, src), _token_row(buf_ref.at[into], choice * tt + tok),
                                       sems.at[into])
            if wait:
                cp.wait()
            else:
                cp.start()
            return cr

        lax.fori_loop(0, n_assign, one, 0)

    @pl.when(step == 0)
    def _():
        gather(eid_ref, rank_ref, 0, wait=False)

    @pl.when(step + 1 < n_steps)
    def _():
        gather(eid_next_ref, rank_next_ref, 1 - slot, wait=False)

    gather(eid_ref, rank_ref, slot, wait=True)

    rw = rw_ref[...]
    weights = [jnp.broadcast_to(rw[:, kk:kk + 1], (tt, LANES)) for kk in range(EXPERT_TOPK)]
    rows = buf_ref.at[slot]
    pieces = []
    for c in range(SUBLANES):
        acc = h1_ref[:, c * LANES:(c + 1) * LANES]
        for kk in range(EXPERT_TOPK):
            acc = acc + weights[kk] * rows[pl.ds(kk * tt * SUBLANES + c, tt, stride=SUBLANES), :]
        pieces.append(acc)
    h = jnp.concatenate(pieces, axis=1)
    o_ref[...] = h * lax.rsqrt(jnp.mean(h * h, axis=-1, keepdims=True) + NORM_EPS) * fg_ref[...]


def _combine(starts, eid, rank, rw, h1, fg, ys_rows):
    t, d = h1.shape
    tt = COMBINE_TOKENS
    n_steps = t // tt
    n_assign = tt * EXPERT_TOPK
    smem_cur = pl.BlockSpec((n_assign,), lambda c, *_: (c,), memory_space=pltpu.SMEM)
    smem_next = pl.BlockSpec((n_assign,), lambda c, *_: (jnp.minimum(c + 1, n_steps - 1),),
                             memory_space=pltpu.SMEM)
    grid_spec = pltpu.PrefetchScalarGridSpec(
        num_scalar_prefetch=1,
        grid=(n_steps,),
        in_specs=[
            smem_cur, smem_cur, smem_next, smem_next,
            pl.BlockSpec((tt, LANES), lambda c, *_: (c, 0)),
            pl.BlockSpec((tt, d), lambda c, *_: (c, 0)),
            pl.BlockSpec((1, d), lambda c, *_: (0, 0)),
            pl.BlockSpec(memory_space=pl.ANY),
        ],
        out_specs=pl.BlockSpec((tt, d), lambda c, *_: (c, 0)),
        scratch_shapes=[pltpu.VMEM((2, EXPERT_TOPK * tt * SUBLANES, LANES), F32), pltpu.SemaphoreType.DMA((2,))],
    )
    return pl.pallas_call(
        _combine_kernel,
        grid_spec=grid_spec,
        out_shape=jax.ShapeDtypeStruct((t, d), F32),
        compiler_params=pltpu.CompilerParams(dimension_semantics=("arbitrary",), vmem_limit_bytes=VMEM_LIMIT),
        name="combine",
    )(starts, eid, rank, eid, rank, rw, h1, fg, ys_rows)


def _layer(h, mix_norm_g, w_in, w_attn_out, sgu_ln_g, sgu_ln_b, w_spatial, b_spatial, w_sgu_out,
           w_mix_out, ffn_norm_g, w_router, b_router, w_gate_up, b_gate_up, w_down, b_down, out_g):
    b, s, d = h.shape
    t = b * s
    sgu_width = sgu_ln_g.shape[0]
    x2 = h.reshape(t, d)

    q, k, v, kmean, u, vvn, ga, gs = _in_proj(
        x2, mix_norm_g.reshape(1, d), w_in.astype(BF16), sgu_ln_g.reshape(1, sgu_width),
        sgu_ln_b.reshape(1, sgu_width), sgu_width)

    slopes = 2.0 ** (-8.0 * jnp.arange(1, N_HEADS + 1, dtype=F32) / N_HEADS)
    nb = s // MOBA_BLOCK
    attn = _moba(q.reshape(b, s, ATTN_WIDTH), k.reshape(b, s, ATTN_WIDTH), v.reshape(b, s, ATTN_WIDTH),
                 kmean.reshape(b, nb, ATTN_WIDTH), slopes).reshape(t, ATTN_WIDTH)

    gdim = sgu_width // SGU_GROUPS
    bsp_full = jnp.repeat(b_spatial.T, gdim, axis=1)
    wr_pad = jnp.zeros((d, LANES), F32).at[:, :N_EXPERTS].set(w_router)
    br_pad = jnp.zeros((1, LANES), F32).at[0, :N_EXPERTS].set(b_router)
    h1, xn2_rows, ri, rw, cnt = _mix(
        attn, u, vvn, ga, gs, x2, w_attn_out.astype(BF16), w_sgu_out.astype(BF16), w_mix_out.astype(BF16),
        w_spatial, bsp_full, ffn_norm_g.reshape(1, d), wr_pad, br_pad)

    n_assign = t * EXPERT_TOPK
    n_tiles = -(-(n_assign + N_EXPERTS * (EXPERT_ROWS - 1)) // EXPERT_ROWS)
    n_rows = n_tiles * EXPERT_ROWS
    counts = cnt[0, :N_EXPERTS].astype(jnp.int32)
    padded = (counts + EXPERT_ROWS - 1) // EXPERT_ROWS * EXPERT_ROWS
    pad_end = jnp.cumsum(padded)
    starts = pad_end - padded
    tile_start = jnp.arange(n_tiles, dtype=jnp.int32) * EXPERT_ROWS
    block_expert = jnp.minimum(
        jnp.sum((pad_end[None, :] <= tile_start[:, None]).astype(jnp.int32), axis=1), N_EXPERTS - 1)
    n_used = (pad_end[-1:] // EXPERT_ROWS).astype(jnp.int32)
    eid = ri[:, :EXPERT_TOPK].reshape(n_assign)
    rank = ri[:, EXPERT_TOPK:2 * EXPERT_TOPK].reshape(n_assign)

    xs_rows = _dispatch(starts, starts + counts, pad_end, n_used, eid, rank, xn2_rows, n_rows)

    de = w_down.shape[1]
    bg = b_gate_up[:, 0::2].reshape(N_EXPERTS, 1, de)
    bu = b_gate_up[:, 1::2].reshape(N_EXPERTS, 1, de)
    ys_rows = _experts(block_expert, n_used, xs_rows, w_gate_up, bg, bu, w_down, b_down.reshape(N_EXPERTS, 1, d))

    out = _combine(starts, eid, rank, rw, h1, out_g.reshape(1, d), ys_rows)
    return out.reshape(b, s, d)


def kernel(x, mix_norm_g, w_in, w_attn_out, sgu_ln_g, sgu_ln_b, w_spatial, b_spatial, w_sgu_out, w_mix_out,
           ffn_norm_g, w_router, b_router, w_gate_up, b_gate_up, w_down, b_down, final_norm_g):
    depth = w_in.shape[0]
    assert depth == 1, "the final RMSNorm is fused into the single layer's combine step"
    return _layer(x, mix_norm_g[0], w_in[0], w_attn_out[0], sgu_ln_g[0], sgu_ln_b[0], w_spatial[0],
                  b_spatial[0], w_sgu_out[0], w_mix_out[0], ffn_norm_g[0], w_router[0], b_router[0],
                  w_gate_up[0], b_gate_up[0], w_down[0], b_down[0], final_norm_g)
```

```python
import jax
import jax.numpy as jnp
import numpy as np
from jax import lax
from jax.experimental import pallas as pl
from jax.experimental.pallas import tpu as pltpu

F32 = jnp.float32
BF16 = jnp.bfloat16
NEG_INF = float("-inf")
MASK_VALUE = -1e30


def _bf16_pieces(x, n=3):
    pieces = []
    for _ in range(n):
        p = float(np.asarray(x, np.float32).astype(BF16).astype(np.float32))
        pieces.append(p)
        x = x - p
    return pieces


LOG2E = 1.4426950408889634
LOG2E_PIECES = _bf16_pieces(LOG2E)

N_HEADS = 8
HEAD_DIM = 64
ATTN_WIDTH = N_HEADS * HEAD_DIM
MOBA_BLOCK = 256
MOBA_TOPK = 3
SGU_CHUNK = 128
SGU_GROUPS = 8
N_EXPERTS = 32
EXPERT_TOPK = 4
SWIGLU_LIMIT = 7.0
SWIGLU_ALPHA = 1.702
NORM_EPS = 1e-5

LANES = 128
SUBLANES = 8
HEADS_PER_LANE_TILE = LANES // HEAD_DIM
MOBA_GROUP = 2
MOBA_BIAS_LANE0 = 32
MOBA_DUMMY_LANE = LANES - 1
DMA_PRIORITIES = 2
PROJ_ROWS = 256
EXPERT_ROWS = 256
DISPATCH_TOKENS = 512
COMBINE_TOKENS = 128
VMEM_LIMIT = 48 * 1024 * 1024
EXPERT_VMEM_LIMIT = 56 * 1024 * 1024


def _sigmoid(x):
    return 1.0 / (1.0 + jnp.exp(-x))


def _store_token_rows(ref, value, base=0):
    n = value.shape[0]
    for c in range(value.shape[1] // LANES):
        ref[pl.ds(base + c, n, stride=SUBLANES), :] = value[:, c * LANES:(c + 1) * LANES]


def _load_token_rows(ref, n, base=0):
    return jnp.concatenate([ref[pl.ds(base + c, n, stride=SUBLANES), :] for c in range(SUBLANES)], axis=1)


def _token_row(ref, r):
    return ref.at[pl.ds(pl.multiple_of(r * SUBLANES, SUBLANES), SUBLANES)]


def _gelu_exact(x):
    return 0.5 * x * (1.0 + lax.erf(x * (0.5 ** 0.5)))


def _in_proj_kernel(x_ref, g_ref, w_ref, lng_ref, lnb_ref,
                    q_ref, kt_ref, v_ref, kmean_ref, u_ref, vvn_ref, ga_ref, gs_ref):
    x = x_ref[...]
    xn = x * lax.rsqrt(jnp.mean(x * x, axis=-1, keepdims=True) + NORM_EPS) * g_ref[...]
    xb = xn.astype(BF16)

    def proj(lo, hi):
        return jnp.dot(xb, w_ref[:, lo:hi], preferred_element_type=F32)

    a = ATTN_WIDTH
    sw = u_ref.shape[1]
    d = ga_ref.shape[1]
    q_ref[...] = proj(0, a)
    kf = proj(a, 2 * a)
    for j in range(kf.shape[0] // MOBA_BLOCK):
        kblk = kf[j * MOBA_BLOCK:(j + 1) * MOBA_BLOCK]
        kt_ref[j] = kblk.T.astype(BF16)
        kmean_ref[j] = jnp.mean(kblk, axis=0, keepdims=True)
    v_ref[...] = proj(2 * a, 3 * a).astype(BF16)
    z0 = 3 * a
    u_ref[...] = _gelu_exact(proj(z0, z0 + sw))
    zv = _gelu_exact(proj(z0 + sw, z0 + 2 * sw))
    mu = jnp.mean(zv, axis=-1, keepdims=True)
    zc = zv - mu
    var = jnp.mean(zc * zc, axis=-1, keepdims=True)
    vvn_ref[...] = zc * lax.rsqrt(var + NORM_EPS) * lng_ref[...] + lnb_ref[...]
    g0 = z0 + 2 * sw
    ga_ref[...] = _sigmoid(proj(g0, g0 + d))
    gs_ref[...] = _sigmoid(proj(g0 + d, g0 + 2 * d))


def _in_proj(x2, g, w_bf, lng, lnb, sgu_width):
    t, d = x2.shape
    tm = PROJ_ROWS
    n_in = w_bf.shape[1]
    row = lambda w: pl.BlockSpec((tm, w), lambda i: (i, 0))
    const = lambda shape: pl.BlockSpec(shape, lambda i: (0,) * len(shape))
    out_shape = (
        jax.ShapeDtypeStruct((t, ATTN_WIDTH), F32),
        jax.ShapeDtypeStruct((t // MOBA_BLOCK, ATTN_WIDTH, MOBA_BLOCK), BF16),
        jax.ShapeDtypeStruct((t, ATTN_WIDTH), BF16),
        jax.ShapeDtypeStruct((t // MOBA_BLOCK, 1, ATTN_WIDTH), F32),
        jax.ShapeDtypeStruct((t, sgu_width), F32),
        jax.ShapeDtypeStruct((t, sgu_width), F32),
        jax.ShapeDtypeStruct((t, d), F32),
        jax.ShapeDtypeStruct((t, d), F32),
    )
    out_specs = (
        row(ATTN_WIDTH),
        pl.BlockSpec((tm // MOBA_BLOCK, ATTN_WIDTH, MOBA_BLOCK), lambda i: (i, 0, 0)),
        row(ATTN_WIDTH),
        pl.BlockSpec((tm // MOBA_BLOCK, 1, ATTN_WIDTH), lambda i: (i, 0, 0)),
        row(sgu_width), row(sgu_width), row(d), row(d),
    )
    return pl.pallas_call(
        _in_proj_kernel,
        grid=(t // tm,),
        in_specs=[row(d), const((1, d)), const((d, n_in)), const((1, sgu_width)), const((1, sgu_width))],
        out_specs=out_specs,
        out_shape=out_shape,
        compiler_params=pltpu.CompilerParams(dimension_semantics=("arbitrary",), vmem_limit_bytes=VMEM_LIMIT),
        name="in_proj",
    )(x2, g, w_bf, lng, lnb)


def _alibi_slopes():
    slopes = 2.0 ** (-8.0 * np.arange(1, N_HEADS + 1, dtype=np.float64) / N_HEADS)
    assert all(np.log2(s) == np.round(np.log2(s)) for s in slopes), "ALiBi slopes must be powers of two"
    return slopes


def _moba_bias_lane(head_in_tile, part, piece):
    return MOBA_BIAS_LANE0 + (head_in_tile * 2 + part) * len(LOG2E_PIECES) + piece


def _moba_key_table(s):
    nb = s // MOBA_BLOCK
    slopes = _alibi_slopes()
    n_pairs = N_HEADS // HEADS_PER_LANE_TILE
    table = np.zeros((nb + 1, n_pairs, LANES, MOBA_BLOCK), np.float32)
    offs = np.arange(MOBA_BLOCK, dtype=np.float32)
    for j in range(nb):
        table[j, :, j, :] = 1.0
        for p in range(n_pairs):
            for hh in range(HEADS_PER_LANE_TILE):
                slope = slopes[p * HEADS_PER_LANE_TILE + hh]
                for piece in range(len(LOG2E_PIECES)):
                    table[j, p, _moba_bias_lane(hh, 0, piece), :] = slope * MOBA_BLOCK * j
                    table[j, p, _moba_bias_lane(hh, 1, piece), :] = slope * offs
    table[nb, :, MOBA_DUMMY_LANE, :] = 1.0
    as_bf16 = table.astype(BF16)
    assert np.array_equal(as_bf16.astype(np.float32), table), "bias table must be exact in bf16"
    return jnp.asarray(as_bf16.reshape(nb + 1, n_pairs * LANES, MOBA_BLOCK))


def _moba_kernel(q_ref, kt_ref, v_ref, kmt_ref, ct_ref, o_ref, qaug_ref, s_ref, mpart_ref, mrow_ref, acc_ref):
    i = pl.program_id(2)
    nb = kt_ref.shape[0]
    blk = MOBA_BLOCK
    q = q_ref[...]
    lane = lax.broadcasted_iota(jnp.int32, (1, LANES), 1)
    lane_f = lane.astype(F32)
    n_groups = lax.div(i + (MOBA_GROUP - 1), MOBA_GROUP)
    heads = range(HEADS_PER_LANE_TILE)
    hmasks = [(lane >= HEAD_DIM * hh) & (lane < HEAD_DIM * (hh + 1)) for hh in heads]

    for hh in heads:
        qh = jnp.where(hmasks[hh], q, 0.0)
        gate = jnp.dot(qh, kmt_ref[...], precision=lax.Precision.HIGHEST, preferred_element_type=F32)
        gate = jnp.where(lane < i, gate, NEG_INF)
        blockmask = jnp.where(lane == i, 0.0, MASK_VALUE)
        for kk in range(MOBA_TOPK):
            gmax = jnp.max(gate, axis=1, keepdims=True)
            first = jnp.min(jnp.where(gate == gmax, lane_f, float(LANES)), axis=1, keepdims=True)
            valid = (i > kk).astype(F32)
            first = first * valid + (valid - 1.0)
            hit = lane_f == first
            blockmask = jnp.where(hit, 0.0, blockmask)
            gate = jnp.where(hit, NEG_INF, gate)
        feats = jnp.where(lane == MOBA_DUMMY_LANE, MASK_VALUE, 0.0)
        for part in range(2):
            for piece, value in enumerate(LOG2E_PIECES):
                feats = jnp.where(lane == _moba_bias_lane(hh, part, piece), value, feats)
        extra = jnp.where(lane < nb, blockmask, feats)
        qaug_ref[hh, :, 0:LANES] = (qh * (HEAD_DIM ** -0.5 * LOG2E)).astype(BF16)
        qaug_ref[hh, :, LANES:2 * LANES] = extra.astype(BF16)

    def scores(hh, jk, jc):
        rhs = jnp.concatenate([kt_ref[jk], ct_ref[jc]], axis=0)
        return jnp.dot(qaug_ref[hh], rhs, preferred_element_type=F32)

    def lane_halves_max(s):
        return jnp.maximum(s[:, 0:LANES], s[:, LANES:2 * LANES])

    row_ge_col = (lax.broadcasted_iota(jnp.int32, (blk, blk), 0)
                  >= lax.broadcasted_iota(jnp.int32, (blk, blk), 1))
    for hh in heads:
        s_own = jnp.where(row_ge_col, scores(hh, i, i), MASK_VALUE)
        s_ref[hh, nb] = s_own
        mpart_ref[hh] = lane_halves_max(s_own)

    def past_block(g, u):
        j = g * MOBA_GROUP + u
        return j, jnp.minimum(j, nb - 1), jnp.where(j < i, j, nb)

    def pass1(g, carry):
        for hh in heads:
            part = mpart_ref[hh]
            for u in range(MOBA_GROUP):
                j, jk, jc = past_block(g, u)
                s = scores(hh, jk, jc)
                s_ref[hh, j] = s
                part = jnp.maximum(part, lane_halves_max(s))
            mpart_ref[hh] = part
        return carry

    lax.fori_loop(0, n_groups, pass1, 0)

    ones = jnp.ones((blk, LANES), BF16)
    for hh in heads:
        rowmax = jnp.max(mpart_ref[hh], axis=1, keepdims=True)
        mrow_ref[hh] = jnp.broadcast_to(rowmax, (blk, 2 * LANES))

    def weighted(hh, slot, jk):
        p = jnp.exp2(s_ref[hh, slot] - mrow_ref[hh]).astype(BF16)
        vaug = jnp.concatenate([v_ref[pl.ds(pl.multiple_of(jk * blk, blk), blk), :], ones], axis=1)
        return jnp.dot(p, vaug, preferred_element_type=F32)

    for hh in heads:
        acc_ref[hh] = weighted(hh, nb, i)

    def pass2(g, carry):
        for hh in heads:
            tot = acc_ref[hh]
            for u in range(MOBA_GROUP):
                j, jk, _ = past_block(g, u)
                tot = tot + weighted(hh, j, jk)
            acc_ref[hh] = tot
        return carry

    lax.fori_loop(0, n_groups, pass2, 0)

    out = jnp.zeros((blk, LANES), F32)
    for hh in heads:
        acc = acc_ref[hh]
        out = jnp.where(hmasks[hh], acc[:, 0:LANES] / acc[:, LANES:2 * LANES], out)
    o_ref[...] = out


def _moba(q, kt, v, kmean_t, key_table):
    b, s, a = q.shape
    nb = s // MOBA_BLOCK
    n_pairs = a // LANES
    heads = HEADS_PER_LANE_TILE
    return pl.pallas_call(
        _moba_kernel,
        grid=(b, n_pairs, nb),
        in_specs=[
            pl.BlockSpec((None, MOBA_BLOCK, LANES), lambda bi, p, i: (bi, i, p)),
            pl.BlockSpec((nb, LANES, MOBA_BLOCK), lambda bi, p, i: (bi, p, 0)),
            pl.BlockSpec((None, s, LANES), lambda bi, p, i: (bi, 0, p)),
            pl.BlockSpec((None, LANES, LANES), lambda bi, p, i: (bi, p, 0)),
            pl.BlockSpec((nb + 1, LANES, MOBA_BLOCK), lambda bi, p, i: (0, p, 0)),
        ],
        out_specs=pl.BlockSpec((None, MOBA_BLOCK, LANES), lambda bi, p, i: (bi, i, p)),
        out_shape=jax.ShapeDtypeStruct((b, s, a), F32),
        scratch_shapes=[
            pltpu.VMEM((heads, MOBA_BLOCK, 2 * LANES), BF16),
            pltpu.VMEM((heads, nb + 1, MOBA_BLOCK, MOBA_BLOCK), F32),
            pltpu.VMEM((heads, MOBA_BLOCK, LANES), F32),
            pltpu.VMEM((heads, MOBA_BLOCK, 2 * LANES), F32),
            pltpu.VMEM((heads, MOBA_BLOCK, 2 * LANES), F32),
        ],
        compiler_params=pltpu.CompilerParams(
            dimension_semantics=("arbitrary", "arbitrary", "arbitrary"), vmem_limit_bytes=VMEM_LIMIT),
        name="moba",
    )(q, kt, v, kmean_t, key_table)


def _mix_kernel(attn_ref, u_ref, vvn_ref, ga_ref, gs_ref, x_ref,
                wao_ref, wso_ref, wmo_ref, wsp_ref, bsp_ref, fg_ref, wr_ref, br_ref,
                h1_ref, xn2_ref, ri_ref, rw_ref, cnt_ref, carry_ref):
    step = pl.program_id(0)
    tm = x_ref.shape[0]

    @pl.when(step == 0)
    def _():
        carry_ref[...] = jnp.zeros_like(carry_ref)

    y_attn = jnp.dot(attn_ref[...].astype(BF16), wao_ref[...], preferred_element_type=F32)

    ch = SGU_CHUNK
    lane = lax.broadcasted_iota(jnp.int32, (1, LANES), 1)
    tril = lax.broadcasted_iota(jnp.int32, (ch, ch), 0) >= lax.broadcasted_iota(jnp.int32, (ch, ch), 1)
    gdim = vvn_ref.shape[1] // SGU_GROUPS
    groups_per_tile = LANES // gdim
    w_causal = [jnp.where(tril, wsp_ref[g], 0.0).astype(BF16) for g in range(SGU_GROUPS)]
    rows = []
    for c in range(tm // ch):
        cols = []
        for ct in range(vvn_ref.shape[1] // LANES):
            vp = vvn_ref[c * ch:(c + 1) * ch, ct * LANES:(ct + 1) * LANES]
            acc = jnp.zeros((ch, LANES), F32)
            for gg in range(groups_per_tile):
                gmask = (lane >= gdim * gg) & (lane < gdim * (gg + 1))
                vm = jnp.where(gmask, vp, 0.0).astype(BF16)
                acc = acc + jnp.dot(w_causal[ct * groups_per_tile + gg], vm, preferred_element_type=F32)
            cols.append(acc)
        rows.append(jnp.concatenate(cols, axis=1) + bsp_ref[...])
    mixed = jnp.concatenate(rows, axis=0)
    sgu = u_ref[...] * mixed
    y_sgu = jnp.dot(sgu.astype(BF16), wso_ref[...], preferred_element_type=F32)

    merged = ga_ref[...] * y_attn + gs_ref[...] * y_sgu
    h1 = x_ref[...] + jnp.dot(merged.astype(BF16), wmo_ref[...], preferred_element_type=F32)
    h1_ref[...] = h1
    xn2 = h1 * lax.rsqrt(jnp.mean(h1 * h1, axis=-1, keepdims=True) + NORM_EPS) * fg_ref[...]
    _store_token_rows(xn2_ref, xn2)

    logits = jnp.dot(xn2, wr_ref[...], precision=lax.Precision.HIGHEST,
                     preferred_element_type=F32) + br_ref[...]
    work = jnp.where(lane < N_EXPERTS, logits, NEG_INF)
    lane_f = lane.astype(F32)
    vals, idxs = [], []
    for _ in range(EXPERT_TOPK):
        vmax = jnp.max(work, axis=1, keepdims=True)
        first = jnp.min(jnp.where(work == vmax, lane_f, float(LANES)), axis=1, keepdims=True)
        vals.append(vmax)
        idxs.append(first)
        work = jnp.where(lane_f == first, NEG_INF, work)
    exps = [jnp.exp(v - vals[0]) for v in vals]
    denom = exps[0]
    for e in exps[1:]:
        denom = denom + e
    chosen = jnp.zeros((tm, LANES), F32)
    for first in idxs:
        chosen = jnp.where(lane_f == first, 1.0, chosen)

    strict_lower = (lax.broadcasted_iota(jnp.int32, (tm, tm), 0)
                    > lax.broadcasted_iota(jnp.int32, (tm, tm), 1))
    before = jnp.dot(jnp.where(strict_lower, 1.0, 0.0).astype(BF16), chosen.astype(BF16),
                     preferred_element_type=F32) + carry_ref[0:1, :]
    ri = jnp.zeros((tm, LANES), jnp.int32)
    rw = jnp.zeros((tm, LANES), F32)
    for kk in range(EXPERT_TOPK):
        rank = jnp.sum(jnp.where(lane_f == idxs[kk], before, 0.0), axis=1, keepdims=True)
        ri = jnp.where(lane == kk, idxs[kk].astype(jnp.int32), ri)
        ri = jnp.where(lane == EXPERT_TOPK + kk, rank.astype(jnp.int32), ri)
        rw = jnp.where(lane == kk, exps[kk] / denom, rw)
    ri_ref[...] = ri
    rw_ref[...] = rw
    total = carry_ref[0:1, :] + jnp.sum(chosen, axis=0, keepdims=True)
    carry_ref[...] = jnp.broadcast_to(total, carry_ref.shape)
    cnt_ref[...] = jnp.broadcast_to(total, cnt_ref.shape)


def _mix(attn, u, vvn, ga, gs, x2, wao, wso, wmo, wsp, bsp_full, fg, wr_pad, br_pad):
    t, d = x2.shape
    tm = PROJ_ROWS
    row = lambda w: pl.BlockSpec((tm, w), lambda i: (i, 0))
    const = lambda shape: pl.BlockSpec(shape, lambda i: (0,) * len(shape))
    ins = (attn, u, vvn, ga, gs, x2, wao, wso, wmo, wsp, bsp_full, fg, wr_pad, br_pad)
    in_specs = [row(a.shape[1]) for a in ins[:6]] + [const(a.shape) for a in ins[6:]]
    out_shape = (
        jax.ShapeDtypeStruct((t, d), F32),
        jax.ShapeDtypeStruct((t * SUBLANES, LANES), F32),
        jax.ShapeDtypeStruct((t, LANES), jnp.int32),
        jax.ShapeDtypeStruct((t, LANES), F32),
        jax.ShapeDtypeStruct((8, LANES), F32),
    )
    assert d == SUBLANES * LANES, "a token row is stored as one (8, 128) tile"
    token_rows = pl.BlockSpec((tm * SUBLANES, LANES), lambda i: (i, 0))
    out_specs = (row(d), token_rows, row(LANES), row(LANES), const((8, LANES)))
    return pl.pallas_call(
        _mix_kernel,
        grid=(t // tm,),
        in_specs=in_specs,
        out_specs=out_specs,
        out_shape=out_shape,
        scratch_shapes=[pltpu.VMEM((8, LANES), F32)],
        compiler_params=pltpu.CompilerParams(dimension_semantics=("arbitrary",), vmem_limit_bytes=VMEM_LIMIT),
        name="mix_route",
    )(*ins)


def _dispatch_kernel(start_ref, padlo_ref, padhi_ref, nused_ref, eid_ref, rank_ref, x_ref, xs_hbm,
                     zeros_ref, sem, zsem):
    c = pl.program_id(0)
    n_assign = eid_ref.shape[0]
    tile_rows = zeros_ref.shape[0]

    def pad_copy(r):
        return pltpu.make_async_copy(zeros_ref.at[pl.ds(0, SUBLANES)], _token_row(xs_hbm, r), zsem)

    def tail_copy(tl):
        dst = xs_hbm.at[pl.ds(pl.multiple_of(tl * tile_rows, tile_rows), tile_rows)]
        return pltpu.make_async_copy(zeros_ref, dst, zsem)

    @pl.when(c == 0)
    def _():
        zeros_ref[...] = jnp.zeros_like(zeros_ref)

        def tail_tile(tl, carry):
            tail_copy(tl).start()
            tail_copy(tl).wait()
            return carry

        lax.fori_loop(nused_ref[0], xs_hbm.shape[0] // tile_rows, tail_tile, 0)

        def per_expert(e, carry):
            lo = padlo_ref[e]
            hi = padhi_ref[e]

            def start_row(r, cr):
                pad_copy(r).start()
                return cr

            def wait_row(r, cr):
                pad_copy(r).wait()
                return cr

            lax.fori_loop(lo, hi, start_row, 0)
            lax.fori_loop(lo, hi, wait_row, 0)
            return carry

        lax.fori_loop(0, N_EXPERTS, per_expert, 0)

    def row_copy(a):
        tok = lax.shift_right_logical(a, 2)
        dst = start_ref[eid_ref[a]] + rank_ref[a]
        return pltpu.make_async_copy(_token_row(x_ref, tok), _token_row(xs_hbm, dst), sem)

    def start_rows(a2, cr):
        for prio in range(DMA_PRIORITIES):
            row_copy(a2 * DMA_PRIORITIES + prio).start(priority=prio)
        return cr

    def wait_row(a, cr):
        row_copy(a).wait()
        return cr

    lax.fori_loop(0, n_assign // DMA_PRIORITIES, start_rows, 0)
    lax.fori_loop(0, n_assign, wait_row, 0)


def _dispatch(starts, padlo, padhi, n_used, eid, rank, xn2_rows, n_rows):
    t = xn2_rows.shape[0] // SUBLANES
    n_assign = DISPATCH_TOKENS * EXPERT_TOPK
    smem_blk = pl.BlockSpec((n_assign,), lambda c, *_: (c,), memory_space=pltpu.SMEM)
    grid_spec = pltpu.PrefetchScalarGridSpec(
        num_scalar_prefetch=4,
        grid=(t // DISPATCH_TOKENS,),
        in_specs=[smem_blk, smem_blk, pl.BlockSpec((DISPATCH_TOKENS * SUBLANES, LANES), lambda c, *_: (c, 0))],
        out_specs=pl.BlockSpec(memory_space=pl.ANY),
        scratch_shapes=[pltpu.VMEM((EXPERT_ROWS * SUBLANES, LANES), F32), pltpu.SemaphoreType.DMA(()),
                        pltpu.SemaphoreType.DMA(())],
    )
    return pl.pallas_call(
        _dispatch_kernel,
        grid_spec=grid_spec,
        out_shape=jax.ShapeDtypeStruct((n_rows * SUBLANES, LANES), F32),
        compiler_params=pltpu.CompilerParams(dimension_semantics=("arbitrary",)),
        name="dispatch",
    )(starts, padlo, padhi, n_used, eid, rank, xn2_rows)


def _expert_kernel(be_ref, nused_ref, x_ref, wgu_ref, bg_ref, bu_ref, wd_ref, bd_ref, y_ref, wgu_bf, wd_bf):
    t = pl.program_id(0)
    rows = y_ref.shape[0] // SUBLANES
    de = wd_ref.shape[0]
    grp = 2 * LANES
    active = t < nused_ref[0]
    fresh = jnp.logical_and(
        active, jnp.logical_or(t == 0, be_ref[t] != be_ref[jnp.maximum(t - 1, 0)]))

    @pl.when(fresh)
    def _():
        src = lax.broadcasted_iota(jnp.int32, (grp, grp), 0)
        dst = lax.broadcasted_iota(jnp.int32, (grp, grp), 1)
        wanted = jnp.where(dst < LANES, 2 * dst, 2 * (dst - LANES) + 1)
        perm = jnp.where(src == wanted, 1.0, 0.0).astype(BF16)
        for g in range(wgu_ref.shape[1] // grp):
            cols = slice(g * grp, (g + 1) * grp)
            wgu_bf[:, cols] = jnp.dot(wgu_ref[:, cols].astype(BF16), perm,
                                      preferred_element_type=F32).astype(BF16)
        wd_bf[...] = wd_ref[...].astype(BF16)

    @pl.when(active)
    def _():
        xb = _load_token_rows(x_ref, rows).astype(BF16)
        gu = jnp.dot(xb, wgu_bf[...], preferred_element_type=F32)
        hid = []
        for g in range(de // LANES):
            cols = slice(g * LANES, (g + 1) * LANES)
            gate = gu[:, g * grp:g * grp + LANES] + bg_ref[:, cols]
            up = gu[:, g * grp + LANES:(g + 1) * grp] + bu_ref[:, cols]
            gate = jnp.minimum(gate, SWIGLU_LIMIT)
            up = jnp.clip(up, -SWIGLU_LIMIT, SWIGLU_LIMIT)
            hid.append((gate * _sigmoid(SWIGLU_ALPHA * gate) * (up + 1.0)).astype(BF16))
        y = jnp.dot(jnp.concatenate(hid, axis=1), wd_bf[...], preferred_element_type=F32) + bd_ref[...]
        _store_token_rows(y_ref, y)

    @pl.when(jnp.logical_not(active))
    def _():
        y_ref[...] = jnp.zeros_like(y_ref)


def _experts(block_expert, n_used, xs_rows, wgu, bg, bu, wd, bd):
    n_rows = xs_rows.shape[0] // SUBLANES
    _, d, de2 = wgu.shape
    de = wd.shape[1]
    n_tiles = n_rows // EXPERT_ROWS
    tile_rows = EXPERT_ROWS * SUBLANES
    per_expert = lambda t, be, nu: (be[t], 0, 0)
    grid_spec = pltpu.PrefetchScalarGridSpec(
        num_scalar_prefetch=2,
        grid=(n_tiles,),
        in_specs=[
            pl.BlockSpec((tile_rows, LANES), lambda t, be, nu: (jnp.minimum(t, nu[0] - 1), 0)),
            pl.BlockSpec((None, d, de2), per_expert),
            pl.BlockSpec((None, 1, de), per_expert),
            pl.BlockSpec((None, 1, de), per_expert),
            pl.BlockSpec((None, de, d), per_expert),
            pl.BlockSpec((None, 1, d), per_expert),
        ],
        out_specs=pl.BlockSpec((tile_rows, LANES), lambda t, be, nu: (t, 0)),
        scratch_shapes=[pltpu.VMEM((d, de2), BF16), pltpu.VMEM((de, d), BF16)],
    )
    return pl.pallas_call(
        _expert_kernel,
        grid_spec=grid_spec,
        out_shape=jax.ShapeDtypeStruct((n_rows * SUBLANES, LANES), F32),
        compiler_params=pltpu.CompilerParams(
            dimension_semantics=("arbitrary",), vmem_limit_bytes=EXPERT_VMEM_LIMIT),
        name="experts",
    )(block_expert, n_used, xs_rows, wgu, bg, bu, wd, bd)


def _combine_kernel(start_ref, eid_ref, rank_ref, eid_next_ref, rank_next_ref, rw_ref, h1_ref, fg_ref, ys_hbm,
                    o_ref, buf_ref, sems):
    step = pl.program_id(0)
    n_steps = pl.num_programs(0)
    tt = h1_ref.shape[0]
    n_assign = eid_ref.shape[0]
    slot = lax.rem(step, 2)

    def gather(e_ref, r_ref, into, wait):
        def row_copy(a):
            tok = lax.shift_right_logical(a, 2)
            choice = lax.bitwise_and(a, EXPERT_TOPK - 1)
            src = start_ref[e_ref[a]] + r_ref[a]
            return pltpu.make_async_copy(_token_row(ys_hbm, src),
                                         _token_row(buf_ref.at[into], choice * tt + tok), sems.at[into])

        def start_rows(a2, cr):
            for prio in range(DMA_PRIORITIES):
                row_copy(a2 * DMA_PRIORITIES + prio).start(priority=prio)
            return cr

        def wait_row(a, cr):
            row_copy(a).wait()
            return cr

        if wait:
            lax.fori_loop(0, n_assign, wait_row, 0)
        else:
            lax.fori_loop(0, n_assign // DMA_PRIORITIES, start_rows, 0)

    @pl.when(step == 0)
    def _():
        gather(eid_ref, rank_ref, 0, wait=False)

    @pl.when(step + 1 < n_steps)
    def _():
        gather(eid_next_ref, rank_next_ref, 1 - slot, wait=False)

    gather(eid_ref, rank_ref, slot, wait=True)

    rw = rw_ref[...]
    weights = [jnp.broadcast_to(rw[:, kk:kk + 1], (tt, LANES)) for kk in range(EXPERT_TOPK)]
    rows = buf_ref.at[slot]
    pieces = []
    for c in range(SUBLANES):
        acc = h1_ref[:, c * LANES:(c + 1) * LANES]
        for kk in range(EXPERT_TOPK):
            acc = acc + weights[kk] * rows[pl.ds(kk * tt * SUBLANES + c, tt, stride=SUBLANES), :]
        pieces.append(acc)
    h = jnp.concatenate(pieces, axis=1)
    o_ref[...] = h * lax.rsqrt(jnp.mean(h * h, axis=-1, keepdims=True) + NORM_EPS) * fg_ref[...]


def _combine(starts, eid, rank, rw, h1, fg, ys_rows):
    t, d = h1.shape
    tt = COMBINE_TOKENS
    n_steps = t // tt
    n_assign = tt * EXPERT_TOPK
    smem_cur = pl.BlockSpec((n_assign,), lambda c, *_: (c,), memory_space=pltpu.SMEM)
    smem_next = pl.BlockSpec((n_assign,), lambda c, *_: (jnp.minimum(c + 1, n_steps - 1),),
                             memory_space=pltpu.SMEM)
    grid_spec = pltpu.PrefetchScalarGridSpec(
        num_scalar_prefetch=1,
        grid=(n_steps,),
        in_specs=[
            smem_cur, smem_cur, smem_next, smem_next,
            pl.BlockSpec((tt, LANES), lambda c, *_: (c, 0)),
            pl.BlockSpec((tt, d), lambda c, *_: (c, 0)),
            pl.BlockSpec((1, d), lambda c, *_: (0, 0)),
            pl.BlockSpec(memory_space=pl.ANY),
        ],
        out_specs=pl.BlockSpec((tt, d), lambda c, *_: (c, 0)),
        scratch_shapes=[pltpu.VMEM((2, EXPERT_TOPK * tt * SUBLANES, LANES), F32), pltpu.SemaphoreType.DMA((2,))],
    )
    return pl.pallas_call(
        _combine_kernel,
        grid_spec=grid_spec,
        out_shape=jax.ShapeDtypeStruct((t, d), F32),
        compiler_params=pltpu.CompilerParams(dimension_semantics=("arbitrary",), vmem_limit_bytes=VMEM_LIMIT),
        name="combine",
    )(starts, eid, rank, eid, rank, rw, h1, fg, ys_rows)


def _layer(h, mix_norm_g, w_in, w_attn_out, sgu_ln_g, sgu_ln_b, w_spatial, b_spatial, w_sgu_out,
           w_mix_out, ffn_norm_g, w_router, b_router, w_gate_up, b_gate_up, w_down, b_down, out_g):
    b, s, d = h.shape
    t = b * s
    sgu_width = sgu_ln_g.shape[0]
    x2 = h.reshape(t, d)

    q, kt, v, kmean, u, vvn, ga, gs = _in_proj(
        x2, mix_norm_g.reshape(1, d), w_in.astype(BF16), sgu_ln_g.reshape(1, sgu_width),
        sgu_ln_b.reshape(1, sgu_width), sgu_width)

    nb = s // MOBA_BLOCK
    assert nb <= MOBA_BIAS_LANE0, "block one-hot features must not overlap the ALiBi features"
    kmean_t = jnp.pad(kmean.reshape(b, nb, ATTN_WIDTH).transpose(0, 2, 1), ((0, 0), (0, 0), (0, LANES - nb)))
    attn = _moba(q.reshape(b, s, ATTN_WIDTH), kt, v.reshape(b, s, ATTN_WIDTH), kmean_t,
                 _moba_key_table(s)).reshape(t, ATTN_WIDTH)

    gdim = sgu_width // SGU_GROUPS
    bsp_full = jnp.repeat(b_spatial.T, gdim, axis=1)
    wr_pad = jnp.zeros((d, LANES), F32).at[:, :N_EXPERTS].set(w_router)
    br_pad = jnp.zeros((1, LANES), F32).at[0, :N_EXPERTS].set(b_router)
    h1, xn2_rows, ri, rw, cnt = _mix(
        attn, u, vvn, ga, gs, x2, w_attn_out.astype(BF16), w_sgu_out.astype(BF16), w_mix_out.astype(BF16),
        w_spatial, bsp_full, ffn_norm_g.reshape(1, d), wr_pad, br_pad)

    n_assign = t * EXPERT_TOPK
    n_tiles = -(-(n_assign + N_EXPERTS * (EXPERT_ROWS - 1)) // EXPERT_ROWS)
    n_rows = n_tiles * EXPERT_ROWS
    counts = cnt[0, :N_EXPERTS].astype(jnp.int32)
    padded = (counts + EXPERT_ROWS - 1) // EXPERT_ROWS * EXPERT_ROWS
    pad_end = jnp.cumsum(padded)
    starts = pad_end - padded
    tile_start = jnp.arange(n_tiles, dtype=jnp.int32) * EXPERT_ROWS
    block_expert = jnp.minimum(
        jnp.sum((pad_end[None, :] <= tile_start[:, None]).astype(jnp.int32), axis=1), N_EXPERTS - 1)
    n_used = (pad_end[-1:] // EXPERT_ROWS).astype(jnp.int32)
    eid = ri[:, :EXPERT_TOPK].reshape(n_assign)
    rank = ri[:, EXPERT_TOPK:2 * EXPERT_TOPK].reshape(n_assign)

    xs_rows = _dispatch(starts, starts + counts, pad_end, n_used, eid, rank, xn2_rows, n_rows)

    de = w_down.shape[1]
    bg = b_gate_up[:, 0::2].reshape(N_EXPERTS, 1, de)
    bu = b_gate_up[:, 1::2].reshape(N_EXPERTS, 1, de)
    ys_rows = _experts(block_expert, n_used, xs_rows, w_gate_up, bg, bu, w_down, b_down.reshape(N_EXPERTS, 1, d))

    out = _combine(starts, eid, rank, rw, h1, out_g.reshape(1, d), ys_rows)
    return out.reshape(b, s, d)


def kernel(x, mix_norm_g, w_in, w_attn_out, sgu_ln_g, sgu_ln_b, w_spatial, b_spatial, w_sgu_out, w_mix_out,
           ffn_norm_g, w_router, b_router, w_gate_up, b_gate_up, w_down, b_down, final_norm_g):
    depth = w_in.shape[0]
    assert depth == 1, "the final RMSNorm is fused into the single layer's combine step"
    return _layer(x, mix_norm_g[0], w_in[0], w_attn_out[0], sgu_ln_g[0], sgu_ln_b[0], w_spatial[0],
                  b_spatial[0], w_sgu_out[0], w_mix_out[0], ffn_norm_g[0], w_router[0], b_router[0],
                  w_gate_up[0], b_gate_up[0], w_down[0], b_down[0], final_norm_g)
```

```python
import jax
import jax.numpy as jnp
import numpy as np
from jax import lax
from jax.experimental import pallas as pl
from jax.experimental.pallas import tpu as pltpu

F32 = jnp.float32
BF16 = jnp.bfloat16
NEG_INF = float("-inf")
MASK_VALUE = -1e30


def _bf16_pieces(x, n=3):
    pieces = []
    for _ in range(n):
        p = float(np.asarray(x, np.float32).astype(BF16).astype(np.float32))
        pieces.append(p)
        x = x - p
    return pieces


LOG2E = 1.4426950408889634
LOG2E_PIECES = _bf16_pieces(LOG2E)

N_HEADS = 8
HEAD_DIM = 64
ATTN_WIDTH = N_HEADS * HEAD_DIM
MOBA_BLOCK = 256
MOBA_TOPK = 3
SGU_CHUNK = 128
SGU_GROUPS = 8
N_EXPERTS = 32
EXPERT_TOPK = 4
SWIGLU_LIMIT = 7.0
SWIGLU_ALPHA = 1.702
NORM_EPS = 1e-5

LANES = 128
SUBLANES = 8
HEADS_PER_LANE_TILE = LANES // HEAD_DIM
MOBA_GROUP = 2
MOBA_BIAS_LANE0 = 32
MOBA_DUMMY_LANE = LANES - 1
PROJ_ROWS = 256
EXPERT_ROWS = 256
ROUTE_TOKENS = PROJ_ROWS
VMEM_LIMIT = 48 * 1024 * 1024
EXPERT_VMEM_LIMIT = 56 * 1024 * 1024


def _sigmoid(x):
    return 1.0 / (1.0 + jnp.exp(-x))


def _store_token_rows(ref, value, base=0):
    n = value.shape[0]
    for c in range(value.shape[1] // LANES):
        ref[pl.ds(base + c, n, stride=SUBLANES), :] = value[:, c * LANES:(c + 1) * LANES]


def _load_token_rows(ref, n, base=0):
    return jnp.concatenate([ref[pl.ds(base + c, n, stride=SUBLANES), :] for c in range(SUBLANES)], axis=1)


def _gelu_exact(x):
    return 0.5 * x * (1.0 + lax.erf(x * (0.5 ** 0.5)))


def _in_proj_kernel(x_ref, g_ref, w_ref, lng_ref, lnb_ref,
                    q_ref, kt_ref, v_ref, kmean_ref, u_ref, vvn_ref, ga_ref, gs_ref):
    x = x_ref[...]
    xn = x * lax.rsqrt(jnp.mean(x * x, axis=-1, keepdims=True) + NORM_EPS) * g_ref[...]
    xb = xn.astype(BF16)

    def proj(lo, hi):
        return jnp.dot(xb, w_ref[:, lo:hi], preferred_element_type=F32)

    a = ATTN_WIDTH
    sw = u_ref.shape[1]
    d = ga_ref.shape[1]
    q_ref[...] = proj(0, a)
    kf = proj(a, 2 * a)
    for j in range(kf.shape[0] // MOBA_BLOCK):
        kblk = kf[j * MOBA_BLOCK:(j + 1) * MOBA_BLOCK]
        kt_ref[j] = kblk.T.astype(BF16)
        kmean_ref[j] = jnp.mean(kblk, axis=0, keepdims=True)
    v_ref[...] = proj(2 * a, 3 * a).astype(BF16)
    z0 = 3 * a
    u_ref[...] = _gelu_exact(proj(z0, z0 + sw))
    zv = _gelu_exact(proj(z0 + sw, z0 + 2 * sw))
    mu = jnp.mean(zv, axis=-1, keepdims=True)
    zc = zv - mu
    var = jnp.mean(zc * zc, axis=-1, keepdims=True)
    vvn_ref[...] = zc * lax.rsqrt(var + NORM_EPS) * lng_ref[...] + lnb_ref[...]
    g0 = z0 + 2 * sw
    ga_ref[...] = _sigmoid(proj(g0, g0 + d))
    gs_ref[...] = _sigmoid(proj(g0 + d, g0 + 2 * d))


def _in_proj(x2, g, w_bf, lng, lnb, sgu_width):
    t, d = x2.shape
    tm = PROJ_ROWS
    n_in = w_bf.shape[1]
    row = lambda w: pl.BlockSpec((tm, w), lambda i: (i, 0))
    const = lambda shape: pl.BlockSpec(shape, lambda i: (0,) * len(shape))
    out_shape = (
        jax.ShapeDtypeStruct((t, ATTN_WIDTH), F32),
        jax.ShapeDtypeStruct((t // MOBA_BLOCK, ATTN_WIDTH, MOBA_BLOCK), BF16),
        jax.ShapeDtypeStruct((t, ATTN_WIDTH), BF16),
        jax.ShapeDtypeStruct((t // MOBA_BLOCK, 1, ATTN_WIDTH), F32),
        jax.ShapeDtypeStruct((t, sgu_width), F32),
        jax.ShapeDtypeStruct((t, sgu_width), F32),
        jax.ShapeDtypeStruct((t, d), F32),
        jax.ShapeDtypeStruct((t, d), F32),
    )
    out_specs = (
        row(ATTN_WIDTH),
        pl.BlockSpec((tm // MOBA_BLOCK, ATTN_WIDTH, MOBA_BLOCK), lambda i: (i, 0, 0)),
        row(ATTN_WIDTH),
        pl.BlockSpec((tm // MOBA_BLOCK, 1, ATTN_WIDTH), lambda i: (i, 0, 0)),
        row(sgu_width), row(sgu_width), row(d), row(d),
    )
    return pl.pallas_call(
        _in_proj_kernel,
        grid=(t // tm,),
        in_specs=[row(d), const((1, d)), const((d, n_in)), const((1, sgu_width)), const((1, sgu_width))],
        out_specs=out_specs,
        out_shape=out_shape,
        compiler_params=pltpu.CompilerParams(dimension_semantics=("arbitrary",), vmem_limit_bytes=VMEM_LIMIT),
        name="in_proj",
    )(x2, g, w_bf, lng, lnb)


def _alibi_slopes():
    slopes = 2.0 ** (-8.0 * np.arange(1, N_HEADS + 1, dtype=np.float64) / N_HEADS)
    assert all(np.log2(s) == np.round(np.log2(s)) for s in slopes), "ALiBi slopes must be powers of two"
    return slopes


def _moba_bias_lane(head_in_tile, part, piece):
    return MOBA_BIAS_LANE0 + (head_in_tile * 2 + part) * len(LOG2E_PIECES) + piece


def _moba_key_table(s):
    nb = s // MOBA_BLOCK
    slopes = _alibi_slopes()
    n_pairs = N_HEADS // HEADS_PER_LANE_TILE
    table = np.zeros((nb + 1, n_pairs, LANES, MOBA_BLOCK), np.float32)
    offs = np.arange(MOBA_BLOCK, dtype=np.float32)
    for j in range(nb):
        table[j, :, j, :] = 1.0
        for p in range(n_pairs):
            for hh in range(HEADS_PER_LANE_TILE):
                slope = slopes[p * HEADS_PER_LANE_TILE + hh]
                for piece in range(len(LOG2E_PIECES)):
                    table[j, p, _moba_bias_lane(hh, 0, piece), :] = slope * MOBA_BLOCK * j
                    table[j, p, _moba_bias_lane(hh, 1, piece), :] = slope * offs
    table[nb, :, MOBA_DUMMY_LANE, :] = 1.0
    as_bf16 = table.astype(BF16)
    assert np.array_equal(as_bf16.astype(np.float32), table), "bias table must be exact in bf16"
    return jnp.asarray(as_bf16.reshape(nb + 1, n_pairs * LANES, MOBA_BLOCK))


def _moba_kernel(q_ref, kt_ref, v_ref, kmt_ref, ct_ref, o_ref, qaug_ref, s_ref, mpart_ref, mrow_ref, acc_ref):
    i = pl.program_id(2)
    nb = kt_ref.shape[0]
    blk = MOBA_BLOCK
    q = q_ref[...]
    lane = lax.broadcasted_iota(jnp.int32, (1, LANES), 1)
    lane_f = lane.astype(F32)
    n_groups = lax.div(i + (MOBA_GROUP - 1), MOBA_GROUP)
    heads = range(HEADS_PER_LANE_TILE)
    hmasks = [(lane >= HEAD_DIM * hh) & (lane < HEAD_DIM * (hh + 1)) for hh in heads]

    for hh in heads:
        qh = jnp.where(hmasks[hh], q, 0.0)
        gate = jnp.dot(qh, kmt_ref[...], precision=lax.Precision.HIGHEST, preferred_element_type=F32)
        gate = jnp.where(lane < i, gate, NEG_INF)
        blockmask = jnp.where(lane == i, 0.0, MASK_VALUE)
        for kk in range(MOBA_TOPK):
            gmax = jnp.max(gate, axis=1, keepdims=True)
            first = jnp.min(jnp.where(gate == gmax, lane_f, float(LANES)), axis=1, keepdims=True)
            valid = (i > kk).astype(F32)
            first = first * valid + (valid - 1.0)
            hit = lane_f == first
            blockmask = jnp.where(hit, 0.0, blockmask)
            gate = jnp.where(hit, NEG_INF, gate)
        feats = jnp.where(lane == MOBA_DUMMY_LANE, MASK_VALUE, 0.0)
        for part in range(2):
            for piece, value in enumerate(LOG2E_PIECES):
                feats = jnp.where(lane == _moba_bias_lane(hh, part, piece), value, feats)
        extra = jnp.where(lane < nb, blockmask, feats)
        qaug_ref[hh, :, 0:LANES] = (qh * (HEAD_DIM ** -0.5 * LOG2E)).astype(BF16)
        qaug_ref[hh, :, LANES:2 * LANES] = extra.astype(BF16)

    def scores(hh, jk, jc):
        rhs = jnp.concatenate([kt_ref[jk], ct_ref[jc]], axis=0)
        return jnp.dot(qaug_ref[hh], rhs, preferred_element_type=F32)

    def lane_halves_max(s):
        return jnp.maximum(s[:, 0:LANES], s[:, LANES:2 * LANES])

    row_ge_col = (lax.broadcasted_iota(jnp.int32, (blk, blk), 0)
                  >= lax.broadcasted_iota(jnp.int32, (blk, blk), 1))
    for hh in heads:
        s_own = jnp.where(row_ge_col, scores(hh, i, i), MASK_VALUE)
        s_ref[hh, nb] = s_own
        mpart_ref[hh] = lane_halves_max(s_own)

    def past_block(g, u):
        j = g * MOBA_GROUP + u
        return j, jnp.minimum(j, nb - 1), jnp.where(j < i, j, nb)

    def pass1(g, carry):
        for hh in heads:
            part = mpart_ref[hh]
            for u in range(MOBA_GROUP):
                j, jk, jc = past_block(g, u)
                s = scores(hh, jk, jc)
                s_ref[hh, j] = s
                part = jnp.maximum(part, lane_halves_max(s))
            mpart_ref[hh] = part
        return carry

    lax.fori_loop(0, n_groups, pass1, 0)

    ones = jnp.ones((blk, LANES), BF16)
    for hh in heads:
        rowmax = jnp.max(mpart_ref[hh], axis=1, keepdims=True)
        mrow_ref[hh] = jnp.broadcast_to(rowmax, (blk, 2 * LANES))

    def weighted(hh, slot, jk):
        p = jnp.exp2(s_ref[hh, slot] - mrow_ref[hh]).astype(BF16)
        vaug = jnp.concatenate([v_ref[pl.ds(pl.multiple_of(jk * blk, blk), blk), :], ones], axis=1)
        return jnp.dot(p, vaug, preferred_element_type=F32)

    for hh in heads:
        acc_ref[hh] = weighted(hh, nb, i)

    def pass2(g, carry):
        for hh in heads:
            tot = acc_ref[hh]
            for u in range(MOBA_GROUP):
                j, jk, _ = past_block(g, u)
                tot = tot + weighted(hh, j, jk)
            acc_ref[hh] = tot
        return carry

    lax.fori_loop(0, n_groups, pass2, 0)

    out = jnp.zeros((blk, LANES), F32)
    for hh in heads:
        acc = acc_ref[hh]
        out = jnp.where(hmasks[hh], acc[:, 0:LANES] / acc[:, LANES:2 * LANES], out)
    o_ref[...] = out


def _moba(q, kt, v, kmean_t, key_table):
    b, s, a = q.shape
    nb = s // MOBA_BLOCK
    n_pairs = a // LANES
    heads = HEADS_PER_LANE_TILE
    return pl.pallas_call(
        _moba_kernel,
        grid=(b, n_pairs, nb),
        in_specs=[
            pl.BlockSpec((None, MOBA_BLOCK, LANES), lambda bi, p, i: (bi, i, p)),
            pl.BlockSpec((nb, LANES, MOBA_BLOCK), lambda bi, p, i: (bi, p, 0)),
            pl.BlockSpec((None, s, LANES), lambda bi, p, i: (bi, 0, p)),
            pl.BlockSpec((None, LANES, LANES), lambda bi, p, i: (bi, p, 0)),
            pl.BlockSpec((nb + 1, LANES, MOBA_BLOCK), lambda bi, p, i: (0, p, 0)),
        ],
        out_specs=pl.BlockSpec((None, MOBA_BLOCK, LANES), lambda bi, p, i: (bi, i, p)),
        out_shape=jax.ShapeDtypeStruct((b, s, a), F32),
        scratch_shapes=[
            pltpu.VMEM((heads, MOBA_BLOCK, 2 * LANES), BF16),
            pltpu.VMEM((heads, nb + 1, MOBA_BLOCK, MOBA_BLOCK), F32),
            pltpu.VMEM((heads, MOBA_BLOCK, LANES), F32),
            pltpu.VMEM((heads, MOBA_BLOCK, 2 * LANES), F32),
            pltpu.VMEM((heads, MOBA_BLOCK, 2 * LANES), F32),
        ],
        compiler_params=pltpu.CompilerParams(
            dimension_semantics=("arbitrary", "arbitrary", "arbitrary"), vmem_limit_bytes=VMEM_LIMIT),
        name="moba",
    )(q, kt, v, kmean_t, key_table)


def _mix_kernel(attn_ref, u_ref, vvn_ref, ga_ref, gs_ref, x_ref,
                wao_ref, wso_ref, wmo_ref, wsp_ref, bsp_ref, fg_ref, wr_ref, br_ref,
                h1_ref, xn2_ref, ri_ref, rw_ref, tcnt_ref, wcausal_ref):
    tm = x_ref.shape[0]
    ch = SGU_CHUNK

    @pl.when(pl.program_id(0) == 0)
    def _():
        tril = lax.broadcasted_iota(jnp.int32, (ch, ch), 0) >= lax.broadcasted_iota(jnp.int32, (ch, ch), 1)
        for g in range(SGU_GROUPS):
            wcausal_ref[g] = jnp.where(tril, wsp_ref[g], 0.0).astype(BF16)

    y_attn = jnp.dot(attn_ref[...].astype(BF16), wao_ref[...], preferred_element_type=F32)

    lane = lax.broadcasted_iota(jnp.int32, (1, LANES), 1)
    gdim = vvn_ref.shape[1] // SGU_GROUPS
    groups_per_tile = LANES // gdim
    w_causal = [wcausal_ref[g] for g in range(SGU_GROUPS)]
    rows = []
    for c in range(tm // ch):
        cols = []
        for ct in range(vvn_ref.shape[1] // LANES):
            vp = vvn_ref[c * ch:(c + 1) * ch, ct * LANES:(ct + 1) * LANES]
            acc = jnp.zeros((ch, LANES), F32)
            for gg in range(groups_per_tile):
                gmask = (lane >= gdim * gg) & (lane < gdim * (gg + 1))
                vm = jnp.where(gmask, vp, 0.0).astype(BF16)
                acc = acc + jnp.dot(w_causal[ct * groups_per_tile + gg], vm, preferred_element_type=F32)
            cols.append(acc)
        rows.append(jnp.concatenate(cols, axis=1) + bsp_ref[...])
    mixed = jnp.concatenate(rows, axis=0)
    sgu = u_ref[...] * mixed
    y_sgu = jnp.dot(sgu.astype(BF16), wso_ref[...], preferred_element_type=F32)

    merged = ga_ref[...] * y_attn + gs_ref[...] * y_sgu
    h1 = x_ref[...] + jnp.dot(merged.astype(BF16), wmo_ref[...], preferred_element_type=F32)
    h1_ref[...] = h1
    xn2 = h1 * lax.rsqrt(jnp.mean(h1 * h1, axis=-1, keepdims=True) + NORM_EPS) * fg_ref[...]
    _store_token_rows(xn2_ref, xn2)

    logits = jnp.dot(xn2, wr_ref[...], precision=lax.Precision.HIGHEST,
                     preferred_element_type=F32) + br_ref[...]
    work = jnp.where(lane < N_EXPERTS, logits, NEG_INF)
    lane_f = lane.astype(F32)
    vals, idxs = [], []
    for _ in range(EXPERT_TOPK):
        vmax = jnp.max(work, axis=1, keepdims=True)
        first = jnp.min(jnp.where(work == vmax, lane_f, float(LANES)), axis=1, keepdims=True)
        vals.append(vmax)
        idxs.append(first)
        work = jnp.where(lane_f == first, NEG_INF, work)
    exps = [jnp.exp(v - vals[0]) for v in vals]
    denom = exps[0]
    for e in exps[1:]:
        denom = denom + e
    chosen = jnp.zeros((tm, LANES), F32)
    for first in idxs:
        chosen = jnp.where(lane_f == first, 1.0, chosen)

    strict_lower = (lax.broadcasted_iota(jnp.int32, (tm, tm), 0)
                    > lax.broadcasted_iota(jnp.int32, (tm, tm), 1))
    before = jnp.dot(jnp.where(strict_lower, 1.0, 0.0).astype(BF16), chosen.astype(BF16),
                     preferred_element_type=F32)
    ri = jnp.zeros((tm, LANES), jnp.int32)
    rw = jnp.zeros((tm, LANES), F32)
    for kk in range(EXPERT_TOPK):
        rank = jnp.sum(jnp.where(lane_f == idxs[kk], before, 0.0), axis=1, keepdims=True)
        ri = jnp.where(lane == kk, idxs[kk].astype(jnp.int32), ri)
        ri = jnp.where(lane == EXPERT_TOPK + kk, rank.astype(jnp.int32), ri)
        rw = jnp.where(lane == kk, exps[kk] / denom, rw)
    ri_ref[...] = ri
    rw_ref[...] = rw
    tcnt_ref[...] = jnp.broadcast_to(jnp.sum(chosen, axis=0, keepdims=True), tcnt_ref.shape)


def _mix(attn, u, vvn, ga, gs, x2, wao, wso, wmo, wsp, bsp_full, fg, wr_pad, br_pad):
    t, d = x2.shape
    tm = PROJ_ROWS
    row = lambda w: pl.BlockSpec((tm, w), lambda i: (i, 0))
    const = lambda shape: pl.BlockSpec(shape, lambda i: (0,) * len(shape))
    ins = (attn, u, vvn, ga, gs, x2, wao, wso, wmo, wsp, bsp_full, fg, wr_pad, br_pad)
    in_specs = [row(a.shape[1]) for a in ins[:6]] + [const(a.shape) for a in ins[6:]]
    out_shape = (
        jax.ShapeDtypeStruct((t, d), F32),
        jax.ShapeDtypeStruct((t * SUBLANES, LANES), F32),
        jax.ShapeDtypeStruct((t, LANES), jnp.int32),
        jax.ShapeDtypeStruct((t, LANES), F32),
        jax.ShapeDtypeStruct((t // tm * SUBLANES, LANES), F32),
    )
    assert d == SUBLANES * LANES, "a token row is stored as one (8, 128) tile"
    assert tm == ROUTE_TOKENS
    token_rows = pl.BlockSpec((tm * SUBLANES, LANES), lambda i: (i, 0))
    out_specs = (row(d), token_rows, row(LANES), row(LANES), pl.BlockSpec((SUBLANES, LANES), lambda i: (i, 0)))
    return pl.pallas_call(
        _mix_kernel,
        grid=(t // tm,),
        in_specs=in_specs,
        out_specs=out_specs,
        out_shape=out_shape,
        scratch_shapes=[pltpu.VMEM(wsp.shape, BF16)],
        compiler_params=pltpu.CompilerParams(dimension_semantics=("arbitrary",), vmem_limit_bytes=VMEM_LIMIT),
        name="mix_route",
    )(*ins)


def _run_pieces(n, body):
    off = jnp.int32(0)
    for bit in reversed(range(ROUTE_TOKENS.bit_length())):
        size = 1 << bit
        take = lax.bitwise_and(n, size) != 0

        @pl.when(take)
        def _(off=off, size=size):
            body(off, size)

        off = off + jnp.where(take, size, 0)


def _token_rows(ref, r, n):
    return ref.at[pl.ds(pl.multiple_of(r * SUBLANES, SUBLANES), n * SUBLANES)]


def _dispatch_kernel(len_ref, tpos_ref, dst_ref, padlo_ref, padlen_ref, nused_ref, eid_ref, rank_ref, x_ref,
                     xs_hbm, sorted_ref, zeros_ref, sems, zsem):
    step = pl.program_id(0)
    n_steps = pl.num_programs(0)
    n_assign = eid_ref.shape[0]
    tile_rows = zeros_ref.shape[0] // SUBLANES
    slot = lax.rem(step, 2)

    def runs(tile, buf, wait):
        def per_expert(e, carry):
            k = tile * N_EXPERTS + e

            def piece(off, size):
                cp = pltpu.make_async_copy(_token_rows(sorted_ref.at[buf], tpos_ref[k] + off, size),
                                           _token_rows(xs_hbm, dst_ref[k] + off, size), sems.at[buf])
                if wait:
                    cp.wait()
                else:
                    cp.start()

            _run_pieces(len_ref[k], piece)
            return carry

        lax.fori_loop(0, N_EXPERTS, per_expert, 0)

    @pl.when(step == 0)
    def _():
        zeros_ref[...] = jnp.zeros_like(zeros_ref)

        def tail_tile(tl, carry):
            cp = pltpu.make_async_copy(zeros_ref, _token_rows(xs_hbm, tl * tile_rows, tile_rows), zsem)
            cp.start()
            cp.wait()
            return carry

        lax.fori_loop(nused_ref[0], xs_hbm.shape[0] // (tile_rows * SUBLANES), tail_tile, 0)

        def per_expert(e, carry):
            def piece(off, size):
                cp = pltpu.make_async_copy(_token_rows(zeros_ref, 0, size),
                                           _token_rows(xs_hbm, padlo_ref[e] + off, size), zsem)
                cp.start()
                cp.wait()

            _run_pieces(padlen_ref[e], piece)
            return carry

        lax.fori_loop(0, N_EXPERTS, per_expert, 0)

    @pl.when(step >= 2)
    def _():
        runs(step - 2, slot, wait=True)

    def place(a, carry):
        tok = lax.shift_right_logical(a, 2)
        pos = tpos_ref[step * N_EXPERTS + eid_ref[a]] + rank_ref[a]
        sorted_ref[slot, pl.ds(pl.multiple_of(pos * SUBLANES, SUBLANES), SUBLANES), :] = (
            x_ref[pl.ds(pl.multiple_of(tok * SUBLANES, SUBLANES), SUBLANES), :])
        return carry

    lax.fori_loop(0, n_assign, place, 0, unroll=8)
    runs(step, slot, wait=False)

    @pl.when(step == n_steps - 1)
    def _():
        @pl.when(step >= 1)
        def _():
            runs(step - 1, 1 - slot, wait=True)

        runs(step, slot, wait=True)


def _dispatch(run_len, run_tpos, run_dst, padlo, padlen, n_used, eid, rank, xn2_rows, n_rows):
    t = xn2_rows.shape[0] // SUBLANES
    n_assign = ROUTE_TOKENS * EXPERT_TOPK
    smem_blk = pl.BlockSpec((n_assign,), lambda c, *_: (c,), memory_space=pltpu.SMEM)
    grid_spec = pltpu.PrefetchScalarGridSpec(
        num_scalar_prefetch=6,
        grid=(t // ROUTE_TOKENS,),
        in_specs=[smem_blk, smem_blk, pl.BlockSpec((ROUTE_TOKENS * SUBLANES, LANES), lambda c, *_: (c, 0))],
        out_specs=pl.BlockSpec(memory_space=pl.ANY),
        scratch_shapes=[
            pltpu.VMEM((2, n_assign * SUBLANES, LANES), F32),
            pltpu.VMEM((EXPERT_ROWS * SUBLANES, LANES), F32),
            pltpu.SemaphoreType.DMA((2,)),
            pltpu.SemaphoreType.DMA(()),
        ],
    )
    return pl.pallas_call(
        _dispatch_kernel,
        grid_spec=grid_spec,
        out_shape=jax.ShapeDtypeStruct((n_rows * SUBLANES, LANES), F32),
        compiler_params=pltpu.CompilerParams(dimension_semantics=("arbitrary",), vmem_limit_bytes=VMEM_LIMIT),
        name="dispatch",
    )(run_len, run_tpos, run_dst, padlo, padlen, n_used, eid, rank, xn2_rows)


def _expert_kernel(be_ref, nused_ref, x_ref, wgu_ref, bg_ref, bu_ref, wd_ref, bd_ref, y_ref, wgu_bf, wd_bf):
    t = pl.program_id(0)
    rows = y_ref.shape[0] // SUBLANES
    de = wd_ref.shape[0]
    grp = 2 * LANES
    active = t < nused_ref[0]
    fresh = jnp.logical_and(
        active, jnp.logical_or(t == 0, be_ref[t] != be_ref[jnp.maximum(t - 1, 0)]))

    @pl.when(fresh)
    def _():
        src = lax.broadcasted_iota(jnp.int32, (grp, grp), 0)
        dst = lax.broadcasted_iota(jnp.int32, (grp, grp), 1)
        wanted = jnp.where(dst < LANES, 2 * dst, 2 * (dst - LANES) + 1)
        perm = jnp.where(src == wanted, 1.0, 0.0).astype(BF16)
        for g in range(wgu_ref.shape[1] // grp):
            cols = slice(g * grp, (g + 1) * grp)
            wgu_bf[:, cols] = jnp.dot(wgu_ref[:, cols].astype(BF16), perm,
                                      preferred_element_type=F32).astype(BF16)
        wd_bf[...] = wd_ref[...].astype(BF16)

    @pl.when(active)
    def _():
        xb = _load_token_rows(x_ref, rows).astype(BF16)
        gu = jnp.dot(xb, wgu_bf[...], preferred_element_type=F32)
        hid = []
        for g in range(de // LANES):
            cols = slice(g * LANES, (g + 1) * LANES)
            gate = gu[:, g * grp:g * grp + LANES] + bg_ref[:, cols]
            up = gu[:, g * grp + LANES:(g + 1) * grp] + bu_ref[:, cols]
            gate = jnp.minimum(gate, SWIGLU_LIMIT)
            up = jnp.clip(up, -SWIGLU_LIMIT, SWIGLU_LIMIT)
            hid.append((gate * _sigmoid(SWIGLU_ALPHA * gate) * (up + 1.0)).astype(BF16))
        y = jnp.dot(jnp.concatenate(hid, axis=1), wd_bf[...], preferred_element_type=F32) + bd_ref[...]
        _store_token_rows(y_ref, y)

    @pl.when(jnp.logical_not(active))
    def _():
        y_ref[...] = jnp.zeros_like(y_ref)


def _experts(block_expert, n_used, xs_rows, wgu, bg, bu, wd, bd):
    n_rows = xs_rows.shape[0] // SUBLANES
    _, d, de2 = wgu.shape
    de = wd.shape[1]
    n_tiles = n_rows // EXPERT_ROWS
    tile_rows = EXPERT_ROWS * SUBLANES
    per_expert = lambda t, be, nu: (be[t], 0, 0)
    grid_spec = pltpu.PrefetchScalarGridSpec(
        num_scalar_prefetch=2,
        grid=(n_tiles,),
        in_specs=[
            pl.BlockSpec((tile_rows, LANES), lambda t, be, nu: (jnp.minimum(t, nu[0] - 1), 0)),
            pl.BlockSpec((None, d, de2), per_expert),
            pl.BlockSpec((None, 1, de), per_expert),
            pl.BlockSpec((None, 1, de), per_expert),
            pl.BlockSpec((None, de, d), per_expert),
            pl.BlockSpec((None, 1, d), per_expert),
        ],
        out_specs=pl.BlockSpec((tile_rows, LANES), lambda t, be, nu: (t, 0)),
        scratch_shapes=[pltpu.VMEM((d, de2), BF16), pltpu.VMEM((de, d), BF16)],
    )
    return pl.pallas_call(
        _expert_kernel,
        grid_spec=grid_spec,
        out_shape=jax.ShapeDtypeStruct((n_rows * SUBLANES, LANES), F32),
        compiler_params=pltpu.CompilerParams(
            dimension_semantics=("arbitrary",), vmem_limit_bytes=EXPERT_VMEM_LIMIT),
        name="experts",
    )(block_expert, n_used, xs_rows, wgu, bg, bu, wd, bd)


def _combine_kernel(len_ref, tpos_ref, src_ref, eid_ref, rank_ref, w_ref, h1_ref, fg_ref, ys_hbm,
                    o_ref, buf_ref, h1rows_ref, outrows_ref, sems):
    step = pl.program_id(0)
    n_steps = pl.num_programs(0)
    tt = h1_ref.shape[0]
    slot = lax.rem(step, 2)

    def runs(tile, buf, wait):
        def per_expert(e, carry):
            k = tile * N_EXPERTS + e

            def piece(off, size):
                cp = pltpu.make_async_copy(_token_rows(ys_hbm, src_ref[k] + off, size),
                                           _token_rows(buf_ref.at[buf], tpos_ref[k] + off, size), sems.at[buf])
                if wait:
                    cp.wait()
                else:
                    cp.start()

            _run_pieces(len_ref[k], piece)
            return carry

        lax.fori_loop(0, N_EXPERTS, per_expert, 0)

    @pl.when(step == 0)
    def _():
        runs(0, 0, wait=False)

    @pl.when(step + 1 < n_steps)
    def _():
        runs(step + 1, 1 - slot, wait=False)

    runs(step, slot, wait=True)

    _store_token_rows(h1rows_ref, h1_ref[...])

    def one_token(tok, carry):
        acc = h1rows_ref[pl.ds(pl.multiple_of(tok * SUBLANES, SUBLANES), SUBLANES), :]
        for kk in range(EXPERT_TOPK):
            a = tok * EXPERT_TOPK + kk
            pos = tpos_ref[step * N_EXPERTS + eid_ref[a]] + rank_ref[a]
            acc = acc + w_ref[a] * buf_ref[slot, pl.ds(pl.multiple_of(pos * SUBLANES, SUBLANES), SUBLANES), :]
        outrows_ref[pl.ds(pl.multiple_of(tok * SUBLANES, SUBLANES), SUBLANES), :] = acc
        return carry

    lax.fori_loop(0, tt, one_token, 0, unroll=4)
    h = _load_token_rows(outrows_ref, tt)
    o_ref[...] = h * lax.rsqrt(jnp.mean(h * h, axis=-1, keepdims=True) + NORM_EPS) * fg_ref[...]


def _combine(run_len, run_tpos, run_src, eid, rank, w_flat, h1, fg, ys_rows):
    t, d = h1.shape
    tt = ROUTE_TOKENS
    n_assign = tt * EXPERT_TOPK
    smem_blk = pl.BlockSpec((n_assign,), lambda c, *_: (c,), memory_space=pltpu.SMEM)
    grid_spec = pltpu.PrefetchScalarGridSpec(
        num_scalar_prefetch=3,
        grid=(t // tt,),
        in_specs=[
            smem_blk, smem_blk, smem_blk,
            pl.BlockSpec((tt, d), lambda c, *_: (c, 0)),
            pl.BlockSpec((1, d), lambda c, *_: (0, 0)),
            pl.BlockSpec(memory_space=pl.ANY),
        ],
        out_specs=pl.BlockSpec((tt, d), lambda c, *_: (c, 0)),
        scratch_shapes=[
            pltpu.VMEM((2, n_assign * SUBLANES, LANES), F32),
            pltpu.VMEM((tt * SUBLANES, LANES), F32),
            pltpu.VMEM((tt * SUBLANES, LANES), F32),
            pltpu.SemaphoreType.DMA((2,)),
        ],
    )
    return pl.pallas_call(
        _combine_kernel,
        grid_spec=grid_spec,
        out_shape=jax.ShapeDtypeStruct((t, d), F32),
        compiler_params=pltpu.CompilerParams(dimension_semantics=("arbitrary",), vmem_limit_bytes=VMEM_LIMIT),
        name="combine",
    )(run_len, run_tpos, run_src, eid, rank, w_flat, h1, fg, ys_rows)


def _layer(h, mix_norm_g, w_in, w_attn_out, sgu_ln_g, sgu_ln_b, w_spatial, b_spatial, w_sgu_out,
           w_mix_out, ffn_norm_g, w_router, b_router, w_gate_up, b_gate_up, w_down, b_down, out_g):
    b, s, d = h.shape
    t = b * s
    sgu_width = sgu_ln_g.shape[0]
    x2 = h.reshape(t, d)

    q, kt, v, kmean, u, vvn, ga, gs = _in_proj(
        x2, mix_norm_g.reshape(1, d), w_in.astype(BF16), sgu_ln_g.reshape(1, sgu_width),
        sgu_ln_b.reshape(1, sgu_width), sgu_width)

    nb = s // MOBA_BLOCK
    assert nb <= MOBA_BIAS_LANE0, "block one-hot features must not overlap the ALiBi features"
    kmean_t = jnp.pad(kmean.reshape(b, nb, ATTN_WIDTH).transpose(0, 2, 1), ((0, 0), (0, 0), (0, LANES - nb)))
    attn = _moba(q.reshape(b, s, ATTN_WIDTH), kt, v.reshape(b, s, ATTN_WIDTH), kmean_t,
                 _moba_key_table(s)).reshape(t, ATTN_WIDTH)

    gdim = sgu_width // SGU_GROUPS
    bsp_full = jnp.repeat(b_spatial.T, gdim, axis=1)
    wr_pad = jnp.zeros((d, LANES), F32).at[:, :N_EXPERTS].set(w_router)
    br_pad = jnp.zeros((1, LANES), F32).at[0, :N_EXPERTS].set(b_router)
    h1, xn2_rows, ri, rw, tile_cnt = _mix(
        attn, u, vvn, ga, gs, x2, w_attn_out.astype(BF16), w_sgu_out.astype(BF16), w_mix_out.astype(BF16),
        w_spatial, bsp_full, ffn_norm_g.reshape(1, d), wr_pad, br_pad)

    n_assign = t * EXPERT_TOPK
    n_tiles = -(-(n_assign + N_EXPERTS * (EXPERT_ROWS - 1)) // EXPERT_ROWS)
    n_rows = n_tiles * EXPERT_ROWS
    run_len = tile_cnt[::SUBLANES, :N_EXPERTS].astype(jnp.int32)
    counts = jnp.sum(run_len, axis=0)
    padded = (counts + EXPERT_ROWS - 1) // EXPERT_ROWS * EXPERT_ROWS
    pad_end = jnp.cumsum(padded)
    starts = pad_end - padded
    run_tpos = jnp.cumsum(run_len, axis=1) - run_len
    run_row = starts[None, :] + jnp.cumsum(run_len, axis=0) - run_len
    tile_start = jnp.arange(n_tiles, dtype=jnp.int32) * EXPERT_ROWS
    block_expert = jnp.minimum(
        jnp.sum((pad_end[None, :] <= tile_start[:, None]).astype(jnp.int32), axis=1), N_EXPERTS - 1)
    n_used = (pad_end[-1:] // EXPERT_ROWS).astype(jnp.int32)
    eid = ri[:, :EXPERT_TOPK].reshape(n_assign)
    rank = ri[:, EXPERT_TOPK:2 * EXPERT_TOPK].reshape(n_assign)
    tables = (run_len.reshape(-1), run_tpos.reshape(-1), run_row.reshape(-1))

    xs_rows = _dispatch(*tables, starts + counts, padded - counts, n_used, eid, rank, xn2_rows, n_rows)

    de = w_down.shape[1]
    bg = b_gate_up[:, 0::2].reshape(N_EXPERTS, 1, de)
    bu = b_gate_up[:, 1::2].reshape(N_EXPERTS, 1, de)
    ys_rows = _experts(block_expert, n_used, xs_rows, w_gate_up, bg, bu, w_down, b_down.reshape(N_EXPERTS, 1, d))

    out = _combine(*tables, eid, rank, rw[:, :EXPERT_TOPK].reshape(n_assign), h1, out_g.reshape(1, d), ys_rows)
    return out.reshape(b, s, d)


def kernel(x, mix_norm_g, w_in, w_attn_out, sgu_ln_g, sgu_ln_b, w_spatial, b_spatial, w_sgu_out, w_mix_out,
           ffn_norm_g, w_router, b_router, w_gate_up, b_gate_up, w_down, b_down, final_norm_g):
    depth = w_in.shape[0]
    assert depth == 1, "the final RMSNorm is fused into the single layer's combine step"
    return _layer(x, mix_norm_g[0], w_in[0], w_attn_out[0], sgu_ln_g[0], sgu_ln_b[0], w_spatial[0],
                  b_spatial[0], w_sgu_out[0], w_mix_out[0], ffn_norm_g[0], w_router[0], b_router[0],
                  w_gate_up[0], b_gate_up[0], w_down[0], b_down[0], final_norm_g)
```

```python
import jax
import jax.numpy as jnp
import numpy as np
from jax import lax
from jax.experimental import pallas as pl
from jax.experimental.pallas import tpu as pltpu

F32 = jnp.float32
BF16 = jnp.bfloat16
NEG_INF = float("-inf")
MASK_VALUE = -1e30


def _bf16_pieces(x, n=3):
    pieces = []
    for _ in range(n):
        p = float(np.asarray(x, np.float32).astype(BF16).astype(np.float32))
        pieces.append(p)
        x = x - p
    return pieces


LOG2E = 1.4426950408889634
LOG2E_PIECES = _bf16_pieces(LOG2E)

N_HEADS = 8
HEAD_DIM = 64
ATTN_WIDTH = N_HEADS * HEAD_DIM
MOBA_BLOCK = 256
MOBA_TOPK = 3
SGU_CHUNK = 128
SGU_GROUPS = 8
N_EXPERTS = 32
EXPERT_TOPK = 4
SWIGLU_LIMIT = 7.0
SWIGLU_ALPHA = 1.702
NORM_EPS = 1e-5

LANES = 128
SUBLANES = 8
HEADS_PER_LANE_TILE = LANES // HEAD_DIM
MOBA_GROUP = 4
MOBA_BIAS_LANE0 = 32
MOBA_DUMMY_LANE = LANES - 1
PROJ_ROWS = 256
EXPERT_ROWS = 512
ROUTE_TOKENS = PROJ_ROWS
VMEM_LIMIT = 48 * 1024 * 1024
EXPERT_VMEM_LIMIT = 56 * 1024 * 1024


def _sigmoid(x):
    return 1.0 / (1.0 + jnp.exp(-x))


def _store_token_rows(ref, value, base=0):
    n = value.shape[0]
    for c in range(value.shape[1] // LANES):
        ref[pl.ds(base + c, n, stride=SUBLANES), :] = value[:, c * LANES:(c + 1) * LANES]


def _load_token_rows(ref, n, base=0):
    return jnp.concatenate([ref[pl.ds(base + c, n, stride=SUBLANES), :] for c in range(SUBLANES)], axis=1)


def _gelu_exact(x):
    return 0.5 * x * (1.0 + lax.erf(x * (0.5 ** 0.5)))


def _in_proj_kernel(x_ref, g_ref, w_ref, lng_ref, lnb_ref,
                    q_ref, kt_ref, v_ref, kmean_ref, u_ref, vvn_ref, ga_ref, gs_ref):
    x = x_ref[...]
    xn = x * lax.rsqrt(jnp.mean(x * x, axis=-1, keepdims=True) + NORM_EPS) * g_ref[...]
    xb = xn.astype(BF16)

    def proj(lo, hi):
        return jnp.dot(xb, w_ref[:, lo:hi], preferred_element_type=F32)

    a = ATTN_WIDTH
    sw = u_ref.shape[1]
    d = ga_ref.shape[1]
    q_ref[...] = proj(0, a)
    kf = proj(a, 2 * a)
    for j in range(kf.shape[0] // MOBA_BLOCK):
        kblk = kf[j * MOBA_BLOCK:(j + 1) * MOBA_BLOCK]
        kt_ref[j] = kblk.T.astype(BF16)
        kmean_ref[j] = jnp.mean(kblk, axis=0, keepdims=True)
    v_ref[...] = proj(2 * a, 3 * a).astype(BF16)
    z0 = 3 * a
    u_ref[...] = _gelu_exact(proj(z0, z0 + sw))
    zv = _gelu_exact(proj(z0 + sw, z0 + 2 * sw))
    mu = jnp.mean(zv, axis=-1, keepdims=True)
    zc = zv - mu
    var = jnp.mean(zc * zc, axis=-1, keepdims=True)
    vvn_ref[...] = zc * lax.rsqrt(var + NORM_EPS) * lng_ref[...] + lnb_ref[...]
    g0 = z0 + 2 * sw
    ga_ref[...] = _sigmoid(proj(g0, g0 + d))
    gs_ref[...] = _sigmoid(proj(g0 + d, g0 + 2 * d))


def _in_proj(x2, g, w_bf, lng, lnb, sgu_width):
    t, d = x2.shape
    tm = PROJ_ROWS
    n_in = w_bf.shape[1]
    row = lambda w: pl.BlockSpec((tm, w), lambda i: (i, 0))
    const = lambda shape: pl.BlockSpec(shape, lambda i: (0,) * len(shape))
    out_shape = (
        jax.ShapeDtypeStruct((t, ATTN_WIDTH), F32),
        jax.ShapeDtypeStruct((t // MOBA_BLOCK, ATTN_WIDTH, MOBA_BLOCK), BF16),
        jax.ShapeDtypeStruct((t, ATTN_WIDTH), BF16),
        jax.ShapeDtypeStruct((t // MOBA_BLOCK, 1, ATTN_WIDTH), F32),
        jax.ShapeDtypeStruct((t, sgu_width), F32),
        jax.ShapeDtypeStruct((t, sgu_width), F32),
        jax.ShapeDtypeStruct((t, d), F32),
        jax.ShapeDtypeStruct((t, d), F32),
    )
    out_specs = (
        row(ATTN_WIDTH),
        pl.BlockSpec((tm // MOBA_BLOCK, ATTN_WIDTH, MOBA_BLOCK), lambda i: (i, 0, 0)),
        row(ATTN_WIDTH),
        pl.BlockSpec((tm // MOBA_BLOCK, 1, ATTN_WIDTH), lambda i: (i, 0, 0)),
        row(sgu_width), row(sgu_width), row(d), row(d),
    )
    return pl.pallas_call(
        _in_proj_kernel,
        grid=(t // tm,),
        in_specs=[row(d), const((1, d)), const((d, n_in)), const((1, sgu_width)), const((1, sgu_width))],
        out_specs=out_specs,
        out_shape=out_shape,
        compiler_params=pltpu.CompilerParams(dimension_semantics=("arbitrary",), vmem_limit_bytes=VMEM_LIMIT),
        name="in_proj",
    )(x2, g, w_bf, lng, lnb)


def _alibi_slopes():
    slopes = 2.0 ** (-8.0 * np.arange(1, N_HEADS + 1, dtype=np.float64) / N_HEADS)
    assert all(np.log2(s) == np.round(np.log2(s)) for s in slopes), "ALiBi slopes must be powers of two"
    return slopes


def _moba_bias_lane(head_in_tile, part, piece):
    return MOBA_BIAS_LANE0 + (head_in_tile * 2 + part) * len(LOG2E_PIECES) + piece


def _moba_key_table(s):
    nb = s // MOBA_BLOCK
    slopes = _alibi_slopes()
    n_pairs = N_HEADS // HEADS_PER_LANE_TILE
    table = np.zeros((nb + 1, n_pairs, LANES, MOBA_BLOCK), np.float32)
    offs = np.arange(MOBA_BLOCK, dtype=np.float32)
    for j in range(nb):
        table[j, :, j, :] = 1.0
        for p in range(n_pairs):
            for hh in range(HEADS_PER_LANE_TILE):
                slope = slopes[p * HEADS_PER_LANE_TILE + hh]
                for piece in range(len(LOG2E_PIECES)):
                    table[j, p, _moba_bias_lane(hh, 0, piece), :] = slope * MOBA_BLOCK * j
                    table[j, p, _moba_bias_lane(hh, 1, piece), :] = slope * offs
    table[nb, :, MOBA_DUMMY_LANE, :] = 1.0
    as_bf16 = table.astype(BF16)
    assert np.array_equal(as_bf16.astype(np.float32), table), "bias table must be exact in bf16"
    return jnp.asarray(as_bf16.reshape(nb + 1, n_pairs * LANES, MOBA_BLOCK))


def _moba_kernel(q_ref, kt_ref, v_ref, kmt_ref, ct_ref, o_ref, qaug_ref, s_ref, mpart_ref, mrow_ref, acc_ref):
    i = pl.program_id(2)
    nb = kt_ref.shape[0]
    blk = MOBA_BLOCK
    q = q_ref[...]
    lane = lax.broadcasted_iota(jnp.int32, (1, LANES), 1)
    lane_f = lane.astype(F32)
    n_groups = lax.div(i + (MOBA_GROUP - 1), MOBA_GROUP)
    heads = range(HEADS_PER_LANE_TILE)
    hmasks = [(lane >= HEAD_DIM * hh) & (lane < HEAD_DIM * (hh + 1)) for hh in heads]

    for hh in heads:
        qh = jnp.where(hmasks[hh], q, 0.0)
        gate = jnp.dot(qh, kmt_ref[...], precision=lax.Precision.HIGHEST, preferred_element_type=F32)
        gate = jnp.where(lane < i, gate, NEG_INF)
        blockmask = jnp.where(lane == i, 0.0, MASK_VALUE)
        for kk in range(MOBA_TOPK):
            gmax = jnp.max(gate, axis=1, keepdims=True)
            first = jnp.min(jnp.where(gate == gmax, lane_f, float(LANES)), axis=1, keepdims=True)
            valid = (i > kk).astype(F32)
            first = first * valid + (valid - 1.0)
            hit = lane_f == first
            blockmask = jnp.where(hit, 0.0, blockmask)
            gate = jnp.where(hit, NEG_INF, gate)
        feats = jnp.where(lane == MOBA_DUMMY_LANE, MASK_VALUE, 0.0)
        for part in range(2):
            for piece, value in enumerate(LOG2E_PIECES):
                feats = jnp.where(lane == _moba_bias_lane(hh, part, piece), value, feats)
        extra = jnp.where(lane < nb, blockmask, feats)
        qaug_ref[hh, :, 0:LANES] = (qh * (HEAD_DIM ** -0.5 * LOG2E)).astype(BF16)
        qaug_ref[hh, :, LANES:2 * LANES] = extra.astype(BF16)

    def scores(hh, jk, jc):
        rhs = jnp.concatenate([kt_ref[jk], ct_ref[jc]], axis=0)
        return jnp.dot(qaug_ref[hh], rhs, preferred_element_type=F32)

    def lane_halves_max(s):
        return jnp.maximum(s[:, 0:LANES], s[:, LANES:2 * LANES])

    row_ge_col = (lax.broadcasted_iota(jnp.int32, (blk, blk), 0)
                  >= lax.broadcasted_iota(jnp.int32, (blk, blk), 1))
    for hh in heads:
        s_own = jnp.where(row_ge_col, scores(hh, i, i), MASK_VALUE)
        s_ref[hh, nb] = s_own
        mpart_ref[hh] = lane_halves_max(s_own)

    def past_block(g, u):
        j = g * MOBA_GROUP + u
        return j, jnp.minimum(j, nb - 1), jnp.where(j < i, j, nb)

    def pass1(g, carry):
        for hh in heads:
            part = mpart_ref[hh]
            for u in range(MOBA_GROUP):
                j, jk, jc = past_block(g, u)
                s = scores(hh, jk, jc)
                s_ref[hh, j] = s
                part = jnp.maximum(part, lane_halves_max(s))
            mpart_ref[hh] = part
        return carry

    lax.fori_loop(0, n_groups, pass1, 0)

    ones = jnp.ones((blk, LANES), BF16)
    for hh in heads:
        rowmax = jnp.max(mpart_ref[hh], axis=1, keepdims=True)
        mrow_ref[hh] = jnp.broadcast_to(rowmax, (blk, 2 * LANES))

    def weighted(hh, slot, jk):
        p = jnp.exp2(s_ref[hh, slot] - mrow_ref[hh]).astype(BF16)
        vaug = jnp.concatenate([v_ref[pl.ds(pl.multiple_of(jk * blk, blk), blk), :], ones], axis=1)
        return jnp.dot(p, vaug, preferred_element_type=F32)

    for hh in heads:
        acc_ref[hh] = weighted(hh, nb, i)

    def pass2(g, carry):
        for hh in heads:
            tot = acc_ref[hh]
            for u in range(MOBA_GROUP):
                j, jk, _ = past_block(g, u)
                tot = tot + weighted(hh, j, jk)
            acc_ref[hh] = tot
        return carry

    lax.fori_loop(0, n_groups, pass2, 0)

    out = jnp.zeros((blk, LANES), F32)
    for hh in heads:
        acc = acc_ref[hh]
        out = jnp.where(hmasks[hh], acc[:, 0:LANES] / acc[:, LANES:2 * LANES], out)
    o_ref[...] = out


def _moba(q, kt, v, kmean_t, key_table):
    b, s, a = q.shape
    nb = s // MOBA_BLOCK
    n_pairs = a // LANES
    heads = HEADS_PER_LANE_TILE
    return pl.pallas_call(
        _moba_kernel,
        grid=(b, n_pairs, nb),
        in_specs=[
            pl.BlockSpec((None, MOBA_BLOCK, LANES), lambda bi, p, i: (bi, i, p)),
            pl.BlockSpec((nb, LANES, MOBA_BLOCK), lambda bi, p, i: (bi, p, 0)),
            pl.BlockSpec((None, s, LANES), lambda bi, p, i: (bi, 0, p)),
            pl.BlockSpec((None, LANES, LANES), lambda bi, p, i: (bi, p, 0)),
            pl.BlockSpec((nb + 1, LANES, MOBA_BLOCK), lambda bi, p, i: (0, p, 0)),
        ],
        out_specs=pl.BlockSpec((None, MOBA_BLOCK, LANES), lambda bi, p, i: (bi, i, p)),
        out_shape=jax.ShapeDtypeStruct((b, s, a), F32),
        scratch_shapes=[
            pltpu.VMEM((heads, MOBA_BLOCK, 2 * LANES), BF16),
            pltpu.VMEM((heads, nb + 1, MOBA_BLOCK, MOBA_BLOCK), F32),
            pltpu.VMEM((heads, MOBA_BLOCK, LANES), F32),
            pltpu.VMEM((heads, MOBA_BLOCK, 2 * LANES), F32),
            pltpu.VMEM((heads, MOBA_BLOCK, 2 * LANES), F32),
        ],
        compiler_params=pltpu.CompilerParams(
            dimension_semantics=("arbitrary", "arbitrary", "arbitrary"), vmem_limit_bytes=VMEM_LIMIT),
        name="moba",
    )(q, kt, v, kmean_t, key_table)


def _mix_kernel(attn_ref, u_ref, vvn_ref, ga_ref, gs_ref, x_ref,
                wao_ref, wso_ref, wmo_ref, wsp_ref, bsp_ref, fg_ref, wr_ref, br_ref,
                h1_ref, xn2_ref, ri_ref, rw_ref, tcnt_ref, wcausal_ref):
    tm = x_ref.shape[0]
    ch = SGU_CHUNK

    @pl.when(pl.program_id(0) == 0)
    def _():
        tril = lax.broadcasted_iota(jnp.int32, (ch, ch), 0) >= lax.broadcasted_iota(jnp.int32, (ch, ch), 1)
        for g in range(SGU_GROUPS):
            wcausal_ref[g] = jnp.where(tril, wsp_ref[g], 0.0).astype(BF16)

    y_attn = jnp.dot(attn_ref[...].astype(BF16), wao_ref[...], preferred_element_type=F32)

    lane = lax.broadcasted_iota(jnp.int32, (1, LANES), 1)
    gdim = vvn_ref.shape[1] // SGU_GROUPS
    groups_per_tile = LANES // gdim
    w_causal = [wcausal_ref[g] for g in range(SGU_GROUPS)]
    rows = []
    for c in range(tm // ch):
        cols = []
        for ct in range(vvn_ref.shape[1] // LANES):
            vp = vvn_ref[c * ch:(c + 1) * ch, ct * LANES:(ct + 1) * LANES]
            acc = jnp.zeros((ch, LANES), F32)
            for gg in range(groups_per_tile):
                gmask = (lane >= gdim * gg) & (lane < gdim * (gg + 1))
                vm = jnp.where(gmask, vp, 0.0).astype(BF16)
                acc = acc + jnp.dot(w_causal[ct * groups_per_tile + gg], vm, preferred_element_type=F32)
            cols.append(acc)
        rows.append(jnp.concatenate(cols, axis=1) + bsp_ref[...])
    mixed = jnp.concatenate(rows, axis=0)
    sgu = u_ref[...] * mixed
    y_sgu = jnp.dot(sgu.astype(BF16), wso_ref[...], preferred_element_type=F32)

    merged = ga_ref[...] * y_attn + gs_ref[...] * y_sgu
    h1 = x_ref[...] + jnp.dot(merged.astype(BF16), wmo_ref[...], preferred_element_type=F32)
    h1_ref[...] = h1
    xn2 = h1 * lax.rsqrt(jnp.mean(h1 * h1, axis=-1, keepdims=True) + NORM_EPS) * fg_ref[...]
    _store_token_rows(xn2_ref, xn2)

    logits = jnp.dot(xn2, wr_ref[...], precision=lax.Precision.HIGHEST,
                     preferred_element_type=F32) + br_ref[...]
    work = jnp.where(lane < N_EXPERTS, logits, NEG_INF)
    lane_f = lane.astype(F32)
    vals, idxs = [], []
    for _ in range(EXPERT_TOPK):
        vmax = jnp.max(work, axis=1, keepdims=True)
        first = jnp.min(jnp.where(work == vmax, lane_f, float(LANES)), axis=1, keepdims=True)
        vals.append(vmax)
        idxs.append(first)
        work = jnp.where(lane_f == first, NEG_INF, work)
    exps = [jnp.exp(v - vals[0]) for v in vals]
    denom = exps[0]
    for e in exps[1:]:
        denom = denom + e
    chosen = jnp.zeros((tm, LANES), F32)
    for first in idxs:
        chosen = jnp.where(lane_f == first, 1.0, chosen)

    strict_lower = (lax.broadcasted_iota(jnp.int32, (tm, tm), 0)
                    > lax.broadcasted_iota(jnp.int32, (tm, tm), 1))
    before = jnp.dot(jnp.where(strict_lower, 1.0, 0.0).astype(BF16), chosen.astype(BF16),
                     preferred_element_type=F32)
    counts = jnp.broadcast_to(jnp.sum(chosen, axis=0, keepdims=True), tcnt_ref.shape)
    lower_expert = (lax.broadcasted_iota(jnp.int32, (LANES, LANES), 0)
                    < lax.broadcasted_iota(jnp.int32, (LANES, LANES), 1))
    expert_start = jnp.dot(counts.astype(BF16), jnp.where(lower_expert, 1.0, 0.0).astype(BF16),
                           preferred_element_type=F32)[0:1, :]
    sorted_pos = before + expert_start
    ri = jnp.zeros((tm, LANES), jnp.int32)
    rw = jnp.zeros((tm, LANES), F32)
    for kk in range(EXPERT_TOPK):
        pos = jnp.sum(jnp.where(lane_f == idxs[kk], sorted_pos, 0.0), axis=1, keepdims=True)
        ri = jnp.where(lane == kk, idxs[kk].astype(jnp.int32), ri)
        ri = jnp.where(lane == EXPERT_TOPK + kk, pos.astype(jnp.int32), ri)
        rw = jnp.where(lane == kk, exps[kk] / denom, rw)
    ri_ref[...] = ri
    rw_ref[...] = rw
    tcnt_ref[...] = counts


def _mix(attn, u, vvn, ga, gs, x2, wao, wso, wmo, wsp, bsp_full, fg, wr_pad, br_pad):
    t, d = x2.shape
    tm = PROJ_ROWS
    row = lambda w: pl.BlockSpec((tm, w), lambda i: (i, 0))
    const = lambda shape: pl.BlockSpec(shape, lambda i: (0,) * len(shape))
    ins = (attn, u, vvn, ga, gs, x2, wao, wso, wmo, wsp, bsp_full, fg, wr_pad, br_pad)
    in_specs = [row(a.shape[1]) for a in ins[:6]] + [const(a.shape) for a in ins[6:]]
    out_shape = (
        jax.ShapeDtypeStruct((t, d), F32),
        jax.ShapeDtypeStruct((t * SUBLANES, LANES), F32),
        jax.ShapeDtypeStruct((t, LANES), jnp.int32),
        jax.ShapeDtypeStruct((t, LANES), F32),
        jax.ShapeDtypeStruct((t // tm * SUBLANES, LANES), F32),
    )
    assert d == SUBLANES * LANES, "a token row is stored as one (8, 128) tile"
    assert tm == ROUTE_TOKENS
    token_rows = pl.BlockSpec((tm * SUBLANES, LANES), lambda i: (i, 0))
    out_specs = (row(d), token_rows, row(LANES), row(LANES), pl.BlockSpec((SUBLANES, LANES), lambda i: (i, 0)))
    return pl.pallas_call(
        _mix_kernel,
        grid=(t // tm,),
        in_specs=in_specs,
        out_specs=out_specs,
        out_shape=out_shape,
        scratch_shapes=[pltpu.VMEM(wsp.shape, BF16)],
        compiler_params=pltpu.CompilerParams(dimension_semantics=("arbitrary",), vmem_limit_bytes=VMEM_LIMIT),
        name="mix_route",
    )(*ins)


def _run_pieces(n, body):
    off = jnp.int32(0)
    for bit in reversed(range(ROUTE_TOKENS.bit_length())):
        size = 1 << bit
        take = lax.bitwise_and(n, size) != 0

        @pl.when(take)
        def _(off=off, size=size):
            body(off, size)

        off = off + jnp.where(take, size, 0)


def _token_rows(ref, r, n):
    return ref.at[pl.ds(pl.multiple_of(r * SUBLANES, SUBLANES), n * SUBLANES)]


def _dispatch_kernel(len_ref, tpos_ref, dst_ref, padlo_ref, padlen_ref, nused_ref, pos_ref, x_ref,
                     xs_hbm, sorted_ref, zeros_ref, sems, zsem):
    step = pl.program_id(0)
    n_steps = pl.num_programs(0)
    n_assign = pos_ref.shape[0]
    tile_rows = zeros_ref.shape[0] // SUBLANES
    slot = lax.rem(step, 2)

    def runs(tile, buf, wait):
        def per_expert(e, carry):
            k = tile * N_EXPERTS + e

            def piece(off, size):
                cp = pltpu.make_async_copy(_token_rows(sorted_ref.at[buf], tpos_ref[k] + off, size),
                                           _token_rows(xs_hbm, dst_ref[k] + off, size), sems.at[buf])
                if wait:
                    cp.wait()
                else:
                    cp.start()

            _run_pieces(len_ref[k], piece)
            return carry

        lax.fori_loop(0, N_EXPERTS, per_expert, 0)

    @pl.when(step == 0)
    def _():
        zeros_ref[...] = jnp.zeros_like(zeros_ref)

        def tail_tile(tl, carry):
            cp = pltpu.make_async_copy(zeros_ref, _token_rows(xs_hbm, tl * tile_rows, tile_rows), zsem)
            cp.start()
            cp.wait()
            return carry

        lax.fori_loop(nused_ref[0], xs_hbm.shape[0] // (tile_rows * SUBLANES), tail_tile, 0)

        def per_expert(e, carry):
            def piece(off, size):
                cp = pltpu.make_async_copy(_token_rows(zeros_ref, 0, size),
                                           _token_rows(xs_hbm, padlo_ref[e] + off, size), zsem)
                cp.start()
                cp.wait()

            _run_pieces(padlen_ref[e], piece)
            return carry

        lax.fori_loop(0, N_EXPERTS, per_expert, 0)

    @pl.when(step >= 2)
    def _():
        runs(step - 2, slot, wait=True)

    def place(a, carry):
        tok = lax.shift_right_logical(a, 2)
        pos = pos_ref[a]
        sorted_ref[slot, pl.ds(pl.multiple_of(pos * SUBLANES, SUBLANES), SUBLANES), :] = (
            x_ref[pl.ds(pl.multiple_of(tok * SUBLANES, SUBLANES), SUBLANES), :])
        return carry

    lax.fori_loop(0, n_assign, place, 0, unroll=8)
    runs(step, slot, wait=False)

    @pl.when(step == n_steps - 1)
    def _():
        @pl.when(step >= 1)
        def _():
            runs(step - 1, 1 - slot, wait=True)

        runs(step, slot, wait=True)


def _dispatch(run_len, run_tpos, run_dst, padlo, padlen, n_used, pos, xn2_rows, n_rows):
    t = xn2_rows.shape[0] // SUBLANES
    n_assign = ROUTE_TOKENS * EXPERT_TOPK
    smem_blk = pl.BlockSpec((n_assign,), lambda c, *_: (c,), memory_space=pltpu.SMEM)
    grid_spec = pltpu.PrefetchScalarGridSpec(
        num_scalar_prefetch=6,
        grid=(t // ROUTE_TOKENS,),
        in_specs=[smem_blk, pl.BlockSpec((ROUTE_TOKENS * SUBLANES, LANES), lambda c, *_: (c, 0))],
        out_specs=pl.BlockSpec(memory_space=pl.ANY),
        scratch_shapes=[
            pltpu.VMEM((2, n_assign * SUBLANES, LANES), F32),
            pltpu.VMEM((EXPERT_ROWS * SUBLANES, LANES), F32),
            pltpu.SemaphoreType.DMA((2,)),
            pltpu.SemaphoreType.DMA(()),
        ],
    )
    return pl.pallas_call(
        _dispatch_kernel,
        grid_spec=grid_spec,
        out_shape=jax.ShapeDtypeStruct((n_rows * SUBLANES, LANES), F32),
        compiler_params=pltpu.CompilerParams(dimension_semantics=("arbitrary",), vmem_limit_bytes=VMEM_LIMIT),
        name="dispatch",
    )(run_len, run_tpos, run_dst, padlo, padlen, n_used, pos, xn2_rows)


def _expert_kernel(be_ref, nused_ref, x_ref, wgu_ref, bg_ref, bu_ref, wd_ref, bd_ref, y_ref, wgu_bf, wd_bf):
    t = pl.program_id(0)
    rows = y_ref.shape[0] // SUBLANES
    de = wd_ref.shape[0]
    grp = 2 * LANES
    active = t < nused_ref[0]
    fresh = jnp.logical_and(
        active, jnp.logical_or(t == 0, be_ref[t] != be_ref[jnp.maximum(t - 1, 0)]))

    @pl.when(fresh)
    def _():
        src = lax.broadcasted_iota(jnp.int32, (grp, grp), 0)
        dst = lax.broadcasted_iota(jnp.int32, (grp, grp), 1)
        wanted = jnp.where(dst < LANES, 2 * dst, 2 * (dst - LANES) + 1)
        perm = jnp.where(src == wanted, 1.0, 0.0).astype(BF16)
        for g in range(wgu_ref.shape[1] // grp):
            cols = slice(g * grp, (g + 1) * grp)
            wgu_bf[:, cols] = jnp.dot(wgu_ref[:, cols].astype(BF16), perm,
                                      preferred_element_type=F32).astype(BF16)
        wd_bf[...] = wd_ref[...].astype(BF16)

    @pl.when(active)
    def _():
        xb = _load_token_rows(x_ref, rows).astype(BF16)
        gu = jnp.dot(xb, wgu_bf[...], preferred_element_type=F32)
        hid = []
        for g in range(de // LANES):
            cols = slice(g * LANES, (g + 1) * LANES)
            gate = gu[:, g * grp:g * grp + LANES] + bg_ref[:, cols]
            up = gu[:, g * grp + LANES:(g + 1) * grp] + bu_ref[:, cols]
            gate = jnp.minimum(gate, SWIGLU_LIMIT)
            up = jnp.clip(up, -SWIGLU_LIMIT, SWIGLU_LIMIT)
            hid.append((gate * _sigmoid(SWIGLU_ALPHA * gate) * (up + 1.0)).astype(BF16))
        y = jnp.dot(jnp.concatenate(hid, axis=1), wd_bf[...], preferred_element_type=F32) + bd_ref[...]
        _store_token_rows(y_ref, y)

    @pl.when(jnp.logical_not(active))
    def _():
        y_ref[...] = jnp.zeros_like(y_ref)


def _experts(block_expert, n_used, xs_rows, wgu, bg, bu, wd, bd):
    n_rows = xs_rows.shape[0] // SUBLANES
    _, d, de2 = wgu.shape
    de = wd.shape[1]
    n_tiles = n_rows // EXPERT_ROWS
    tile_rows = EXPERT_ROWS * SUBLANES
    per_expert = lambda t, be, nu: (be[t], 0, 0)
    grid_spec = pltpu.PrefetchScalarGridSpec(
        num_scalar_prefetch=2,
        grid=(n_tiles,),
        in_specs=[
            pl.BlockSpec((tile_rows, LANES), lambda t, be, nu: (jnp.minimum(t, nu[0] - 1), 0)),
            pl.BlockSpec((None, d, de2), per_expert),
            pl.BlockSpec((None, 1, de), per_expert),
            pl.BlockSpec((None, 1, de), per_expert),
            pl.BlockSpec((None, de, d), per_expert),
            pl.BlockSpec((None, 1, d), per_expert),
        ],
        out_specs=pl.BlockSpec((tile_rows, LANES), lambda t, be, nu: (t, 0)),
        scratch_shapes=[pltpu.VMEM((d, de2), BF16), pltpu.VMEM((de, d), BF16)],
    )
    return pl.pallas_call(
        _expert_kernel,
        grid_spec=grid_spec,
        out_shape=jax.ShapeDtypeStruct((n_rows * SUBLANES, LANES), F32),
        compiler_params=pltpu.CompilerParams(
            dimension_semantics=("arbitrary",), vmem_limit_bytes=EXPERT_VMEM_LIMIT),
        name="experts",
    )(block_expert, n_used, xs_rows, wgu, bg, bu, wd, bd)


def _combine_kernel(len_ref, tpos_ref, src_ref, pos_ref, w_ref, h1_ref, fg_ref, ys_hbm,
                    o_ref, buf_ref, h1rows_ref, outrows_ref, sems):
    step = pl.program_id(0)
    n_steps = pl.num_programs(0)
    tt = h1_ref.shape[0]
    slot = lax.rem(step, 2)

    def runs(tile, buf, wait):
        def per_expert(e, carry):
            k = tile * N_EXPERTS + e

            def piece(off, size):
                cp = pltpu.make_async_copy(_token_rows(ys_hbm, src_ref[k] + off, size),
                                           _token_rows(buf_ref.at[buf], tpos_ref[k] + off, size), sems.at[buf])
                if wait:
                    cp.wait()
                else:
                    cp.start()

            _run_pieces(len_ref[k], piece)
            return carry

        lax.fori_loop(0, N_EXPERTS, per_expert, 0)

    @pl.when(step == 0)
    def _():
        runs(0, 0, wait=False)

    @pl.when(step + 1 < n_steps)
    def _():
        runs(step + 1, 1 - slot, wait=False)

    runs(step, slot, wait=True)

    _store_token_rows(h1rows_ref, h1_ref[...])

    def one_token(tok, carry):
        acc = h1rows_ref[pl.ds(pl.multiple_of(tok * SUBLANES, SUBLANES), SUBLANES), :]
        for kk in range(EXPERT_TOPK):
            a = tok * EXPERT_TOPK + kk
            pos = pos_ref[a]
            acc = acc + w_ref[a] * buf_ref[slot, pl.ds(pl.multiple_of(pos * SUBLANES, SUBLANES), SUBLANES), :]
        outrows_ref[pl.ds(pl.multiple_of(tok * SUBLANES, SUBLANES), SUBLANES), :] = acc
        return carry

    lax.fori_loop(0, tt, one_token, 0, unroll=4)
    h = _load_token_rows(outrows_ref, tt)
    o_ref[...] = h * lax.rsqrt(jnp.mean(h * h, axis=-1, keepdims=True) + NORM_EPS) * fg_ref[...]


def _combine(run_len, run_tpos, run_src, pos, w_flat, h1, fg, ys_rows):
    t, d = h1.shape
    tt = ROUTE_TOKENS
    n_assign = tt * EXPERT_TOPK
    smem_blk = pl.BlockSpec((n_assign,), lambda c, *_: (c,), memory_space=pltpu.SMEM)
    grid_spec = pltpu.PrefetchScalarGridSpec(
        num_scalar_prefetch=3,
        grid=(t // tt,),
        in_specs=[
            smem_blk, smem_blk,
            pl.BlockSpec((tt, d), lambda c, *_: (c, 0)),
            pl.BlockSpec((1, d), lambda c, *_: (0, 0)),
            pl.BlockSpec(memory_space=pl.ANY),
        ],
        out_specs=pl.BlockSpec((tt, d), lambda c, *_: (c, 0)),
        scratch_shapes=[
            pltpu.VMEM((2, n_assign * SUBLANES, LANES), F32),
            pltpu.VMEM((tt * SUBLANES, LANES), F32),
            pltpu.VMEM((tt * SUBLANES, LANES), F32),
            pltpu.SemaphoreType.DMA((2,)),
        ],
    )
    return pl.pallas_call(
        _combine_kernel,
        grid_spec=grid_spec,
        out_shape=jax.ShapeDtypeStruct((t, d), F32),
        compiler_params=pltpu.CompilerParams(dimension_semantics=("arbitrary",), vmem_limit_bytes=VMEM_LIMIT),
        name="combine",
    )(run_len, run_tpos, run_src, pos, w_flat, h1, fg, ys_rows)


def _layer(h, mix_norm_g, w_in, w_attn_out, sgu_ln_g, sgu_ln_b, w_spatial, b_spatial, w_sgu_out,
           w_mix_out, ffn_norm_g, w_router, b_router, w_gate_up, b_gate_up, w_down, b_down, out_g):
    b, s, d = h.shape
    t = b * s
    sgu_width = sgu_ln_g.shape[0]
    x2 = h.reshape(t, d)

    q, kt, v, kmean, u, vvn, ga, gs = _in_proj(
        x2, mix_norm_g.reshape(1, d), w_in.astype(BF16), sgu_ln_g.reshape(1, sgu_width),
        sgu_ln_b.reshape(1, sgu_width), sgu_width)

    nb = s // MOBA_BLOCK
    assert nb <= MOBA_BIAS_LANE0, "block one-hot features must not overlap the ALiBi features"
    kmean_t = jnp.pad(kmean.reshape(b, nb, ATTN_WIDTH).transpose(0, 2, 1), ((0, 0), (0, 0), (0, LANES - nb)))
    attn = _moba(q.reshape(b, s, ATTN_WIDTH), kt, v.reshape(b, s, ATTN_WIDTH), kmean_t,
                 _moba_key_table(s)).reshape(t, ATTN_WIDTH)

    gdim = sgu_width // SGU_GROUPS
    bsp_full = jnp.repeat(b_spatial.T, gdim, axis=1)
    wr_pad = jnp.zeros((d, LANES), F32).at[:, :N_EXPERTS].set(w_router)
    br_pad = jnp.zeros((1, LANES), F32).at[0, :N_EXPERTS].set(b_router)
    h1, xn2_rows, ri, rw, tile_cnt = _mix(
        attn, u, vvn, ga, gs, x2, w_attn_out.astype(BF16), w_sgu_out.astype(BF16), w_mix_out.astype(BF16),
        w_spatial, bsp_full, ffn_norm_g.reshape(1, d), wr_pad, br_pad)

    n_assign = t * EXPERT_TOPK
    n_tiles = -(-(n_assign + N_EXPERTS * (EXPERT_ROWS - 1)) // EXPERT_ROWS)
    n_rows = n_tiles * EXPERT_ROWS
    run_len = tile_cnt[::SUBLANES, :N_EXPERTS].astype(jnp.int32)
    counts = jnp.sum(run_len, axis=0)
    padded = (counts + EXPERT_ROWS - 1) // EXPERT_ROWS * EXPERT_ROWS
    pad_end = jnp.cumsum(padded)
    starts = pad_end - padded
    run_tpos = jnp.cumsum(run_len, axis=1) - run_len
    run_row = starts[None, :] + jnp.cumsum(run_len, axis=0) - run_len
    tile_start = jnp.arange(n_tiles, dtype=jnp.int32) * EXPERT_ROWS
    block_expert = jnp.minimum(
        jnp.sum((pad_end[None, :] <= tile_start[:, None]).astype(jnp.int32), axis=1), N_EXPERTS - 1)
    n_used = (pad_end[-1:] // EXPERT_ROWS).astype(jnp.int32)
    pos = ri[:, EXPERT_TOPK:2 * EXPERT_TOPK].reshape(n_assign)
    tables = (run_len.reshape(-1), run_tpos.reshape(-1), run_row.reshape(-1))

    xs_rows = _dispatch(*tables, starts + counts, padded - counts, n_used, pos, xn2_rows, n_rows)

    de = w_down.shape[1]
    bg = b_gate_up[:, 0::2].reshape(N_EXPERTS, 1, de)
    bu = b_gate_up[:, 1::2].reshape(N_EXPERTS, 1, de)
    ys_rows = _experts(block_expert, n_used, xs_rows, w_gate_up, bg, bu, w_down, b_down.reshape(N_EXPERTS, 1, d))

    out = _combine(*tables, pos, rw[:, :EXPERT_TOPK].reshape(n_assign), h1, out_g.reshape(1, d), ys_rows)
    return out.reshape(b, s, d)


def kernel(x, mix_norm_g, w_in, w_attn_out, sgu_ln_g, sgu_ln_b, w_spatial, b_spatial, w_sgu_out, w_mix_out,
           ffn_norm_g, w_router, b_router, w_gate_up, b_gate_up, w_down, b_down, final_norm_g):
    depth = w_in.shape[0]
    assert depth == 1, "the final RMSNorm is fused into the single layer's combine step"
    return _layer(x, mix_norm_g[0], w_in[0], w_attn_out[0], sgu_ln_g[0], sgu_ln_b[0], w_spatial[0],
                  b_spatial[0], w_sgu_out[0], w_mix_out[0], ffn_norm_g[0], w_router[0], b_router[0],
                  w_gate_up[0], b_gate_up[0], w_down[0], b_down[0], final_norm_g)
```

```python
import jax
import jax.numpy as jnp
import numpy as np
from jax import lax
from jax.experimental import pallas as pl
from jax.experimental.pallas import tpu as pltpu

F32 = jnp.float32
BF16 = jnp.bfloat16
NEG_INF = float("-inf")
MASK_VALUE = -1e30


def _bf16_pieces(x, n=3):
    pieces = []
    for _ in range(n):
        p = float(np.asarray(x, np.float32).astype(BF16).astype(np.float32))
        pieces.append(p)
        x = x - p
    return pieces


LOG2E = 1.4426950408889634
LOG2E_PIECES = _bf16_pieces(LOG2E)

N_HEADS = 8
HEAD_DIM = 64
ATTN_WIDTH = N_HEADS * HEAD_DIM
MOBA_BLOCK = 256
MOBA_TOPK = 3
SGU_CHUNK = 128
SGU_GROUPS = 8
N_EXPERTS = 32
EXPERT_TOPK = 4
SWIGLU_LIMIT = 7.0
SWIGLU_ALPHA = 1.702
NORM_EPS = 1e-5

LANES = 128
SUBLANES = 8
HEADS_PER_LANE_TILE = LANES // HEAD_DIM
MOBA_GROUP = 4
MOBA_BIAS_LANE0 = 64
MOBA_DUMMY_LANE = LANES - 1
PROJ_ROWS = 256
EXPERT_ROWS = 512
ROUTE_TOKENS = PROJ_ROWS
VMEM_LIMIT = 48 * 1024 * 1024
EXPERT_VMEM_LIMIT = 56 * 1024 * 1024


def _sigmoid(x):
    return 1.0 / (1.0 + jnp.exp(-x))


def _store_token_rows(ref, value, base=0):
    n = value.shape[0]
    for c in range(value.shape[1] // LANES):
        ref[pl.ds(base + c, n, stride=SUBLANES), :] = value[:, c * LANES:(c + 1) * LANES]


def _load_token_rows(ref, n, base=0):
    return jnp.concatenate([ref[pl.ds(base + c, n, stride=SUBLANES), :] for c in range(SUBLANES)], axis=1)


def _gelu_exact(x):
    return 0.5 * x * (1.0 + lax.erf(x * (0.5 ** 0.5)))


def _in_proj_kernel(x_ref, g_ref, w_ref, lng_ref, lnb_ref,
                    q_ref, qt_ref, kt_ref, v_ref, kmean_ref, u_ref, vvn_ref, ga_ref, gs_ref):
    x = x_ref[...]
    xn = x * lax.rsqrt(jnp.mean(x * x, axis=-1, keepdims=True) + NORM_EPS) * g_ref[...]
    xb = xn.astype(BF16)

    def proj(lo, hi):
        return jnp.dot(xb, w_ref[:, lo:hi], preferred_element_type=F32)

    a = ATTN_WIDTH
    sw = u_ref.shape[1]
    d = ga_ref.shape[1]
    qf = proj(0, a)
    q_ref[...] = qf
    kf = proj(a, 2 * a)
    for j in range(kf.shape[0] // MOBA_BLOCK):
        qt_ref[j] = qf[j * MOBA_BLOCK:(j + 1) * MOBA_BLOCK].T
        kblk = kf[j * MOBA_BLOCK:(j + 1) * MOBA_BLOCK]
        kt_ref[j] = kblk.T.astype(BF16)
        kmean_ref[j] = jnp.mean(kblk, axis=0, keepdims=True)
    v_ref[...] = proj(2 * a, 3 * a).astype(BF16)
    z0 = 3 * a
    u_ref[...] = _gelu_exact(proj(z0, z0 + sw))
    zv = _gelu_exact(proj(z0 + sw, z0 + 2 * sw))
    mu = jnp.mean(zv, axis=-1, keepdims=True)
    zc = zv - mu
    var = jnp.mean(zc * zc, axis=-1, keepdims=True)
    vvn_ref[...] = zc * lax.rsqrt(var + NORM_EPS) * lng_ref[...] + lnb_ref[...]
    g0 = z0 + 2 * sw
    ga_ref[...] = _sigmoid(proj(g0, g0 + d))
    gs_ref[...] = _sigmoid(proj(g0 + d, g0 + 2 * d))


def _in_proj(x2, g, w_bf, lng, lnb, sgu_width):
    t, d = x2.shape
    tm = PROJ_ROWS
    n_in = w_bf.shape[1]
    row = lambda w: pl.BlockSpec((tm, w), lambda i: (i, 0))
    const = lambda shape: pl.BlockSpec(shape, lambda i: (0,) * len(shape))
    out_shape = (
        jax.ShapeDtypeStruct((t, ATTN_WIDTH), F32),
        jax.ShapeDtypeStruct((t // MOBA_BLOCK, ATTN_WIDTH, MOBA_BLOCK), F32),
        jax.ShapeDtypeStruct((t // MOBA_BLOCK, ATTN_WIDTH, MOBA_BLOCK), BF16),
        jax.ShapeDtypeStruct((t, ATTN_WIDTH), BF16),
        jax.ShapeDtypeStruct((t // MOBA_BLOCK, 1, ATTN_WIDTH), F32),
        jax.ShapeDtypeStruct((t, sgu_width), F32),
        jax.ShapeDtypeStruct((t, sgu_width), F32),
        jax.ShapeDtypeStruct((t, d), F32),
        jax.ShapeDtypeStruct((t, d), F32),
    )
    out_specs = (
        row(ATTN_WIDTH),
        pl.BlockSpec((tm // MOBA_BLOCK, ATTN_WIDTH, MOBA_BLOCK), lambda i: (i, 0, 0)),
        pl.BlockSpec((tm // MOBA_BLOCK, ATTN_WIDTH, MOBA_BLOCK), lambda i: (i, 0, 0)),
        row(ATTN_WIDTH),
        pl.BlockSpec((tm // MOBA_BLOCK, 1, ATTN_WIDTH), lambda i: (i, 0, 0)),
        row(sgu_width), row(sgu_width), row(d), row(d),
    )
    return pl.pallas_call(
        _in_proj_kernel,
        grid=(t // tm,),
        in_specs=[row(d), const((1, d)), const((d, n_in)), const((1, sgu_width)), const((1, sgu_width))],
        out_specs=out_specs,
        out_shape=out_shape,
        compiler_params=pltpu.CompilerParams(dimension_semantics=("arbitrary",), vmem_limit_bytes=VMEM_LIMIT),
        name="in_proj",
    )(x2, g, w_bf, lng, lnb)


def _alibi_slopes():
    slopes = 2.0 ** (-8.0 * np.arange(1, N_HEADS + 1, dtype=np.float64) / N_HEADS)
    assert all(np.log2(s) == np.round(np.log2(s)) for s in slopes), "ALiBi slopes must be powers of two"
    return slopes


def _moba_bias_lane(head_in_tile, part, piece):
    return MOBA_BIAS_LANE0 + (head_in_tile * 2 + part) * len(LOG2E_PIECES) + piece


def _moba_key_table(s):
    nb = s // MOBA_BLOCK
    slopes = _alibi_slopes()
    n_pairs = N_HEADS // HEADS_PER_LANE_TILE
    table = np.zeros((nb + 1, n_pairs, LANES, MOBA_BLOCK), np.float32)
    offs = np.arange(MOBA_BLOCK, dtype=np.float32)
    assert HEADS_PER_LANE_TILE * nb <= MOBA_BIAS_LANE0, "block one-hot rows must not overlap the ALiBi rows"
    for j in range(nb):
        for hh in range(HEADS_PER_LANE_TILE):
            table[j, :, hh * nb + j, :] = 1.0
        for p in range(n_pairs):
            for hh in range(HEADS_PER_LANE_TILE):
                slope = slopes[p * HEADS_PER_LANE_TILE + hh]
                for piece in range(len(LOG2E_PIECES)):
                    table[j, p, _moba_bias_lane(hh, 0, piece), :] = slope * MOBA_BLOCK * j
                    table[j, p, _moba_bias_lane(hh, 1, piece), :] = slope * offs
    table[nb, :, MOBA_DUMMY_LANE, :] = 1.0
    as_bf16 = table.astype(BF16)
    assert np.array_equal(as_bf16.astype(np.float32), table), "bias table must be exact in bf16"
    return jnp.asarray(as_bf16.reshape(nb + 1, n_pairs * LANES, MOBA_BLOCK))


def _moba_kernel(q_ref, qt_ref, kt_ref, v_ref, km_ref, ct_ref, o_ref, qaug_ref, s_ref, mpart_ref, mrow_ref,
                 acc_ref):
    i = pl.program_id(2)
    nb = kt_ref.shape[0]
    blk = MOBA_BLOCK
    q = q_ref[...]
    lane = lax.broadcasted_iota(jnp.int32, (1, LANES), 1)
    n_groups = lax.div(i + (MOBA_GROUP - 1), MOBA_GROUP)
    heads = range(HEADS_PER_LANE_TILE)
    hmasks = [(lane >= HEAD_DIM * hh) & (lane < HEAD_DIM * (hh + 1)) for hh in heads]

    bid = lax.broadcasted_iota(jnp.int32, (nb, blk), 0)
    bid_f = bid.astype(F32)
    masks_t = []
    for hh in heads:
        km_h = jnp.where(hmasks[hh], km_ref[...], 0.0)
        gate = jnp.dot(km_h, qt_ref[...], precision=lax.Precision.HIGHEST, preferred_element_type=F32)
        gate = jnp.where(bid < i, gate, NEG_INF)
        blockmask = jnp.where(bid == i, 0.0, MASK_VALUE)
        for kk in range(MOBA_TOPK):
            gmax = jnp.max(gate, axis=0, keepdims=True)
            first = jnp.min(jnp.where(gate == gmax, bid_f, float(nb)), axis=0, keepdims=True)
            valid = (i > kk).astype(F32)
            first = first * valid + (valid - 1.0)
            hit = bid_f == first
            blockmask = jnp.where(hit, 0.0, blockmask)
            gate = jnp.where(hit, NEG_INF, gate)
        masks_t.append(blockmask)
    masks_t.append(jnp.zeros((LANES - len(masks_t) * nb, blk), F32))
    masks = jnp.concatenate(masks_t, axis=0).T

    for hh in heads:
        feats = jnp.where(lane == MOBA_DUMMY_LANE, MASK_VALUE, 0.0)
        for part in range(2):
            for piece, value in enumerate(LOG2E_PIECES):
                feats = jnp.where(lane == _moba_bias_lane(hh, part, piece), value, feats)
        extra = jnp.where((lane >= hh * nb) & (lane < (hh + 1) * nb), masks, feats)
        qh = jnp.where(hmasks[hh], q, 0.0)
        qaug_ref[hh * blk:(hh + 1) * blk, 0:LANES] = (qh * (HEAD_DIM ** -0.5 * LOG2E)).astype(BF16)
        qaug_ref[hh * blk:(hh + 1) * blk, LANES:2 * LANES] = extra.astype(BF16)

    def scores(jk, jc):
        rhs = jnp.concatenate([kt_ref[jk], ct_ref[jc]], axis=0)
        return jnp.dot(qaug_ref[...], rhs, preferred_element_type=F32)

    def lane_halves_max(s):
        return jnp.maximum(s[:, 0:LANES], s[:, LANES:2 * LANES])

    n_rows = len(heads) * blk
    row_in_block = lax.bitwise_and(lax.broadcasted_iota(jnp.int32, (n_rows, blk), 0), blk - 1)
    row_ge_col = row_in_block >= lax.broadcasted_iota(jnp.int32, (n_rows, blk), 1)
    s_own = jnp.where(row_ge_col, scores(i, i), MASK_VALUE)
    s_ref[nb] = s_own
    mpart_ref[...] = lane_halves_max(s_own)

    def past_block(g, u):
        j = g * MOBA_GROUP + u
        return j, jnp.minimum(j, nb - 1), jnp.where(j < i, j, nb)

    def pass1(g, carry):
        part = mpart_ref[...]
        for u in range(MOBA_GROUP):
            j, jk, jc = past_block(g, u)
            s = scores(jk, jc)
            s_ref[j] = s
            part = jnp.maximum(part, lane_halves_max(s))
        mpart_ref[...] = part
        return carry

    lax.fori_loop(0, n_groups, pass1, 0)

    ones = jnp.ones((blk, LANES), BF16)
    rowmax = jnp.max(mpart_ref[...], axis=1, keepdims=True)
    mrow_ref[...] = jnp.broadcast_to(rowmax, (n_rows, 2 * LANES))

    def weighted(slot, jk):
        p = jnp.exp2(s_ref[slot] - mrow_ref[...]).astype(BF16)
        vaug = jnp.concatenate([v_ref[pl.ds(pl.multiple_of(jk * blk, blk), blk), :], ones], axis=1)
        return jnp.dot(p, vaug, preferred_element_type=F32)

    acc_ref[...] = weighted(nb, i)

    def pass2(g, carry):
        tot = acc_ref[...]
        for u in range(MOBA_GROUP):
            j, jk, _ = past_block(g, u)
            tot = tot + weighted(j, jk)
        acc_ref[...] = tot
        return carry

    lax.fori_loop(0, n_groups, pass2, 0)

    out = jnp.zeros((blk, LANES), F32)
    for hh in heads:
        acc = acc_ref[hh * blk:(hh + 1) * blk, :]
        out = jnp.where(hmasks[hh], acc[:, 0:LANES] / acc[:, LANES:2 * LANES], out)
    o_ref[...] = out


def _moba(q, qt, kt, v, kmean, key_table):
    b, s, a = q.shape
    nb = s // MOBA_BLOCK
    n_pairs = a // LANES
    rows = HEADS_PER_LANE_TILE * MOBA_BLOCK
    return pl.pallas_call(
        _moba_kernel,
        grid=(b, n_pairs, nb),
        in_specs=[
            pl.BlockSpec((None, MOBA_BLOCK, LANES), lambda bi, p, i: (bi, i, p)),
            pl.BlockSpec((None, LANES, MOBA_BLOCK), lambda bi, p, i: (bi * nb + i, p, 0)),
            pl.BlockSpec((nb, LANES, MOBA_BLOCK), lambda bi, p, i: (bi, p, 0)),
            pl.BlockSpec((None, s, LANES), lambda bi, p, i: (bi, 0, p)),
            pl.BlockSpec((None, nb, LANES), lambda bi, p, i: (bi, 0, p)),
            pl.BlockSpec((nb + 1, LANES, MOBA_BLOCK), lambda bi, p, i: (0, p, 0)),
        ],
        out_specs=pl.BlockSpec((None, MOBA_BLOCK, LANES), lambda bi, p, i: (bi, i, p)),
        out_shape=jax.ShapeDtypeStruct((b, s, a), F32),
        scratch_shapes=[
            pltpu.VMEM((rows, 2 * LANES), BF16),
            pltpu.VMEM((nb + 1, rows, MOBA_BLOCK), F32),
            pltpu.VMEM((rows, LANES), F32),
            pltpu.VMEM((rows, 2 * LANES), F32),
            pltpu.VMEM((rows, 2 * LANES), F32),
        ],
        compiler_params=pltpu.CompilerParams(
            dimension_semantics=("arbitrary", "arbitrary", "arbitrary"), vmem_limit_bytes=VMEM_LIMIT),
        name="moba",
    )(q, qt, kt, v, kmean, key_table)


def _mix_kernel(attn_ref, u_ref, vvn_ref, ga_ref, gs_ref, x_ref,
                wao_ref, wso_ref, wmo_ref, wsp_ref, bsp_ref, fg_ref, wr_ref, br_ref,
                h1_ref, xn2_ref, ri_ref, rw_ref, tcnt_ref, wcausal_ref):
    tm = x_ref.shape[0]
    ch = SGU_CHUNK

    @pl.when(pl.program_id(0) == 0)
    def _():
        tril = lax.broadcasted_iota(jnp.int32, (ch, ch), 0) >= lax.broadcasted_iota(jnp.int32, (ch, ch), 1)
        for g in range(SGU_GROUPS):
            wcausal_ref[g] = jnp.where(tril, wsp_ref[g], 0.0).astype(BF16)

    y_attn = jnp.dot(attn_ref[...].astype(BF16), wao_ref[...], preferred_element_type=F32)

    lane = lax.broadcasted_iota(jnp.int32, (1, LANES), 1)
    gdim = vvn_ref.shape[1] // SGU_GROUPS
    groups_per_tile = LANES // gdim
    w_causal = [wcausal_ref[g] for g in range(SGU_GROUPS)]
    rows = []
    for c in range(tm // ch):
        cols = []
        for ct in range(vvn_ref.shape[1] // LANES):
            vp = vvn_ref[c * ch:(c + 1) * ch, ct * LANES:(ct + 1) * LANES]
            acc = jnp.zeros((ch, LANES), F32)
            for gg in range(groups_per_tile):
                gmask = (lane >= gdim * gg) & (lane < gdim * (gg + 1))
                vm = jnp.where(gmask, vp, 0.0).astype(BF16)
                acc = acc + jnp.dot(w_causal[ct * groups_per_tile + gg], vm, preferred_element_type=F32)
            cols.append(acc)
        rows.append(jnp.concatenate(cols, axis=1) + bsp_ref[...])
    mixed = jnp.concatenate(rows, axis=0)
    sgu = u_ref[...] * mixed
    y_sgu = jnp.dot(sgu.astype(BF16), wso_ref[...], preferred_element_type=F32)

    merged = ga_ref[...] * y_attn + gs_ref[...] * y_sgu
    h1 = x_ref[...] + jnp.dot(merged.astype(BF16), wmo_ref[...], preferred_element_type=F32)
    h1_ref[...] = h1
    xn2 = h1 * lax.rsqrt(jnp.mean(h1 * h1, axis=-1, keepdims=True) + NORM_EPS) * fg_ref[...]
    _store_token_rows(xn2_ref, xn2)

    logits = jnp.dot(xn2, wr_ref[...], precision=lax.Precision.HIGHEST,
                     preferred_element_type=F32) + br_ref[...]
    work = jnp.where(lane < N_EXPERTS, logits, NEG_INF)
    lane_f = lane.astype(F32)
    vals, idxs = [], []
    for _ in range(EXPERT_TOPK):
        vmax = jnp.max(work, axis=1, keepdims=True)
        first = jnp.min(jnp.where(work == vmax, lane_f, float(LANES)), axis=1, keepdims=True)
        vals.append(vmax)
        idxs.append(first)
        work = jnp.where(lane_f == first, NEG_INF, work)
    exps = [jnp.exp(v - vals[0]) for v in vals]
    denom = exps[0]
    for e in exps[1:]:
        denom = denom + e
    chosen = jnp.zeros((tm, LANES), F32)
    for first in idxs:
        chosen = jnp.where(lane_f == first, 1.0, chosen)

    strict_lower = (lax.broadcasted_iota(jnp.int32, (tm, tm), 0)
                    > lax.broadcasted_iota(jnp.int32, (tm, tm), 1))
    before = jnp.dot(jnp.where(strict_lower, 1.0, 0.0).astype(BF16), chosen.astype(BF16),
                     preferred_element_type=F32)
    counts = jnp.broadcast_to(jnp.sum(chosen, axis=0, keepdims=True), tcnt_ref.shape)
    lower_expert = (lax.broadcasted_iota(jnp.int32, (LANES, LANES), 0)
                    < lax.broadcasted_iota(jnp.int32, (LANES, LANES), 1))
    expert_start = jnp.dot(counts.astype(BF16), jnp.where(lower_expert, 1.0, 0.0).astype(BF16),
                           preferred_element_type=F32)[0:1, :]
    sorted_pos = before + expert_start
    ri = jnp.zeros((tm, LANES), jnp.int32)
    rw = jnp.zeros((tm, LANES), F32)
    for kk in range(EXPERT_TOPK):
        pos = jnp.sum(jnp.where(lane_f == idxs[kk], sorted_pos, 0.0), axis=1, keepdims=True)
        ri = jnp.where(lane == kk, idxs[kk].astype(jnp.int32), ri)
        ri = jnp.where(lane == EXPERT_TOPK + kk, pos.astype(jnp.int32), ri)
        rw = jnp.where(lane == kk, exps[kk] / denom, rw)
    ri_ref[...] = ri
    rw_ref[...] = rw
    tcnt_ref[...] = counts


def _mix(attn, u, vvn, ga, gs, x2, wao, wso, wmo, wsp, bsp_full, fg, wr_pad, br_pad):
    t, d = x2.shape
    tm = PROJ_ROWS
    row = lambda w: pl.BlockSpec((tm, w), lambda i: (i, 0))
    const = lambda shape: pl.BlockSpec(shape, lambda i: (0,) * len(shape))
    ins = (attn, u, vvn, ga, gs, x2, wao, wso, wmo, wsp, bsp_full, fg, wr_pad, br_pad)
    in_specs = [row(a.shape[1]) for a in ins[:6]] + [const(a.shape) for a in ins[6:]]
    out_shape = (
        jax.ShapeDtypeStruct((t, d), F32),
        jax.ShapeDtypeStruct((t * SUBLANES, LANES), F32),
        jax.ShapeDtypeStruct((t, LANES), jnp.int32),
        jax.ShapeDtypeStruct((t, LANES), F32),
        jax.ShapeDtypeStruct((t // tm * SUBLANES, LANES), F32),
    )
    assert d == SUBLANES * LANES, "a token row is stored as one (8, 128) tile"
    assert tm == ROUTE_TOKENS
    token_rows = pl.BlockSpec((tm * SUBLANES, LANES), lambda i: (i, 0))
    out_specs = (row(d), token_rows, row(LANES), row(LANES), pl.BlockSpec((SUBLANES, LANES), lambda i: (i, 0)))
    return pl.pallas_call(
        _mix_kernel,
        grid=(t // tm,),
        in_specs=in_specs,
        out_specs=out_specs,
        out_shape=out_shape,
        scratch_shapes=[pltpu.VMEM(wsp.shape, BF16)],
        compiler_params=pltpu.CompilerParams(dimension_semantics=("arbitrary",), vmem_limit_bytes=VMEM_LIMIT),
        name="mix_route",
    )(*ins)


def _run_pieces(n, body):
    off = jnp.int32(0)
    for bit in reversed(range(ROUTE_TOKENS.bit_length())):
        size = 1 << bit
        take = lax.bitwise_and(n, size) != 0

        @pl.when(take)
        def _(off=off, size=size):
            body(off, size)

        off = off + jnp.where(take, size, 0)


def _token_rows(ref, r, n):
    return ref.at[pl.ds(pl.multiple_of(r * SUBLANES, SUBLANES), n * SUBLANES)]


def _dispatch_kernel(len_ref, tpos_ref, dst_ref, padlo_ref, padlen_ref, nused_ref, pos_ref, x_ref,
                     xs_hbm, sorted_ref, zeros_ref, sems, zsem):
    step = pl.program_id(0)
    n_steps = pl.num_programs(0)
    n_assign = pos_ref.shape[0]
    tile_rows = zeros_ref.shape[0] // SUBLANES
    slot = lax.rem(step, 2)

    def runs(tile, buf, wait):
        def per_expert(e, carry):
            k = tile * N_EXPERTS + e

            def piece(off, size):
                cp = pltpu.make_async_copy(_token_rows(sorted_ref.at[buf], tpos_ref[k] + off, size),
                                           _token_rows(xs_hbm, dst_ref[k] + off, size), sems.at[buf])
                if wait:
                    cp.wait()
                else:
                    cp.start()

            _run_pieces(len_ref[k], piece)
            return carry

        lax.fori_loop(0, N_EXPERTS, per_expert, 0)

    @pl.when(step == 0)
    def _():
        zeros_ref[...] = jnp.zeros_like(zeros_ref)

        def tail_tile(tl, carry):
            cp = pltpu.make_async_copy(zeros_ref, _token_rows(xs_hbm, tl * tile_rows, tile_rows), zsem)
            cp.start()
            cp.wait()
            return carry

        lax.fori_loop(nused_ref[0], xs_hbm.shape[0] // (tile_rows * SUBLANES), tail_tile, 0)

        def per_expert(e, carry):
            def piece(off, size):
                cp = pltpu.make_async_copy(_token_rows(zeros_ref, 0, size),
                                           _token_rows(xs_hbm, padlo_ref[e] + off, size), zsem)
                cp.start()
                cp.wait()

            _run_pieces(padlen_ref[e], piece)
            return carry

        lax.fori_loop(0, N_EXPERTS, per_expert, 0)

    @pl.when(step >= 2)
    def _():
        runs(step - 2, slot, wait=True)

    def place(a, carry):
        tok = lax.shift_right_logical(a, 2)
        pos = pos_ref[a]
        sorted_ref[slot, pl.ds(pl.multiple_of(pos * SUBLANES, SUBLANES), SUBLANES), :] = (
            x_ref[pl.ds(pl.multiple_of(tok * SUBLANES, SUBLANES), SUBLANES), :])
        return carry

    lax.fori_loop(0, n_assign, place, 0, unroll=8)
    runs(step, slot, wait=False)

    @pl.when(step == n_steps - 1)
    def _():
        @pl.when(step >= 1)
        def _():
            runs(step - 1, 1 - slot, wait=True)

        runs(step, slot, wait=True)


def _dispatch(run_len, run_tpos, run_dst, padlo, padlen, n_used, pos, xn2_rows, n_rows):
    t = xn2_rows.shape[0] // SUBLANES
    n_assign = ROUTE_TOKENS * EXPERT_TOPK
    smem_blk = pl.BlockSpec((n_assign,), lambda c, *_: (c,), memory_space=pltpu.SMEM)
    grid_spec = pltpu.PrefetchScalarGridSpec(
        num_scalar_prefetch=6,
        grid=(t // ROUTE_TOKENS,),
        in_specs=[smem_blk, pl.BlockSpec((ROUTE_TOKENS * SUBLANES, LANES), lambda c, *_: (c, 0))],
        out_specs=pl.BlockSpec(memory_space=pl.ANY),
        scratch_shapes=[
            pltpu.VMEM((2, n_assign * SUBLANES, LANES), F32),
            pltpu.VMEM((EXPERT_ROWS * SUBLANES, LANES), F32),
            pltpu.SemaphoreType.DMA((2,)),
            pltpu.SemaphoreType.DMA(()),
        ],
    )
    return pl.pallas_call(
        _dispatch_kernel,
        grid_spec=grid_spec,
        out_shape=jax.ShapeDtypeStruct((n_rows * SUBLANES, LANES), F32),
        compiler_params=pltpu.CompilerParams(dimension_semantics=("arbitrary",), vmem_limit_bytes=VMEM_LIMIT),
        name="dispatch",
    )(run_len, run_tpos, run_dst, padlo, padlen, n_used, pos, xn2_rows)


def _expert_kernel(be_ref, nused_ref, x_ref, wgu_ref, bg_ref, bu_ref, wd_ref, bd_ref, y_ref, wgu_bf, wd_bf):
    t = pl.program_id(0)
    rows = y_ref.shape[0] // SUBLANES
    de = wd_ref.shape[0]
    grp = 2 * LANES
    active = t < nused_ref[0]
    fresh = jnp.logical_and(
        active, jnp.logical_or(t == 0, be_ref[t] != be_ref[jnp.maximum(t - 1, 0)]))

    @pl.when(fresh)
    def _():
        src = lax.broadcasted_iota(jnp.int32, (grp, grp), 0)
        dst = lax.broadcasted_iota(jnp.int32, (grp, grp), 1)
        wanted = jnp.where(dst < LANES, 2 * dst, 2 * (dst - LANES) + 1)
        perm = jnp.where(src == wanted, 1.0, 0.0).astype(BF16)
        for g in range(wgu_ref.shape[1] // grp):
            cols = slice(g * grp, (g + 1) * grp)
            wgu_bf[:, cols] = jnp.dot(wgu_ref[:, cols].astype(BF16), perm,
                                      preferred_element_type=F32).astype(BF16)
        wd_bf[...] = wd_ref[...].astype(BF16)

    @pl.when(active)
    def _():
        xb = _load_token_rows(x_ref, rows).astype(BF16)
        gu = jnp.dot(xb, wgu_bf[...], preferred_element_type=F32)
        hid = []
        for g in range(de // LANES):
            cols = slice(g * LANES, (g + 1) * LANES)
            gate = gu[:, g * grp:g * grp + LANES] + bg_ref[:, cols]
            up = gu[:, g * grp + LANES:(g + 1) * grp] + bu_ref[:, cols]
            gate = jnp.minimum(gate, SWIGLU_LIMIT)
            up = jnp.clip(up, -SWIGLU_LIMIT, SWIGLU_LIMIT)
            hid.append((gate * _sigmoid(SWIGLU_ALPHA * gate) * (up + 1.0)).astype(BF16))
        y = jnp.dot(jnp.concatenate(hid, axis=1), wd_bf[...], preferred_element_type=F32) + bd_ref[...]
        _store_token_rows(y_ref, y)

    @pl.when(jnp.logical_not(active))
    def _():
        y_ref[...] = jnp.zeros_like(y_ref)


def _experts(block_expert, n_used, xs_rows, wgu, bg, bu, wd, bd):
    n_rows = xs_rows.shape[0] // SUBLANES
    _, d, de2 = wgu.shape
    de = wd.shape[1]
    n_tiles = n_rows // EXPERT_ROWS
    tile_rows = EXPERT_ROWS * SUBLANES
    per_expert = lambda t, be, nu: (be[t], 0, 0)
    grid_spec = pltpu.PrefetchScalarGridSpec(
        num_scalar_prefetch=2,
        grid=(n_tiles,),
        in_specs=[
            pl.BlockSpec((tile_rows, LANES), lambda t, be, nu: (jnp.minimum(t, jnp.maximum(nu[0] - 1, 0)), 0)),
            pl.BlockSpec((None, d, de2), per_expert),
            pl.BlockSpec((None, 1, de), per_expert),
            pl.BlockSpec((None, 1, de), per_expert),
            pl.BlockSpec((None, de, d), per_expert),
            pl.BlockSpec((None, 1, d), per_expert),
        ],
        out_specs=pl.BlockSpec((tile_rows, LANES), lambda t, be, nu: (t, 0)),
        scratch_shapes=[pltpu.VMEM((d, de2), BF16), pltpu.VMEM((de, d), BF16)],
    )
    return pl.pallas_call(
        _expert_kernel,
        grid_spec=grid_spec,
        out_shape=jax.ShapeDtypeStruct((n_rows * SUBLANES, LANES), F32),
        compiler_params=pltpu.CompilerParams(
            dimension_semantics=("arbitrary",), vmem_limit_bytes=EXPERT_VMEM_LIMIT),
        name="experts",
    )(block_expert, n_used, xs_rows, wgu, bg, bu, wd, bd)


def _combine_kernel(len_ref, tpos_ref, src_ref, pos_ref, w_ref, h1_ref, fg_ref, ys_hbm,
                    o_ref, buf_ref, h1rows_ref, outrows_ref, sems):
    step = pl.program_id(0)
    n_steps = pl.num_programs(0)
    tt = h1_ref.shape[0]
    slot = lax.rem(step, 2)

    def runs(tile, buf, wait):
        def per_expert(e, carry):
            k = tile * N_EXPERTS + e

            def piece(off, size):
                cp = pltpu.make_async_copy(_token_rows(ys_hbm, src_ref[k] + off, size),
                                           _token_rows(buf_ref.at[buf], tpos_ref[k] + off, size), sems.at[buf])
                if wait:
                    cp.wait()
                else:
                    cp.start()

            _run_pieces(len_ref[k], piece)
            return carry

        lax.fori_loop(0, N_EXPERTS, per_expert, 0)

    @pl.when(step == 0)
    def _():
        runs(0, 0, wait=False)

    @pl.when(step + 1 < n_steps)
    def _():
        runs(step + 1, 1 - slot, wait=False)

    runs(step, slot, wait=True)

    _store_token_rows(h1rows_ref, h1_ref[...])

    def one_token(tok, carry):
        acc = h1rows_ref[pl.ds(pl.multiple_of(tok * SUBLANES, SUBLANES), SUBLANES), :]
        for kk in range(EXPERT_TOPK):
            a = tok * EXPERT_TOPK + kk
            pos = pos_ref[a]
            acc = acc + w_ref[a] * buf_ref[slot, pl.ds(pl.multiple_of(pos * SUBLANES, SUBLANES), SUBLANES), :]
        outrows_ref[pl.ds(pl.multiple_of(tok * SUBLANES, SUBLANES), SUBLANES), :] = acc
        return carry

    lax.fori_loop(0, tt, one_token, 0, unroll=4)
    h = _load_token_rows(outrows_ref, tt)
    o_ref[...] = h * lax.rsqrt(jnp.mean(h * h, axis=-1, keepdims=True) + NORM_EPS) * fg_ref[...]


def _combine(run_len, run_tpos, run_src, pos, w_flat, h1, fg, ys_rows):
    t, d = h1.shape
    tt = ROUTE_TOKENS
    n_assign = tt * EXPERT_TOPK
    smem_blk = pl.BlockSpec((n_assign,), lambda c, *_: (c,), memory_space=pltpu.SMEM)
    grid_spec = pltpu.PrefetchScalarGridSpec(
        num_scalar_prefetch=3,
        grid=(t // tt,),
        in_specs=[
            smem_blk, smem_blk,
            pl.BlockSpec((tt, d), lambda c, *_: (c, 0)),
            pl.BlockSpec((1, d), lambda c, *_: (0, 0)),
            pl.BlockSpec(memory_space=pl.ANY),
        ],
        out_specs=pl.BlockSpec((tt, d), lambda c, *_: (c, 0)),
        scratch_shapes=[
            pltpu.VMEM((2, n_assign * SUBLANES, LANES), F32),
            pltpu.VMEM((tt * SUBLANES, LANES), F32),
            pltpu.VMEM((tt * SUBLANES, LANES), F32),
            pltpu.SemaphoreType.DMA((2,)),
        ],
    )
    return pl.pallas_call(
        _combine_kernel,
        grid_spec=grid_spec,
        out_shape=jax.ShapeDtypeStruct((t, d), F32),
        compiler_params=pltpu.CompilerParams(dimension_semantics=("arbitrary",), vmem_limit_bytes=VMEM_LIMIT),
        name="combine",
    )(run_len, run_tpos, run_src, pos, w_flat, h1, fg, ys_rows)


def _layer(h, mix_norm_g, w_in, w_attn_out, sgu_ln_g, sgu_ln_b, w_spatial, b_spatial, w_sgu_out,
           w_mix_out, ffn_norm_g, w_router, b_router, w_gate_up, b_gate_up, w_down, b_down, out_g):
    b, s, d = h.shape
    t = b * s
    sgu_width = sgu_ln_g.shape[0]
    x2 = h.reshape(t, d)

    q, qt, kt, v, kmean, u, vvn, ga, gs = _in_proj(
        x2, mix_norm_g.reshape(1, d), w_in.astype(BF16), sgu_ln_g.reshape(1, sgu_width),
        sgu_ln_b.reshape(1, sgu_width), sgu_width)

    nb = s // MOBA_BLOCK
    attn = _moba(q.reshape(b, s, ATTN_WIDTH), qt, kt, v.reshape(b, s, ATTN_WIDTH),
                 kmean.reshape(b, nb, ATTN_WIDTH), _moba_key_table(s)).reshape(t, ATTN_WIDTH)

    gdim = sgu_width // SGU_GROUPS
    bsp_full = jnp.repeat(b_spatial.T, gdim, axis=1)
    wr_pad = jnp.zeros((d, LANES), F32).at[:, :N_EXPERTS].set(w_router)
    br_pad = jnp.zeros((1, LANES), F32).at[0, :N_EXPERTS].set(b_router)
    h1, xn2_rows, ri, rw, tile_cnt = _mix(
        attn, u, vvn, ga, gs, x2, w_attn_out.astype(BF16), w_sgu_out.astype(BF16), w_mix_out.astype(BF16),
        w_spatial, bsp_full, ffn_norm_g.reshape(1, d), wr_pad, br_pad)

    n_assign = t * EXPERT_TOPK
    n_tiles = -(-(n_assign + N_EXPERTS * (EXPERT_ROWS - 1)) // EXPERT_ROWS)
    n_rows = n_tiles * EXPERT_ROWS
    run_len = tile_cnt[::SUBLANES, :N_EXPERTS].astype(jnp.int32)
    counts = jnp.sum(run_len, axis=0)
    padded = (counts + EXPERT_ROWS - 1) // EXPERT_ROWS * EXPERT_ROWS
    pad_end = jnp.cumsum(padded)
    starts = pad_end - padded
    run_tpos = jnp.cumsum(run_len, axis=1) - run_len
    run_row = starts[None, :] + jnp.cumsum(run_len, axis=0) - run_len
    tile_start = jnp.arange(n_tiles, dtype=jnp.int32) * EXPERT_ROWS
    block_expert = jnp.minimum(
        jnp.sum((pad_end[None, :] <= tile_start[:, None]).astype(jnp.int32), axis=1), N_EXPERTS - 1)
    n_used = (pad_end[-1:] // EXPERT_ROWS).astype(jnp.int32)
    pos = ri[:, EXPERT_TOPK:2 * EXPERT_TOPK].reshape(n_assign)
    tables = (run_len.reshape(-1), run_tpos.reshape(-1), run_row.reshape(-1))

    xs_rows = _dispatch(*tables, starts + counts, padded - counts, n_used, pos, xn2_rows, n_rows)

    de = w_down.shape[1]
    bg = b_gate_up[:, 0::2].reshape(N_EXPERTS, 1, de)
    bu = b_gate_up[:, 1::2].reshape(N_EXPERTS, 1, de)
    ys_rows = _experts(block_expert, n_used, xs_rows, w_gate_up, bg, bu, w_down, b_down.reshape(N_EXPERTS, 1, d))

    out = _combine(*tables, pos, rw[:, :EXPERT_TOPK].reshape(n_assign), h1, out_g.reshape(1, d), ys_rows)
    return out.reshape(b, s, d)


def kernel(x, mix_norm_g, w_in, w_attn_out, sgu_ln_g, sgu_ln_b, w_spatial, b_spatial, w_sgu_out, w_mix_out,
           ffn_norm_g, w_router, b_router, w_gate_up, b_gate_up, w_down, b_down, final_norm_g):
    depth = w_in.shape[0]
    assert depth == 1, "the final RMSNorm is fused into the single layer's combine step"
    return _layer(x, mix_norm_g[0], w_in[0], w_attn_out[0], sgu_ln_g[0], sgu_ln_b[0], w_spatial[0],
                  b_spatial[0], w_sgu_out[0], w_mix_out[0], ffn_norm_g[0], w_router[0], b_router[0],
                  w_gate_up[0], b_gate_up[0], w_down[0], b_down[0], final_norm_g)
```

```python
import jax
import jax.numpy as jnp
import numpy as np
from jax import lax
from jax.experimental import pallas as pl
from jax.experimental.pallas import tpu as pltpu

F32 = jnp.float32
BF16 = jnp.bfloat16
NEG_INF = float("-inf")
MASK_VALUE = -1e30


def _bf16_pieces(x, n=3):
    pieces = []
    for _ in range(n):
        p = float(np.asarray(x, np.float32).astype(BF16).astype(np.float32))
        pieces.append(p)
        x = x - p
    return pieces


LOG2E = 1.4426950408889634
LOG2E_PIECES = _bf16_pieces(LOG2E)

N_HEADS = 8
HEAD_DIM = 64
ATTN_WIDTH = N_HEADS * HEAD_DIM
MOBA_BLOCK = 256
MOBA_TOPK = 3
SGU_CHUNK = 128
SGU_GROUPS = 8
N_EXPERTS = 32
EXPERT_TOPK = 4
SWIGLU_LIMIT = 7.0
SWIGLU_ALPHA = 1.702
NORM_EPS = 1e-5

LANES = 128
SUBLANES = 8
HEADS_PER_LANE_TILE = LANES // HEAD_DIM
MOBA_GROUP = 4
MOBA_BIAS_LANE0 = 64
MOBA_DUMMY_LANE = LANES - 1
PROJ_ROWS = 256
EXPERT_ROWS = 512
ROUTE_TOKENS = PROJ_ROWS
VMEM_LIMIT = 48 * 1024 * 1024
EXPERT_VMEM_LIMIT = 56 * 1024 * 1024


def _sigmoid(x):
    return 1.0 / (1.0 + jnp.exp(-x))


def _store_token_rows(ref, value, base=0):
    n = value.shape[0]
    for c in range(value.shape[1] // LANES):
        ref[pl.ds(base + c, n, stride=SUBLANES), :] = value[:, c * LANES:(c + 1) * LANES]


def _load_token_rows(ref, n, base=0):
    return jnp.concatenate([ref[pl.ds(base + c, n, stride=SUBLANES), :] for c in range(SUBLANES)], axis=1)


def _gelu_exact(x):
    return 0.5 * x * (1.0 + lax.erf(x * (0.5 ** 0.5)))


def _in_proj_kernel(x_ref, g_ref, w_ref, lng_ref, lnb_ref,
                    q_ref, qt_ref, kt_ref, v_ref, kmean_ref, u_ref, vvn_ref, ga_ref, gs_ref):
    x = x_ref[...]
    xn = x * lax.rsqrt(jnp.mean(x * x, axis=-1, keepdims=True) + NORM_EPS) * g_ref[...]
    xb = xn.astype(BF16)

    def proj(lo, hi):
        return jnp.dot(xb, w_ref[:, lo:hi], preferred_element_type=F32)

    a = ATTN_WIDTH
    sw = u_ref.shape[1]
    d = ga_ref.shape[1]
    qf = proj(0, a)
    q_ref[...] = qf
    kf = proj(a, 2 * a)
    for j in range(kf.shape[0] // MOBA_BLOCK):
        qt_ref[j] = qf[j * MOBA_BLOCK:(j + 1) * MOBA_BLOCK].T
        kblk = kf[j * MOBA_BLOCK:(j + 1) * MOBA_BLOCK]
        kt_ref[j] = kblk.T.astype(BF16)
        kmean_ref[j] = jnp.mean(kblk, axis=0, keepdims=True)
    v_ref[...] = proj(2 * a, 3 * a).astype(BF16)
    z0 = 3 * a
    u_ref[...] = _gelu_exact(proj(z0, z0 + sw))
    zv = _gelu_exact(proj(z0 + sw, z0 + 2 * sw))
    mu = jnp.mean(zv, axis=-1, keepdims=True)
    zc = zv - mu
    var = jnp.mean(zc * zc, axis=-1, keepdims=True)
    vvn_ref[...] = zc * lax.rsqrt(var + NORM_EPS) * lng_ref[...] + lnb_ref[...]
    g0 = z0 + 2 * sw
    ga_ref[...] = _sigmoid(proj(g0, g0 + d))
    gs_ref[...] = _sigmoid(proj(g0 + d, g0 + 2 * d))


def _in_proj(x2, g, w_bf, lng, lnb, sgu_width):
    t, d = x2.shape
    tm = PROJ_ROWS
    n_in = w_bf.shape[1]
    row = lambda w: pl.BlockSpec((tm, w), lambda i: (i, 0))
    const = lambda shape: pl.BlockSpec(shape, lambda i: (0,) * len(shape))
    out_shape = (
        jax.ShapeDtypeStruct((t, ATTN_WIDTH), F32),
        jax.ShapeDtypeStruct((t // MOBA_BLOCK, ATTN_WIDTH, MOBA_BLOCK), F32),
        jax.ShapeDtypeStruct((t // MOBA_BLOCK, ATTN_WIDTH, MOBA_BLOCK), BF16),
        jax.ShapeDtypeStruct((t, ATTN_WIDTH), BF16),
        jax.ShapeDtypeStruct((t // MOBA_BLOCK, 1, ATTN_WIDTH), F32),
        jax.ShapeDtypeStruct((t, sgu_width), F32),
        jax.ShapeDtypeStruct((t, sgu_width), F32),
        jax.ShapeDtypeStruct((t, d), F32),
        jax.ShapeDtypeStruct((t, d), F32),
    )
    out_specs = (
        row(ATTN_WIDTH),
        pl.BlockSpec((tm // MOBA_BLOCK, ATTN_WIDTH, MOBA_BLOCK), lambda i: (i, 0, 0)),
        pl.BlockSpec((tm // MOBA_BLOCK, ATTN_WIDTH, MOBA_BLOCK), lambda i: (i, 0, 0)),
        row(ATTN_WIDTH),
        pl.BlockSpec((tm // MOBA_BLOCK, 1, ATTN_WIDTH), lambda i: (i, 0, 0)),
        row(sgu_width), row(sgu_width), row(d), row(d),
    )
    return pl.pallas_call(
        _in_proj_kernel,
        grid=(t // tm,),
        in_specs=[row(d), const((1, d)), const((d, n_in)), const((1, sgu_width)), const((1, sgu_width))],
        out_specs=out_specs,
        out_shape=out_shape,
        compiler_params=pltpu.CompilerParams(dimension_semantics=("arbitrary",), vmem_limit_bytes=VMEM_LIMIT),
        name="in_proj",
    )(x2, g, w_bf, lng, lnb)


def _alibi_slopes():
    slopes = 2.0 ** (-8.0 * np.arange(1, N_HEADS + 1, dtype=np.float64) / N_HEADS)
    assert all(np.log2(s) == np.round(np.log2(s)) for s in slopes), "ALiBi slopes must be powers of two"
    return slopes


def _moba_bias_lane(head_in_tile, part, piece):
    return MOBA_BIAS_LANE0 + (head_in_tile * 2 + part) * len(LOG2E_PIECES) + piece


def _moba_key_table(s):
    nb = s // MOBA_BLOCK
    slopes = _alibi_slopes()
    n_pairs = N_HEADS // HEADS_PER_LANE_TILE
    table = np.zeros((nb + 1, n_pairs, LANES, MOBA_BLOCK), np.float32)
    offs = np.arange(MOBA_BLOCK, dtype=np.float32)
    assert HEADS_PER_LANE_TILE * nb <= MOBA_BIAS_LANE0, "block one-hot rows must not overlap the ALiBi rows"
    for j in range(nb):
        for hh in range(HEADS_PER_LANE_TILE):
            table[j, :, hh * nb + j, :] = 1.0
        for p in range(n_pairs):
            for hh in range(HEADS_PER_LANE_TILE):
                slope = slopes[p * HEADS_PER_LANE_TILE + hh]
                for piece in range(len(LOG2E_PIECES)):
                    table[j, p, _moba_bias_lane(hh, 0, piece), :] = slope * MOBA_BLOCK * j
                    table[j, p, _moba_bias_lane(hh, 1, piece), :] = slope * offs
    table[nb, :, MOBA_DUMMY_LANE, :] = 1.0
    as_bf16 = table.astype(BF16)
    assert np.array_equal(as_bf16.astype(np.float32), table), "bias table must be exact in bf16"
    return jnp.asarray(as_bf16.reshape(nb + 1, n_pairs * LANES, MOBA_BLOCK))


def _moba_kernel(q_ref, qt_ref, kt_ref, v_ref, km_ref, ct_ref, causal_ref, o_ref, qaug_ref, s_ref, mpart_ref,
                 mrow_ref, acc_ref):
    i = pl.program_id(2)
    nb = kt_ref.shape[0]
    blk = MOBA_BLOCK
    q = q_ref[...]
    lane = lax.broadcasted_iota(jnp.int32, (1, LANES), 1)
    n_groups = lax.div(i + MOBA_GROUP, MOBA_GROUP)
    heads = range(HEADS_PER_LANE_TILE)
    hmasks = [(lane >= HEAD_DIM * hh) & (lane < HEAD_DIM * (hh + 1)) for hh in heads]

    bid = lax.broadcasted_iota(jnp.int32, (nb, blk), 0)
    bid_f = bid.astype(F32)
    masks_t = []
    for hh in heads:
        km_h = jnp.where(hmasks[hh], km_ref[...], 0.0)
        gate = jnp.dot(km_h, qt_ref[...], precision=lax.Precision.HIGHEST, preferred_element_type=F32)
        gate = jnp.where(bid < i, gate, NEG_INF)
        blockmask = jnp.where(bid == i, 0.0, MASK_VALUE)
        for kk in range(MOBA_TOPK):
            gmax = jnp.max(gate, axis=0, keepdims=True)
            first = jnp.min(jnp.where(gate == gmax, bid_f, float(nb)), axis=0, keepdims=True)
            valid = (i > kk).astype(F32)
            first = first * valid + (valid - 1.0)
            hit = bid_f == first
            blockmask = jnp.where(hit, 0.0, blockmask)
            gate = jnp.where(hit, NEG_INF, gate)
        masks_t.append(blockmask)
    masks_t.append(jnp.zeros((LANES - len(masks_t) * nb, blk), F32))
    masks = jnp.concatenate(masks_t, axis=0).T

    for hh in heads:
        feats = jnp.where(lane == MOBA_DUMMY_LANE, MASK_VALUE, 0.0)
        for part in range(2):
            for piece, value in enumerate(LOG2E_PIECES):
                feats = jnp.where(lane == _moba_bias_lane(hh, part, piece), value, feats)
        extra = jnp.where((lane >= hh * nb) & (lane < (hh + 1) * nb), masks, feats)
        qh = jnp.where(hmasks[hh], q, 0.0)
        qaug_ref[hh * blk:(hh + 1) * blk, 0:LANES] = (qh * (HEAD_DIM ** -0.5 * LOG2E)).astype(BF16)
        qaug_ref[hh * blk:(hh + 1) * blk, LANES:2 * LANES] = extra.astype(BF16)

    def scores(jk, jc):
        rhs = jnp.concatenate([kt_ref[jk], ct_ref[jc]], axis=0)
        return jnp.dot(qaug_ref[...], rhs, preferred_element_type=F32)

    def lane_halves_max(s):
        return jnp.maximum(s[:, 0:LANES], s[:, LANES:2 * LANES])

    n_rows = len(heads) * blk

    def key_block(g, u):
        j = g * MOBA_GROUP + u
        return j, jnp.minimum(j, nb - 1), jnp.where(j <= i, j, nb)

    mpart_ref[...] = jnp.full((n_rows, LANES), NEG_INF, F32)

    def pass1(g, carry):
        part = mpart_ref[...]
        for u in range(MOBA_GROUP):
            j, jk, jc = key_block(g, u)
            s = scores(jk, jc) + causal_ref[(j == i).astype(jnp.int32)]
            s_ref[j] = s
            part = jnp.maximum(part, lane_halves_max(s))
        mpart_ref[...] = part
        return carry

    lax.fori_loop(0, n_groups, pass1, 0)

    ones = jnp.ones((blk, LANES), BF16)
    rowmax = jnp.max(mpart_ref[...], axis=1, keepdims=True)
    mrow_ref[...] = jnp.broadcast_to(rowmax, (n_rows, 2 * LANES))

    def weighted(slot, jk):
        p = jnp.exp2(s_ref[slot] - mrow_ref[...]).astype(BF16)
        vaug = jnp.concatenate([v_ref[pl.ds(pl.multiple_of(jk * blk, blk), blk), :], ones], axis=1)
        return jnp.dot(p, vaug, preferred_element_type=F32)

    acc_ref[...] = jnp.zeros_like(acc_ref)

    def pass2(g, carry):
        tot = acc_ref[...]
        for u in range(MOBA_GROUP):
            j, jk, _ = key_block(g, u)
            tot = tot + weighted(j, jk)
        acc_ref[...] = tot
        return carry

    lax.fori_loop(0, n_groups, pass2, 0)

    out = jnp.zeros((blk, LANES), F32)
    for hh in heads:
        acc = acc_ref[hh * blk:(hh + 1) * blk, :]
        out = jnp.where(hmasks[hh], acc[:, 0:LANES] / acc[:, LANES:2 * LANES], out)
    o_ref[...] = out


def _moba(q, qt, kt, v, kmean, key_table):
    b, s, a = q.shape
    nb = s // MOBA_BLOCK
    n_pairs = a // LANES
    rows = HEADS_PER_LANE_TILE * MOBA_BLOCK
    row_in_block = np.arange(rows)[:, None] % MOBA_BLOCK
    causal = np.where(row_in_block >= np.arange(MOBA_BLOCK)[None, :], 0.0, MASK_VALUE).astype(np.float32)
    causal_tiles = jnp.asarray(np.stack([np.zeros_like(causal), causal]))
    return pl.pallas_call(
        _moba_kernel,
        grid=(b, n_pairs, nb),
        in_specs=[
            pl.BlockSpec((None, MOBA_BLOCK, LANES), lambda bi, p, i: (bi, i, p)),
            pl.BlockSpec((None, LANES, MOBA_BLOCK), lambda bi, p, i: (bi * nb + i, p, 0)),
            pl.BlockSpec((nb, LANES, MOBA_BLOCK), lambda bi, p, i: (bi, p, 0)),
            pl.BlockSpec((None, s, LANES), lambda bi, p, i: (bi, 0, p)),
            pl.BlockSpec((None, nb, LANES), lambda bi, p, i: (bi, 0, p)),
            pl.BlockSpec((nb + 1, LANES, MOBA_BLOCK), lambda bi, p, i: (0, p, 0)),
            pl.BlockSpec((2, rows, MOBA_BLOCK), lambda bi, p, i: (0, 0, 0)),
        ],
        out_specs=pl.BlockSpec((None, MOBA_BLOCK, LANES), lambda bi, p, i: (bi, i, p)),
        out_shape=jax.ShapeDtypeStruct((b, s, a), F32),
        scratch_shapes=[
            pltpu.VMEM((rows, 2 * LANES), BF16),
            pltpu.VMEM((nb + 1, rows, MOBA_BLOCK), F32),
            pltpu.VMEM((rows, LANES), F32),
            pltpu.VMEM((rows, 2 * LANES), F32),
            pltpu.VMEM((rows, 2 * LANES), F32),
        ],
        compiler_params=pltpu.CompilerParams(
            dimension_semantics=("arbitrary", "arbitrary", "arbitrary"), vmem_limit_bytes=VMEM_LIMIT),
        name="moba",
    )(q, qt, kt, v, kmean, key_table, causal_tiles)


def _mix_kernel(attn_ref, u_ref, vvn_ref, ga_ref, gs_ref, x_ref,
                wao_ref, wso_ref, wmo_ref, wsp_ref, bsp_ref, fg_ref, wr_ref, br_ref,
                h1_ref, xn2_ref, ri_ref, rw_ref, tcnt_ref, wcausal_ref):
    tm = x_ref.shape[0]
    ch = SGU_CHUNK

    @pl.when(pl.program_id(0) == 0)
    def _():
        tril = lax.broadcasted_iota(jnp.int32, (ch, ch), 0) >= lax.broadcasted_iota(jnp.int32, (ch, ch), 1)
        for g in range(SGU_GROUPS):
            wcausal_ref[g] = jnp.where(tril, wsp_ref[g], 0.0).astype(BF16)

    y_attn = jnp.dot(attn_ref[...].astype(BF16), wao_ref[...], preferred_element_type=F32)

    lane = lax.broadcasted_iota(jnp.int32, (1, LANES), 1)
    gdim = vvn_ref.shape[1] // SGU_GROUPS
    groups_per_tile = LANES // gdim
    w_causal = [wcausal_ref[g] for g in range(SGU_GROUPS)]
    rows = []
    for c in range(tm // ch):
        cols = []
        for ct in range(vvn_ref.shape[1] // LANES):
            vp = vvn_ref[c * ch:(c + 1) * ch, ct * LANES:(ct + 1) * LANES]
            acc = jnp.zeros((ch, LANES), F32)
            for gg in range(groups_per_tile):
                gmask = (lane >= gdim * gg) & (lane < gdim * (gg + 1))
                vm = jnp.where(gmask, vp, 0.0).astype(BF16)
                acc = acc + jnp.dot(w_causal[ct * groups_per_tile + gg], vm, preferred_element_type=F32)
            cols.append(acc)
        rows.append(jnp.concatenate(cols, axis=1) + bsp_ref[...])
    mixed = jnp.concatenate(rows, axis=0)
    sgu = u_ref[...] * mixed
    y_sgu = jnp.dot(sgu.astype(BF16), wso_ref[...], preferred_element_type=F32)

    merged = ga_ref[...] * y_attn + gs_ref[...] * y_sgu
    h1 = x_ref[...] + jnp.dot(merged.astype(BF16), wmo_ref[...], preferred_element_type=F32)
    h1_ref[...] = h1
    xn2 = h1 * lax.rsqrt(jnp.mean(h1 * h1, axis=-1, keepdims=True) + NORM_EPS) * fg_ref[...]
    _store_token_rows(xn2_ref, xn2)

    def split(a):
        hi = a.astype(BF16)
        return hi, (a - hi.astype(F32)).astype(BF16)

    x_hi, x_lo = split(xn2)
    w_hi, w_lo = split(wr_ref[...])
    logits = (jnp.dot(x_hi, w_hi, preferred_element_type=F32) + jnp.dot(x_lo, w_hi, preferred_element_type=F32)
              + jnp.dot(x_hi, w_lo, preferred_element_type=F32)) + br_ref[...]
    work = jnp.where(lane < N_EXPERTS, logits, NEG_INF)
    lane_f = lane.astype(F32)
    vals, idxs = [], []
    for _ in range(EXPERT_TOPK):
        vmax = jnp.max(work, axis=1, keepdims=True)
        first = jnp.min(jnp.where(work == vmax, lane_f, float(LANES)), axis=1, keepdims=True)
        vals.append(vmax)
        idxs.append(first)
        work = jnp.where(lane_f == first, NEG_INF, work)
    exps = [jnp.exp(v - vals[0]) for v in vals]
    denom = exps[0]
    for e in exps[1:]:
        denom = denom + e
    chosen = jnp.zeros((tm, LANES), F32)
    for first in idxs:
        chosen = jnp.where(lane_f == first, 1.0, chosen)

    strict_lower = (lax.broadcasted_iota(jnp.int32, (tm, tm), 0)
                    > lax.broadcasted_iota(jnp.int32, (tm, tm), 1))
    before = jnp.dot(jnp.where(strict_lower, 1.0, 0.0).astype(BF16), chosen.astype(BF16),
                     preferred_element_type=F32)
    counts = jnp.broadcast_to(jnp.sum(chosen, axis=0, keepdims=True), tcnt_ref.shape)
    lower_expert = (lax.broadcasted_iota(jnp.int32, (LANES, LANES), 0)
                    < lax.broadcasted_iota(jnp.int32, (LANES, LANES), 1))
    expert_start = jnp.dot(counts.astype(BF16), jnp.where(lower_expert, 1.0, 0.0).astype(BF16),
                           preferred_element_type=F32)[0:1, :]
    sorted_pos = before + expert_start
    ri = jnp.zeros((tm, LANES), jnp.int32)
    rw = jnp.zeros((tm, LANES), F32)
    for kk in range(EXPERT_TOPK):
        pos = jnp.sum(jnp.where(lane_f == idxs[kk], sorted_pos, 0.0), axis=1, keepdims=True)
        ri = jnp.where(lane == kk, idxs[kk].astype(jnp.int32), ri)
        ri = jnp.where(lane == EXPERT_TOPK + kk, pos.astype(jnp.int32), ri)
        rw = jnp.where(lane == kk, exps[kk] / denom, rw)
    ri_ref[...] = ri
    rw_ref[...] = rw
    tcnt_ref[...] = counts


def _mix(attn, u, vvn, ga, gs, x2, wao, wso, wmo, wsp, bsp_full, fg, wr_pad, br_pad):
    t, d = x2.shape
    tm = PROJ_ROWS
    row = lambda w: pl.BlockSpec((tm, w), lambda i: (i, 0))
    const = lambda shape: pl.BlockSpec(shape, lambda i: (0,) * len(shape))
    ins = (attn, u, vvn, ga, gs, x2, wao, wso, wmo, wsp, bsp_full, fg, wr_pad, br_pad)
    in_specs = [row(a.shape[1]) for a in ins[:6]] + [const(a.shape) for a in ins[6:]]
    out_shape = (
        jax.ShapeDtypeStruct((t, d), F32),
        jax.ShapeDtypeStruct((t * SUBLANES, LANES), F32),
        jax.ShapeDtypeStruct((t, LANES), jnp.int32),
        jax.ShapeDtypeStruct((t, LANES), F32),
        jax.ShapeDtypeStruct((t // tm * SUBLANES, LANES), F32),
    )
    assert d == SUBLANES * LANES, "a token row is stored as one (8, 128) tile"
    assert tm == ROUTE_TOKENS
    token_rows = pl.BlockSpec((tm * SUBLANES, LANES), lambda i: (i, 0))
    out_specs = (row(d), token_rows, row(LANES), row(LANES), pl.BlockSpec((SUBLANES, LANES), lambda i: (i, 0)))
    return pl.pallas_call(
        _mix_kernel,
        grid=(t // tm,),
        in_specs=in_specs,
        out_specs=out_specs,
        out_shape=out_shape,
        scratch_shapes=[pltpu.VMEM(wsp.shape, BF16)],
        compiler_params=pltpu.CompilerParams(dimension_semantics=("arbitrary",), vmem_limit_bytes=VMEM_LIMIT),
        name="mix_route",
    )(*ins)


def _run_pieces(n, body):
    off = jnp.int32(0)
    for bit in reversed(range(ROUTE_TOKENS.bit_length())):
        size = 1 << bit
        take = lax.bitwise_and(n, size) != 0

        @pl.when(take)
        def _(off=off, size=size):
            body(off, size)

        off = off + jnp.where(take, size, 0)


def _token_rows(ref, r, n):
    return ref.at[pl.ds(pl.multiple_of(r * SUBLANES, SUBLANES), n * SUBLANES)]


def _dispatch_kernel(len_ref, tpos_ref, dst_ref, padlo_ref, padlen_ref, nused_ref, pos_ref, x_ref,
                     xs_hbm, sorted_ref, zeros_ref, sems, zsem):
    step = pl.program_id(0)
    n_steps = pl.num_programs(0)
    n_assign = pos_ref.shape[0]
    tile_rows = zeros_ref.shape[0] // SUBLANES
    slot = lax.rem(step, 2)

    def runs(tile, buf, wait):
        def per_expert(e, carry):
            k = tile * N_EXPERTS + e

            def piece(off, size):
                cp = pltpu.make_async_copy(_token_rows(sorted_ref.at[buf], tpos_ref[k] + off, size),
                                           _token_rows(xs_hbm, dst_ref[k] + off, size), sems.at[buf])
                if wait:
                    cp.wait()
                else:
                    cp.start()

            _run_pieces(len_ref[k], piece)
            return carry

        lax.fori_loop(0, N_EXPERTS, per_expert, 0)

    @pl.when(step == 0)
    def _():
        zeros_ref[...] = jnp.zeros_like(zeros_ref)

        def tail_tile(tl, carry):
            cp = pltpu.make_async_copy(zeros_ref, _token_rows(xs_hbm, tl * tile_rows, tile_rows), zsem)
            cp.start()
            cp.wait()
            return carry

        lax.fori_loop(nused_ref[0], xs_hbm.shape[0] // (tile_rows * SUBLANES), tail_tile, 0)

        def per_expert(e, carry):
            def piece(off, size):
                cp = pltpu.make_async_copy(_token_rows(zeros_ref, 0, size),
                                           _token_rows(xs_hbm, padlo_ref[e] + off, size), zsem)
                cp.start()
                cp.wait()

            _run_pieces(padlen_ref[e], piece)
            return carry

        lax.fori_loop(0, N_EXPERTS, per_expert, 0)

    @pl.when(step >= 2)
    def _():
        runs(step - 2, slot, wait=True)

    def place(a, carry):
        tok = lax.shift_right_logical(a, 2)
        pos = pos_ref[a]
        sorted_ref[slot, pl.ds(pl.multiple_of(pos * SUBLANES, SUBLANES), SUBLANES), :] = (
            x_ref[pl.ds(pl.multiple_of(tok * SUBLANES, SUBLANES), SUBLANES), :])
        return carry

    lax.fori_loop(0, n_assign, place, 0, unroll=8)
    runs(step, slot, wait=False)

    @pl.when(step == n_steps - 1)
    def _():
        @pl.when(step >= 1)
        def _():
            runs(step - 1, 1 - slot, wait=True)

        runs(step, slot, wait=True)


def _dispatch(run_len, run_tpos, run_dst, padlo, padlen, n_used, pos, xn2_rows, n_rows):
    t = xn2_rows.shape[0] // SUBLANES
    n_assign = ROUTE_TOKENS * EXPERT_TOPK
    smem_blk = pl.BlockSpec((n_assign,), lambda c, *_: (c,), memory_space=pltpu.SMEM)
    grid_spec = pltpu.PrefetchScalarGridSpec(
        num_scalar_prefetch=6,
        grid=(t // ROUTE_TOKENS,),
        in_specs=[smem_blk, pl.BlockSpec((ROUTE_TOKENS * SUBLANES, LANES), lambda c, *_: (c, 0))],
        out_specs=pl.BlockSpec(memory_space=pl.ANY),
        scratch_shapes=[
            pltpu.VMEM((2, n_assign * SUBLANES, LANES), F32),
            pltpu.VMEM((EXPERT_ROWS * SUBLANES, LANES), F32),
            pltpu.SemaphoreType.DMA((2,)),
            pltpu.SemaphoreType.DMA(()),
        ],
    )
    return pl.pallas_call(
        _dispatch_kernel,
        grid_spec=grid_spec,
        out_shape=jax.ShapeDtypeStruct((n_rows * SUBLANES, LANES), F32),
        compiler_params=pltpu.CompilerParams(dimension_semantics=("arbitrary",), vmem_limit_bytes=VMEM_LIMIT),
        name="dispatch",
    )(run_len, run_tpos, run_dst, padlo, padlen, n_used, pos, xn2_rows)


def _expert_kernel(be_ref, nused_ref, x_ref, wgu_ref, bg_ref, bu_ref, wd_ref, bd_ref, y_ref, wgu_bf, wd_bf):
    t = pl.program_id(0)
    rows = y_ref.shape[0] // SUBLANES
    de = wd_ref.shape[0]
    grp = 2 * LANES
    active = t < nused_ref[0]
    fresh = jnp.logical_and(
        active, jnp.logical_or(t == 0, be_ref[t] != be_ref[jnp.maximum(t - 1, 0)]))

    @pl.when(fresh)
    def _():
        src = lax.broadcasted_iota(jnp.int32, (grp, grp), 0)
        dst = lax.broadcasted_iota(jnp.int32, (grp, grp), 1)
        wanted = jnp.where(dst < LANES, 2 * dst, 2 * (dst - LANES) + 1)
        perm = jnp.where(src == wanted, 1.0, 0.0).astype(BF16)
        for g in range(wgu_ref.shape[1] // grp):
            cols = slice(g * grp, (g + 1) * grp)
            wgu_bf[:, cols] = jnp.dot(wgu_ref[:, cols].astype(BF16), perm,
                                      preferred_element_type=F32).astype(BF16)
        wd_bf[...] = wd_ref[...].astype(BF16)

    @pl.when(active)
    def _():
        xb = _load_token_rows(x_ref, rows).astype(BF16)
        gu = jnp.dot(xb, wgu_bf[...], preferred_element_type=F32)
        hid = []
        for g in range(de // LANES):
            cols = slice(g * LANES, (g + 1) * LANES)
            gate = gu[:, g * grp:g * grp + LANES] + bg_ref[:, cols]
            up = gu[:, g * grp + LANES:(g + 1) * grp] + bu_ref[:, cols]
            gate = jnp.minimum(gate, SWIGLU_LIMIT)
            up = jnp.clip(up, -SWIGLU_LIMIT, SWIGLU_LIMIT)
            hid.append((gate * _sigmoid(SWIGLU_ALPHA * gate) * (up + 1.0)).astype(BF16))
        y = jnp.dot(jnp.concatenate(hid, axis=1), wd_bf[...], preferred_element_type=F32) + bd_ref[...]
        _store_token_rows(y_ref, y)

    @pl.when(jnp.logical_not(active))
    def _():
        y_ref[...] = jnp.zeros_like(y_ref)


def _experts(block_expert, n_used, xs_rows, wgu, bg, bu, wd, bd):
    n_rows = xs_rows.shape[0] // SUBLANES
    _, d, de2 = wgu.shape
    de = wd.shape[1]
    n_tiles = n_rows // EXPERT_ROWS
    tile_rows = EXPERT_ROWS * SUBLANES
    per_expert = lambda t, be, nu: (be[t], 0, 0)
    grid_spec = pltpu.PrefetchScalarGridSpec(
        num_scalar_prefetch=2,
        grid=(n_tiles,),
        in_specs=[
            pl.BlockSpec((tile_rows, LANES), lambda t, be, nu: (jnp.minimum(t, jnp.maximum(nu[0] - 1, 0)), 0)),
            pl.BlockSpec((None, d, de2), per_expert),
            pl.BlockSpec((None, 1, de), per_expert),
            pl.BlockSpec((None, 1, de), per_expert),
            pl.BlockSpec((None, de, d), per_expert),
            pl.BlockSpec((None, 1, d), per_expert),
        ],
        out_specs=pl.BlockSpec((tile_rows, LANES), lambda t, be, nu: (t, 0)),
        scratch_shapes=[pltpu.VMEM((d, de2), BF16), pltpu.VMEM((de, d), BF16)],
    )
    return pl.pallas_call(
        _expert_kernel,
        grid_spec=grid_spec,
        out_shape=jax.ShapeDtypeStruct((n_rows * SUBLANES, LANES), F32),
        compiler_params=pltpu.CompilerParams(
            dimension_semantics=("arbitrary",), vmem_limit_bytes=EXPERT_VMEM_LIMIT),
        name="experts",
    )(block_expert, n_used, xs_rows, wgu, bg, bu, wd, bd)


def _combine_kernel(len_ref, tpos_ref, src_ref, pos_ref, w_ref, h1_ref, fg_ref, ys_hbm,
                    o_ref, buf_ref, h1rows_ref, outrows_ref, sems):
    step = pl.program_id(0)
    n_steps = pl.num_programs(0)
    tt = h1_ref.shape[0]
    slot = lax.rem(step, 2)

    def runs(tile, buf, wait):
        def per_expert(e, carry):
            k = tile * N_EXPERTS + e

            def piece(off, size):
                cp = pltpu.make_async_copy(_token_rows(ys_hbm, src_ref[k] + off, size),
                                           _token_rows(buf_ref.at[buf], tpos_ref[k] + off, size), sems.at[buf])
                if wait:
                    cp.wait()
                else:
                    cp.start()

            _run_pieces(len_ref[k], piece)
            return carry

        lax.fori_loop(0, N_EXPERTS, per_expert, 0)

    @pl.when(step == 0)
    def _():
        runs(0, 0, wait=False)

    @pl.when(step + 1 < n_steps)
    def _():
        runs(step + 1, 1 - slot, wait=False)

    runs(step, slot, wait=True)

    _store_token_rows(h1rows_ref, h1_ref[...])

    def one_token(tok, carry):
        acc = h1rows_ref[pl.ds(pl.multiple_of(tok * SUBLANES, SUBLANES), SUBLANES), :]
        for kk in range(EXPERT_TOPK):
            a = tok * EXPERT_TOPK + kk
            pos = pos_ref[a]
            acc = acc + w_ref[a] * buf_ref[slot, pl.ds(pl.multiple_of(pos * SUBLANES, SUBLANES), SUBLANES), :]
        outrows_ref[pl.ds(pl.multiple_of(tok * SUBLANES, SUBLANES), SUBLANES), :] = acc
        return carry

    lax.fori_loop(0, tt, one_token, 0, unroll=4)
    h = _load_token_rows(outrows_ref, tt)
    o_ref[...] = h * lax.rsqrt(jnp.mean(h * h, axis=-1, keepdims=True) + NORM_EPS) * fg_ref[...]


def _combine(run_len, run_tpos, run_src, pos, w_flat, h1, fg, ys_rows):
    t, d = h1.shape
    tt = ROUTE_TOKENS
    n_assign = tt * EXPERT_TOPK
    smem_blk = pl.BlockSpec((n_assign,), lambda c, *_: (c,), memory_space=pltpu.SMEM)
    grid_spec = pltpu.PrefetchScalarGridSpec(
        num_scalar_prefetch=3,
        grid=(t // tt,),
        in_specs=[
            smem_blk, smem_blk,
            pl.BlockSpec((tt, d), lambda c, *_: (c, 0)),
            pl.BlockSpec((1, d), lambda c, *_: (0, 0)),
            pl.BlockSpec(memory_space=pl.ANY),
        ],
        out_specs=pl.BlockSpec((tt, d), lambda c, *_: (c, 0)),
        scratch_shapes=[
            pltpu.VMEM((2, n_assign * SUBLANES, LANES), F32),
            pltpu.VMEM((tt * SUBLANES, LANES), F32),
            pltpu.VMEM((tt * SUBLANES, LANES), F32),
            pltpu.SemaphoreType.DMA((2,)),
        ],
    )
    return pl.pallas_call(
        _combine_kernel,
        grid_spec=grid_spec,
        out_shape=jax.ShapeDtypeStruct((t, d), F32),
        compiler_params=pltpu.CompilerParams(dimension_semantics=("arbitrary",), vmem_limit_bytes=VMEM_LIMIT),
        name="combine",
    )(run_len, run_tpos, run_src, pos, w_flat, h1, fg, ys_rows)


def _layer(h, mix_norm_g, w_in, w_attn_out, sgu_ln_g, sgu_ln_b, w_spatial, b_spatial, w_sgu_out,
           w_mix_out, ffn_norm_g, w_router, b_router, w_gate_up, b_gate_up, w_down, b_down, out_g):
    b, s, d = h.shape
    t = b * s
    sgu_width = sgu_ln_g.shape[0]
    x2 = h.reshape(t, d)

    q, qt, kt, v, kmean, u, vvn, ga, gs = _in_proj(
        x2, mix_norm_g.reshape(1, d), w_in.astype(BF16), sgu_ln_g.reshape(1, sgu_width),
        sgu_ln_b.reshape(1, sgu_width), sgu_width)

    nb = s // MOBA_BLOCK
    attn = _moba(q.reshape(b, s, ATTN_WIDTH), qt, kt, v.reshape(b, s, ATTN_WIDTH),
                 kmean.reshape(b, nb, ATTN_WIDTH), _moba_key_table(s)).reshape(t, ATTN_WIDTH)

    gdim = sgu_width // SGU_GROUPS
    bsp_full = jnp.repeat(b_spatial.T, gdim, axis=1)
    wr_pad = jnp.zeros((d, LANES), F32).at[:, :N_EXPERTS].set(w_router)
    br_pad = jnp.zeros((1, LANES), F32).at[0, :N_EXPERTS].set(b_router)
    h1, xn2_rows, ri, rw, tile_cnt = _mix(
        attn, u, vvn, ga, gs, x2, w_attn_out.astype(BF16), w_sgu_out.astype(BF16), w_mix_out.astype(BF16),
        w_spatial, bsp_full, ffn_norm_g.reshape(1, d), wr_pad, br_pad)

    n_assign = t * EXPERT_TOPK
    n_tiles = -(-(n_assign + N_EXPERTS * (EXPERT_ROWS - 1)) // EXPERT_ROWS)
    n_rows = n_tiles * EXPERT_ROWS
    run_len = tile_cnt[::SUBLANES, :N_EXPERTS].astype(jnp.int32)
    counts = jnp.sum(run_len, axis=0)
    padded = (counts + EXPERT_ROWS - 1) // EXPERT_ROWS * EXPERT_ROWS
    pad_end = jnp.cumsum(padded)
    starts = pad_end - padded
    run_tpos = jnp.cumsum(run_len, axis=1) - run_len
    run_row = starts[None, :] + jnp.cumsum(run_len, axis=0) - run_len
    tile_start = jnp.arange(n_tiles, dtype=jnp.int32) * EXPERT_ROWS
    block_expert = jnp.minimum(
        jnp.sum((pad_end[None, :] <= tile_start[:, None]).astype(jnp.int32), axis=1), N_EXPERTS - 1)
    n_used = (pad_end[-1:] // EXPERT_ROWS).astype(jnp.int32)
    pos = ri[:, EXPERT_TOPK:2 * EXPERT_TOPK].reshape(n_assign)
    tables = (run_len.reshape(-1), run_tpos.reshape(-1), run_row.reshape(-1))

    xs_rows = _dispatch(*tables, starts + counts, padded - counts, n_used, pos, xn2_rows, n_rows)

    de = w_down.shape[1]
    bg = b_gate_up[:, 0::2].reshape(N_EXPERTS, 1, de)
    bu = b_gate_up[:, 1::2].reshape(N_EXPERTS, 1, de)
    ys_rows = _experts(block_expert, n_used, xs_rows, w_gate_up, bg, bu, w_down, b_down.reshape(N_EXPERTS, 1, d))

    out = _combine(*tables, pos, rw[:, :EXPERT_TOPK].reshape(n_assign), h1, out_g.reshape(1, d), ys_rows)
    return out.reshape(b, s, d)


def kernel(x, mix_norm_g, w_in, w_attn_out, sgu_ln_g, sgu_ln_b, w_spatial, b_spatial, w_sgu_out, w_mix_out,
           ffn_norm_g, w_router, b_router, w_gate_up, b_gate_up, w_down, b_down, final_norm_g):
    depth = w_in.shape[0]
    assert depth == 1, "the final RMSNorm is fused into the single layer's combine step"
    return _layer(x, mix_norm_g[0], w_in[0], w_attn_out[0], sgu_ln_g[0], sgu_ln_b[0], w_spatial[0],
                  b_spatial[0], w_sgu_out[0], w_mix_out[0], ffn_norm_g[0], w_router[0], b_router[0],
                  w_gate_up[0], b_gate_up[0], w_down[0], b_down[0], final_norm_g)
```

```python
import jax
import jax.numpy as jnp
import numpy as np
from jax import lax
from jax.experimental import pallas as pl
from jax.experimental.pallas import tpu as pltpu

F32 = jnp.float32
BF16 = jnp.bfloat16
NEG_INF = float("-inf")
MASK_VALUE = -1e30


def _bf16_pieces(x, n=3):
    pieces = []
    for _ in range(n):
        p = float(np.asarray(x, np.float32).astype(BF16).astype(np.float32))
        pieces.append(p)
        x = x - p
    return pieces


LOG2E = 1.4426950408889634
LOG2E_PIECES = _bf16_pieces(LOG2E)

N_HEADS = 8
HEAD_DIM = 64
ATTN_WIDTH = N_HEADS * HEAD_DIM
MOBA_BLOCK = 256
MOBA_TOPK = 3
SGU_CHUNK = 128
SGU_GROUPS = 8
N_EXPERTS = 32
EXPERT_TOPK = 4
SWIGLU_LIMIT = 7.0
SWIGLU_ALPHA = 1.702
NORM_EPS = 1e-5

LANES = 128
SUBLANES = 8
HEADS_PER_LANE_TILE = LANES // HEAD_DIM
MOBA_GROUP = 4
MOBA_BIAS_LANE0 = 64
MOBA_DUMMY_LANE = LANES - 1
PROJ_ROWS = 256
EXPERT_ROWS = 512
ROUTE_TOKENS = PROJ_ROWS
VMEM_LIMIT = 48 * 1024 * 1024
EXPERT_VMEM_LIMIT = 56 * 1024 * 1024


def _sigmoid(x):
    return 1.0 / (1.0 + jnp.exp(-x))


def _store_token_rows(ref, value, base=0):
    n = value.shape[0]
    for c in range(value.shape[1] // LANES):
        ref[pl.ds(base + c, n, stride=SUBLANES), :] = value[:, c * LANES:(c + 1) * LANES]


def _load_token_rows(ref, n, base=0):
    return jnp.concatenate([ref[pl.ds(base + c, n, stride=SUBLANES), :] for c in range(SUBLANES)], axis=1)


def _gelu_exact(x):
    return 0.5 * x * (1.0 + lax.erf(x * (0.5 ** 0.5)))


def _in_proj_kernel(x_ref, g_ref, w_ref, lng_ref, lnb_ref,
                    q_ref, qt_ref, kt_ref, v_ref, kmean_ref, u_ref, vvn_ref, ga_ref, gs_ref):
    x = x_ref[...]
    xn = x * lax.rsqrt(jnp.mean(x * x, axis=-1, keepdims=True) + NORM_EPS) * g_ref[...]
    xb = xn.astype(BF16)

    def proj(lo, hi):
        return jnp.dot(xb, w_ref[:, lo:hi], preferred_element_type=F32)

    a = ATTN_WIDTH
    sw = u_ref.shape[1]
    d = ga_ref.shape[1]
    qf = proj(0, a)
    q_ref[...] = qf
    kf = proj(a, 2 * a)
    for j in range(kf.shape[0] // MOBA_BLOCK):
        qt_ref[j] = qf[j * MOBA_BLOCK:(j + 1) * MOBA_BLOCK].T
        kblk = kf[j * MOBA_BLOCK:(j + 1) * MOBA_BLOCK]
        kt_ref[j] = kblk.T.astype(BF16)
        kmean_ref[j] = jnp.mean(kblk, axis=0, keepdims=True)
    v_ref[...] = proj(2 * a, 3 * a).astype(BF16)
    z0 = 3 * a
    u_ref[...] = _gelu_exact(proj(z0, z0 + sw))
    zv = _gelu_exact(proj(z0 + sw, z0 + 2 * sw))
    mu = jnp.mean(zv, axis=-1, keepdims=True)
    zc = zv - mu
    var = jnp.mean(zc * zc, axis=-1, keepdims=True)
    vvn_ref[...] = zc * lax.rsqrt(var + NORM_EPS) * lng_ref[...] + lnb_ref[...]
    g0 = z0 + 2 * sw
    ga_ref[...] = _sigmoid(proj(g0, g0 + d))
    gs_ref[...] = _sigmoid(proj(g0 + d, g0 + 2 * d))


def _in_proj(x2, g, w_bf, lng, lnb, sgu_width):
    t, d = x2.shape
    tm = PROJ_ROWS
    n_in = w_bf.shape[1]
    row = lambda w: pl.BlockSpec((tm, w), lambda i: (i, 0))
    const = lambda shape: pl.BlockSpec(shape, lambda i: (0,) * len(shape))
    out_shape = (
        jax.ShapeDtypeStruct((t, ATTN_WIDTH), F32),
        jax.ShapeDtypeStruct((t // MOBA_BLOCK, ATTN_WIDTH, MOBA_BLOCK), F32),
        jax.ShapeDtypeStruct((t // MOBA_BLOCK, ATTN_WIDTH, MOBA_BLOCK), BF16),
        jax.ShapeDtypeStruct((t, ATTN_WIDTH), BF16),
        jax.ShapeDtypeStruct((t // MOBA_BLOCK, 1, ATTN_WIDTH), F32),
        jax.ShapeDtypeStruct((t, sgu_width), F32),
        jax.ShapeDtypeStruct((t, sgu_width), F32),
        jax.ShapeDtypeStruct((t, d), F32),
        jax.ShapeDtypeStruct((t, d), F32),
    )
    out_specs = (
        row(ATTN_WIDTH),
        pl.BlockSpec((tm // MOBA_BLOCK, ATTN_WIDTH, MOBA_BLOCK), lambda i: (i, 0, 0)),
        pl.BlockSpec((tm // MOBA_BLOCK, ATTN_WIDTH, MOBA_BLOCK), lambda i: (i, 0, 0)),
        row(ATTN_WIDTH),
        pl.BlockSpec((tm // MOBA_BLOCK, 1, ATTN_WIDTH), lambda i: (i, 0, 0)),
        row(sgu_width), row(sgu_width), row(d), row(d),
    )
    return pl.pallas_call(
        _in_proj_kernel,
        grid=(t // tm,),
        in_specs=[row(d), const((1, d)), const((d, n_in)), const((1, sgu_width)), const((1, sgu_width))],
        out_specs=out_specs,
        out_shape=out_shape,
        compiler_params=pltpu.CompilerParams(dimension_semantics=("arbitrary",), vmem_limit_bytes=VMEM_LIMIT),
        name="in_proj",
    )(x2, g, w_bf, lng, lnb)


def _alibi_slopes():
    slopes = 2.0 ** (-8.0 * np.arange(1, N_HEADS + 1, dtype=np.float64) / N_HEADS)
    assert all(np.log2(s) == np.round(np.log2(s)) for s in slopes), "ALiBi slopes must be powers of two"
    return slopes


def _moba_bias_lane(head_in_tile, part, piece):
    return MOBA_BIAS_LANE0 + (head_in_tile * 2 + part) * len(LOG2E_PIECES) + piece


def _moba_key_table(s):
    nb = s // MOBA_BLOCK
    slopes = _alibi_slopes()
    n_pairs = N_HEADS // HEADS_PER_LANE_TILE
    table = np.zeros((nb + 1, n_pairs, LANES, MOBA_BLOCK), np.float32)
    offs = np.arange(MOBA_BLOCK, dtype=np.float32)
    assert HEADS_PER_LANE_TILE * nb <= MOBA_BIAS_LANE0, "block one-hot rows must not overlap the ALiBi rows"
    for j in range(nb):
        for hh in range(HEADS_PER_LANE_TILE):
            table[j, :, hh * nb + j, :] = 1.0
        for p in range(n_pairs):
            for hh in range(HEADS_PER_LANE_TILE):
                slope = slopes[p * HEADS_PER_LANE_TILE + hh]
                for piece in range(len(LOG2E_PIECES)):
                    table[j, p, _moba_bias_lane(hh, 0, piece), :] = slope * MOBA_BLOCK * j
                    table[j, p, _moba_bias_lane(hh, 1, piece), :] = slope * offs
    table[nb, :, MOBA_DUMMY_LANE, :] = 1.0
    as_bf16 = table.astype(BF16)
    assert np.array_equal(as_bf16.astype(np.float32), table), "bias table must be exact in bf16"
    return jnp.asarray(as_bf16.reshape(nb + 1, n_pairs * LANES, MOBA_BLOCK))


def _moba_kernel(q_ref, qt_ref, kt_ref, v_ref, km_ref, ct_ref, causal_ref, o_ref, qaug_ref, s_ref, mpart_ref,
                 mrow_ref, acc_ref):
    i = pl.program_id(2)
    nb = kt_ref.shape[0]
    blk = MOBA_BLOCK
    q = q_ref[...]
    lane = lax.broadcasted_iota(jnp.int32, (1, LANES), 1)
    n_groups = lax.div(i + MOBA_GROUP, MOBA_GROUP)
    heads = range(HEADS_PER_LANE_TILE)
    hmasks = [(lane >= HEAD_DIM * hh) & (lane < HEAD_DIM * (hh + 1)) for hh in heads]

    bid = lax.broadcasted_iota(jnp.int32, (nb, blk), 0)
    bid_f = bid.astype(F32)
    masks_t = []
    for hh in heads:
        km_h = jnp.where(hmasks[hh], km_ref[...], 0.0)
        gate = jnp.dot(km_h, qt_ref[...], precision=lax.Precision.HIGHEST, preferred_element_type=F32)
        gate = jnp.where(bid < i, gate, NEG_INF)
        blockmask = jnp.where(bid == i, 0.0, MASK_VALUE)
        for kk in range(MOBA_TOPK):
            gmax = jnp.max(gate, axis=0, keepdims=True)
            first = jnp.min(jnp.where(gate == gmax, bid_f, float(nb)), axis=0, keepdims=True)
            valid = (i > kk).astype(F32)
            first = first * valid + (valid - 1.0)
            hit = bid_f == first
            blockmask = jnp.where(hit, 0.0, blockmask)
            gate = jnp.where(hit, NEG_INF, gate)
        masks_t.append(blockmask)
    masks_t.append(jnp.zeros((LANES - len(masks_t) * nb, blk), F32))
    masks = jnp.concatenate(masks_t, axis=0).T

    for hh in heads:
        feats = jnp.where(lane == MOBA_DUMMY_LANE, MASK_VALUE, 0.0)
        for part in range(2):
            for piece, value in enumerate(LOG2E_PIECES):
                feats = jnp.where(lane == _moba_bias_lane(hh, part, piece), value, feats)
        extra = jnp.where((lane >= hh * nb) & (lane < (hh + 1) * nb), masks, feats)
        qh = jnp.where(hmasks[hh], q, 0.0)
        qaug_ref[hh * blk:(hh + 1) * blk, 0:LANES] = (qh * (HEAD_DIM ** -0.5 * LOG2E)).astype(BF16)
        qaug_ref[hh * blk:(hh + 1) * blk, LANES:2 * LANES] = extra.astype(BF16)

    def scores(jk, jc):
        rhs = jnp.concatenate([kt_ref[jk], ct_ref[jc]], axis=0)
        return jnp.dot(qaug_ref[...], rhs, preferred_element_type=F32)

    def lane_halves_max(s):
        return jnp.maximum(s[:, 0:LANES], s[:, LANES:2 * LANES])

    n_rows = len(heads) * blk

    def key_block(g, u):
        j = g * MOBA_GROUP + u
        return j, jnp.minimum(j, nb - 1), jnp.where(j <= i, j, nb)

    mpart_ref[...] = jnp.full((n_rows, LANES), NEG_INF, F32)

    def pass1(g, carry):
        part = mpart_ref[...]
        for u in range(MOBA_GROUP):
            j, jk, jc = key_block(g, u)
            s = scores(jk, jc) + causal_ref[(j == i).astype(jnp.int32)]
            s_ref[j] = s
            part = jnp.maximum(part, lane_halves_max(s))
        mpart_ref[...] = part
        return carry

    lax.fori_loop(0, n_groups, pass1, 0)

    ones = jnp.ones((blk, LANES), BF16)
    rowmax = jnp.max(mpart_ref[...], axis=1, keepdims=True)
    mrow_ref[...] = jnp.broadcast_to(rowmax, (n_rows, 2 * LANES))

    def weighted(slot, jk):
        p = jnp.exp2(s_ref[slot] - mrow_ref[...]).astype(BF16)
        vaug = jnp.concatenate([v_ref[pl.ds(pl.multiple_of(jk * blk, blk), blk), :], ones], axis=1)
        return jnp.dot(p, vaug, preferred_element_type=F32)

    acc_ref[...] = jnp.zeros_like(acc_ref)

    def pass2(g, carry):
        tot = acc_ref[...]
        for u in range(MOBA_GROUP):
            j, jk, _ = key_block(g, u)
            tot = tot + weighted(j, jk)
        acc_ref[...] = tot
        return carry

    lax.fori_loop(0, n_groups, pass2, 0)

    out = jnp.zeros((blk, LANES), F32)
    for hh in heads:
        acc = acc_ref[hh * blk:(hh + 1) * blk, :]
        out = jnp.where(hmasks[hh], acc[:, 0:LANES] / acc[:, LANES:2 * LANES], out)
    o_ref[...] = out


def _moba(q, qt, kt, v, kmean, key_table):
    b, s, a = q.shape
    nb = s // MOBA_BLOCK
    n_pairs = a // LANES
    rows = HEADS_PER_LANE_TILE * MOBA_BLOCK
    row_in_block = np.arange(rows)[:, None] % MOBA_BLOCK
    causal = np.where(row_in_block >= np.arange(MOBA_BLOCK)[None, :], 0.0, MASK_VALUE).astype(np.float32)
    causal_tiles = jnp.asarray(np.stack([np.zeros_like(causal), causal]))
    return pl.pallas_call(
        _moba_kernel,
        grid=(b, n_pairs, nb),
        in_specs=[
            pl.BlockSpec((None, MOBA_BLOCK, LANES), lambda bi, p, i: (bi, i, p)),
            pl.BlockSpec((None, LANES, MOBA_BLOCK), lambda bi, p, i: (bi * nb + i, p, 0)),
            pl.BlockSpec((nb, LANES, MOBA_BLOCK), lambda bi, p, i: (bi, p, 0)),
            pl.BlockSpec((None, s, LANES), lambda bi, p, i: (bi, 0, p)),
            pl.BlockSpec((None, nb, LANES), lambda bi, p, i: (bi, 0, p)),
            pl.BlockSpec((nb + 1, LANES, MOBA_BLOCK), lambda bi, p, i: (0, p, 0)),
            pl.BlockSpec((2, rows, MOBA_BLOCK), lambda bi, p, i: (0, 0, 0)),
        ],
        out_specs=pl.BlockSpec((None, MOBA_BLOCK, LANES), lambda bi, p, i: (bi, i, p)),
        out_shape=jax.ShapeDtypeStruct((b, s, a), F32),
        scratch_shapes=[
            pltpu.VMEM((rows, 2 * LANES), BF16),
            pltpu.VMEM((nb + 1, rows, MOBA_BLOCK), F32),
            pltpu.VMEM((rows, LANES), F32),
            pltpu.VMEM((rows, 2 * LANES), F32),
            pltpu.VMEM((rows, 2 * LANES), F32),
        ],
        compiler_params=pltpu.CompilerParams(
            dimension_semantics=("arbitrary", "arbitrary", "arbitrary"), vmem_limit_bytes=VMEM_LIMIT),
        name="moba",
    )(q, qt, kt, v, kmean, key_table, causal_tiles)


def _mix_kernel(attn_ref, u_ref, vvn_ref, ga_ref, gs_ref, x_ref,
                wao_ref, wso_ref, wmo_ref, wsp_ref, bsp_ref, fg_ref, wr_ref, br_ref,
                h1_ref, xn2_ref, ri_ref, rw_ref, tcnt_ref, wcausal_ref):
    tm = x_ref.shape[0]
    ch = SGU_CHUNK

    @pl.when(pl.program_id(0) == 0)
    def _():
        tril = lax.broadcasted_iota(jnp.int32, (ch, ch), 0) >= lax.broadcasted_iota(jnp.int32, (ch, ch), 1)
        for g in range(SGU_GROUPS):
            wcausal_ref[g] = jnp.where(tril, wsp_ref[g], 0.0).astype(BF16)

    y_attn = jnp.dot(attn_ref[...].astype(BF16), wao_ref[...], preferred_element_type=F32)

    lane = lax.broadcasted_iota(jnp.int32, (1, LANES), 1)
    gdim = vvn_ref.shape[1] // SGU_GROUPS
    groups_per_tile = LANES // gdim
    w_causal = [wcausal_ref[g] for g in range(SGU_GROUPS)]
    rows = []
    for c in range(tm // ch):
        cols = []
        for ct in range(vvn_ref.shape[1] // LANES):
            vp = vvn_ref[c * ch:(c + 1) * ch, ct * LANES:(ct + 1) * LANES]
            acc = jnp.zeros((ch, LANES), F32)
            for gg in range(groups_per_tile):
                gmask = (lane >= gdim * gg) & (lane < gdim * (gg + 1))
                vm = jnp.where(gmask, vp, 0.0).astype(BF16)
                acc = acc + jnp.dot(w_causal[ct * groups_per_tile + gg], vm, preferred_element_type=F32)
            cols.append(acc)
        rows.append(jnp.concatenate(cols, axis=1) + bsp_ref[...])
    mixed = jnp.concatenate(rows, axis=0)
    sgu = u_ref[...] * mixed
    y_sgu = jnp.dot(sgu.astype(BF16), wso_ref[...], preferred_element_type=F32)

    merged = ga_ref[...] * y_attn + gs_ref[...] * y_sgu
    h1 = x_ref[...] + jnp.dot(merged.astype(BF16), wmo_ref[...], preferred_element_type=F32)
    h1_ref[...] = h1
    xn2 = h1 * lax.rsqrt(jnp.mean(h1 * h1, axis=-1, keepdims=True) + NORM_EPS) * fg_ref[...]
    _store_token_rows(xn2_ref, xn2)

    def split(a):
        hi = a.astype(BF16)
        return hi, (a - hi.astype(F32)).astype(BF16)

    x_hi, x_lo = split(xn2)
    w_hi, w_lo = split(wr_ref[...])
    logits = (jnp.dot(x_hi, w_hi, preferred_element_type=F32) + jnp.dot(x_lo, w_hi, preferred_element_type=F32)
              + jnp.dot(x_hi, w_lo, preferred_element_type=F32)) + br_ref[...]
    work = jnp.where(lane < N_EXPERTS, logits, NEG_INF)
    lane_f = lane.astype(F32)
    vals, idxs = [], []
    for _ in range(EXPERT_TOPK):
        vmax = jnp.max(work, axis=1, keepdims=True)
        first = jnp.min(jnp.where(work == vmax, lane_f, float(LANES)), axis=1, keepdims=True)
        vals.append(vmax)
        idxs.append(first)
        work = jnp.where(lane_f == first, NEG_INF, work)
    exps = [jnp.exp(v - vals[0]) for v in vals]
    denom = exps[0]
    for e in exps[1:]:
        denom = denom + e
    chosen = jnp.zeros((tm, LANES), F32)
    for first in idxs:
        chosen = jnp.where(lane_f == first, 1.0, chosen)

    strict_lower = (lax.broadcasted_iota(jnp.int32, (tm, tm), 0)
                    > lax.broadcasted_iota(jnp.int32, (tm, tm), 1))
    before = jnp.dot(jnp.where(strict_lower, 1.0, 0.0).astype(BF16), chosen.astype(BF16),
                     preferred_element_type=F32)
    counts = jnp.broadcast_to(jnp.sum(chosen, axis=0, keepdims=True), tcnt_ref.shape)
    lower_expert = (lax.broadcasted_iota(jnp.int32, (LANES, LANES), 0)
                    < lax.broadcasted_iota(jnp.int32, (LANES, LANES), 1))
    expert_start = jnp.dot(counts.astype(BF16), jnp.where(lower_expert, 1.0, 0.0).astype(BF16),
                           preferred_element_type=F32)[0:1, :]
    sorted_pos = before + expert_start
    ri = jnp.zeros((tm, LANES), jnp.int32)
    rw = jnp.zeros((tm, LANES), F32)
    for kk in range(EXPERT_TOPK):
        pos = jnp.sum(jnp.where(lane_f == idxs[kk], sorted_pos, 0.0), axis=1, keepdims=True)
        ri = jnp.where(lane == kk, idxs[kk].astype(jnp.int32), ri)
        ri = jnp.where(lane == EXPERT_TOPK + kk, pos.astype(jnp.int32), ri)
        rw = jnp.where(lane == kk, exps[kk] / denom, rw)
    ri_ref[...] = ri
    rw_ref[...] = rw
    tcnt_ref[...] = counts


def _mix(attn, u, vvn, ga, gs, x2, wao, wso, wmo, wsp, bsp_full, fg, wr_pad, br_pad):
    t, d = x2.shape
    tm = PROJ_ROWS
    row = lambda w: pl.BlockSpec((tm, w), lambda i: (i, 0))
    const = lambda shape: pl.BlockSpec(shape, lambda i: (0,) * len(shape))
    ins = (attn, u, vvn, ga, gs, x2, wao, wso, wmo, wsp, bsp_full, fg, wr_pad, br_pad)
    in_specs = [row(a.shape[1]) for a in ins[:6]] + [const(a.shape) for a in ins[6:]]
    out_shape = (
        jax.ShapeDtypeStruct((t, d), F32),
        jax.ShapeDtypeStruct((t * SUBLANES, LANES), F32),
        jax.ShapeDtypeStruct((t, LANES), jnp.int32),
        jax.ShapeDtypeStruct((t, LANES), F32),
        jax.ShapeDtypeStruct((t // tm * SUBLANES, LANES), F32),
    )
    assert d == SUBLANES * LANES, "a token row is stored as one (8, 128) tile"
    assert tm == ROUTE_TOKENS
    token_rows = pl.BlockSpec((tm * SUBLANES, LANES), lambda i: (i, 0))
    out_specs = (row(d), token_rows, row(LANES), row(LANES), pl.BlockSpec((SUBLANES, LANES), lambda i: (i, 0)))
    return pl.pallas_call(
        _mix_kernel,
        grid=(t // tm,),
        in_specs=in_specs,
        out_specs=out_specs,
        out_shape=out_shape,
        scratch_shapes=[pltpu.VMEM(wsp.shape, BF16)],
        compiler_params=pltpu.CompilerParams(dimension_semantics=("arbitrary",), vmem_limit_bytes=VMEM_LIMIT),
        name="mix_route",
    )(*ins)


def _run_pieces(n, body):
    off = jnp.int32(0)
    for bit in reversed(range(ROUTE_TOKENS.bit_length())):
        size = 1 << bit
        take = lax.bitwise_and(n, size) != 0

        @pl.when(take)
        def _(off=off, size=size):
            body(off, size)

        off = off + jnp.where(take, size, 0)


def _token_rows(ref, r, n):
    return ref.at[pl.ds(pl.multiple_of(r * SUBLANES, SUBLANES), n * SUBLANES)]


def _dispatch_kernel(len_ref, tpos_ref, dst_ref, padlo_ref, padlen_ref, nused_ref, pos_ref, x_ref,
                     xs_hbm, sorted_ref, zeros_ref, sems, zsem):
    step = pl.program_id(0)
    n_steps = pl.num_programs(0)
    n_assign = pos_ref.shape[0]
    tile_rows = zeros_ref.shape[0] // SUBLANES
    slot = lax.rem(step, 2)

    def start_runs(tile, buf):
        def per_expert(e, carry):
            k = tile * N_EXPERTS + e

            def piece(off, size):
                pltpu.make_async_copy(_token_rows(sorted_ref.at[buf], tpos_ref[k] + off, size),
                                      _token_rows(xs_hbm, dst_ref[k] + off, size), sems.at[buf]).start()

            _run_pieces(len_ref[k], piece)
            return carry

        lax.fori_loop(0, N_EXPERTS, per_expert, 0)

    def wait_runs(buf):
        whole = sorted_ref.at[buf]
        pltpu.make_async_copy(whole, _token_rows(xs_hbm, 0, n_assign), sems.at[buf]).wait()

    @pl.when(step == 0)
    def _():
        zeros_ref[...] = jnp.zeros_like(zeros_ref)

        def tail_tile(tl, carry):
            cp = pltpu.make_async_copy(zeros_ref, _token_rows(xs_hbm, tl * tile_rows, tile_rows), zsem)
            cp.start()
            cp.wait()
            return carry

        lax.fori_loop(nused_ref[0], xs_hbm.shape[0] // (tile_rows * SUBLANES), tail_tile, 0)

        def per_expert(e, carry):
            def piece(off, size):
                cp = pltpu.make_async_copy(_token_rows(zeros_ref, 0, size),
                                           _token_rows(xs_hbm, padlo_ref[e] + off, size), zsem)
                cp.start()
                cp.wait()

            _run_pieces(padlen_ref[e], piece)
            return carry

        lax.fori_loop(0, N_EXPERTS, per_expert, 0)

    @pl.when(step >= 2)
    def _():
        wait_runs(slot)

    def place(a, carry):
        tok = lax.shift_right_logical(a, 2)
        pos = pos_ref[a]
        sorted_ref[slot, pl.ds(pl.multiple_of(pos * SUBLANES, SUBLANES), SUBLANES), :] = (
            x_ref[pl.ds(pl.multiple_of(tok * SUBLANES, SUBLANES), SUBLANES), :])
        return carry

    lax.fori_loop(0, n_assign, place, 0, unroll=8)
    start_runs(step, slot)

    @pl.when(step == n_steps - 1)
    def _():
        @pl.when(step >= 1)
        def _():
            wait_runs(1 - slot)

        wait_runs(slot)


def _dispatch(run_len, run_tpos, run_dst, padlo, padlen, n_used, pos, xn2_rows, n_rows):
    t = xn2_rows.shape[0] // SUBLANES
    n_assign = ROUTE_TOKENS * EXPERT_TOPK
    smem_blk = pl.BlockSpec((n_assign,), lambda c, *_: (c,), memory_space=pltpu.SMEM)
    grid_spec = pltpu.PrefetchScalarGridSpec(
        num_scalar_prefetch=6,
        grid=(t // ROUTE_TOKENS,),
        in_specs=[smem_blk, pl.BlockSpec((ROUTE_TOKENS * SUBLANES, LANES), lambda c, *_: (c, 0))],
        out_specs=pl.BlockSpec(memory_space=pl.ANY),
        scratch_shapes=[
            pltpu.VMEM((2, n_assign * SUBLANES, LANES), F32),
            pltpu.VMEM((EXPERT_ROWS * SUBLANES, LANES), F32),
            pltpu.SemaphoreType.DMA((2,)),
            pltpu.SemaphoreType.DMA(()),
        ],
    )
    return pl.pallas_call(
        _dispatch_kernel,
        grid_spec=grid_spec,
        out_shape=jax.ShapeDtypeStruct((n_rows * SUBLANES, LANES), F32),
        compiler_params=pltpu.CompilerParams(dimension_semantics=("arbitrary",), vmem_limit_bytes=VMEM_LIMIT),
        name="dispatch",
    )(run_len, run_tpos, run_dst, padlo, padlen, n_used, pos, xn2_rows)


def _expert_kernel(be_ref, nused_ref, x_ref, wgu_ref, bg_ref, bu_ref, wd_ref, bd_ref, y_ref, wgu_bf, wd_bf):
    t = pl.program_id(0)
    rows = y_ref.shape[0] // SUBLANES
    de = wd_ref.shape[0]
    grp = 2 * LANES
    active = t < nused_ref[0]
    fresh = jnp.logical_and(
        active, jnp.logical_or(t == 0, be_ref[t] != be_ref[jnp.maximum(t - 1, 0)]))

    @pl.when(fresh)
    def _():
        src = lax.broadcasted_iota(jnp.int32, (grp, grp), 0)
        dst = lax.broadcasted_iota(jnp.int32, (grp, grp), 1)
        wanted = jnp.where(dst < LANES, 2 * dst, 2 * (dst - LANES) + 1)
        perm = jnp.where(src == wanted, 1.0, 0.0).astype(BF16)
        for g in range(wgu_ref.shape[1] // grp):
            cols = slice(g * grp, (g + 1) * grp)
            wgu_bf[:, cols] = jnp.dot(wgu_ref[:, cols].astype(BF16), perm,
                                      preferred_element_type=F32).astype(BF16)
        wd_bf[...] = wd_ref[...].astype(BF16)

    @pl.when(active)
    def _():
        xb = _load_token_rows(x_ref, rows).astype(BF16)
        gu = jnp.dot(xb, wgu_bf[...], preferred_element_type=F32)
        hid = []
        for g in range(de // LANES):
            cols = slice(g * LANES, (g + 1) * LANES)
            gate = gu[:, g * grp:g * grp + LANES] + bg_ref[:, cols]
            up = gu[:, g * grp + LANES:(g + 1) * grp] + bu_ref[:, cols]
            gate = jnp.minimum(gate, SWIGLU_LIMIT)
            up = jnp.clip(up, -SWIGLU_LIMIT, SWIGLU_LIMIT)
            hid.append((gate * _sigmoid(SWIGLU_ALPHA * gate) * (up + 1.0)).astype(BF16))
        y = jnp.dot(jnp.concatenate(hid, axis=1), wd_bf[...], preferred_element_type=F32) + bd_ref[...]
        _store_token_rows(y_ref, y)

    @pl.when(jnp.logical_not(active))
    def _():
        y_ref[...] = jnp.zeros_like(y_ref)


def _experts(block_expert, n_used, xs_rows, wgu, bg, bu, wd, bd):
    n_rows = xs_rows.shape[0] // SUBLANES
    _, d, de2 = wgu.shape
    de = wd.shape[1]
    n_tiles = n_rows // EXPERT_ROWS
    tile_rows = EXPERT_ROWS * SUBLANES
    per_expert = lambda t, be, nu: (be[t], 0, 0)
    grid_spec = pltpu.PrefetchScalarGridSpec(
        num_scalar_prefetch=2,
        grid=(n_tiles,),
        in_specs=[
            pl.BlockSpec((tile_rows, LANES), lambda t, be, nu: (jnp.minimum(t, jnp.maximum(nu[0] - 1, 0)), 0)),
            pl.BlockSpec((None, d, de2), per_expert),
            pl.BlockSpec((None, 1, de), per_expert),
            pl.BlockSpec((None, 1, de), per_expert),
            pl.BlockSpec((None, de, d), per_expert),
            pl.BlockSpec((None, 1, d), per_expert),
        ],
        out_specs=pl.BlockSpec((tile_rows, LANES), lambda t, be, nu: (t, 0)),
        scratch_shapes=[pltpu.VMEM((d, de2), BF16), pltpu.VMEM((de, d), BF16)],
    )
    return pl.pallas_call(
        _expert_kernel,
        grid_spec=grid_spec,
        out_shape=jax.ShapeDtypeStruct((n_rows * SUBLANES, LANES), F32),
        compiler_params=pltpu.CompilerParams(
            dimension_semantics=("arbitrary",), vmem_limit_bytes=EXPERT_VMEM_LIMIT),
        name="experts",
    )(block_expert, n_used, xs_rows, wgu, bg, bu, wd, bd)


def _combine_kernel(len_ref, tpos_ref, src_ref, pos_ref, w_ref, h1_ref, fg_ref, ys_hbm,
                    o_ref, buf_ref, h1rows_ref, outrows_ref, sems):
    step = pl.program_id(0)
    n_steps = pl.num_programs(0)
    tt = h1_ref.shape[0]
    slot = lax.rem(step, 2)

    def start_runs(tile, buf):
        def per_expert(e, carry):
            k = tile * N_EXPERTS + e

            def piece(off, size):
                pltpu.make_async_copy(_token_rows(ys_hbm, src_ref[k] + off, size),
                                      _token_rows(buf_ref.at[buf], tpos_ref[k] + off, size), sems.at[buf]).start()

            _run_pieces(len_ref[k], piece)
            return carry

        lax.fori_loop(0, N_EXPERTS, per_expert, 0)

    def wait_runs(buf):
        whole = buf_ref.at[buf]
        pltpu.make_async_copy(_token_rows(ys_hbm, 0, tt * EXPERT_TOPK), whole, sems.at[buf]).wait()

    @pl.when(step == 0)
    def _():
        start_runs(0, 0)

    @pl.when(step + 1 < n_steps)
    def _():
        start_runs(step + 1, 1 - slot)

    wait_runs(slot)

    _store_token_rows(h1rows_ref, h1_ref[...])

    def one_token(tok, carry):
        acc = h1rows_ref[pl.ds(pl.multiple_of(tok * SUBLANES, SUBLANES), SUBLANES), :]
        for kk in range(EXPERT_TOPK):
            a = tok * EXPERT_TOPK + kk
            pos = pos_ref[a]
            acc = acc + w_ref[a] * buf_ref[slot, pl.ds(pl.multiple_of(pos * SUBLANES, SUBLANES), SUBLANES), :]
        outrows_ref[pl.ds(pl.multiple_of(tok * SUBLANES, SUBLANES), SUBLANES), :] = acc
        return carry

    lax.fori_loop(0, tt, one_token, 0, unroll=4)
    h = _load_token_rows(outrows_ref, tt)
    o_ref[...] = h * lax.rsqrt(jnp.mean(h * h, axis=-1, keepdims=True) + NORM_EPS) * fg_ref[...]


def _combine(run_len, run_tpos, run_src, pos, w_flat, h1, fg, ys_rows):
    t, d = h1.shape
    tt = ROUTE_TOKENS
    n_assign = tt * EXPERT_TOPK
    smem_blk = pl.BlockSpec((n_assign,), lambda c, *_: (c,), memory_space=pltpu.SMEM)
    grid_spec = pltpu.PrefetchScalarGridSpec(
        num_scalar_prefetch=3,
        grid=(t // tt,),
        in_specs=[
            smem_blk, smem_blk,
            pl.BlockSpec((tt, d), lambda c, *_: (c, 0)),
            pl.BlockSpec((1, d), lambda c, *_: (0, 0)),
            pl.BlockSpec(memory_space=pl.ANY),
        ],
        out_specs=pl.BlockSpec((tt, d), lambda c, *_: (c, 0)),
        scratch_shapes=[
            pltpu.VMEM((2, n_assign * SUBLANES, LANES), F32),
            pltpu.VMEM((tt * SUBLANES, LANES), F32),
            pltpu.VMEM((tt * SUBLANES, LANES), F32),
            pltpu.SemaphoreType.DMA((2,)),
        ],
    )
    return pl.pallas_call(
        _combine_kernel,
        grid_spec=grid_spec,
        out_shape=jax.ShapeDtypeStruct((t, d), F32),
        compiler_params=pltpu.CompilerParams(dimension_semantics=("arbitrary",), vmem_limit_bytes=VMEM_LIMIT),
        name="combine",
    )(run_len, run_tpos, run_src, pos, w_flat, h1, fg, ys_rows)


def _layer(h, mix_norm_g, w_in, w_attn_out, sgu_ln_g, sgu_ln_b, w_spatial, b_spatial, w_sgu_out,
           w_mix_out, ffn_norm_g, w_router, b_router, w_gate_up, b_gate_up, w_down, b_down, out_g):
    b, s, d = h.shape
    t = b * s
    sgu_width = sgu_ln_g.shape[0]
    x2 = h.reshape(t, d)

    q, qt, kt, v, kmean, u, vvn, ga, gs = _in_proj(
        x2, mix_norm_g.reshape(1, d), w_in.astype(BF16), sgu_ln_g.reshape(1, sgu_width),
        sgu_ln_b.reshape(1, sgu_width), sgu_width)

    nb = s // MOBA_BLOCK
    attn = _moba(q.reshape(b, s, ATTN_WIDTH), qt, kt, v.reshape(b, s, ATTN_WIDTH),
                 kmean.reshape(b, nb, ATTN_WIDTH), _moba_key_table(s)).reshape(t, ATTN_WIDTH)

    gdim = sgu_width // SGU_GROUPS
    bsp_full = jnp.repeat(b_spatial.T, gdim, axis=1)
    wr_pad = jnp.zeros((d, LANES), F32).at[:, :N_EXPERTS].set(w_router)
    br_pad = jnp.zeros((1, LANES), F32).at[0, :N_EXPERTS].set(b_router)
    h1, xn2_rows, ri, rw, tile_cnt = _mix(
        attn, u, vvn, ga, gs, x2, w_attn_out.astype(BF16), w_sgu_out.astype(BF16), w_mix_out.astype(BF16),
        w_spatial, bsp_full, ffn_norm_g.reshape(1, d), wr_pad, br_pad)

    n_assign = t * EXPERT_TOPK
    n_tiles = -(-(n_assign + N_EXPERTS * (EXPERT_ROWS - 1)) // EXPERT_ROWS)
    n_rows = n_tiles * EXPERT_ROWS
    run_len = tile_cnt[::SUBLANES, :N_EXPERTS].astype(jnp.int32)
    counts = jnp.sum(run_len, axis=0)
    padded = (counts + EXPERT_ROWS - 1) // EXPERT_ROWS * EXPERT_ROWS
    pad_end = jnp.cumsum(padded)
    starts = pad_end - padded
    run_tpos = jnp.cumsum(run_len, axis=1) - run_len
    run_row = starts[None, :] + jnp.cumsum(run_len, axis=0) - run_len
    tile_start = jnp.arange(n_tiles, dtype=jnp.int32) * EXPERT_ROWS
    block_expert = jnp.minimum(
        jnp.sum((pad_end[None, :] <= tile_start[:, None]).astype(jnp.int32), axis=1), N_EXPERTS - 1)
    n_used = (pad_end[-1:] // EXPERT_ROWS).astype(jnp.int32)
    pos = ri[:, EXPERT_TOPK:2 * EXPERT_TOPK].reshape(n_assign)
    tables = (run_len.reshape(-1), run_tpos.reshape(-1), run_row.reshape(-1))

    xs_rows = _dispatch(*tables, starts + counts, padded - counts, n_used, pos, xn2_rows, n_rows)

    de = w_down.shape[1]
    bg = b_gate_up[:, 0::2].reshape(N_EXPERTS, 1, de)
    bu = b_gate_up[:, 1::2].reshape(N_EXPERTS, 1, de)
    ys_rows = _experts(block_expert, n_used, xs_rows, w_gate_up, bg, bu, w_down, b_down.reshape(N_EXPERTS, 1, d))

    out = _combine(*tables, pos, rw[:, :EXPERT_TOPK].reshape(n_assign), h1, out_g.reshape(1, d), ys_rows)
    return out.reshape(b, s, d)


def kernel(x, mix_norm_g, w_in, w_attn_out, sgu_ln_g, sgu_ln_b, w_spatial, b_spatial, w_sgu_out, w_mix_out,
           ffn_norm_g, w_router, b_router, w_gate_up, b_gate_up, w_down, b_down, final_norm_g):
    depth = w_in.shape[0]
    assert depth == 1, "the final RMSNorm is fused into the single layer's combine step"
    return _layer(x, mix_norm_g[0], w_in[0], w_attn_out[0], sgu_ln_g[0], sgu_ln_b[0], w_spatial[0],
                  b_spatial[0], w_sgu_out[0], w_mix_out[0], ffn_norm_g[0], w_router[0], b_router[0],
                  w_gate_up[0], b_gate_up[0], w_down[0], b_down[0], final_norm_g)
```

```python
import jax
import jax.numpy as jnp
import numpy as np
from jax import lax
from jax.experimental import pallas as pl
from jax.experimental.pallas import tpu as pltpu

F32 = jnp.float32
BF16 = jnp.bfloat16
NEG_INF = float("-inf")
MASK_VALUE = -1e30


def _bf16_pieces(x, n=3):
    pieces = []
    for _ in range(n):
        p = float(np.asarray(x, np.float32).astype(BF16).astype(np.float32))
        pieces.append(p)
        x = x - p
    return pieces


LOG2E = 1.4426950408889634
LOG2E_PIECES = _bf16_pieces(LOG2E)

N_HEADS = 8
HEAD_DIM = 64
ATTN_WIDTH = N_HEADS * HEAD_DIM
MOBA_BLOCK = 256
MOBA_TOPK = 3
SGU_CHUNK = 128
SGU_GROUPS = 8
N_EXPERTS = 32
EXPERT_TOPK = 4
SWIGLU_LIMIT = 7.0
SWIGLU_ALPHA = 1.702
NORM_EPS = 1e-5

LANES = 128
SUBLANES = 8
HEADS_PER_LANE_TILE = LANES // HEAD_DIM
MOBA_GROUP = 1
MOBA_BIAS_LANE0 = 64
MOBA_DUMMY_LANE = LANES - 1
PROJ_ROWS = 256
EXPERT_ROWS = 512
ROUTE_TOKENS = PROJ_ROWS
VMEM_LIMIT = 48 * 1024 * 1024
EXPERT_VMEM_LIMIT = 56 * 1024 * 1024


def _sigmoid(x):
    return 1.0 / (1.0 + jnp.exp(-x))


def _store_token_rows(ref, value, base=0):
    n = value.shape[0]
    for c in range(value.shape[1] // LANES):
        ref[pl.ds(base + c, n, stride=SUBLANES), :] = value[:, c * LANES:(c + 1) * LANES]


def _load_token_rows(ref, n, base=0):
    return jnp.concatenate([ref[pl.ds(base + c, n, stride=SUBLANES), :] for c in range(SUBLANES)], axis=1)


def _gelu_exact(x):
    return 0.5 * x * (1.0 + lax.erf(x * (0.5 ** 0.5)))


def _in_proj_kernel(x_ref, g_ref, w_ref, lng_ref, lnb_ref,
                    q_ref, qt_ref, kt_ref, v_ref, kmean_ref, u_ref, vvn_ref, ga_ref, gs_ref):
    x = x_ref[...]
    xn = x * lax.rsqrt(jnp.mean(x * x, axis=-1, keepdims=True) + NORM_EPS) * g_ref[...]
    xb = xn.astype(BF16)

    def proj(lo, hi):
        return jnp.dot(xb, w_ref[:, lo:hi], preferred_element_type=F32)

    a = ATTN_WIDTH
    sw = u_ref.shape[1]
    d = ga_ref.shape[1]
    qf = proj(0, a)
    q_ref[...] = qf
    kf = proj(a, 2 * a)
    for j in range(kf.shape[0] // MOBA_BLOCK):
        qt_ref[j] = qf[j * MOBA_BLOCK:(j + 1) * MOBA_BLOCK].T
        kblk = kf[j * MOBA_BLOCK:(j + 1) * MOBA_BLOCK]
        kt_ref[j] = kblk.T.astype(BF16)
        kmean_ref[j] = jnp.mean(kblk, axis=0, keepdims=True)
    v_ref[...] = proj(2 * a, 3 * a).astype(BF16)
    z0 = 3 * a
    u_ref[...] = _gelu_exact(proj(z0, z0 + sw))
    zv = _gelu_exact(proj(z0 + sw, z0 + 2 * sw))
    mu = jnp.mean(zv, axis=-1, keepdims=True)
    zc = zv - mu
    var = jnp.mean(zc * zc, axis=-1, keepdims=True)
    vvn_ref[...] = zc * lax.rsqrt(var + NORM_EPS) * lng_ref[...] + lnb_ref[...]
    g0 = z0 + 2 * sw
    ga_ref[...] = _sigmoid(proj(g0, g0 + d))
    gs_ref[...] = _sigmoid(proj(g0 + d, g0 + 2 * d))


def _in_proj(x2, g, w_bf, lng, lnb, sgu_width):
    t, d = x2.shape
    tm = PROJ_ROWS
    n_in = w_bf.shape[1]
    row = lambda w: pl.BlockSpec((tm, w), lambda i: (i, 0))
    const = lambda shape: pl.BlockSpec(shape, lambda i: (0,) * len(shape))
    out_shape = (
        jax.ShapeDtypeStruct((t, ATTN_WIDTH), F32),
        jax.ShapeDtypeStruct((t // MOBA_BLOCK, ATTN_WIDTH, MOBA_BLOCK), F32),
        jax.ShapeDtypeStruct((t // MOBA_BLOCK, ATTN_WIDTH, MOBA_BLOCK), BF16),
        jax.ShapeDtypeStruct((t, ATTN_WIDTH), BF16),
        jax.ShapeDtypeStruct((t // MOBA_BLOCK, 1, ATTN_WIDTH), F32),
        jax.ShapeDtypeStruct((t, sgu_width), F32),
        jax.ShapeDtypeStruct((t, sgu_width), F32),
        jax.ShapeDtypeStruct((t, d), F32),
        jax.ShapeDtypeStruct((t, d), F32),
    )
    out_specs = (
        row(ATTN_WIDTH),
        pl.BlockSpec((tm // MOBA_BLOCK, ATTN_WIDTH, MOBA_BLOCK), lambda i: (i, 0, 0)),
        pl.BlockSpec((tm // MOBA_BLOCK, ATTN_WIDTH, MOBA_BLOCK), lambda i: (i, 0, 0)),
        row(ATTN_WIDTH),
        pl.BlockSpec((tm // MOBA_BLOCK, 1, ATTN_WIDTH), lambda i: (i, 0, 0)),
        row(sgu_width), row(sgu_width), row(d), row(d),
    )
    return pl.pallas_call(
        _in_proj_kernel,
        grid=(t // tm,),
        in_specs=[row(d), const((1, d)), const((d, n_in)), const((1, sgu_width)), const((1, sgu_width))],
        out_specs=out_specs,
        out_shape=out_shape,
        compiler_params=pltpu.CompilerParams(dimension_semantics=("arbitrary",), vmem_limit_bytes=VMEM_LIMIT),
        name="in_proj",
    )(x2, g, w_bf, lng, lnb)


def _alibi_slopes():
    slopes = 2.0 ** (-8.0 * np.arange(1, N_HEADS + 1, dtype=np.float64) / N_HEADS)
    assert all(np.log2(s) == np.round(np.log2(s)) for s in slopes), "ALiBi slopes must be powers of two"
    return slopes


def _moba_bias_lane(head_in_tile, part, piece):
    return MOBA_BIAS_LANE0 + (head_in_tile * 2 + part) * len(LOG2E_PIECES) + piece


def _moba_key_table(s):
    nb = s // MOBA_BLOCK
    slopes = _alibi_slopes()
    n_pairs = N_HEADS // HEADS_PER_LANE_TILE
    table = np.zeros((nb + 1, n_pairs, LANES, MOBA_BLOCK), np.float32)
    offs = np.arange(MOBA_BLOCK, dtype=np.float32)
    assert HEADS_PER_LANE_TILE * nb <= MOBA_BIAS_LANE0, "block one-hot rows must not overlap the ALiBi rows"
    for j in range(nb):
        for hh in range(HEADS_PER_LANE_TILE):
            table[j, :, hh * nb + j, :] = 1.0
        for p in range(n_pairs):
            for hh in range(HEADS_PER_LANE_TILE):
                slope = slopes[p * HEADS_PER_LANE_TILE + hh]
                for piece in range(len(LOG2E_PIECES)):
                    table[j, p, _moba_bias_lane(hh, 0, piece), :] = slope * MOBA_BLOCK * j
                    table[j, p, _moba_bias_lane(hh, 1, piece), :] = slope * offs
    table[nb, :, MOBA_DUMMY_LANE, :] = 1.0
    as_bf16 = table.astype(BF16)
    assert np.array_equal(as_bf16.astype(np.float32), table), "bias table must be exact in bf16"
    return jnp.asarray(as_bf16.reshape(nb + 1, n_pairs * LANES, MOBA_BLOCK))


def _moba_kernel(q_ref, qt_ref, kt_ref, v_ref, km_ref, ct_ref, causal_ref, o_ref, qaug_ref, s_ref, mpart_ref,
                 mrow_ref, acc_ref):
    i = pl.program_id(2)
    nb = kt_ref.shape[0]
    blk = MOBA_BLOCK
    q = q_ref[...]
    lane = lax.broadcasted_iota(jnp.int32, (1, LANES), 1)
    n_groups = lax.div(i + MOBA_GROUP, MOBA_GROUP)
    heads = range(HEADS_PER_LANE_TILE)
    hmasks = [(lane >= HEAD_DIM * hh) & (lane < HEAD_DIM * (hh + 1)) for hh in heads]

    bid = lax.broadcasted_iota(jnp.int32, (nb, blk), 0)
    bid_f = bid.astype(F32)
    masks_t = []
    for hh in heads:
        km_h = jnp.where(hmasks[hh], km_ref[...], 0.0)
        gate = jnp.dot(km_h, qt_ref[...], precision=lax.Precision.HIGHEST, preferred_element_type=F32)
        gate = jnp.where(bid < i, gate, NEG_INF)
        blockmask = jnp.where(bid == i, 0.0, MASK_VALUE)
        for kk in range(MOBA_TOPK):
            gmax = jnp.max(gate, axis=0, keepdims=True)
            first = jnp.min(jnp.where(gate == gmax, bid_f, float(nb)), axis=0, keepdims=True)
            valid = (i > kk).astype(F32)
            first = first * valid + (valid - 1.0)
            hit = bid_f == first
            blockmask = jnp.where(hit, 0.0, blockmask)
            gate = jnp.where(hit, NEG_INF, gate)
        masks_t.append(blockmask)
    masks_t.append(jnp.zeros((LANES - len(masks_t) * nb, blk), F32))
    masks = jnp.concatenate(masks_t, axis=0).T

    for hh in heads:
        feats = jnp.where(lane == MOBA_DUMMY_LANE, MASK_VALUE, 0.0)
        for part in range(2):
            for piece, value in enumerate(LOG2E_PIECES):
                feats = jnp.where(lane == _moba_bias_lane(hh, part, piece), value, feats)
        extra = jnp.where((lane >= hh * nb) & (lane < (hh + 1) * nb), masks, feats)
        qh = jnp.where(hmasks[hh], q, 0.0)
        qaug_ref[hh * blk:(hh + 1) * blk, 0:LANES] = (qh * (HEAD_DIM ** -0.5 * LOG2E)).astype(BF16)
        qaug_ref[hh * blk:(hh + 1) * blk, LANES:2 * LANES] = extra.astype(BF16)

    def scores(jk, jc):
        rhs = jnp.concatenate([kt_ref[jk], ct_ref[jc]], axis=0)
        return jnp.dot(qaug_ref[...], rhs, preferred_element_type=F32)

    def lane_halves_max(s):
        return jnp.maximum(s[:, 0:LANES], s[:, LANES:2 * LANES])

    n_rows = len(heads) * blk
    variants = [(groups, min(groups * MOBA_GROUP, nb)) for groups in range(1, pl.cdiv(nb, MOBA_GROUP) + 1)]

    for groups, n_blocks in variants:
        @pl.when(n_groups == groups)
        def _(n_blocks=n_blocks):
            part = None
            for j in range(n_blocks):
                s = scores(j, jnp.where(j <= i, j, nb)) + causal_ref[(i == j).astype(jnp.int32)]
                s_ref[j] = s
                part = lane_halves_max(s) if part is None else jnp.maximum(part, lane_halves_max(s))
            mpart_ref[...] = part

    ones = jnp.ones((blk, LANES), BF16)
    rowmax = jnp.max(mpart_ref[...], axis=1, keepdims=True)
    mrow_ref[...] = jnp.broadcast_to(rowmax, (n_rows, 2 * LANES))

    def weighted(j):
        p = jnp.exp2(s_ref[j] - mrow_ref[...]).astype(BF16)
        vaug = jnp.concatenate([v_ref[j * blk:(j + 1) * blk, :], ones], axis=1)
        return jnp.dot(p, vaug, preferred_element_type=F32)

    for groups, n_blocks in variants:
        @pl.when(n_groups == groups)
        def _(n_blocks=n_blocks):
            tot = weighted(0)
            for j in range(1, n_blocks):
                tot = tot + weighted(j)
            acc_ref[...] = tot

    out = jnp.zeros((blk, LANES), F32)
    for hh in heads:
        acc = acc_ref[hh * blk:(hh + 1) * blk, :]
        out = jnp.where(hmasks[hh], acc[:, 0:LANES] / acc[:, LANES:2 * LANES], out)
    o_ref[...] = out


def _moba(q, qt, kt, v, kmean, key_table):
    b, s, a = q.shape
    nb = s // MOBA_BLOCK
    n_pairs = a // LANES
    rows = HEADS_PER_LANE_TILE * MOBA_BLOCK
    row_in_block = np.arange(rows)[:, None] % MOBA_BLOCK
    causal = np.where(row_in_block >= np.arange(MOBA_BLOCK)[None, :], 0.0, MASK_VALUE).astype(np.float32)
    causal_tiles = jnp.asarray(np.stack([np.zeros_like(causal), causal]))
    return pl.pallas_call(
        _moba_kernel,
        grid=(b, n_pairs, nb),
        in_specs=[
            pl.BlockSpec((None, MOBA_BLOCK, LANES), lambda bi, p, i: (bi, i, p)),
            pl.BlockSpec((None, LANES, MOBA_BLOCK), lambda bi, p, i: (bi * nb + i, p, 0)),
            pl.BlockSpec((nb, LANES, MOBA_BLOCK), lambda bi, p, i: (bi, p, 0)),
            pl.BlockSpec((None, s, LANES), lambda bi, p, i: (bi, 0, p)),
            pl.BlockSpec((None, nb, LANES), lambda bi, p, i: (bi, 0, p)),
            pl.BlockSpec((nb + 1, LANES, MOBA_BLOCK), lambda bi, p, i: (0, p, 0)),
            pl.BlockSpec((2, rows, MOBA_BLOCK), lambda bi, p, i: (0, 0, 0)),
        ],
        out_specs=pl.BlockSpec((None, MOBA_BLOCK, LANES), lambda bi, p, i: (bi, i, p)),
        out_shape=jax.ShapeDtypeStruct((b, s, a), F32),
        scratch_shapes=[
            pltpu.VMEM((rows, 2 * LANES), BF16),
            pltpu.VMEM((nb + 1, rows, MOBA_BLOCK), F32),
            pltpu.VMEM((rows, LANES), F32),
            pltpu.VMEM((rows, 2 * LANES), F32),
            pltpu.VMEM((rows, 2 * LANES), F32),
        ],
        compiler_params=pltpu.CompilerParams(
            dimension_semantics=("arbitrary", "arbitrary", "arbitrary"), vmem_limit_bytes=VMEM_LIMIT),
        name="moba",
    )(q, qt, kt, v, kmean, key_table, causal_tiles)


def _mix_kernel(attn_ref, u_ref, vvn_ref, ga_ref, gs_ref, x_ref,
                wao_ref, wso_ref, wmo_ref, wsp_ref, bsp_ref, fg_ref, wr_ref, br_ref,
                h1_ref, xn2_ref, ri_ref, rw_ref, tcnt_ref, wcausal_ref):
    tm = x_ref.shape[0]
    ch = SGU_CHUNK

    @pl.when(pl.program_id(0) == 0)
    def _():
        tril = lax.broadcasted_iota(jnp.int32, (ch, ch), 0) >= lax.broadcasted_iota(jnp.int32, (ch, ch), 1)
        for g in range(SGU_GROUPS):
            wcausal_ref[g] = jnp.where(tril, wsp_ref[g], 0.0).astype(BF16)

    y_attn = jnp.dot(attn_ref[...].astype(BF16), wao_ref[...], preferred_element_type=F32)

    lane = lax.broadcasted_iota(jnp.int32, (1, LANES), 1)
    gdim = vvn_ref.shape[1] // SGU_GROUPS
    groups_per_tile = LANES // gdim
    w_causal = [wcausal_ref[g] for g in range(SGU_GROUPS)]
    rows = []
    for c in range(tm // ch):
        cols = []
        for ct in range(vvn_ref.shape[1] // LANES):
            vp = vvn_ref[c * ch:(c + 1) * ch, ct * LANES:(ct + 1) * LANES]
            acc = jnp.zeros((ch, LANES), F32)
            for gg in range(groups_per_tile):
                gmask = (lane >= gdim * gg) & (lane < gdim * (gg + 1))
                vm = jnp.where(gmask, vp, 0.0).astype(BF16)
                acc = acc + jnp.dot(w_causal[ct * groups_per_tile + gg], vm, preferred_element_type=F32)
            cols.append(acc)
        rows.append(jnp.concatenate(cols, axis=1) + bsp_ref[...])
    mixed = jnp.concatenate(rows, axis=0)
    sgu = u_ref[...] * mixed
    y_sgu = jnp.dot(sgu.astype(BF16), wso_ref[...], preferred_element_type=F32)

    merged = ga_ref[...] * y_attn + gs_ref[...] * y_sgu
    h1 = x_ref[...] + jnp.dot(merged.astype(BF16), wmo_ref[...], preferred_element_type=F32)
    h1_ref[...] = h1
    xn2 = h1 * lax.rsqrt(jnp.mean(h1 * h1, axis=-1, keepdims=True) + NORM_EPS) * fg_ref[...]
    _store_token_rows(xn2_ref, xn2)

    def split(a):
        hi = a.astype(BF16)
        return hi, (a - hi.astype(F32)).astype(BF16)

    x_hi, x_lo = split(xn2)
    w_hi, w_lo = split(wr_ref[...])
    logits = (jnp.dot(x_hi, w_hi, preferred_element_type=F32) + jnp.dot(x_lo, w_hi, preferred_element_type=F32)
              + jnp.dot(x_hi, w_lo, preferred_element_type=F32)) + br_ref[...]
    work = jnp.where(lane < N_EXPERTS, logits, NEG_INF)
    lane_f = lane.astype(F32)
    vals, idxs = [], []
    for _ in range(EXPERT_TOPK):
        vmax = jnp.max(work, axis=1, keepdims=True)
        first = jnp.min(jnp.where(work == vmax, lane_f, float(LANES)), axis=1, keepdims=True)
        vals.append(vmax)
        idxs.append(first)
        work = jnp.where(lane_f == first, NEG_INF, work)
    exps = [jnp.exp(v - vals[0]) for v in vals]
    denom = exps[0]
    for e in exps[1:]:
        denom = denom + e
    chosen = jnp.zeros((tm, LANES), F32)
    for first in idxs:
        chosen = jnp.where(lane_f == first, 1.0, chosen)

    strict_lower = (lax.broadcasted_iota(jnp.int32, (tm, tm), 0)
                    > lax.broadcasted_iota(jnp.int32, (tm, tm), 1))
    before = jnp.dot(jnp.where(strict_lower, 1.0, 0.0).astype(BF16), chosen.astype(BF16),
                     preferred_element_type=F32)
    counts = jnp.broadcast_to(jnp.sum(chosen, axis=0, keepdims=True), tcnt_ref.shape)
    lower_expert = (lax.broadcasted_iota(jnp.int32, (LANES, LANES), 0)
                    < lax.broadcasted_iota(jnp.int32, (LANES, LANES), 1))
    expert_start = jnp.dot(counts.astype(BF16), jnp.where(lower_expert, 1.0, 0.0).astype(BF16),
                           preferred_element_type=F32)[0:1, :]
    sorted_pos = before + expert_start
    ri = jnp.zeros((tm, LANES), jnp.int32)
    rw = jnp.zeros((tm, LANES), F32)
    for kk in range(EXPERT_TOPK):
        pos = jnp.sum(jnp.where(lane_f == idxs[kk], sorted_pos, 0.0), axis=1, keepdims=True)
        ri = jnp.where(lane == kk, idxs[kk].astype(jnp.int32), ri)
        ri = jnp.where(lane == EXPERT_TOPK + kk, pos.astype(jnp.int32), ri)
        rw = jnp.where(lane == kk, exps[kk] / denom, rw)
    ri_ref[...] = ri
    rw_ref[...] = rw
    tcnt_ref[...] = counts


def _mix(attn, u, vvn, ga, gs, x2, wao, wso, wmo, wsp, bsp_full, fg, wr_pad, br_pad):
    t, d = x2.shape
    tm = PROJ_ROWS
    row = lambda w: pl.BlockSpec((tm, w), lambda i: (i, 0))
    const = lambda shape: pl.BlockSpec(shape, lambda i: (0,) * len(shape))
    ins = (attn, u, vvn, ga, gs, x2, wao, wso, wmo, wsp, bsp_full, fg, wr_pad, br_pad)
    in_specs = [row(a.shape[1]) for a in ins[:6]] + [const(a.shape) for a in ins[6:]]
    out_shape = (
        jax.ShapeDtypeStruct((t, d), F32),
        jax.ShapeDtypeStruct((t * SUBLANES, LANES), F32),
        jax.ShapeDtypeStruct((t, LANES), jnp.int32),
        jax.ShapeDtypeStruct((t, LANES), F32),
        jax.ShapeDtypeStruct((t // tm * SUBLANES, LANES), F32),
    )
    assert d == SUBLANES * LANES, "a token row is stored as one (8, 128) tile"
    assert tm == ROUTE_TOKENS
    token_rows = pl.BlockSpec((tm * SUBLANES, LANES), lambda i: (i, 0))
    out_specs = (row(d), token_rows, row(LANES), row(LANES), pl.BlockSpec((SUBLANES, LANES), lambda i: (i, 0)))
    return pl.pallas_call(
        _mix_kernel,
        grid=(t // tm,),
        in_specs=in_specs,
        out_specs=out_specs,
        out_shape=out_shape,
        scratch_shapes=[pltpu.VMEM(wsp.shape, BF16)],
        compiler_params=pltpu.CompilerParams(dimension_semantics=("arbitrary",), vmem_limit_bytes=VMEM_LIMIT),
        name="mix_route",
    )(*ins)


def _run_pieces(n, body):
    off = jnp.int32(0)
    for bit in reversed(range(ROUTE_TOKENS.bit_length())):
        size = 1 << bit
        take = lax.bitwise_and(n, size) != 0

        @pl.when(take)
        def _(off=off, size=size):
            body(off, size)

        off = off + jnp.where(take, size, 0)


def _token_rows(ref, r, n):
    return ref.at[pl.ds(pl.multiple_of(r * SUBLANES, SUBLANES), n * SUBLANES)]


def _dispatch_kernel(len_ref, tpos_ref, dst_ref, padlo_ref, padlen_ref, nused_ref, pos_ref, x_ref,
                     xs_hbm, sorted_ref, zeros_ref, sems, zsem):
    step = pl.program_id(0)
    n_steps = pl.num_programs(0)
    n_assign = pos_ref.shape[0]
    tile_rows = zeros_ref.shape[0] // SUBLANES
    slot = lax.rem(step, 2)

    def start_runs(tile, buf):
        def per_expert(e, carry):
            k = tile * N_EXPERTS + e

            def piece(off, size):
                pltpu.make_async_copy(_token_rows(sorted_ref.at[buf], tpos_ref[k] + off, size),
                                      _token_rows(xs_hbm, dst_ref[k] + off, size), sems.at[buf]).start()

            _run_pieces(len_ref[k], piece)
            return carry

        lax.fori_loop(0, N_EXPERTS, per_expert, 0)

    def wait_runs(buf):
        whole = sorted_ref.at[buf]
        pltpu.make_async_copy(whole, _token_rows(xs_hbm, 0, n_assign), sems.at[buf]).wait()

    @pl.when(step == 0)
    def _():
        zeros_ref[...] = jnp.zeros_like(zeros_ref)

        def tail_tile(tl, carry):
            cp = pltpu.make_async_copy(zeros_ref, _token_rows(xs_hbm, tl * tile_rows, tile_rows), zsem)
            cp.start()
            cp.wait()
            return carry

        lax.fori_loop(nused_ref[0], xs_hbm.shape[0] // (tile_rows * SUBLANES), tail_tile, 0)

        def per_expert(e, carry):
            def piece(off, size):
                cp = pltpu.make_async_copy(_token_rows(zeros_ref, 0, size),
                                           _token_rows(xs_hbm, padlo_ref[e] + off, size), zsem)
                cp.start()
                cp.wait()

            _run_pieces(padlen_ref[e], piece)
            return carry

        lax.fori_loop(0, N_EXPERTS, per_expert, 0)

    @pl.when(step >= 2)
    def _():
        wait_runs(slot)

    def place(a, carry):
        tok = lax.shift_right_logical(a, 2)
        pos = pos_ref[a]
        sorted_ref[slot, pl.ds(pl.multiple_of(pos * SUBLANES, SUBLANES), SUBLANES), :] = (
            x_ref[pl.ds(pl.multiple_of(tok * SUBLANES, SUBLANES), SUBLANES), :])
        return carry

    lax.fori_loop(0, n_assign, place, 0, unroll=8)
    start_runs(step, slot)

    @pl.when(step == n_steps - 1)
    def _():
        @pl.when(step >= 1)
        def _():
            wait_runs(1 - slot)

        wait_runs(slot)


def _dispatch(run_len, run_tpos, run_dst, padlo, padlen, n_used, pos, xn2_rows, n_rows):
    t = xn2_rows.shape[0] // SUBLANES
    n_assign = ROUTE_TOKENS * EXPERT_TOPK
    smem_blk = pl.BlockSpec((n_assign,), lambda c, *_: (c,), memory_space=pltpu.SMEM)
    grid_spec = pltpu.PrefetchScalarGridSpec(
        num_scalar_prefetch=6,
        grid=(t // ROUTE_TOKENS,),
        in_specs=[smem_blk, pl.BlockSpec((ROUTE_TOKENS * SUBLANES, LANES), lambda c, *_: (c, 0))],
        out_specs=pl.BlockSpec(memory_space=pl.ANY),
        scratch_shapes=[
            pltpu.VMEM((2, n_assign * SUBLANES, LANES), F32),
            pltpu.VMEM((EXPERT_ROWS * SUBLANES, LANES), F32),
            pltpu.SemaphoreType.DMA((2,)),
            pltpu.SemaphoreType.DMA(()),
        ],
    )
    return pl.pallas_call(
        _dispatch_kernel,
        grid_spec=grid_spec,
        out_shape=jax.ShapeDtypeStruct((n_rows * SUBLANES, LANES), F32),
        compiler_params=pltpu.CompilerParams(dimension_semantics=("arbitrary",), vmem_limit_bytes=VMEM_LIMIT),
        name="dispatch",
    )(run_len, run_tpos, run_dst, padlo, padlen, n_used, pos, xn2_rows)


def _expert_kernel(be_ref, nused_ref, x_ref, wgu_ref, bg_ref, bu_ref, wd_ref, bd_ref, y_ref, wgu_bf, wd_bf):
    t = pl.program_id(0)
    rows = y_ref.shape[0] // SUBLANES
    de = wd_ref.shape[0]
    grp = 2 * LANES
    active = t < nused_ref[0]
    fresh = jnp.logical_and(
        active, jnp.logical_or(t == 0, be_ref[t] != be_ref[jnp.maximum(t - 1, 0)]))

    @pl.when(fresh)
    def _():
        src = lax.broadcasted_iota(jnp.int32, (grp, grp), 0)
        dst = lax.broadcasted_iota(jnp.int32, (grp, grp), 1)
        wanted = jnp.where(dst < LANES, 2 * dst, 2 * (dst - LANES) + 1)
        perm = jnp.where(src == wanted, 1.0, 0.0).astype(BF16)
        for g in range(wgu_ref.shape[1] // grp):
            cols = slice(g * grp, (g + 1) * grp)
            wgu_bf[:, cols] = jnp.dot(wgu_ref[:, cols].astype(BF16), perm,
                                      preferred_element_type=F32).astype(BF16)
        wd_bf[...] = wd_ref[...].astype(BF16)

    @pl.when(active)
    def _():
        xb = _load_token_rows(x_ref, rows).astype(BF16)
        gu = jnp.dot(xb, wgu_bf[...], preferred_element_type=F32)
        hid = []
        for g in range(de // LANES):
            cols = slice(g * LANES, (g + 1) * LANES)
            gate = gu[:, g * grp:g * grp + LANES] + bg_ref[:, cols]
            up = gu[:, g * grp + LANES:(g + 1) * grp] + bu_ref[:, cols]
            gate = jnp.minimum(gate, SWIGLU_LIMIT)
            up = jnp.clip(up, -SWIGLU_LIMIT, SWIGLU_LIMIT)
            hid.append((gate * _sigmoid(SWIGLU_ALPHA * gate) * (up + 1.0)).astype(BF16))
        y = jnp.dot(jnp.concatenate(hid, axis=1), wd_bf[...], preferred_element_type=F32) + bd_ref[...]
        _store_token_rows(y_ref, y)

    @pl.when(jnp.logical_not(active))
    def _():
        y_ref[...] = jnp.zeros_like(y_ref)


def _experts(block_expert, n_used, xs_rows, wgu, bg, bu, wd, bd):
    n_rows = xs_rows.shape[0] // SUBLANES
    _, d, de2 = wgu.shape
    de = wd.shape[1]
    n_tiles = n_rows // EXPERT_ROWS
    tile_rows = EXPERT_ROWS * SUBLANES
    per_expert = lambda t, be, nu: (be[t], 0, 0)
    grid_spec = pltpu.PrefetchScalarGridSpec(
        num_scalar_prefetch=2,
        grid=(n_tiles,),
        in_specs=[
            pl.BlockSpec((tile_rows, LANES), lambda t, be, nu: (jnp.minimum(t, jnp.maximum(nu[0] - 1, 0)), 0)),
            pl.BlockSpec((None, d, de2), per_expert),
            pl.BlockSpec((None, 1, de), per_expert),
            pl.BlockSpec((None, 1, de), per_expert),
            pl.BlockSpec((None, de, d), per_expert),
            pl.BlockSpec((None, 1, d), per_expert),
        ],
        out_specs=pl.BlockSpec((tile_rows, LANES), lambda t, be, nu: (t, 0)),
        scratch_shapes=[pltpu.VMEM((d, de2), BF16), pltpu.VMEM((de, d), BF16)],
    )
    return pl.pallas_call(
        _expert_kernel,
        grid_spec=grid_spec,
        out_shape=jax.ShapeDtypeStruct((n_rows * SUBLANES, LANES), F32),
        compiler_params=pltpu.CompilerParams(
            dimension_semantics=("arbitrary",), vmem_limit_bytes=EXPERT_VMEM_LIMIT),
        name="experts",
    )(block_expert, n_used, xs_rows, wgu, bg, bu, wd, bd)


def _combine_kernel(len_ref, tpos_ref, src_ref, pos_ref, w_ref, h1_ref, fg_ref, ys_hbm,
                    o_ref, buf_ref, h1rows_ref, outrows_ref, sems):
    step = pl.program_id(0)
    n_steps = pl.num_programs(0)
    tt = h1_ref.shape[0]
    slot = lax.rem(step, 2)

    def start_runs(tile, buf):
        def per_expert(e, carry):
            k = tile * N_EXPERTS + e

            def piece(off, size):
                pltpu.make_async_copy(_token_rows(ys_hbm, src_ref[k] + off, size),
                                      _token_rows(buf_ref.at[buf], tpos_ref[k] + off, size), sems.at[buf]).start()

            _run_pieces(len_ref[k], piece)
            return carry

        lax.fori_loop(0, N_EXPERTS, per_expert, 0)

    def wait_runs(buf):
        whole = buf_ref.at[buf]
        pltpu.make_async_copy(_token_rows(ys_hbm, 0, tt * EXPERT_TOPK), whole, sems.at[buf]).wait()

    @pl.when(step == 0)
    def _():
        start_runs(0, 0)

    @pl.when(step + 1 < n_steps)
    def _():
        start_runs(step + 1, 1 - slot)

    wait_runs(slot)

    _store_token_rows(h1rows_ref, h1_ref[...])

    def one_token(tok, carry):
        acc = h1rows_ref[pl.ds(pl.multiple_of(tok * SUBLANES, SUBLANES), SUBLANES), :]
        for kk in range(EXPERT_TOPK):
            a = tok * EXPERT_TOPK + kk
            pos = pos_ref[a]
            acc = acc + w_ref[a] * buf_ref[slot, pl.ds(pl.multiple_of(pos * SUBLANES, SUBLANES), SUBLANES), :]
        outrows_ref[pl.ds(pl.multiple_of(tok * SUBLANES, SUBLANES), SUBLANES), :] = acc
        return carry

    lax.fori_loop(0, tt, one_token, 0, unroll=4)
    h = _load_token_rows(outrows_ref, tt)
    o_ref[...] = h * lax.rsqrt(jnp.mean(h * h, axis=-1, keepdims=True) + NORM_EPS) * fg_ref[...]


def _combine(run_len, run_tpos, run_src, pos, w_flat, h1, fg, ys_rows):
    t, d = h1.shape
    tt = ROUTE_TOKENS
    n_assign = tt * EXPERT_TOPK
    smem_blk = pl.BlockSpec((n_assign,), lambda c, *_: (c,), memory_space=pltpu.SMEM)
    grid_spec = pltpu.PrefetchScalarGridSpec(
        num_scalar_prefetch=3,
        grid=(t // tt,),
        in_specs=[
            smem_blk, smem_blk,
            pl.BlockSpec((tt, d), lambda c, *_: (c, 0)),
            pl.BlockSpec((1, d), lambda c, *_: (0, 0)),
            pl.BlockSpec(memory_space=pl.ANY),
        ],
        out_specs=pl.BlockSpec((tt, d), lambda c, *_: (c, 0)),
        scratch_shapes=[
            pltpu.VMEM((2, n_assign * SUBLANES, LANES), F32),
            pltpu.VMEM((tt * SUBLANES, LANES), F32),
            pltpu.VMEM((tt * SUBLANES, LANES), F32),
            pltpu.SemaphoreType.DMA((2,)),
        ],
    )
    return pl.pallas_call(
        _combine_kernel,
        grid_spec=grid_spec,
        out_shape=jax.ShapeDtypeStruct((t, d), F32),
        compiler_params=pltpu.CompilerParams(dimension_semantics=("arbitrary",), vmem_limit_bytes=VMEM_LIMIT),
        name="combine",
    )(run_len, run_tpos, run_src, pos, w_flat, h1, fg, ys_rows)


def _layer(h, mix_norm_g, w_in, w_attn_out, sgu_ln_g, sgu_ln_b, w_spatial, b_spatial, w_sgu_out,
           w_mix_out, ffn_norm_g, w_router, b_router, w_gate_up, b_gate_up, w_down, b_down, out_g):
    b, s, d = h.shape
    t = b * s
    sgu_width = sgu_ln_g.shape[0]
    x2 = h.reshape(t, d)

    q, qt, kt, v, kmean, u, vvn, ga, gs = _in_proj(
        x2, mix_norm_g.reshape(1, d), w_in.astype(BF16), sgu_ln_g.reshape(1, sgu_width),
        sgu_ln_b.reshape(1, sgu_width), sgu_width)

    nb = s // MOBA_BLOCK
    attn = _moba(q.reshape(b, s, ATTN_WIDTH), qt, kt, v.reshape(b, s, ATTN_WIDTH),
                 kmean.reshape(b, nb, ATTN_WIDTH), _moba_key_table(s)).reshape(t, ATTN_WIDTH)

    gdim = sgu_width // SGU_GROUPS
    bsp_full = jnp.repeat(b_spatial.T, gdim, axis=1)
    wr_pad = jnp.zeros((d, LANES), F32).at[:, :N_EXPERTS].set(w_router)
    br_pad = jnp.zeros((1, LANES), F32).at[0, :N_EXPERTS].set(b_router)
    h1, xn2_rows, ri, rw, tile_cnt = _mix(
        attn, u, vvn, ga, gs, x2, w_attn_out.astype(BF16), w_sgu_out.astype(BF16), w_mix_out.astype(BF16),
        w_spatial, bsp_full, ffn_norm_g.reshape(1, d), wr_pad, br_pad)

    n_assign = t * EXPERT_TOPK
    n_tiles = -(-(n_assign + N_EXPERTS * (EXPERT_ROWS - 1)) // EXPERT_ROWS)
    n_rows = n_tiles * EXPERT_ROWS
    run_len = tile_cnt[::SUBLANES, :N_EXPERTS].astype(jnp.int32)
    counts = jnp.sum(run_len, axis=0)
    padded = (counts + EXPERT_ROWS - 1) // EXPERT_ROWS * EXPERT_ROWS
    pad_end = jnp.cumsum(padded)
    starts = pad_end - padded
    run_tpos = jnp.cumsum(run_len, axis=1) - run_len
    run_row = starts[None, :] + jnp.cumsum(run_len, axis=0) - run_len
    tile_start = jnp.arange(n_tiles, dtype=jnp.int32) * EXPERT_ROWS
    block_expert = jnp.minimum(
        jnp.sum((pad_end[None, :] <= tile_start[:, None]).astype(jnp.int32), axis=1), N_EXPERTS - 1)
    n_used = (pad_end[-1:] // EXPERT_ROWS).astype(jnp.int32)
    pos = ri[:, EXPERT_TOPK:2 * EXPERT_TOPK].reshape(n_assign)
    tables = (run_len.reshape(-1), run_tpos.reshape(-1), run_row.reshape(-1))

    xs_rows = _dispatch(*tables, starts + counts, padded - counts, n_used, pos, xn2_rows, n_rows)

    de = w_down.shape[1]
    bg = b_gate_up[:, 0::2].reshape(N_EXPERTS, 1, de)
    bu = b_gate_up[:, 1::2].reshape(N_EXPERTS, 1, de)
    ys_rows = _experts(block_expert, n_used, xs_rows, w_gate_up, bg, bu, w_down, b_down.reshape(N_EXPERTS, 1, d))

    out = _combine(*tables, pos, rw[:, :EXPERT_TOPK].reshape(n_assign), h1, out_g.reshape(1, d), ys_rows)
    return out.reshape(b, s, d)


def kernel(x, mix_norm_g, w_in, w_attn_out, sgu_ln_g, sgu_ln_b, w_spatial, b_spatial, w_sgu_out, w_mix_out,
           ffn_norm_g, w_router, b_router, w_gate_up, b_gate_up, w_down, b_down, final_norm_g):
    depth = w_in.shape[0]
    assert depth == 1, "the final RMSNorm is fused into the single layer's combine step"
    return _layer(x, mix_norm_g[0], w_in[0], w_attn_out[0], sgu_ln_g[0], sgu_ln_b[0], w_spatial[0],
                  b_spatial[0], w_sgu_out[0], w_mix_out[0], ffn_norm_g[0], w_router[0], b_router[0],
                  w_gate_up[0], b_gate_up[0], w_down[0], b_down[0], final_norm_g)
```

```python
import jax
import jax.numpy as jnp
import numpy as np
from jax import lax
from jax.experimental import pallas as pl
from jax.experimental.pallas import tpu as pltpu

F32 = jnp.float32
BF16 = jnp.bfloat16
NEG_INF = float("-inf")
MASK_VALUE = -1e30


def _bf16_pieces(x, n=3):
    pieces = []
    for _ in range(n):
        p = float(np.asarray(x, np.float32).astype(BF16).astype(np.float32))
        pieces.append(p)
        x = x - p
    return pieces


LOG2E = 1.4426950408889634
LOG2E_PIECES = _bf16_pieces(LOG2E)

N_HEADS = 8
HEAD_DIM = 64
ATTN_WIDTH = N_HEADS * HEAD_DIM
MOBA_BLOCK = 256
MOBA_TOPK = 3
SGU_CHUNK = 128
SGU_GROUPS = 8
N_EXPERTS = 32
EXPERT_TOPK = 4
SWIGLU_LIMIT = 7.0
SWIGLU_ALPHA = 1.702
NORM_EPS = 1e-5

LANES = 128
SUBLANES = 8
HEADS_PER_LANE_TILE = LANES // HEAD_DIM
MOBA_GROUP = 1
MOBA_BIAS_LANE0 = 64
MOBA_DUMMY_LANE = LANES - 1
PROJ_ROWS = 256
EXPERT_ROWS = 512
ROUTE_TOKENS = PROJ_ROWS
VMEM_LIMIT = 48 * 1024 * 1024
EXPERT_VMEM_LIMIT = 56 * 1024 * 1024


def _sigmoid(x):
    return 1.0 / (1.0 + jnp.exp(-x))


def _store_token_rows(ref, value, base=0):
    n = value.shape[0]
    for c in range(value.shape[1] // LANES):
        ref[pl.ds(base + c, n, stride=SUBLANES), :] = value[:, c * LANES:(c + 1) * LANES]


def _load_token_rows(ref, n, base=0):
    return jnp.concatenate([ref[pl.ds(base + c, n, stride=SUBLANES), :] for c in range(SUBLANES)], axis=1)


def _gelu_exact(x):
    return 0.5 * x * (1.0 + lax.erf(x * (0.5 ** 0.5)))


def _in_proj_kernel(x_ref, g_ref, w_ref, lng_ref, lnb_ref,
                    q_ref, qt_ref, kt_ref, v_ref, kmean_ref, u_ref, vvn_ref, ga_ref, gs_ref):
    x = x_ref[...]
    xn = x * lax.rsqrt(jnp.mean(x * x, axis=-1, keepdims=True) + NORM_EPS) * g_ref[...]
    xb = xn.astype(BF16)

    def proj(lo, hi):
        return jnp.dot(xb, w_ref[:, lo:hi], preferred_element_type=F32)

    a = ATTN_WIDTH
    sw = u_ref.shape[1]
    d = ga_ref.shape[1]
    qf = proj(0, a)
    q_ref[...] = qf
    kf = proj(a, 2 * a)
    for j in range(kf.shape[0] // MOBA_BLOCK):
        qt_ref[j] = qf[j * MOBA_BLOCK:(j + 1) * MOBA_BLOCK].T
        kblk = kf[j * MOBA_BLOCK:(j + 1) * MOBA_BLOCK]
        kt_ref[j] = kblk.T.astype(BF16)
        kmean_ref[j] = jnp.mean(kblk, axis=0, keepdims=True)
    v_ref[...] = proj(2 * a, 3 * a).astype(BF16)
    z0 = 3 * a
    u_ref[...] = _gelu_exact(proj(z0, z0 + sw))
    zv = _gelu_exact(proj(z0 + sw, z0 + 2 * sw))
    mu = jnp.mean(zv, axis=-1, keepdims=True)
    zc = zv - mu
    var = jnp.mean(zc * zc, axis=-1, keepdims=True)
    vvn_ref[...] = zc * lax.rsqrt(var + NORM_EPS) * lng_ref[...] + lnb_ref[...]
    g0 = z0 + 2 * sw
    ga_ref[...] = _sigmoid(proj(g0, g0 + d))
    gs_ref[...] = _sigmoid(proj(g0 + d, g0 + 2 * d))


def _in_proj(x2, g, w_bf, lng, lnb, sgu_width):
    t, d = x2.shape
    tm = PROJ_ROWS
    n_in = w_bf.shape[1]
    row = lambda w: pl.BlockSpec((tm, w), lambda i: (i, 0))
    const = lambda shape: pl.BlockSpec(shape, lambda i: (0,) * len(shape))
    out_shape = (
        jax.ShapeDtypeStruct((t, ATTN_WIDTH), F32),
        jax.ShapeDtypeStruct((t // MOBA_BLOCK, ATTN_WIDTH, MOBA_BLOCK), F32),
        jax.ShapeDtypeStruct((t // MOBA_BLOCK, ATTN_WIDTH, MOBA_BLOCK), BF16),
        jax.ShapeDtypeStruct((t, ATTN_WIDTH), BF16),
        jax.ShapeDtypeStruct((t // MOBA_BLOCK, 1, ATTN_WIDTH), F32),
        jax.ShapeDtypeStruct((t, sgu_width), F32),
        jax.ShapeDtypeStruct((t, sgu_width), F32),
        jax.ShapeDtypeStruct((t, d), F32),
        jax.ShapeDtypeStruct((t, d), F32),
    )
    out_specs = (
        row(ATTN_WIDTH),
        pl.BlockSpec((tm // MOBA_BLOCK, ATTN_WIDTH, MOBA_BLOCK), lambda i: (i, 0, 0)),
        pl.BlockSpec((tm // MOBA_BLOCK, ATTN_WIDTH, MOBA_BLOCK), lambda i: (i, 0, 0)),
        row(ATTN_WIDTH),
        pl.BlockSpec((tm // MOBA_BLOCK, 1, ATTN_WIDTH), lambda i: (i, 0, 0)),
        row(sgu_width), row(sgu_width), row(d), row(d),
    )
    return pl.pallas_call(
        _in_proj_kernel,
        grid=(t // tm,),
        in_specs=[row(d), const((1, d)), const((d, n_in)), const((1, sgu_width)), const((1, sgu_width))],
        out_specs=out_specs,
        out_shape=out_shape,
        compiler_params=pltpu.CompilerParams(dimension_semantics=("arbitrary",), vmem_limit_bytes=VMEM_LIMIT),
        name="in_proj",
    )(x2, g, w_bf, lng, lnb)


def _alibi_slopes():
    slopes = 2.0 ** (-8.0 * np.arange(1, N_HEADS + 1, dtype=np.float64) / N_HEADS)
    assert all(np.log2(s) == np.round(np.log2(s)) for s in slopes), "ALiBi slopes must be powers of two"
    return slopes


def _moba_bias_lane(head_in_tile, part, piece):
    return MOBA_BIAS_LANE0 + (head_in_tile * 2 + part) * len(LOG2E_PIECES) + piece


def _moba_key_table(s):
    nb = s // MOBA_BLOCK
    slopes = _alibi_slopes()
    n_pairs = N_HEADS // HEADS_PER_LANE_TILE
    table = np.zeros((nb + 1, n_pairs, LANES, MOBA_BLOCK), np.float32)
    offs = np.arange(MOBA_BLOCK, dtype=np.float32)
    assert HEADS_PER_LANE_TILE * nb <= MOBA_BIAS_LANE0, "block one-hot rows must not overlap the ALiBi rows"
    for j in range(nb):
        for hh in range(HEADS_PER_LANE_TILE):
            table[j, :, hh * nb + j, :] = 1.0
        for p in range(n_pairs):
            for hh in range(HEADS_PER_LANE_TILE):
                slope = slopes[p * HEADS_PER_LANE_TILE + hh]
                for piece in range(len(LOG2E_PIECES)):
                    table[j, p, _moba_bias_lane(hh, 0, piece), :] = slope * MOBA_BLOCK * j
                    table[j, p, _moba_bias_lane(hh, 1, piece), :] = slope * offs
    table[nb, :, MOBA_DUMMY_LANE, :] = 1.0
    as_bf16 = table.astype(BF16)
    assert np.array_equal(as_bf16.astype(np.float32), table), "bias table must be exact in bf16"
    return jnp.asarray(as_bf16.reshape(nb + 1, n_pairs * LANES, MOBA_BLOCK))


def _moba_kernel(q_ref, qt_ref, qn_ref, qtn_ref, kt_ref, v_ref, km_ref, ct_ref, causal_ref, o_ref, qaug_ref,
                 s_ref, mpart_ref, mrow_ref, acc_ref):
    i = pl.program_id(2)
    nb = kt_ref.shape[0]
    blk = MOBA_BLOCK
    lane = lax.broadcasted_iota(jnp.int32, (1, LANES), 1)
    n_groups = lax.div(i + MOBA_GROUP, MOBA_GROUP)
    heads = range(HEADS_PER_LANE_TILE)
    hmasks = [(lane >= HEAD_DIM * hh) & (lane < HEAD_DIM * (hh + 1)) for hh in heads]
    slot = lax.rem(i, 2)

    def build_query_operand(q, qt, tile, into):
        bid = lax.broadcasted_iota(jnp.int32, (nb, blk), 0)
        bid_f = bid.astype(F32)
        masks_t = []
        for hh in heads:
            km_h = jnp.where(hmasks[hh], km_ref[...], 0.0)
            gate = jnp.dot(km_h, qt, precision=lax.Precision.HIGHEST, preferred_element_type=F32)
            gate = jnp.where(bid < tile, gate, NEG_INF)
            blockmask = jnp.where(bid == tile, 0.0, MASK_VALUE)
            for kk in range(MOBA_TOPK):
                gmax = jnp.max(gate, axis=0, keepdims=True)
                first = jnp.min(jnp.where(gate == gmax, bid_f, float(nb)), axis=0, keepdims=True)
                valid = (tile > kk).astype(F32)
                first = first * valid + (valid - 1.0)
                hit = bid_f == first
                blockmask = jnp.where(hit, 0.0, blockmask)
                gate = jnp.where(hit, NEG_INF, gate)
            masks_t.append(blockmask)
        masks_t.append(jnp.zeros((LANES - len(masks_t) * nb, blk), F32))
        masks = jnp.concatenate(masks_t, axis=0).T

        for hh in heads:
            feats = jnp.where(lane == MOBA_DUMMY_LANE, MASK_VALUE, 0.0)
            for part in range(2):
                for piece, value in enumerate(LOG2E_PIECES):
                    feats = jnp.where(lane == _moba_bias_lane(hh, part, piece), value, feats)
            extra = jnp.where((lane >= hh * nb) & (lane < (hh + 1) * nb), masks, feats)
            qh = jnp.where(hmasks[hh], q, 0.0)
            qaug_ref[into, hh * blk:(hh + 1) * blk, 0:LANES] = (qh * (HEAD_DIM ** -0.5 * LOG2E)).astype(BF16)
            qaug_ref[into, hh * blk:(hh + 1) * blk, LANES:2 * LANES] = extra.astype(BF16)

    @pl.when(i == 0)
    def _():
        build_query_operand(q_ref[...], qt_ref[...], i, slot)

    def scores(jk, jc):
        rhs = jnp.concatenate([kt_ref[jk], ct_ref[jc]], axis=0)
        return jnp.dot(qaug_ref[slot], rhs, preferred_element_type=F32)

    def lane_halves_max(s):
        return jnp.maximum(s[:, 0:LANES], s[:, LANES:2 * LANES])

    n_rows = len(heads) * blk
    variants = [(groups, min(groups * MOBA_GROUP, nb)) for groups in range(1, pl.cdiv(nb, MOBA_GROUP) + 1)]

    for groups, n_blocks in variants:
        @pl.when(n_groups == groups)
        def _(n_blocks=n_blocks):
            part = None
            for j in range(n_blocks):
                s = scores(j, jnp.where(j <= i, j, nb)) + causal_ref[(i == j).astype(jnp.int32)]
                s_ref[j] = s
                part = lane_halves_max(s) if part is None else jnp.maximum(part, lane_halves_max(s))
            mpart_ref[...] = part

    ones = jnp.ones((blk, LANES), BF16)
    rowmax = jnp.max(mpart_ref[...], axis=1, keepdims=True)
    mrow_ref[...] = jnp.broadcast_to(rowmax, (n_rows, 2 * LANES))

    def weighted(j):
        p = jnp.exp2(s_ref[j] - mrow_ref[...]).astype(BF16)
        vaug = jnp.concatenate([v_ref[j * blk:(j + 1) * blk, :], ones], axis=1)
        return jnp.dot(p, vaug, preferred_element_type=F32)

    next_tile = jnp.minimum(i + 1, nb - 1)
    for groups, n_blocks in variants:
        @pl.when(n_groups == groups)
        def _(n_blocks=n_blocks):
            build_query_operand(qn_ref[...], qtn_ref[...], next_tile, 1 - slot)
            tot = weighted(0)
            for j in range(1, n_blocks):
                tot = tot + weighted(j)
            acc_ref[...] = tot

    out = jnp.zeros((blk, LANES), F32)
    for hh in heads:
        acc = acc_ref[hh * blk:(hh + 1) * blk, :]
        out = jnp.where(hmasks[hh], acc[:, 0:LANES] / acc[:, LANES:2 * LANES], out)
    o_ref[...] = out


def _moba(q, qt, kt, v, kmean, key_table):
    b, s, a = q.shape
    nb = s // MOBA_BLOCK
    n_pairs = a // LANES
    rows = HEADS_PER_LANE_TILE * MOBA_BLOCK
    row_in_block = np.arange(rows)[:, None] % MOBA_BLOCK
    causal = np.where(row_in_block >= np.arange(MOBA_BLOCK)[None, :], 0.0, MASK_VALUE).astype(np.float32)
    causal_tiles = jnp.asarray(np.stack([np.zeros_like(causal), causal]))
    nxt = lambda i: jnp.minimum(i + 1, nb - 1)
    return pl.pallas_call(
        _moba_kernel,
        grid=(b, n_pairs, nb),
        in_specs=[
            pl.BlockSpec((None, MOBA_BLOCK, LANES), lambda bi, p, i: (bi, i, p)),
            pl.BlockSpec((None, LANES, MOBA_BLOCK), lambda bi, p, i: (bi * nb + i, p, 0)),
            pl.BlockSpec((None, MOBA_BLOCK, LANES), lambda bi, p, i: (bi, nxt(i), p)),
            pl.BlockSpec((None, LANES, MOBA_BLOCK), lambda bi, p, i: (bi * nb + nxt(i), p, 0)),
            pl.BlockSpec((nb, LANES, MOBA_BLOCK), lambda bi, p, i: (bi, p, 0)),
            pl.BlockSpec((None, s, LANES), lambda bi, p, i: (bi, 0, p)),
            pl.BlockSpec((None, nb, LANES), lambda bi, p, i: (bi, 0, p)),
            pl.BlockSpec((nb + 1, LANES, MOBA_BLOCK), lambda bi, p, i: (0, p, 0)),
            pl.BlockSpec((2, rows, MOBA_BLOCK), lambda bi, p, i: (0, 0, 0)),
        ],
        out_specs=pl.BlockSpec((None, MOBA_BLOCK, LANES), lambda bi, p, i: (bi, i, p)),
        out_shape=jax.ShapeDtypeStruct((b, s, a), F32),
        scratch_shapes=[
            pltpu.VMEM((2, rows, 2 * LANES), BF16),
            pltpu.VMEM((nb + 1, rows, MOBA_BLOCK), F32),
            pltpu.VMEM((rows, LANES), F32),
            pltpu.VMEM((rows, 2 * LANES), F32),
            pltpu.VMEM((rows, 2 * LANES), F32),
        ],
        compiler_params=pltpu.CompilerParams(
            dimension_semantics=("arbitrary", "arbitrary", "arbitrary"), vmem_limit_bytes=VMEM_LIMIT),
        name="moba",
    )(q, qt, q, qt, kt, v, kmean, key_table, causal_tiles)


def _mix_kernel(attn_ref, u_ref, vvn_ref, ga_ref, gs_ref, x_ref,
                wao_ref, wso_ref, wmo_ref, wsp_ref, bsp_ref, fg_ref, wr_ref, br_ref,
                h1_ref, xn2_ref, ri_ref, rw_ref, tcnt_ref, wcausal_ref):
    tm = x_ref.shape[0]
    ch = SGU_CHUNK

    @pl.when(pl.program_id(0) == 0)
    def _():
        tril = lax.broadcasted_iota(jnp.int32, (ch, ch), 0) >= lax.broadcasted_iota(jnp.int32, (ch, ch), 1)
        for g in range(SGU_GROUPS):
            wcausal_ref[g] = jnp.where(tril, wsp_ref[g], 0.0).astype(BF16)

    y_attn = jnp.dot(attn_ref[...].astype(BF16), wao_ref[...], preferred_element_type=F32)

    lane = lax.broadcasted_iota(jnp.int32, (1, LANES), 1)
    gdim = vvn_ref.shape[1] // SGU_GROUPS
    groups_per_tile = LANES // gdim
    w_causal = [wcausal_ref[g] for g in range(SGU_GROUPS)]
    rows = []
    for c in range(tm // ch):
        cols = []
        for ct in range(vvn_ref.shape[1] // LANES):
            vp = vvn_ref[c * ch:(c + 1) * ch, ct * LANES:(ct + 1) * LANES]
            acc = jnp.zeros((ch, LANES), F32)
            for gg in range(groups_per_tile):
                gmask = (lane >= gdim * gg) & (lane < gdim * (gg + 1))
                vm = jnp.where(gmask, vp, 0.0).astype(BF16)
                acc = acc + jnp.dot(w_causal[ct * groups_per_tile + gg], vm, preferred_element_type=F32)
            cols.append(acc)
        rows.append(jnp.concatenate(cols, axis=1) + bsp_ref[...])
    mixed = jnp.concatenate(rows, axis=0)
    sgu = u_ref[...] * mixed
    y_sgu = jnp.dot(sgu.astype(BF16), wso_ref[...], preferred_element_type=F32)

    merged = ga_ref[...] * y_attn + gs_ref[...] * y_sgu
    h1 = x_ref[...] + jnp.dot(merged.astype(BF16), wmo_ref[...], preferred_element_type=F32)
    h1_ref[...] = h1
    xn2 = h1 * lax.rsqrt(jnp.mean(h1 * h1, axis=-1, keepdims=True) + NORM_EPS) * fg_ref[...]
    _store_token_rows(xn2_ref, xn2)

    def split(a):
        hi = a.astype(BF16)
        return hi, (a - hi.astype(F32)).astype(BF16)

    x_hi, x_lo = split(xn2)
    w_hi, w_lo = split(wr_ref[...])
    logits = (jnp.dot(x_hi, w_hi, preferred_element_type=F32) + jnp.dot(x_lo, w_hi, preferred_element_type=F32)
              + jnp.dot(x_hi, w_lo, preferred_element_type=F32)) + br_ref[...]
    work = jnp.where(lane < N_EXPERTS, logits, NEG_INF)
    lane_f = lane.astype(F32)
    vals, idxs = [], []
    for _ in range(EXPERT_TOPK):
        vmax = jnp.max(work, axis=1, keepdims=True)
        first = jnp.min(jnp.where(work == vmax, lane_f, float(LANES)), axis=1, keepdims=True)
        vals.append(vmax)
        idxs.append(first)
        work = jnp.where(lane_f == first, NEG_INF, work)
    exps = [jnp.exp(v - vals[0]) for v in vals]
    denom = exps[0]
    for e in exps[1:]:
        denom = denom + e
    chosen = jnp.zeros((tm, LANES), F32)
    for first in idxs:
        chosen = jnp.where(lane_f == first, 1.0, chosen)

    strict_lower = (lax.broadcasted_iota(jnp.int32, (tm, tm), 0)
                    > lax.broadcasted_iota(jnp.int32, (tm, tm), 1))
    before = jnp.dot(jnp.where(strict_lower, 1.0, 0.0).astype(BF16), chosen.astype(BF16),
                     preferred_element_type=F32)
    counts = jnp.broadcast_to(jnp.sum(chosen, axis=0, keepdims=True), tcnt_ref.shape)
    lower_expert = (lax.broadcasted_iota(jnp.int32, (LANES, LANES), 0)
                    < lax.broadcasted_iota(jnp.int32, (LANES, LANES), 1))
    expert_start = jnp.dot(counts.astype(BF16), jnp.where(lower_expert, 1.0, 0.0).astype(BF16),
                           preferred_element_type=F32)[0:1, :]
    sorted_pos = before + expert_start
    ri = jnp.zeros((tm, LANES), jnp.int32)
    rw = jnp.zeros((tm, LANES), F32)
    for kk in range(EXPERT_TOPK):
        pos = jnp.sum(jnp.where(lane_f == idxs[kk], sorted_pos, 0.0), axis=1, keepdims=True)
        ri = jnp.where(lane == kk, idxs[kk].astype(jnp.int32), ri)
        ri = jnp.where(lane == EXPERT_TOPK + kk, pos.astype(jnp.int32), ri)
        rw = jnp.where(lane == kk, exps[kk] / denom, rw)
    ri_ref[...] = ri
    rw_ref[...] = rw
    tcnt_ref[...] = counts


def _mix(attn, u, vvn, ga, gs, x2, wao, wso, wmo, wsp, bsp_full, fg, wr_pad, br_pad):
    t, d = x2.shape
    tm = PROJ_ROWS
    row = lambda w: pl.BlockSpec((tm, w), lambda i: (i, 0))
    const = lambda shape: pl.BlockSpec(shape, lambda i: (0,) * len(shape))
    ins = (attn, u, vvn, ga, gs, x2, wao, wso, wmo, wsp, bsp_full, fg, wr_pad, br_pad)
    in_specs = [row(a.shape[1]) for a in ins[:6]] + [const(a.shape) for a in ins[6:]]
    out_shape = (
        jax.ShapeDtypeStruct((t, d), F32),
        jax.ShapeDtypeStruct((t * SUBLANES, LANES), F32),
        jax.ShapeDtypeStruct((t, LANES), jnp.int32),
        jax.ShapeDtypeStruct((t, LANES), F32),
        jax.ShapeDtypeStruct((t // tm * SUBLANES, LANES), F32),
    )
    assert d == SUBLANES * LANES, "a token row is stored as one (8, 128) tile"
    assert tm == ROUTE_TOKENS
    token_rows = pl.BlockSpec((tm * SUBLANES, LANES), lambda i: (i, 0))
    out_specs = (row(d), token_rows, row(LANES), row(LANES), pl.BlockSpec((SUBLANES, LANES), lambda i: (i, 0)))
    return pl.pallas_call(
        _mix_kernel,
        grid=(t // tm,),
        in_specs=in_specs,
        out_specs=out_specs,
        out_shape=out_shape,
        scratch_shapes=[pltpu.VMEM(wsp.shape, BF16)],
        compiler_params=pltpu.CompilerParams(dimension_semantics=("arbitrary",), vmem_limit_bytes=VMEM_LIMIT),
        name="mix_route",
    )(*ins)


def _run_pieces(n, body):
    off = jnp.int32(0)
    for bit in reversed(range(ROUTE_TOKENS.bit_length())):
        size = 1 << bit
        take = lax.bitwise_and(n, size) != 0

        @pl.when(take)
        def _(off=off, size=size):
            body(off, size)

        off = off + jnp.where(take, size, 0)


def _token_rows(ref, r, n):
    return ref.at[pl.ds(pl.multiple_of(r * SUBLANES, SUBLANES), n * SUBLANES)]


def _dispatch_kernel(len_ref, tpos_ref, dst_ref, padlo_ref, padlen_ref, nused_ref, pos_ref, x_ref,
                     xs_hbm, sorted_ref, zeros_ref, sems, zsem):
    step = pl.program_id(0)
    n_steps = pl.num_programs(0)
    n_assign = pos_ref.shape[0]
    tile_rows = zeros_ref.shape[0] // SUBLANES
    slot = lax.rem(step, 2)

    def start_runs(tile, buf):
        def per_expert(e, carry):
            k = tile * N_EXPERTS + e

            def piece(off, size):
                pltpu.make_async_copy(_token_rows(sorted_ref.at[buf], tpos_ref[k] + off, size),
                                      _token_rows(xs_hbm, dst_ref[k] + off, size), sems.at[buf]).start()

            _run_pieces(len_ref[k], piece)
            return carry

        lax.fori_loop(0, N_EXPERTS, per_expert, 0)

    def wait_runs(buf):
        whole = sorted_ref.at[buf]
        pltpu.make_async_copy(whole, _token_rows(xs_hbm, 0, n_assign), sems.at[buf]).wait()

    @pl.when(step == 0)
    def _():
        zeros_ref[...] = jnp.zeros_like(zeros_ref)

        def tail_tile(tl, carry):
            cp = pltpu.make_async_copy(zeros_ref, _token_rows(xs_hbm, tl * tile_rows, tile_rows), zsem)
            cp.start()
            cp.wait()
            return carry

        lax.fori_loop(nused_ref[0], xs_hbm.shape[0] // (tile_rows * SUBLANES), tail_tile, 0)

        def per_expert(e, carry):
            def piece(off, size):
                cp = pltpu.make_async_copy(_token_rows(zeros_ref, 0, size),
                                           _token_rows(xs_hbm, padlo_ref[e] + off, size), zsem)
                cp.start()
                cp.wait()

            _run_pieces(padlen_ref[e], piece)
            return carry

        lax.fori_loop(0, N_EXPERTS, per_expert, 0)

    @pl.when(step >= 2)
    def _():
        wait_runs(slot)

    def place(a, carry):
        tok = lax.shift_right_logical(a, 2)
        pos = pos_ref[a]
        sorted_ref[slot, pl.ds(pl.multiple_of(pos * SUBLANES, SUBLANES), SUBLANES), :] = (
            x_ref[pl.ds(pl.multiple_of(tok * SUBLANES, SUBLANES), SUBLANES), :])
        return carry

    lax.fori_loop(0, n_assign, place, 0, unroll=8)
    start_runs(step, slot)

    @pl.when(step == n_steps - 1)
    def _():
        @pl.when(step >= 1)
        def _():
            wait_runs(1 - slot)

        wait_runs(slot)


def _dispatch(run_len, run_tpos, run_dst, padlo, padlen, n_used, pos, xn2_rows, n_rows):
    t = xn2_rows.shape[0] // SUBLANES
    n_assign = ROUTE_TOKENS * EXPERT_TOPK
    smem_blk = pl.BlockSpec((n_assign,), lambda c, *_: (c,), memory_space=pltpu.SMEM)
    grid_spec = pltpu.PrefetchScalarGridSpec(
        num_scalar_prefetch=6,
        grid=(t // ROUTE_TOKENS,),
        in_specs=[smem_blk, pl.BlockSpec((ROUTE_TOKENS * SUBLANES, LANES), lambda c, *_: (c, 0))],
        out_specs=pl.BlockSpec(memory_space=pl.ANY),
        scratch_shapes=[
            pltpu.VMEM((2, n_assign * SUBLANES, LANES), F32),
            pltpu.VMEM((EXPERT_ROWS * SUBLANES, LANES), F32),
            pltpu.SemaphoreType.DMA((2,)),
            pltpu.SemaphoreType.DMA(()),
        ],
    )
    return pl.pallas_call(
        _dispatch_kernel,
        grid_spec=grid_spec,
        out_shape=jax.ShapeDtypeStruct((n_rows * SUBLANES, LANES), F32),
        compiler_params=pltpu.CompilerParams(dimension_semantics=("arbitrary",), vmem_limit_bytes=VMEM_LIMIT),
        name="dispatch",
    )(run_len, run_tpos, run_dst, padlo, padlen, n_used, pos, xn2_rows)


def _expert_kernel(be_ref, nused_ref, x_ref, wgu_ref, bg_ref, bu_ref, wd_ref, bd_ref, y_ref, wgu_bf, wd_bf):
    t = pl.program_id(0)
    rows = y_ref.shape[0] // SUBLANES
    de = wd_ref.shape[0]
    grp = 2 * LANES
    active = t < nused_ref[0]
    fresh = jnp.logical_and(
        active, jnp.logical_or(t == 0, be_ref[t] != be_ref[jnp.maximum(t - 1, 0)]))

    @pl.when(fresh)
    def _():
        src = lax.broadcasted_iota(jnp.int32, (grp, grp), 0)
        dst = lax.broadcasted_iota(jnp.int32, (grp, grp), 1)
        wanted = jnp.where(dst < LANES, 2 * dst, 2 * (dst - LANES) + 1)
        perm = jnp.where(src == wanted, 1.0, 0.0).astype(BF16)
        for g in range(wgu_ref.shape[1] // grp):
            cols = slice(g * grp, (g + 1) * grp)
            wgu_bf[:, cols] = jnp.dot(wgu_ref[:, cols].astype(BF16), perm,
                                      preferred_element_type=F32).astype(BF16)
        wd_bf[...] = wd_ref[...].astype(BF16)

    @pl.when(active)
    def _():
        xb = _load_token_rows(x_ref, rows).astype(BF16)
        gu = jnp.dot(xb, wgu_bf[...], preferred_element_type=F32)
        hid = []
        for g in range(de // LANES):
            cols = slice(g * LANES, (g + 1) * LANES)
            gate = gu[:, g * grp:g * grp + LANES] + bg_ref[:, cols]
            up = gu[:, g * grp + LANES:(g + 1) * grp] + bu_ref[:, cols]
            gate = jnp.minimum(gate, SWIGLU_LIMIT)
            up = jnp.clip(up, -SWIGLU_LIMIT, SWIGLU_LIMIT)
            hid.append((gate * _sigmoid(SWIGLU_ALPHA * gate) * (up + 1.0)).astype(BF16))
        y = jnp.dot(jnp.concatenate(hid, axis=1), wd_bf[...], preferred_element_type=F32) + bd_ref[...]
        _store_token_rows(y_ref, y)

    @pl.when(jnp.logical_not(active))
    def _():
        y_ref[...] = jnp.zeros_like(y_ref)


def _experts(block_expert, n_used, xs_rows, wgu, bg, bu, wd, bd):
    n_rows = xs_rows.shape[0] // SUBLANES
    _, d, de2 = wgu.shape
    de = wd.shape[1]
    n_tiles = n_rows // EXPERT_ROWS
    tile_rows = EXPERT_ROWS * SUBLANES
    per_expert = lambda t, be, nu: (be[t], 0, 0)
    grid_spec = pltpu.PrefetchScalarGridSpec(
        num_scalar_prefetch=2,
        grid=(n_tiles,),
        in_specs=[
            pl.BlockSpec((tile_rows, LANES), lambda t, be, nu: (jnp.minimum(t, jnp.maximum(nu[0] - 1, 0)), 0)),
            pl.BlockSpec((None, d, de2), per_expert),
            pl.BlockSpec((None, 1, de), per_expert),
            pl.BlockSpec((None, 1, de), per_expert),
            pl.BlockSpec((None, de, d), per_expert),
            pl.BlockSpec((None, 1, d), per_expert),
        ],
        out_specs=pl.BlockSpec((tile_rows, LANES), lambda t, be, nu: (t, 0)),
        scratch_shapes=[pltpu.VMEM((d, de2), BF16), pltpu.VMEM((de, d), BF16)],
    )
    return pl.pallas_call(
        _expert_kernel,
        grid_spec=grid_spec,
        out_shape=jax.ShapeDtypeStruct((n_rows * SUBLANES, LANES), F32),
        compiler_params=pltpu.CompilerParams(
            dimension_semantics=("arbitrary",), vmem_limit_bytes=EXPERT_VMEM_LIMIT),
        name="experts",
    )(block_expert, n_used, xs_rows, wgu, bg, bu, wd, bd)


def _combine_kernel(len_ref, tpos_ref, src_ref, pos_ref, w_ref, h1_ref, fg_ref, ys_hbm,
                    o_ref, buf_ref, h1rows_ref, outrows_ref, sems):
    step = pl.program_id(0)
    n_steps = pl.num_programs(0)
    tt = h1_ref.shape[0]
    slot = lax.rem(step, 2)

    def start_runs(tile, buf):
        def per_expert(e, carry):
            k = tile * N_EXPERTS + e

            def piece(off, size):
                pltpu.make_async_copy(_token_rows(ys_hbm, src_ref[k] + off, size),
                                      _token_rows(buf_ref.at[buf], tpos_ref[k] + off, size), sems.at[buf]).start()

            _run_pieces(len_ref[k], piece)
            return carry

        lax.fori_loop(0, N_EXPERTS, per_expert, 0)

    def wait_runs(buf):
        whole = buf_ref.at[buf]
        pltpu.make_async_copy(_token_rows(ys_hbm, 0, tt * EXPERT_TOPK), whole, sems.at[buf]).wait()

    @pl.when(step == 0)
    def _():
        start_runs(0, 0)

    @pl.when(step + 1 < n_steps)
    def _():
        start_runs(step + 1, 1 - slot)

    wait_runs(slot)

    _store_token_rows(h1rows_ref, h1_ref[...])

    def one_token(tok, carry):
        acc = h1rows_ref[pl.ds(pl.multiple_of(tok * SUBLANES, SUBLANES), SUBLANES), :]
        for kk in range(EXPERT_TOPK):
            a = tok * EXPERT_TOPK + kk
            pos = pos_ref[a]
            acc = acc + w_ref[a] * buf_ref[slot, pl.ds(pl.multiple_of(pos * SUBLANES, SUBLANES), SUBLANES), :]
        outrows_ref[pl.ds(pl.multiple_of(tok * SUBLANES, SUBLANES), SUBLANES), :] = acc
        return carry

    lax.fori_loop(0, tt, one_token, 0, unroll=4)
    h = _load_token_rows(outrows_ref, tt)
    o_ref[...] = h * lax.rsqrt(jnp.mean(h * h, axis=-1, keepdims=True) + NORM_EPS) * fg_ref[...]


def _combine(run_len, run_tpos, run_src, pos, w_flat, h1, fg, ys_rows):
    t, d = h1.shape
    tt = ROUTE_TOKENS
    n_assign = tt * EXPERT_TOPK
    smem_blk = pl.BlockSpec((n_assign,), lambda c, *_: (c,), memory_space=pltpu.SMEM)
    grid_spec = pltpu.PrefetchScalarGridSpec(
        num_scalar_prefetch=3,
        grid=(t // tt,),
        in_specs=[
            smem_blk, smem_blk,
            pl.BlockSpec((tt, d), lambda c, *_: (c, 0)),
            pl.BlockSpec((1, d), lambda c, *_: (0, 0)),
            pl.BlockSpec(memory_space=pl.ANY),
        ],
        out_specs=pl.BlockSpec((tt, d), lambda c, *_: (c, 0)),
        scratch_shapes=[
            pltpu.VMEM((2, n_assign * SUBLANES, LANES), F32),
            pltpu.VMEM((tt * SUBLANES, LANES), F32),
            pltpu.VMEM((tt * SUBLANES, LANES), F32),
            pltpu.SemaphoreType.DMA((2,)),
        ],
    )
    return pl.pallas_call(
        _combine_kernel,
        grid_spec=grid_spec,
        out_shape=jax.ShapeDtypeStruct((t, d), F32),
        compiler_params=pltpu.CompilerParams(dimension_semantics=("arbitrary",), vmem_limit_bytes=VMEM_LIMIT),
        name="combine",
    )(run_len, run_tpos, run_src, pos, w_flat, h1, fg, ys_rows)


def _layer(h, mix_norm_g, w_in, w_attn_out, sgu_ln_g, sgu_ln_b, w_spatial, b_spatial, w_sgu_out,
           w_mix_out, ffn_norm_g, w_router, b_router, w_gate_up, b_gate_up, w_down, b_down, out_g):
    b, s, d = h.shape
    t = b * s
    sgu_width = sgu_ln_g.shape[0]
    x2 = h.reshape(t, d)

    q, qt, kt, v, kmean, u, vvn, ga, gs = _in_proj(
        x2, mix_norm_g.reshape(1, d), w_in.astype(BF16), sgu_ln_g.reshape(1, sgu_width),
        sgu_ln_b.reshape(1, sgu_width), sgu_width)

    nb = s // MOBA_BLOCK
    attn = _moba(q.reshape(b, s, ATTN_WIDTH), qt, kt, v.reshape(b, s, ATTN_WIDTH),
                 kmean.reshape(b, nb, ATTN_WIDTH), _moba_key_table(s)).reshape(t, ATTN_WIDTH)

    gdim = sgu_width // SGU_GROUPS
    bsp_full = jnp.repeat(b_spatial.T, gdim, axis=1)
    wr_pad = jnp.zeros((d, LANES), F32).at[:, :N_EXPERTS].set(w_router)
    br_pad = jnp.zeros((1, LANES), F32).at[0, :N_EXPERTS].set(b_router)
    h1, xn2_rows, ri, rw, tile_cnt = _mix(
        attn, u, vvn, ga, gs, x2, w_attn_out.astype(BF16), w_sgu_out.astype(BF16), w_mix_out.astype(BF16),
        w_spatial, bsp_full, ffn_norm_g.reshape(1, d), wr_pad, br_pad)

    n_assign = t * EXPERT_TOPK
    n_tiles = -(-(n_assign + N_EXPERTS * (EXPERT_ROWS - 1)) // EXPERT_ROWS)
    n_rows = n_tiles * EXPERT_ROWS
    run_len = tile_cnt[::SUBLANES, :N_EXPERTS].astype(jnp.int32)
    counts = jnp.sum(run_len, axis=0)
    padded = (counts + EXPERT_ROWS - 1) // EXPERT_ROWS * EXPERT_ROWS
    pad_end = jnp.cumsum(padded)
    starts = pad_end - padded
    run_tpos = jnp.cumsum(run_len, axis=1) - run_len
    run_row = starts[None, :] + jnp.cumsum(run_len, axis=0) - run_len
    tile_start = jnp.arange(n_tiles, dtype=jnp.int32) * EXPERT_ROWS
    block_expert = jnp.minimum(
        jnp.sum((pad_end[None, :] <= tile_start[:, None]).astype(jnp.int32), axis=1), N_EXPERTS - 1)
    n_used = (pad_end[-1:] // EXPERT_ROWS).astype(jnp.int32)
    pos = ri[:, EXPERT_TOPK:2 * EXPERT_TOPK].reshape(n_assign)
    tables = (run_len.reshape(-1), run_tpos.reshape(-1), run_row.reshape(-1))

    xs_rows = _dispatch(*tables, starts + counts, padded - counts, n_used, pos, xn2_rows, n_rows)

    de = w_down.shape[1]
    bg = b_gate_up[:, 0::2].reshape(N_EXPERTS, 1, de)
    bu = b_gate_up[:, 1::2].reshape(N_EXPERTS, 1, de)
    ys_rows = _experts(block_expert, n_used, xs_rows, w_gate_up, bg, bu, w_down, b_down.reshape(N_EXPERTS, 1, d))

    out = _combine(*tables, pos, rw[:, :EXPERT_TOPK].reshape(n_assign), h1, out_g.reshape(1, d), ys_rows)
    return out.reshape(b, s, d)


def kernel(x, mix_norm_g, w_in, w_attn_out, sgu_ln_g, sgu_ln_b, w_spatial, b_spatial, w_sgu_out, w_mix_out,
           ffn_norm_g, w_router, b_router, w_gate_up, b_gate_up, w_down, b_down, final_norm_g):
    depth = w_in.shape[0]
    assert depth == 1, "the final RMSNorm is fused into the single layer's combine step"
    return _layer(x, mix_norm_g[0], w_in[0], w_attn_out[0], sgu_ln_g[0], sgu_ln_b[0], w_spatial[0],
                  b_spatial[0], w_sgu_out[0], w_mix_out[0], ffn_norm_g[0], w_router[0], b_router[0],
                  w_gate_up[0], b_gate_up[0], w_down[0], b_down[0], final_norm_g)
```

```python
import jax
import jax.numpy as jnp
import numpy as np
from jax import lax
from jax.experimental import pallas as pl
from jax.experimental.pallas import tpu as pltpu

F32 = jnp.float32
BF16 = jnp.bfloat16
NEG_INF = float("-inf")
MASK_VALUE = -1e30


def _bf16_pieces(x, n=3):
    pieces = []
    for _ in range(n):
        p = float(np.asarray(x, np.float32).astype(BF16).astype(np.float32))
        pieces.append(p)
        x = x - p
    return pieces


LOG2E = 1.4426950408889634
LOG2E_PIECES = _bf16_pieces(LOG2E)

N_HEADS = 8
HEAD_DIM = 64
ATTN_WIDTH = N_HEADS * HEAD_DIM
MOBA_BLOCK = 256
MOBA_TOPK = 3
SGU_CHUNK = 128
SGU_GROUPS = 8
N_EXPERTS = 32
EXPERT_TOPK = 4
SWIGLU_LIMIT = 7.0
SWIGLU_ALPHA = 1.702
NORM_EPS = 1e-5

LANES = 128
SUBLANES = 8
HEADS_PER_LANE_TILE = LANES // HEAD_DIM
MOBA_GROUP = 1
MOBA_BIAS_LANE0 = 64
MOBA_DUMMY_LANE = LANES - 1
PROJ_ROWS = 256
EXPERT_ROWS = 512
ROUTE_TOKENS = 512
VMEM_LIMIT = 48 * 1024 * 1024
BIG_VMEM_LIMIT = 56 * 1024 * 1024


def _sigmoid(x):
    return 1.0 / (1.0 + jnp.exp(-x))


def _store_token_rows(ref, value, base=0):
    n = value.shape[0]
    for c in range(value.shape[1] // LANES):
        ref[pl.ds(base + c, n, stride=SUBLANES), :] = value[:, c * LANES:(c + 1) * LANES]


def _load_token_rows(ref, n, base=0):
    return jnp.concatenate([ref[pl.ds(base + c, n, stride=SUBLANES), :] for c in range(SUBLANES)], axis=1)


def _gelu_exact(x):
    return 0.5 * x * (1.0 + lax.erf(x * (0.5 ** 0.5)))


def _in_proj_kernel(x_ref, g_ref, w_ref, lng_ref, lnb_ref,
                    q_ref, qt_ref, kt_ref, v_ref, kmean_ref, u_ref, vvn_ref, ga_ref, gs_ref):
    x = x_ref[...]
    xn = x * lax.rsqrt(jnp.mean(x * x, axis=-1, keepdims=True) + NORM_EPS) * g_ref[...]
    xb = xn.astype(BF16)

    def proj(lo, hi):
        return jnp.dot(xb, w_ref[:, lo:hi], preferred_element_type=F32)

    a = ATTN_WIDTH
    sw = u_ref.shape[1]
    d = ga_ref.shape[1]
    qf = proj(0, a)
    q_ref[...] = qf
    kf = proj(a, 2 * a)
    for j in range(kf.shape[0] // MOBA_BLOCK):
        qt_ref[j] = qf[j * MOBA_BLOCK:(j + 1) * MOBA_BLOCK].T
        kblk = kf[j * MOBA_BLOCK:(j + 1) * MOBA_BLOCK]
        kt_ref[j] = kblk.T.astype(BF16)
        kmean_ref[j] = jnp.mean(kblk, axis=0, keepdims=True)
    v_ref[...] = proj(2 * a, 3 * a).astype(BF16)
    z0 = 3 * a
    u_ref[...] = _gelu_exact(proj(z0, z0 + sw))
    zv = _gelu_exact(proj(z0 + sw, z0 + 2 * sw))
    mu = jnp.mean(zv, axis=-1, keepdims=True)
    zc = zv - mu
    var = jnp.mean(zc * zc, axis=-1, keepdims=True)
    vvn_ref[...] = zc * lax.rsqrt(var + NORM_EPS) * lng_ref[...] + lnb_ref[...]
    g0 = z0 + 2 * sw
    ga_ref[...] = _sigmoid(proj(g0, g0 + d))
    gs_ref[...] = _sigmoid(proj(g0 + d, g0 + 2 * d))


def _in_proj(x2, g, w_bf, lng, lnb, sgu_width):
    t, d = x2.shape
    tm = PROJ_ROWS
    n_in = w_bf.shape[1]
    row = lambda w: pl.BlockSpec((tm, w), lambda i: (i, 0))
    const = lambda shape: pl.BlockSpec(shape, lambda i: (0,) * len(shape))
    out_shape = (
        jax.ShapeDtypeStruct((t, ATTN_WIDTH), F32),
        jax.ShapeDtypeStruct((t // MOBA_BLOCK, ATTN_WIDTH, MOBA_BLOCK), F32),
        jax.ShapeDtypeStruct((t // MOBA_BLOCK, ATTN_WIDTH, MOBA_BLOCK), BF16),
        jax.ShapeDtypeStruct((t, ATTN_WIDTH), BF16),
        jax.ShapeDtypeStruct((t // MOBA_BLOCK, 1, ATTN_WIDTH), F32),
        jax.ShapeDtypeStruct((t, sgu_width), F32),
        jax.ShapeDtypeStruct((t, sgu_width), F32),
        jax.ShapeDtypeStruct((t, d), F32),
        jax.ShapeDtypeStruct((t, d), F32),
    )
    out_specs = (
        row(ATTN_WIDTH),
        pl.BlockSpec((tm // MOBA_BLOCK, ATTN_WIDTH, MOBA_BLOCK), lambda i: (i, 0, 0)),
        pl.BlockSpec((tm // MOBA_BLOCK, ATTN_WIDTH, MOBA_BLOCK), lambda i: (i, 0, 0)),
        row(ATTN_WIDTH),
        pl.BlockSpec((tm // MOBA_BLOCK, 1, ATTN_WIDTH), lambda i: (i, 0, 0)),
        row(sgu_width), row(sgu_width), row(d), row(d),
    )
    return pl.pallas_call(
        _in_proj_kernel,
        grid=(t // tm,),
        in_specs=[row(d), const((1, d)), const((d, n_in)), const((1, sgu_width)), const((1, sgu_width))],
        out_specs=out_specs,
        out_shape=out_shape,
        compiler_params=pltpu.CompilerParams(dimension_semantics=("arbitrary",), vmem_limit_bytes=VMEM_LIMIT),
        name="in_proj",
    )(x2, g, w_bf, lng, lnb)


def _alibi_slopes():
    slopes = 2.0 ** (-8.0 * np.arange(1, N_HEADS + 1, dtype=np.float64) / N_HEADS)
    assert all(np.log2(s) == np.round(np.log2(s)) for s in slopes), "ALiBi slopes must be powers of two"
    return slopes


def _moba_bias_lane(head_in_tile, part, piece):
    return MOBA_BIAS_LANE0 + (head_in_tile * 2 + part) * len(LOG2E_PIECES) + piece


def _moba_key_table(s):
    nb = s // MOBA_BLOCK
    slopes = _alibi_slopes()
    n_pairs = N_HEADS // HEADS_PER_LANE_TILE
    table = np.zeros((nb + 1, n_pairs, LANES, MOBA_BLOCK), np.float32)
    offs = np.arange(MOBA_BLOCK, dtype=np.float32)
    assert HEADS_PER_LANE_TILE * nb <= MOBA_BIAS_LANE0, "block one-hot rows must not overlap the ALiBi rows"
    for j in range(nb):
        for hh in range(HEADS_PER_LANE_TILE):
            table[j, :, hh * nb + j, :] = 1.0
        for p in range(n_pairs):
            for hh in range(HEADS_PER_LANE_TILE):
                slope = slopes[p * HEADS_PER_LANE_TILE + hh]
                for piece in range(len(LOG2E_PIECES)):
                    table[j, p, _moba_bias_lane(hh, 0, piece), :] = slope * MOBA_BLOCK * j
                    table[j, p, _moba_bias_lane(hh, 1, piece), :] = slope * offs
    table[nb, :, MOBA_DUMMY_LANE, :] = 1.0
    as_bf16 = table.astype(BF16)
    assert np.array_equal(as_bf16.astype(np.float32), table), "bias table must be exact in bf16"
    return jnp.asarray(as_bf16.reshape(nb + 1, n_pairs * LANES, MOBA_BLOCK))


def _moba_kernel(q_ref, qt_ref, qn_ref, qtn_ref, kt_ref, v_ref, km_ref, ct_ref, causal_ref, o_ref, qaug_ref,
                 s_ref, mpart_ref, mrow_ref, acc_ref):
    i = pl.program_id(2)
    nb = kt_ref.shape[0]
    blk = MOBA_BLOCK
    lane = lax.broadcasted_iota(jnp.int32, (1, LANES), 1)
    n_groups = lax.div(i + MOBA_GROUP, MOBA_GROUP)
    heads = range(HEADS_PER_LANE_TILE)
    hmasks = [(lane >= HEAD_DIM * hh) & (lane < HEAD_DIM * (hh + 1)) for hh in heads]
    slot = lax.rem(i, 2)

    def build_query_operand(q, qt, tile, into):
        bid = lax.broadcasted_iota(jnp.int32, (nb, blk), 0)
        bid_f = bid.astype(F32)
        masks_t = []
        for hh in heads:
            km_h = jnp.where(hmasks[hh], km_ref[...], 0.0)
            gate = jnp.dot(km_h, qt, precision=lax.Precision.HIGHEST, preferred_element_type=F32)
            gate = jnp.where(bid < tile, gate, NEG_INF)
            blockmask = jnp.where(bid == tile, 0.0, MASK_VALUE)
            for kk in range(MOBA_TOPK):
                gmax = jnp.max(gate, axis=0, keepdims=True)
                first = jnp.min(jnp.where(gate == gmax, bid_f, float(nb)), axis=0, keepdims=True)
                valid = (tile > kk).astype(F32)
                first = first * valid + (valid - 1.0)
                hit = bid_f == first
                blockmask = jnp.where(hit, 0.0, blockmask)
                gate = jnp.where(hit, NEG_INF, gate)
            masks_t.append(blockmask)
        masks_t.append(jnp.zeros((LANES - len(masks_t) * nb, blk), F32))
        masks = jnp.concatenate(masks_t, axis=0).T

        for hh in heads:
            feats = jnp.where(lane == MOBA_DUMMY_LANE, MASK_VALUE, 0.0)
            for part in range(2):
                for piece, value in enumerate(LOG2E_PIECES):
                    feats = jnp.where(lane == _moba_bias_lane(hh, part, piece), value, feats)
            extra = jnp.where((lane >= hh * nb) & (lane < (hh + 1) * nb), masks, feats)
            qh = jnp.where(hmasks[hh], q, 0.0)
            qaug_ref[into, hh * blk:(hh + 1) * blk, 0:LANES] = (qh * (HEAD_DIM ** -0.5 * LOG2E)).astype(BF16)
            qaug_ref[into, hh * blk:(hh + 1) * blk, LANES:2 * LANES] = extra.astype(BF16)

    @pl.when(i == 0)
    def _():
        build_query_operand(q_ref[...], qt_ref[...], i, slot)

    def scores(jk, jc):
        rhs = jnp.concatenate([kt_ref[jk], ct_ref[jc]], axis=0)
        return jnp.dot(qaug_ref[slot], rhs, preferred_element_type=F32)

    def lane_halves_max(s):
        return jnp.maximum(s[:, 0:LANES], s[:, LANES:2 * LANES])

    n_rows = len(heads) * blk
    variants = [(groups, min(groups * MOBA_GROUP, nb)) for groups in range(1, pl.cdiv(nb, MOBA_GROUP) + 1)]

    for groups, n_blocks in variants:
        @pl.when(n_groups == groups)
        def _(n_blocks=n_blocks):
            part = None
            for j in range(n_blocks):
                s = scores(j, jnp.where(j <= i, j, nb)) + causal_ref[(i == j).astype(jnp.int32)]
                s_ref[j] = s
                part = lane_halves_max(s) if part is None else jnp.maximum(part, lane_halves_max(s))
            mpart_ref[...] = part

    ones = jnp.ones((blk, LANES), BF16)
    rowmax = jnp.max(mpart_ref[...], axis=1, keepdims=True)
    mrow_ref[...] = jnp.broadcast_to(rowmax, (n_rows, 2 * LANES))

    def weighted(j):
        p = jnp.exp2(s_ref[j] - mrow_ref[...]).astype(BF16)
        vaug = jnp.concatenate([v_ref[j * blk:(j + 1) * blk, :], ones], axis=1)
        return jnp.dot(p, vaug, preferred_element_type=F32)

    next_tile = jnp.minimum(i + 1, nb - 1)
    for groups, n_blocks in variants:
        @pl.when(n_groups == groups)
        def _(n_blocks=n_blocks):
            build_query_operand(qn_ref[...], qtn_ref[...], next_tile, 1 - slot)
            tot = weighted(0)
            for j in range(1, n_blocks):
                tot = tot + weighted(j)
            acc_ref[...] = tot

    out = jnp.zeros((blk, LANES), F32)
    for hh in heads:
        acc = acc_ref[hh * blk:(hh + 1) * blk, :]
        out = jnp.where(hmasks[hh], acc[:, 0:LANES] / acc[:, LANES:2 * LANES], out)
    o_ref[...] = out


def _moba(q, qt, kt, v, kmean, key_table):
    b, s, a = q.shape
    nb = s // MOBA_BLOCK
    n_pairs = a // LANES
    rows = HEADS_PER_LANE_TILE * MOBA_BLOCK
    row_in_block = np.arange(rows)[:, None] % MOBA_BLOCK
    causal = np.where(row_in_block >= np.arange(MOBA_BLOCK)[None, :], 0.0, MASK_VALUE).astype(np.float32)
    causal_tiles = jnp.asarray(np.stack([np.zeros_like(causal), causal]))
    nxt = lambda i: jnp.minimum(i + 1, nb - 1)
    return pl.pallas_call(
        _moba_kernel,
        grid=(b, n_pairs, nb),
        in_specs=[
            pl.BlockSpec((None, MOBA_BLOCK, LANES), lambda bi, p, i: (bi, i, p)),
            pl.BlockSpec((None, LANES, MOBA_BLOCK), lambda bi, p, i: (bi * nb + i, p, 0)),
            pl.BlockSpec((None, MOBA_BLOCK, LANES), lambda bi, p, i: (bi, nxt(i), p)),
            pl.BlockSpec((None, LANES, MOBA_BLOCK), lambda bi, p, i: (bi * nb + nxt(i), p, 0)),
            pl.BlockSpec((nb, LANES, MOBA_BLOCK), lambda bi, p, i: (bi, p, 0)),
            pl.BlockSpec((None, s, LANES), lambda bi, p, i: (bi, 0, p)),
            pl.BlockSpec((None, nb, LANES), lambda bi, p, i: (bi, 0, p)),
            pl.BlockSpec((nb + 1, LANES, MOBA_BLOCK), lambda bi, p, i: (0, p, 0)),
            pl.BlockSpec((2, rows, MOBA_BLOCK), lambda bi, p, i: (0, 0, 0)),
        ],
        out_specs=pl.BlockSpec((None, MOBA_BLOCK, LANES), lambda bi, p, i: (bi, i, p)),
        out_shape=jax.ShapeDtypeStruct((b, s, a), F32),
        scratch_shapes=[
            pltpu.VMEM((2, rows, 2 * LANES), BF16),
            pltpu.VMEM((nb + 1, rows, MOBA_BLOCK), F32),
            pltpu.VMEM((rows, LANES), F32),
            pltpu.VMEM((rows, 2 * LANES), F32),
            pltpu.VMEM((rows, 2 * LANES), F32),
        ],
        compiler_params=pltpu.CompilerParams(
            dimension_semantics=("arbitrary", "arbitrary", "arbitrary"), vmem_limit_bytes=VMEM_LIMIT),
        name="moba",
    )(q, qt, q, qt, kt, v, kmean, key_table, causal_tiles)


def _mix_kernel(attn_ref, u_ref, vvn_ref, ga_ref, gs_ref, x_ref,
                wao_ref, wso_ref, wmo_ref, wsp_ref, bsp_ref, fg_ref, wr_ref, br_ref,
                h1_ref, xn2_ref, ri_ref, rw_ref, tcnt_ref, wcausal_ref):
    tm = x_ref.shape[0]
    ch = SGU_CHUNK

    @pl.when(pl.program_id(0) == 0)
    def _():
        tril = lax.broadcasted_iota(jnp.int32, (ch, ch), 0) >= lax.broadcasted_iota(jnp.int32, (ch, ch), 1)
        for g in range(SGU_GROUPS):
            wcausal_ref[g] = jnp.where(tril, wsp_ref[g], 0.0).astype(BF16)

    y_attn = jnp.dot(attn_ref[...].astype(BF16), wao_ref[...], preferred_element_type=F32)

    lane = lax.broadcasted_iota(jnp.int32, (1, LANES), 1)
    gdim = vvn_ref.shape[1] // SGU_GROUPS
    groups_per_tile = LANES // gdim
    w_causal = [wcausal_ref[g] for g in range(SGU_GROUPS)]
    rows = []
    for c in range(tm // ch):
        cols = []
        for ct in range(vvn_ref.shape[1] // LANES):
            vp = vvn_ref[c * ch:(c + 1) * ch, ct * LANES:(ct + 1) * LANES]
            acc = jnp.zeros((ch, LANES), F32)
            for gg in range(groups_per_tile):
                gmask = (lane >= gdim * gg) & (lane < gdim * (gg + 1))
                vm = jnp.where(gmask, vp, 0.0).astype(BF16)
                acc = acc + jnp.dot(w_causal[ct * groups_per_tile + gg], vm, preferred_element_type=F32)
            cols.append(acc)
        rows.append(jnp.concatenate(cols, axis=1) + bsp_ref[...])
    mixed = jnp.concatenate(rows, axis=0)
    sgu = u_ref[...] * mixed
    y_sgu = jnp.dot(sgu.astype(BF16), wso_ref[...], preferred_element_type=F32)

    merged = ga_ref[...] * y_attn + gs_ref[...] * y_sgu
    h1 = x_ref[...] + jnp.dot(merged.astype(BF16), wmo_ref[...], preferred_element_type=F32)
    h1_ref[...] = h1
    xn2 = h1 * lax.rsqrt(jnp.mean(h1 * h1, axis=-1, keepdims=True) + NORM_EPS) * fg_ref[...]
    _store_token_rows(xn2_ref, xn2)

    def split(a):
        hi = a.astype(BF16)
        return hi, (a - hi.astype(F32)).astype(BF16)

    x_hi, x_lo = split(xn2)
    w_hi, w_lo = split(wr_ref[...])
    logits = (jnp.dot(x_hi, w_hi, preferred_element_type=F32) + jnp.dot(x_lo, w_hi, preferred_element_type=F32)
              + jnp.dot(x_hi, w_lo, preferred_element_type=F32)) + br_ref[...]
    work = jnp.where(lane < N_EXPERTS, logits, NEG_INF)
    lane_f = lane.astype(F32)
    vals, idxs = [], []
    for _ in range(EXPERT_TOPK):
        vmax = jnp.max(work, axis=1, keepdims=True)
        first = jnp.min(jnp.where(work == vmax, lane_f, float(LANES)), axis=1, keepdims=True)
        vals.append(vmax)
        idxs.append(first)
        work = jnp.where(lane_f == first, NEG_INF, work)
    exps = [jnp.exp(v - vals[0]) for v in vals]
    denom = exps[0]
    for e in exps[1:]:
        denom = denom + e
    chosen = jnp.zeros((tm, LANES), F32)
    for first in idxs:
        chosen = jnp.where(lane_f == first, 1.0, chosen)

    strict_lower = (lax.broadcasted_iota(jnp.int32, (tm, tm), 0)
                    > lax.broadcasted_iota(jnp.int32, (tm, tm), 1))
    before = jnp.dot(jnp.where(strict_lower, 1.0, 0.0).astype(BF16), chosen.astype(BF16),
                     preferred_element_type=F32)
    counts = jnp.broadcast_to(jnp.sum(chosen, axis=0, keepdims=True), tcnt_ref.shape)
    lower_expert = (lax.broadcasted_iota(jnp.int32, (LANES, LANES), 0)
                    < lax.broadcasted_iota(jnp.int32, (LANES, LANES), 1))
    expert_start = jnp.dot(counts, jnp.where(lower_expert, 1.0, 0.0), precision=lax.Precision.HIGHEST,
                           preferred_element_type=F32)[0:1, :]
    sorted_pos = before + expert_start
    ri = jnp.zeros((tm, LANES), jnp.int32)
    rw = jnp.zeros((tm, LANES), F32)
    for kk in range(EXPERT_TOPK):
        pos = jnp.sum(jnp.where(lane_f == idxs[kk], sorted_pos, 0.0), axis=1, keepdims=True)
        ri = jnp.where(lane == kk, idxs[kk].astype(jnp.int32), ri)
        ri = jnp.where(lane == EXPERT_TOPK + kk, pos.astype(jnp.int32), ri)
        rw = jnp.where(lane == kk, exps[kk] / denom, rw)
    ri_ref[...] = ri
    rw_ref[...] = rw
    tcnt_ref[...] = counts


def _mix(attn, u, vvn, ga, gs, x2, wao, wso, wmo, wsp, bsp_full, fg, wr_pad, br_pad):
    t, d = x2.shape
    tm = ROUTE_TOKENS
    row = lambda w: pl.BlockSpec((tm, w), lambda i: (i, 0))
    const = lambda shape: pl.BlockSpec(shape, lambda i: (0,) * len(shape))
    ins = (attn, u, vvn, ga, gs, x2, wao, wso, wmo, wsp, bsp_full, fg, wr_pad, br_pad)
    in_specs = [row(a.shape[1]) for a in ins[:6]] + [const(a.shape) for a in ins[6:]]
    out_shape = (
        jax.ShapeDtypeStruct((t, d), F32),
        jax.ShapeDtypeStruct((t * SUBLANES, LANES), F32),
        jax.ShapeDtypeStruct((t, LANES), jnp.int32),
        jax.ShapeDtypeStruct((t, LANES), F32),
        jax.ShapeDtypeStruct((t // tm * SUBLANES, LANES), F32),
    )
    assert d == SUBLANES * LANES, "a token row is stored as one (8, 128) tile"
    token_rows = pl.BlockSpec((tm * SUBLANES, LANES), lambda i: (i, 0))
    out_specs = (row(d), token_rows, row(LANES), row(LANES), pl.BlockSpec((SUBLANES, LANES), lambda i: (i, 0)))
    return pl.pallas_call(
        _mix_kernel,
        grid=(t // tm,),
        in_specs=in_specs,
        out_specs=out_specs,
        out_shape=out_shape,
        scratch_shapes=[pltpu.VMEM(wsp.shape, BF16)],
        compiler_params=pltpu.CompilerParams(dimension_semantics=("arbitrary",), vmem_limit_bytes=BIG_VMEM_LIMIT),
        name="mix_route",
    )(*ins)


def _run_pieces(n, body):
    off = jnp.int32(0)
    for bit in reversed(range(ROUTE_TOKENS.bit_length())):
        size = 1 << bit
        take = lax.bitwise_and(n, size) != 0

        @pl.when(take)
        def _(off=off, size=size):
            body(off, size)

        off = off + jnp.where(take, size, 0)


def _token_rows(ref, r, n):
    return ref.at[pl.ds(pl.multiple_of(r * SUBLANES, SUBLANES), n * SUBLANES)]


def _dispatch_kernel(len_ref, tpos_ref, dst_ref, padlo_ref, padlen_ref, nused_ref, pos_ref, x_ref,
                     xs_hbm, sorted_ref, zeros_ref, sems, zsem):
    step = pl.program_id(0)
    n_steps = pl.num_programs(0)
    n_assign = pos_ref.shape[0]
    tile_rows = zeros_ref.shape[0] // SUBLANES
    slot = lax.rem(step, 2)

    def start_runs(tile, buf):
        def per_expert(e, carry):
            k = tile * N_EXPERTS + e

            def piece(off, size):
                pltpu.make_async_copy(_token_rows(sorted_ref.at[buf], tpos_ref[k] + off, size),
                                      _token_rows(xs_hbm, dst_ref[k] + off, size), sems.at[buf]).start()

            _run_pieces(len_ref[k], piece)
            return carry

        lax.fori_loop(0, N_EXPERTS, per_expert, 0)

    def wait_runs(buf):
        whole = sorted_ref.at[buf]
        pltpu.make_async_copy(whole, _token_rows(xs_hbm, 0, n_assign), sems.at[buf]).wait()

    @pl.when(step == 0)
    def _():
        zeros_ref[...] = jnp.zeros_like(zeros_ref)

        def tail_tile(tl, carry):
            cp = pltpu.make_async_copy(zeros_ref, _token_rows(xs_hbm, tl * tile_rows, tile_rows), zsem)
            cp.start()
            cp.wait()
            return carry

        lax.fori_loop(nused_ref[0], xs_hbm.shape[0] // (tile_rows * SUBLANES), tail_tile, 0)

        def per_expert(e, carry):
            def piece(off, size):
                cp = pltpu.make_async_copy(_token_rows(zeros_ref, 0, size),
                                           _token_rows(xs_hbm, padlo_ref[e] + off, size), zsem)
                cp.start()
                cp.wait()

            _run_pieces(padlen_ref[e], piece)
            return carry

        lax.fori_loop(0, N_EXPERTS, per_expert, 0)

    @pl.when(step >= 2)
    def _():
        wait_runs(slot)

    def place(a, carry):
        tok = lax.shift_right_logical(a, 2)
        pos = pos_ref[a]
        sorted_ref[slot, pl.ds(pl.multiple_of(pos * SUBLANES, SUBLANES), SUBLANES), :] = (
            x_ref[pl.ds(pl.multiple_of(tok * SUBLANES, SUBLANES), SUBLANES), :])
        return carry

    lax.fori_loop(0, n_assign, place, 0, unroll=8)
    start_runs(step, slot)

    @pl.when(step == n_steps - 1)
    def _():
        @pl.when(step >= 1)
        def _():
            wait_runs(1 - slot)

        wait_runs(slot)


def _dispatch(run_len, run_tpos, run_dst, padlo, padlen, n_used, pos, xn2_rows, n_rows):
    t = xn2_rows.shape[0] // SUBLANES
    n_assign = ROUTE_TOKENS * EXPERT_TOPK
    smem_blk = pl.BlockSpec((n_assign,), lambda c, *_: (c,), memory_space=pltpu.SMEM)
    grid_spec = pltpu.PrefetchScalarGridSpec(
        num_scalar_prefetch=6,
        grid=(t // ROUTE_TOKENS,),
        in_specs=[smem_blk, pl.BlockSpec((ROUTE_TOKENS * SUBLANES, LANES), lambda c, *_: (c, 0))],
        out_specs=pl.BlockSpec(memory_space=pl.ANY),
        scratch_shapes=[
            pltpu.VMEM((2, n_assign * SUBLANES, LANES), F32),
            pltpu.VMEM((EXPERT_ROWS * SUBLANES, LANES), F32),
            pltpu.SemaphoreType.DMA((2,)),
            pltpu.SemaphoreType.DMA(()),
        ],
    )
    return pl.pallas_call(
        _dispatch_kernel,
        grid_spec=grid_spec,
        out_shape=jax.ShapeDtypeStruct((n_rows * SUBLANES, LANES), F32),
        compiler_params=pltpu.CompilerParams(dimension_semantics=("arbitrary",), vmem_limit_bytes=VMEM_LIMIT),
        name="dispatch",
    )(run_len, run_tpos, run_dst, padlo, padlen, n_used, pos, xn2_rows)


def _expert_kernel(be_ref, nused_ref, x_ref, wgu_ref, bg_ref, bu_ref, wd_ref, bd_ref, y_ref, wgu_bf, wd_bf):
    t = pl.program_id(0)
    rows = y_ref.shape[0] // SUBLANES
    de = wd_ref.shape[0]
    grp = 2 * LANES
    active = t < nused_ref[0]
    fresh = jnp.logical_and(
        active, jnp.logical_or(t == 0, be_ref[t] != be_ref[jnp.maximum(t - 1, 0)]))

    @pl.when(fresh)
    def _():
        src = lax.broadcasted_iota(jnp.int32, (grp, grp), 0)
        dst = lax.broadcasted_iota(jnp.int32, (grp, grp), 1)
        wanted = jnp.where(dst < LANES, 2 * dst, 2 * (dst - LANES) + 1)
        perm = jnp.where(src == wanted, 1.0, 0.0).astype(BF16)
        for g in range(wgu_ref.shape[1] // grp):
            cols = slice(g * grp, (g + 1) * grp)
            wgu_bf[:, cols] = jnp.dot(wgu_ref[:, cols].astype(BF16), perm,
                                      preferred_element_type=F32).astype(BF16)
        wd_bf[...] = wd_ref[...].astype(BF16)

    @pl.when(active)
    def _():
        xb = _load_token_rows(x_ref, rows).astype(BF16)
        gu = jnp.dot(xb, wgu_bf[...], preferred_element_type=F32)
        hid = []
        for g in range(de // LANES):
            cols = slice(g * LANES, (g + 1) * LANES)
            gate = gu[:, g * grp:g * grp + LANES] + bg_ref[:, cols]
            up = gu[:, g * grp + LANES:(g + 1) * grp] + bu_ref[:, cols]
            gate = jnp.minimum(gate, SWIGLU_LIMIT)
            up = jnp.clip(up, -SWIGLU_LIMIT, SWIGLU_LIMIT)
            hid.append((gate * _sigmoid(SWIGLU_ALPHA * gate) * (up + 1.0)).astype(BF16))
        y = jnp.dot(jnp.concatenate(hid, axis=1), wd_bf[...], preferred_element_type=F32) + bd_ref[...]
        _store_token_rows(y_ref, y)

    @pl.when(jnp.logical_not(active))
    def _():
        y_ref[...] = jnp.zeros_like(y_ref)


def _experts(block_expert, n_used, xs_rows, wgu, bg, bu, wd, bd):
    n_rows = xs_rows.shape[0] // SUBLANES
    _, d, de2 = wgu.shape
    de = wd.shape[1]
    n_tiles = n_rows // EXPERT_ROWS
    tile_rows = EXPERT_ROWS * SUBLANES
    per_expert = lambda t, be, nu: (be[t], 0, 0)
    grid_spec = pltpu.PrefetchScalarGridSpec(
        num_scalar_prefetch=2,
        grid=(n_tiles,),
        in_specs=[
            pl.BlockSpec((tile_rows, LANES), lambda t, be, nu: (jnp.minimum(t, jnp.maximum(nu[0] - 1, 0)), 0)),
            pl.BlockSpec((None, d, de2), per_expert),
            pl.BlockSpec((None, 1, de), per_expert),
            pl.BlockSpec((None, 1, de), per_expert),
            pl.BlockSpec((None, de, d), per_expert),
            pl.BlockSpec((None, 1, d), per_expert),
        ],
        out_specs=pl.BlockSpec((tile_rows, LANES), lambda t, be, nu: (t, 0)),
        scratch_shapes=[pltpu.VMEM((d, de2), BF16), pltpu.VMEM((de, d), BF16)],
    )
    return pl.pallas_call(
        _expert_kernel,
        grid_spec=grid_spec,
        out_shape=jax.ShapeDtypeStruct((n_rows * SUBLANES, LANES), F32),
        compiler_params=pltpu.CompilerParams(
            dimension_semantics=("arbitrary",), vmem_limit_bytes=BIG_VMEM_LIMIT),
        name="experts",
    )(block_expert, n_used, xs_rows, wgu, bg, bu, wd, bd)


def _combine_kernel(len_ref, tpos_ref, src_ref, pos_ref, w_ref, h1_ref, fg_ref, ys_hbm,
                    o_ref, buf_ref, h1rows_ref, outrows_ref, sems):
    step = pl.program_id(0)
    n_steps = pl.num_programs(0)
    tt = h1_ref.shape[0]
    slot = lax.rem(step, 2)

    def start_runs(tile, buf):
        def per_expert(e, carry):
            k = tile * N_EXPERTS + e

            def piece(off, size):
                pltpu.make_async_copy(_token_rows(ys_hbm, src_ref[k] + off, size),
                                      _token_rows(buf_ref.at[buf], tpos_ref[k] + off, size), sems.at[buf]).start()

            _run_pieces(len_ref[k], piece)
            return carry

        lax.fori_loop(0, N_EXPERTS, per_expert, 0)

    def wait_runs(buf):
        whole = buf_ref.at[buf]
        pltpu.make_async_copy(_token_rows(ys_hbm, 0, tt * EXPERT_TOPK), whole, sems.at[buf]).wait()

    @pl.when(step == 0)
    def _():
        start_runs(0, 0)

    @pl.when(step + 1 < n_steps)
    def _():
        start_runs(step + 1, 1 - slot)

    wait_runs(slot)

    _store_token_rows(h1rows_ref, h1_ref[...])

    def one_token(tok, carry):
        acc = h1rows_ref[pl.ds(pl.multiple_of(tok * SUBLANES, SUBLANES), SUBLANES), :]
        for kk in range(EXPERT_TOPK):
            a = tok * EXPERT_TOPK + kk
            pos = pos_ref[a]
            acc = acc + w_ref[a] * buf_ref[slot, pl.ds(pl.multiple_of(pos * SUBLANES, SUBLANES), SUBLANES), :]
        outrows_ref[pl.ds(pl.multiple_of(tok * SUBLANES, SUBLANES), SUBLANES), :] = acc
        return carry

    lax.fori_loop(0, tt, one_token, 0, unroll=4)
    h = _load_token_rows(outrows_ref, tt)
    o_ref[...] = h * lax.rsqrt(jnp.mean(h * h, axis=-1, keepdims=True) + NORM_EPS) * fg_ref[...]


def _combine(run_len, run_tpos, run_src, pos, w_flat, h1, fg, ys_rows):
    t, d = h1.shape
    tt = ROUTE_TOKENS
    n_assign = tt * EXPERT_TOPK
    smem_blk = pl.BlockSpec((n_assign,), lambda c, *_: (c,), memory_space=pltpu.SMEM)
    grid_spec = pltpu.PrefetchScalarGridSpec(
        num_scalar_prefetch=3,
        grid=(t // tt,),
        in_specs=[
            smem_blk, smem_blk,
            pl.BlockSpec((tt, d), lambda c, *_: (c, 0)),
            pl.BlockSpec((1, d), lambda c, *_: (0, 0)),
            pl.BlockSpec(memory_space=pl.ANY),
        ],
        out_specs=pl.BlockSpec((tt, d), lambda c, *_: (c, 0)),
        scratch_shapes=[
            pltpu.VMEM((2, n_assign * SUBLANES, LANES), F32),
            pltpu.VMEM((tt * SUBLANES, LANES), F32),
            pltpu.VMEM((tt * SUBLANES, LANES), F32),
            pltpu.SemaphoreType.DMA((2,)),
        ],
    )
    return pl.pallas_call(
        _combine_kernel,
        grid_spec=grid_spec,
        out_shape=jax.ShapeDtypeStruct((t, d), F32),
        compiler_params=pltpu.CompilerParams(dimension_semantics=("arbitrary",), vmem_limit_bytes=VMEM_LIMIT),
        name="combine",
    )(run_len, run_tpos, run_src, pos, w_flat, h1, fg, ys_rows)


def _layer(h, mix_norm_g, w_in, w_attn_out, sgu_ln_g, sgu_ln_b, w_spatial, b_spatial, w_sgu_out,
           w_mix_out, ffn_norm_g, w_router, b_router, w_gate_up, b_gate_up, w_down, b_down, out_g):
    b, s, d = h.shape
    t = b * s
    sgu_width = sgu_ln_g.shape[0]
    x2 = h.reshape(t, d)

    q, qt, kt, v, kmean, u, vvn, ga, gs = _in_proj(
        x2, mix_norm_g.reshape(1, d), w_in.astype(BF16), sgu_ln_g.reshape(1, sgu_width),
        sgu_ln_b.reshape(1, sgu_width), sgu_width)

    nb = s // MOBA_BLOCK
    attn = _moba(q.reshape(b, s, ATTN_WIDTH), qt, kt, v.reshape(b, s, ATTN_WIDTH),
                 kmean.reshape(b, nb, ATTN_WIDTH), _moba_key_table(s)).reshape(t, ATTN_WIDTH)

    gdim = sgu_width // SGU_GROUPS
    bsp_full = jnp.repeat(b_spatial.T, gdim, axis=1)
    wr_pad = jnp.zeros((d, LANES), F32).at[:, :N_EXPERTS].set(w_router)
    br_pad = jnp.zeros((1, LANES), F32).at[0, :N_EXPERTS].set(b_router)
    h1, xn2_rows, ri, rw, tile_cnt = _mix(
        attn, u, vvn, ga, gs, x2, w_attn_out.astype(BF16), w_sgu_out.astype(BF16), w_mix_out.astype(BF16),
        w_spatial, bsp_full, ffn_norm_g.reshape(1, d), wr_pad, br_pad)

    n_assign = t * EXPERT_TOPK
    n_tiles = -(-(n_assign + N_EXPERTS * (EXPERT_ROWS - 1)) // EXPERT_ROWS)
    n_rows = n_tiles * EXPERT_ROWS
    run_len = tile_cnt[::SUBLANES, :N_EXPERTS].astype(jnp.int32)
    counts = jnp.sum(run_len, axis=0)
    padded = (counts + EXPERT_ROWS - 1) // EXPERT_ROWS * EXPERT_ROWS
    pad_end = jnp.cumsum(padded)
    starts = pad_end - padded
    run_tpos = jnp.cumsum(run_len, axis=1) - run_len
    run_row = starts[None, :] + jnp.cumsum(run_len, axis=0) - run_len
    tile_start = jnp.arange(n_tiles, dtype=jnp.int32) * EXPERT_ROWS
    block_expert = jnp.minimum(
        jnp.sum((pad_end[None, :] <= tile_start[:, None]).astype(jnp.int32), axis=1), N_EXPERTS - 1)
    n_used = (pad_end[-1:] // EXPERT_ROWS).astype(jnp.int32)
    pos = ri[:, EXPERT_TOPK:2 * EXPERT_TOPK].reshape(n_assign)
    tables = (run_len.reshape(-1), run_tpos.reshape(-1), run_row.reshape(-1))

    xs_rows = _dispatch(*tables, starts + counts, padded - counts, n_used, pos, xn2_rows, n_rows)

    de = w_down.shape[1]
    bg = b_gate_up[:, 0::2].reshape(N_EXPERTS, 1, de)
    bu = b_gate_up[:, 1::2].reshape(N_EXPERTS, 1, de)
    ys_rows = _experts(block_expert, n_used, xs_rows, w_gate_up, bg, bu, w_down, b_down.reshape(N_EXPERTS, 1, d))

    out = _combine(*tables, pos, rw[:, :EXPERT_TOPK].reshape(n_assign), h1, out_g.reshape(1, d), ys_rows)
    return out.reshape(b, s, d)


def kernel(x, mix_norm_g, w_in, w_attn_out, sgu_ln_g, sgu_ln_b, w_spatial, b_spatial, w_sgu_out, w_mix_out,
           ffn_norm_g, w_router, b_router, w_gate_up, b_gate_up, w_down, b_down, final_norm_g):
    depth = w_in.shape[0]
    assert depth == 1, "the final RMSNorm is fused into the single layer's combine step"
    return _layer(x, mix_norm_g[0], w_in[0], w_attn_out[0], sgu_ln_g[0], sgu_ln_b[0], w_spatial[0],
                  b_spatial[0], w_sgu_out[0], w_mix_out[0], ffn_norm_g[0], w_router[0], b_router[0],
                  w_gate_up[0], b_gate_up[0], w_down[0], b_down[0], final_norm_g)
```

```python
import jax
import jax.numpy as jnp
import numpy as np
from jax import lax
from jax.experimental import pallas as pl
from jax.experimental.pallas import tpu as pltpu

F32 = jnp.float32
BF16 = jnp.bfloat16
NEG_INF = float("-inf")
MASK_VALUE = -1e30


def _bf16_pieces(x, n=3):
    pieces = []
    for _ in range(n):
        p = float(np.asarray(x, np.float32).astype(BF16).astype(np.float32))
        pieces.append(p)
        x = x - p
    return pieces


LOG2E = 1.4426950408889634
LOG2E_PIECES = _bf16_pieces(LOG2E)

N_HEADS = 8
HEAD_DIM = 64
ATTN_WIDTH = N_HEADS * HEAD_DIM
MOBA_BLOCK = 256
MOBA_TOPK = 3
SGU_CHUNK = 128
SGU_GROUPS = 8
N_EXPERTS = 32
EXPERT_TOPK = 4
SWIGLU_LIMIT = 7.0
SWIGLU_ALPHA = 1.702
NORM_EPS = 1e-5

LANES = 128
SUBLANES = 8
HEADS_PER_LANE_TILE = LANES // HEAD_DIM
MOBA_GROUP = 1
MOBA_BIAS_LANE0 = 64
MOBA_DUMMY_LANE = LANES - 1
PROJ_ROWS = 256
EXPERT_ROWS = 512
ROUTE_TOKENS = 512
VMEM_LIMIT = 48 * 1024 * 1024
BIG_VMEM_LIMIT = 56 * 1024 * 1024


def _sigmoid(x):
    return 1.0 / (1.0 + jnp.exp(-x))


def _store_token_rows(ref, value, base=0):
    n = value.shape[0]
    for c in range(value.shape[1] // LANES):
        ref[pl.ds(base + c, n, stride=SUBLANES), :] = value[:, c * LANES:(c + 1) * LANES]


def _load_token_rows(ref, n, base=0):
    return jnp.concatenate([ref[pl.ds(base + c, n, stride=SUBLANES), :] for c in range(SUBLANES)], axis=1)


def _gelu_exact(x):
    return 0.5 * x * (1.0 + lax.erf(x * (0.5 ** 0.5)))


def _in_proj_kernel(x_ref, g_ref, w_ref, lng_ref, lnb_ref,
                    q_ref, qt_ref, kt_ref, v_ref, kmean_ref, u_ref, vvn_ref, ga_ref, gs_ref):
    x = x_ref[...]
    xn = x * lax.rsqrt(jnp.mean(x * x, axis=-1, keepdims=True) + NORM_EPS) * g_ref[...]
    xb = xn.astype(BF16)

    def proj(lo, hi):
        return jnp.dot(xb, w_ref[:, lo:hi], preferred_element_type=F32)

    a = ATTN_WIDTH
    sw = u_ref.shape[1]
    d = ga_ref.shape[1]
    qf = proj(0, a)
    q_ref[...] = qf
    kf = proj(a, 2 * a)
    for j in range(kf.shape[0] // MOBA_BLOCK):
        qt_ref[j] = qf[j * MOBA_BLOCK:(j + 1) * MOBA_BLOCK].T
        kblk = kf[j * MOBA_BLOCK:(j + 1) * MOBA_BLOCK]
        kt_ref[j] = kblk.T.astype(BF16)
        kmean_ref[j] = jnp.mean(kblk, axis=0, keepdims=True)
    v_ref[...] = proj(2 * a, 3 * a).astype(BF16)
    z0 = 3 * a
    u_ref[...] = _gelu_exact(proj(z0, z0 + sw))
    zv = _gelu_exact(proj(z0 + sw, z0 + 2 * sw))
    mu = jnp.mean(zv, axis=-1, keepdims=True)
    zc = zv - mu
    var = jnp.mean(zc * zc, axis=-1, keepdims=True)
    vvn_ref[...] = zc * lax.rsqrt(var + NORM_EPS) * lng_ref[...] + lnb_ref[...]
    g0 = z0 + 2 * sw
    ga_ref[...] = _sigmoid(proj(g0, g0 + d))
    gs_ref[...] = _sigmoid(proj(g0 + d, g0 + 2 * d))


def _in_proj(x2, g, w_bf, lng, lnb, sgu_width):
    t, d = x2.shape
    tm = PROJ_ROWS
    n_in = w_bf.shape[1]
    row = lambda w: pl.BlockSpec((tm, w), lambda i: (i, 0))
    const = lambda shape: pl.BlockSpec(shape, lambda i: (0,) * len(shape))
    out_shape = (
        jax.ShapeDtypeStruct((t, ATTN_WIDTH), F32),
        jax.ShapeDtypeStruct((t // MOBA_BLOCK, ATTN_WIDTH, MOBA_BLOCK), F32),
        jax.ShapeDtypeStruct((t // MOBA_BLOCK, ATTN_WIDTH, MOBA_BLOCK), BF16),
        jax.ShapeDtypeStruct((t, ATTN_WIDTH), BF16),
        jax.ShapeDtypeStruct((t // MOBA_BLOCK, 1, ATTN_WIDTH), F32),
        jax.ShapeDtypeStruct((t, sgu_width), F32),
        jax.ShapeDtypeStruct((t, sgu_width), F32),
        jax.ShapeDtypeStruct((t, d), F32),
        jax.ShapeDtypeStruct((t, d), F32),
    )
    out_specs = (
        row(ATTN_WIDTH),
        pl.BlockSpec((tm // MOBA_BLOCK, ATTN_WIDTH, MOBA_BLOCK), lambda i: (i, 0, 0)),
        pl.BlockSpec((tm // MOBA_BLOCK, ATTN_WIDTH, MOBA_BLOCK), lambda i: (i, 0, 0)),
        row(ATTN_WIDTH),
        pl.BlockSpec((tm // MOBA_BLOCK, 1, ATTN_WIDTH), lambda i: (i, 0, 0)),
        row(sgu_width), row(sgu_width), row(d), row(d),
    )
    return pl.pallas_call(
        _in_proj_kernel,
        grid=(t // tm,),
        in_specs=[row(d), const((1, d)), const((d, n_in)), const((1, sgu_width)), const((1, sgu_width))],
        out_specs=out_specs,
        out_shape=out_shape,
        compiler_params=pltpu.CompilerParams(dimension_semantics=("arbitrary",), vmem_limit_bytes=VMEM_LIMIT),
        name="in_proj",
    )(x2, g, w_bf, lng, lnb)


def _alibi_slopes():
    slopes = 2.0 ** (-8.0 * np.arange(1, N_HEADS + 1, dtype=np.float64) / N_HEADS)
    assert all(np.log2(s) == np.round(np.log2(s)) for s in slopes), "ALiBi slopes must be powers of two"
    return slopes


def _moba_bias_lane(head_in_tile, part, piece):
    return MOBA_BIAS_LANE0 + (head_in_tile * 2 + part) * len(LOG2E_PIECES) + piece


def _moba_key_table(s):
    nb = s // MOBA_BLOCK
    slopes = _alibi_slopes()
    n_pairs = N_HEADS // HEADS_PER_LANE_TILE
    table = np.zeros((nb + 1, n_pairs, LANES, MOBA_BLOCK), np.float32)
    offs = np.arange(MOBA_BLOCK, dtype=np.float32)
    assert HEADS_PER_LANE_TILE * nb <= MOBA_BIAS_LANE0, "block one-hot rows must not overlap the ALiBi rows"
    for j in range(nb):
        for hh in range(HEADS_PER_LANE_TILE):
            table[j, :, hh * nb + j, :] = 1.0
        for p in range(n_pairs):
            for hh in range(HEADS_PER_LANE_TILE):
                slope = slopes[p * HEADS_PER_LANE_TILE + hh]
                for piece in range(len(LOG2E_PIECES)):
                    table[j, p, _moba_bias_lane(hh, 0, piece), :] = slope * MOBA_BLOCK * j
                    table[j, p, _moba_bias_lane(hh, 1, piece), :] = slope * offs
    table[nb, :, MOBA_DUMMY_LANE, :] = 1.0
    as_bf16 = table.astype(BF16)
    assert np.array_equal(as_bf16.astype(np.float32), table), "bias table must be exact in bf16"
    return jnp.asarray(as_bf16.reshape(nb + 1, n_pairs * LANES, MOBA_BLOCK))


def _moba_kernel(q_ref, qt_ref, qn_ref, qtn_ref, kt_ref, v_ref, km_ref, ct_ref, causal_ref, o_ref, qaug_ref,
                 s_ref, mpart_ref, mrow_ref, acc_ref):
    i = pl.program_id(2)
    nb = kt_ref.shape[0]
    blk = MOBA_BLOCK
    lane = lax.broadcasted_iota(jnp.int32, (1, LANES), 1)
    n_groups = lax.div(i + MOBA_GROUP, MOBA_GROUP)
    heads = range(HEADS_PER_LANE_TILE)
    hmasks = [(lane >= HEAD_DIM * hh) & (lane < HEAD_DIM * (hh + 1)) for hh in heads]
    slot = lax.rem(i, 2)

    def build_query_operand(q, qt, tile, into):
        bid = lax.broadcasted_iota(jnp.int32, (nb, blk), 0)
        bid_f = bid.astype(F32)
        masks_t = []
        for hh in heads:
            km_h = jnp.where(hmasks[hh], km_ref[...], 0.0)
            gate = jnp.dot(km_h, qt, precision=lax.Precision.HIGHEST, preferred_element_type=F32)
            gate = jnp.where(bid < tile, gate, NEG_INF)
            blockmask = jnp.where(bid == tile, 0.0, MASK_VALUE)
            for kk in range(MOBA_TOPK):
                gmax = jnp.max(gate, axis=0, keepdims=True)
                first = jnp.min(jnp.where(gate == gmax, bid_f, float(nb)), axis=0, keepdims=True)
                valid = (tile > kk).astype(F32)
                first = first * valid + (valid - 1.0)
                hit = bid_f == first
                blockmask = jnp.where(hit, 0.0, blockmask)
                gate = jnp.where(hit, NEG_INF, gate)
            masks_t.append(blockmask)
        masks_t.append(jnp.zeros((LANES - len(masks_t) * nb, blk), F32))
        masks = jnp.concatenate(masks_t, axis=0).T

        for hh in heads:
            feats = jnp.where(lane == MOBA_DUMMY_LANE, MASK_VALUE, 0.0)
            for part in range(2):
                for piece, value in enumerate(LOG2E_PIECES):
                    feats = jnp.where(lane == _moba_bias_lane(hh, part, piece), value, feats)
            extra = jnp.where((lane >= hh * nb) & (lane < (hh + 1) * nb), masks, feats)
            qh = jnp.where(hmasks[hh], q, 0.0)
            qaug_ref[into, hh * blk:(hh + 1) * blk, 0:LANES] = (qh * (HEAD_DIM ** -0.5 * LOG2E)).astype(BF16)
            qaug_ref[into, hh * blk:(hh + 1) * blk, LANES:2 * LANES] = extra.astype(BF16)

    @pl.when(i == 0)
    def _():
        build_query_operand(q_ref[...], qt_ref[...], i, slot)

    def scores(jk, jc):
        rhs = jnp.concatenate([kt_ref[jk], ct_ref[jc]], axis=0)
        return jnp.dot(qaug_ref[slot], rhs, preferred_element_type=F32)

    def lane_halves_max(s):
        return jnp.maximum(s[:, 0:LANES], s[:, LANES:2 * LANES])

    n_rows = len(heads) * blk
    variants = [(groups, min(groups * MOBA_GROUP, nb)) for groups in range(1, pl.cdiv(nb, MOBA_GROUP) + 1)]

    for groups, n_blocks in variants:
        @pl.when(n_groups == groups)
        def _(n_blocks=n_blocks):
            part = None
            for j in range(n_blocks):
                s = scores(j, jnp.where(j <= i, j, nb)) + causal_ref[(i == j).astype(jnp.int32)]
                s_ref[j] = s
                part = lane_halves_max(s) if part is None else jnp.maximum(part, lane_halves_max(s))
            mpart_ref[...] = part

    ones = jnp.ones((blk, LANES), BF16)
    rowmax = jnp.max(mpart_ref[...], axis=1, keepdims=True)
    mrow_ref[...] = jnp.broadcast_to(rowmax, (n_rows, 2 * LANES))

    def weighted(j):
        p = jnp.exp2(s_ref[j] - mrow_ref[...]).astype(BF16)
        vaug = jnp.concatenate([v_ref[j * blk:(j + 1) * blk, :], ones], axis=1)
        return jnp.dot(p, vaug, preferred_element_type=F32)

    next_tile = jnp.minimum(i + 1, nb - 1)
    for groups, n_blocks in variants:
        @pl.when(n_groups == groups)
        def _(n_blocks=n_blocks):
            build_query_operand(qn_ref[...], qtn_ref[...], next_tile, 1 - slot)
            tot = weighted(0)
            for j in range(1, n_blocks):
                tot = tot + weighted(j)
            acc_ref[...] = tot

    out = jnp.zeros((blk, LANES), F32)
    for hh in heads:
        acc = acc_ref[hh * blk:(hh + 1) * blk, :]
        out = jnp.where(hmasks[hh], acc[:, 0:LANES] / acc[:, LANES:2 * LANES], out)
    o_ref[...] = out


def _moba(q, qt, kt, v, kmean, key_table):
    b, s, a = q.shape
    nb = s // MOBA_BLOCK
    n_pairs = a // LANES
    rows = HEADS_PER_LANE_TILE * MOBA_BLOCK
    row_in_block = np.arange(rows)[:, None] % MOBA_BLOCK
    causal = np.where(row_in_block >= np.arange(MOBA_BLOCK)[None, :], 0.0, MASK_VALUE).astype(np.float32)
    causal_tiles = jnp.asarray(np.stack([np.zeros_like(causal), causal]))
    nxt = lambda i: jnp.minimum(i + 1, nb - 1)
    return pl.pallas_call(
        _moba_kernel,
        grid=(b, n_pairs, nb),
        in_specs=[
            pl.BlockSpec((None, MOBA_BLOCK, LANES), lambda bi, p, i: (bi, i, p)),
            pl.BlockSpec((None, LANES, MOBA_BLOCK), lambda bi, p, i: (bi * nb + i, p, 0)),
            pl.BlockSpec((None, MOBA_BLOCK, LANES), lambda bi, p, i: (bi, nxt(i), p)),
            pl.BlockSpec((None, LANES, MOBA_BLOCK), lambda bi, p, i: (bi * nb + nxt(i), p, 0)),
            pl.BlockSpec((nb, LANES, MOBA_BLOCK), lambda bi, p, i: (bi, p, 0)),
            pl.BlockSpec((None, s, LANES), lambda bi, p, i: (bi, 0, p)),
            pl.BlockSpec((None, nb, LANES), lambda bi, p, i: (bi, 0, p)),
            pl.BlockSpec((nb + 1, LANES, MOBA_BLOCK), lambda bi, p, i: (0, p, 0)),
            pl.BlockSpec((2, rows, MOBA_BLOCK), lambda bi, p, i: (0, 0, 0)),
        ],
        out_specs=pl.BlockSpec((None, MOBA_BLOCK, LANES), lambda bi, p, i: (bi, i, p)),
        out_shape=jax.ShapeDtypeStruct((b, s, a), F32),
        scratch_shapes=[
            pltpu.VMEM((2, rows, 2 * LANES), BF16),
            pltpu.VMEM((nb + 1, rows, MOBA_BLOCK), F32),
            pltpu.VMEM((rows, LANES), F32),
            pltpu.VMEM((rows, 2 * LANES), F32),
            pltpu.VMEM((rows, 2 * LANES), F32),
        ],
        compiler_params=pltpu.CompilerParams(
            dimension_semantics=("arbitrary", "arbitrary", "arbitrary"), vmem_limit_bytes=VMEM_LIMIT),
        name="moba",
    )(q, qt, q, qt, kt, v, kmean, key_table, causal_tiles)


def _mix_kernel(attn_ref, u_ref, vvn_ref, ga_ref, gs_ref, x_ref,
                wao_ref, wso_ref, wmo_ref, wsp_ref, bsp_ref, fg_ref, wr_ref, br_ref,
                h1_ref, xn2_ref, ri_ref, rw_ref, tcnt_ref, wcausal_ref):
    tm = x_ref.shape[0]
    ch = SGU_CHUNK

    @pl.when(pl.program_id(0) == 0)
    def _():
        tril = lax.broadcasted_iota(jnp.int32, (ch, ch), 0) >= lax.broadcasted_iota(jnp.int32, (ch, ch), 1)
        for g in range(SGU_GROUPS):
            wcausal_ref[g] = jnp.where(tril, wsp_ref[g], 0.0).astype(BF16)

    y_attn = jnp.dot(attn_ref[...].astype(BF16), wao_ref[...], preferred_element_type=F32)

    lane = lax.broadcasted_iota(jnp.int32, (1, LANES), 1)
    gdim = vvn_ref.shape[1] // SGU_GROUPS
    groups_per_tile = LANES // gdim
    w_causal = [wcausal_ref[g] for g in range(SGU_GROUPS)]
    rows = []
    for c in range(tm // ch):
        cols = []
        for ct in range(vvn_ref.shape[1] // LANES):
            vp = vvn_ref[c * ch:(c + 1) * ch, ct * LANES:(ct + 1) * LANES]
            acc = jnp.zeros((ch, LANES), F32)
            for gg in range(groups_per_tile):
                gmask = (lane >= gdim * gg) & (lane < gdim * (gg + 1))
                vm = jnp.where(gmask, vp, 0.0).astype(BF16)
                acc = acc + jnp.dot(w_causal[ct * groups_per_tile + gg], vm, preferred_element_type=F32)
            cols.append(acc)
        rows.append(jnp.concatenate(cols, axis=1) + bsp_ref[...])
    mixed = jnp.concatenate(rows, axis=0)
    sgu = u_ref[...] * mixed
    y_sgu = jnp.dot(sgu.astype(BF16), wso_ref[...], preferred_element_type=F32)

    merged = ga_ref[...] * y_attn + gs_ref[...] * y_sgu
    h1 = x_ref[...] + jnp.dot(merged.astype(BF16), wmo_ref[...], preferred_element_type=F32)
    h1_ref[...] = h1
    xn2 = h1 * lax.rsqrt(jnp.mean(h1 * h1, axis=-1, keepdims=True) + NORM_EPS) * fg_ref[...]
    _store_token_rows(xn2_ref, xn2)

    def split(a):
        hi = a.astype(BF16)
        return hi, (a - hi.astype(F32)).astype(BF16)

    x_hi, x_lo = split(xn2)
    w_hi, w_lo = split(wr_ref[...])
    logits = (jnp.dot(x_hi, w_hi, preferred_element_type=F32) + jnp.dot(x_lo, w_hi, preferred_element_type=F32)
              + jnp.dot(x_hi, w_lo, preferred_element_type=F32)) + br_ref[...]
    work = jnp.where(lane < N_EXPERTS, logits, NEG_INF)
    lane_f = lane.astype(F32)
    vals, idxs = [], []
    for _ in range(EXPERT_TOPK):
        vmax = jnp.max(work, axis=1, keepdims=True)
        first = jnp.min(jnp.where(work == vmax, lane_f, float(LANES)), axis=1, keepdims=True)
        vals.append(vmax)
        idxs.append(first)
        work = jnp.where(lane_f == first, NEG_INF, work)
    exps = [jnp.exp(v - vals[0]) for v in vals]
    denom = exps[0]
    for e in exps[1:]:
        denom = denom + e
    chosen = jnp.zeros((tm, LANES), F32)
    for first in idxs:
        chosen = jnp.where(lane_f == first, 1.0, chosen)

    strict_lower = (lax.broadcasted_iota(jnp.int32, (tm, tm), 0)
                    > lax.broadcasted_iota(jnp.int32, (tm, tm), 1))
    before = jnp.dot(jnp.where(strict_lower, 1.0, 0.0).astype(BF16), chosen.astype(BF16),
                     preferred_element_type=F32)
    counts = jnp.broadcast_to(jnp.sum(chosen, axis=0, keepdims=True), tcnt_ref.shape)
    lower_expert = (lax.broadcasted_iota(jnp.int32, (LANES, LANES), 0)
                    < lax.broadcasted_iota(jnp.int32, (LANES, LANES), 1))
    expert_start = jnp.dot(counts, jnp.where(lower_expert, 1.0, 0.0), precision=lax.Precision.HIGHEST,
                           preferred_element_type=F32)[0:1, :]
    sorted_pos = before + expert_start
    ri = jnp.zeros((tm, LANES), jnp.int32)
    rw = jnp.zeros((tm, LANES), F32)
    for kk in range(EXPERT_TOPK):
        pos = jnp.sum(jnp.where(lane_f == idxs[kk], sorted_pos, 0.0), axis=1, keepdims=True)
        ri = jnp.where(lane == kk, idxs[kk].astype(jnp.int32), ri)
        ri = jnp.where(lane == EXPERT_TOPK + kk, pos.astype(jnp.int32), ri)
        rw = jnp.where(lane == kk, exps[kk] / denom, rw)
    ri_ref[...] = ri
    rw_ref[...] = rw
    tcnt_ref[...] = counts


def _mix(attn, u, vvn, ga, gs, x2, wao, wso, wmo, wsp, bsp_full, fg, wr_pad, br_pad):
    t, d = x2.shape
    tm = ROUTE_TOKENS
    row = lambda w: pl.BlockSpec((tm, w), lambda i: (i, 0))
    const = lambda shape: pl.BlockSpec(shape, lambda i: (0,) * len(shape))
    ins = (attn, u, vvn, ga, gs, x2, wao, wso, wmo, wsp, bsp_full, fg, wr_pad, br_pad)
    in_specs = [row(a.shape[1]) for a in ins[:6]] + [const(a.shape) for a in ins[6:]]
    out_shape = (
        jax.ShapeDtypeStruct((t, d), F32),
        jax.ShapeDtypeStruct((t * SUBLANES, LANES), F32),
        jax.ShapeDtypeStruct((t, LANES), jnp.int32),
        jax.ShapeDtypeStruct((t, LANES), F32),
        jax.ShapeDtypeStruct((t // tm * SUBLANES, LANES), F32),
    )
    assert d == SUBLANES * LANES, "a token row is stored as one (8, 128) tile"
    token_rows = pl.BlockSpec((tm * SUBLANES, LANES), lambda i: (i, 0))
    out_specs = (row(d), token_rows, row(LANES), row(LANES), pl.BlockSpec((SUBLANES, LANES), lambda i: (i, 0)))
    return pl.pallas_call(
        _mix_kernel,
        grid=(t // tm,),
        in_specs=in_specs,
        out_specs=out_specs,
        out_shape=out_shape,
        scratch_shapes=[pltpu.VMEM(wsp.shape, BF16)],
        compiler_params=pltpu.CompilerParams(dimension_semantics=("arbitrary",), vmem_limit_bytes=BIG_VMEM_LIMIT),
        name="mix_route",
    )(*ins)


def _run_pieces(n, body):
    off = jnp.int32(0)
    for bit in reversed(range(ROUTE_TOKENS.bit_length())):
        size = 1 << bit
        take = lax.bitwise_and(n, size) != 0

        @pl.when(take)
        def _(off=off, size=size):
            body(off, size)

        off = off + jnp.where(take, size, 0)


def _token_rows(ref, r, n):
    return ref.at[pl.ds(pl.multiple_of(r * SUBLANES, SUBLANES), n * SUBLANES)]


def _dispatch_kernel(len_ref, tpos_ref, dst_ref, padlo_ref, padlen_ref, nused_ref, pos_ref, x_ref,
                     xs_hbm, sorted_ref, zeros_ref, sems, zsem):
    step = pl.program_id(0)
    n_steps = pl.num_programs(0)
    n_assign = pos_ref.shape[0]
    tile_rows = zeros_ref.shape[0] // SUBLANES
    slot = lax.rem(step, 2)

    def start_run(tile, e, buf):
        k = tile * N_EXPERTS + e

        def piece(off, size):
            pltpu.make_async_copy(_token_rows(sorted_ref.at[buf], tpos_ref[k] + off, size),
                                  _token_rows(xs_hbm, dst_ref[k] + off, size), sems.at[buf]).start()

        _run_pieces(len_ref[k], piece)

    def wait_runs(buf):
        whole = sorted_ref.at[buf]
        pltpu.make_async_copy(whole, _token_rows(xs_hbm, 0, n_assign), sems.at[buf]).wait()

    @pl.when(step == 0)
    def _():
        zeros_ref[...] = jnp.zeros_like(zeros_ref)

        def tail_tile(tl, carry):
            cp = pltpu.make_async_copy(zeros_ref, _token_rows(xs_hbm, tl * tile_rows, tile_rows), zsem)
            cp.start()
            cp.wait()
            return carry

        lax.fori_loop(nused_ref[0], xs_hbm.shape[0] // (tile_rows * SUBLANES), tail_tile, 0)

        def per_expert(e, carry):
            def piece(off, size):
                cp = pltpu.make_async_copy(_token_rows(zeros_ref, 0, size),
                                           _token_rows(xs_hbm, padlo_ref[e] + off, size), zsem)
                cp.start()
                cp.wait()

            _run_pieces(padlen_ref[e], piece)
            return carry

        lax.fori_loop(0, N_EXPERTS, per_expert, 0)

    @pl.when(step >= 2)
    def _():
        wait_runs(slot)

    per_slice = n_assign // N_EXPERTS

    def fill_slice(c, carry):
        def place(k, cr):
            a = c * per_slice + k
            tok = lax.shift_right_logical(a, 2)
            pos = pos_ref[a]
            sorted_ref[slot, pl.ds(pl.multiple_of(pos * SUBLANES, SUBLANES), SUBLANES), :] = (
                x_ref[pl.ds(pl.multiple_of(tok * SUBLANES, SUBLANES), SUBLANES), :])
            return cr

        lax.fori_loop(0, per_slice, place, 0, unroll=8)

        @pl.when(step >= 1)
        def _():
            start_run(step - 1, c, 1 - slot)

        return carry

    lax.fori_loop(0, N_EXPERTS, fill_slice, 0)

    @pl.when(step == n_steps - 1)
    def _():
        def last_runs(e, carry):
            start_run(step, e, slot)
            return carry

        lax.fori_loop(0, N_EXPERTS, last_runs, 0)

        @pl.when(step >= 1)
        def _():
            wait_runs(1 - slot)

        wait_runs(slot)


def _dispatch(run_len, run_tpos, run_dst, padlo, padlen, n_used, pos, xn2_rows, n_rows):
    t = xn2_rows.shape[0] // SUBLANES
    n_assign = ROUTE_TOKENS * EXPERT_TOPK
    smem_blk = pl.BlockSpec((n_assign,), lambda c, *_: (c,), memory_space=pltpu.SMEM)
    grid_spec = pltpu.PrefetchScalarGridSpec(
        num_scalar_prefetch=6,
        grid=(t // ROUTE_TOKENS,),
        in_specs=[smem_blk, pl.BlockSpec((ROUTE_TOKENS * SUBLANES, LANES), lambda c, *_: (c, 0))],
        out_specs=pl.BlockSpec(memory_space=pl.ANY),
        scratch_shapes=[
            pltpu.VMEM((2, n_assign * SUBLANES, LANES), F32),
            pltpu.VMEM((EXPERT_ROWS * SUBLANES, LANES), F32),
            pltpu.SemaphoreType.DMA((2,)),
            pltpu.SemaphoreType.DMA(()),
        ],
    )
    return pl.pallas_call(
        _dispatch_kernel,
        grid_spec=grid_spec,
        out_shape=jax.ShapeDtypeStruct((n_rows * SUBLANES, LANES), F32),
        compiler_params=pltpu.CompilerParams(dimension_semantics=("arbitrary",), vmem_limit_bytes=VMEM_LIMIT),
        name="dispatch",
    )(run_len, run_tpos, run_dst, padlo, padlen, n_used, pos, xn2_rows)


def _expert_kernel(be_ref, nused_ref, x_ref, wgu_ref, bg_ref, bu_ref, wd_ref, bd_ref, y_ref, wgu_bf, wd_bf):
    t = pl.program_id(0)
    rows = y_ref.shape[0] // SUBLANES
    de = wd_ref.shape[0]
    grp = 2 * LANES
    active = t < nused_ref[0]
    fresh = jnp.logical_and(
        active, jnp.logical_or(t == 0, be_ref[t] != be_ref[jnp.maximum(t - 1, 0)]))

    @pl.when(fresh)
    def _():
        src = lax.broadcasted_iota(jnp.int32, (grp, grp), 0)
        dst = lax.broadcasted_iota(jnp.int32, (grp, grp), 1)
        wanted = jnp.where(dst < LANES, 2 * dst, 2 * (dst - LANES) + 1)
        perm = jnp.where(src == wanted, 1.0, 0.0).astype(BF16)
        for g in range(wgu_ref.shape[1] // grp):
            cols = slice(g * grp, (g + 1) * grp)
            wgu_bf[:, cols] = jnp.dot(wgu_ref[:, cols].astype(BF16), perm,
                                      preferred_element_type=F32).astype(BF16)
        wd_bf[...] = wd_ref[...].astype(BF16)

    @pl.when(active)
    def _():
        xb = _load_token_rows(x_ref, rows).astype(BF16)
        gu = jnp.dot(xb, wgu_bf[...], preferred_element_type=F32)
        hid = []
        for g in range(de // LANES):
            cols = slice(g * LANES, (g + 1) * LANES)
            gate = gu[:, g * grp:g * grp + LANES] + bg_ref[:, cols]
            up = gu[:, g * grp + LANES:(g + 1) * grp] + bu_ref[:, cols]
            gate = jnp.minimum(gate, SWIGLU_LIMIT)
            up = jnp.clip(up, -SWIGLU_LIMIT, SWIGLU_LIMIT)
            hid.append((gate * _sigmoid(SWIGLU_ALPHA * gate) * (up + 1.0)).astype(BF16))
        y = jnp.dot(jnp.concatenate(hid, axis=1), wd_bf[...], preferred_element_type=F32) + bd_ref[...]
        _store_token_rows(y_ref, y)

    @pl.when(jnp.logical_not(active))
    def _():
        y_ref[...] = jnp.zeros_like(y_ref)


def _experts(block_expert, n_used, xs_rows, wgu, bg, bu, wd, bd):
    n_rows = xs_rows.shape[0] // SUBLANES
    _, d, de2 = wgu.shape
    de = wd.shape[1]
    n_tiles = n_rows // EXPERT_ROWS
    tile_rows = EXPERT_ROWS * SUBLANES
    per_expert = lambda t, be, nu: (be[t], 0, 0)
    grid_spec = pltpu.PrefetchScalarGridSpec(
        num_scalar_prefetch=2,
        grid=(n_tiles,),
        in_specs=[
            pl.BlockSpec((tile_rows, LANES), lambda t, be, nu: (jnp.minimum(t, jnp.maximum(nu[0] - 1, 0)), 0)),
            pl.BlockSpec((None, d, de2), per_expert),
            pl.BlockSpec((None, 1, de), per_expert),
            pl.BlockSpec((None, 1, de), per_expert),
            pl.BlockSpec((None, de, d), per_expert),
            pl.BlockSpec((None, 1, d), per_expert),
        ],
        out_specs=pl.BlockSpec((tile_rows, LANES), lambda t, be, nu: (t, 0)),
        scratch_shapes=[pltpu.VMEM((d, de2), BF16), pltpu.VMEM((de, d), BF16)],
    )
    return pl.pallas_call(
        _expert_kernel,
        grid_spec=grid_spec,
        out_shape=jax.ShapeDtypeStruct((n_rows * SUBLANES, LANES), F32),
        compiler_params=pltpu.CompilerParams(
            dimension_semantics=("arbitrary",), vmem_limit_bytes=BIG_VMEM_LIMIT),
        name="experts",
    )(block_expert, n_used, xs_rows, wgu, bg, bu, wd, bd)


def _combine_kernel(len_ref, tpos_ref, src_ref, pos_ref, w_ref, h1_ref, fg_ref, ys_hbm,
                    o_ref, buf_ref, h1rows_ref, outrows_ref, sems):
    step = pl.program_id(0)
    n_steps = pl.num_programs(0)
    tt = h1_ref.shape[0]
    slot = lax.rem(step, 2)

    def start_run(tile, e, buf):
        k = tile * N_EXPERTS + e

        def piece(off, size):
            pltpu.make_async_copy(_token_rows(ys_hbm, src_ref[k] + off, size),
                                  _token_rows(buf_ref.at[buf], tpos_ref[k] + off, size), sems.at[buf]).start()

        _run_pieces(len_ref[k], piece)

    def wait_runs(buf):
        whole = buf_ref.at[buf]
        pltpu.make_async_copy(_token_rows(ys_hbm, 0, tt * EXPERT_TOPK), whole, sems.at[buf]).wait()

    @pl.when(step == 0)
    def _():
        def first_runs(e, carry):
            start_run(0, e, 0)
            return carry

        lax.fori_loop(0, N_EXPERTS, first_runs, 0)

    wait_runs(slot)

    _store_token_rows(h1rows_ref, h1_ref[...])
    per_slice = tt // N_EXPERTS

    def combine_slice(c, carry):
        def one_token(k, cr):
            tok = c * per_slice + k
            acc = h1rows_ref[pl.ds(pl.multiple_of(tok * SUBLANES, SUBLANES), SUBLANES), :]
            for kk in range(EXPERT_TOPK):
                a = tok * EXPERT_TOPK + kk
                pos = pos_ref[a]
                acc = acc + w_ref[a] * buf_ref[slot, pl.ds(pl.multiple_of(pos * SUBLANES, SUBLANES), SUBLANES), :]
            outrows_ref[pl.ds(pl.multiple_of(tok * SUBLANES, SUBLANES), SUBLANES), :] = acc
            return cr

        lax.fori_loop(0, per_slice, one_token, 0, unroll=4)

        @pl.when(step + 1 < n_steps)
        def _():
            start_run(step + 1, c, 1 - slot)

        return carry

    lax.fori_loop(0, N_EXPERTS, combine_slice, 0)
    h = _load_token_rows(outrows_ref, tt)
    o_ref[...] = h * lax.rsqrt(jnp.mean(h * h, axis=-1, keepdims=True) + NORM_EPS) * fg_ref[...]


def _combine(run_len, run_tpos, run_src, pos, w_flat, h1, fg, ys_rows):
    t, d = h1.shape
    tt = ROUTE_TOKENS
    n_assign = tt * EXPERT_TOPK
    smem_blk = pl.BlockSpec((n_assign,), lambda c, *_: (c,), memory_space=pltpu.SMEM)
    grid_spec = pltpu.PrefetchScalarGridSpec(
        num_scalar_prefetch=3,
        grid=(t // tt,),
        in_specs=[
            smem_blk, smem_blk,
            pl.BlockSpec((tt, d), lambda c, *_: (c, 0)),
            pl.BlockSpec((1, d), lambda c, *_: (0, 0)),
            pl.BlockSpec(memory_space=pl.ANY),
        ],
        out_specs=pl.BlockSpec((tt, d), lambda c, *_: (c, 0)),
        scratch_shapes=[
            pltpu.VMEM((2, n_assign * SUBLANES, LANES), F32),
            pltpu.VMEM((tt * SUBLANES, LANES), F32),
            pltpu.VMEM((tt * SUBLANES, LANES), F32),
            pltpu.SemaphoreType.DMA((2,)),
        ],
    )
    return pl.pallas_call(
        _combine_kernel,
        grid_spec=grid_spec,
        out_shape=jax.ShapeDtypeStruct((t, d), F32),
        compiler_params=pltpu.CompilerParams(dimension_semantics=("arbitrary",), vmem_limit_bytes=VMEM_LIMIT),
        name="combine",
    )(run_len, run_tpos, run_src, pos, w_flat, h1, fg, ys_rows)


def _layer(h, mix_norm_g, w_in, w_attn_out, sgu_ln_g, sgu_ln_b, w_spatial, b_spatial, w_sgu_out,
           w_mix_out, ffn_norm_g, w_router, b_router, w_gate_up, b_gate_up, w_down, b_down, out_g):
    b, s, d = h.shape
    t = b * s
    sgu_width = sgu_ln_g.shape[0]
    x2 = h.reshape(t, d)

    q, qt, kt, v, kmean, u, vvn, ga, gs = _in_proj(
        x2, mix_norm_g.reshape(1, d), w_in.astype(BF16), sgu_ln_g.reshape(1, sgu_width),
        sgu_ln_b.reshape(1, sgu_width), sgu_width)

    nb = s // MOBA_BLOCK
    attn = _moba(q.reshape(b, s, ATTN_WIDTH), qt, kt, v.reshape(b, s, ATTN_WIDTH),
                 kmean.reshape(b, nb, ATTN_WIDTH), _moba_key_table(s)).reshape(t, ATTN_WIDTH)

    gdim = sgu_width // SGU_GROUPS
    bsp_full = jnp.repeat(b_spatial.T, gdim, axis=1)
    wr_pad = jnp.zeros((d, LANES), F32).at[:, :N_EXPERTS].set(w_router)
    br_pad = jnp.zeros((1, LANES), F32).at[0, :N_EXPERTS].set(b_router)
    h1, xn2_rows, ri, rw, tile_cnt = _mix(
        attn, u, vvn, ga, gs, x2, w_attn_out.astype(BF16), w_sgu_out.astype(BF16), w_mix_out.astype(BF16),
        w_spatial, bsp_full, ffn_norm_g.reshape(1, d), wr_pad, br_pad)

    n_assign = t * EXPERT_TOPK
    n_tiles = -(-(n_assign + N_EXPERTS * (EXPERT_ROWS - 1)) // EXPERT_ROWS)
    n_rows = n_tiles * EXPERT_ROWS
    run_len = tile_cnt[::SUBLANES, :N_EXPERTS].astype(jnp.int32)
    counts = jnp.sum(run_len, axis=0)
    padded = (counts + EXPERT_ROWS - 1) // EXPERT_ROWS * EXPERT_ROWS
    pad_end = jnp.cumsum(padded)
    starts = pad_end - padded
    run_tpos = jnp.cumsum(run_len, axis=1) - run_len
    run_row = starts[None, :] + jnp.cumsum(run_len, axis=0) - run_len
    tile_start = jnp.arange(n_tiles, dtype=jnp.int32) * EXPERT_ROWS
    block_expert = jnp.minimum(
        jnp.sum((pad_end[None, :] <= tile_start[:, None]).astype(jnp.int32), axis=1), N_EXPERTS - 1)
    n_used = (pad_end[-1:] // EXPERT_ROWS).astype(jnp.int32)
    pos = ri[:, EXPERT_TOPK:2 * EXPERT_TOPK].reshape(n_assign)
    tables = (run_len.reshape(-1), run_tpos.reshape(-1), run_row.reshape(-1))

    xs_rows = _dispatch(*tables, starts + counts, padded - counts, n_used, pos, xn2_rows, n_rows)

    de = w_down.shape[1]
    bg = b_gate_up[:, 0::2].reshape(N_EXPERTS, 1, de)
    bu = b_gate_up[:, 1::2].reshape(N_EXPERTS, 1, de)
    ys_rows = _experts(block_expert, n_used, xs_rows, w_gate_up, bg, bu, w_down, b_down.reshape(N_EXPERTS, 1, d))

    out = _combine(*tables, pos, rw[:, :EXPERT_TOPK].reshape(n_assign), h1, out_g.reshape(1, d), ys_rows)
    return out.reshape(b, s, d)


def kernel(x, mix_norm_g, w_in, w_attn_out, sgu_ln_g, sgu_ln_b, w_spatial, b_spatial, w_sgu_out, w_mix_out,
           ffn_norm_g, w_router, b_router, w_gate_up, b_gate_up, w_down, b_down, final_norm_g):
    depth = w_in.shape[0]
    assert depth == 1, "the final RMSNorm is fused into the single layer's combine step"
    return _layer(x, mix_norm_g[0], w_in[0], w_attn_out[0], sgu_ln_g[0], sgu_ln_b[0], w_spatial[0],
                  b_spatial[0], w_sgu_out[0], w_mix_out[0], ffn_norm_g[0], w_router[0], b_router[0],
                  w_gate_up[0], b_gate_up[0], w_down[0], b_down[0], final_norm_g)
```

```python
import jax
import jax.numpy as jnp
import numpy as np
from jax import lax
from jax.experimental import pallas as pl
from jax.experimental.pallas import tpu as pltpu

F32 = jnp.float32
BF16 = jnp.bfloat16
NEG_INF = float("-inf")
MASK_VALUE = -1e30


def _bf16_pieces(x, n=3):
    pieces = []
    for _ in range(n):
        p = float(np.asarray(x, np.float32).astype(BF16).astype(np.float32))
        pieces.append(p)
        x = x - p
    return pieces


LOG2E = 1.4426950408889634
LOG2E_PIECES = _bf16_pieces(LOG2E)

N_HEADS = 8
HEAD_DIM = 64
ATTN_WIDTH = N_HEADS * HEAD_DIM
MOBA_BLOCK = 256
MOBA_TOPK = 3
SGU_CHUNK = 128
SGU_GROUPS = 8
N_EXPERTS = 32
EXPERT_TOPK = 4
SWIGLU_LIMIT = 7.0
SWIGLU_ALPHA = 1.702
NORM_EPS = 1e-5

LANES = 128
SUBLANES = 8
HEADS_PER_LANE_TILE = LANES // HEAD_DIM
MOBA_GROUP = 1
MOBA_BIAS_LANE0 = 64
MOBA_DUMMY_LANE = LANES - 1
PROJ_ROWS = 256
EXPERT_ROWS = 512
ROUTE_TOKENS = 512
VMEM_LIMIT = 48 * 1024 * 1024
BIG_VMEM_LIMIT = 56 * 1024 * 1024


def _sigmoid(x):
    return 1.0 / (1.0 + jnp.exp(-x))


def _store_token_rows(ref, value, base=0):
    n = value.shape[0]
    for c in range(value.shape[1] // LANES):
        ref[pl.ds(base + c, n, stride=SUBLANES), :] = value[:, c * LANES:(c + 1) * LANES]


def _load_token_rows(ref, n, base=0):
    return jnp.concatenate([ref[pl.ds(base + c, n, stride=SUBLANES), :] for c in range(SUBLANES)], axis=1)


def _gelu_exact(x):
    return 0.5 * x * (1.0 + lax.erf(x * (0.5 ** 0.5)))


def _in_proj_kernel(x_ref, g_ref, w_ref, lng_ref, lnb_ref,
                    q_ref, qt_ref, kt_ref, v_ref, kmean_ref, u_ref, vvn_ref, ga_ref, gs_ref):
    x = x_ref[...]
    xn = x * lax.rsqrt(jnp.mean(x * x, axis=-1, keepdims=True) + NORM_EPS) * g_ref[...]
    xb = xn.astype(BF16)

    def proj(lo, hi):
        return jnp.dot(xb, w_ref[:, lo:hi], preferred_element_type=F32)

    a = ATTN_WIDTH
    sw = u_ref.shape[1]
    d = ga_ref.shape[1]
    qf = proj(0, a)
    q_ref[...] = qf
    kf = proj(a, 2 * a)
    for j in range(kf.shape[0] // MOBA_BLOCK):
        qt_ref[j] = qf[j * MOBA_BLOCK:(j + 1) * MOBA_BLOCK].T
        kblk = kf[j * MOBA_BLOCK:(j + 1) * MOBA_BLOCK]
        kt_ref[j] = kblk.T.astype(BF16)
        kmean_ref[j] = jnp.mean(kblk, axis=0, keepdims=True)
    v_ref[...] = proj(2 * a, 3 * a).astype(BF16)
    z0 = 3 * a
    u_ref[...] = _gelu_exact(proj(z0, z0 + sw))
    zv = _gelu_exact(proj(z0 + sw, z0 + 2 * sw))
    mu = jnp.mean(zv, axis=-1, keepdims=True)
    zc = zv - mu
    var = jnp.mean(zc * zc, axis=-1, keepdims=True)
    vvn_ref[...] = zc * lax.rsqrt(var + NORM_EPS) * lng_ref[...] + lnb_ref[...]
    g0 = z0 + 2 * sw
    ga_ref[...] = _sigmoid(proj(g0, g0 + d))
    gs_ref[...] = _sigmoid(proj(g0 + d, g0 + 2 * d))


def _in_proj(x2, g, w_bf, lng, lnb, sgu_width):
    t, d = x2.shape
    tm = PROJ_ROWS
    n_in = w_bf.shape[1]
    row = lambda w: pl.BlockSpec((tm, w), lambda i: (i, 0))
    const = lambda shape: pl.BlockSpec(shape, lambda i: (0,) * len(shape))
    out_shape = (
        jax.ShapeDtypeStruct((t, ATTN_WIDTH), F32),
        jax.ShapeDtypeStruct((t // MOBA_BLOCK, ATTN_WIDTH, MOBA_BLOCK), F32),
        jax.ShapeDtypeStruct((t // MOBA_BLOCK, ATTN_WIDTH, MOBA_BLOCK), BF16),
        jax.ShapeDtypeStruct((t, ATTN_WIDTH), BF16),
        jax.ShapeDtypeStruct((t // MOBA_BLOCK, 1, ATTN_WIDTH), F32),
        jax.ShapeDtypeStruct((t, sgu_width), F32),
        jax.ShapeDtypeStruct((t, sgu_width), F32),
        jax.ShapeDtypeStruct((t, d), F32),
        jax.ShapeDtypeStruct((t, d), F32),
    )
    out_specs = (
        row(ATTN_WIDTH),
        pl.BlockSpec((tm // MOBA_BLOCK, ATTN_WIDTH, MOBA_BLOCK), lambda i: (i, 0, 0)),
        pl.BlockSpec((tm // MOBA_BLOCK, ATTN_WIDTH, MOBA_BLOCK), lambda i: (i, 0, 0)),
        row(ATTN_WIDTH),
        pl.BlockSpec((tm // MOBA_BLOCK, 1, ATTN_WIDTH), lambda i: (i, 0, 0)),
        row(sgu_width), row(sgu_width), row(d), row(d),
    )
    return pl.pallas_call(
        _in_proj_kernel,
        grid=(t // tm,),
        in_specs=[row(d), const((1, d)), const((d, n_in)), const((1, sgu_width)), const((1, sgu_width))],
        out_specs=out_specs,
        out_shape=out_shape,
        compiler_params=pltpu.CompilerParams(dimension_semantics=("arbitrary",), vmem_limit_bytes=VMEM_LIMIT),
        name="in_proj",
    )(x2, g, w_bf, lng, lnb)


def _alibi_slopes():
    slopes = 2.0 ** (-8.0 * np.arange(1, N_HEADS + 1, dtype=np.float64) / N_HEADS)
    assert all(np.log2(s) == np.round(np.log2(s)) for s in slopes), "ALiBi slopes must be powers of two"
    return slopes


def _moba_bias_lane(head_in_tile, part, piece):
    return MOBA_BIAS_LANE0 + (head_in_tile * 2 + part) * len(LOG2E_PIECES) + piece


def _moba_key_table(s):
    nb = s // MOBA_BLOCK
    slopes = _alibi_slopes()
    n_pairs = N_HEADS // HEADS_PER_LANE_TILE
    table = np.zeros((nb + 1, n_pairs, LANES, MOBA_BLOCK), np.float32)
    offs = np.arange(MOBA_BLOCK, dtype=np.float32)
    assert HEADS_PER_LANE_TILE * nb <= MOBA_BIAS_LANE0, "block one-hot rows must not overlap the ALiBi rows"
    for j in range(nb):
        for hh in range(HEADS_PER_LANE_TILE):
            table[j, :, hh * nb + j, :] = 1.0
        for p in range(n_pairs):
            for hh in range(HEADS_PER_LANE_TILE):
                slope = slopes[p * HEADS_PER_LANE_TILE + hh]
                for piece in range(len(LOG2E_PIECES)):
                    table[j, p, _moba_bias_lane(hh, 0, piece), :] = slope * MOBA_BLOCK * j
                    table[j, p, _moba_bias_lane(hh, 1, piece), :] = slope * offs
    table[nb, :, MOBA_DUMMY_LANE, :] = 1.0
    as_bf16 = table.astype(BF16)
    assert np.array_equal(as_bf16.astype(np.float32), table), "bias table must be exact in bf16"
    return jnp.asarray(as_bf16.reshape(nb + 1, n_pairs * LANES, MOBA_BLOCK))


def _moba_kernel(q_ref, qt_ref, qn_ref, qtn_ref, kt_ref, v_ref, km_ref, ct_ref, causal_ref, o_ref, qaug_ref,
                 s_ref, mpart_ref, mrow_ref, acc_ref):
    i = pl.program_id(2)
    nb = kt_ref.shape[0]
    blk = MOBA_BLOCK
    lane = lax.broadcasted_iota(jnp.int32, (1, LANES), 1)
    n_groups = lax.div(i + MOBA_GROUP, MOBA_GROUP)
    heads = range(HEADS_PER_LANE_TILE)
    hmasks = [(lane >= HEAD_DIM * hh) & (lane < HEAD_DIM * (hh + 1)) for hh in heads]
    slot = lax.rem(i, 2)

    def build_query_operand(q, qt, tile, into):
        bid = lax.broadcasted_iota(jnp.int32, (nb, blk), 0)
        bid_f = bid.astype(F32)
        masks_t = []
        for hh in heads:
            km_h = jnp.where(hmasks[hh], km_ref[...], 0.0)
            gate = jnp.dot(km_h, qt, precision=lax.Precision.HIGHEST, preferred_element_type=F32)
            gate = jnp.where(bid < tile, gate, NEG_INF)
            blockmask = jnp.where(bid == tile, 0.0, MASK_VALUE)
            for kk in range(MOBA_TOPK):
                gmax = jnp.max(gate, axis=0, keepdims=True)
                first = jnp.min(jnp.where(gate == gmax, bid_f, float(nb)), axis=0, keepdims=True)
                valid = (tile > kk).astype(F32)
                first = first * valid + (valid - 1.0)
                hit = bid_f == first
                blockmask = jnp.where(hit, 0.0, blockmask)
                gate = jnp.where(hit, NEG_INF, gate)
            masks_t.append(blockmask)
        masks_t.append(jnp.zeros((LANES - len(masks_t) * nb, blk), F32))
        masks = jnp.concatenate(masks_t, axis=0).T

        for hh in heads:
            feats = jnp.where(lane == MOBA_DUMMY_LANE, MASK_VALUE, 0.0)
            for part in range(2):
                for piece, value in enumerate(LOG2E_PIECES):
                    feats = jnp.where(lane == _moba_bias_lane(hh, part, piece), value, feats)
            extra = jnp.where((lane >= hh * nb) & (lane < (hh + 1) * nb), masks, feats)
            qh = jnp.where(hmasks[hh], q, 0.0)
            qaug_ref[into, hh * blk:(hh + 1) * blk, 0:LANES] = (qh * (HEAD_DIM ** -0.5 * LOG2E)).astype(BF16)
            qaug_ref[into, hh * blk:(hh + 1) * blk, LANES:2 * LANES] = extra.astype(BF16)

    @pl.when(i == 0)
    def _():
        build_query_operand(q_ref[...], qt_ref[...], i, slot)

    def scores(jk, jc):
        rhs = jnp.concatenate([kt_ref[jk], ct_ref[jc]], axis=0)
        return jnp.dot(qaug_ref[slot], rhs, preferred_element_type=F32)

    def lane_halves_max(s):
        return jnp.maximum(s[:, 0:LANES], s[:, LANES:2 * LANES])

    n_rows = len(heads) * blk
    variants = [(groups, min(groups * MOBA_GROUP, nb)) for groups in range(1, pl.cdiv(nb, MOBA_GROUP) + 1)]

    for groups, n_blocks in variants:
        @pl.when(n_groups == groups)
        def _(n_blocks=n_blocks):
            part = None
            for j in range(n_blocks):
                s = scores(j, jnp.where(j <= i, j, nb)) + causal_ref[(i == j).astype(jnp.int32)]
                s_ref[j] = s
                part = lane_halves_max(s) if part is None else jnp.maximum(part, lane_halves_max(s))
            mpart_ref[...] = part

    ones = jnp.ones((blk, LANES), BF16)
    rowmax = jnp.max(mpart_ref[...], axis=1, keepdims=True)
    mrow_ref[...] = jnp.broadcast_to(rowmax, (n_rows, 2 * LANES))

    def weighted(j):
        p = jnp.exp2(s_ref[j] - mrow_ref[...]).astype(BF16)
        vaug = jnp.concatenate([v_ref[j * blk:(j + 1) * blk, :], ones], axis=1)
        return jnp.dot(p, vaug, preferred_element_type=F32)

    next_tile = jnp.minimum(i + 1, nb - 1)
    for groups, n_blocks in variants:
        @pl.when(n_groups == groups)
        def _(n_blocks=n_blocks):
            build_query_operand(qn_ref[...], qtn_ref[...], next_tile, 1 - slot)
            tot = weighted(0)
            for j in range(1, n_blocks):
                tot = tot + weighted(j)
            acc_ref[...] = tot

    out = jnp.zeros((blk, LANES), F32)
    for hh in heads:
        acc = acc_ref[hh * blk:(hh + 1) * blk, :]
        out = jnp.where(hmasks[hh], acc[:, 0:LANES] / acc[:, LANES:2 * LANES], out)
    o_ref[...] = out


def _moba(q, qt, kt, v, kmean, key_table):
    b, s, a = q.shape
    nb = s // MOBA_BLOCK
    n_pairs = a // LANES
    rows = HEADS_PER_LANE_TILE * MOBA_BLOCK
    row_in_block = np.arange(rows)[:, None] % MOBA_BLOCK
    causal = np.where(row_in_block >= np.arange(MOBA_BLOCK)[None, :], 0.0, MASK_VALUE).astype(np.float32)
    causal_tiles = jnp.asarray(np.stack([np.zeros_like(causal), causal]))
    nxt = lambda i: jnp.minimum(i + 1, nb - 1)
    return pl.pallas_call(
        _moba_kernel,
        grid=(b, n_pairs, nb),
        in_specs=[
            pl.BlockSpec((None, MOBA_BLOCK, LANES), lambda bi, p, i: (bi, i, p)),
            pl.BlockSpec((None, LANES, MOBA_BLOCK), lambda bi, p, i: (bi * nb + i, p, 0)),
            pl.BlockSpec((None, MOBA_BLOCK, LANES), lambda bi, p, i: (bi, nxt(i), p)),
            pl.BlockSpec((None, LANES, MOBA_BLOCK), lambda bi, p, i: (bi * nb + nxt(i), p, 0)),
            pl.BlockSpec((nb, LANES, MOBA_BLOCK), lambda bi, p, i: (bi, p, 0)),
            pl.BlockSpec((None, s, LANES), lambda bi, p, i: (bi, 0, p)),
            pl.BlockSpec((None, nb, LANES), lambda bi, p, i: (bi, 0, p)),
            pl.BlockSpec((nb + 1, LANES, MOBA_BLOCK), lambda bi, p, i: (0, p, 0)),
            pl.BlockSpec((2, rows, MOBA_BLOCK), lambda bi, p, i: (0, 0, 0)),
        ],
        out_specs=pl.BlockSpec((None, MOBA_BLOCK, LANES), lambda bi, p, i: (bi, i, p)),
        out_shape=jax.ShapeDtypeStruct((b, s, a), F32),
        scratch_shapes=[
            pltpu.VMEM((2, rows, 2 * LANES), BF16),
            pltpu.VMEM((nb + 1, rows, MOBA_BLOCK), F32),
            pltpu.VMEM((rows, LANES), F32),
            pltpu.VMEM((rows, 2 * LANES), F32),
            pltpu.VMEM((rows, 2 * LANES), F32),
        ],
        compiler_params=pltpu.CompilerParams(
            dimension_semantics=("arbitrary", "arbitrary", "arbitrary"), vmem_limit_bytes=VMEM_LIMIT),
        name="moba",
    )(q, qt, q, qt, kt, v, kmean, key_table, causal_tiles)


def _mix_kernel(attn_ref, u_ref, vvn_ref, ga_ref, gs_ref, x_ref,
                wao_ref, wso_ref, wmo_ref, wsp_ref, bsp_ref, fg_ref, wr_ref, br_ref,
                h1_ref, xn2_ref, ri_ref, rw_ref, tcnt_ref, post_ref, wcausal_ref):
    tm = x_ref.shape[0]
    ch = SGU_CHUNK

    @pl.when(pl.program_id(0) == 0)
    def _():
        tril = lax.broadcasted_iota(jnp.int32, (ch, ch), 0) >= lax.broadcasted_iota(jnp.int32, (ch, ch), 1)
        for g in range(SGU_GROUPS):
            wcausal_ref[g] = jnp.where(tril, wsp_ref[g], 0.0).astype(BF16)

    y_attn = jnp.dot(attn_ref[...].astype(BF16), wao_ref[...], preferred_element_type=F32)

    lane = lax.broadcasted_iota(jnp.int32, (1, LANES), 1)
    gdim = vvn_ref.shape[1] // SGU_GROUPS
    groups_per_tile = LANES // gdim
    w_causal = [wcausal_ref[g] for g in range(SGU_GROUPS)]
    rows = []
    for c in range(tm // ch):
        cols = []
        for ct in range(vvn_ref.shape[1] // LANES):
            vp = vvn_ref[c * ch:(c + 1) * ch, ct * LANES:(ct + 1) * LANES]
            acc = jnp.zeros((ch, LANES), F32)
            for gg in range(groups_per_tile):
                gmask = (lane >= gdim * gg) & (lane < gdim * (gg + 1))
                vm = jnp.where(gmask, vp, 0.0).astype(BF16)
                acc = acc + jnp.dot(w_causal[ct * groups_per_tile + gg], vm, preferred_element_type=F32)
            cols.append(acc)
        rows.append(jnp.concatenate(cols, axis=1) + bsp_ref[...])
    mixed = jnp.concatenate(rows, axis=0)
    sgu = u_ref[...] * mixed
    y_sgu = jnp.dot(sgu.astype(BF16), wso_ref[...], preferred_element_type=F32)

    merged = ga_ref[...] * y_attn + gs_ref[...] * y_sgu
    h1 = x_ref[...] + jnp.dot(merged.astype(BF16), wmo_ref[...], preferred_element_type=F32)
    h1_ref[...] = h1
    xn2 = h1 * lax.rsqrt(jnp.mean(h1 * h1, axis=-1, keepdims=True) + NORM_EPS) * fg_ref[...]
    xn2_ref[...] = xn2.astype(BF16)

    def split(a):
        hi = a.astype(BF16)
        return hi, (a - hi.astype(F32)).astype(BF16)

    x_hi, x_lo = split(xn2)
    w_hi, w_lo = split(wr_ref[...])
    logits = (jnp.dot(x_hi, w_hi, preferred_element_type=F32) + jnp.dot(x_lo, w_hi, preferred_element_type=F32)
              + jnp.dot(x_hi, w_lo, preferred_element_type=F32)) + br_ref[...]
    work = jnp.where(lane < N_EXPERTS, logits, NEG_INF)
    lane_f = lane.astype(F32)
    vals, idxs = [], []
    for _ in range(EXPERT_TOPK):
        vmax = jnp.max(work, axis=1, keepdims=True)
        first = jnp.min(jnp.where(work == vmax, lane_f, float(LANES)), axis=1, keepdims=True)
        vals.append(vmax)
        idxs.append(first)
        work = jnp.where(lane_f == first, NEG_INF, work)
    exps = [jnp.exp(v - vals[0]) for v in vals]
    denom = exps[0]
    for e in exps[1:]:
        denom = denom + e
    chosen = jnp.zeros((tm, LANES), F32)
    for first in idxs:
        chosen = jnp.where(lane_f == first, 1.0, chosen)

    strict_lower = (lax.broadcasted_iota(jnp.int32, (tm, tm), 0)
                    > lax.broadcasted_iota(jnp.int32, (tm, tm), 1))
    before = jnp.dot(jnp.where(strict_lower, 1.0, 0.0).astype(BF16), chosen.astype(BF16),
                     preferred_element_type=F32)
    counts = jnp.broadcast_to(jnp.sum(chosen, axis=0, keepdims=True), tcnt_ref.shape)
    lower_expert = (lax.broadcasted_iota(jnp.int32, (LANES, LANES), 0)
                    < lax.broadcasted_iota(jnp.int32, (LANES, LANES), 1))
    expert_start = jnp.dot(counts, jnp.where(lower_expert, 1.0, 0.0), precision=lax.Precision.HIGHEST,
                           preferred_element_type=F32)[0:1, :]
    sorted_pos = before + expert_start
    ri = jnp.zeros((tm, LANES), jnp.int32)
    rw = jnp.zeros((tm, LANES), F32)
    pos_lanes = jnp.zeros((tm, LANES), F32)
    for kk in range(EXPERT_TOPK):
        pos = jnp.sum(jnp.where(lane_f == idxs[kk], sorted_pos, 0.0), axis=1, keepdims=True)
        ri = jnp.where(lane == kk, idxs[kk].astype(jnp.int32), ri)
        ri = jnp.where(lane == EXPERT_TOPK + kk, pos.astype(jnp.int32), ri)
        rw = jnp.where(lane == kk, exps[kk] / denom, rw)
        pos_lanes = jnp.where(lane == kk, pos, pos_lanes)
    ri_ref[...] = ri
    rw_ref[...] = rw
    tcnt_ref[...] = counts
    post_ref[...] = pos_lanes.T[0:SUBLANES, :]


def _mix(attn, u, vvn, ga, gs, x2, wao, wso, wmo, wsp, bsp_full, fg, wr_pad, br_pad):
    t, d = x2.shape
    tm = ROUTE_TOKENS
    row = lambda w: pl.BlockSpec((tm, w), lambda i: (i, 0))
    const = lambda shape: pl.BlockSpec(shape, lambda i: (0,) * len(shape))
    ins = (attn, u, vvn, ga, gs, x2, wao, wso, wmo, wsp, bsp_full, fg, wr_pad, br_pad)
    in_specs = [row(a.shape[1]) for a in ins[:6]] + [const(a.shape) for a in ins[6:]]
    out_shape = (
        jax.ShapeDtypeStruct((t, d), F32),
        jax.ShapeDtypeStruct((t, d), BF16),
        jax.ShapeDtypeStruct((t, LANES), jnp.int32),
        jax.ShapeDtypeStruct((t, LANES), F32),
        jax.ShapeDtypeStruct((t // tm * SUBLANES, LANES), F32),
        jax.ShapeDtypeStruct((t // tm * SUBLANES, tm), F32),
    )
    assert d == SUBLANES * LANES, "a token row is stored as one (8, 128) tile"
    per_tile = lambda w: pl.BlockSpec((SUBLANES, w), lambda i: (i, 0))
    out_specs = (row(d), row(d), row(LANES), row(LANES), per_tile(LANES), per_tile(tm))
    return pl.pallas_call(
        _mix_kernel,
        grid=(t // tm,),
        in_specs=in_specs,
        out_specs=out_specs,
        out_shape=out_shape,
        scratch_shapes=[pltpu.VMEM(wsp.shape, BF16)],
        compiler_params=pltpu.CompilerParams(dimension_semantics=("arbitrary",), vmem_limit_bytes=BIG_VMEM_LIMIT),
        name="mix_route",
    )(*ins)


def _run_pieces(n, body):
    off = jnp.int32(0)
    for bit in reversed(range(ROUTE_TOKENS.bit_length())):
        size = 1 << bit
        take = lax.bitwise_and(n, size) != 0

        @pl.when(take)
        def _(off=off, size=size):
            body(off, size)

        off = off + jnp.where(take, size, 0)


def _token_rows(ref, r, n):
    return ref.at[pl.ds(pl.multiple_of(r * SUBLANES, SUBLANES), n * SUBLANES)]


def _dispatch_kernel(len_ref, tpos_ref, dst_ref, padlo_ref, padlen_ref, nused_ref, post_ref, x_ref,
                     xs_hbm, sorted_ref, zeros_ref, sems, zsem):
    step = pl.program_id(0)
    n_steps = pl.num_programs(0)
    tt = x_ref.shape[0]
    n_assign = tt * EXPERT_TOPK
    tile_rows = zeros_ref.shape[0] // SUBLANES
    slot = lax.rem(step, 2)

    def start_runs(tile, buf):
        def per_expert(e, carry):
            k = tile * N_EXPERTS + e

            def piece(off, size):
                pltpu.make_async_copy(_token_rows(sorted_ref.at[buf], tpos_ref[k] + off, size),
                                      _token_rows(xs_hbm, dst_ref[k] + off, size), sems.at[buf]).start()

            _run_pieces(len_ref[k], piece)
            return carry

        lax.fori_loop(0, N_EXPERTS, per_expert, 0)

    def wait_runs(buf):
        whole = sorted_ref.at[buf]
        pltpu.make_async_copy(whole, _token_rows(xs_hbm, 0, n_assign), sems.at[buf]).wait()

    @pl.when(step == 0)
    def _():
        zeros_ref[...] = jnp.zeros_like(zeros_ref)

        def tail_tile(tl, carry):
            cp = pltpu.make_async_copy(zeros_ref, _token_rows(xs_hbm, tl * tile_rows, tile_rows), zsem)
            cp.start()
            cp.wait()
            return carry

        lax.fori_loop(nused_ref[0], xs_hbm.shape[0] // (tile_rows * SUBLANES), tail_tile, 0)

        def per_expert(e, carry):
            def piece(off, size):
                cp = pltpu.make_async_copy(_token_rows(zeros_ref, 0, size),
                                           _token_rows(xs_hbm, padlo_ref[e] + off, size), zsem)
                cp.start()
                cp.wait()

            _run_pieces(padlen_ref[e], piece)
            return carry

        lax.fori_loop(0, N_EXPERTS, per_expert, 0)

    @pl.when(step >= 2)
    def _():
        wait_runs(slot)

    x = x_ref[...]
    pos_by_choice = [post_ref[kk:kk + 1, :] for kk in range(EXPERT_TOPK)]
    chunk = MOBA_BLOCK
    for rc in range(n_assign // chunk):
        r = (lax.broadcasted_iota(jnp.int32, (chunk, 1), 0) + rc * chunk).astype(F32)
        select = jnp.zeros((chunk, tt), F32)
        for pos in pos_by_choice:
            select = jnp.where(r == pos, 1.0, select)
        rows = jnp.dot(select.astype(BF16), x, preferred_element_type=F32)
        _store_token_rows(sorted_ref.at[slot], rows, base=rc * chunk * SUBLANES)
    start_runs(step, slot)

    @pl.when(step == n_steps - 1)
    def _():
        @pl.when(step >= 1)
        def _():
            wait_runs(1 - slot)

        wait_runs(slot)


def _dispatch(run_len, run_tpos, run_dst, padlo, padlen, n_used, pos_t, xn2, n_rows):
    t, d = xn2.shape
    n_assign = ROUTE_TOKENS * EXPERT_TOPK
    grid_spec = pltpu.PrefetchScalarGridSpec(
        num_scalar_prefetch=6,
        grid=(t // ROUTE_TOKENS,),
        in_specs=[pl.BlockSpec((SUBLANES, ROUTE_TOKENS), lambda c, *_: (c, 0)),
                  pl.BlockSpec((ROUTE_TOKENS, d), lambda c, *_: (c, 0))],
        out_specs=pl.BlockSpec(memory_space=pl.ANY),
        scratch_shapes=[
            pltpu.VMEM((2, n_assign * SUBLANES, LANES), F32),
            pltpu.VMEM((EXPERT_ROWS * SUBLANES, LANES), F32),
            pltpu.SemaphoreType.DMA((2,)),
            pltpu.SemaphoreType.DMA(()),
        ],
    )
    return pl.pallas_call(
        _dispatch_kernel,
        grid_spec=grid_spec,
        out_shape=jax.ShapeDtypeStruct((n_rows * SUBLANES, LANES), F32),
        compiler_params=pltpu.CompilerParams(dimension_semantics=("arbitrary",), vmem_limit_bytes=VMEM_LIMIT),
        name="dispatch",
    )(run_len, run_tpos, run_dst, padlo, padlen, n_used, pos_t, xn2)


def _expert_kernel(be_ref, nused_ref, x_ref, wgu_ref, bg_ref, bu_ref, wd_ref, bd_ref, y_ref, wgu_bf, wd_bf):
    t = pl.program_id(0)
    rows = y_ref.shape[0] // SUBLANES
    de = wd_ref.shape[0]
    grp = 2 * LANES
    active = t < nused_ref[0]
    fresh = jnp.logical_and(
        active, jnp.logical_or(t == 0, be_ref[t] != be_ref[jnp.maximum(t - 1, 0)]))

    @pl.when(fresh)
    def _():
        src = lax.broadcasted_iota(jnp.int32, (grp, grp), 0)
        dst = lax.broadcasted_iota(jnp.int32, (grp, grp), 1)
        wanted = jnp.where(dst < LANES, 2 * dst, 2 * (dst - LANES) + 1)
        perm = jnp.where(src == wanted, 1.0, 0.0).astype(BF16)
        for g in range(wgu_ref.shape[1] // grp):
            cols = slice(g * grp, (g + 1) * grp)
            wgu_bf[:, cols] = jnp.dot(wgu_ref[:, cols].astype(BF16), perm,
                                      preferred_element_type=F32).astype(BF16)
        wd_bf[...] = wd_ref[...].astype(BF16)

    @pl.when(active)
    def _():
        xb = _load_token_rows(x_ref, rows).astype(BF16)
        gu = jnp.dot(xb, wgu_bf[...], preferred_element_type=F32)
        hid = []
        for g in range(de // LANES):
            cols = slice(g * LANES, (g + 1) * LANES)
            gate = gu[:, g * grp:g * grp + LANES] + bg_ref[:, cols]
            up = gu[:, g * grp + LANES:(g + 1) * grp] + bu_ref[:, cols]
            gate = jnp.minimum(gate, SWIGLU_LIMIT)
            up = jnp.clip(up, -SWIGLU_LIMIT, SWIGLU_LIMIT)
            hid.append((gate * _sigmoid(SWIGLU_ALPHA * gate) * (up + 1.0)).astype(BF16))
        y = jnp.dot(jnp.concatenate(hid, axis=1), wd_bf[...], preferred_element_type=F32) + bd_ref[...]
        _store_token_rows(y_ref, y)

    @pl.when(jnp.logical_not(active))
    def _():
        y_ref[...] = jnp.zeros_like(y_ref)


def _experts(block_expert, n_used, xs_rows, wgu, bg, bu, wd, bd):
    n_rows = xs_rows.shape[0] // SUBLANES
    _, d, de2 = wgu.shape
    de = wd.shape[1]
    n_tiles = n_rows // EXPERT_ROWS
    tile_rows = EXPERT_ROWS * SUBLANES
    per_expert = lambda t, be, nu: (be[t], 0, 0)
    grid_spec = pltpu.PrefetchScalarGridSpec(
        num_scalar_prefetch=2,
        grid=(n_tiles,),
        in_specs=[
            pl.BlockSpec((tile_rows, LANES), lambda t, be, nu: (jnp.minimum(t, jnp.maximum(nu[0] - 1, 0)), 0)),
            pl.BlockSpec((None, d, de2), per_expert),
            pl.BlockSpec((None, 1, de), per_expert),
            pl.BlockSpec((None, 1, de), per_expert),
            pl.BlockSpec((None, de, d), per_expert),
            pl.BlockSpec((None, 1, d), per_expert),
        ],
        out_specs=pl.BlockSpec((tile_rows, LANES), lambda t, be, nu: (t, 0)),
        scratch_shapes=[pltpu.VMEM((d, de2), BF16), pltpu.VMEM((de, d), BF16)],
    )
    return pl.pallas_call(
        _expert_kernel,
        grid_spec=grid_spec,
        out_shape=jax.ShapeDtypeStruct((n_rows * SUBLANES, LANES), F32),
        compiler_params=pltpu.CompilerParams(
            dimension_semantics=("arbitrary",), vmem_limit_bytes=BIG_VMEM_LIMIT),
        name="experts",
    )(block_expert, n_used, xs_rows, wgu, bg, bu, wd, bd)


def _combine_kernel(len_ref, tpos_ref, src_ref, pos_ref, w_ref, h1_ref, fg_ref, ys_hbm,
                    o_ref, buf_ref, h1rows_ref, outrows_ref, sems):
    step = pl.program_id(0)
    n_steps = pl.num_programs(0)
    tt = h1_ref.shape[0]
    slot = lax.rem(step, 2)

    def start_runs(tile, buf):
        def per_expert(e, carry):
            k = tile * N_EXPERTS + e

            def piece(off, size):
                pltpu.make_async_copy(_token_rows(ys_hbm, src_ref[k] + off, size),
                                      _token_rows(buf_ref.at[buf], tpos_ref[k] + off, size), sems.at[buf]).start()

            _run_pieces(len_ref[k], piece)
            return carry

        lax.fori_loop(0, N_EXPERTS, per_expert, 0)

    def wait_runs(buf):
        whole = buf_ref.at[buf]
        pltpu.make_async_copy(_token_rows(ys_hbm, 0, tt * EXPERT_TOPK), whole, sems.at[buf]).wait()

    @pl.when(step == 0)
    def _():
        start_runs(0, 0)

    @pl.when(step + 1 < n_steps)
    def _():
        start_runs(step + 1, 1 - slot)

    wait_runs(slot)

    _store_token_rows(h1rows_ref, h1_ref[...])

    def one_token(tok, carry):
        acc = h1rows_ref[pl.ds(pl.multiple_of(tok * SUBLANES, SUBLANES), SUBLANES), :]
        for kk in range(EXPERT_TOPK):
            a = tok * EXPERT_TOPK + kk
            pos = pos_ref[a]
            acc = acc + w_ref[a] * buf_ref[slot, pl.ds(pl.multiple_of(pos * SUBLANES, SUBLANES), SUBLANES), :]
        outrows_ref[pl.ds(pl.multiple_of(tok * SUBLANES, SUBLANES), SUBLANES), :] = acc
        return carry

    lax.fori_loop(0, tt, one_token, 0, unroll=4)
    h = _load_token_rows(outrows_ref, tt)
    o_ref[...] = h * lax.rsqrt(jnp.mean(h * h, axis=-1, keepdims=True) + NORM_EPS) * fg_ref[...]


def _combine(run_len, run_tpos, run_src, pos, w_flat, h1, fg, ys_rows):
    t, d = h1.shape
    tt = ROUTE_TOKENS
    n_assign = tt * EXPERT_TOPK
    smem_blk = pl.BlockSpec((n_assign,), lambda c, *_: (c,), memory_space=pltpu.SMEM)
    grid_spec = pltpu.PrefetchScalarGridSpec(
        num_scalar_prefetch=3,
        grid=(t // tt,),
        in_specs=[
            smem_blk, smem_blk,
            pl.BlockSpec((tt, d), lambda c, *_: (c, 0)),
            pl.BlockSpec((1, d), lambda c, *_: (0, 0)),
            pl.BlockSpec(memory_space=pl.ANY),
        ],
        out_specs=pl.BlockSpec((tt, d), lambda c, *_: (c, 0)),
        scratch_shapes=[
            pltpu.VMEM((2, n_assign * SUBLANES, LANES), F32),
            pltpu.VMEM((tt * SUBLANES, LANES), F32),
            pltpu.VMEM((tt * SUBLANES, LANES), F32),
            pltpu.SemaphoreType.DMA((2,)),
        ],
    )
    return pl.pallas_call(
        _combine_kernel,
        grid_spec=grid_spec,
        out_shape=jax.ShapeDtypeStruct((t, d), F32),
        compiler_params=pltpu.CompilerParams(dimension_semantics=("arbitrary",), vmem_limit_bytes=VMEM_LIMIT),
        name="combine",
    )(run_len, run_tpos, run_src, pos, w_flat, h1, fg, ys_rows)


def _layer(h, mix_norm_g, w_in, w_attn_out, sgu_ln_g, sgu_ln_b, w_spatial, b_spatial, w_sgu_out,
           w_mix_out, ffn_norm_g, w_router, b_router, w_gate_up, b_gate_up, w_down, b_down, out_g):
    b, s, d = h.shape
    t = b * s
    sgu_width = sgu_ln_g.shape[0]
    x2 = h.reshape(t, d)

    q, qt, kt, v, kmean, u, vvn, ga, gs = _in_proj(
        x2, mix_norm_g.reshape(1, d), w_in.astype(BF16), sgu_ln_g.reshape(1, sgu_width),
        sgu_ln_b.reshape(1, sgu_width), sgu_width)

    nb = s // MOBA_BLOCK
    attn = _moba(q.reshape(b, s, ATTN_WIDTH), qt, kt, v.reshape(b, s, ATTN_WIDTH),
                 kmean.reshape(b, nb, ATTN_WIDTH), _moba_key_table(s)).reshape(t, ATTN_WIDTH)

    gdim = sgu_width // SGU_GROUPS
    bsp_full = jnp.repeat(b_spatial.T, gdim, axis=1)
    wr_pad = jnp.zeros((d, LANES), F32).at[:, :N_EXPERTS].set(w_router)
    br_pad = jnp.zeros((1, LANES), F32).at[0, :N_EXPERTS].set(b_router)
    h1, xn2, ri, rw, tile_cnt, pos_t = _mix(
        attn, u, vvn, ga, gs, x2, w_attn_out.astype(BF16), w_sgu_out.astype(BF16), w_mix_out.astype(BF16),
        w_spatial, bsp_full, ffn_norm_g.reshape(1, d), wr_pad, br_pad)

    n_assign = t * EXPERT_TOPK
    n_tiles = -(-(n_assign + N_EXPERTS * (EXPERT_ROWS - 1)) // EXPERT_ROWS)
    n_rows = n_tiles * EXPERT_ROWS
    run_len = tile_cnt[::SUBLANES, :N_EXPERTS].astype(jnp.int32)
    counts = jnp.sum(run_len, axis=0)
    padded = (counts + EXPERT_ROWS - 1) // EXPERT_ROWS * EXPERT_ROWS
    pad_end = jnp.cumsum(padded)
    starts = pad_end - padded
    run_tpos = jnp.cumsum(run_len, axis=1) - run_len
    run_row = starts[None, :] + jnp.cumsum(run_len, axis=0) - run_len
    tile_start = jnp.arange(n_tiles, dtype=jnp.int32) * EXPERT_ROWS
    block_expert = jnp.minimum(
        jnp.sum((pad_end[None, :] <= tile_start[:, None]).astype(jnp.int32), axis=1), N_EXPERTS - 1)
    n_used = (pad_end[-1:] // EXPERT_ROWS).astype(jnp.int32)
    pos = ri[:, EXPERT_TOPK:2 * EXPERT_TOPK].reshape(n_assign)
    tables = (run_len.reshape(-1), run_tpos.reshape(-1), run_row.reshape(-1))

    xs_rows = _dispatch(*tables, starts + counts, padded - counts, n_used, pos_t, xn2, n_rows)

    de = w_down.shape[1]
    bg = b_gate_up[:, 0::2].reshape(N_EXPERTS, 1, de)
    bu = b_gate_up[:, 1::2].reshape(N_EXPERTS, 1, de)
    ys_rows = _experts(block_expert, n_used, xs_rows, w_gate_up, bg, bu, w_down, b_down.reshape(N_EXPERTS, 1, d))

    out = _combine(*tables, pos, rw[:, :EXPERT_TOPK].reshape(n_assign), h1, out_g.reshape(1, d), ys_rows)
    return out.reshape(b, s, d)


def kernel(x, mix_norm_g, w_in, w_attn_out, sgu_ln_g, sgu_ln_b, w_spatial, b_spatial, w_sgu_out, w_mix_out,
           ffn_norm_g, w_router, b_router, w_gate_up, b_gate_up, w_down, b_down, final_norm_g):
    depth = w_in.shape[0]
    assert depth == 1, "the final RMSNorm is fused into the single layer's combine step"
    return _layer(x, mix_norm_g[0], w_in[0], w_attn_out[0], sgu_ln_g[0], sgu_ln_b[0], w_spatial[0],
                  b_spatial[0], w_sgu_out[0], w_mix_out[0], ffn_norm_g[0], w_router[0], b_router[0],
                  w_gate_up[0], b_gate_up[0], w_down[0], b_down[0], final_norm_g)
```

```python
import jax
import jax.numpy as jnp
import numpy as np
from jax import lax
from jax.experimental import pallas as pl
from jax.experimental.pallas import tpu as pltpu

F32 = jnp.float32
BF16 = jnp.bfloat16
NEG_INF = float("-inf")
MASK_VALUE = -1e30


def _bf16_pieces(x, n=3):
    pieces = []
    for _ in range(n):
        p = float(np.asarray(x, np.float32).astype(BF16).astype(np.float32))
        pieces.append(p)
        x = x - p
    return pieces


LOG2E = 1.4426950408889634
LOG2E_PIECES = _bf16_pieces(LOG2E)

N_HEADS = 8
HEAD_DIM = 64
ATTN_WIDTH = N_HEADS * HEAD_DIM
MOBA_BLOCK = 256
MOBA_TOPK = 3
SGU_CHUNK = 128
SGU_GROUPS = 8
N_EXPERTS = 32
EXPERT_TOPK = 4
SWIGLU_LIMIT = 7.0
SWIGLU_ALPHA = 1.702
NORM_EPS = 1e-5

LANES = 128
SUBLANES = 8
PACKED_ROWS = 4
HEADS_PER_LANE_TILE = LANES // HEAD_DIM
MOBA_GROUP = 1
MOBA_BIAS_LANE0 = 64
MOBA_DUMMY_LANE = LANES - 1
PROJ_ROWS = 256
EXPERT_ROWS = 512
ROUTE_TOKENS = 512
VMEM_LIMIT = 48 * 1024 * 1024
BIG_VMEM_LIMIT = 56 * 1024 * 1024


def _sigmoid(x):
    return 1.0 / (1.0 + jnp.exp(-x))


def _store_token_rows(ref, value, base=0):
    n = value.shape[0]
    for c in range(value.shape[1] // LANES):
        ref[pl.ds(base + c, n, stride=SUBLANES), :] = value[:, c * LANES:(c + 1) * LANES]


def _load_token_rows(ref, n, base=0):
    return jnp.concatenate([ref[pl.ds(base + c, n, stride=SUBLANES), :] for c in range(SUBLANES)], axis=1)


def _gelu_exact(x):
    return 0.5 * x * (1.0 + lax.erf(x * (0.5 ** 0.5)))


def _in_proj_kernel(x_ref, g_ref, w_ref, lng_ref, lnb_ref,
                    q_ref, qt_ref, kt_ref, v_ref, kmean_ref, u_ref, vvn_ref, ga_ref, gs_ref):
    x = x_ref[...]
    xn = x * lax.rsqrt(jnp.mean(x * x, axis=-1, keepdims=True) + NORM_EPS) * g_ref[...]
    xb = xn.astype(BF16)

    def proj(lo, hi):
        return jnp.dot(xb, w_ref[:, lo:hi], preferred_element_type=F32)

    a = ATTN_WIDTH
    sw = u_ref.shape[1]
    d = ga_ref.shape[1]
    qf = proj(0, a)
    q_ref[...] = qf
    kf = proj(a, 2 * a)
    for j in range(kf.shape[0] // MOBA_BLOCK):
        qt_ref[j] = qf[j * MOBA_BLOCK:(j + 1) * MOBA_BLOCK].T
        kblk = kf[j * MOBA_BLOCK:(j + 1) * MOBA_BLOCK]
        kt_ref[j] = kblk.T.astype(BF16)
        kmean_ref[j] = jnp.mean(kblk, axis=0, keepdims=True)
    v_ref[...] = proj(2 * a, 3 * a).astype(BF16)
    z0 = 3 * a
    u_ref[...] = _gelu_exact(proj(z0, z0 + sw))
    zv = _gelu_exact(proj(z0 + sw, z0 + 2 * sw))
    mu = jnp.mean(zv, axis=-1, keepdims=True)
    zc = zv - mu
    var = jnp.mean(zc * zc, axis=-1, keepdims=True)
    vvn_ref[...] = zc * lax.rsqrt(var + NORM_EPS) * lng_ref[...] + lnb_ref[...]
    g0 = z0 + 2 * sw
    ga_ref[...] = _sigmoid(proj(g0, g0 + d))
    gs_ref[...] = _sigmoid(proj(g0 + d, g0 + 2 * d))


def _in_proj(x2, g, w_bf, lng, lnb, sgu_width):
    t, d = x2.shape
    tm = PROJ_ROWS
    n_in = w_bf.shape[1]
    row = lambda w: pl.BlockSpec((tm, w), lambda i: (i, 0))
    const = lambda shape: pl.BlockSpec(shape, lambda i: (0,) * len(shape))
    out_shape = (
        jax.ShapeDtypeStruct((t, ATTN_WIDTH), F32),
        jax.ShapeDtypeStruct((t // MOBA_BLOCK, ATTN_WIDTH, MOBA_BLOCK), F32),
        jax.ShapeDtypeStruct((t // MOBA_BLOCK, ATTN_WIDTH, MOBA_BLOCK), BF16),
        jax.ShapeDtypeStruct((t, ATTN_WIDTH), BF16),
        jax.ShapeDtypeStruct((t // MOBA_BLOCK, 1, ATTN_WIDTH), F32),
        jax.ShapeDtypeStruct((t, sgu_width), F32),
        jax.ShapeDtypeStruct((t, sgu_width), F32),
        jax.ShapeDtypeStruct((t, d), F32),
        jax.ShapeDtypeStruct((t, d), F32),
    )
    out_specs = (
        row(ATTN_WIDTH),
        pl.BlockSpec((tm // MOBA_BLOCK, ATTN_WIDTH, MOBA_BLOCK), lambda i: (i, 0, 0)),
        pl.BlockSpec((tm // MOBA_BLOCK, ATTN_WIDTH, MOBA_BLOCK), lambda i: (i, 0, 0)),
        row(ATTN_WIDTH),
        pl.BlockSpec((tm // MOBA_BLOCK, 1, ATTN_WIDTH), lambda i: (i, 0, 0)),
        row(sgu_width), row(sgu_width), row(d), row(d),
    )
    return pl.pallas_call(
        _in_proj_kernel,
        grid=(t // tm,),
        in_specs=[row(d), const((1, d)), const((d, n_in)), const((1, sgu_width)), const((1, sgu_width))],
        out_specs=out_specs,
        out_shape=out_shape,
        compiler_params=pltpu.CompilerParams(dimension_semantics=("arbitrary",), vmem_limit_bytes=VMEM_LIMIT),
        name="in_proj",
    )(x2, g, w_bf, lng, lnb)


def _alibi_slopes():
    slopes = 2.0 ** (-8.0 * np.arange(1, N_HEADS + 1, dtype=np.float64) / N_HEADS)
    assert all(np.log2(s) == np.round(np.log2(s)) for s in slopes), "ALiBi slopes must be powers of two"
    return slopes


def _moba_bias_lane(head_in_tile, part, piece):
    return MOBA_BIAS_LANE0 + (head_in_tile * 2 + part) * len(LOG2E_PIECES) + piece


def _moba_key_table(s):
    nb = s // MOBA_BLOCK
    slopes = _alibi_slopes()
    n_pairs = N_HEADS // HEADS_PER_LANE_TILE
    table = np.zeros((nb + 1, n_pairs, LANES, MOBA_BLOCK), np.float32)
    offs = np.arange(MOBA_BLOCK, dtype=np.float32)
    assert HEADS_PER_LANE_TILE * nb <= MOBA_BIAS_LANE0, "block one-hot rows must not overlap the ALiBi rows"
    for j in range(nb):
        for hh in range(HEADS_PER_LANE_TILE):
            table[j, :, hh * nb + j, :] = 1.0
        for p in range(n_pairs):
            for hh in range(HEADS_PER_LANE_TILE):
                slope = slopes[p * HEADS_PER_LANE_TILE + hh]
                for piece in range(len(LOG2E_PIECES)):
                    table[j, p, _moba_bias_lane(hh, 0, piece), :] = slope * MOBA_BLOCK * j
                    table[j, p, _moba_bias_lane(hh, 1, piece), :] = slope * offs
    table[nb, :, MOBA_DUMMY_LANE, :] = 1.0
    as_bf16 = table.astype(BF16)
    assert np.array_equal(as_bf16.astype(np.float32), table), "bias table must be exact in bf16"
    return jnp.asarray(as_bf16.reshape(nb + 1, n_pairs * LANES, MOBA_BLOCK))


def _moba_kernel(q_ref, qt_ref, qn_ref, qtn_ref, kt_ref, v_ref, km_ref, ct_ref, causal_ref, o_ref, qaug_ref,
                 s_ref, mpart_ref, mrow_ref, acc_ref):
    i = pl.program_id(2)
    nb = kt_ref.shape[0]
    blk = MOBA_BLOCK
    lane = lax.broadcasted_iota(jnp.int32, (1, LANES), 1)
    n_groups = lax.div(i + MOBA_GROUP, MOBA_GROUP)
    heads = range(HEADS_PER_LANE_TILE)
    hmasks = [(lane >= HEAD_DIM * hh) & (lane < HEAD_DIM * (hh + 1)) for hh in heads]
    slot = lax.rem(i, 2)

    def build_query_operand(q, qt, tile, into):
        bid = lax.broadcasted_iota(jnp.int32, (nb, blk), 0)
        bid_f = bid.astype(F32)
        masks_t = []
        for hh in heads:
            km_h = jnp.where(hmasks[hh], km_ref[...], 0.0)
            gate = jnp.dot(km_h, qt, precision=lax.Precision.HIGHEST, preferred_element_type=F32)
            gate = jnp.where(bid < tile, gate, NEG_INF)
            blockmask = jnp.where(bid == tile, 0.0, MASK_VALUE)
            for kk in range(MOBA_TOPK):
                gmax = jnp.max(gate, axis=0, keepdims=True)
                first = jnp.min(jnp.where(gate == gmax, bid_f, float(nb)), axis=0, keepdims=True)
                valid = (tile > kk).astype(F32)
                first = first * valid + (valid - 1.0)
                hit = bid_f == first
                blockmask = jnp.where(hit, 0.0, blockmask)
                gate = jnp.where(hit, NEG_INF, gate)
            masks_t.append(blockmask)
        masks_t.append(jnp.zeros((LANES - len(masks_t) * nb, blk), F32))
        masks = jnp.concatenate(masks_t, axis=0).T

        for hh in heads:
            feats = jnp.where(lane == MOBA_DUMMY_LANE, MASK_VALUE, 0.0)
            for part in range(2):
                for piece, value in enumerate(LOG2E_PIECES):
                    feats = jnp.where(lane == _moba_bias_lane(hh, part, piece), value, feats)
            extra = jnp.where((lane >= hh * nb) & (lane < (hh + 1) * nb), masks, feats)
            qh = jnp.where(hmasks[hh], q, 0.0)
            qaug_ref[into, hh * blk:(hh + 1) * blk, 0:LANES] = (qh * (HEAD_DIM ** -0.5 * LOG2E)).astype(BF16)
            qaug_ref[into, hh * blk:(hh + 1) * blk, LANES:2 * LANES] = extra.astype(BF16)

    @pl.when(i == 0)
    def _():
        build_query_operand(q_ref[...], qt_ref[...], i, slot)

    def scores(jk, jc):
        rhs = jnp.concatenate([kt_ref[jk], ct_ref[jc]], axis=0)
        return jnp.dot(qaug_ref[slot], rhs, preferred_element_type=F32)

    def lane_halves_max(s):
        return jnp.maximum(s[:, 0:LANES], s[:, LANES:2 * LANES])

    n_rows = len(heads) * blk
    variants = [(groups, min(groups * MOBA_GROUP, nb)) for groups in range(1, pl.cdiv(nb, MOBA_GROUP) + 1)]

    for groups, n_blocks in variants:
        @pl.when(n_groups == groups)
        def _(n_blocks=n_blocks):
            part = None
            for j in range(n_blocks):
                s = scores(j, jnp.where(j <= i, j, nb)) + causal_ref[(i == j).astype(jnp.int32)]
                s_ref[j] = s
                part = lane_halves_max(s) if part is None else jnp.maximum(part, lane_halves_max(s))
            mpart_ref[...] = part

    ones = jnp.ones((blk, LANES), BF16)
    rowmax = jnp.max(mpart_ref[...], axis=1, keepdims=True)
    mrow_ref[...] = jnp.broadcast_to(rowmax, (n_rows, 2 * LANES))

    def weighted(j):
        p = jnp.exp2(s_ref[j] - mrow_ref[...]).astype(BF16)
        vaug = jnp.concatenate([v_ref[j * blk:(j + 1) * blk, :], ones], axis=1)
        return jnp.dot(p, vaug, preferred_element_type=F32)

    next_tile = jnp.minimum(i + 1, nb - 1)
    for groups, n_blocks in variants:
        @pl.when(n_groups == groups)
        def _(n_blocks=n_blocks):
            build_query_operand(qn_ref[...], qtn_ref[...], next_tile, 1 - slot)
            tot = weighted(0)
            for j in range(1, n_blocks):
                tot = tot + weighted(j)
            acc_ref[...] = tot

    out = jnp.zeros((blk, LANES), F32)
    for hh in heads:
        acc = acc_ref[hh * blk:(hh + 1) * blk, :]
        out = jnp.where(hmasks[hh], acc[:, 0:LANES] / acc[:, LANES:2 * LANES], out)
    o_ref[...] = out


def _moba(q, qt, kt, v, kmean, key_table):
    b, s, a = q.shape
    nb = s // MOBA_BLOCK
    n_pairs = a // LANES
    rows = HEADS_PER_LANE_TILE * MOBA_BLOCK
    row_in_block = np.arange(rows)[:, None] % MOBA_BLOCK
    causal = np.where(row_in_block >= np.arange(MOBA_BLOCK)[None, :], 0.0, MASK_VALUE).astype(np.float32)
    causal_tiles = jnp.asarray(np.stack([np.zeros_like(causal), causal]))
    nxt = lambda i: jnp.minimum(i + 1, nb - 1)
    return pl.pallas_call(
        _moba_kernel,
        grid=(b, n_pairs, nb),
        in_specs=[
            pl.BlockSpec((None, MOBA_BLOCK, LANES), lambda bi, p, i: (bi, i, p)),
            pl.BlockSpec((None, LANES, MOBA_BLOCK), lambda bi, p, i: (bi * nb + i, p, 0)),
            pl.BlockSpec((None, MOBA_BLOCK, LANES), lambda bi, p, i: (bi, nxt(i), p)),
            pl.BlockSpec((None, LANES, MOBA_BLOCK), lambda bi, p, i: (bi * nb + nxt(i), p, 0)),
            pl.BlockSpec((nb, LANES, MOBA_BLOCK), lambda bi, p, i: (bi, p, 0)),
            pl.BlockSpec((None, s, LANES), lambda bi, p, i: (bi, 0, p)),
            pl.BlockSpec((None, nb, LANES), lambda bi, p, i: (bi, 0, p)),
            pl.BlockSpec((nb + 1, LANES, MOBA_BLOCK), lambda bi, p, i: (0, p, 0)),
            pl.BlockSpec((2, rows, MOBA_BLOCK), lambda bi, p, i: (0, 0, 0)),
        ],
        out_specs=pl.BlockSpec((None, MOBA_BLOCK, LANES), lambda bi, p, i: (bi, i, p)),
        out_shape=jax.ShapeDtypeStruct((b, s, a), F32),
        scratch_shapes=[
            pltpu.VMEM((2, rows, 2 * LANES), BF16),
            pltpu.VMEM((nb + 1, rows, MOBA_BLOCK), F32),
            pltpu.VMEM((rows, LANES), F32),
            pltpu.VMEM((rows, 2 * LANES), F32),
            pltpu.VMEM((rows, 2 * LANES), F32),
        ],
        compiler_params=pltpu.CompilerParams(
            dimension_semantics=("arbitrary", "arbitrary", "arbitrary"), vmem_limit_bytes=VMEM_LIMIT),
        name="moba",
    )(q, qt, q, qt, kt, v, kmean, key_table, causal_tiles)


def _mix_kernel(attn_ref, u_ref, vvn_ref, ga_ref, gs_ref, x_ref,
                wao_ref, wso_ref, wmo_ref, wsp_ref, bsp_ref, fg_ref, wr_ref, br_ref,
                h1_ref, xn2_ref, ri_ref, rw_ref, tcnt_ref, post_ref, wcausal_ref):
    tm = x_ref.shape[0]
    ch = SGU_CHUNK

    @pl.when(pl.program_id(0) == 0)
    def _():
        tril = lax.broadcasted_iota(jnp.int32, (ch, ch), 0) >= lax.broadcasted_iota(jnp.int32, (ch, ch), 1)
        for g in range(SGU_GROUPS):
            wcausal_ref[g] = jnp.where(tril, wsp_ref[g], 0.0).astype(BF16)

    y_attn = jnp.dot(attn_ref[...].astype(BF16), wao_ref[...], preferred_element_type=F32)

    lane = lax.broadcasted_iota(jnp.int32, (1, LANES), 1)
    gdim = vvn_ref.shape[1] // SGU_GROUPS
    groups_per_tile = LANES // gdim
    w_causal = [wcausal_ref[g] for g in range(SGU_GROUPS)]
    rows = []
    for c in range(tm // ch):
        cols = []
        for ct in range(vvn_ref.shape[1] // LANES):
            vp = vvn_ref[c * ch:(c + 1) * ch, ct * LANES:(ct + 1) * LANES]
            acc = jnp.zeros((ch, LANES), F32)
            for gg in range(groups_per_tile):
                gmask = (lane >= gdim * gg) & (lane < gdim * (gg + 1))
                vm = jnp.where(gmask, vp, 0.0).astype(BF16)
                acc = acc + jnp.dot(w_causal[ct * groups_per_tile + gg], vm, preferred_element_type=F32)
            cols.append(acc)
        rows.append(jnp.concatenate(cols, axis=1) + bsp_ref[...])
    mixed = jnp.concatenate(rows, axis=0)
    sgu = u_ref[...] * mixed
    y_sgu = jnp.dot(sgu.astype(BF16), wso_ref[...], preferred_element_type=F32)

    merged = ga_ref[...] * y_attn + gs_ref[...] * y_sgu
    h1 = x_ref[...] + jnp.dot(merged.astype(BF16), wmo_ref[...], preferred_element_type=F32)
    h1_ref[...] = h1
    xn2 = h1 * lax.rsqrt(jnp.mean(h1 * h1, axis=-1, keepdims=True) + NORM_EPS) * fg_ref[...]
    xn2_ref[...] = xn2.astype(BF16)

    def split(a):
        hi = a.astype(BF16)
        return hi, (a - hi.astype(F32)).astype(BF16)

    x_hi, x_lo = split(xn2)
    w_hi, w_lo = split(wr_ref[...])
    logits = (jnp.dot(x_hi, w_hi, preferred_element_type=F32) + jnp.dot(x_lo, w_hi, preferred_element_type=F32)
              + jnp.dot(x_hi, w_lo, preferred_element_type=F32)) + br_ref[...]
    work = jnp.where(lane < N_EXPERTS, logits, NEG_INF)
    lane_f = lane.astype(F32)
    vals, idxs = [], []
    for _ in range(EXPERT_TOPK):
        vmax = jnp.max(work, axis=1, keepdims=True)
        first = jnp.min(jnp.where(work == vmax, lane_f, float(LANES)), axis=1, keepdims=True)
        vals.append(vmax)
        idxs.append(first)
        work = jnp.where(lane_f == first, NEG_INF, work)
    exps = [jnp.exp(v - vals[0]) for v in vals]
    denom = exps[0]
    for e in exps[1:]:
        denom = denom + e
    chosen = jnp.zeros((tm, LANES), F32)
    for first in idxs:
        chosen = jnp.where(lane_f == first, 1.0, chosen)

    strict_lower = (lax.broadcasted_iota(jnp.int32, (tm, tm), 0)
                    > lax.broadcasted_iota(jnp.int32, (tm, tm), 1))
    before = jnp.dot(jnp.where(strict_lower, 1.0, 0.0).astype(BF16), chosen.astype(BF16),
                     preferred_element_type=F32)
    counts = jnp.broadcast_to(jnp.sum(chosen, axis=0, keepdims=True), tcnt_ref.shape)
    lower_expert = (lax.broadcasted_iota(jnp.int32, (LANES, LANES), 0)
                    < lax.broadcasted_iota(jnp.int32, (LANES, LANES), 1))
    expert_start = jnp.dot(counts, jnp.where(lower_expert, 1.0, 0.0), precision=lax.Precision.HIGHEST,
                           preferred_element_type=F32)[0:1, :]
    sorted_pos = before + expert_start
    ri = jnp.zeros((tm, LANES), jnp.int32)
    rw = jnp.zeros((tm, LANES), F32)
    pos_lanes = jnp.zeros((tm, LANES), F32)
    for kk in range(EXPERT_TOPK):
        pos = jnp.sum(jnp.where(lane_f == idxs[kk], sorted_pos, 0.0), axis=1, keepdims=True)
        ri = jnp.where(lane == kk, idxs[kk].astype(jnp.int32), ri)
        ri = jnp.where(lane == EXPERT_TOPK + kk, pos.astype(jnp.int32), ri)
        rw = jnp.where(lane == kk, exps[kk] / denom, rw)
        pos_lanes = jnp.where(lane == kk, pos, pos_lanes)
    ri_ref[...] = ri
    rw_ref[...] = rw
    tcnt_ref[...] = counts
    post_ref[...] = pos_lanes.T[0:SUBLANES, :]


def _mix(attn, u, vvn, ga, gs, x2, wao, wso, wmo, wsp, bsp_full, fg, wr_pad, br_pad):
    t, d = x2.shape
    tm = ROUTE_TOKENS
    row = lambda w: pl.BlockSpec((tm, w), lambda i: (i, 0))
    const = lambda shape: pl.BlockSpec(shape, lambda i: (0,) * len(shape))
    ins = (attn, u, vvn, ga, gs, x2, wao, wso, wmo, wsp, bsp_full, fg, wr_pad, br_pad)
    in_specs = [row(a.shape[1]) for a in ins[:6]] + [const(a.shape) for a in ins[6:]]
    out_shape = (
        jax.ShapeDtypeStruct((t, d), F32),
        jax.ShapeDtypeStruct((t, d), BF16),
        jax.ShapeDtypeStruct((t, LANES), jnp.int32),
        jax.ShapeDtypeStruct((t, LANES), F32),
        jax.ShapeDtypeStruct((t // tm * SUBLANES, LANES), F32),
        jax.ShapeDtypeStruct((t // tm * SUBLANES, tm), F32),
    )
    assert d == SUBLANES * LANES, "a token row is stored as one (8, 128) tile"
    per_tile = lambda w: pl.BlockSpec((SUBLANES, w), lambda i: (i, 0))
    out_specs = (row(d), row(d), row(LANES), row(LANES), per_tile(LANES), per_tile(tm))
    return pl.pallas_call(
        _mix_kernel,
        grid=(t // tm,),
        in_specs=in_specs,
        out_specs=out_specs,
        out_shape=out_shape,
        scratch_shapes=[pltpu.VMEM(wsp.shape, BF16)],
        compiler_params=pltpu.CompilerParams(dimension_semantics=("arbitrary",), vmem_limit_bytes=BIG_VMEM_LIMIT),
        name="mix_route",
    )(*ins)


def _run_pieces(n, body):
    off = jnp.int32(0)
    for bit in reversed(range(ROUTE_TOKENS.bit_length())):
        size = 1 << bit
        take = lax.bitwise_and(n, size) != 0

        @pl.when(take)
        def _(off=off, size=size):
            body(off, size)

        off = off + jnp.where(take, size, 0)


def _token_rows(ref, r, n, rows_per_token=SUBLANES):
    return ref.at[pl.ds(pl.multiple_of(r * rows_per_token, rows_per_token), n * rows_per_token)]


def _pack_bf16_pairs(value):
    half = value.shape[1] // 2
    lo = lax.shift_right_logical(lax.bitcast_convert_type(value[:, :half], jnp.uint32), jnp.uint32(16))
    hi = lax.bitwise_and(lax.bitcast_convert_type(value[:, half:], jnp.uint32), jnp.uint32(0xFFFF0000))
    return lax.bitwise_or(lo, hi)


def _unpack_bf16_pairs(words):
    lo = lax.bitcast_convert_type(lax.shift_left(words, jnp.uint32(16)), F32)
    hi = lax.bitcast_convert_type(lax.bitwise_and(words, jnp.uint32(0xFFFF0000)), F32)
    return jnp.concatenate([lo, hi], axis=1).astype(BF16)


def _dispatch_kernel(len_ref, tpos_ref, dst_ref, padlo_ref, padlen_ref, nused_ref, post_ref, x_ref,
                     xs_hbm, sorted_ref, zeros_ref, sems, zsem):
    step = pl.program_id(0)
    n_steps = pl.num_programs(0)
    tt = x_ref.shape[0]
    n_assign = tt * EXPERT_TOPK
    rpt = PACKED_ROWS
    tile_rows = zeros_ref.shape[0] // rpt
    slot = lax.rem(step, 2)

    def start_runs(tile, buf):
        def per_expert(e, carry):
            k = tile * N_EXPERTS + e

            def piece(off, size):
                pltpu.make_async_copy(_token_rows(sorted_ref.at[buf], tpos_ref[k] + off, size, rpt),
                                      _token_rows(xs_hbm, dst_ref[k] + off, size, rpt), sems.at[buf]).start()

            _run_pieces(len_ref[k], piece)
            return carry

        lax.fori_loop(0, N_EXPERTS, per_expert, 0)

    def wait_runs(buf):
        whole = sorted_ref.at[buf]
        pltpu.make_async_copy(whole, _token_rows(xs_hbm, 0, n_assign, rpt), sems.at[buf]).wait()

    @pl.when(step == 0)
    def _():
        zeros_ref[...] = jnp.zeros_like(zeros_ref)

        def tail_tile(tl, carry):
            cp = pltpu.make_async_copy(zeros_ref, _token_rows(xs_hbm, tl * tile_rows, tile_rows, rpt), zsem)
            cp.start()
            cp.wait()
            return carry

        lax.fori_loop(nused_ref[0], xs_hbm.shape[0] // (tile_rows * rpt), tail_tile, 0)

        def per_expert(e, carry):
            def piece(off, size):
                cp = pltpu.make_async_copy(_token_rows(zeros_ref, 0, size, rpt),
                                           _token_rows(xs_hbm, padlo_ref[e] + off, size, rpt), zsem)
                cp.start()
                cp.wait()

            _run_pieces(padlen_ref[e], piece)
            return carry

        lax.fori_loop(0, N_EXPERTS, per_expert, 0)

    @pl.when(step >= 2)
    def _():
        wait_runs(slot)

    x = x_ref[...]
    pos_by_choice = [post_ref[kk:kk + 1, :] for kk in range(EXPERT_TOPK)]
    chunk = MOBA_BLOCK
    for rc in range(n_assign // chunk):
        r = (lax.broadcasted_iota(jnp.int32, (chunk, 1), 0) + rc * chunk).astype(F32)
        select = jnp.zeros((chunk, tt), F32)
        for pos in pos_by_choice:
            select = jnp.where(r == pos, 1.0, select)
        rows = jnp.dot(select.astype(BF16), x, preferred_element_type=F32)
        words = _pack_bf16_pairs(rows)
        for c in range(rpt):
            sorted_ref[slot, pl.ds(rc * chunk * rpt + c, chunk, stride=rpt), :] = words[:, c * LANES:(c + 1) * LANES]
    start_runs(step, slot)

    @pl.when(step == n_steps - 1)
    def _():
        @pl.when(step >= 1)
        def _():
            wait_runs(1 - slot)

        wait_runs(slot)


def _dispatch(run_len, run_tpos, run_dst, padlo, padlen, n_used, pos_t, xn2, n_rows):
    t, d = xn2.shape
    n_assign = ROUTE_TOKENS * EXPERT_TOPK
    grid_spec = pltpu.PrefetchScalarGridSpec(
        num_scalar_prefetch=6,
        grid=(t // ROUTE_TOKENS,),
        in_specs=[pl.BlockSpec((SUBLANES, ROUTE_TOKENS), lambda c, *_: (c, 0)),
                  pl.BlockSpec((ROUTE_TOKENS, d), lambda c, *_: (c, 0))],
        out_specs=pl.BlockSpec(memory_space=pl.ANY),
        scratch_shapes=[
            pltpu.VMEM((2, n_assign * PACKED_ROWS, LANES), jnp.uint32),
            pltpu.VMEM((EXPERT_ROWS * PACKED_ROWS, LANES), jnp.uint32),
            pltpu.SemaphoreType.DMA((2,)),
            pltpu.SemaphoreType.DMA(()),
        ],
    )
    assert d == 2 * PACKED_ROWS * LANES
    return pl.pallas_call(
        _dispatch_kernel,
        grid_spec=grid_spec,
        out_shape=jax.ShapeDtypeStruct((n_rows * PACKED_ROWS, LANES), jnp.uint32),
        compiler_params=pltpu.CompilerParams(dimension_semantics=("arbitrary",), vmem_limit_bytes=VMEM_LIMIT),
        name="dispatch",
    )(run_len, run_tpos, run_dst, padlo, padlen, n_used, pos_t, xn2)


def _expert_kernel(be_ref, nused_ref, x_ref, wgu_ref, bg_ref, bu_ref, wd_ref, bd_ref, y_ref, wgu_bf, wd_bf):
    t = pl.program_id(0)
    rows = y_ref.shape[0] // SUBLANES
    de = wd_ref.shape[0]
    grp = 2 * LANES
    active = t < nused_ref[0]
    fresh = jnp.logical_and(
        active, jnp.logical_or(t == 0, be_ref[t] != be_ref[jnp.maximum(t - 1, 0)]))

    @pl.when(fresh)
    def _():
        src = lax.broadcasted_iota(jnp.int32, (grp, grp), 0)
        dst = lax.broadcasted_iota(jnp.int32, (grp, grp), 1)
        wanted = jnp.where(dst < LANES, 2 * dst, 2 * (dst - LANES) + 1)
        perm = jnp.where(src == wanted, 1.0, 0.0).astype(BF16)
        for g in range(wgu_ref.shape[1] // grp):
            cols = slice(g * grp, (g + 1) * grp)
            wgu_bf[:, cols] = jnp.dot(wgu_ref[:, cols].astype(BF16), perm,
                                      preferred_element_type=F32).astype(BF16)
        wd_bf[...] = wd_ref[...].astype(BF16)

    @pl.when(active)
    def _():
        words = jnp.concatenate([x_ref[pl.ds(c, rows, stride=PACKED_ROWS), :] for c in range(PACKED_ROWS)], axis=1)
        xb = _unpack_bf16_pairs(words)
        gu = jnp.dot(xb, wgu_bf[...], preferred_element_type=F32)
        hid = []
        for g in range(de // LANES):
            cols = slice(g * LANES, (g + 1) * LANES)
            gate = gu[:, g * grp:g * grp + LANES] + bg_ref[:, cols]
            up = gu[:, g * grp + LANES:(g + 1) * grp] + bu_ref[:, cols]
            gate = jnp.minimum(gate, SWIGLU_LIMIT)
            up = jnp.clip(up, -SWIGLU_LIMIT, SWIGLU_LIMIT)
            hid.append((gate * _sigmoid(SWIGLU_ALPHA * gate) * (up + 1.0)).astype(BF16))
        y = jnp.dot(jnp.concatenate(hid, axis=1), wd_bf[...], preferred_element_type=F32) + bd_ref[...]
        _store_token_rows(y_ref, y)

    @pl.when(jnp.logical_not(active))
    def _():
        y_ref[...] = jnp.zeros_like(y_ref)


def _experts(block_expert, n_used, xs_rows, wgu, bg, bu, wd, bd):
    n_rows = xs_rows.shape[0] // PACKED_ROWS
    _, d, de2 = wgu.shape
    de = wd.shape[1]
    n_tiles = n_rows // EXPERT_ROWS
    tile_rows = EXPERT_ROWS * SUBLANES
    per_expert = lambda t, be, nu: (be[t], 0, 0)
    grid_spec = pltpu.PrefetchScalarGridSpec(
        num_scalar_prefetch=2,
        grid=(n_tiles,),
        in_specs=[
            pl.BlockSpec((EXPERT_ROWS * PACKED_ROWS, LANES),
                         lambda t, be, nu: (jnp.minimum(t, jnp.maximum(nu[0] - 1, 0)), 0)),
            pl.BlockSpec((None, d, de2), per_expert),
            pl.BlockSpec((None, 1, de), per_expert),
            pl.BlockSpec((None, 1, de), per_expert),
            pl.BlockSpec((None, de, d), per_expert),
            pl.BlockSpec((None, 1, d), per_expert),
        ],
        out_specs=pl.BlockSpec((tile_rows, LANES), lambda t, be, nu: (t, 0)),
        scratch_shapes=[pltpu.VMEM((d, de2), BF16), pltpu.VMEM((de, d), BF16)],
    )
    return pl.pallas_call(
        _expert_kernel,
        grid_spec=grid_spec,
        out_shape=jax.ShapeDtypeStruct((n_rows * SUBLANES, LANES), F32),
        compiler_params=pltpu.CompilerParams(
            dimension_semantics=("arbitrary",), vmem_limit_bytes=BIG_VMEM_LIMIT),
        name="experts",
    )(block_expert, n_used, xs_rows, wgu, bg, bu, wd, bd)


def _combine_kernel(len_ref, tpos_ref, src_ref, pos_ref, w_ref, h1_ref, fg_ref, ys_hbm,
                    o_ref, buf_ref, h1rows_ref, outrows_ref, sems):
    step = pl.program_id(0)
    n_steps = pl.num_programs(0)
    tt = h1_ref.shape[0]
    slot = lax.rem(step, 2)

    def start_runs(tile, buf):
        def per_expert(e, carry):
            k = tile * N_EXPERTS + e

            def piece(off, size):
                pltpu.make_async_copy(_token_rows(ys_hbm, src_ref[k] + off, size),
                                      _token_rows(buf_ref.at[buf], tpos_ref[k] + off, size), sems.at[buf]).start()

            _run_pieces(len_ref[k], piece)
            return carry

        lax.fori_loop(0, N_EXPERTS, per_expert, 0)

    def wait_runs(buf):
        whole = buf_ref.at[buf]
        pltpu.make_async_copy(_token_rows(ys_hbm, 0, tt * EXPERT_TOPK), whole, sems.at[buf]).wait()

    @pl.when(step == 0)
    def _():
        start_runs(0, 0)

    @pl.when(step + 1 < n_steps)
    def _():
        start_runs(step + 1, 1 - slot)

    wait_runs(slot)

    _store_token_rows(h1rows_ref, h1_ref[...])

    def one_token(tok, carry):
        acc = h1rows_ref[pl.ds(pl.multiple_of(tok * SUBLANES, SUBLANES), SUBLANES), :]
        for kk in range(EXPERT_TOPK):
            a = tok * EXPERT_TOPK + kk
            pos = pos_ref[a]
            acc = acc + w_ref[a] * buf_ref[slot, pl.ds(pl.multiple_of(pos * SUBLANES, SUBLANES), SUBLANES), :]
        outrows_ref[pl.ds(pl.multiple_of(tok * SUBLANES, SUBLANES), SUBLANES), :] = acc
        return carry

    lax.fori_loop(0, tt, one_token, 0, unroll=4)
    h = _load_token_rows(outrows_ref, tt)
    o_ref[...] = h * lax.rsqrt(jnp.mean(h * h, axis=-1, keepdims=True) + NORM_EPS) * fg_ref[...]


def _combine(run_len, run_tpos, run_src, pos, w_flat, h1, fg, ys_rows):
    t, d = h1.shape
    tt = ROUTE_TOKENS
    n_assign = tt * EXPERT_TOPK
    smem_blk = pl.BlockSpec((n_assign,), lambda c, *_: (c,), memory_space=pltpu.SMEM)
    grid_spec = pltpu.PrefetchScalarGridSpec(
        num_scalar_prefetch=3,
        grid=(t // tt,),
        in_specs=[
            smem_blk, smem_blk,
            pl.BlockSpec((tt, d), lambda c, *_: (c, 0)),
            pl.BlockSpec((1, d), lambda c, *_: (0, 0)),
            pl.BlockSpec(memory_space=pl.ANY),
        ],
        out_specs=pl.BlockSpec((tt, d), lambda c, *_: (c, 0)),
        scratch_shapes=[
            pltpu.VMEM((2, n_assign * SUBLANES, LANES), F32),
            pltpu.VMEM((tt * SUBLANES, LANES), F32),
            pltpu.VMEM((tt * SUBLANES, LANES), F32),
            pltpu.SemaphoreType.DMA((2,)),
        ],
    )
    return pl.pallas_call(
        _combine_kernel,
        grid_spec=grid_spec,
        out_shape=jax.ShapeDtypeStruct((t, d), F32),
        compiler_params=pltpu.CompilerParams(dimension_semantics=("arbitrary",), vmem_limit_bytes=VMEM_LIMIT),
        name="combine",
    )(run_len, run_tpos, run_src, pos, w_flat, h1, fg, ys_rows)


def _layer(h, mix_norm_g, w_in, w_attn_out, sgu_ln_g, sgu_ln_b, w_spatial, b_spatial, w_sgu_out,
           w_mix_out, ffn_norm_g, w_router, b_router, w_gate_up, b_gate_up, w_down, b_down, out_g):
    b, s, d = h.shape
    t = b * s
    sgu_width = sgu_ln_g.shape[0]
    x2 = h.reshape(t, d)

    q, qt, kt, v, kmean, u, vvn, ga, gs = _in_proj(
        x2, mix_norm_g.reshape(1, d), w_in.astype(BF16), sgu_ln_g.reshape(1, sgu_width),
        sgu_ln_b.reshape(1, sgu_width), sgu_width)

    nb = s // MOBA_BLOCK
    attn = _moba(q.reshape(b, s, ATTN_WIDTH), qt, kt, v.reshape(b, s, ATTN_WIDTH),
                 kmean.reshape(b, nb, ATTN_WIDTH), _moba_key_table(s)).reshape(t, ATTN_WIDTH)

    gdim = sgu_width // SGU_GROUPS
    bsp_full = jnp.repeat(b_spatial.T, gdim, axis=1)
    wr_pad = jnp.zeros((d, LANES), F32).at[:, :N_EXPERTS].set(w_router)
    br_pad = jnp.zeros((1, LANES), F32).at[0, :N_EXPERTS].set(b_router)
    h1, xn2, ri, rw, tile_cnt, pos_t = _mix(
        attn, u, vvn, ga, gs, x2, w_attn_out.astype(BF16), w_sgu_out.astype(BF16), w_mix_out.astype(BF16),
        w_spatial, bsp_full, ffn_norm_g.reshape(1, d), wr_pad, br_pad)

    n_assign = t * EXPERT_TOPK
    n_tiles = -(-(n_assign + N_EXPERTS * (EXPERT_ROWS - 1)) // EXPERT_ROWS)
    n_rows = n_tiles * EXPERT_ROWS
    run_len = tile_cnt[::SUBLANES, :N_EXPERTS].astype(jnp.int32)
    counts = jnp.sum(run_len, axis=0)
    padded = (counts + EXPERT_ROWS - 1) // EXPERT_ROWS * EXPERT_ROWS
    pad_end = jnp.cumsum(padded)
    starts = pad_end - padded
    run_tpos = jnp.cumsum(run_len, axis=1) - run_len
    run_row = starts[None, :] + jnp.cumsum(run_len, axis=0) - run_len
    tile_start = jnp.arange(n_tiles, dtype=jnp.int32) * EXPERT_ROWS
    block_expert = jnp.minimum(
        jnp.sum((pad_end[None, :] <= tile_start[:, None]).astype(jnp.int32), axis=1), N_EXPERTS - 1)
    n_used = (pad_end[-1:] // EXPERT_ROWS).astype(jnp.int32)
    pos = ri[:, EXPERT_TOPK:2 * EXPERT_TOPK].reshape(n_assign)
    tables = (run_len.reshape(-1), run_tpos.reshape(-1), run_row.reshape(-1))

    xs_rows = _dispatch(*tables, starts + counts, padded - counts, n_used, pos_t, xn2, n_rows)

    de = w_down.shape[1]
    bg = b_gate_up[:, 0::2].reshape(N_EXPERTS, 1, de)
    bu = b_gate_up[:, 1::2].reshape(N_EXPERTS, 1, de)
    ys_rows = _experts(block_expert, n_used, xs_rows, w_gate_up, bg, bu, w_down, b_down.reshape(N_EXPERTS, 1, d))

    out = _combine(*tables, pos, rw[:, :EXPERT_TOPK].reshape(n_assign), h1, out_g.reshape(1, d), ys_rows)
    return out.reshape(b, s, d)


def kernel(x, mix_norm_g, w_in, w_attn_out, sgu_ln_g, sgu_ln_b, w_spatial, b_spatial, w_sgu_out, w_mix_out,
           ffn_norm_g, w_router, b_router, w_gate_up, b_gate_up, w_down, b_down, final_norm_g):
    depth = w_in.shape[0]
    assert depth == 1, "the final RMSNorm is fused into the single layer's combine step"
    return _layer(x, mix_norm_g[0], w_in[0], w_attn_out[0], sgu_ln_g[0], sgu_ln_b[0], w_spatial[0],
                  b_spatial[0], w_sgu_out[0], w_mix_out[0], ffn_norm_g[0], w_router[0], b_router[0],
                  w_gate_up[0], b_gate_up[0], w_down[0], b_down[0], final_norm_g)
```

```python
import jax
import jax.numpy as jnp
import numpy as np
from jax import lax
from jax.experimental import pallas as pl
from jax.experimental.pallas import tpu as pltpu

F32 = jnp.float32
BF16 = jnp.bfloat16
NEG_INF = float("-inf")
MASK_VALUE = -1e30


def _bf16_pieces(x, n=3):
    pieces = []
    for _ in range(n):
        p = float(np.asarray(x, np.float32).astype(BF16).astype(np.float32))
        pieces.append(p)
        x = x - p
    return pieces


LOG2E = 1.4426950408889634
LOG2E_PIECES = _bf16_pieces(LOG2E)

N_HEADS = 8
HEAD_DIM = 64
ATTN_WIDTH = N_HEADS * HEAD_DIM
MOBA_BLOCK = 256
MOBA_TOPK = 3
SGU_CHUNK = 128
SGU_GROUPS = 8
N_EXPERTS = 32
EXPERT_TOPK = 4
SWIGLU_LIMIT = 7.0
SWIGLU_ALPHA = 1.702
NORM_EPS = 1e-5

LANES = 128
SUBLANES = 8
PACKED_ROWS = 4
HEADS_PER_LANE_TILE = LANES // HEAD_DIM
MOBA_GROUP = 1
MOBA_BIAS_LANE0 = 64
MOBA_DUMMY_LANE = LANES - 1
PROJ_ROWS = 512
EXPERT_ROWS = 512
ROUTE_TOKENS = 512
VMEM_LIMIT = 48 * 1024 * 1024
BIG_VMEM_LIMIT = 56 * 1024 * 1024


def _sigmoid(x):
    return 1.0 / (1.0 + jnp.exp(-x))


def _store_token_rows(ref, value, base=0):
    n = value.shape[0]
    for c in range(value.shape[1] // LANES):
        ref[pl.ds(base + c, n, stride=SUBLANES), :] = value[:, c * LANES:(c + 1) * LANES]


def _load_token_rows(ref, n, base=0):
    return jnp.concatenate([ref[pl.ds(base + c, n, stride=SUBLANES), :] for c in range(SUBLANES)], axis=1)


def _gelu_exact(x):
    return 0.5 * x * (1.0 + lax.erf(x * (0.5 ** 0.5)))


def _in_proj_kernel(x_ref, g_ref, w_ref, lng_ref, lnb_ref,
                    q_ref, qt_ref, kt_ref, v_ref, kmean_ref, u_ref, vvn_ref, ga_ref, gs_ref):
    x = x_ref[...]
    xn = x * lax.rsqrt(jnp.mean(x * x, axis=-1, keepdims=True) + NORM_EPS) * g_ref[...]
    xb = xn.astype(BF16)

    def proj(lo, hi):
        return jnp.dot(xb, w_ref[:, lo:hi], preferred_element_type=F32)

    a = ATTN_WIDTH
    sw = u_ref.shape[1]
    d = ga_ref.shape[1]
    qf = proj(0, a)
    q_ref[...] = qf
    kf = proj(a, 2 * a)
    for j in range(kf.shape[0] // MOBA_BLOCK):
        qt_ref[j] = qf[j * MOBA_BLOCK:(j + 1) * MOBA_BLOCK].T
        kblk = kf[j * MOBA_BLOCK:(j + 1) * MOBA_BLOCK]
        kt_ref[j] = kblk.T.astype(BF16)
        kmean_ref[j] = jnp.mean(kblk, axis=0, keepdims=True)
    v_ref[...] = proj(2 * a, 3 * a).astype(BF16)
    z0 = 3 * a
    u_ref[...] = _gelu_exact(proj(z0, z0 + sw))
    zv = _gelu_exact(proj(z0 + sw, z0 + 2 * sw))
    mu = jnp.mean(zv, axis=-1, keepdims=True)
    zc = zv - mu
    var = jnp.mean(zc * zc, axis=-1, keepdims=True)
    vvn_ref[...] = zc * lax.rsqrt(var + NORM_EPS) * lng_ref[...] + lnb_ref[...]
    g0 = z0 + 2 * sw
    ga_ref[...] = _sigmoid(proj(g0, g0 + d))
    gs_ref[...] = _sigmoid(proj(g0 + d, g0 + 2 * d))


def _in_proj(x2, g, w_bf, lng, lnb, sgu_width):
    t, d = x2.shape
    tm = PROJ_ROWS
    n_in = w_bf.shape[1]
    row = lambda w: pl.BlockSpec((tm, w), lambda i: (i, 0))
    const = lambda shape: pl.BlockSpec(shape, lambda i: (0,) * len(shape))
    out_shape = (
        jax.ShapeDtypeStruct((t, ATTN_WIDTH), F32),
        jax.ShapeDtypeStruct((t // MOBA_BLOCK, ATTN_WIDTH, MOBA_BLOCK), F32),
        jax.ShapeDtypeStruct((t // MOBA_BLOCK, ATTN_WIDTH, MOBA_BLOCK), BF16),
        jax.ShapeDtypeStruct((t, ATTN_WIDTH), BF16),
        jax.ShapeDtypeStruct((t // MOBA_BLOCK, 1, ATTN_WIDTH), F32),
        jax.ShapeDtypeStruct((t, sgu_width), F32),
        jax.ShapeDtypeStruct((t, sgu_width), F32),
        jax.ShapeDtypeStruct((t, d), F32),
        jax.ShapeDtypeStruct((t, d), F32),
    )
    out_specs = (
        row(ATTN_WIDTH),
        pl.BlockSpec((tm // MOBA_BLOCK, ATTN_WIDTH, MOBA_BLOCK), lambda i: (i, 0, 0)),
        pl.BlockSpec((tm // MOBA_BLOCK, ATTN_WIDTH, MOBA_BLOCK), lambda i: (i, 0, 0)),
        row(ATTN_WIDTH),
        pl.BlockSpec((tm // MOBA_BLOCK, 1, ATTN_WIDTH), lambda i: (i, 0, 0)),
        row(sgu_width), row(sgu_width), row(d), row(d),
    )
    return pl.pallas_call(
        _in_proj_kernel,
        grid=(t // tm,),
        in_specs=[row(d), const((1, d)),
                  pl.BlockSpec((d, n_in), lambda i: (0, 0), pipeline_mode=pl.Buffered(1)),
                  const((1, sgu_width)), const((1, sgu_width))],
        out_specs=out_specs,
        out_shape=out_shape,
        compiler_params=pltpu.CompilerParams(dimension_semantics=("arbitrary",), vmem_limit_bytes=BIG_VMEM_LIMIT),
        name="in_proj",
    )(x2, g, w_bf, lng, lnb)


def _alibi_slopes():
    slopes = 2.0 ** (-8.0 * np.arange(1, N_HEADS + 1, dtype=np.float64) / N_HEADS)
    assert all(np.log2(s) == np.round(np.log2(s)) for s in slopes), "ALiBi slopes must be powers of two"
    return slopes


def _moba_bias_lane(head_in_tile, part, piece):
    return MOBA_BIAS_LANE0 + (head_in_tile * 2 + part) * len(LOG2E_PIECES) + piece


def _moba_key_table(s):
    nb = s // MOBA_BLOCK
    slopes = _alibi_slopes()
    n_pairs = N_HEADS // HEADS_PER_LANE_TILE
    table = np.zeros((nb + 1, n_pairs, LANES, MOBA_BLOCK), np.float32)
    offs = np.arange(MOBA_BLOCK, dtype=np.float32)
    assert HEADS_PER_LANE_TILE * nb <= MOBA_BIAS_LANE0, "block one-hot rows must not overlap the ALiBi rows"
    for j in range(nb):
        for hh in range(HEADS_PER_LANE_TILE):
            table[j, :, hh * nb + j, :] = 1.0
        for p in range(n_pairs):
            for hh in range(HEADS_PER_LANE_TILE):
                slope = slopes[p * HEADS_PER_LANE_TILE + hh]
                for piece in range(len(LOG2E_PIECES)):
                    table[j, p, _moba_bias_lane(hh, 0, piece), :] = slope * MOBA_BLOCK * j
                    table[j, p, _moba_bias_lane(hh, 1, piece), :] = slope * offs
    table[nb, :, MOBA_DUMMY_LANE, :] = 1.0
    as_bf16 = table.astype(BF16)
    assert np.array_equal(as_bf16.astype(np.float32), table), "bias table must be exact in bf16"
    return jnp.asarray(as_bf16.reshape(nb + 1, n_pairs * LANES, MOBA_BLOCK))


def _moba_kernel(q_ref, qt_ref, qn_ref, qtn_ref, kt_ref, v_ref, km_ref, ct_ref, causal_ref, o_ref, qaug_ref,
                 s_ref, mpart_ref, mrow_ref, acc_ref):
    i = pl.program_id(2)
    nb = kt_ref.shape[0]
    blk = MOBA_BLOCK
    lane = lax.broadcasted_iota(jnp.int32, (1, LANES), 1)
    n_groups = lax.div(i + MOBA_GROUP, MOBA_GROUP)
    heads = range(HEADS_PER_LANE_TILE)
    hmasks = [(lane >= HEAD_DIM * hh) & (lane < HEAD_DIM * (hh + 1)) for hh in heads]
    slot = lax.rem(i, 2)

    def build_query_operand(q, qt, tile, into):
        bid = lax.broadcasted_iota(jnp.int32, (nb, blk), 0)
        bid_f = bid.astype(F32)
        masks_t = []
        for hh in heads:
            km_h = jnp.where(hmasks[hh], km_ref[...], 0.0)
            gate = jnp.dot(km_h, qt, precision=lax.Precision.HIGHEST, preferred_element_type=F32)
            gate = jnp.where(bid < tile, gate, NEG_INF)
            blockmask = jnp.where(bid == tile, 0.0, MASK_VALUE)
            for kk in range(MOBA_TOPK):
                gmax = jnp.max(gate, axis=0, keepdims=True)
                first = jnp.min(jnp.where(gate == gmax, bid_f, float(nb)), axis=0, keepdims=True)
                valid = (tile > kk).astype(F32)
                first = first * valid + (valid - 1.0)
                hit = bid_f == first
                blockmask = jnp.where(hit, 0.0, blockmask)
                gate = jnp.where(hit, NEG_INF, gate)
            masks_t.append(blockmask)
        masks_t.append(jnp.zeros((LANES - len(masks_t) * nb, blk), F32))
        masks = jnp.concatenate(masks_t, axis=0).T

        for hh in heads:
            feats = jnp.where(lane == MOBA_DUMMY_LANE, MASK_VALUE, 0.0)
            for part in range(2):
                for piece, value in enumerate(LOG2E_PIECES):
                    feats = jnp.where(lane == _moba_bias_lane(hh, part, piece), value, feats)
            extra = jnp.where((lane >= hh * nb) & (lane < (hh + 1) * nb), masks, feats)
            qh = jnp.where(hmasks[hh], q, 0.0)
            qaug_ref[into, hh * blk:(hh + 1) * blk, 0:LANES] = (qh * (HEAD_DIM ** -0.5 * LOG2E)).astype(BF16)
            qaug_ref[into, hh * blk:(hh + 1) * blk, LANES:2 * LANES] = extra.astype(BF16)

    @pl.when(i == 0)
    def _():
        build_query_operand(q_ref[...], qt_ref[...], i, slot)

    def scores(jk, jc):
        rhs = jnp.concatenate([kt_ref[jk], ct_ref[jc]], axis=0)
        return jnp.dot(qaug_ref[slot], rhs, preferred_element_type=F32)

    def lane_halves_max(s):
        return jnp.maximum(s[:, 0:LANES], s[:, LANES:2 * LANES])

    n_rows = len(heads) * blk
    variants = [(groups, min(groups * MOBA_GROUP, nb)) for groups in range(1, pl.cdiv(nb, MOBA_GROUP) + 1)]

    for groups, n_blocks in variants:
        @pl.when(n_groups == groups)
        def _(n_blocks=n_blocks):
            part = None
            for j in range(n_blocks):
                s = scores(j, jnp.where(j <= i, j, nb)) + causal_ref[(i == j).astype(jnp.int32)]
                s_ref[j] = s
                part = lane_halves_max(s) if part is None else jnp.maximum(part, lane_halves_max(s))
            mpart_ref[...] = part

    ones = jnp.ones((blk, LANES), BF16)
    rowmax = jnp.max(mpart_ref[...], axis=1, keepdims=True)
    mrow_ref[...] = jnp.broadcast_to(rowmax, (n_rows, 2 * LANES))

    def weighted(j):
        p = jnp.exp2(s_ref[j] - mrow_ref[...]).astype(BF16)
        vaug = jnp.concatenate([v_ref[j * blk:(j + 1) * blk, :], ones], axis=1)
        return jnp.dot(p, vaug, preferred_element_type=F32)

    next_tile = jnp.minimum(i + 1, nb - 1)
    for groups, n_blocks in variants:
        @pl.when(n_groups == groups)
        def _(n_blocks=n_blocks):
            build_query_operand(qn_ref[...], qtn_ref[...], next_tile, 1 - slot)
            tot = weighted(0)
            for j in range(1, n_blocks):
                tot = tot + weighted(j)
            acc_ref[...] = tot

    out = jnp.zeros((blk, LANES), F32)
    for hh in heads:
        acc = acc_ref[hh * blk:(hh + 1) * blk, :]
        out = jnp.where(hmasks[hh], acc[:, 0:LANES] / acc[:, LANES:2 * LANES], out)
    o_ref[...] = out


def _moba(q, qt, kt, v, kmean, key_table):
    b, s, a = q.shape
    nb = s // MOBA_BLOCK
    n_pairs = a // LANES
    rows = HEADS_PER_LANE_TILE * MOBA_BLOCK
    row_in_block = np.arange(rows)[:, None] % MOBA_BLOCK
    causal = np.where(row_in_block >= np.arange(MOBA_BLOCK)[None, :], 0.0, MASK_VALUE).astype(np.float32)
    causal_tiles = jnp.asarray(np.stack([np.zeros_like(causal), causal]))
    nxt = lambda i: jnp.minimum(i + 1, nb - 1)
    return pl.pallas_call(
        _moba_kernel,
        grid=(b, n_pairs, nb),
        in_specs=[
            pl.BlockSpec((None, MOBA_BLOCK, LANES), lambda bi, p, i: (bi, i, p)),
            pl.BlockSpec((None, LANES, MOBA_BLOCK), lambda bi, p, i: (bi * nb + i, p, 0)),
            pl.BlockSpec((None, MOBA_BLOCK, LANES), lambda bi, p, i: (bi, nxt(i), p)),
            pl.BlockSpec((None, LANES, MOBA_BLOCK), lambda bi, p, i: (bi * nb + nxt(i), p, 0)),
            pl.BlockSpec((nb, LANES, MOBA_BLOCK), lambda bi, p, i: (bi, p, 0)),
            pl.BlockSpec((None, s, LANES), lambda bi, p, i: (bi, 0, p)),
            pl.BlockSpec((None, nb, LANES), lambda bi, p, i: (bi, 0, p)),
            pl.BlockSpec((nb + 1, LANES, MOBA_BLOCK), lambda bi, p, i: (0, p, 0)),
            pl.BlockSpec((2, rows, MOBA_BLOCK), lambda bi, p, i: (0, 0, 0)),
        ],
        out_specs=pl.BlockSpec((None, MOBA_BLOCK, LANES), lambda bi, p, i: (bi, i, p)),
        out_shape=jax.ShapeDtypeStruct((b, s, a), F32),
        scratch_shapes=[
            pltpu.VMEM((2, rows, 2 * LANES), BF16),
            pltpu.VMEM((nb + 1, rows, MOBA_BLOCK), F32),
            pltpu.VMEM((rows, LANES), F32),
            pltpu.VMEM((rows, 2 * LANES), F32),
            pltpu.VMEM((rows, 2 * LANES), F32),
        ],
        compiler_params=pltpu.CompilerParams(
            dimension_semantics=("arbitrary", "arbitrary", "arbitrary"), vmem_limit_bytes=VMEM_LIMIT),
        name="moba",
    )(q, qt, q, qt, kt, v, kmean, key_table, causal_tiles)


def _mix_kernel(attn_ref, u_ref, vvn_ref, ga_ref, gs_ref, x_ref,
                wao_ref, wso_ref, wmo_ref, wsp_ref, bsp_ref, fg_ref, wr_ref, br_ref,
                h1_ref, xn2_ref, ri_ref, rw_ref, tcnt_ref, post_ref, wcausal_ref):
    tm = x_ref.shape[0]
    ch = SGU_CHUNK

    @pl.when(pl.program_id(0) == 0)
    def _():
        tril = lax.broadcasted_iota(jnp.int32, (ch, ch), 0) >= lax.broadcasted_iota(jnp.int32, (ch, ch), 1)
        for g in range(SGU_GROUPS):
            wcausal_ref[g] = jnp.where(tril, wsp_ref[g], 0.0).astype(BF16)

    y_attn = jnp.dot(attn_ref[...].astype(BF16), wao_ref[...], preferred_element_type=F32)

    lane = lax.broadcasted_iota(jnp.int32, (1, LANES), 1)
    gdim = vvn_ref.shape[1] // SGU_GROUPS
    groups_per_tile = LANES // gdim
    w_causal = [wcausal_ref[g] for g in range(SGU_GROUPS)]
    rows = []
    for c in range(tm // ch):
        cols = []
        for ct in range(vvn_ref.shape[1] // LANES):
            vp = vvn_ref[c * ch:(c + 1) * ch, ct * LANES:(ct + 1) * LANES]
            acc = jnp.zeros((ch, LANES), F32)
            for gg in range(groups_per_tile):
                gmask = (lane >= gdim * gg) & (lane < gdim * (gg + 1))
                vm = jnp.where(gmask, vp, 0.0).astype(BF16)
                acc = acc + jnp.dot(w_causal[ct * groups_per_tile + gg], vm, preferred_element_type=F32)
            cols.append(acc)
        rows.append(jnp.concatenate(cols, axis=1) + bsp_ref[...])
    mixed = jnp.concatenate(rows, axis=0)
    sgu = u_ref[...] * mixed
    y_sgu = jnp.dot(sgu.astype(BF16), wso_ref[...], preferred_element_type=F32)

    merged = ga_ref[...] * y_attn + gs_ref[...] * y_sgu
    h1 = x_ref[...] + jnp.dot(merged.astype(BF16), wmo_ref[...], preferred_element_type=F32)
    h1_ref[...] = h1
    xn2 = h1 * lax.rsqrt(jnp.mean(h1 * h1, axis=-1, keepdims=True) + NORM_EPS) * fg_ref[...]
    xn2_ref[...] = xn2.astype(BF16)

    def split(a):
        hi = a.astype(BF16)
        return hi, (a - hi.astype(F32)).astype(BF16)

    x_hi, x_lo = split(xn2)
    w_hi, w_lo = split(wr_ref[...])
    logits = (jnp.dot(x_hi, w_hi, preferred_element_type=F32) + jnp.dot(x_lo, w_hi, preferred_element_type=F32)
              + jnp.dot(x_hi, w_lo, preferred_element_type=F32)) + br_ref[...]
    work = jnp.where(lane < N_EXPERTS, logits, NEG_INF)
    lane_f = lane.astype(F32)
    vals, idxs = [], []
    for _ in range(EXPERT_TOPK):
        vmax = jnp.max(work, axis=1, keepdims=True)
        first = jnp.min(jnp.where(work == vmax, lane_f, float(LANES)), axis=1, keepdims=True)
        vals.append(vmax)
        idxs.append(first)
        work = jnp.where(lane_f == first, NEG_INF, work)
    exps = [jnp.exp(v - vals[0]) for v in vals]
    denom = exps[0]
    for e in exps[1:]:
        denom = denom + e
    chosen = jnp.zeros((tm, LANES), F32)
    for first in idxs:
        chosen = jnp.where(lane_f == first, 1.0, chosen)

    strict_lower = (lax.broadcasted_iota(jnp.int32, (tm, tm), 0)
                    > lax.broadcasted_iota(jnp.int32, (tm, tm), 1))
    before = jnp.dot(jnp.where(strict_lower, 1.0, 0.0).astype(BF16), chosen.astype(BF16),
                     preferred_element_type=F32)
    counts = jnp.broadcast_to(jnp.sum(chosen, axis=0, keepdims=True), tcnt_ref.shape)
    lower_expert = (lax.broadcasted_iota(jnp.int32, (LANES, LANES), 0)
                    < lax.broadcasted_iota(jnp.int32, (LANES, LANES), 1))
    expert_start = jnp.dot(counts, jnp.where(lower_expert, 1.0, 0.0), precision=lax.Precision.HIGHEST,
                           preferred_element_type=F32)[0:1, :]
    sorted_pos = before + expert_start
    ri = jnp.zeros((tm, LANES), jnp.int32)
    rw = jnp.zeros((tm, LANES), F32)
    pos_lanes = jnp.zeros((tm, LANES), F32)
    for kk in range(EXPERT_TOPK):
        pos = jnp.sum(jnp.where(lane_f == idxs[kk], sorted_pos, 0.0), axis=1, keepdims=True)
        ri = jnp.where(lane == kk, idxs[kk].astype(jnp.int32), ri)
        ri = jnp.where(lane == EXPERT_TOPK + kk, pos.astype(jnp.int32), ri)
        rw = jnp.where(lane == kk, exps[kk] / denom, rw)
        pos_lanes = jnp.where(lane == kk, pos, pos_lanes)
    ri_ref[...] = ri
    rw_ref[...] = rw
    tcnt_ref[...] = counts
    post_ref[...] = pos_lanes.T[0:SUBLANES, :]


def _mix(attn, u, vvn, ga, gs, x2, wao, wso, wmo, wsp, bsp_full, fg, wr_pad, br_pad):
    t, d = x2.shape
    tm = ROUTE_TOKENS
    row = lambda w: pl.BlockSpec((tm, w), lambda i: (i, 0))
    const = lambda shape: pl.BlockSpec(shape, lambda i: (0,) * len(shape))
    ins = (attn, u, vvn, ga, gs, x2, wao, wso, wmo, wsp, bsp_full, fg, wr_pad, br_pad)
    in_specs = [row(a.shape[1]) for a in ins[:6]] + [const(a.shape) for a in ins[6:]]
    out_shape = (
        jax.ShapeDtypeStruct((t, d), F32),
        jax.ShapeDtypeStruct((t, d), BF16),
        jax.ShapeDtypeStruct((t, LANES), jnp.int32),
        jax.ShapeDtypeStruct((t, LANES), F32),
        jax.ShapeDtypeStruct((t // tm * SUBLANES, LANES), F32),
        jax.ShapeDtypeStruct((t // tm * SUBLANES, tm), F32),
    )
    assert d == SUBLANES * LANES, "a token row is stored as one (8, 128) tile"
    per_tile = lambda w: pl.BlockSpec((SUBLANES, w), lambda i: (i, 0))
    out_specs = (row(d), row(d), row(LANES), row(LANES), per_tile(LANES), per_tile(tm))
    return pl.pallas_call(
        _mix_kernel,
        grid=(t // tm,),
        in_specs=in_specs,
        out_specs=out_specs,
        out_shape=out_shape,
        scratch_shapes=[pltpu.VMEM(wsp.shape, BF16)],
        compiler_params=pltpu.CompilerParams(dimension_semantics=("arbitrary",), vmem_limit_bytes=BIG_VMEM_LIMIT),
        name="mix_route",
    )(*ins)


def _run_pieces(n, body):
    off = jnp.int32(0)
    for bit in reversed(range(ROUTE_TOKENS.bit_length())):
        size = 1 << bit
        take = lax.bitwise_and(n, size) != 0

        @pl.when(take)
        def _(off=off, size=size):
            body(off, size)

        off = off + jnp.where(take, size, 0)


def _token_rows(ref, r, n, rows_per_token=SUBLANES):
    return ref.at[pl.ds(pl.multiple_of(r * rows_per_token, rows_per_token), n * rows_per_token)]


def _pack_bf16_pairs(value):
    half = value.shape[1] // 2
    lo = lax.shift_right_logical(lax.bitcast_convert_type(value[:, :half], jnp.uint32), jnp.uint32(16))
    hi = lax.bitwise_and(lax.bitcast_convert_type(value[:, half:], jnp.uint32), jnp.uint32(0xFFFF0000))
    return lax.bitwise_or(lo, hi)


def _unpack_bf16_pairs(words):
    lo = lax.bitcast_convert_type(lax.shift_left(words, jnp.uint32(16)), F32)
    hi = lax.bitcast_convert_type(lax.bitwise_and(words, jnp.uint32(0xFFFF0000)), F32)
    return jnp.concatenate([lo, hi], axis=1).astype(BF16)


def _dispatch_kernel(len_ref, tpos_ref, dst_ref, padlo_ref, padlen_ref, nused_ref, post_ref, x_ref,
                     xs_hbm, sorted_ref, zeros_ref, sems, zsem):
    step = pl.program_id(0)
    n_steps = pl.num_programs(0)
    tt = x_ref.shape[0]
    n_assign = tt * EXPERT_TOPK
    rpt = PACKED_ROWS
    tile_rows = zeros_ref.shape[0] // rpt
    slot = lax.rem(step, 2)

    def start_runs(tile, buf):
        def per_expert(e, carry):
            k = tile * N_EXPERTS + e

            def piece(off, size):
                pltpu.make_async_copy(_token_rows(sorted_ref.at[buf], tpos_ref[k] + off, size, rpt),
                                      _token_rows(xs_hbm, dst_ref[k] + off, size, rpt), sems.at[buf]).start()

            _run_pieces(len_ref[k], piece)
            return carry

        lax.fori_loop(0, N_EXPERTS, per_expert, 0)

    def wait_runs(buf):
        whole = sorted_ref.at[buf]
        pltpu.make_async_copy(whole, _token_rows(xs_hbm, 0, n_assign, rpt), sems.at[buf]).wait()

    @pl.when(step == 0)
    def _():
        zeros_ref[...] = jnp.zeros_like(zeros_ref)

        def tail_tile(tl, carry):
            cp = pltpu.make_async_copy(zeros_ref, _token_rows(xs_hbm, tl * tile_rows, tile_rows, rpt), zsem)
            cp.start()
            cp.wait()
            return carry

        lax.fori_loop(nused_ref[0], xs_hbm.shape[0] // (tile_rows * rpt), tail_tile, 0)

        def per_expert(e, carry):
            def piece(off, size):
                cp = pltpu.make_async_copy(_token_rows(zeros_ref, 0, size, rpt),
                                           _token_rows(xs_hbm, padlo_ref[e] + off, size, rpt), zsem)
                cp.start()
                cp.wait()

            _run_pieces(padlen_ref[e], piece)
            return carry

        lax.fori_loop(0, N_EXPERTS, per_expert, 0)

    @pl.when(step >= 2)
    def _():
        wait_runs(slot)

    x = x_ref[...]
    pos_by_choice = [post_ref[kk:kk + 1, :] for kk in range(EXPERT_TOPK)]
    chunk = MOBA_BLOCK
    for rc in range(n_assign // chunk):
        r = (lax.broadcasted_iota(jnp.int32, (chunk, 1), 0) + rc * chunk).astype(F32)
        select = jnp.zeros((chunk, tt), F32)
        for pos in pos_by_choice:
            select = jnp.where(r == pos, 1.0, select)
        rows = jnp.dot(select.astype(BF16), x, preferred_element_type=F32)
        words = _pack_bf16_pairs(rows)
        for c in range(rpt):
            sorted_ref[slot, pl.ds(rc * chunk * rpt + c, chunk, stride=rpt), :] = words[:, c * LANES:(c + 1) * LANES]
    start_runs(step, slot)

    @pl.when(step == n_steps - 1)
    def _():
        @pl.when(step >= 1)
        def _():
            wait_runs(1 - slot)

        wait_runs(slot)


def _dispatch(run_len, run_tpos, run_dst, padlo, padlen, n_used, pos_t, xn2, n_rows):
    t, d = xn2.shape
    n_assign = ROUTE_TOKENS * EXPERT_TOPK
    grid_spec = pltpu.PrefetchScalarGridSpec(
        num_scalar_prefetch=6,
        grid=(t // ROUTE_TOKENS,),
        in_specs=[pl.BlockSpec((SUBLANES, ROUTE_TOKENS), lambda c, *_: (c, 0)),
                  pl.BlockSpec((ROUTE_TOKENS, d), lambda c, *_: (c, 0))],
        out_specs=pl.BlockSpec(memory_space=pl.ANY),
        scratch_shapes=[
            pltpu.VMEM((2, n_assign * PACKED_ROWS, LANES), jnp.uint32),
            pltpu.VMEM((EXPERT_ROWS * PACKED_ROWS, LANES), jnp.uint32),
            pltpu.SemaphoreType.DMA((2,)),
            pltpu.SemaphoreType.DMA(()),
        ],
    )
    assert d == 2 * PACKED_ROWS * LANES
    return pl.pallas_call(
        _dispatch_kernel,
        grid_spec=grid_spec,
        out_shape=jax.ShapeDtypeStruct((n_rows * PACKED_ROWS, LANES), jnp.uint32),
        compiler_params=pltpu.CompilerParams(dimension_semantics=("arbitrary",), vmem_limit_bytes=VMEM_LIMIT),
        name="dispatch",
    )(run_len, run_tpos, run_dst, padlo, padlen, n_used, pos_t, xn2)


def _expert_kernel(be_ref, nused_ref, valid_ref, x_ref, wgu_ref, bg_ref, bu_ref, wd_ref, bd_ref, y_ref,
                   wgu_bf, wd_bf):
    t = pl.program_id(0)
    rows = y_ref.shape[0] // SUBLANES
    de = wd_ref.shape[0]
    grp = 2 * LANES
    active = t < nused_ref[0]
    fresh = jnp.logical_and(
        active, jnp.logical_or(t == 0, be_ref[t] != be_ref[jnp.maximum(t - 1, 0)]))

    @pl.when(fresh)
    def _():
        src = lax.broadcasted_iota(jnp.int32, (grp, grp), 0)
        dst = lax.broadcasted_iota(jnp.int32, (grp, grp), 1)
        wanted = jnp.where(dst < LANES, 2 * dst, 2 * (dst - LANES) + 1)
        perm = jnp.where(src == wanted, 1.0, 0.0).astype(BF16)
        for g in range(wgu_ref.shape[1] // grp):
            cols = slice(g * grp, (g + 1) * grp)
            wgu_bf[:, cols] = jnp.dot(wgu_ref[:, cols].astype(BF16), perm,
                                      preferred_element_type=F32).astype(BF16)
        wd_bf[...] = wd_ref[...].astype(BF16)

    def mlp(n):
        words = jnp.concatenate([x_ref[pl.ds(c, n, stride=PACKED_ROWS), :] for c in range(PACKED_ROWS)], axis=1)
        xb = _unpack_bf16_pairs(words)
        gu = jnp.dot(xb, wgu_bf[...], preferred_element_type=F32)
        hid = []
        for g in range(de // LANES):
            cols = slice(g * LANES, (g + 1) * LANES)
            gate = gu[:, g * grp:g * grp + LANES] + bg_ref[:, cols]
            up = gu[:, g * grp + LANES:(g + 1) * grp] + bu_ref[:, cols]
            gate = jnp.minimum(gate, SWIGLU_LIMIT)
            up = jnp.clip(up, -SWIGLU_LIMIT, SWIGLU_LIMIT)
            hid.append((gate * _sigmoid(SWIGLU_ALPHA * gate) * (up + 1.0)).astype(BF16))
        y = jnp.dot(jnp.concatenate(hid, axis=1), wd_bf[...], preferred_element_type=F32) + bd_ref[...]
        _store_token_rows(y_ref, y)
        if n < rows:
            y_ref[n * SUBLANES:, :] = jnp.zeros(((rows - n) * SUBLANES, LANES), F32)

    half_only = valid_ref[t] <= rows // 2

    @pl.when(jnp.logical_and(active, jnp.logical_not(half_only)))
    def _():
        mlp(rows)

    @pl.when(jnp.logical_and(active, half_only))
    def _():
        mlp(rows // 2)

    @pl.when(jnp.logical_not(active))
    def _():
        y_ref[...] = jnp.zeros_like(y_ref)


def _experts(block_expert, n_used, tile_valid, xs_rows, wgu, bg, bu, wd, bd):
    n_rows = xs_rows.shape[0] // PACKED_ROWS
    _, d, de2 = wgu.shape
    de = wd.shape[1]
    n_tiles = n_rows // EXPERT_ROWS
    tile_rows = EXPERT_ROWS * SUBLANES
    per_expert = lambda t, be, nu, va: (be[t], 0, 0)
    grid_spec = pltpu.PrefetchScalarGridSpec(
        num_scalar_prefetch=3,
        grid=(n_tiles,),
        in_specs=[
            pl.BlockSpec((EXPERT_ROWS * PACKED_ROWS, LANES),
                         lambda t, be, nu, va: (jnp.minimum(t, jnp.maximum(nu[0] - 1, 0)), 0)),
            pl.BlockSpec((None, d, de2), per_expert),
            pl.BlockSpec((None, 1, de), per_expert),
            pl.BlockSpec((None, 1, de), per_expert),
            pl.BlockSpec((None, de, d), per_expert),
            pl.BlockSpec((None, 1, d), per_expert),
        ],
        out_specs=pl.BlockSpec((tile_rows, LANES), lambda t, be, nu, va: (t, 0)),
        scratch_shapes=[pltpu.VMEM((d, de2), BF16), pltpu.VMEM((de, d), BF16)],
    )
    return pl.pallas_call(
        _expert_kernel,
        grid_spec=grid_spec,
        out_shape=jax.ShapeDtypeStruct((n_rows * SUBLANES, LANES), F32),
        compiler_params=pltpu.CompilerParams(
            dimension_semantics=("arbitrary",), vmem_limit_bytes=BIG_VMEM_LIMIT),
        name="experts",
    )(block_expert, n_used, tile_valid, xs_rows, wgu, bg, bu, wd, bd)


def _combine_kernel(len_ref, tpos_ref, src_ref, pos_ref, w_ref, h1_ref, fg_ref, ys_hbm,
                    o_ref, buf_ref, h1rows_ref, outrows_ref, sems):
    step = pl.program_id(0)
    n_steps = pl.num_programs(0)
    tt = h1_ref.shape[0]
    slot = lax.rem(step, 2)

    def start_runs(tile, buf):
        def per_expert(e, carry):
            k = tile * N_EXPERTS + e

            def piece(off, size):
                pltpu.make_async_copy(_token_rows(ys_hbm, src_ref[k] + off, size),
                                      _token_rows(buf_ref.at[buf], tpos_ref[k] + off, size), sems.at[buf]).start()

            _run_pieces(len_ref[k], piece)
            return carry

        lax.fori_loop(0, N_EXPERTS, per_expert, 0)

    def wait_runs(buf):
        whole = buf_ref.at[buf]
        pltpu.make_async_copy(_token_rows(ys_hbm, 0, tt * EXPERT_TOPK), whole, sems.at[buf]).wait()

    @pl.when(step == 0)
    def _():
        start_runs(0, 0)

    @pl.when(step + 1 < n_steps)
    def _():
        start_runs(step + 1, 1 - slot)

    wait_runs(slot)

    _store_token_rows(h1rows_ref, h1_ref[...])

    def one_token(tok, carry):
        acc = h1rows_ref[pl.ds(pl.multiple_of(tok * SUBLANES, SUBLANES), SUBLANES), :]
        for kk in range(EXPERT_TOPK):
            a = tok * EXPERT_TOPK + kk
            pos = pos_ref[a]
            acc = acc + w_ref[a] * buf_ref[slot, pl.ds(pl.multiple_of(pos * SUBLANES, SUBLANES), SUBLANES), :]
        outrows_ref[pl.ds(pl.multiple_of(tok * SUBLANES, SUBLANES), SUBLANES), :] = acc
        return carry

    lax.fori_loop(0, tt, one_token, 0, unroll=4)
    h = _load_token_rows(outrows_ref, tt)
    o_ref[...] = h * lax.rsqrt(jnp.mean(h * h, axis=-1, keepdims=True) + NORM_EPS) * fg_ref[...]


def _combine(run_len, run_tpos, run_src, pos, w_flat, h1, fg, ys_rows):
    t, d = h1.shape
    tt = ROUTE_TOKENS
    n_assign = tt * EXPERT_TOPK
    smem_blk = pl.BlockSpec((n_assign,), lambda c, *_: (c,), memory_space=pltpu.SMEM)
    grid_spec = pltpu.PrefetchScalarGridSpec(
        num_scalar_prefetch=3,
        grid=(t // tt,),
        in_specs=[
            smem_blk, smem_blk,
            pl.BlockSpec((tt, d), lambda c, *_: (c, 0)),
            pl.BlockSpec((1, d), lambda c, *_: (0, 0)),
            pl.BlockSpec(memory_space=pl.ANY),
        ],
        out_specs=pl.BlockSpec((tt, d), lambda c, *_: (c, 0)),
        scratch_shapes=[
            pltpu.VMEM((2, n_assign * SUBLANES, LANES), F32),
            pltpu.VMEM((tt * SUBLANES, LANES), F32),
            pltpu.VMEM((tt * SUBLANES, LANES), F32),
            pltpu.SemaphoreType.DMA((2,)),
        ],
    )
    return pl.pallas_call(
        _combine_kernel,
        grid_spec=grid_spec,
        out_shape=jax.ShapeDtypeStruct((t, d), F32),
        compiler_params=pltpu.CompilerParams(dimension_semantics=("arbitrary",), vmem_limit_bytes=VMEM_LIMIT),
        name="combine",
    )(run_len, run_tpos, run_src, pos, w_flat, h1, fg, ys_rows)


def _layer(h, mix_norm_g, w_in, w_attn_out, sgu_ln_g, sgu_ln_b, w_spatial, b_spatial, w_sgu_out,
           w_mix_out, ffn_norm_g, w_router, b_router, w_gate_up, b_gate_up, w_down, b_down, out_g):
    b, s, d = h.shape
    t = b * s
    sgu_width = sgu_ln_g.shape[0]
    x2 = h.reshape(t, d)

    q, qt, kt, v, kmean, u, vvn, ga, gs = _in_proj(
        x2, mix_norm_g.reshape(1, d), w_in.astype(BF16), sgu_ln_g.reshape(1, sgu_width),
        sgu_ln_b.reshape(1, sgu_width), sgu_width)

    nb = s // MOBA_BLOCK
    attn = _moba(q.reshape(b, s, ATTN_WIDTH), qt, kt, v.reshape(b, s, ATTN_WIDTH),
                 kmean.reshape(b, nb, ATTN_WIDTH), _moba_key_table(s)).reshape(t, ATTN_WIDTH)

    gdim = sgu_width // SGU_GROUPS
    bsp_full = jnp.repeat(b_spatial.T, gdim, axis=1)
    wr_pad = jnp.zeros((d, LANES), F32).at[:, :N_EXPERTS].set(w_router)
    br_pad = jnp.zeros((1, LANES), F32).at[0, :N_EXPERTS].set(b_router)
    h1, xn2, ri, rw, tile_cnt, pos_t = _mix(
        attn, u, vvn, ga, gs, x2, w_attn_out.astype(BF16), w_sgu_out.astype(BF16), w_mix_out.astype(BF16),
        w_spatial, bsp_full, ffn_norm_g.reshape(1, d), wr_pad, br_pad)

    n_assign = t * EXPERT_TOPK
    n_tiles = -(-(n_assign + N_EXPERTS * (EXPERT_ROWS - 1)) // EXPERT_ROWS)
    n_rows = n_tiles * EXPERT_ROWS
    run_len = tile_cnt[::SUBLANES, :N_EXPERTS].astype(jnp.int32)
    counts = jnp.sum(run_len, axis=0)
    padded = (counts + EXPERT_ROWS - 1) // EXPERT_ROWS * EXPERT_ROWS
    pad_end = jnp.cumsum(padded)
    starts = pad_end - padded
    run_tpos = jnp.cumsum(run_len, axis=1) - run_len
    run_row = starts[None, :] + jnp.cumsum(run_len, axis=0) - run_len
    tile_start = jnp.arange(n_tiles, dtype=jnp.int32) * EXPERT_ROWS
    block_expert = jnp.minimum(
        jnp.sum((pad_end[None, :] <= tile_start[:, None]).astype(jnp.int32), axis=1), N_EXPERTS - 1)
    n_used = (pad_end[-1:] // EXPERT_ROWS).astype(jnp.int32)
    tile_valid = jnp.clip(counts[block_expert] - (tile_start - starts[block_expert]), 0, EXPERT_ROWS)
    pos = ri[:, EXPERT_TOPK:2 * EXPERT_TOPK].reshape(n_assign)
    tables = (run_len.reshape(-1), run_tpos.reshape(-1), run_row.reshape(-1))

    xs_rows = _dispatch(*tables, starts + counts, padded - counts, n_used, pos_t, xn2, n_rows)

    de = w_down.shape[1]
    bg = b_gate_up[:, 0::2].reshape(N_EXPERTS, 1, de)
    bu = b_gate_up[:, 1::2].reshape(N_EXPERTS, 1, de)
    ys_rows = _experts(block_expert, n_used, tile_valid.astype(jnp.int32), xs_rows, w_gate_up, bg, bu, w_down,
                       b_down.reshape(N_EXPERTS, 1, d))

    out = _combine(*tables, pos, rw[:, :EXPERT_TOPK].reshape(n_assign), h1, out_g.reshape(1, d), ys_rows)
    return out.reshape(b, s, d)


def kernel(x, mix_norm_g, w_in, w_attn_out, sgu_ln_g, sgu_ln_b, w_spatial, b_spatial, w_sgu_out, w_mix_out,
           ffn_norm_g, w_router, b_router, w_gate_up, b_gate_up, w_down, b_down, final_norm_g):
    depth = w_in.shape[0]
    assert depth == 1, "the final RMSNorm is fused into the single layer's combine step"
    return _layer(x, mix_norm_g[0], w_in[0], w_attn_out[0], sgu_ln_g[0], sgu_ln_b[0], w_spatial[0],
                  b_spatial[0], w_sgu_out[0], w_mix_out[0], ffn_norm_g[0], w_router[0], b_router[0],
                  w_gate_up[0], b_gate_up[0], w_down[0], b_down[0], final_norm_g)
```

```python
import jax
import jax.numpy as jnp
import numpy as np
from jax import lax
from jax.experimental import pallas as pl
from jax.experimental.pallas import tpu as pltpu

F32 = jnp.float32
BF16 = jnp.bfloat16
NEG_INF = float("-inf")
MASK_VALUE = -1e30


def _bf16_pieces(x, n=3):
    pieces = []
    for _ in range(n):
        p = float(np.asarray(x, np.float32).astype(BF16).astype(np.float32))
        pieces.append(p)
        x = x - p
    return pieces


LOG2E = 1.4426950408889634
LOG2E_PIECES = _bf16_pieces(LOG2E)

N_HEADS = 8
HEAD_DIM = 64
ATTN_WIDTH = N_HEADS * HEAD_DIM
MOBA_BLOCK = 256
MOBA_TOPK = 3
SGU_CHUNK = 128
SGU_GROUPS = 8
N_EXPERTS = 32
EXPERT_TOPK = 4
SWIGLU_LIMIT = 7.0
SWIGLU_ALPHA = 1.702
NORM_EPS = 1e-5

LANES = 128
SUBLANES = 8
PACKED_ROWS = 4
HEADS_PER_LANE_TILE = LANES // HEAD_DIM
MOBA_GROUP = 1
MOBA_BIAS_LANE0 = 64
MOBA_DUMMY_LANE = LANES - 1
PROJ_ROWS = 256
EXPERT_ROWS = 512
ROUTE_TOKENS = 512
VMEM_LIMIT = 48 * 1024 * 1024
BIG_VMEM_LIMIT = 56 * 1024 * 1024


def _sigmoid(x):
    return 1.0 / (1.0 + jnp.exp(-x))


def _store_token_rows(ref, value, base=0):
    n = value.shape[0]
    for c in range(value.shape[1] // LANES):
        ref[pl.ds(base + c, n, stride=SUBLANES), :] = value[:, c * LANES:(c + 1) * LANES]


def _load_token_rows(ref, n, base=0):
    return jnp.concatenate([ref[pl.ds(base + c, n, stride=SUBLANES), :] for c in range(SUBLANES)], axis=1)


def _gelu_exact(x):
    return 0.5 * x * (1.0 + lax.erf(x * (0.5 ** 0.5)))


def _in_proj_kernel(x_ref, g_ref, w_ref, lng_ref, lnb_ref,
                    q_ref, qt_ref, kt_ref, v_ref, kmean_ref, u_ref, vvn_ref, ga_ref, gs_ref):
    x = x_ref[...]
    xn = x * lax.rsqrt(jnp.mean(x * x, axis=-1, keepdims=True) + NORM_EPS) * g_ref[...]
    xb = xn.astype(BF16)

    def proj(lo, hi):
        return jnp.dot(xb, w_ref[:, lo:hi], preferred_element_type=F32)

    a = ATTN_WIDTH
    sw = u_ref.shape[1]
    d = ga_ref.shape[1]
    qf = proj(0, a)
    q_ref[...] = (qf * (HEAD_DIM ** -0.5 * LOG2E)).astype(BF16)
    kf = proj(a, 2 * a)
    for j in range(kf.shape[0] // MOBA_BLOCK):
        qt_ref[j] = qf[j * MOBA_BLOCK:(j + 1) * MOBA_BLOCK].T
        kblk = kf[j * MOBA_BLOCK:(j + 1) * MOBA_BLOCK]
        kt_ref[j] = kblk.T.astype(BF16)
        kmean_ref[j] = jnp.mean(kblk, axis=0, keepdims=True)
    v_ref[...] = proj(2 * a, 3 * a).astype(BF16)
    z0 = 3 * a
    u_ref[...] = _gelu_exact(proj(z0, z0 + sw)).astype(BF16)
    zv = _gelu_exact(proj(z0 + sw, z0 + 2 * sw))
    mu = jnp.mean(zv, axis=-1, keepdims=True)
    zc = zv - mu
    var = jnp.mean(zc * zc, axis=-1, keepdims=True)
    vvn_ref[...] = (zc * lax.rsqrt(var + NORM_EPS) * lng_ref[...] + lnb_ref[...]).astype(BF16)
    g0 = z0 + 2 * sw
    ga_ref[...] = _sigmoid(proj(g0, g0 + d)).astype(BF16)
    gs_ref[...] = _sigmoid(proj(g0 + d, g0 + 2 * d)).astype(BF16)


def _in_proj(x2, g, w_bf, lng, lnb, sgu_width):
    t, d = x2.shape
    tm = PROJ_ROWS
    n_in = w_bf.shape[1]
    row = lambda w: pl.BlockSpec((tm, w), lambda i: (i, 0))
    const = lambda shape: pl.BlockSpec(shape, lambda i: (0,) * len(shape))
    out_shape = (
        jax.ShapeDtypeStruct((t, ATTN_WIDTH), BF16),
        jax.ShapeDtypeStruct((t // MOBA_BLOCK, ATTN_WIDTH, MOBA_BLOCK), F32),
        jax.ShapeDtypeStruct((t // MOBA_BLOCK, ATTN_WIDTH, MOBA_BLOCK), BF16),
        jax.ShapeDtypeStruct((t, ATTN_WIDTH), BF16),
        jax.ShapeDtypeStruct((t // MOBA_BLOCK, 1, ATTN_WIDTH), F32),
        jax.ShapeDtypeStruct((t, sgu_width), BF16),
        jax.ShapeDtypeStruct((t, sgu_width), BF16),
        jax.ShapeDtypeStruct((t, d), BF16),
        jax.ShapeDtypeStruct((t, d), BF16),
    )
    out_specs = (
        row(ATTN_WIDTH),
        pl.BlockSpec((tm // MOBA_BLOCK, ATTN_WIDTH, MOBA_BLOCK), lambda i: (i, 0, 0)),
        pl.BlockSpec((tm // MOBA_BLOCK, ATTN_WIDTH, MOBA_BLOCK), lambda i: (i, 0, 0)),
        row(ATTN_WIDTH),
        pl.BlockSpec((tm // MOBA_BLOCK, 1, ATTN_WIDTH), lambda i: (i, 0, 0)),
        row(sgu_width), row(sgu_width), row(d), row(d),
    )
    return pl.pallas_call(
        _in_proj_kernel,
        grid=(t // tm,),
        in_specs=[row(d), const((1, d)), const((d, n_in)), const((1, sgu_width)), const((1, sgu_width))],
        out_specs=out_specs,
        out_shape=out_shape,
        compiler_params=pltpu.CompilerParams(dimension_semantics=("arbitrary",), vmem_limit_bytes=VMEM_LIMIT),
        name="in_proj",
    )(x2, g, w_bf, lng, lnb)


def _alibi_slopes():
    slopes = 2.0 ** (-8.0 * np.arange(1, N_HEADS + 1, dtype=np.float64) / N_HEADS)
    assert all(np.log2(s) == np.round(np.log2(s)) for s in slopes), "ALiBi slopes must be powers of two"
    return slopes


def _moba_bias_lane(head_in_tile, part, piece):
    return MOBA_BIAS_LANE0 + (head_in_tile * 2 + part) * len(LOG2E_PIECES) + piece


def _moba_key_table(s):
    nb = s // MOBA_BLOCK
    slopes = _alibi_slopes()
    n_pairs = N_HEADS // HEADS_PER_LANE_TILE
    table = np.zeros((nb + 1, n_pairs, LANES, MOBA_BLOCK), np.float32)
    offs = np.arange(MOBA_BLOCK, dtype=np.float32)
    assert HEADS_PER_LANE_TILE * nb <= MOBA_BIAS_LANE0, "block one-hot rows must not overlap the ALiBi rows"
    for j in range(nb):
        for hh in range(HEADS_PER_LANE_TILE):
            table[j, :, hh * nb + j, :] = 1.0
        for p in range(n_pairs):
            for hh in range(HEADS_PER_LANE_TILE):
                slope = slopes[p * HEADS_PER_LANE_TILE + hh]
                for piece in range(len(LOG2E_PIECES)):
                    table[j, p, _moba_bias_lane(hh, 0, piece), :] = slope * MOBA_BLOCK * j
                    table[j, p, _moba_bias_lane(hh, 1, piece), :] = slope * offs
    table[nb, :, MOBA_DUMMY_LANE, :] = 1.0
    as_bf16 = table.astype(BF16)
    assert np.array_equal(as_bf16.astype(np.float32), table), "bias table must be exact in bf16"
    return jnp.asarray(as_bf16.reshape(nb + 1, n_pairs * LANES, MOBA_BLOCK))


def _moba_kernel(q_ref, qt_ref, qn_ref, qtn_ref, kt_ref, v_ref, km_ref, ct_ref, causal_ref, o_ref, qaug_ref,
                 s_ref, mpart_ref, mrow_ref, acc_ref):
    i = pl.program_id(2)
    nb = kt_ref.shape[0]
    blk = MOBA_BLOCK
    lane = lax.broadcasted_iota(jnp.int32, (1, LANES), 1)
    n_groups = lax.div(i + MOBA_GROUP, MOBA_GROUP)
    heads = range(HEADS_PER_LANE_TILE)
    hmasks = [(lane >= HEAD_DIM * hh) & (lane < HEAD_DIM * (hh + 1)) for hh in heads]
    slot = lax.rem(i, 2)

    def build_query_operand(q, qt, tile, into):
        bid = lax.broadcasted_iota(jnp.int32, (nb, blk), 0)
        bid_f = bid.astype(F32)
        masks_t = []
        for hh in heads:
            km_h = jnp.where(hmasks[hh], km_ref[...], 0.0)
            gate = jnp.dot(km_h, qt, precision=lax.Precision.HIGHEST, preferred_element_type=F32)
            gate = jnp.where(bid < tile, gate, NEG_INF)
            blockmask = jnp.where(bid == tile, 0.0, MASK_VALUE)
            for kk in range(MOBA_TOPK):
                gmax = jnp.max(gate, axis=0, keepdims=True)
                first = jnp.min(jnp.where(gate == gmax, bid_f, float(nb)), axis=0, keepdims=True)
                valid = (tile > kk).astype(F32)
                first = first * valid + (valid - 1.0)
                hit = bid_f == first
                blockmask = jnp.where(hit, 0.0, blockmask)
                gate = jnp.where(hit, NEG_INF, gate)
            masks_t.append(blockmask)
        masks_t.append(jnp.zeros((LANES - len(masks_t) * nb, blk), F32))
        masks = jnp.concatenate(masks_t, axis=0).T

        for hh in heads:
            feats = jnp.where(lane == MOBA_DUMMY_LANE, MASK_VALUE, 0.0)
            for part in range(2):
                for piece, value in enumerate(LOG2E_PIECES):
                    feats = jnp.where(lane == _moba_bias_lane(hh, part, piece), value, feats)
            extra = jnp.where((lane >= hh * nb) & (lane < (hh + 1) * nb), masks, feats)
            qaug_ref[into, hh * blk:(hh + 1) * blk, 0:LANES] = jnp.where(hmasks[hh], q, 0.0).astype(BF16)
            qaug_ref[into, hh * blk:(hh + 1) * blk, LANES:2 * LANES] = extra.astype(BF16)

    @pl.when(i == 0)
    def _():
        build_query_operand(q_ref[...], qt_ref[...], i, slot)

    def scores(jk, jc):
        rhs = jnp.concatenate([kt_ref[jk], ct_ref[jc]], axis=0)
        return jnp.dot(qaug_ref[slot], rhs, preferred_element_type=F32)

    def lane_halves_max(s):
        return jnp.maximum(s[:, 0:LANES], s[:, LANES:2 * LANES])

    n_rows = len(heads) * blk
    variants = [(groups, min(groups * MOBA_GROUP, nb)) for groups in range(1, pl.cdiv(nb, MOBA_GROUP) + 1)]

    for groups, n_blocks in variants:
        @pl.when(n_groups == groups)
        def _(n_blocks=n_blocks):
            part = None
            for j in range(n_blocks):
                s = scores(j, jnp.where(j <= i, j, nb)) + causal_ref[(i == j).astype(jnp.int32)]
                s_ref[j] = s
                part = lane_halves_max(s) if part is None else jnp.maximum(part, lane_halves_max(s))
            mpart_ref[...] = part

    ones = jnp.ones((blk, LANES), BF16)
    rowmax = jnp.max(mpart_ref[...], axis=1, keepdims=True)
    mrow_ref[...] = jnp.broadcast_to(rowmax, (n_rows, 2 * LANES))

    def weighted(j):
        p = jnp.exp2(s_ref[j] - mrow_ref[...]).astype(BF16)
        vaug = jnp.concatenate([v_ref[j * blk:(j + 1) * blk, :], ones], axis=1)
        return jnp.dot(p, vaug, preferred_element_type=F32)

    next_tile = jnp.minimum(i + 1, nb - 1)
    for groups, n_blocks in variants:
        @pl.when(n_groups == groups)
        def _(n_blocks=n_blocks):
            build_query_operand(qn_ref[...], qtn_ref[...], next_tile, 1 - slot)
            tot = weighted(0)
            for j in range(1, n_blocks):
                tot = tot + weighted(j)
            acc_ref[...] = tot

    out = jnp.zeros((blk, LANES), F32)
    for hh in heads:
        acc = acc_ref[hh * blk:(hh + 1) * blk, :]
        out = jnp.where(hmasks[hh], acc[:, 0:LANES] / acc[:, LANES:2 * LANES], out)
    o_ref[...] = out


def _moba(q, qt, kt, v, kmean, key_table):
    b, s, a = q.shape
    nb = s // MOBA_BLOCK
    n_pairs = a // LANES
    rows = HEADS_PER_LANE_TILE * MOBA_BLOCK
    row_in_block = np.arange(rows)[:, None] % MOBA_BLOCK
    causal = np.where(row_in_block >= np.arange(MOBA_BLOCK)[None, :], 0.0, MASK_VALUE).astype(np.float32)
    causal_tiles = jnp.asarray(np.stack([np.zeros_like(causal), causal]))
    nxt = lambda i: jnp.minimum(i + 1, nb - 1)
    return pl.pallas_call(
        _moba_kernel,
        grid=(b, n_pairs, nb),
        in_specs=[
            pl.BlockSpec((None, MOBA_BLOCK, LANES), lambda bi, p, i: (bi, i, p)),
            pl.BlockSpec((None, LANES, MOBA_BLOCK), lambda bi, p, i: (bi * nb + i, p, 0)),
            pl.BlockSpec((None, MOBA_BLOCK, LANES), lambda bi, p, i: (bi, nxt(i), p)),
            pl.BlockSpec((None, LANES, MOBA_BLOCK), lambda bi, p, i: (bi * nb + nxt(i), p, 0)),
            pl.BlockSpec((nb, LANES, MOBA_BLOCK), lambda bi, p, i: (bi, p, 0)),
            pl.BlockSpec((None, s, LANES), lambda bi, p, i: (bi, 0, p)),
            pl.BlockSpec((None, nb, LANES), lambda bi, p, i: (bi, 0, p)),
            pl.BlockSpec((nb + 1, LANES, MOBA_BLOCK), lambda bi, p, i: (0, p, 0)),
            pl.BlockSpec((2, rows, MOBA_BLOCK), lambda bi, p, i: (0, 0, 0)),
        ],
        out_specs=pl.BlockSpec((None, MOBA_BLOCK, LANES), lambda bi, p, i: (bi, i, p)),
        out_shape=jax.ShapeDtypeStruct((b, s, a), F32),
        scratch_shapes=[
            pltpu.VMEM((2, rows, 2 * LANES), BF16),
            pltpu.VMEM((nb + 1, rows, MOBA_BLOCK), F32),
            pltpu.VMEM((rows, LANES), F32),
            pltpu.VMEM((rows, 2 * LANES), F32),
            pltpu.VMEM((rows, 2 * LANES), F32),
        ],
        compiler_params=pltpu.CompilerParams(
            dimension_semantics=("arbitrary", "arbitrary", "arbitrary"), vmem_limit_bytes=VMEM_LIMIT),
        name="moba",
    )(q, qt, q, qt, kt, v, kmean, key_table, causal_tiles)


def _mix_kernel(attn_ref, u_ref, vvn_ref, ga_ref, gs_ref, x_ref,
                wao_ref, wso_ref, wmo_ref, wsp_ref, bsp_ref, fg_ref, wr_ref, br_ref,
                h1_ref, xn2_ref, ri_ref, rw_ref, tcnt_ref, post_ref, wcausal_ref):
    tm = x_ref.shape[0]
    ch = SGU_CHUNK

    @pl.when(pl.program_id(0) == 0)
    def _():
        tril = lax.broadcasted_iota(jnp.int32, (ch, ch), 0) >= lax.broadcasted_iota(jnp.int32, (ch, ch), 1)
        for g in range(SGU_GROUPS):
            wcausal_ref[g] = jnp.where(tril, wsp_ref[g], 0.0).astype(BF16)

    y_attn = jnp.dot(attn_ref[...].astype(BF16), wao_ref[...], preferred_element_type=F32)

    lane = lax.broadcasted_iota(jnp.int32, (1, LANES), 1)
    gdim = vvn_ref.shape[1] // SGU_GROUPS
    groups_per_tile = LANES // gdim
    w_causal = [wcausal_ref[g] for g in range(SGU_GROUPS)]
    rows = []
    for c in range(tm // ch):
        cols = []
        for ct in range(vvn_ref.shape[1] // LANES):
            vp = vvn_ref[c * ch:(c + 1) * ch, ct * LANES:(ct + 1) * LANES]
            acc = jnp.zeros((ch, LANES), F32)
            for gg in range(groups_per_tile):
                gmask = (lane >= gdim * gg) & (lane < gdim * (gg + 1))
                vm = jnp.where(gmask, vp, 0.0).astype(BF16)
                acc = acc + jnp.dot(w_causal[ct * groups_per_tile + gg], vm, preferred_element_type=F32)
            cols.append(acc)
        rows.append(jnp.concatenate(cols, axis=1) + bsp_ref[...])
    mixed = jnp.concatenate(rows, axis=0)
    sgu = u_ref[...].astype(F32) * mixed
    y_sgu = jnp.dot(sgu.astype(BF16), wso_ref[...], preferred_element_type=F32)

    merged = ga_ref[...].astype(F32) * y_attn + gs_ref[...].astype(F32) * y_sgu
    h1 = x_ref[...] + jnp.dot(merged.astype(BF16), wmo_ref[...], preferred_element_type=F32)
    h1_ref[...] = h1
    xn2 = h1 * lax.rsqrt(jnp.mean(h1 * h1, axis=-1, keepdims=True) + NORM_EPS) * fg_ref[...]
    xn2_ref[...] = xn2.astype(BF16)

    def split(a):
        hi = a.astype(BF16)
        return hi, (a - hi.astype(F32)).astype(BF16)

    x_hi, x_lo = split(xn2)
    w_hi, w_lo = split(wr_ref[...])
    logits = (jnp.dot(x_hi, w_hi, preferred_element_type=F32) + jnp.dot(x_lo, w_hi, preferred_element_type=F32)
              + jnp.dot(x_hi, w_lo, preferred_element_type=F32)) + br_ref[...]
    work = jnp.where(lane < N_EXPERTS, logits, NEG_INF)
    lane_f = lane.astype(F32)
    vals, idxs = [], []
    for _ in range(EXPERT_TOPK):
        vmax = jnp.max(work, axis=1, keepdims=True)
        first = jnp.min(jnp.where(work == vmax, lane_f, float(LANES)), axis=1, keepdims=True)
        vals.append(vmax)
        idxs.append(first)
        work = jnp.where(lane_f == first, NEG_INF, work)
    exps = [jnp.exp(v - vals[0]) for v in vals]
    denom = exps[0]
    for e in exps[1:]:
        denom = denom + e
    chosen = jnp.zeros((tm, LANES), F32)
    for first in idxs:
        chosen = jnp.where(lane_f == first, 1.0, chosen)

    strict_lower = (lax.broadcasted_iota(jnp.int32, (tm, tm), 0)
                    > lax.broadcasted_iota(jnp.int32, (tm, tm), 1))
    before = jnp.dot(jnp.where(strict_lower, 1.0, 0.0).astype(BF16), chosen.astype(BF16),
                     preferred_element_type=F32)
    counts = jnp.broadcast_to(jnp.sum(chosen, axis=0, keepdims=True), tcnt_ref.shape)
    lower_expert = (lax.broadcasted_iota(jnp.int32, (LANES, LANES), 0)
                    < lax.broadcasted_iota(jnp.int32, (LANES, LANES), 1))
    expert_start = jnp.dot(counts, jnp.where(lower_expert, 1.0, 0.0), precision=lax.Precision.HIGHEST,
                           preferred_element_type=F32)[0:1, :]
    sorted_pos = before + expert_start
    ri = jnp.zeros((tm, LANES), jnp.int32)
    rw = jnp.zeros((tm, LANES), F32)
    pos_lanes = jnp.zeros((tm, LANES), F32)
    for kk in range(EXPERT_TOPK):
        pos = jnp.sum(jnp.where(lane_f == idxs[kk], sorted_pos, 0.0), axis=1, keepdims=True)
        ri = jnp.where(lane == kk, idxs[kk].astype(jnp.int32), ri)
        ri = jnp.where(lane == EXPERT_TOPK + kk, pos.astype(jnp.int32), ri)
        rw = jnp.where(lane == kk, exps[kk] / denom, rw)
        pos_lanes = jnp.where(lane == kk, pos, pos_lanes)
    ri_ref[...] = ri
    rw_ref[...] = rw
    tcnt_ref[...] = counts
    post_ref[...] = pos_lanes.T[0:SUBLANES, :]


def _mix(attn, u, vvn, ga, gs, x2, wao, wso, wmo, wsp, bsp_full, fg, wr_pad, br_pad):
    t, d = x2.shape
    tm = ROUTE_TOKENS
    row = lambda w: pl.BlockSpec((tm, w), lambda i: (i, 0))
    const = lambda shape: pl.BlockSpec(shape, lambda i: (0,) * len(shape))
    ins = (attn, u, vvn, ga, gs, x2, wao, wso, wmo, wsp, bsp_full, fg, wr_pad, br_pad)
    in_specs = [row(a.shape[1]) for a in ins[:6]] + [const(a.shape) for a in ins[6:]]
    out_shape = (
        jax.ShapeDtypeStruct((t, d), F32),
        jax.ShapeDtypeStruct((t, d), BF16),
        jax.ShapeDtypeStruct((t, LANES), jnp.int32),
        jax.ShapeDtypeStruct((t, LANES), F32),
        jax.ShapeDtypeStruct((t // tm * SUBLANES, LANES), F32),
        jax.ShapeDtypeStruct((t // tm * SUBLANES, tm), F32),
    )
    assert d == SUBLANES * LANES, "a token row is stored as one (8, 128) tile"
    per_tile = lambda w: pl.BlockSpec((SUBLANES, w), lambda i: (i, 0))
    out_specs = (row(d), row(d), row(LANES), row(LANES), per_tile(LANES), per_tile(tm))
    return pl.pallas_call(
        _mix_kernel,
        grid=(t // tm,),
        in_specs=in_specs,
        out_specs=out_specs,
        out_shape=out_shape,
        scratch_shapes=[pltpu.VMEM(wsp.shape, BF16)],
        compiler_params=pltpu.CompilerParams(dimension_semantics=("arbitrary",), vmem_limit_bytes=BIG_VMEM_LIMIT),
        name="mix_route",
    )(*ins)


def _run_pieces(n, body):
    off = jnp.int32(0)
    for bit in reversed(range(ROUTE_TOKENS.bit_length())):
        size = 1 << bit
        take = lax.bitwise_and(n, size) != 0

        @pl.when(take)
        def _(off=off, size=size):
            body(off, size)

        off = off + jnp.where(take, size, 0)


def _token_rows(ref, r, n, rows_per_token=SUBLANES):
    return ref.at[pl.ds(pl.multiple_of(r * rows_per_token, rows_per_token), n * rows_per_token)]


def _pack_bf16_pairs(value):
    half = value.shape[1] // 2
    lo = lax.shift_right_logical(lax.bitcast_convert_type(value[:, :half], jnp.uint32), jnp.uint32(16))
    hi = lax.bitwise_and(lax.bitcast_convert_type(value[:, half:], jnp.uint32), jnp.uint32(0xFFFF0000))
    return lax.bitwise_or(lo, hi)


def _unpack_bf16_pairs(words):
    lo = lax.bitcast_convert_type(lax.shift_left(words, jnp.uint32(16)), F32)
    hi = lax.bitcast_convert_type(lax.bitwise_and(words, jnp.uint32(0xFFFF0000)), F32)
    return jnp.concatenate([lo, hi], axis=1).astype(BF16)


def _dispatch_kernel(len_ref, tpos_ref, dst_ref, padlo_ref, padlen_ref, nused_ref, post_ref, x_ref,
                     xs_hbm, sorted_ref, zeros_ref, sems, zsem):
    step = pl.program_id(0)
    n_steps = pl.num_programs(0)
    tt = x_ref.shape[0]
    n_assign = tt * EXPERT_TOPK
    rpt = PACKED_ROWS
    tile_rows = zeros_ref.shape[0] // rpt
    slot = lax.rem(step, 2)

    def start_runs(tile, buf):
        def per_expert(e, carry):
            k = tile * N_EXPERTS + e

            def piece(off, size):
                pltpu.make_async_copy(_token_rows(sorted_ref.at[buf], tpos_ref[k] + off, size, rpt),
                                      _token_rows(xs_hbm, dst_ref[k] + off, size, rpt), sems.at[buf]).start()

            _run_pieces(len_ref[k], piece)
            return carry

        lax.fori_loop(0, N_EXPERTS, per_expert, 0)

    def wait_runs(buf):
        whole = sorted_ref.at[buf]
        pltpu.make_async_copy(whole, _token_rows(xs_hbm, 0, n_assign, rpt), sems.at[buf]).wait()

    @pl.when(step == 0)
    def _():
        zeros_ref[...] = jnp.zeros_like(zeros_ref)

        def tail_tile(tl, carry):
            cp = pltpu.make_async_copy(zeros_ref, _token_rows(xs_hbm, tl * tile_rows, tile_rows, rpt), zsem)
            cp.start()
            cp.wait()
            return carry

        lax.fori_loop(nused_ref[0], xs_hbm.shape[0] // (tile_rows * rpt), tail_tile, 0)

        def per_expert(e, carry):
            def piece(off, size):
                cp = pltpu.make_async_copy(_token_rows(zeros_ref, 0, size, rpt),
                                           _token_rows(xs_hbm, padlo_ref[e] + off, size, rpt), zsem)
                cp.start()
                cp.wait()

            _run_pieces(padlen_ref[e], piece)
            return carry

        lax.fori_loop(0, N_EXPERTS, per_expert, 0)

    @pl.when(step >= 2)
    def _():
        wait_runs(slot)

    x = x_ref[...]
    pos_by_choice = [post_ref[kk:kk + 1, :] for kk in range(EXPERT_TOPK)]
    chunk = MOBA_BLOCK
    for rc in range(n_assign // chunk):
        r = (lax.broadcasted_iota(jnp.int32, (chunk, 1), 0) + rc * chunk).astype(F32)
        select = jnp.zeros((chunk, tt), F32)
        for pos in pos_by_choice:
            select = jnp.where(r == pos, 1.0, select)
        rows = jnp.dot(select.astype(BF16), x, preferred_element_type=F32)
        words = _pack_bf16_pairs(rows)
        for c in range(rpt):
            sorted_ref[slot, pl.ds(rc * chunk * rpt + c, chunk, stride=rpt), :] = words[:, c * LANES:(c + 1) * LANES]
    start_runs(step, slot)

    @pl.when(step == n_steps - 1)
    def _():
        @pl.when(step >= 1)
        def _():
            wait_runs(1 - slot)

        wait_runs(slot)


def _dispatch(run_len, run_tpos, run_dst, padlo, padlen, n_used, pos_t, xn2, n_rows):
    t, d = xn2.shape
    n_assign = ROUTE_TOKENS * EXPERT_TOPK
    grid_spec = pltpu.PrefetchScalarGridSpec(
        num_scalar_prefetch=6,
        grid=(t // ROUTE_TOKENS,),
        in_specs=[pl.BlockSpec((SUBLANES, ROUTE_TOKENS), lambda c, *_: (c, 0)),
                  pl.BlockSpec((ROUTE_TOKENS, d), lambda c, *_: (c, 0))],
        out_specs=pl.BlockSpec(memory_space=pl.ANY),
        scratch_shapes=[
            pltpu.VMEM((2, n_assign * PACKED_ROWS, LANES), jnp.uint32),
            pltpu.VMEM((EXPERT_ROWS * PACKED_ROWS, LANES), jnp.uint32),
            pltpu.SemaphoreType.DMA((2,)),
            pltpu.SemaphoreType.DMA(()),
        ],
    )
    assert d == 2 * PACKED_ROWS * LANES
    return pl.pallas_call(
        _dispatch_kernel,
        grid_spec=grid_spec,
        out_shape=jax.ShapeDtypeStruct((n_rows * PACKED_ROWS, LANES), jnp.uint32),
        compiler_params=pltpu.CompilerParams(dimension_semantics=("arbitrary",), vmem_limit_bytes=VMEM_LIMIT),
        name="dispatch",
    )(run_len, run_tpos, run_dst, padlo, padlen, n_used, pos_t, xn2)


def _expert_kernel(be_ref, nused_ref, x_ref, wgu_ref, bg_ref, bu_ref, wd_ref, bd_ref, y_ref, wgu_bf, wd_bf):
    t = pl.program_id(0)
    rows = y_ref.shape[0] // SUBLANES
    de = wd_ref.shape[0]
    grp = 2 * LANES
    active = t < nused_ref[0]
    fresh = jnp.logical_and(
        active, jnp.logical_or(t == 0, be_ref[t] != be_ref[jnp.maximum(t - 1, 0)]))

    @pl.when(fresh)
    def _():
        src = lax.broadcasted_iota(jnp.int32, (grp, grp), 0)
        dst = lax.broadcasted_iota(jnp.int32, (grp, grp), 1)
        wanted = jnp.where(dst < LANES, 2 * dst, 2 * (dst - LANES) + 1)
        perm = jnp.where(src == wanted, 1.0, 0.0).astype(BF16)
        for g in range(wgu_ref.shape[1] // grp):
            cols = slice(g * grp, (g + 1) * grp)
            wgu_bf[:, cols] = jnp.dot(wgu_ref[:, cols].astype(BF16), perm,
                                      preferred_element_type=F32).astype(BF16)
        wd_bf[...] = wd_ref[...].astype(BF16)

    @pl.when(active)
    def _():
        words = jnp.concatenate([x_ref[pl.ds(c, rows, stride=PACKED_ROWS), :] for c in range(PACKED_ROWS)], axis=1)
        xb = _unpack_bf16_pairs(words)
        gu = jnp.dot(xb, wgu_bf[...], preferred_element_type=F32)
        hid = []
        for g in range(de // LANES):
            cols = slice(g * LANES, (g + 1) * LANES)
            gate = gu[:, g * grp:g * grp + LANES] + bg_ref[:, cols]
            up = gu[:, g * grp + LANES:(g + 1) * grp] + bu_ref[:, cols]
            gate = jnp.minimum(gate, SWIGLU_LIMIT)
            up = jnp.clip(up, -SWIGLU_LIMIT, SWIGLU_LIMIT)
            hid.append((gate * _sigmoid(SWIGLU_ALPHA * gate) * (up + 1.0)).astype(BF16))
        y = jnp.dot(jnp.concatenate(hid, axis=1), wd_bf[...], preferred_element_type=F32) + bd_ref[...]
        _store_token_rows(y_ref, y)

    @pl.when(jnp.logical_not(active))
    def _():
        y_ref[...] = jnp.zeros_like(y_ref)


def _experts(block_expert, n_used, xs_rows, wgu, bg, bu, wd, bd):
    n_rows = xs_rows.shape[0] // PACKED_ROWS
    _, d, de2 = wgu.shape
    de = wd.shape[1]
    n_tiles = n_rows // EXPERT_ROWS
    tile_rows = EXPERT_ROWS * SUBLANES
    per_expert = lambda t, be, nu: (be[t], 0, 0)
    grid_spec = pltpu.PrefetchScalarGridSpec(
        num_scalar_prefetch=2,
        grid=(n_tiles,),
        in_specs=[
            pl.BlockSpec((EXPERT_ROWS * PACKED_ROWS, LANES),
                         lambda t, be, nu: (jnp.minimum(t, jnp.maximum(nu[0] - 1, 0)), 0)),
            pl.BlockSpec((None, d, de2), per_expert),
            pl.BlockSpec((None, 1, de), per_expert),
            pl.BlockSpec((None, 1, de), per_expert),
            pl.BlockSpec((None, de, d), per_expert),
            pl.BlockSpec((None, 1, d), per_expert),
        ],
        out_specs=pl.BlockSpec((tile_rows, LANES), lambda t, be, nu: (t, 0)),
        scratch_shapes=[pltpu.VMEM((d, de2), BF16), pltpu.VMEM((de, d), BF16)],
    )
    return pl.pallas_call(
        _expert_kernel,
        grid_spec=grid_spec,
        out_shape=jax.ShapeDtypeStruct((n_rows * SUBLANES, LANES), F32),
        compiler_params=pltpu.CompilerParams(
            dimension_semantics=("arbitrary",), vmem_limit_bytes=BIG_VMEM_LIMIT),
        name="experts",
    )(block_expert, n_used, xs_rows, wgu, bg, bu, wd, bd)


def _combine_kernel(len_ref, tpos_ref, src_ref, pos_ref, w_ref, h1_ref, fg_ref, ys_hbm,
                    o_ref, buf_ref, h1rows_ref, outrows_ref, sems):
    step = pl.program_id(0)
    n_steps = pl.num_programs(0)
    tt = h1_ref.shape[0]
    slot = lax.rem(step, 2)

    def start_runs(tile, buf):
        def per_expert(e, carry):
            k = tile * N_EXPERTS + e

            def piece(off, size):
                pltpu.make_async_copy(_token_rows(ys_hbm, src_ref[k] + off, size),
                                      _token_rows(buf_ref.at[buf], tpos_ref[k] + off, size), sems.at[buf]).start()

            _run_pieces(len_ref[k], piece)
            return carry

        lax.fori_loop(0, N_EXPERTS, per_expert, 0)

    def wait_runs(buf):
        whole = buf_ref.at[buf]
        pltpu.make_async_copy(_token_rows(ys_hbm, 0, tt * EXPERT_TOPK), whole, sems.at[buf]).wait()

    @pl.when(step == 0)
    def _():
        start_runs(0, 0)

    @pl.when(step + 1 < n_steps)
    def _():
        start_runs(step + 1, 1 - slot)

    wait_runs(slot)

    _store_token_rows(h1rows_ref, h1_ref[...])

    def one_token(tok, carry):
        acc = h1rows_ref[pl.ds(pl.multiple_of(tok * SUBLANES, SUBLANES), SUBLANES), :]
        for kk in range(EXPERT_TOPK):
            a = tok * EXPERT_TOPK + kk
            pos = pos_ref[a]
            acc = acc + w_ref[a] * buf_ref[slot, pl.ds(pl.multiple_of(pos * SUBLANES, SUBLANES), SUBLANES), :]
        outrows_ref[pl.ds(pl.multiple_of(tok * SUBLANES, SUBLANES), SUBLANES), :] = acc
        return carry

    lax.fori_loop(0, tt, one_token, 0, unroll=4)
    h = _load_token_rows(outrows_ref, tt)
    o_ref[...] = h * lax.rsqrt(jnp.mean(h * h, axis=-1, keepdims=True) + NORM_EPS) * fg_ref[...]


def _combine(run_len, run_tpos, run_src, pos, w_flat, h1, fg, ys_rows):
    t, d = h1.shape
    tt = ROUTE_TOKENS
    n_assign = tt * EXPERT_TOPK
    smem_blk = pl.BlockSpec((n_assign,), lambda c, *_: (c,), memory_space=pltpu.SMEM)
    grid_spec = pltpu.PrefetchScalarGridSpec(
        num_scalar_prefetch=3,
        grid=(t // tt,),
        in_specs=[
            smem_blk, smem_blk,
            pl.BlockSpec((tt, d), lambda c, *_: (c, 0)),
            pl.BlockSpec((1, d), lambda c, *_: (0, 0)),
            pl.BlockSpec(memory_space=pl.ANY),
        ],
        out_specs=pl.BlockSpec((tt, d), lambda c, *_: (c, 0)),
        scratch_shapes=[
            pltpu.VMEM((2, n_assign * SUBLANES, LANES), F32),
            pltpu.VMEM((tt * SUBLANES, LANES), F32),
            pltpu.VMEM((tt * SUBLANES, LANES), F32),
            pltpu.SemaphoreType.DMA((2,)),
        ],
    )
    return pl.pallas_call(
        _combine_kernel,
        grid_spec=grid_spec,
        out_shape=jax.ShapeDtypeStruct((t, d), F32),
        compiler_params=pltpu.CompilerParams(dimension_semantics=("arbitrary",), vmem_limit_bytes=VMEM_LIMIT),
        name="combine",
    )(run_len, run_tpos, run_src, pos, w_flat, h1, fg, ys_rows)


def _layer(h, mix_norm_g, w_in, w_attn_out, sgu_ln_g, sgu_ln_b, w_spatial, b_spatial, w_sgu_out,
           w_mix_out, ffn_norm_g, w_router, b_router, w_gate_up, b_gate_up, w_down, b_down, out_g):
    b, s, d = h.shape
    t = b * s
    sgu_width = sgu_ln_g.shape[0]
    x2 = h.reshape(t, d)

    q, qt, kt, v, kmean, u, vvn, ga, gs = _in_proj(
        x2, mix_norm_g.reshape(1, d), w_in.astype(BF16), sgu_ln_g.reshape(1, sgu_width),
        sgu_ln_b.reshape(1, sgu_width), sgu_width)

    nb = s // MOBA_BLOCK
    attn = _moba(q.reshape(b, s, ATTN_WIDTH), qt, kt, v.reshape(b, s, ATTN_WIDTH),
                 kmean.reshape(b, nb, ATTN_WIDTH), _moba_key_table(s)).reshape(t, ATTN_WIDTH)

    gdim = sgu_width // SGU_GROUPS
    bsp_full = jnp.repeat(b_spatial.T, gdim, axis=1)
    wr_pad = jnp.zeros((d, LANES), F32).at[:, :N_EXPERTS].set(w_router)
    br_pad = jnp.zeros((1, LANES), F32).at[0, :N_EXPERTS].set(b_router)
    h1, xn2, ri, rw, tile_cnt, pos_t = _mix(
        attn, u, vvn, ga, gs, x2, w_attn_out.astype(BF16), w_sgu_out.astype(BF16), w_mix_out.astype(BF16),
        w_spatial, bsp_full, ffn_norm_g.reshape(1, d), wr_pad, br_pad)

    n_assign = t * EXPERT_TOPK
    n_tiles = -(-(n_assign + N_EXPERTS * (EXPERT_ROWS - 1)) // EXPERT_ROWS)
    n_rows = n_tiles * EXPERT_ROWS
    run_len = tile_cnt[::SUBLANES, :N_EXPERTS].astype(jnp.int32)
    counts = jnp.sum(run_len, axis=0)
    padded = (counts + EXPERT_ROWS - 1) // EXPERT_ROWS * EXPERT_ROWS
    pad_end = jnp.cumsum(padded)
    starts = pad_end - padded
    run_tpos = jnp.cumsum(run_len, axis=1) - run_len
    run_row = starts[None, :] + jnp.cumsum(run_len, axis=0) - run_len
    tile_start = jnp.arange(n_tiles, dtype=jnp.int32) * EXPERT_ROWS
    block_expert = jnp.minimum(
        jnp.sum((pad_end[None, :] <= tile_start[:, None]).astype(jnp.int32), axis=1), N_EXPERTS - 1)
    n_used = (pad_end[-1:] // EXPERT_ROWS).astype(jnp.int32)
    pos = ri[:, EXPERT_TOPK:2 * EXPERT_TOPK].reshape(n_assign)
    tables = (run_len.reshape(-1), run_tpos.reshape(-1), run_row.reshape(-1))

    xs_rows = _dispatch(*tables, starts + counts, padded - counts, n_used, pos_t, xn2, n_rows)

    de = w_down.shape[1]
    bg = b_gate_up[:, 0::2].reshape(N_EXPERTS, 1, de)
    bu = b_gate_up[:, 1::2].reshape(N_EXPERTS, 1, de)
    ys_rows = _experts(block_expert, n_used, xs_rows, w_gate_up, bg, bu, w_down, b_down.reshape(N_EXPERTS, 1, d))

    out = _combine(*tables, pos, rw[:, :EXPERT_TOPK].reshape(n_assign), h1, out_g.reshape(1, d), ys_rows)
    return out.reshape(b, s, d)


def kernel(x, mix_norm_g, w_in, w_attn_out, sgu_ln_g, sgu_ln_b, w_spatial, b_spatial, w_sgu_out, w_mix_out,
           ffn_norm_g, w_router, b_router, w_gate_up, b_gate_up, w_down, b_down, final_norm_g):
    depth = w_in.shape[0]
    assert depth == 1, "the final RMSNorm is fused into the single layer's combine step"
    return _layer(x, mix_norm_g[0], w_in[0], w_attn_out[0], sgu_ln_g[0], sgu_ln_b[0], w_spatial[0],
                  b_spatial[0], w_sgu_out[0], w_mix_out[0], ffn_norm_g[0], w_router[0], b_router[0],
                  w_gate_up[0], b_gate_up[0], w_down[0], b_down[0], final_norm_g)
```

```python
import jax
import jax.numpy as jnp
import numpy as np
from jax import lax
from jax.experimental import pallas as pl
from jax.experimental.pallas import tpu as pltpu

F32 = jnp.float32
BF16 = jnp.bfloat16
NEG_INF = float("-inf")
MASK_VALUE = -1e30


def _bf16_pieces(x, n=3):
    pieces = []
    for _ in range(n):
        p = float(np.asarray(x, np.float32).astype(BF16).astype(np.float32))
        pieces.append(p)
        x = x - p
    return pieces


LOG2E = 1.4426950408889634
LOG2E_PIECES = _bf16_pieces(LOG2E)

N_HEADS = 8
HEAD_DIM = 64
ATTN_WIDTH = N_HEADS * HEAD_DIM
MOBA_BLOCK = 256
MOBA_TOPK = 3
SGU_CHUNK = 128
SGU_GROUPS = 8
N_EXPERTS = 32
EXPERT_TOPK = 4
SWIGLU_LIMIT = 7.0
SWIGLU_ALPHA = 1.702
NORM_EPS = 1e-5

LANES = 128
SUBLANES = 8
PACKED_ROWS = 4
HEADS_PER_LANE_TILE = LANES // HEAD_DIM
MOBA_GROUP = 1
MOBA_BIAS_LANE0 = 64
MOBA_DUMMY_LANE = LANES - 1
PROJ_ROWS = 256
EXPERT_ROWS = 432
ROUTE_TOKENS = 512
VMEM_LIMIT = 48 * 1024 * 1024
BIG_VMEM_LIMIT = 56 * 1024 * 1024


def _sigmoid(x):
    return 1.0 / (1.0 + jnp.exp(-x))


def _store_token_rows(ref, value, base=0):
    n = value.shape[0]
    for c in range(value.shape[1] // LANES):
        ref[pl.ds(base + c, n, stride=SUBLANES), :] = value[:, c * LANES:(c + 1) * LANES]


def _load_token_rows(ref, n, base=0):
    return jnp.concatenate([ref[pl.ds(base + c, n, stride=SUBLANES), :] for c in range(SUBLANES)], axis=1)


def _gelu_exact(x):
    return 0.5 * x * (1.0 + lax.erf(x * (0.5 ** 0.5)))


def _in_proj_kernel(x_ref, g_ref, w_ref, lng_ref, lnb_ref,
                    q_ref, qt_ref, kt_ref, v_ref, kmean_ref, u_ref, vvn_ref, ga_ref, gs_ref):
    x = x_ref[...]
    xn = x * lax.rsqrt(jnp.mean(x * x, axis=-1, keepdims=True) + NORM_EPS) * g_ref[...]
    xb = xn.astype(BF16)

    def proj(lo, hi):
        return jnp.dot(xb, w_ref[:, lo:hi], preferred_element_type=F32)

    a = ATTN_WIDTH
    sw = u_ref.shape[1]
    d = ga_ref.shape[1]
    qf = proj(0, a)
    q_ref[...] = qf
    kf = proj(a, 2 * a)
    for j in range(kf.shape[0] // MOBA_BLOCK):
        qt_ref[j] = qf[j * MOBA_BLOCK:(j + 1) * MOBA_BLOCK].T
        kblk = kf[j * MOBA_BLOCK:(j + 1) * MOBA_BLOCK]
        kt_ref[j] = kblk.T.astype(BF16)
        kmean_ref[j] = jnp.mean(kblk, axis=0, keepdims=True)
    v_ref[...] = proj(2 * a, 3 * a).astype(BF16)
    z0 = 3 * a
    u_ref[...] = _gelu_exact(proj(z0, z0 + sw))
    zv = _gelu_exact(proj(z0 + sw, z0 + 2 * sw))
    mu = jnp.mean(zv, axis=-1, keepdims=True)
    zc = zv - mu
    var = jnp.mean(zc * zc, axis=-1, keepdims=True)
    vvn_ref[...] = zc * lax.rsqrt(var + NORM_EPS) * lng_ref[...] + lnb_ref[...]
    g0 = z0 + 2 * sw
    ga_ref[...] = _sigmoid(proj(g0, g0 + d))
    gs_ref[...] = _sigmoid(proj(g0 + d, g0 + 2 * d))


def _in_proj(x2, g, w_bf, lng, lnb, sgu_width):
    t, d = x2.shape
    tm = PROJ_ROWS
    n_in = w_bf.shape[1]
    row = lambda w: pl.BlockSpec((tm, w), lambda i: (i, 0))
    const = lambda shape: pl.BlockSpec(shape, lambda i: (0,) * len(shape))
    out_shape = (
        jax.ShapeDtypeStruct((t, ATTN_WIDTH), F32),
        jax.ShapeDtypeStruct((t // MOBA_BLOCK, ATTN_WIDTH, MOBA_BLOCK), F32),
        jax.ShapeDtypeStruct((t // MOBA_BLOCK, ATTN_WIDTH, MOBA_BLOCK), BF16),
        jax.ShapeDtypeStruct((t, ATTN_WIDTH), BF16),
        jax.ShapeDtypeStruct((t // MOBA_BLOCK, 1, ATTN_WIDTH), F32),
        jax.ShapeDtypeStruct((t, sgu_width), F32),
        jax.ShapeDtypeStruct((t, sgu_width), F32),
        jax.ShapeDtypeStruct((t, d), F32),
        jax.ShapeDtypeStruct((t, d), F32),
    )
    out_specs = (
        row(ATTN_WIDTH),
        pl.BlockSpec((tm // MOBA_BLOCK, ATTN_WIDTH, MOBA_BLOCK), lambda i: (i, 0, 0)),
        pl.BlockSpec((tm // MOBA_BLOCK, ATTN_WIDTH, MOBA_BLOCK), lambda i: (i, 0, 0)),
        row(ATTN_WIDTH),
        pl.BlockSpec((tm // MOBA_BLOCK, 1, ATTN_WIDTH), lambda i: (i, 0, 0)),
        row(sgu_width), row(sgu_width), row(d), row(d),
    )
    return pl.pallas_call(
        _in_proj_kernel,
        grid=(t // tm,),
        in_specs=[row(d), const((1, d)), const((d, n_in)), const((1, sgu_width)), const((1, sgu_width))],
        out_specs=out_specs,
        out_shape=out_shape,
        compiler_params=pltpu.CompilerParams(dimension_semantics=("arbitrary",), vmem_limit_bytes=VMEM_LIMIT),
        name="in_proj",
    )(x2, g, w_bf, lng, lnb)


def _alibi_slopes():
    slopes = 2.0 ** (-8.0 * np.arange(1, N_HEADS + 1, dtype=np.float64) / N_HEADS)
    assert all(np.log2(s) == np.round(np.log2(s)) for s in slopes), "ALiBi slopes must be powers of two"
    return slopes


def _moba_bias_lane(head_in_tile, part, piece):
    return MOBA_BIAS_LANE0 + (head_in_tile * 2 + part) * len(LOG2E_PIECES) + piece


def _moba_key_table(s):
    nb = s // MOBA_BLOCK
    slopes = _alibi_slopes()
    n_pairs = N_HEADS // HEADS_PER_LANE_TILE
    table = np.zeros((nb + 1, n_pairs, LANES, MOBA_BLOCK), np.float32)
    offs = np.arange(MOBA_BLOCK, dtype=np.float32)
    assert HEADS_PER_LANE_TILE * nb <= MOBA_BIAS_LANE0, "block one-hot rows must not overlap the ALiBi rows"
    for j in range(nb):
        for hh in range(HEADS_PER_LANE_TILE):
            table[j, :, hh * nb + j, :] = 1.0
        for p in range(n_pairs):
            for hh in range(HEADS_PER_LANE_TILE):
                slope = slopes[p * HEADS_PER_LANE_TILE + hh]
                for piece in range(len(LOG2E_PIECES)):
                    table[j, p, _moba_bias_lane(hh, 0, piece), :] = slope * MOBA_BLOCK * j
                    table[j, p, _moba_bias_lane(hh, 1, piece), :] = slope * offs
    table[nb, :, MOBA_DUMMY_LANE, :] = 1.0
    as_bf16 = table.astype(BF16)
    assert np.array_equal(as_bf16.astype(np.float32), table), "bias table must be exact in bf16"
    return jnp.asarray(as_bf16.reshape(nb + 1, n_pairs * LANES, MOBA_BLOCK))


def _moba_kernel(q_ref, qt_ref, qn_ref, qtn_ref, kt_ref, v_ref, km_ref, ct_ref, causal_ref, o_ref, qaug_ref,
                 s_ref, mpart_ref, mrow_ref, acc_ref):
    i = pl.program_id(2)
    nb = kt_ref.shape[0]
    blk = MOBA_BLOCK
    lane = lax.broadcasted_iota(jnp.int32, (1, LANES), 1)
    n_groups = lax.div(i + MOBA_GROUP, MOBA_GROUP)
    heads = range(HEADS_PER_LANE_TILE)
    hmasks = [(lane >= HEAD_DIM * hh) & (lane < HEAD_DIM * (hh + 1)) for hh in heads]
    slot = lax.rem(i, 2)

    def build_query_operand(q, qt, tile, into):
        bid = lax.broadcasted_iota(jnp.int32, (nb, blk), 0)
        bid_f = bid.astype(F32)
        masks_t = []
        for hh in heads:
            km_h = jnp.where(hmasks[hh], km_ref[...], 0.0)
            gate = jnp.dot(km_h, qt, precision=lax.Precision.HIGHEST, preferred_element_type=F32)
            gate = jnp.where(bid < tile, gate, NEG_INF)
            blockmask = jnp.where(bid == tile, 0.0, MASK_VALUE)
            for kk in range(MOBA_TOPK):
                gmax = jnp.max(gate, axis=0, keepdims=True)
                first = jnp.min(jnp.where(gate == gmax, bid_f, float(nb)), axis=0, keepdims=True)
                valid = (tile > kk).astype(F32)
                first = first * valid + (valid - 1.0)
                hit = bid_f == first
                blockmask = jnp.where(hit, 0.0, blockmask)
                gate = jnp.where(hit, NEG_INF, gate)
            masks_t.append(blockmask)
        masks_t.append(jnp.zeros((LANES - len(masks_t) * nb, blk), F32))
        masks = jnp.concatenate(masks_t, axis=0).T

        for hh in heads:
            feats = jnp.where(lane == MOBA_DUMMY_LANE, MASK_VALUE, 0.0)
            for part in range(2):
                for piece, value in enumerate(LOG2E_PIECES):
                    feats = jnp.where(lane == _moba_bias_lane(hh, part, piece), value, feats)
            extra = jnp.where((lane >= hh * nb) & (lane < (hh + 1) * nb), masks, feats)
            qh = jnp.where(hmasks[hh], q, 0.0)
            qaug_ref[into, hh * blk:(hh + 1) * blk, 0:LANES] = (qh * (HEAD_DIM ** -0.5 * LOG2E)).astype(BF16)
            qaug_ref[into, hh * blk:(hh + 1) * blk, LANES:2 * LANES] = extra.astype(BF16)

    @pl.when(i == 0)
    def _():
        build_query_operand(q_ref[...], qt_ref[...], i, slot)

    def scores(jk, jc):
        rhs = jnp.concatenate([kt_ref[jk], ct_ref[jc]], axis=0)
        return jnp.dot(qaug_ref[slot], rhs, preferred_element_type=F32)

    def lane_halves_max(s):
        return jnp.maximum(s[:, 0:LANES], s[:, LANES:2 * LANES])

    n_rows = len(heads) * blk
    variants = [(groups, min(groups * MOBA_GROUP, nb)) for groups in range(1, pl.cdiv(nb, MOBA_GROUP) + 1)]

    for groups, n_blocks in variants:
        @pl.when(n_groups == groups)
        def _(n_blocks=n_blocks):
            part = None
            for j in range(n_blocks):
                s = scores(j, jnp.where(j <= i, j, nb)) + causal_ref[(i == j).astype(jnp.int32)]
                s_ref[j] = s
                part = lane_halves_max(s) if part is None else jnp.maximum(part, lane_halves_max(s))
            mpart_ref[...] = part

    ones = jnp.ones((blk, LANES), BF16)
    rowmax = jnp.max(mpart_ref[...], axis=1, keepdims=True)
    mrow_ref[...] = jnp.broadcast_to(rowmax, (n_rows, 2 * LANES))

    def weighted(j):
        p = jnp.exp2(s_ref[j] - mrow_ref[...]).astype(BF16)
        vaug = jnp.concatenate([v_ref[j * blk:(j + 1) * blk, :], ones], axis=1)
        return jnp.dot(p, vaug, preferred_element_type=F32)

    next_tile = jnp.minimum(i + 1, nb - 1)
    for groups, n_blocks in variants:
        @pl.when(n_groups == groups)
        def _(n_blocks=n_blocks):
            build_query_operand(qn_ref[...], qtn_ref[...], next_tile, 1 - slot)
            tot = weighted(0)
            for j in range(1, n_blocks):
                tot = tot + weighted(j)
            acc_ref[...] = tot

    out = jnp.zeros((blk, LANES), F32)
    for hh in heads:
        acc = acc_ref[hh * blk:(hh + 1) * blk, :]
        out = jnp.where(hmasks[hh], acc[:, 0:LANES] / acc[:, LANES:2 * LANES], out)
    o_ref[...] = out


def _moba(q, qt, kt, v, kmean, key_table):
    b, s, a = q.shape
    nb = s // MOBA_BLOCK
    n_pairs = a // LANES
    rows = HEADS_PER_LANE_TILE * MOBA_BLOCK
    row_in_block = np.arange(rows)[:, None] % MOBA_BLOCK
    causal = np.where(row_in_block >= np.arange(MOBA_BLOCK)[None, :], 0.0, MASK_VALUE).astype(np.float32)
    causal_tiles = jnp.asarray(np.stack([np.zeros_like(causal), causal]))
    nxt = lambda i: jnp.minimum(i + 1, nb - 1)
    return pl.pallas_call(
        _moba_kernel,
        grid=(b, n_pairs, nb),
        in_specs=[
            pl.BlockSpec((None, MOBA_BLOCK, LANES), lambda bi, p, i: (bi, i, p)),
            pl.BlockSpec((None, LANES, MOBA_BLOCK), lambda bi, p, i: (bi * nb + i, p, 0)),
            pl.BlockSpec((None, MOBA_BLOCK, LANES), lambda bi, p, i: (bi, nxt(i), p)),
            pl.BlockSpec((None, LANES, MOBA_BLOCK), lambda bi, p, i: (bi * nb + nxt(i), p, 0)),
            pl.BlockSpec((nb, LANES, MOBA_BLOCK), lambda bi, p, i: (bi, p, 0)),
            pl.BlockSpec((None, s, LANES), lambda bi, p, i: (bi, 0, p)),
            pl.BlockSpec((None, nb, LANES), lambda bi, p, i: (bi, 0, p)),
            pl.BlockSpec((nb + 1, LANES, MOBA_BLOCK), lambda bi, p, i: (0, p, 0)),
            pl.BlockSpec((2, rows, MOBA_BLOCK), lambda bi, p, i: (0, 0, 0)),
        ],
        out_specs=pl.BlockSpec((None, MOBA_BLOCK, LANES), lambda bi, p, i: (bi, i, p)),
        out_shape=jax.ShapeDtypeStruct((b, s, a), F32),
        scratch_shapes=[
            pltpu.VMEM((2, rows, 2 * LANES), BF16),
            pltpu.VMEM((nb + 1, rows, MOBA_BLOCK), F32),
            pltpu.VMEM((rows, LANES), F32),
            pltpu.VMEM((rows, 2 * LANES), F32),
            pltpu.VMEM((rows, 2 * LANES), F32),
        ],
        compiler_params=pltpu.CompilerParams(
            dimension_semantics=("arbitrary", "arbitrary", "arbitrary"), vmem_limit_bytes=VMEM_LIMIT),
        name="moba",
    )(q, qt, q, qt, kt, v, kmean, key_table, causal_tiles)


def _mix_kernel(attn_ref, u_ref, vvn_ref, ga_ref, gs_ref, x_ref,
                wao_ref, wso_ref, wmo_ref, wsp_ref, bsp_ref, fg_ref, wr_ref, br_ref,
                h1_ref, xn2_ref, ri_ref, rw_ref, tcnt_ref, post_ref, wcausal_ref):
    tm = x_ref.shape[0]
    ch = SGU_CHUNK

    @pl.when(pl.program_id(0) == 0)
    def _():
        tril = lax.broadcasted_iota(jnp.int32, (ch, ch), 0) >= lax.broadcasted_iota(jnp.int32, (ch, ch), 1)
        for g in range(SGU_GROUPS):
            wcausal_ref[g] = jnp.where(tril, wsp_ref[g], 0.0).astype(BF16)

    y_attn = jnp.dot(attn_ref[...].astype(BF16), wao_ref[...], preferred_element_type=F32)

    lane = lax.broadcasted_iota(jnp.int32, (1, LANES), 1)
    gdim = vvn_ref.shape[1] // SGU_GROUPS
    groups_per_tile = LANES // gdim
    w_causal = [wcausal_ref[g] for g in range(SGU_GROUPS)]
    rows = []
    for c in range(tm // ch):
        cols = []
        for ct in range(vvn_ref.shape[1] // LANES):
            vp = vvn_ref[c * ch:(c + 1) * ch, ct * LANES:(ct + 1) * LANES]
            acc = jnp.zeros((ch, LANES), F32)
            for gg in range(groups_per_tile):
                gmask = (lane >= gdim * gg) & (lane < gdim * (gg + 1))
                vm = jnp.where(gmask, vp, 0.0).astype(BF16)
                acc = acc + jnp.dot(w_causal[ct * groups_per_tile + gg], vm, preferred_element_type=F32)
            cols.append(acc)
        rows.append(jnp.concatenate(cols, axis=1) + bsp_ref[...])
    mixed = jnp.concatenate(rows, axis=0)
    sgu = u_ref[...] * mixed
    y_sgu = jnp.dot(sgu.astype(BF16), wso_ref[...], preferred_element_type=F32)

    merged = ga_ref[...] * y_attn + gs_ref[...] * y_sgu
    h1 = x_ref[...] + jnp.dot(merged.astype(BF16), wmo_ref[...], preferred_element_type=F32)
    h1_ref[...] = h1
    xn2 = h1 * lax.rsqrt(jnp.mean(h1 * h1, axis=-1, keepdims=True) + NORM_EPS) * fg_ref[...]
    xn2_ref[...] = xn2.astype(BF16)

    def split(a):
        hi = a.astype(BF16)
        return hi, (a - hi.astype(F32)).astype(BF16)

    x_hi, x_lo = split(xn2)
    w_hi, w_lo = split(wr_ref[...])
    logits = (jnp.dot(x_hi, w_hi, preferred_element_type=F32) + jnp.dot(x_lo, w_hi, preferred_element_type=F32)
              + jnp.dot(x_hi, w_lo, preferred_element_type=F32)) + br_ref[...]
    work = jnp.where(lane < N_EXPERTS, logits, NEG_INF)
    lane_f = lane.astype(F32)
    vals, idxs = [], []
    for _ in range(EXPERT_TOPK):
        vmax = jnp.max(work, axis=1, keepdims=True)
        first = jnp.min(jnp.where(work == vmax, lane_f, float(LANES)), axis=1, keepdims=True)
        vals.append(vmax)
        idxs.append(first)
        work = jnp.where(lane_f == first, NEG_INF, work)
    exps = [jnp.exp(v - vals[0]) for v in vals]
    denom = exps[0]
    for e in exps[1:]:
        denom = denom + e
    chosen = jnp.zeros((tm, LANES), F32)
    for first in idxs:
        chosen = jnp.where(lane_f == first, 1.0, chosen)

    strict_lower = (lax.broadcasted_iota(jnp.int32, (tm, tm), 0)
                    > lax.broadcasted_iota(jnp.int32, (tm, tm), 1))
    before = jnp.dot(jnp.where(strict_lower, 1.0, 0.0).astype(BF16), chosen.astype(BF16),
                     preferred_element_type=F32)
    counts = jnp.broadcast_to(jnp.sum(chosen, axis=0, keepdims=True), tcnt_ref.shape)
    lower_expert = (lax.broadcasted_iota(jnp.int32, (LANES, LANES), 0)
                    < lax.broadcasted_iota(jnp.int32, (LANES, LANES), 1))
    expert_start = jnp.dot(counts, jnp.where(lower_expert, 1.0, 0.0), precision=lax.Precision.HIGHEST,
                           preferred_element_type=F32)[0:1, :]
    sorted_pos = before + expert_start
    ri = jnp.zeros((tm, LANES), jnp.int32)
    rw = jnp.zeros((tm, LANES), F32)
    pos_lanes = jnp.zeros((tm, LANES), F32)
    for kk in range(EXPERT_TOPK):
        pos = jnp.sum(jnp.where(lane_f == idxs[kk], sorted_pos, 0.0), axis=1, keepdims=True)
        ri = jnp.where(lane == kk, idxs[kk].astype(jnp.int32), ri)
        ri = jnp.where(lane == EXPERT_TOPK + kk, pos.astype(jnp.int32), ri)
        rw = jnp.where(lane == kk, exps[kk] / denom, rw)
        pos_lanes = jnp.where(lane == kk, pos, pos_lanes)
    ri_ref[...] = ri
    rw_ref[...] = rw
    tcnt_ref[...] = counts
    post_ref[...] = pos_lanes.T[0:SUBLANES, :]


def _mix(attn, u, vvn, ga, gs, x2, wao, wso, wmo, wsp, bsp_full, fg, wr_pad, br_pad):
    t, d = x2.shape
    tm = ROUTE_TOKENS
    row = lambda w: pl.BlockSpec((tm, w), lambda i: (i, 0))
    const = lambda shape: pl.BlockSpec(shape, lambda i: (0,) * len(shape))
    ins = (attn, u, vvn, ga, gs, x2, wao, wso, wmo, wsp, bsp_full, fg, wr_pad, br_pad)
    in_specs = [row(a.shape[1]) for a in ins[:6]] + [const(a.shape) for a in ins[6:]]
    out_shape = (
        jax.ShapeDtypeStruct((t, d), F32),
        jax.ShapeDtypeStruct((t, d), BF16),
        jax.ShapeDtypeStruct((t, LANES), jnp.int32),
        jax.ShapeDtypeStruct((t, LANES), F32),
        jax.ShapeDtypeStruct((t // tm * SUBLANES, LANES), F32),
        jax.ShapeDtypeStruct((t // tm * SUBLANES, tm), F32),
    )
    assert d == SUBLANES * LANES, "a token row is stored as one (8, 128) tile"
    per_tile = lambda w: pl.BlockSpec((SUBLANES, w), lambda i: (i, 0))
    out_specs = (row(d), row(d), row(LANES), row(LANES), per_tile(LANES), per_tile(tm))
    return pl.pallas_call(
        _mix_kernel,
        grid=(t // tm,),
        in_specs=in_specs,
        out_specs=out_specs,
        out_shape=out_shape,
        scratch_shapes=[pltpu.VMEM(wsp.shape, BF16)],
        compiler_params=pltpu.CompilerParams(dimension_semantics=("arbitrary",), vmem_limit_bytes=BIG_VMEM_LIMIT),
        name="mix_route",
    )(*ins)


def _run_pieces(n, body):
    off = jnp.int32(0)
    for bit in reversed(range(ROUTE_TOKENS.bit_length())):
        size = 1 << bit
        take = lax.bitwise_and(n, size) != 0

        @pl.when(take)
        def _(off=off, size=size):
            body(off, size)

        off = off + jnp.where(take, size, 0)


def _token_rows(ref, r, n, rows_per_token=SUBLANES):
    return ref.at[pl.ds(pl.multiple_of(r * rows_per_token, rows_per_token), n * rows_per_token)]


def _pack_bf16_pairs(value):
    half = value.shape[1] // 2
    lo = lax.shift_right_logical(lax.bitcast_convert_type(value[:, :half], jnp.uint32), jnp.uint32(16))
    hi = lax.bitwise_and(lax.bitcast_convert_type(value[:, half:], jnp.uint32), jnp.uint32(0xFFFF0000))
    return lax.bitwise_or(lo, hi)


def _unpack_bf16_pairs(words):
    lo = lax.bitcast_convert_type(lax.shift_left(words, jnp.uint32(16)), F32)
    hi = lax.bitcast_convert_type(lax.bitwise_and(words, jnp.uint32(0xFFFF0000)), F32)
    return jnp.concatenate([lo, hi], axis=1).astype(BF16)


def _dispatch_kernel(len_ref, tpos_ref, dst_ref, padlo_ref, padlen_ref, nused_ref, post_ref, x_ref,
                     xs_hbm, sorted_ref, zeros_ref, sems, zsem):
    step = pl.program_id(0)
    n_steps = pl.num_programs(0)
    tt = x_ref.shape[0]
    n_assign = tt * EXPERT_TOPK
    rpt = PACKED_ROWS
    tile_rows = zeros_ref.shape[0] // rpt
    slot = lax.rem(step, 2)

    def start_runs(tile, buf):
        def per_expert(e, carry):
            k = tile * N_EXPERTS + e

            def piece(off, size):
                pltpu.make_async_copy(_token_rows(sorted_ref.at[buf], tpos_ref[k] + off, size, rpt),
                                      _token_rows(xs_hbm, dst_ref[k] + off, size, rpt), sems.at[buf]).start()

            _run_pieces(len_ref[k], piece)
            return carry

        lax.fori_loop(0, N_EXPERTS, per_expert, 0)

    def wait_runs(buf):
        whole = sorted_ref.at[buf]
        pltpu.make_async_copy(whole, _token_rows(xs_hbm, 0, n_assign, rpt), sems.at[buf]).wait()

    @pl.when(step == 0)
    def _():
        zeros_ref[...] = jnp.zeros_like(zeros_ref)

        def tail_tile(tl, carry):
            cp = pltpu.make_async_copy(zeros_ref, _token_rows(xs_hbm, tl * tile_rows, tile_rows, rpt), zsem)
            cp.start()
            cp.wait()
            return carry

        lax.fori_loop(nused_ref[0], xs_hbm.shape[0] // (tile_rows * rpt), tail_tile, 0)

        def per_expert(e, carry):
            def piece(off, size):
                cp = pltpu.make_async_copy(_token_rows(zeros_ref, 0, size, rpt),
                                           _token_rows(xs_hbm, padlo_ref[e] + off, size, rpt), zsem)
                cp.start()
                cp.wait()

            _run_pieces(padlen_ref[e], piece)
            return carry

        lax.fori_loop(0, N_EXPERTS, per_expert, 0)

    @pl.when(step >= 2)
    def _():
        wait_runs(slot)

    x = x_ref[...]
    pos_by_choice = [post_ref[kk:kk + 1, :] for kk in range(EXPERT_TOPK)]
    chunk = MOBA_BLOCK
    for rc in range(n_assign // chunk):
        r = (lax.broadcasted_iota(jnp.int32, (chunk, 1), 0) + rc * chunk).astype(F32)
        select = jnp.zeros((chunk, tt), F32)
        for pos in pos_by_choice:
            select = jnp.where(r == pos, 1.0, select)
        rows = jnp.dot(select.astype(BF16), x, preferred_element_type=F32)
        words = _pack_bf16_pairs(rows)
        for c in range(rpt):
            sorted_ref[slot, pl.ds(rc * chunk * rpt + c, chunk, stride=rpt), :] = words[:, c * LANES:(c + 1) * LANES]
    start_runs(step, slot)

    @pl.when(step == n_steps - 1)
    def _():
        @pl.when(step >= 1)
        def _():
            wait_runs(1 - slot)

        wait_runs(slot)


def _dispatch(run_len, run_tpos, run_dst, padlo, padlen, n_used, pos_t, xn2, n_rows):
    t, d = xn2.shape
    n_assign = ROUTE_TOKENS * EXPERT_TOPK
    grid_spec = pltpu.PrefetchScalarGridSpec(
        num_scalar_prefetch=6,
        grid=(t // ROUTE_TOKENS,),
        in_specs=[pl.BlockSpec((SUBLANES, ROUTE_TOKENS), lambda c, *_: (c, 0)),
                  pl.BlockSpec((ROUTE_TOKENS, d), lambda c, *_: (c, 0))],
        out_specs=pl.BlockSpec(memory_space=pl.ANY),
        scratch_shapes=[
            pltpu.VMEM((2, n_assign * PACKED_ROWS, LANES), jnp.uint32),
            pltpu.VMEM((EXPERT_ROWS * PACKED_ROWS, LANES), jnp.uint32),
            pltpu.SemaphoreType.DMA((2,)),
            pltpu.SemaphoreType.DMA(()),
        ],
    )
    assert d == 2 * PACKED_ROWS * LANES
    return pl.pallas_call(
        _dispatch_kernel,
        grid_spec=grid_spec,
        out_shape=jax.ShapeDtypeStruct((n_rows * PACKED_ROWS, LANES), jnp.uint32),
        compiler_params=pltpu.CompilerParams(dimension_semantics=("arbitrary",), vmem_limit_bytes=VMEM_LIMIT),
        name="dispatch",
    )(run_len, run_tpos, run_dst, padlo, padlen, n_used, pos_t, xn2)


def _expert_kernel(be_ref, nused_ref, x_ref, wgu_ref, bg_ref, bu_ref, wd_ref, bd_ref, y_ref, wgu_bf, wd_bf):
    t = pl.program_id(0)
    rows = y_ref.shape[0] // SUBLANES
    de = wd_ref.shape[0]
    grp = 2 * LANES
    active = t < nused_ref[0]
    fresh = jnp.logical_and(
        active, jnp.logical_or(t == 0, be_ref[t] != be_ref[jnp.maximum(t - 1, 0)]))

    @pl.when(fresh)
    def _():
        src = lax.broadcasted_iota(jnp.int32, (grp, grp), 0)
        dst = lax.broadcasted_iota(jnp.int32, (grp, grp), 1)
        wanted = jnp.where(dst < LANES, 2 * dst, 2 * (dst - LANES) + 1)
        perm = jnp.where(src == wanted, 1.0, 0.0).astype(BF16)
        for g in range(wgu_ref.shape[1] // grp):
            cols = slice(g * grp, (g + 1) * grp)
            wgu_bf[:, cols] = jnp.dot(wgu_ref[:, cols].astype(BF16), perm,
                                      preferred_element_type=F32).astype(BF16)
        wd_bf[...] = wd_ref[...].astype(BF16)

    @pl.when(active)
    def _():
        words = jnp.concatenate([x_ref[pl.ds(c, rows, stride=PACKED_ROWS), :] for c in range(PACKED_ROWS)], axis=1)
        xb = _unpack_bf16_pairs(words)
        gu = jnp.dot(xb, wgu_bf[...], preferred_element_type=F32)
        hid = []
        for g in range(de // LANES):
            cols = slice(g * LANES, (g + 1) * LANES)
            gate = gu[:, g * grp:g * grp + LANES] + bg_ref[:, cols]
            up = gu[:, g * grp + LANES:(g + 1) * grp] + bu_ref[:, cols]
            gate = jnp.minimum(gate, SWIGLU_LIMIT)
            up = jnp.clip(up, -SWIGLU_LIMIT, SWIGLU_LIMIT)
            hid.append((gate * _sigmoid(SWIGLU_ALPHA * gate) * (up + 1.0)).astype(BF16))
        y = jnp.dot(jnp.concatenate(hid, axis=1), wd_bf[...], preferred_element_type=F32) + bd_ref[...]
        _store_token_rows(y_ref, y)

    @pl.when(jnp.logical_not(active))
    def _():
        y_ref[...] = jnp.zeros_like(y_ref)


def _experts(block_expert, n_used, xs_rows, wgu, bg, bu, wd, bd):
    n_rows = xs_rows.shape[0] // PACKED_ROWS
    _, d, de2 = wgu.shape
    de = wd.shape[1]
    n_tiles = n_rows // EXPERT_ROWS
    tile_rows = EXPERT_ROWS * SUBLANES
    per_expert = lambda t, be, nu: (be[t], 0, 0)
    grid_spec = pltpu.PrefetchScalarGridSpec(
        num_scalar_prefetch=2,
        grid=(n_tiles,),
        in_specs=[
            pl.BlockSpec((EXPERT_ROWS * PACKED_ROWS, LANES),
                         lambda t, be, nu: (jnp.minimum(t, jnp.maximum(nu[0] - 1, 0)), 0)),
            pl.BlockSpec((None, d, de2), per_expert),
            pl.BlockSpec((None, 1, de), per_expert),
            pl.BlockSpec((None, 1, de), per_expert),
            pl.BlockSpec((None, de, d), per_expert),
            pl.BlockSpec((None, 1, d), per_expert),
        ],
        out_specs=pl.BlockSpec((tile_rows, LANES), lambda t, be, nu: (t, 0)),
        scratch_shapes=[pltpu.VMEM((d, de2), BF16), pltpu.VMEM((de, d), BF16)],
    )
    return pl.pallas_call(
        _expert_kernel,
        grid_spec=grid_spec,
        out_shape=jax.ShapeDtypeStruct((n_rows * SUBLANES, LANES), F32),
        compiler_params=pltpu.CompilerParams(
            dimension_semantics=("arbitrary",), vmem_limit_bytes=BIG_VMEM_LIMIT),
        name="experts",
    )(block_expert, n_used, xs_rows, wgu, bg, bu, wd, bd)


def _combine_kernel(len_ref, tpos_ref, src_ref, pos_ref, w_ref, h1_ref, fg_ref, ys_hbm,
                    o_ref, buf_ref, h1rows_ref, outrows_ref, sems):
    step = pl.program_id(0)
    n_steps = pl.num_programs(0)
    tt = h1_ref.shape[0]
    slot = lax.rem(step, 2)

    def start_runs(tile, buf):
        def per_expert(e, carry):
            k = tile * N_EXPERTS + e

            def piece(off, size):
                pltpu.make_async_copy(_token_rows(ys_hbm, src_ref[k] + off, size),
                                      _token_rows(buf_ref.at[buf], tpos_ref[k] + off, size), sems.at[buf]).start()

            _run_pieces(len_ref[k], piece)
            return carry

        lax.fori_loop(0, N_EXPERTS, per_expert, 0)

    def wait_runs(buf):
        whole = buf_ref.at[buf]
        pltpu.make_async_copy(_token_rows(ys_hbm, 0, tt * EXPERT_TOPK), whole, sems.at[buf]).wait()

    @pl.when(step == 0)
    def _():
        start_runs(0, 0)

    @pl.when(step + 1 < n_steps)
    def _():
        start_runs(step + 1, 1 - slot)

    wait_runs(slot)

    _store_token_rows(h1rows_ref, h1_ref[...])

    def one_token(tok, carry):
        acc = h1rows_ref[pl.ds(pl.multiple_of(tok * SUBLANES, SUBLANES), SUBLANES), :]
        for kk in range(EXPERT_TOPK):
            a = tok * EXPERT_TOPK + kk
            pos = pos_ref[a]
            acc = acc + w_ref[a] * buf_ref[slot, pl.ds(pl.multiple_of(pos * SUBLANES, SUBLANES), SUBLANES), :]
        outrows_ref[pl.ds(pl.multiple_of(tok * SUBLANES, SUBLANES), SUBLANES), :] = acc
        return carry

    lax.fori_loop(0, tt, one_token, 0, unroll=4)
    h = _load_token_rows(outrows_ref, tt)
    o_ref[...] = h * lax.rsqrt(jnp.mean(h * h, axis=-1, keepdims=True) + NORM_EPS) * fg_ref[...]


def _combine(run_len, run_tpos, run_src, pos, w_flat, h1, fg, ys_rows):
    t, d = h1.shape
    tt = ROUTE_TOKENS
    n_assign = tt * EXPERT_TOPK
    smem_blk = pl.BlockSpec((n_assign,), lambda c, *_: (c,), memory_space=pltpu.SMEM)
    grid_spec = pltpu.PrefetchScalarGridSpec(
        num_scalar_prefetch=3,
        grid=(t // tt,),
        in_specs=[
            smem_blk, smem_blk,
            pl.BlockSpec((tt, d), lambda c, *_: (c, 0)),
            pl.BlockSpec((1, d), lambda c, *_: (0, 0)),
            pl.BlockSpec(memory_space=pl.ANY),
        ],
        out_specs=pl.BlockSpec((tt, d), lambda c, *_: (c, 0)),
        scratch_shapes=[
            pltpu.VMEM((2, n_assign * SUBLANES, LANES), F32),
            pltpu.VMEM((tt * SUBLANES, LANES), F32),
            pltpu.VMEM((tt * SUBLANES, LANES), F32),
            pltpu.SemaphoreType.DMA((2,)),
        ],
    )
    return pl.pallas_call(
        _combine_kernel,
        grid_spec=grid_spec,
        out_shape=jax.ShapeDtypeStruct((t, d), F32),
        compiler_params=pltpu.CompilerParams(dimension_semantics=("arbitrary",), vmem_limit_bytes=VMEM_LIMIT),
        name="combine",
    )(run_len, run_tpos, run_src, pos, w_flat, h1, fg, ys_rows)


def _layer(h, mix_norm_g, w_in, w_attn_out, sgu_ln_g, sgu_ln_b, w_spatial, b_spatial, w_sgu_out,
           w_mix_out, ffn_norm_g, w_router, b_router, w_gate_up, b_gate_up, w_down, b_down, out_g):
    b, s, d = h.shape
    t = b * s
    sgu_width = sgu_ln_g.shape[0]
    x2 = h.reshape(t, d)

    q, qt, kt, v, kmean, u, vvn, ga, gs = _in_proj(
        x2, mix_norm_g.reshape(1, d), w_in.astype(BF16), sgu_ln_g.reshape(1, sgu_width),
        sgu_ln_b.reshape(1, sgu_width), sgu_width)

    nb = s // MOBA_BLOCK
    attn = _moba(q.reshape(b, s, ATTN_WIDTH), qt, kt, v.reshape(b, s, ATTN_WIDTH),
                 kmean.reshape(b, nb, ATTN_WIDTH), _moba_key_table(s)).reshape(t, ATTN_WIDTH)

    gdim = sgu_width // SGU_GROUPS
    bsp_full = jnp.repeat(b_spatial.T, gdim, axis=1)
    wr_pad = jnp.zeros((d, LANES), F32).at[:, :N_EXPERTS].set(w_router)
    br_pad = jnp.zeros((1, LANES), F32).at[0, :N_EXPERTS].set(b_router)
    h1, xn2, ri, rw, tile_cnt, pos_t = _mix(
        attn, u, vvn, ga, gs, x2, w_attn_out.astype(BF16), w_sgu_out.astype(BF16), w_mix_out.astype(BF16),
        w_spatial, bsp_full, ffn_norm_g.reshape(1, d), wr_pad, br_pad)

    n_assign = t * EXPERT_TOPK
    n_tiles = -(-(n_assign + N_EXPERTS * (EXPERT_ROWS - 1)) // EXPERT_ROWS)
    n_rows = n_tiles * EXPERT_ROWS
    run_len = tile_cnt[::SUBLANES, :N_EXPERTS].astype(jnp.int32)
    counts = jnp.sum(run_len, axis=0)
    padded = (counts + EXPERT_ROWS - 1) // EXPERT_ROWS * EXPERT_ROWS
    pad_end = jnp.cumsum(padded)
    starts = pad_end - padded
    run_tpos = jnp.cumsum(run_len, axis=1) - run_len
    run_row = starts[None, :] + jnp.cumsum(run_len, axis=0) - run_len
    tile_start = jnp.arange(n_tiles, dtype=jnp.int32) * EXPERT_ROWS
    block_expert = jnp.minimum(
        jnp.sum((pad_end[None, :] <= tile_start[:, None]).astype(jnp.int32), axis=1), N_EXPERTS - 1)
    n_used = (pad_end[-1:] // EXPERT_ROWS).astype(jnp.int32)
    pos = ri[:, EXPERT_TOPK:2 * EXPERT_TOPK].reshape(n_assign)
    tables = (run_len.reshape(-1), run_tpos.reshape(-1), run_row.reshape(-1))

    xs_rows = _dispatch(*tables, starts + counts, padded - counts, n_used, pos_t, xn2, n_rows)

    de = w_down.shape[1]
    bg = b_gate_up[:, 0::2].reshape(N_EXPERTS, 1, de)
    bu = b_gate_up[:, 1::2].reshape(N_EXPERTS, 1, de)
    ys_rows = _experts(block_expert, n_used, xs_rows, w_gate_up, bg, bu, w_down, b_down.reshape(N_EXPERTS, 1, d))

    out = _combine(*tables, pos, rw[:, :EXPERT_TOPK].reshape(n_assign), h1, out_g.reshape(1, d), ys_rows)
    return out.reshape(b, s, d)


def kernel(x, mix_norm_g, w_in, w_attn_out, sgu_ln_g, sgu_ln_b, w_spatial, b_spatial, w_sgu_out, w_mix_out,
           ffn_norm_g, w_router, b_router, w_gate_up, b_gate_up, w_down, b_down, final_norm_g):
    depth = w_in.shape[0]
    assert depth == 1, "the final RMSNorm is fused into the single layer's combine step"
    return _layer(x, mix_norm_g[0], w_in[0], w_attn_out[0], sgu_ln_g[0], sgu_ln_b[0], w_spatial[0],
                  b_spatial[0], w_sgu_out[0], w_mix_out[0], ffn_norm_g[0], w_router[0], b_router[0],
                  w_gate_up[0], b_gate_up[0], w_down[0], b_down[0], final_norm_g)
```

```python
import jax
import jax.numpy as jnp
import numpy as np
from jax import lax
from jax.experimental import pallas as pl
from jax.experimental.pallas import tpu as pltpu

F32 = jnp.float32
BF16 = jnp.bfloat16
NEG_INF = float("-inf")
MASK_VALUE = -1e30


def _bf16_pieces(x, n=3):
    pieces = []
    for _ in range(n):
        p = float(np.asarray(x, np.float32).astype(BF16).astype(np.float32))
        pieces.append(p)
        x = x - p
    return pieces


LOG2E = 1.4426950408889634
LOG2E_PIECES = _bf16_pieces(LOG2E)

N_HEADS = 8
HEAD_DIM = 64
ATTN_WIDTH = N_HEADS * HEAD_DIM
MOBA_BLOCK = 256
MOBA_TOPK = 3
SGU_CHUNK = 128
SGU_GROUPS = 8
N_EXPERTS = 32
EXPERT_TOPK = 4
SWIGLU_LIMIT = 7.0
SWIGLU_ALPHA = 1.702
NORM_EPS = 1e-5

LANES = 128
SUBLANES = 8
PACKED_ROWS = 4
HEADS_PER_LANE_TILE = LANES // HEAD_DIM
MOBA_GROUP = 1
MOBA_BIAS_LANE0 = 64
MOBA_DUMMY_LANE = LANES - 1
PROJ_ROWS = 256
EXPERT_ROWS = 768
ROUTE_TOKENS = 512
VMEM_LIMIT = 48 * 1024 * 1024
BIG_VMEM_LIMIT = 56 * 1024 * 1024


def _sigmoid(x):
    return 1.0 / (1.0 + jnp.exp(-x))


def _store_token_rows(ref, value, base=0):
    n = value.shape[0]
    for c in range(value.shape[1] // LANES):
        ref[pl.ds(base + c, n, stride=SUBLANES), :] = value[:, c * LANES:(c + 1) * LANES]


def _load_token_rows(ref, n, base=0):
    return jnp.concatenate([ref[pl.ds(base + c, n, stride=SUBLANES), :] for c in range(SUBLANES)], axis=1)


def _gelu_exact(x):
    return 0.5 * x * (1.0 + lax.erf(x * (0.5 ** 0.5)))


def _in_proj_kernel(x_ref, g_ref, w_ref, lng_ref, lnb_ref,
                    q_ref, qt_ref, kt_ref, v_ref, kmean_ref, u_ref, vvn_ref, ga_ref, gs_ref):
    x = x_ref[...]
    xn = x * lax.rsqrt(jnp.mean(x * x, axis=-1, keepdims=True) + NORM_EPS) * g_ref[...]
    xb = xn.astype(BF16)

    def proj(lo, hi):
        return jnp.dot(xb, w_ref[:, lo:hi], preferred_element_type=F32)

    a = ATTN_WIDTH
    sw = u_ref.shape[1]
    d = ga_ref.shape[1]
    qf = proj(0, a)
    q_ref[...] = qf
    kf = proj(a, 2 * a)
    for j in range(kf.shape[0] // MOBA_BLOCK):
        qt_ref[j] = qf[j * MOBA_BLOCK:(j + 1) * MOBA_BLOCK].T
        kblk = kf[j * MOBA_BLOCK:(j + 1) * MOBA_BLOCK]
        kt_ref[j] = kblk.T.astype(BF16)
        kmean_ref[j] = jnp.mean(kblk, axis=0, keepdims=True)
    v_ref[...] = proj(2 * a, 3 * a).astype(BF16)
    z0 = 3 * a
    u_ref[...] = _gelu_exact(proj(z0, z0 + sw))
    zv = _gelu_exact(proj(z0 + sw, z0 + 2 * sw))
    mu = jnp.mean(zv, axis=-1, keepdims=True)
    zc = zv - mu
    var = jnp.mean(zc * zc, axis=-1, keepdims=True)
    vvn_ref[...] = zc * lax.rsqrt(var + NORM_EPS) * lng_ref[...] + lnb_ref[...]
    g0 = z0 + 2 * sw
    ga_ref[...] = _sigmoid(proj(g0, g0 + d))
    gs_ref[...] = _sigmoid(proj(g0 + d, g0 + 2 * d))


def _in_proj(x2, g, w_bf, lng, lnb, sgu_width):
    t, d = x2.shape
    tm = PROJ_ROWS
    n_in = w_bf.shape[1]
    row = lambda w: pl.BlockSpec((tm, w), lambda i: (i, 0))
    const = lambda shape: pl.BlockSpec(shape, lambda i: (0,) * len(shape))
    out_shape = (
        jax.ShapeDtypeStruct((t, ATTN_WIDTH), F32),
        jax.ShapeDtypeStruct((t // MOBA_BLOCK, ATTN_WIDTH, MOBA_BLOCK), F32),
        jax.ShapeDtypeStruct((t // MOBA_BLOCK, ATTN_WIDTH, MOBA_BLOCK), BF16),
        jax.ShapeDtypeStruct((t, ATTN_WIDTH), BF16),
        jax.ShapeDtypeStruct((t // MOBA_BLOCK, 1, ATTN_WIDTH), F32),
        jax.ShapeDtypeStruct((t, sgu_width), F32),
        jax.ShapeDtypeStruct((t, sgu_width), F32),
        jax.ShapeDtypeStruct((t, d), F32),
        jax.ShapeDtypeStruct((t, d), F32),
    )
    out_specs = (
        row(ATTN_WIDTH),
        pl.BlockSpec((tm // MOBA_BLOCK, ATTN_WIDTH, MOBA_BLOCK), lambda i: (i, 0, 0)),
        pl.BlockSpec((tm // MOBA_BLOCK, ATTN_WIDTH, MOBA_BLOCK), lambda i: (i, 0, 0)),
        row(ATTN_WIDTH),
        pl.BlockSpec((tm // MOBA_BLOCK, 1, ATTN_WIDTH), lambda i: (i, 0, 0)),
        row(sgu_width), row(sgu_width), row(d), row(d),
    )
    return pl.pallas_call(
        _in_proj_kernel,
        grid=(t // tm,),
        in_specs=[row(d), const((1, d)), const((d, n_in)), const((1, sgu_width)), const((1, sgu_width))],
        out_specs=out_specs,
        out_shape=out_shape,
        compiler_params=pltpu.CompilerParams(dimension_semantics=("arbitrary",), vmem_limit_bytes=VMEM_LIMIT),
        name="in_proj",
    )(x2, g, w_bf, lng, lnb)


def _alibi_slopes():
    slopes = 2.0 ** (-8.0 * np.arange(1, N_HEADS + 1, dtype=np.float64) / N_HEADS)
    assert all(np.log2(s) == np.round(np.log2(s)) for s in slopes), "ALiBi slopes must be powers of two"
    return slopes


def _moba_bias_lane(head_in_tile, part, piece):
    return MOBA_BIAS_LANE0 + (head_in_tile * 2 + part) * len(LOG2E_PIECES) + piece


def _moba_key_table(s):
    nb = s // MOBA_BLOCK
    slopes = _alibi_slopes()
    n_pairs = N_HEADS // HEADS_PER_LANE_TILE
    table = np.zeros((nb + 1, n_pairs, LANES, MOBA_BLOCK), np.float32)
    offs = np.arange(MOBA_BLOCK, dtype=np.float32)
    assert HEADS_PER_LANE_TILE * nb <= MOBA_BIAS_LANE0, "block one-hot rows must not overlap the ALiBi rows"
    for j in range(nb):
        for hh in range(HEADS_PER_LANE_TILE):
            table[j, :, hh * nb + j, :] = 1.0
        for p in range(n_pairs):
            for hh in range(HEADS_PER_LANE_TILE):
                slope = slopes[p * HEADS_PER_LANE_TILE + hh]
                for piece in range(len(LOG2E_PIECES)):
                    table[j, p, _moba_bias_lane(hh, 0, piece), :] = slope * MOBA_BLOCK * j
                    table[j, p, _moba_bias_lane(hh, 1, piece), :] = slope * offs
    table[nb, :, MOBA_DUMMY_LANE, :] = 1.0
    as_bf16 = table.astype(BF16)
    assert np.array_equal(as_bf16.astype(np.float32), table), "bias table must be exact in bf16"
    return jnp.asarray(as_bf16.reshape(nb + 1, n_pairs * LANES, MOBA_BLOCK))


def _moba_kernel(q_ref, qt_ref, qn_ref, qtn_ref, kt_ref, v_ref, km_ref, ct_ref, causal_ref, o_ref, qaug_ref,
                 s_ref, mpart_ref, mrow_ref, acc_ref):
    i = pl.program_id(2)
    nb = kt_ref.shape[0]
    blk = MOBA_BLOCK
    lane = lax.broadcasted_iota(jnp.int32, (1, LANES), 1)
    n_groups = lax.div(i + MOBA_GROUP, MOBA_GROUP)
    heads = range(HEADS_PER_LANE_TILE)
    hmasks = [(lane >= HEAD_DIM * hh) & (lane < HEAD_DIM * (hh + 1)) for hh in heads]
    slot = lax.rem(i, 2)

    def build_query_operand(q, qt, tile, into):
        bid = lax.broadcasted_iota(jnp.int32, (nb, blk), 0)
        bid_f = bid.astype(F32)
        masks_t = []
        for hh in heads:
            km_h = jnp.where(hmasks[hh], km_ref[...], 0.0)
            gate = jnp.dot(km_h, qt, precision=lax.Precision.HIGHEST, preferred_element_type=F32)
            gate = jnp.where(bid < tile, gate, NEG_INF)
            blockmask = jnp.where(bid == tile, 0.0, MASK_VALUE)
            for kk in range(MOBA_TOPK):
                gmax = jnp.max(gate, axis=0, keepdims=True)
                first = jnp.min(jnp.where(gate == gmax, bid_f, float(nb)), axis=0, keepdims=True)
                valid = (tile > kk).astype(F32)
                first = first * valid + (valid - 1.0)
                hit = bid_f == first
                blockmask = jnp.where(hit, 0.0, blockmask)
                gate = jnp.where(hit, NEG_INF, gate)
            masks_t.append(blockmask)
        masks_t.append(jnp.zeros((LANES - len(masks_t) * nb, blk), F32))
        masks = jnp.concatenate(masks_t, axis=0).T

        for hh in heads:
            feats = jnp.where(lane == MOBA_DUMMY_LANE, MASK_VALUE, 0.0)
            for part in range(2):
                for piece, value in enumerate(LOG2E_PIECES):
                    feats = jnp.where(lane == _moba_bias_lane(hh, part, piece), value, feats)
            extra = jnp.where((lane >= hh * nb) & (lane < (hh + 1) * nb), masks, feats)
            qh = jnp.where(hmasks[hh], q, 0.0)
            qaug_ref[into, hh * blk:(hh + 1) * blk, 0:LANES] = (qh * (HEAD_DIM ** -0.5 * LOG2E)).astype(BF16)
            qaug_ref[into, hh * blk:(hh + 1) * blk, LANES:2 * LANES] = extra.astype(BF16)

    @pl.when(i == 0)
    def _():
        build_query_operand(q_ref[...], qt_ref[...], i, slot)

    def scores(jk, jc):
        rhs = jnp.concatenate([kt_ref[jk], ct_ref[jc]], axis=0)
        return jnp.dot(qaug_ref[slot], rhs, preferred_element_type=F32)

    def lane_halves_max(s):
        return jnp.maximum(s[:, 0:LANES], s[:, LANES:2 * LANES])

    n_rows = len(heads) * blk
    variants = [(groups, min(groups * MOBA_GROUP, nb)) for groups in range(1, pl.cdiv(nb, MOBA_GROUP) + 1)]

    for groups, n_blocks in variants:
        @pl.when(n_groups == groups)
        def _(n_blocks=n_blocks):
            part = None
            for j in range(n_blocks):
                s = scores(j, jnp.where(j <= i, j, nb)) + causal_ref[(i == j).astype(jnp.int32)]
                s_ref[j] = s
                part = lane_halves_max(s) if part is None else jnp.maximum(part, lane_halves_max(s))
            mpart_ref[...] = part

    ones = jnp.ones((blk, LANES), BF16)
    rowmax = jnp.max(mpart_ref[...], axis=1, keepdims=True)
    mrow_ref[...] = jnp.broadcast_to(rowmax, (n_rows, 2 * LANES))

    def weighted(j):
        p = jnp.exp2(s_ref[j] - mrow_ref[...]).astype(BF16)
        vaug = jnp.concatenate([v_ref[j * blk:(j + 1) * blk, :], ones], axis=1)
        return jnp.dot(p, vaug, preferred_element_type=F32)

    next_tile = jnp.minimum(i + 1, nb - 1)
    for groups, n_blocks in variants:
        @pl.when(n_groups == groups)
        def _(n_blocks=n_blocks):
            build_query_operand(qn_ref[...], qtn_ref[...], next_tile, 1 - slot)
            tot = weighted(0)
            for j in range(1, n_blocks):
                tot = tot + weighted(j)
            acc_ref[...] = tot

    out = jnp.zeros((blk, LANES), F32)
    for hh in heads:
        acc = acc_ref[hh * blk:(hh + 1) * blk, :]
        out = jnp.where(hmasks[hh], acc[:, 0:LANES] / acc[:, LANES:2 * LANES], out)
    o_ref[...] = out


def _moba(q, qt, kt, v, kmean, key_table):
    b, s, a = q.shape
    nb = s // MOBA_BLOCK
    n_pairs = a // LANES
    rows = HEADS_PER_LANE_TILE * MOBA_BLOCK
    row_in_block = np.arange(rows)[:, None] % MOBA_BLOCK
    causal = np.where(row_in_block >= np.arange(MOBA_BLOCK)[None, :], 0.0, MASK_VALUE).astype(np.float32)
    causal_tiles = jnp.asarray(np.stack([np.zeros_like(causal), causal]))
    nxt = lambda i: jnp.minimum(i + 1, nb - 1)
    return pl.pallas_call(
        _moba_kernel,
        grid=(b, n_pairs, nb),
        in_specs=[
            pl.BlockSpec((None, MOBA_BLOCK, LANES), lambda bi, p, i: (bi, i, p)),
            pl.BlockSpec((None, LANES, MOBA_BLOCK), lambda bi, p, i: (bi * nb + i, p, 0)),
            pl.BlockSpec((None, MOBA_BLOCK, LANES), lambda bi, p, i: (bi, nxt(i), p)),
            pl.BlockSpec((None, LANES, MOBA_BLOCK), lambda bi, p, i: (bi * nb + nxt(i), p, 0)),
            pl.BlockSpec((nb, LANES, MOBA_BLOCK), lambda bi, p, i: (bi, p, 0)),
            pl.BlockSpec((None, s, LANES), lambda bi, p, i: (bi, 0, p)),
            pl.BlockSpec((None, nb, LANES), lambda bi, p, i: (bi, 0, p)),
            pl.BlockSpec((nb + 1, LANES, MOBA_BLOCK), lambda bi, p, i: (0, p, 0)),
            pl.BlockSpec((2, rows, MOBA_BLOCK), lambda bi, p, i: (0, 0, 0)),
        ],
        out_specs=pl.BlockSpec((None, MOBA_BLOCK, LANES), lambda bi, p, i: (bi, i, p)),
        out_shape=jax.ShapeDtypeStruct((b, s, a), F32),
        scratch_shapes=[
            pltpu.VMEM((2, rows, 2 * LANES), BF16),
            pltpu.VMEM((nb + 1, rows, MOBA_BLOCK), F32),
            pltpu.VMEM((rows, LANES), F32),
            pltpu.VMEM((rows, 2 * LANES), F32),
            pltpu.VMEM((rows, 2 * LANES), F32),
        ],
        compiler_params=pltpu.CompilerParams(
            dimension_semantics=("arbitrary", "arbitrary", "arbitrary"), vmem_limit_bytes=VMEM_LIMIT),
        name="moba",
    )(q, qt, q, qt, kt, v, kmean, key_table, causal_tiles)


def _mix_kernel(attn_ref, u_ref, vvn_ref, ga_ref, gs_ref, x_ref,
                wao_ref, wso_ref, wmo_ref, wsp_ref, bsp_ref, fg_ref, wr_ref, br_ref,
                h1_ref, xn2_ref, ri_ref, rw_ref, tcnt_ref, post_ref, wcausal_ref):
    tm = x_ref.shape[0]
    ch = SGU_CHUNK

    @pl.when(pl.program_id(0) == 0)
    def _():
        tril = lax.broadcasted_iota(jnp.int32, (ch, ch), 0) >= lax.broadcasted_iota(jnp.int32, (ch, ch), 1)
        for g in range(SGU_GROUPS):
            wcausal_ref[g] = jnp.where(tril, wsp_ref[g], 0.0).astype(BF16)

    y_attn = jnp.dot(attn_ref[...].astype(BF16), wao_ref[...], preferred_element_type=F32)

    lane = lax.broadcasted_iota(jnp.int32, (1, LANES), 1)
    gdim = vvn_ref.shape[1] // SGU_GROUPS
    groups_per_tile = LANES // gdim
    w_causal = [wcausal_ref[g] for g in range(SGU_GROUPS)]
    rows = []
    for c in range(tm // ch):
        cols = []
        for ct in range(vvn_ref.shape[1] // LANES):
            vp = vvn_ref[c * ch:(c + 1) * ch, ct * LANES:(ct + 1) * LANES]
            acc = jnp.zeros((ch, LANES), F32)
            for gg in range(groups_per_tile):
                gmask = (lane >= gdim * gg) & (lane < gdim * (gg + 1))
                vm = jnp.where(gmask, vp, 0.0).astype(BF16)
                acc = acc + jnp.dot(w_causal[ct * groups_per_tile + gg], vm, preferred_element_type=F32)
            cols.append(acc)
        rows.append(jnp.concatenate(cols, axis=1) + bsp_ref[...])
    mixed = jnp.concatenate(rows, axis=0)
    sgu = u_ref[...] * mixed
    y_sgu = jnp.dot(sgu.astype(BF16), wso_ref[...], preferred_element_type=F32)

    merged = ga_ref[...] * y_attn + gs_ref[...] * y_sgu
    h1 = x_ref[...] + jnp.dot(merged.astype(BF16), wmo_ref[...], preferred_element_type=F32)
    h1_ref[...] = h1
    xn2 = h1 * lax.rsqrt(jnp.mean(h1 * h1, axis=-1, keepdims=True) + NORM_EPS) * fg_ref[...]
    xn2_ref[...] = xn2.astype(BF16)

    def split(a):
        hi = a.astype(BF16)
        return hi, (a - hi.astype(F32)).astype(BF16)

    x_hi, x_lo = split(xn2)
    w_hi, w_lo = split(wr_ref[...])
    logits = (jnp.dot(x_hi, w_hi, preferred_element_type=F32) + jnp.dot(x_lo, w_hi, preferred_element_type=F32)
              + jnp.dot(x_hi, w_lo, preferred_element_type=F32)) + br_ref[...]
    work = jnp.where(lane < N_EXPERTS, logits, NEG_INF)
    lane_f = lane.astype(F32)
    vals, idxs = [], []
    for _ in range(EXPERT_TOPK):
        vmax = jnp.max(work, axis=1, keepdims=True)
        first = jnp.min(jnp.where(work == vmax, lane_f, float(LANES)), axis=1, keepdims=True)
        vals.append(vmax)
        idxs.append(first)
        work = jnp.where(lane_f == first, NEG_INF, work)
    exps = [jnp.exp(v - vals[0]) for v in vals]
    denom = exps[0]
    for e in exps[1:]:
        denom = denom + e
    chosen = jnp.zeros((tm, LANES), F32)
    for first in idxs:
        chosen = jnp.where(lane_f == first, 1.0, chosen)

    strict_lower = (lax.broadcasted_iota(jnp.int32, (tm, tm), 0)
                    > lax.broadcasted_iota(jnp.int32, (tm, tm), 1))
    before = jnp.dot(jnp.where(strict_lower, 1.0, 0.0).astype(BF16), chosen.astype(BF16),
                     preferred_element_type=F32)
    counts = jnp.broadcast_to(jnp.sum(chosen, axis=0, keepdims=True), tcnt_ref.shape)
    lower_expert = (lax.broadcasted_iota(jnp.int32, (LANES, LANES), 0)
                    < lax.broadcasted_iota(jnp.int32, (LANES, LANES), 1))
    expert_start = jnp.dot(counts, jnp.where(lower_expert, 1.0, 0.0), precision=lax.Precision.HIGHEST,
                           preferred_element_type=F32)[0:1, :]
    sorted_pos = before + expert_start
    ri = jnp.zeros((tm, LANES), jnp.int32)
    rw = jnp.zeros((tm, LANES), F32)
    pos_lanes = jnp.zeros((tm, LANES), F32)
    for kk in range(EXPERT_TOPK):
        pos = jnp.sum(jnp.where(lane_f == idxs[kk], sorted_pos, 0.0), axis=1, keepdims=True)
        ri = jnp.where(lane == kk, idxs[kk].astype(jnp.int32), ri)
        ri = jnp.where(lane == EXPERT_TOPK + kk, pos.astype(jnp.int32), ri)
        rw = jnp.where(lane == kk, exps[kk] / denom, rw)
        pos_lanes = jnp.where(lane == kk, pos, pos_lanes)
    ri_ref[...] = ri
    rw_ref[...] = rw
    tcnt_ref[...] = counts
    post_ref[...] = pos_lanes.T[0:SUBLANES, :]


def _mix(attn, u, vvn, ga, gs, x2, wao, wso, wmo, wsp, bsp_full, fg, wr_pad, br_pad):
    t, d = x2.shape
    tm = ROUTE_TOKENS
    row = lambda w: pl.BlockSpec((tm, w), lambda i: (i, 0))
    const = lambda shape: pl.BlockSpec(shape, lambda i: (0,) * len(shape))
    ins = (attn, u, vvn, ga, gs, x2, wao, wso, wmo, wsp, bsp_full, fg, wr_pad, br_pad)
    in_specs = [row(a.shape[1]) for a in ins[:6]] + [const(a.shape) for a in ins[6:]]
    out_shape = (
        jax.ShapeDtypeStruct((t, d), F32),
        jax.ShapeDtypeStruct((t, d), BF16),
        jax.ShapeDtypeStruct((t, LANES), jnp.int32),
        jax.ShapeDtypeStruct((t, LANES), F32),
        jax.ShapeDtypeStruct((t // tm * SUBLANES, LANES), F32),
        jax.ShapeDtypeStruct((t // tm * SUBLANES, tm), F32),
    )
    assert d == SUBLANES * LANES, "a token row is stored as one (8, 128) tile"
    per_tile = lambda w: pl.BlockSpec((SUBLANES, w), lambda i: (i, 0))
    out_specs = (row(d), row(d), row(LANES), row(LANES), per_tile(LANES), per_tile(tm))
    return pl.pallas_call(
        _mix_kernel,
        grid=(t // tm,),
        in_specs=in_specs,
        out_specs=out_specs,
        out_shape=out_shape,
        scratch_shapes=[pltpu.VMEM(wsp.shape, BF16)],
        compiler_params=pltpu.CompilerParams(dimension_semantics=("arbitrary",), vmem_limit_bytes=BIG_VMEM_LIMIT),
        name="mix_route",
    )(*ins)


def _run_pieces(n, body):
    off = jnp.int32(0)
    for bit in reversed(range(ROUTE_TOKENS.bit_length())):
        size = 1 << bit
        take = lax.bitwise_and(n, size) != 0

        @pl.when(take)
        def _(off=off, size=size):
            body(off, size)

        off = off + jnp.where(take, size, 0)


def _token_rows(ref, r, n, rows_per_token=SUBLANES):
    return ref.at[pl.ds(pl.multiple_of(r * rows_per_token, rows_per_token), n * rows_per_token)]


def _pack_bf16_pairs(value):
    half = value.shape[1] // 2
    lo = lax.shift_right_logical(lax.bitcast_convert_type(value[:, :half], jnp.uint32), jnp.uint32(16))
    hi = lax.bitwise_and(lax.bitcast_convert_type(value[:, half:], jnp.uint32), jnp.uint32(0xFFFF0000))
    return lax.bitwise_or(lo, hi)


def _unpack_bf16_pairs(words):
    lo = lax.bitcast_convert_type(lax.shift_left(words, jnp.uint32(16)), F32)
    hi = lax.bitcast_convert_type(lax.bitwise_and(words, jnp.uint32(0xFFFF0000)), F32)
    return jnp.concatenate([lo, hi], axis=1).astype(BF16)


def _dispatch_kernel(len_ref, tpos_ref, dst_ref, padlo_ref, padlen_ref, nused_ref, post_ref, x_ref,
                     xs_hbm, sorted_ref, zeros_ref, sems, zsem):
    step = pl.program_id(0)
    n_steps = pl.num_programs(0)
    tt = x_ref.shape[0]
    n_assign = tt * EXPERT_TOPK
    rpt = PACKED_ROWS
    tile_rows = zeros_ref.shape[0] // rpt
    slot = lax.rem(step, 2)

    def start_runs(tile, buf):
        def per_expert(e, carry):
            k = tile * N_EXPERTS + e

            def piece(off, size):
                pltpu.make_async_copy(_token_rows(sorted_ref.at[buf], tpos_ref[k] + off, size, rpt),
                                      _token_rows(xs_hbm, dst_ref[k] + off, size, rpt), sems.at[buf]).start()

            _run_pieces(len_ref[k], piece)
            return carry

        lax.fori_loop(0, N_EXPERTS, per_expert, 0)

    def wait_runs(buf):
        whole = sorted_ref.at[buf]
        pltpu.make_async_copy(whole, _token_rows(xs_hbm, 0, n_assign, rpt), sems.at[buf]).wait()

    @pl.when(step == 0)
    def _():
        zeros_ref[...] = jnp.zeros_like(zeros_ref)

        def tail_tile(tl, carry):
            cp = pltpu.make_async_copy(zeros_ref, _token_rows(xs_hbm, tl * tile_rows, tile_rows, rpt), zsem)
            cp.start()
            cp.wait()
            return carry

        lax.fori_loop(nused_ref[0], xs_hbm.shape[0] // (tile_rows * rpt), tail_tile, 0)

        def per_expert(e, carry):
            def piece(off, size):
                cp = pltpu.make_async_copy(_token_rows(zeros_ref, 0, size, rpt),
                                           _token_rows(xs_hbm, padlo_ref[e] + off, size, rpt), zsem)
                cp.start()
                cp.wait()

            _run_pieces(padlen_ref[e], piece)
            return carry

        lax.fori_loop(0, N_EXPERTS, per_expert, 0)

    @pl.when(step >= 2)
    def _():
        wait_runs(slot)

    x = x_ref[...]
    pos_by_choice = [post_ref[kk:kk + 1, :] for kk in range(EXPERT_TOPK)]
    chunk = MOBA_BLOCK
    for rc in range(n_assign // chunk):
        r = (lax.broadcasted_iota(jnp.int32, (chunk, 1), 0) + rc * chunk).astype(F32)
        select = jnp.zeros((chunk, tt), F32)
        for pos in pos_by_choice:
            select = jnp.where(r == pos, 1.0, select)
        rows = jnp.dot(select.astype(BF16), x, preferred_element_type=F32)
        words = _pack_bf16_pairs(rows)
        for c in range(rpt):
            sorted_ref[slot, pl.ds(rc * chunk * rpt + c, chunk, stride=rpt), :] = words[:, c * LANES:(c + 1) * LANES]
    start_runs(step, slot)

    @pl.when(step == n_steps - 1)
    def _():
        @pl.when(step >= 1)
        def _():
            wait_runs(1 - slot)

        wait_runs(slot)


def _dispatch(run_len, run_tpos, run_dst, padlo, padlen, n_used, pos_t, xn2, n_rows):
    t, d = xn2.shape
    n_assign = ROUTE_TOKENS * EXPERT_TOPK
    grid_spec = pltpu.PrefetchScalarGridSpec(
        num_scalar_prefetch=6,
        grid=(t // ROUTE_TOKENS,),
        in_specs=[pl.BlockSpec((SUBLANES, ROUTE_TOKENS), lambda c, *_: (c, 0)),
                  pl.BlockSpec((ROUTE_TOKENS, d), lambda c, *_: (c, 0))],
        out_specs=pl.BlockSpec(memory_space=pl.ANY),
        scratch_shapes=[
            pltpu.VMEM((2, n_assign * PACKED_ROWS, LANES), jnp.uint32),
            pltpu.VMEM((EXPERT_ROWS * PACKED_ROWS, LANES), jnp.uint32),
            pltpu.SemaphoreType.DMA((2,)),
            pltpu.SemaphoreType.DMA(()),
        ],
    )
    assert d == 2 * PACKED_ROWS * LANES
    return pl.pallas_call(
        _dispatch_kernel,
        grid_spec=grid_spec,
        out_shape=jax.ShapeDtypeStruct((n_rows * PACKED_ROWS, LANES), jnp.uint32),
        compiler_params=pltpu.CompilerParams(dimension_semantics=("arbitrary",), vmem_limit_bytes=VMEM_LIMIT),
        name="dispatch",
    )(run_len, run_tpos, run_dst, padlo, padlen, n_used, pos_t, xn2)


def _expert_kernel(be_ref, nused_ref, x_ref, wgu_ref, bg_ref, bu_ref, wd_ref, bd_ref, y_ref, wgu_bf, wd_bf):
    t = pl.program_id(0)
    rows = y_ref.shape[0] // SUBLANES
    de = wd_ref.shape[0]
    grp = 2 * LANES
    active = t < nused_ref[0]
    fresh = jnp.logical_and(
        active, jnp.logical_or(t == 0, be_ref[t] != be_ref[jnp.maximum(t - 1, 0)]))

    @pl.when(fresh)
    def _():
        src = lax.broadcasted_iota(jnp.int32, (grp, grp), 0)
        dst = lax.broadcasted_iota(jnp.int32, (grp, grp), 1)
        wanted = jnp.where(dst < LANES, 2 * dst, 2 * (dst - LANES) + 1)
        perm = jnp.where(src == wanted, 1.0, 0.0).astype(BF16)
        for g in range(wgu_ref.shape[1] // grp):
            cols = slice(g * grp, (g + 1) * grp)
            wgu_bf[:, cols] = jnp.dot(wgu_ref[:, cols].astype(BF16), perm,
                                      preferred_element_type=F32).astype(BF16)
        wd_bf[...] = wd_ref[...].astype(BF16)

    @pl.when(active)
    def _():
        words = jnp.concatenate([x_ref[pl.ds(c, rows, stride=PACKED_ROWS), :] for c in range(PACKED_ROWS)], axis=1)
        xb = _unpack_bf16_pairs(words)
        gu = jnp.dot(xb, wgu_bf[...], preferred_element_type=F32)
        hid = []
        for g in range(de // LANES):
            cols = slice(g * LANES, (g + 1) * LANES)
            gate = gu[:, g * grp:g * grp + LANES] + bg_ref[:, cols]
            up = gu[:, g * grp + LANES:(g + 1) * grp] + bu_ref[:, cols]
            gate = jnp.minimum(gate, SWIGLU_LIMIT)
            up = jnp.clip(up, -SWIGLU_LIMIT, SWIGLU_LIMIT)
            hid.append((gate * _sigmoid(SWIGLU_ALPHA * gate) * (up + 1.0)).astype(BF16))
        y = jnp.dot(jnp.concatenate(hid, axis=1), wd_bf[...], preferred_element_type=F32) + bd_ref[...]
        _store_token_rows(y_ref, y)

    @pl.when(jnp.logical_not(active))
    def _():
        y_ref[...] = jnp.zeros_like(y_ref)


def _experts(block_expert, n_used, xs_rows, wgu, bg, bu, wd, bd):
    n_rows = xs_rows.shape[0] // PACKED_ROWS
    _, d, de2 = wgu.shape
    de = wd.shape[1]
    n_tiles = n_rows // EXPERT_ROWS
    tile_rows = EXPERT_ROWS * SUBLANES
    per_expert = lambda t, be, nu: (be[t], 0, 0)
    grid_spec = pltpu.PrefetchScalarGridSpec(
        num_scalar_prefetch=2,
        grid=(n_tiles,),
        in_specs=[
            pl.BlockSpec((EXPERT_ROWS * PACKED_ROWS, LANES),
                         lambda t, be, nu: (jnp.minimum(t, jnp.maximum(nu[0] - 1, 0)), 0)),
            pl.BlockSpec((None, d, de2), per_expert),
            pl.BlockSpec((None, 1, de), per_expert),
            pl.BlockSpec((None, 1, de), per_expert),
            pl.BlockSpec((None, de, d), per_expert),
            pl.BlockSpec((None, 1, d), per_expert),
        ],
        out_specs=pl.BlockSpec((tile_rows, LANES), lambda t, be, nu: (t, 0)),
        scratch_shapes=[pltpu.VMEM((d, de2), BF16), pltpu.VMEM((de, d), BF16)],
    )
    return pl.pallas_call(
        _expert_kernel,
        grid_spec=grid_spec,
        out_shape=jax.ShapeDtypeStruct((n_rows * SUBLANES, LANES), F32),
        compiler_params=pltpu.CompilerParams(
            dimension_semantics=("arbitrary",), vmem_limit_bytes=BIG_VMEM_LIMIT),
        name="experts",
    )(block_expert, n_used, xs_rows, wgu, bg, bu, wd, bd)


def _combine_kernel(len_ref, tpos_ref, src_ref, pos_ref, w_ref, h1_ref, fg_ref, ys_hbm,
                    o_ref, buf_ref, h1rows_ref, outrows_ref, sems):
    step = pl.program_id(0)
    n_steps = pl.num_programs(0)
    tt = h1_ref.shape[0]
    slot = lax.rem(step, 2)

    def start_runs(tile, buf):
        def per_expert(e, carry):
            k = tile * N_EXPERTS + e

            def piece(off, size):
                pltpu.make_async_copy(_token_rows(ys_hbm, src_ref[k] + off, size),
                                      _token_rows(buf_ref.at[buf], tpos_ref[k] + off, size), sems.at[buf]).start()

            _run_pieces(len_ref[k], piece)
            return carry

        lax.fori_loop(0, N_EXPERTS, per_expert, 0)

    def wait_runs(buf):
        whole = buf_ref.at[buf]
        pltpu.make_async_copy(_token_rows(ys_hbm, 0, tt * EXPERT_TOPK), whole, sems.at[buf]).wait()

    @pl.when(step == 0)
    def _():
        start_runs(0, 0)

    @pl.when(step + 1 < n_steps)
    def _():
        start_runs(step + 1, 1 - slot)

    wait_runs(slot)

    _store_token_rows(h1rows_ref, h1_ref[...])

    def one_token(tok, carry):
        acc = h1rows_ref[pl.ds(pl.multiple_of(tok * SUBLANES, SUBLANES), SUBLANES), :]
        for kk in range(EXPERT_TOPK):
            a = tok * EXPERT_TOPK + kk
            pos = pos_ref[a]
            acc = acc + w_ref[a] * buf_ref[slot, pl.ds(pl.multiple_of(pos * SUBLANES, SUBLANES), SUBLANES), :]
        outrows_ref[pl.ds(pl.multiple_of(tok * SUBLANES, SUBLANES), SUBLANES), :] = acc
        return carry

    lax.fori_loop(0, tt, one_token, 0, unroll=4)
    h = _load_token_rows(outrows_ref, tt)
    o_ref[...] = h * lax.rsqrt(jnp.mean(h * h, axis=-1, keepdims=True) + NORM_EPS) * fg_ref[...]


def _combine(run_len, run_tpos, run_src, pos, w_flat, h1, fg, ys_rows):
    t, d = h1.shape
    tt = ROUTE_TOKENS
    n_assign = tt * EXPERT_TOPK
    smem_blk = pl.BlockSpec((n_assign,), lambda c, *_: (c,), memory_space=pltpu.SMEM)
    grid_spec = pltpu.PrefetchScalarGridSpec(
        num_scalar_prefetch=3,
        grid=(t // tt,),
        in_specs=[
            smem_blk, smem_blk,
            pl.BlockSpec((tt, d), lambda c, *_: (c, 0)),
            pl.BlockSpec((1, d), lambda c, *_: (0, 0)),
            pl.BlockSpec(memory_space=pl.ANY),
        ],
        out_specs=pl.BlockSpec((tt, d), lambda c, *_: (c, 0)),
        scratch_shapes=[
            pltpu.VMEM((2, n_assign * SUBLANES, LANES), F32),
            pltpu.VMEM((tt * SUBLANES, LANES), F32),
            pltpu.VMEM((tt * SUBLANES, LANES), F32),
            pltpu.SemaphoreType.DMA((2,)),
        ],
    )
    return pl.pallas_call(
        _combine_kernel,
        grid_spec=grid_spec,
        out_shape=jax.ShapeDtypeStruct((t, d), F32),
        compiler_params=pltpu.CompilerParams(dimension_semantics=("arbitrary",), vmem_limit_bytes=VMEM_LIMIT),
        name="combine",
    )(run_len, run_tpos, run_src, pos, w_flat, h1, fg, ys_rows)


def _layer(h, mix_norm_g, w_in, w_attn_out, sgu_ln_g, sgu_ln_b, w_spatial, b_spatial, w_sgu_out,
           w_mix_out, ffn_norm_g, w_router, b_router, w_gate_up, b_gate_up, w_down, b_down, out_g):
    b, s, d = h.shape
    t = b * s
    sgu_width = sgu_ln_g.shape[0]
    x2 = h.reshape(t, d)

    q, qt, kt, v, kmean, u, vvn, ga, gs = _in_proj(
        x2, mix_norm_g.reshape(1, d), w_in.astype(BF16), sgu_ln_g.reshape(1, sgu_width),
        sgu_ln_b.reshape(1, sgu_width), sgu_width)

    nb = s // MOBA_BLOCK
    attn = _moba(q.reshape(b, s, ATTN_WIDTH), qt, kt, v.reshape(b, s, ATTN_WIDTH),
                 kmean.reshape(b, nb, ATTN_WIDTH), _moba_key_table(s)).reshape(t, ATTN_WIDTH)

    gdim = sgu_width // SGU_GROUPS
    bsp_full = jnp.repeat(b_spatial.T, gdim, axis=1)
    wr_pad = jnp.zeros((d, LANES), F32).at[:, :N_EXPERTS].set(w_router)
    br_pad = jnp.zeros((1, LANES), F32).at[0, :N_EXPERTS].set(b_router)
    h1, xn2, ri, rw, tile_cnt, pos_t = _mix(
        attn, u, vvn, ga, gs, x2, w_attn_out.astype(BF16), w_sgu_out.astype(BF16), w_mix_out.astype(BF16),
        w_spatial, bsp_full, ffn_norm_g.reshape(1, d), wr_pad, br_pad)

    n_assign = t * EXPERT_TOPK
    n_tiles = -(-(n_assign + N_EXPERTS * (EXPERT_ROWS - 1)) // EXPERT_ROWS)
    n_rows = n_tiles * EXPERT_ROWS
    run_len = tile_cnt[::SUBLANES, :N_EXPERTS].astype(jnp.int32)
    counts = jnp.sum(run_len, axis=0)
    padded = (counts + EXPERT_ROWS - 1) // EXPERT_ROWS * EXPERT_ROWS
    pad_end = jnp.cumsum(padded)
    starts = pad_end - padded
    run_tpos = jnp.cumsum(run_len, axis=1) - run_len
    run_row = starts[None, :] + jnp.cumsum(run_len, axis=0) - run_len
    tile_start = jnp.arange(n_tiles, dtype=jnp.int32) * EXPERT_ROWS
    block_expert = jnp.minimum(
        jnp.sum((pad_end[None, :] <= tile_start[:, None]).astype(jnp.int32), axis=1), N_EXPERTS - 1)
    n_used = (pad_end[-1:] // EXPERT_ROWS).astype(jnp.int32)
    pos = ri[:, EXPERT_TOPK:2 * EXPERT_TOPK].reshape(n_assign)
    tables = (run_len.reshape(-1), run_tpos.reshape(-1), run_row.reshape(-1))

    xs_rows = _dispatch(*tables, starts + counts, padded - counts, n_used, pos_t, xn2, n_rows)

    de = w_down.shape[1]
    bg = b_gate_up[:, 0::2].reshape(N_EXPERTS, 1, de)
    bu = b_gate_up[:, 1::2].reshape(N_EXPERTS, 1, de)
    ys_rows = _experts(block_expert, n_used, xs_rows, w_gate_up, bg, bu, w_down, b_down.reshape(N_EXPERTS, 1, d))

    out = _combine(*tables, pos, rw[:, :EXPERT_TOPK].reshape(n_assign), h1, out_g.reshape(1, d), ys_rows)
    return out.reshape(b, s, d)


def kernel(x, mix_norm_g, w_in, w_attn_out, sgu_ln_g, sgu_ln_b, w_spatial, b_spatial, w_sgu_out, w_mix_out,
           ffn_norm_g, w_router, b_router, w_gate_up, b_gate_up, w_down, b_down, final_norm_g):
    depth = w_in.shape[0]
    assert depth == 1, "the final RMSNorm is fused into the single layer's combine step"
    return _layer(x, mix_norm_g[0], w_in[0], w_attn_out[0], sgu_ln_g[0], sgu_ln_b[0], w_spatial[0],
                  b_spatial[0], w_sgu_out[0], w_mix_out[0], ffn_norm_g[0], w_router[0], b_router[0],
                  w_gate_up[0], b_gate_up[0], w_down[0], b_down[0], final_norm_g)
```

```python
import jax
import jax.numpy as jnp
import numpy as np
from jax import lax
from jax.experimental import pallas as pl
from jax.experimental.pallas import tpu as pltpu

F32 = jnp.float32
BF16 = jnp.bfloat16
NEG_INF = float("-inf")
MASK_VALUE = -1e30


def _bf16_pieces(x, n=3):
    pieces = []
    for _ in range(n):
        p = float(np.asarray(x, np.float32).astype(BF16).astype(np.float32))
        pieces.append(p)
        x = x - p
    return pieces


LOG2E = 1.4426950408889634
LOG2E_PIECES = _bf16_pieces(LOG2E)

N_HEADS = 8
HEAD_DIM = 64
ATTN_WIDTH = N_HEADS * HEAD_DIM
MOBA_BLOCK = 256
MOBA_TOPK = 3
SGU_CHUNK = 128
SGU_GROUPS = 8
N_EXPERTS = 32
EXPERT_TOPK = 4
SWIGLU_LIMIT = 7.0
SWIGLU_ALPHA = 1.702
NORM_EPS = 1e-5

LANES = 128
SUBLANES = 8
PACKED_ROWS = 4
HEADS_PER_LANE_TILE = LANES // HEAD_DIM
MOBA_GROUP = 1
MOBA_BIAS_LANE0 = 64
MOBA_DUMMY_LANE = LANES - 1
PROJ_ROWS = 256
EXPERT_ROWS = 720
ROUTE_TOKENS = 512
VMEM_LIMIT = 48 * 1024 * 1024
BIG_VMEM_LIMIT = 56 * 1024 * 1024


def _sigmoid(x):
    return 1.0 / (1.0 + jnp.exp(-x))


def _store_token_rows(ref, value, base=0):
    n = value.shape[0]
    for c in range(value.shape[1] // LANES):
        ref[pl.ds(base + c, n, stride=SUBLANES), :] = value[:, c * LANES:(c + 1) * LANES]


def _load_token_rows(ref, n, base=0):
    return jnp.concatenate([ref[pl.ds(base + c, n, stride=SUBLANES), :] for c in range(SUBLANES)], axis=1)


def _gelu_exact(x):
    return 0.5 * x * (1.0 + lax.erf(x * (0.5 ** 0.5)))


def _in_proj_kernel(x_ref, g_ref, w_ref, lng_ref, lnb_ref,
                    q_ref, qt_ref, kt_ref, v_ref, kmean_ref, u_ref, vvn_ref, ga_ref, gs_ref):
    x = x_ref[...]
    xn = x * lax.rsqrt(jnp.mean(x * x, axis=-1, keepdims=True) + NORM_EPS) * g_ref[...]
    xb = xn.astype(BF16)

    def proj(lo, hi):
        return jnp.dot(xb, w_ref[:, lo:hi], preferred_element_type=F32)

    a = ATTN_WIDTH
    sw = u_ref.shape[1]
    d = ga_ref.shape[1]
    qf = proj(0, a)
    q_ref[...] = qf
    kf = proj(a, 2 * a)
    for j in range(kf.shape[0] // MOBA_BLOCK):
        qt_ref[j] = qf[j * MOBA_BLOCK:(j + 1) * MOBA_BLOCK].T
        kblk = kf[j * MOBA_BLOCK:(j + 1) * MOBA_BLOCK]
        kt_ref[j] = kblk.T.astype(BF16)
        kmean_ref[j] = jnp.mean(kblk, axis=0, keepdims=True)
    v_ref[...] = proj(2 * a, 3 * a).astype(BF16)
    z0 = 3 * a
    u_ref[...] = _gelu_exact(proj(z0, z0 + sw))
    zv = _gelu_exact(proj(z0 + sw, z0 + 2 * sw))
    mu = jnp.mean(zv, axis=-1, keepdims=True)
    zc = zv - mu
    var = jnp.mean(zc * zc, axis=-1, keepdims=True)
    vvn_ref[...] = zc * lax.rsqrt(var + NORM_EPS) * lng_ref[...] + lnb_ref[...]
    g0 = z0 + 2 * sw
    ga_ref[...] = _sigmoid(proj(g0, g0 + d))
    gs_ref[...] = _sigmoid(proj(g0 + d, g0 + 2 * d))


def _in_proj(x2, g, w_bf, lng, lnb, sgu_width):
    t, d = x2.shape
    tm = PROJ_ROWS
    n_in = w_bf.shape[1]
    row = lambda w: pl.BlockSpec((tm, w), lambda i: (i, 0))
    const = lambda shape: pl.BlockSpec(shape, lambda i: (0,) * len(shape))
    out_shape = (
        jax.ShapeDtypeStruct((t, ATTN_WIDTH), F32),
        jax.ShapeDtypeStruct((t // MOBA_BLOCK, ATTN_WIDTH, MOBA_BLOCK), F32),
        jax.ShapeDtypeStruct((t // MOBA_BLOCK, ATTN_WIDTH, MOBA_BLOCK), BF16),
        jax.ShapeDtypeStruct((t, ATTN_WIDTH), BF16),
        jax.ShapeDtypeStruct((t // MOBA_BLOCK, 1, ATTN_WIDTH), F32),
        jax.ShapeDtypeStruct((t, sgu_width), F32),
        jax.ShapeDtypeStruct((t, sgu_width), F32),
        jax.ShapeDtypeStruct((t, d), F32),
        jax.ShapeDtypeStruct((t, d), F32),
    )
    out_specs = (
        row(ATTN_WIDTH),
        pl.BlockSpec((tm // MOBA_BLOCK, ATTN_WIDTH, MOBA_BLOCK), lambda i: (i, 0, 0)),
        pl.BlockSpec((tm // MOBA_BLOCK, ATTN_WIDTH, MOBA_BLOCK), lambda i: (i, 0, 0)),
        row(ATTN_WIDTH),
        pl.BlockSpec((tm // MOBA_BLOCK, 1, ATTN_WIDTH), lambda i: (i, 0, 0)),
        row(sgu_width), row(sgu_width), row(d), row(d),
    )
    return pl.pallas_call(
        _in_proj_kernel,
        grid=(t // tm,),
        in_specs=[row(d), const((1, d)), const((d, n_in)), const((1, sgu_width)), const((1, sgu_width))],
        out_specs=out_specs,
        out_shape=out_shape,
        compiler_params=pltpu.CompilerParams(dimension_semantics=("arbitrary",), vmem_limit_bytes=VMEM_LIMIT),
        name="in_proj",
    )(x2, g, w_bf, lng, lnb)


def _alibi_slopes():
    slopes = 2.0 ** (-8.0 * np.arange(1, N_HEADS + 1, dtype=np.float64) / N_HEADS)
    assert all(np.log2(s) == np.round(np.log2(s)) for s in slopes), "ALiBi slopes must be powers of two"
    return slopes


def _moba_bias_lane(head_in_tile, part, piece):
    return MOBA_BIAS_LANE0 + (head_in_tile * 2 + part) * len(LOG2E_PIECES) + piece


def _moba_key_table(s):
    nb = s // MOBA_BLOCK
    slopes = _alibi_slopes()
    n_pairs = N_HEADS // HEADS_PER_LANE_TILE
    table = np.zeros((nb + 1, n_pairs, LANES, MOBA_BLOCK), np.float32)
    offs = np.arange(MOBA_BLOCK, dtype=np.float32)
    assert HEADS_PER_LANE_TILE * nb <= MOBA_BIAS_LANE0, "block one-hot rows must not overlap the ALiBi rows"
    for j in range(nb):
        for hh in range(HEADS_PER_LANE_TILE):
            table[j, :, hh * nb + j, :] = 1.0
        for p in range(n_pairs):
            for hh in range(HEADS_PER_LANE_TILE):
                slope = slopes[p * HEADS_PER_LANE_TILE + hh]
                for piece in range(len(LOG2E_PIECES)):
                    table[j, p, _moba_bias_lane(hh, 0, piece), :] = slope * MOBA_BLOCK * j
                    table[j, p, _moba_bias_lane(hh, 1, piece), :] = slope * offs
    table[nb, :, MOBA_DUMMY_LANE, :] = 1.0
    as_bf16 = table.astype(BF16)
    assert np.array_equal(as_bf16.astype(np.float32), table), "bias table must be exact in bf16"
    return jnp.asarray(as_bf16.reshape(nb + 1, n_pairs * LANES, MOBA_BLOCK))


def _moba_kernel(q_ref, qt_ref, qn_ref, qtn_ref, kt_ref, v_ref, km_ref, ct_ref, causal_ref, o_ref, qaug_ref,
                 s_ref, mpart_ref, mrow_ref, acc_ref):
    i = pl.program_id(2)
    nb = kt_ref.shape[0]
    blk = MOBA_BLOCK
    lane = lax.broadcasted_iota(jnp.int32, (1, LANES), 1)
    n_groups = lax.div(i + MOBA_GROUP, MOBA_GROUP)
    heads = range(HEADS_PER_LANE_TILE)
    hmasks = [(lane >= HEAD_DIM * hh) & (lane < HEAD_DIM * (hh + 1)) for hh in heads]
    slot = lax.rem(i, 2)

    def build_query_operand(q, qt, tile, into):
        bid = lax.broadcasted_iota(jnp.int32, (nb, blk), 0)
        bid_f = bid.astype(F32)
        masks_t = []
        for hh in heads:
            km_h = jnp.where(hmasks[hh], km_ref[...], 0.0)
            gate = jnp.dot(km_h, qt, precision=lax.Precision.HIGHEST, preferred_element_type=F32)
            gate = jnp.where(bid < tile, gate, NEG_INF)
            blockmask = jnp.where(bid == tile, 0.0, MASK_VALUE)
            for kk in range(MOBA_TOPK):
                gmax = jnp.max(gate, axis=0, keepdims=True)
                first = jnp.min(jnp.where(gate == gmax, bid_f, float(nb)), axis=0, keepdims=True)
                valid = (tile > kk).astype(F32)
                first = first * valid + (valid - 1.0)
                hit = bid_f == first
                blockmask = jnp.where(hit, 0.0, blockmask)
                gate = jnp.where(hit, NEG_INF, gate)
            masks_t.append(blockmask)
        masks_t.append(jnp.zeros((LANES - len(masks_t) * nb, blk), F32))
        masks = jnp.concatenate(masks_t, axis=0).T

        for hh in heads:
            feats = jnp.where(lane == MOBA_DUMMY_LANE, MASK_VALUE, 0.0)
            for part in range(2):
                for piece, value in enumerate(LOG2E_PIECES):
                    feats = jnp.where(lane == _moba_bias_lane(hh, part, piece), value, feats)
            extra = jnp.where((lane >= hh * nb) & (lane < (hh + 1) * nb), masks, feats)
            qh = jnp.where(hmasks[hh], q, 0.0)
            qaug_ref[into, hh * blk:(hh + 1) * blk, 0:LANES] = (qh * (HEAD_DIM ** -0.5 * LOG2E)).astype(BF16)
            qaug_ref[into, hh * blk:(hh + 1) * blk, LANES:2 * LANES] = extra.astype(BF16)

    @pl.when(i == 0)
    def _():
        build_query_operand(q_ref[...], qt_ref[...], i, slot)

    def scores(jk, jc):
        rhs = jnp.concatenate([kt_ref[jk], ct_ref[jc]], axis=0)
        return jnp.dot(qaug_ref[slot], rhs, preferred_element_type=F32)

    def lane_halves_max(s):
        return jnp.maximum(s[:, 0:LANES], s[:, LANES:2 * LANES])

    n_rows = len(heads) * blk
    variants = [(groups, min(groups * MOBA_GROUP, nb)) for groups in range(1, pl.cdiv(nb, MOBA_GROUP) + 1)]

    for groups, n_blocks in variants:
        @pl.when(n_groups == groups)
        def _(n_blocks=n_blocks):
            part = None
            for j in range(n_blocks):
                s = scores(j, jnp.where(j <= i, j, nb)) + causal_ref[(i == j).astype(jnp.int32)]
                s_ref[j] = s
                part = lane_halves_max(s) if part is None else jnp.maximum(part, lane_halves_max(s))
            mpart_ref[...] = part

    ones = jnp.ones((blk, LANES), BF16)
    rowmax = jnp.max(mpart_ref[...], axis=1, keepdims=True)
    mrow_ref[...] = jnp.broadcast_to(rowmax, (n_rows, 2 * LANES))

    def weighted(j):
        p = jnp.exp2(s_ref[j] - mrow_ref[...]).astype(BF16)
        vaug = jnp.concatenate([v_ref[j * blk:(j + 1) * blk, :], ones], axis=1)
        return jnp.dot(p, vaug, preferred_element_type=F32)

    next_tile = jnp.minimum(i + 1, nb - 1)
    for groups, n_blocks in variants:
        @pl.when(n_groups == groups)
        def _(n_blocks=n_blocks):
            build_query_operand(qn_ref[...], qtn_ref[...], next_tile, 1 - slot)
            tot = weighted(0)
            for j in range(1, n_blocks):
                tot = tot + weighted(j)
            acc_ref[...] = tot

    out = jnp.zeros((blk, LANES), F32)
    for hh in heads:
        acc = acc_ref[hh * blk:(hh + 1) * blk, :]
        out = jnp.where(hmasks[hh], acc[:, 0:LANES] / acc[:, LANES:2 * LANES], out)
    o_ref[...] = out


def _moba(q, qt, kt, v, kmean, key_table):
    b, s, a = q.shape
    nb = s // MOBA_BLOCK
    n_pairs = a // LANES
    rows = HEADS_PER_LANE_TILE * MOBA_BLOCK
    row_in_block = np.arange(rows)[:, None] % MOBA_BLOCK
    causal = np.where(row_in_block >= np.arange(MOBA_BLOCK)[None, :], 0.0, MASK_VALUE).astype(np.float32)
    causal_tiles = jnp.asarray(np.stack([np.zeros_like(causal), causal]))
    nxt = lambda i: jnp.minimum(i + 1, nb - 1)
    return pl.pallas_call(
        _moba_kernel,
        grid=(b, n_pairs, nb),
        in_specs=[
            pl.BlockSpec((None, MOBA_BLOCK, LANES), lambda bi, p, i: (bi, i, p)),
            pl.BlockSpec((None, LANES, MOBA_BLOCK), lambda bi, p, i: (bi * nb + i, p, 0)),
            pl.BlockSpec((None, MOBA_BLOCK, LANES), lambda bi, p, i: (bi, nxt(i), p)),
            pl.BlockSpec((None, LANES, MOBA_BLOCK), lambda bi, p, i: (bi * nb + nxt(i), p, 0)),
            pl.BlockSpec((nb, LANES, MOBA_BLOCK), lambda bi, p, i: (bi, p, 0)),
            pl.BlockSpec((None, s, LANES), lambda bi, p, i: (bi, 0, p)),
            pl.BlockSpec((None, nb, LANES), lambda bi, p, i: (bi, 0, p)),
            pl.BlockSpec((nb + 1, LANES, MOBA_BLOCK), lambda bi, p, i: (0, p, 0)),
            pl.BlockSpec((2, rows, MOBA_BLOCK), lambda bi, p, i: (0, 0, 0)),
        ],
        out_specs=pl.BlockSpec((None, MOBA_BLOCK, LANES), lambda bi, p, i: (bi, i, p)),
        out_shape=jax.ShapeDtypeStruct((b, s, a), F32),
        scratch_shapes=[
            pltpu.VMEM((2, rows, 2 * LANES), BF16),
            pltpu.VMEM((nb + 1, rows, MOBA_BLOCK), F32),
            pltpu.VMEM((rows, LANES), F32),
            pltpu.VMEM((rows, 2 * LANES), F32),
            pltpu.VMEM((rows, 2 * LANES), F32),
        ],
        compiler_params=pltpu.CompilerParams(
            dimension_semantics=("arbitrary", "arbitrary", "arbitrary"), vmem_limit_bytes=VMEM_LIMIT),
        name="moba",
    )(q, qt, q, qt, kt, v, kmean, key_table, causal_tiles)


def _mix_kernel(attn_ref, u_ref, vvn_ref, ga_ref, gs_ref, x_ref,
                wao_ref, wso_ref, wmo_ref, wsp_ref, bsp_ref, fg_ref, wr_ref, br_ref,
                h1_ref, xn2_ref, ri_ref, rw_ref, tcnt_ref, post_ref, wcausal_ref):
    tm = x_ref.shape[0]
    ch = SGU_CHUNK

    @pl.when(pl.program_id(0) == 0)
    def _():
        tril = lax.broadcasted_iota(jnp.int32, (ch, ch), 0) >= lax.broadcasted_iota(jnp.int32, (ch, ch), 1)
        for g in range(SGU_GROUPS):
            wcausal_ref[g] = jnp.where(tril, wsp_ref[g], 0.0).astype(BF16)

    y_attn = jnp.dot(attn_ref[...].astype(BF16), wao_ref[...], preferred_element_type=F32)

    lane = lax.broadcasted_iota(jnp.int32, (1, LANES), 1)
    gdim = vvn_ref.shape[1] // SGU_GROUPS
    groups_per_tile = LANES // gdim
    w_causal = [wcausal_ref[g] for g in range(SGU_GROUPS)]
    rows = []
    for c in range(tm // ch):
        cols = []
        for ct in range(vvn_ref.shape[1] // LANES):
            vp = vvn_ref[c * ch:(c + 1) * ch, ct * LANES:(ct + 1) * LANES]
            acc = jnp.zeros((ch, LANES), F32)
            for gg in range(groups_per_tile):
                gmask = (lane >= gdim * gg) & (lane < gdim * (gg + 1))
                vm = jnp.where(gmask, vp, 0.0).astype(BF16)
                acc = acc + jnp.dot(w_causal[ct * groups_per_tile + gg], vm, preferred_element_type=F32)
            cols.append(acc)
        rows.append(jnp.concatenate(cols, axis=1) + bsp_ref[...])
    mixed = jnp.concatenate(rows, axis=0)
    sgu = u_ref[...] * mixed
    y_sgu = jnp.dot(sgu.astype(BF16), wso_ref[...], preferred_element_type=F32)

    merged = ga_ref[...] * y_attn + gs_ref[...] * y_sgu
    h1 = x_ref[...] + jnp.dot(merged.astype(BF16), wmo_ref[...], preferred_element_type=F32)
    h1_ref[...] = h1
    xn2 = h1 * lax.rsqrt(jnp.mean(h1 * h1, axis=-1, keepdims=True) + NORM_EPS) * fg_ref[...]
    xn2_ref[...] = xn2.astype(BF16)

    def split(a):
        hi = a.astype(BF16)
        return hi, (a - hi.astype(F32)).astype(BF16)

    x_hi, x_lo = split(xn2)
    w_hi, w_lo = split(wr_ref[...])
    logits = (jnp.dot(x_hi, w_hi, preferred_element_type=F32) + jnp.dot(x_lo, w_hi, preferred_element_type=F32)
              + jnp.dot(x_hi, w_lo, preferred_element_type=F32)) + br_ref[...]
    work = jnp.where(lane < N_EXPERTS, logits, NEG_INF)
    lane_f = lane.astype(F32)
    vals, idxs = [], []
    for _ in range(EXPERT_TOPK):
        vmax = jnp.max(work, axis=1, keepdims=True)
        first = jnp.min(jnp.where(work == vmax, lane_f, float(LANES)), axis=1, keepdims=True)
        vals.append(vmax)
        idxs.append(first)
        work = jnp.where(lane_f == first, NEG_INF, work)
    exps = [jnp.exp(v - vals[0]) for v in vals]
    denom = exps[0]
    for e in exps[1:]:
        denom = denom + e
    chosen = jnp.zeros((tm, LANES), F32)
    for first in idxs:
        chosen = jnp.where(lane_f == first, 1.0, chosen)

    strict_lower = (lax.broadcasted_iota(jnp.int32, (tm, tm), 0)
                    > lax.broadcasted_iota(jnp.int32, (tm, tm), 1))
    before = jnp.dot(jnp.where(strict_lower, 1.0, 0.0).astype(BF16), chosen.astype(BF16),
                     preferred_element_type=F32)
    counts = jnp.broadcast_to(jnp.sum(chosen, axis=0, keepdims=True), tcnt_ref.shape)
    lower_expert = (lax.broadcasted_iota(jnp.int32, (LANES, LANES), 0)
                    < lax.broadcasted_iota(jnp.int32, (LANES, LANES), 1))
    expert_start = jnp.dot(counts, jnp.where(lower_expert, 1.0, 0.0), precision=lax.Precision.HIGHEST,
                           preferred_element_type=F32)[0:1, :]
    sorted_pos = before + expert_start
    ri = jnp.zeros((tm, LANES), jnp.int32)
    rw = jnp.zeros((tm, LANES), F32)
    pos_lanes = jnp.zeros((tm, LANES), F32)
    for kk in range(EXPERT_TOPK):
        pos = jnp.sum(jnp.where(lane_f == idxs[kk], sorted_pos, 0.0), axis=1, keepdims=True)
        ri = jnp.where(lane == kk, idxs[kk].astype(jnp.int32), ri)
        ri = jnp.where(lane == EXPERT_TOPK + kk, pos.astype(jnp.int32), ri)
        rw = jnp.where(lane == kk, exps[kk] / denom, rw)
        pos_lanes = jnp.where(lane == kk, pos, pos_lanes)
    ri_ref[...] = ri
    rw_ref[...] = rw
    tcnt_ref[...] = counts
    post_ref[...] = pos_lanes.T[0:SUBLANES, :]


def _mix(attn, u, vvn, ga, gs, x2, wao, wso, wmo, wsp, bsp_full, fg, wr_pad, br_pad):
    t, d = x2.shape
    tm = ROUTE_TOKENS
    row = lambda w: pl.BlockSpec((tm, w), lambda i: (i, 0))
    const = lambda shape: pl.BlockSpec(shape, lambda i: (0,) * len(shape))
    ins = (attn, u, vvn, ga, gs, x2, wao, wso, wmo, wsp, bsp_full, fg, wr_pad, br_pad)
    in_specs = [row(a.shape[1]) for a in ins[:6]] + [const(a.shape) for a in ins[6:]]
    out_shape = (
        jax.ShapeDtypeStruct((t, d), F32),
        jax.ShapeDtypeStruct((t, d), BF16),
        jax.ShapeDtypeStruct((t, LANES), jnp.int32),
        jax.ShapeDtypeStruct((t, LANES), F32),
        jax.ShapeDtypeStruct((t // tm * SUBLANES, LANES), F32),
        jax.ShapeDtypeStruct((t // tm * SUBLANES, tm), F32),
    )
    assert d == SUBLANES * LANES, "a token row is stored as one (8, 128) tile"
    per_tile = lambda w: pl.BlockSpec((SUBLANES, w), lambda i: (i, 0))
    out_specs = (row(d), row(d), row(LANES), row(LANES), per_tile(LANES), per_tile(tm))
    return pl.pallas_call(
        _mix_kernel,
        grid=(t // tm,),
        in_specs=in_specs,
        out_specs=out_specs,
        out_shape=out_shape,
        scratch_shapes=[pltpu.VMEM(wsp.shape, BF16)],
        compiler_params=pltpu.CompilerParams(dimension_semantics=("arbitrary",), vmem_limit_bytes=BIG_VMEM_LIMIT),
        name="mix_route",
    )(*ins)


def _run_pieces(n, body):
    off = jnp.int32(0)
    for bit in reversed(range(ROUTE_TOKENS.bit_length())):
        size = 1 << bit
        take = lax.bitwise_and(n, size) != 0

        @pl.when(take)
        def _(off=off, size=size):
            body(off, size)

        off = off + jnp.where(take, size, 0)


def _token_rows(ref, r, n, rows_per_token=SUBLANES):
    return ref.at[pl.ds(pl.multiple_of(r * rows_per_token, rows_per_token), n * rows_per_token)]


def _pack_bf16_pairs(value):
    half = value.shape[1] // 2
    lo = lax.shift_right_logical(lax.bitcast_convert_type(value[:, :half], jnp.uint32), jnp.uint32(16))
    hi = lax.bitwise_and(lax.bitcast_convert_type(value[:, half:], jnp.uint32), jnp.uint32(0xFFFF0000))
    return lax.bitwise_or(lo, hi)


def _unpack_bf16_pairs(words):
    lo = lax.bitcast_convert_type(lax.shift_left(words, jnp.uint32(16)), F32)
    hi = lax.bitcast_convert_type(lax.bitwise_and(words, jnp.uint32(0xFFFF0000)), F32)
    return jnp.concatenate([lo, hi], axis=1).astype(BF16)


def _dispatch_kernel(len_ref, tpos_ref, dst_ref, padlo_ref, padlen_ref, nused_ref, post_ref, x_ref,
                     xs_hbm, sorted_ref, zeros_ref, sems, zsem):
    step = pl.program_id(0)
    n_steps = pl.num_programs(0)
    tt = x_ref.shape[0]
    n_assign = tt * EXPERT_TOPK
    rpt = PACKED_ROWS
    tile_rows = zeros_ref.shape[0] // rpt
    slot = lax.rem(step, 2)

    def start_runs(tile, buf):
        def per_expert(e, carry):
            k = tile * N_EXPERTS + e

            def piece(off, size):
                pltpu.make_async_copy(_token_rows(sorted_ref.at[buf], tpos_ref[k] + off, size, rpt),
                                      _token_rows(xs_hbm, dst_ref[k] + off, size, rpt), sems.at[buf]).start()

            _run_pieces(len_ref[k], piece)
            return carry

        lax.fori_loop(0, N_EXPERTS, per_expert, 0)

    def wait_runs(buf):
        whole = sorted_ref.at[buf]
        pltpu.make_async_copy(whole, _token_rows(xs_hbm, 0, n_assign, rpt), sems.at[buf]).wait()

    @pl.when(step == 0)
    def _():
        zeros_ref[...] = jnp.zeros_like(zeros_ref)

        def tail_tile(tl, carry):
            cp = pltpu.make_async_copy(zeros_ref, _token_rows(xs_hbm, tl * tile_rows, tile_rows, rpt), zsem)
            cp.start()
            cp.wait()
            return carry

        lax.fori_loop(nused_ref[0], xs_hbm.shape[0] // (tile_rows * rpt), tail_tile, 0)

        def per_expert(e, carry):
            def piece(off, size):
                cp = pltpu.make_async_copy(_token_rows(zeros_ref, 0, size, rpt),
                                           _token_rows(xs_hbm, padlo_ref[e] + off, size, rpt), zsem)
                cp.start()
                cp.wait()

            _run_pieces(padlen_ref[e], piece)
            return carry

        lax.fori_loop(0, N_EXPERTS, per_expert, 0)

    @pl.when(step >= 2)
    def _():
        wait_runs(slot)

    x = x_ref[...]
    pos_by_choice = [post_ref[kk:kk + 1, :] for kk in range(EXPERT_TOPK)]
    chunk = MOBA_BLOCK
    for rc in range(n_assign // chunk):
        r = (lax.broadcasted_iota(jnp.int32, (chunk, 1), 0) + rc * chunk).astype(F32)
        select = jnp.zeros((chunk, tt), F32)
        for pos in pos_by_choice:
            select = jnp.where(r == pos, 1.0, select)
        rows = jnp.dot(select.astype(BF16), x, preferred_element_type=F32)
        words = _pack_bf16_pairs(rows)
        for c in range(rpt):
            sorted_ref[slot, pl.ds(rc * chunk * rpt + c, chunk, stride=rpt), :] = words[:, c * LANES:(c + 1) * LANES]
    start_runs(step, slot)

    @pl.when(step == n_steps - 1)
    def _():
        @pl.when(step >= 1)
        def _():
            wait_runs(1 - slot)

        wait_runs(slot)


def _dispatch(run_len, run_tpos, run_dst, padlo, padlen, n_used, pos_t, xn2, n_rows):
    t, d = xn2.shape
    n_assign = ROUTE_TOKENS * EXPERT_TOPK
    grid_spec = pltpu.PrefetchScalarGridSpec(
        num_scalar_prefetch=6,
        grid=(t // ROUTE_TOKENS,),
        in_specs=[pl.BlockSpec((SUBLANES, ROUTE_TOKENS), lambda c, *_: (c, 0)),
                  pl.BlockSpec((ROUTE_TOKENS, d), lambda c, *_: (c, 0))],
        out_specs=pl.BlockSpec(memory_space=pl.ANY),
        scratch_shapes=[
            pltpu.VMEM((2, n_assign * PACKED_ROWS, LANES), jnp.uint32),
            pltpu.VMEM((EXPERT_ROWS * PACKED_ROWS, LANES), jnp.uint32),
            pltpu.SemaphoreType.DMA((2,)),
            pltpu.SemaphoreType.DMA(()),
        ],
    )
    assert d == 2 * PACKED_ROWS * LANES
    return pl.pallas_call(
        _dispatch_kernel,
        grid_spec=grid_spec,
        out_shape=jax.ShapeDtypeStruct((n_rows * PACKED_ROWS, LANES), jnp.uint32),
        compiler_params=pltpu.CompilerParams(dimension_semantics=("arbitrary",), vmem_limit_bytes=VMEM_LIMIT),
        name="dispatch",
    )(run_len, run_tpos, run_dst, padlo, padlen, n_used, pos_t, xn2)


def _expert_kernel(be_ref, nused_ref, x_ref, wgu_ref, bg_ref, bu_ref, wd_ref, bd_ref, y_ref, wgu_bf, wd_bf):
    t = pl.program_id(0)
    rows = y_ref.shape[0] // SUBLANES
    de = wd_ref.shape[0]
    grp = 2 * LANES
    active = t < nused_ref[0]
    fresh = jnp.logical_and(
        active, jnp.logical_or(t == 0, be_ref[t] != be_ref[jnp.maximum(t - 1, 0)]))

    @pl.when(fresh)
    def _():
        src = lax.broadcasted_iota(jnp.int32, (grp, grp), 0)
        dst = lax.broadcasted_iota(jnp.int32, (grp, grp), 1)
        wanted = jnp.where(dst < LANES, 2 * dst, 2 * (dst - LANES) + 1)
        perm = jnp.where(src == wanted, 1.0, 0.0).astype(BF16)
        for g in range(wgu_ref.shape[1] // grp):
            cols = slice(g * grp, (g + 1) * grp)
            wgu_bf[:, cols] = jnp.dot(wgu_ref[:, cols].astype(BF16), perm,
                                      preferred_element_type=F32).astype(BF16)
        wd_bf[...] = wd_ref[...].astype(BF16)

    @pl.when(active)
    def _():
        words = jnp.concatenate([x_ref[pl.ds(c, rows, stride=PACKED_ROWS), :] for c in range(PACKED_ROWS)], axis=1)
        xb = _unpack_bf16_pairs(words)
        gu = jnp.dot(xb, wgu_bf[...], preferred_element_type=F32)
        hid = []
        for g in range(de // LANES):
            cols = slice(g * LANES, (g + 1) * LANES)
            gate = gu[:, g * grp:g * grp + LANES] + bg_ref[:, cols]
            up = gu[:, g * grp + LANES:(g + 1) * grp] + bu_ref[:, cols]
            gate = jnp.minimum(gate, SWIGLU_LIMIT)
            up = jnp.clip(up, -SWIGLU_LIMIT, SWIGLU_LIMIT)
            hid.append((gate * _sigmoid(SWIGLU_ALPHA * gate) * (up + 1.0)).astype(BF16))
        y = jnp.dot(jnp.concatenate(hid, axis=1), wd_bf[...], preferred_element_type=F32) + bd_ref[...]
        _store_token_rows(y_ref, y)

    @pl.when(jnp.logical_not(active))
    def _():
        y_ref[...] = jnp.zeros_like(y_ref)


def _experts(block_expert, n_used, xs_rows, wgu, bg, bu, wd, bd):
    n_rows = xs_rows.shape[0] // PACKED_ROWS
    _, d, de2 = wgu.shape
    de = wd.shape[1]
    n_tiles = n_rows // EXPERT_ROWS
    tile_rows = EXPERT_ROWS * SUBLANES
    per_expert = lambda t, be, nu: (be[t], 0, 0)
    grid_spec = pltpu.PrefetchScalarGridSpec(
        num_scalar_prefetch=2,
        grid=(n_tiles,),
        in_specs=[
            pl.BlockSpec((EXPERT_ROWS * PACKED_ROWS, LANES),
                         lambda t, be, nu: (jnp.minimum(t, jnp.maximum(nu[0] - 1, 0)), 0)),
            pl.BlockSpec((None, d, de2), per_expert),
            pl.BlockSpec((None, 1, de), per_expert),
            pl.BlockSpec((None, 1, de), per_expert),
            pl.BlockSpec((None, de, d), per_expert),
            pl.BlockSpec((None, 1, d), per_expert),
        ],
        out_specs=pl.BlockSpec((tile_rows, LANES), lambda t, be, nu: (t, 0)),
        scratch_shapes=[pltpu.VMEM((d, de2), BF16), pltpu.VMEM((de, d), BF16)],
    )
    return pl.pallas_call(
        _expert_kernel,
        grid_spec=grid_spec,
        out_shape=jax.ShapeDtypeStruct((n_rows * SUBLANES, LANES), F32),
        compiler_params=pltpu.CompilerParams(
            dimension_semantics=("arbitrary",), vmem_limit_bytes=BIG_VMEM_LIMIT),
        name="experts",
    )(block_expert, n_used, xs_rows, wgu, bg, bu, wd, bd)


def _combine_kernel(len_ref, tpos_ref, src_ref, pos_ref, w_ref, h1_ref, fg_ref, ys_hbm,
                    o_ref, buf_ref, h1rows_ref, outrows_ref, sems):
    step = pl.program_id(0)
    n_steps = pl.num_programs(0)
    tt = h1_ref.shape[0]
    slot = lax.rem(step, 2)

    def start_runs(tile, buf):
        def per_expert(e, carry):
            k = tile * N_EXPERTS + e

            def piece(off, size):
                pltpu.make_async_copy(_token_rows(ys_hbm, src_ref[k] + off, size),
                                      _token_rows(buf_ref.at[buf], tpos_ref[k] + off, size), sems.at[buf]).start()

            _run_pieces(len_ref[k], piece)
            return carry

        lax.fori_loop(0, N_EXPERTS, per_expert, 0)

    def wait_runs(buf):
        whole = buf_ref.at[buf]
        pltpu.make_async_copy(_token_rows(ys_hbm, 0, tt * EXPERT_TOPK), whole, sems.at[buf]).wait()

    @pl.when(step == 0)
    def _():
        start_runs(0, 0)

    @pl.when(step + 1 < n_steps)
    def _():
        start_runs(step + 1, 1 - slot)

    wait_runs(slot)

    _store_token_rows(h1rows_ref, h1_ref[...])

    def one_token(tok, carry):
        acc = h1rows_ref[pl.ds(pl.multiple_of(tok * SUBLANES, SUBLANES), SUBLANES), :]
        for kk in range(EXPERT_TOPK):
            a = tok * EXPERT_TOPK + kk
            pos = pos_ref[a]
            acc = acc + w_ref[a] * buf_ref[slot, pl.ds(pl.multiple_of(pos * SUBLANES, SUBLANES), SUBLANES), :]
        outrows_ref[pl.ds(pl.multiple_of(tok * SUBLANES, SUBLANES), SUBLANES), :] = acc
        return carry

    lax.fori_loop(0, tt, one_token, 0, unroll=4)
    h = _load_token_rows(outrows_ref, tt)
    o_ref[...] = h * lax.rsqrt(jnp.mean(h * h, axis=-1, keepdims=True) + NORM_EPS) * fg_ref[...]


def _combine(run_len, run_tpos, run_src, pos, w_flat, h1, fg, ys_rows):
    t, d = h1.shape
    tt = ROUTE_TOKENS
    n_assign = tt * EXPERT_TOPK
    smem_blk = pl.BlockSpec((n_assign,), lambda c, *_: (c,), memory_space=pltpu.SMEM)
    grid_spec = pltpu.PrefetchScalarGridSpec(
        num_scalar_prefetch=3,
        grid=(t // tt,),
        in_specs=[
            smem_blk, smem_blk,
            pl.BlockSpec((tt, d), lambda c, *_: (c, 0)),
            pl.BlockSpec((1, d), lambda c, *_: (0, 0)),
            pl.BlockSpec(memory_space=pl.ANY),
        ],
        out_specs=pl.BlockSpec((tt, d), lambda c, *_: (c, 0)),
        scratch_shapes=[
            pltpu.VMEM((2, n_assign * SUBLANES, LANES), F32),
            pltpu.VMEM((tt * SUBLANES, LANES), F32),
            pltpu.VMEM((tt * SUBLANES, LANES), F32),
            pltpu.SemaphoreType.DMA((2,)),
        ],
    )
    return pl.pallas_call(
        _combine_kernel,
        grid_spec=grid_spec,
        out_shape=jax.ShapeDtypeStruct((t, d), F32),
        compiler_params=pltpu.CompilerParams(dimension_semantics=("arbitrary",), vmem_limit_bytes=VMEM_LIMIT),
        name="combine",
    )(run_len, run_tpos, run_src, pos, w_flat, h1, fg, ys_rows)


def _layer(h, mix_norm_g, w_in, w_attn_out, sgu_ln_g, sgu_ln_b, w_spatial, b_spatial, w_sgu_out,
           w_mix_out, ffn_norm_g, w_router, b_router, w_gate_up, b_gate_up, w_down, b_down, out_g):
    b, s, d = h.shape
    t = b * s
    sgu_width = sgu_ln_g.shape[0]
    x2 = h.reshape(t, d)

    q, qt, kt, v, kmean, u, vvn, ga, gs = _in_proj(
        x2, mix_norm_g.reshape(1, d), w_in.astype(BF16), sgu_ln_g.reshape(1, sgu_width),
        sgu_ln_b.reshape(1, sgu_width), sgu_width)

    nb = s // MOBA_BLOCK
    attn = _moba(q.reshape(b, s, ATTN_WIDTH), qt, kt, v.reshape(b, s, ATTN_WIDTH),
                 kmean.reshape(b, nb, ATTN_WIDTH), _moba_key_table(s)).reshape(t, ATTN_WIDTH)

    gdim = sgu_width // SGU_GROUPS
    bsp_full = jnp.repeat(b_spatial.T, gdim, axis=1)
    wr_pad = jnp.zeros((d, LANES), F32).at[:, :N_EXPERTS].set(w_router)
    br_pad = jnp.zeros((1, LANES), F32).at[0, :N_EXPERTS].set(b_router)
    h1, xn2, ri, rw, tile_cnt, pos_t = _mix(
        attn, u, vvn, ga, gs, x2, w_attn_out.astype(BF16), w_sgu_out.astype(BF16), w_mix_out.astype(BF16),
        w_spatial, bsp_full, ffn_norm_g.reshape(1, d), wr_pad, br_pad)

    n_assign = t * EXPERT_TOPK
    n_tiles = -(-(n_assign + N_EXPERTS * (EXPERT_ROWS - 1)) // EXPERT_ROWS)
    n_rows = n_tiles * EXPERT_ROWS
    run_len = tile_cnt[::SUBLANES, :N_EXPERTS].astype(jnp.int32)
    counts = jnp.sum(run_len, axis=0)
    padded = (counts + EXPERT_ROWS - 1) // EXPERT_ROWS * EXPERT_ROWS
    pad_end = jnp.cumsum(padded)
    starts = pad_end - padded
    run_tpos = jnp.cumsum(run_len, axis=1) - run_len
    run_row = starts[None, :] + jnp.cumsum(run_len, axis=0) - run_len
    tile_start = jnp.arange(n_tiles, dtype=jnp.int32) * EXPERT_ROWS
    block_expert = jnp.minimum(
        jnp.sum((pad_end[None, :] <= tile_start[:, None]).astype(jnp.int32), axis=1), N_EXPERTS - 1)
    n_used = (pad_end[-1:] // EXPERT_ROWS).astype(jnp.int32)
    pos = ri[:, EXPERT_TOPK:2 * EXPERT_TOPK].reshape(n_assign)
    tables = (run_len.reshape(-1), run_tpos.reshape(-1), run_row.reshape(-1))

    xs_rows = _dispatch(*tables, starts + counts, padded - counts, n_used, pos_t, xn2, n_rows)

    de = w_down.shape[1]
    bg = b_gate_up[:, 0::2].reshape(N_EXPERTS, 1, de)
    bu = b_gate_up[:, 1::2].reshape(N_EXPERTS, 1, de)
    ys_rows = _experts(block_expert, n_used, xs_rows, w_gate_up, bg, bu, w_down, b_down.reshape(N_EXPERTS, 1, d))

    out = _combine(*tables, pos, rw[:, :EXPERT_TOPK].reshape(n_assign), h1, out_g.reshape(1, d), ys_rows)
    return out.reshape(b, s, d)


def kernel(x, mix_norm_g, w_in, w_attn_out, sgu_ln_g, sgu_ln_b, w_spatial, b_spatial, w_sgu_out, w_mix_out,
           ffn_norm_g, w_router, b_router, w_gate_up, b_gate_up, w_down, b_down, final_norm_g):
    depth = w_in.shape[0]
    assert depth == 1, "the final RMSNorm is fused into the single layer's combine step"
    return _layer(x, mix_norm_g[0], w_in[0], w_attn_out[0], sgu_ln_g[0], sgu_ln_b[0], w_spatial[0],
                  b_spatial[0], w_sgu_out[0], w_mix_out[0], ffn_norm_g[0], w_router[0], b_router[0],
                  w_gate_up[0], b_gate_up[0], w_down[0], b_down[0], final_norm_g)
```

```python
import jax
import jax.numpy as jnp
import numpy as np
from jax import lax
from jax.experimental import pallas as pl
from jax.experimental.pallas import tpu as pltpu

F32 = jnp.float32
BF16 = jnp.bfloat16
NEG_INF = float("-inf")
MASK_VALUE = -1e30


def _bf16_pieces(x, n=3):
    pieces = []
    for _ in range(n):
        p = float(np.asarray(x, np.float32).astype(BF16).astype(np.float32))
        pieces.append(p)
        x = x - p
    return pieces


LOG2E = 1.4426950408889634
LOG2E_PIECES = _bf16_pieces(LOG2E)

N_HEADS = 8
HEAD_DIM = 64
ATTN_WIDTH = N_HEADS * HEAD_DIM
MOBA_BLOCK = 256
MOBA_TOPK = 3
SGU_CHUNK = 128
SGU_GROUPS = 8
N_EXPERTS = 32
EXPERT_TOPK = 4
SWIGLU_LIMIT = 7.0
SWIGLU_ALPHA = 1.702
NORM_EPS = 1e-5

LANES = 128
SUBLANES = 8
PACKED_ROWS = 4
HEADS_PER_LANE_TILE = LANES // HEAD_DIM
MOBA_GROUP = 1
MOBA_BIAS_LANE0 = 64
MOBA_DUMMY_LANE = LANES - 1
PROJ_ROWS = 256
EXPERT_ROWS = 720
ROUTE_TOKENS = 512
VMEM_LIMIT = 48 * 1024 * 1024
BIG_VMEM_LIMIT = 56 * 1024 * 1024


def _sigmoid(x):
    return 1.0 / (1.0 + jnp.exp(-x))


def _store_token_rows(ref, value, base=0):
    n = value.shape[0]
    for c in range(value.shape[1] // LANES):
        ref[pl.ds(base + c, n, stride=SUBLANES), :] = value[:, c * LANES:(c + 1) * LANES]


def _load_token_rows(ref, n, base=0):
    return jnp.concatenate([ref[pl.ds(base + c, n, stride=SUBLANES), :] for c in range(SUBLANES)], axis=1)


def _gelu_exact(x):
    return 0.5 * x * (1.0 + lax.erf(x * (0.5 ** 0.5)))


def _in_proj_kernel(x_ref, g_ref, w_ref, lng_ref, lnb_ref,
                    q_ref, qt_ref, kt_ref, v_ref, kmean_ref, u_ref, vvn_ref, ga_ref, gs_ref):
    x = x_ref[...]
    xn = x * lax.rsqrt(jnp.mean(x * x, axis=-1, keepdims=True) + NORM_EPS) * g_ref[...]
    xb = xn.astype(BF16)

    def proj(lo, hi):
        return jnp.dot(xb, w_ref[:, lo:hi], preferred_element_type=F32)

    a = ATTN_WIDTH
    sw = u_ref.shape[1]
    d = ga_ref.shape[1]
    qf = proj(0, a)
    q_ref[...] = qf
    kf = proj(a, 2 * a)
    for j in range(kf.shape[0] // MOBA_BLOCK):
        qt_ref[j] = qf[j * MOBA_BLOCK:(j + 1) * MOBA_BLOCK].T
        kblk = kf[j * MOBA_BLOCK:(j + 1) * MOBA_BLOCK]
        kt_ref[j] = kblk.T.astype(BF16)
        kmean_ref[j] = jnp.mean(kblk, axis=0, keepdims=True)
    v_ref[...] = proj(2 * a, 3 * a).astype(BF16)
    z0 = 3 * a
    u_ref[...] = _gelu_exact(proj(z0, z0 + sw))
    zv = _gelu_exact(proj(z0 + sw, z0 + 2 * sw))
    mu = jnp.mean(zv, axis=-1, keepdims=True)
    zc = zv - mu
    var = jnp.mean(zc * zc, axis=-1, keepdims=True)
    vvn_ref[...] = zc * lax.rsqrt(var + NORM_EPS) * lng_ref[...] + lnb_ref[...]
    g0 = z0 + 2 * sw
    ga_ref[...] = _sigmoid(proj(g0, g0 + d))
    gs_ref[...] = _sigmoid(proj(g0 + d, g0 + 2 * d))


def _in_proj(x2, g, w_bf, lng, lnb, sgu_width):
    t, d = x2.shape
    tm = PROJ_ROWS
    n_in = w_bf.shape[1]
    row = lambda w: pl.BlockSpec((tm, w), lambda i: (i, 0))
    const = lambda shape: pl.BlockSpec(shape, lambda i: (0,) * len(shape))
    out_shape = (
        jax.ShapeDtypeStruct((t, ATTN_WIDTH), F32),
        jax.ShapeDtypeStruct((t // MOBA_BLOCK, ATTN_WIDTH, MOBA_BLOCK), F32),
        jax.ShapeDtypeStruct((t // MOBA_BLOCK, ATTN_WIDTH, MOBA_BLOCK), BF16),
        jax.ShapeDtypeStruct((t, ATTN_WIDTH), BF16),
        jax.ShapeDtypeStruct((t // MOBA_BLOCK, 1, ATTN_WIDTH), F32),
        jax.ShapeDtypeStruct((t, sgu_width), F32),
        jax.ShapeDtypeStruct((t, sgu_width), F32),
        jax.ShapeDtypeStruct((t, d), F32),
        jax.ShapeDtypeStruct((t, d), F32),
    )
    out_specs = (
        row(ATTN_WIDTH),
        pl.BlockSpec((tm // MOBA_BLOCK, ATTN_WIDTH, MOBA_BLOCK), lambda i: (i, 0, 0)),
        pl.BlockSpec((tm // MOBA_BLOCK, ATTN_WIDTH, MOBA_BLOCK), lambda i: (i, 0, 0)),
        row(ATTN_WIDTH),
        pl.BlockSpec((tm // MOBA_BLOCK, 1, ATTN_WIDTH), lambda i: (i, 0, 0)),
        row(sgu_width), row(sgu_width), row(d), row(d),
    )
    return pl.pallas_call(
        _in_proj_kernel,
        grid=(t // tm,),
        in_specs=[row(d), const((1, d)), const((d, n_in)), const((1, sgu_width)), const((1, sgu_width))],
        out_specs=out_specs,
        out_shape=out_shape,
        compiler_params=pltpu.CompilerParams(dimension_semantics=("arbitrary",), vmem_limit_bytes=VMEM_LIMIT),
        name="in_proj",
    )(x2, g, w_bf, lng, lnb)


def _alibi_slopes():
    slopes = 2.0 ** (-8.0 * np.arange(1, N_HEADS + 1, dtype=np.float64) / N_HEADS)
    assert all(np.log2(s) == np.round(np.log2(s)) for s in slopes), "ALiBi slopes must be powers of two"
    return slopes


def _moba_bias_lane(head_in_tile, part, piece):
    return MOBA_BIAS_LANE0 + (head_in_tile * 2 + part) * len(LOG2E_PIECES) + piece


def _moba_key_table(s):
    nb = s // MOBA_BLOCK
    slopes = _alibi_slopes()
    n_pairs = N_HEADS // HEADS_PER_LANE_TILE
    table = np.zeros((nb + 1, n_pairs, LANES, MOBA_BLOCK), np.float32)
    offs = np.arange(MOBA_BLOCK, dtype=np.float32)
    assert HEADS_PER_LANE_TILE * nb <= MOBA_BIAS_LANE0, "block one-hot rows must not overlap the ALiBi rows"
    for j in range(nb):
        for hh in range(HEADS_PER_LANE_TILE):
            table[j, :, hh * nb + j, :] = 1.0
        for p in range(n_pairs):
            for hh in range(HEADS_PER_LANE_TILE):
                slope = slopes[p * HEADS_PER_LANE_TILE + hh]
                for piece in range(len(LOG2E_PIECES)):
                    table[j, p, _moba_bias_lane(hh, 0, piece), :] = slope * MOBA_BLOCK * j
                    table[j, p, _moba_bias_lane(hh, 1, piece), :] = slope * offs
    table[nb, :, MOBA_DUMMY_LANE, :] = 1.0
    as_bf16 = table.astype(BF16)
    assert np.array_equal(as_bf16.astype(np.float32), table), "bias table must be exact in bf16"
    return jnp.asarray(as_bf16.reshape(nb + 1, n_pairs * LANES, MOBA_BLOCK))


def _moba_kernel(q_ref, qt_ref, qn_ref, qtn_ref, kt_ref, v_ref, km_ref, ct_ref, causal_ref, o_ref, qaug_ref,
                 s_ref, mpart_ref, mrow_ref, acc_ref):
    i = pl.program_id(2)
    nb = kt_ref.shape[0]
    blk = MOBA_BLOCK
    lane = lax.broadcasted_iota(jnp.int32, (1, LANES), 1)
    n_groups = lax.div(i + MOBA_GROUP, MOBA_GROUP)
    heads = range(HEADS_PER_LANE_TILE)
    hmasks = [(lane >= HEAD_DIM * hh) & (lane < HEAD_DIM * (hh + 1)) for hh in heads]
    slot = lax.rem(i, 2)

    def build_query_operand(q, qt, tile, into):
        bid = lax.broadcasted_iota(jnp.int32, (nb, blk), 0)
        bid_f = bid.astype(F32)
        masks_t = []
        for hh in heads:
            km_h = jnp.where(hmasks[hh], km_ref[...], 0.0)
            gate = jnp.dot(km_h, qt, precision=lax.Precision.HIGHEST, preferred_element_type=F32)
            gate = jnp.where(bid < tile, gate, NEG_INF)
            blockmask = jnp.where(bid == tile, 0.0, MASK_VALUE)
            for kk in range(MOBA_TOPK):
                gmax = jnp.max(gate, axis=0, keepdims=True)
                first = jnp.min(jnp.where(gate == gmax, bid_f, float(nb)), axis=0, keepdims=True)
                valid = (tile > kk).astype(F32)
                first = first * valid + (valid - 1.0)
                hit = bid_f == first
                blockmask = jnp.where(hit, 0.0, blockmask)
                gate = jnp.where(hit, NEG_INF, gate)
            masks_t.append(blockmask)
        masks_t.append(jnp.zeros((LANES - len(masks_t) * nb, blk), F32))
        masks = jnp.concatenate(masks_t, axis=0).T

        for hh in heads:
            feats = jnp.where(lane == MOBA_DUMMY_LANE, MASK_VALUE, 0.0)
            for part in range(2):
                for piece, value in enumerate(LOG2E_PIECES):
                    feats = jnp.where(lane == _moba_bias_lane(hh, part, piece), value, feats)
            extra = jnp.where((lane >= hh * nb) & (lane < (hh + 1) * nb), masks, feats)
            qh = jnp.where(hmasks[hh], q, 0.0)
            qaug_ref[into, hh * blk:(hh + 1) * blk, 0:LANES] = (qh * (HEAD_DIM ** -0.5 * LOG2E)).astype(BF16)
            qaug_ref[into, hh * blk:(hh + 1) * blk, LANES:2 * LANES] = extra.astype(BF16)

    @pl.when(i == 0)
    def _():
        build_query_operand(q_ref[...], qt_ref[...], i, slot)

    def scores(jk, jc):
        rhs = jnp.concatenate([kt_ref[jk], ct_ref[jc]], axis=0)
        return jnp.dot(qaug_ref[slot], rhs, preferred_element_type=F32)

    def lane_halves_max(s):
        return jnp.maximum(s[:, 0:LANES], s[:, LANES:2 * LANES])

    n_rows = len(heads) * blk
    variants = [(groups, min(groups * MOBA_GROUP, nb)) for groups in range(1, pl.cdiv(nb, MOBA_GROUP) + 1)]

    for groups, n_blocks in variants:
        @pl.when(n_groups == groups)
        def _(n_blocks=n_blocks):
            part = None
            for j in range(n_blocks):
                s = scores(j, jnp.where(j <= i, j, nb)) + causal_ref[(i == j).astype(jnp.int32)]
                s_ref[j] = s
                part = lane_halves_max(s) if part is None else jnp.maximum(part, lane_halves_max(s))
            mpart_ref[...] = part

    ones = jnp.ones((blk, LANES), BF16)
    rowmax = jnp.max(mpart_ref[...], axis=1, keepdims=True)
    mrow_ref[...] = jnp.broadcast_to(rowmax, (n_rows, 2 * LANES))

    def weighted(j):
        p = jnp.exp2(s_ref[j] - mrow_ref[...]).astype(BF16)
        vaug = jnp.concatenate([v_ref[j * blk:(j + 1) * blk, :], ones], axis=1)
        return jnp.dot(p, vaug, preferred_element_type=F32)

    next_tile = jnp.minimum(i + 1, nb - 1)
    for groups, n_blocks in variants:
        @pl.when(n_groups == groups)
        def _(n_blocks=n_blocks):
            build_query_operand(qn_ref[...], qtn_ref[...], next_tile, 1 - slot)
            tot = weighted(0)
            for j in range(1, n_blocks):
                tot = tot + weighted(j)
            acc_ref[...] = tot

    out = jnp.zeros((blk, LANES), F32)
    for hh in heads:
        acc = acc_ref[hh * blk:(hh + 1) * blk, :]
        out = jnp.where(hmasks[hh], acc[:, 0:LANES] / acc[:, LANES:2 * LANES], out)
    o_ref[...] = out


def _moba(q, qt, kt, v, kmean, key_table):
    b, s, a = q.shape
    nb = s // MOBA_BLOCK
    n_pairs = a // LANES
    rows = HEADS_PER_LANE_TILE * MOBA_BLOCK
    row_in_block = np.arange(rows)[:, None] % MOBA_BLOCK
    causal = np.where(row_in_block >= np.arange(MOBA_BLOCK)[None, :], 0.0, MASK_VALUE).astype(np.float32)
    causal_tiles = jnp.asarray(np.stack([np.zeros_like(causal), causal]))
    nxt = lambda i: jnp.minimum(i + 1, nb - 1)
    return pl.pallas_call(
        _moba_kernel,
        grid=(b, n_pairs, nb),
        in_specs=[
            pl.BlockSpec((None, MOBA_BLOCK, LANES), lambda bi, p, i: (bi, i, p)),
            pl.BlockSpec((None, LANES, MOBA_BLOCK), lambda bi, p, i: (bi * nb + i, p, 0)),
            pl.BlockSpec((None, MOBA_BLOCK, LANES), lambda bi, p, i: (bi, nxt(i), p)),
            pl.BlockSpec((None, LANES, MOBA_BLOCK), lambda bi, p, i: (bi * nb + nxt(i), p, 0)),
            pl.BlockSpec((nb, LANES, MOBA_BLOCK), lambda bi, p, i: (bi, p, 0)),
            pl.BlockSpec((None, s, LANES), lambda bi, p, i: (bi, 0, p)),
            pl.BlockSpec((None, nb, LANES), lambda bi, p, i: (bi, 0, p)),
            pl.BlockSpec((nb + 1, LANES, MOBA_BLOCK), lambda bi, p, i: (0, p, 0)),
            pl.BlockSpec((2, rows, MOBA_BLOCK), lambda bi, p, i: (0, 0, 0)),
        ],
        out_specs=pl.BlockSpec((None, MOBA_BLOCK, LANES), lambda bi, p, i: (bi, i, p)),
        out_shape=jax.ShapeDtypeStruct((b, s, a), F32),
        scratch_shapes=[
            pltpu.VMEM((2, rows, 2 * LANES), BF16),
            pltpu.VMEM((nb + 1, rows, MOBA_BLOCK), F32),
            pltpu.VMEM((rows, LANES), F32),
            pltpu.VMEM((rows, 2 * LANES), F32),
            pltpu.VMEM((rows, 2 * LANES), F32),
        ],
        compiler_params=pltpu.CompilerParams(
            dimension_semantics=("arbitrary", "arbitrary", "arbitrary"), vmem_limit_bytes=VMEM_LIMIT),
        name="moba",
    )(q, qt, q, qt, kt, v, kmean, key_table, causal_tiles)


def _mix_kernel(attn_ref, u_ref, vvn_ref, ga_ref, gs_ref, x_ref,
                wao_ref, wso_ref, wmo_ref, wsp_ref, bsp_ref, fg_ref, wr_ref, br_ref,
                h1_ref, xn2_ref, ri_ref, rw_ref, tcnt_ref, post_ref, wcausal_ref):
    tm = x_ref.shape[0]
    ch = SGU_CHUNK

    @pl.when(pl.program_id(0) == 0)
    def _():
        tril = lax.broadcasted_iota(jnp.int32, (ch, ch), 0) >= lax.broadcasted_iota(jnp.int32, (ch, ch), 1)
        for g in range(SGU_GROUPS):
            wcausal_ref[g] = jnp.where(tril, wsp_ref[g], 0.0).astype(BF16)

    y_attn = jnp.dot(attn_ref[...].astype(BF16), wao_ref[...], preferred_element_type=F32)

    lane = lax.broadcasted_iota(jnp.int32, (1, LANES), 1)
    gdim = vvn_ref.shape[1] // SGU_GROUPS
    groups_per_tile = LANES // gdim
    w_causal = [wcausal_ref[g] for g in range(SGU_GROUPS)]
    rows = []
    for c in range(tm // ch):
        cols = []
        for ct in range(vvn_ref.shape[1] // LANES):
            vp = vvn_ref[c * ch:(c + 1) * ch, ct * LANES:(ct + 1) * LANES]
            acc = jnp.zeros((ch, LANES), F32)
            for gg in range(groups_per_tile):
                gmask = (lane >= gdim * gg) & (lane < gdim * (gg + 1))
                vm = jnp.where(gmask, vp, 0.0).astype(BF16)
                acc = acc + jnp.dot(w_causal[ct * groups_per_tile + gg], vm, preferred_element_type=F32)
            cols.append(acc)
        rows.append(jnp.concatenate(cols, axis=1) + bsp_ref[...])
    mixed = jnp.concatenate(rows, axis=0)
    sgu = u_ref[...] * mixed
    y_sgu = jnp.dot(sgu.astype(BF16), wso_ref[...], preferred_element_type=F32)

    merged = ga_ref[...] * y_attn + gs_ref[...] * y_sgu
    h1 = x_ref[...] + jnp.dot(merged.astype(BF16), wmo_ref[...], preferred_element_type=F32)
    h1_ref[...] = h1
    xn2 = h1 * lax.rsqrt(jnp.mean(h1 * h1, axis=-1, keepdims=True) + NORM_EPS) * fg_ref[...]
    xn2_ref[...] = xn2.astype(BF16)

    def split(a):
        hi = a.astype(BF16)
        return hi, (a - hi.astype(F32)).astype(BF16)

    x_hi, x_lo = split(xn2)
    w_hi, w_lo = split(wr_ref[...])
    logits = (jnp.dot(x_hi, w_hi, preferred_element_type=F32) + jnp.dot(x_lo, w_hi, preferred_element_type=F32)
              + jnp.dot(x_hi, w_lo, preferred_element_type=F32)) + br_ref[...]
    work = jnp.where(lane < N_EXPERTS, logits, NEG_INF)
    lane_f = lane.astype(F32)
    vals, idxs = [], []
    for _ in range(EXPERT_TOPK):
        vmax = jnp.max(work, axis=1, keepdims=True)
        first = jnp.min(jnp.where(work == vmax, lane_f, float(LANES)), axis=1, keepdims=True)
        vals.append(vmax)
        idxs.append(first)
        work = jnp.where(lane_f == first, NEG_INF, work)
    exps = [jnp.exp(v - vals[0]) for v in vals]
    denom = exps[0]
    for e in exps[1:]:
        denom = denom + e
    chosen = jnp.zeros((tm, LANES), F32)
    for first in idxs:
        chosen = jnp.where(lane_f == first, 1.0, chosen)

    strict_lower = (lax.broadcasted_iota(jnp.int32, (tm, tm), 0)
                    > lax.broadcasted_iota(jnp.int32, (tm, tm), 1))
    before = jnp.dot(jnp.where(strict_lower, 1.0, 0.0).astype(BF16), chosen.astype(BF16),
                     preferred_element_type=F32)
    counts = jnp.broadcast_to(jnp.sum(chosen, axis=0, keepdims=True), tcnt_ref.shape)
    lower_expert = (lax.broadcasted_iota(jnp.int32, (LANES, LANES), 0)
                    < lax.broadcasted_iota(jnp.int32, (LANES, LANES), 1))
    expert_start = jnp.dot(counts, jnp.where(lower_expert, 1.0, 0.0), precision=lax.Precision.HIGHEST,
                           preferred_element_type=F32)[0:1, :]
    sorted_pos = before + expert_start
    ri = jnp.zeros((tm, LANES), jnp.int32)
    rw = jnp.zeros((tm, LANES), F32)
    pos_lanes = jnp.zeros((tm, LANES), F32)
    for kk in range(EXPERT_TOPK):
        pos = jnp.sum(jnp.where(lane_f == idxs[kk], sorted_pos, 0.0), axis=1, keepdims=True)
        ri = jnp.where(lane == kk, idxs[kk].astype(jnp.int32), ri)
        ri = jnp.where(lane == EXPERT_TOPK + kk, pos.astype(jnp.int32), ri)
        rw = jnp.where(lane == kk, exps[kk] / denom, rw)
        pos_lanes = jnp.where(lane == kk, pos, pos_lanes)
    ri_ref[...] = ri
    rw_ref[...] = rw
    tcnt_ref[...] = counts
    post_ref[...] = pos_lanes.T[0:SUBLANES, :]


def _mix(attn, u, vvn, ga, gs, x2, wao, wso, wmo, wsp, bsp_full, fg, wr_pad, br_pad):
    t, d = x2.shape
    tm = ROUTE_TOKENS
    row = lambda w: pl.BlockSpec((tm, w), lambda i: (i, 0))
    const = lambda shape: pl.BlockSpec(shape, lambda i: (0,) * len(shape))
    ins = (attn, u, vvn, ga, gs, x2, wao, wso, wmo, wsp, bsp_full, fg, wr_pad, br_pad)
    in_specs = [row(a.shape[1]) for a in ins[:6]] + [const(a.shape) for a in ins[6:]]
    out_shape = (
        jax.ShapeDtypeStruct((t, d), F32),
        jax.ShapeDtypeStruct((t, d), BF16),
        jax.ShapeDtypeStruct((t, LANES), jnp.int32),
        jax.ShapeDtypeStruct((t, LANES), F32),
        jax.ShapeDtypeStruct((t // tm * SUBLANES, LANES), F32),
        jax.ShapeDtypeStruct((t // tm * SUBLANES, tm), F32),
    )
    assert d == SUBLANES * LANES, "a token row is stored as one (8, 128) tile"
    per_tile = lambda w: pl.BlockSpec((SUBLANES, w), lambda i: (i, 0))
    out_specs = (row(d), row(d), row(LANES), row(LANES), per_tile(LANES), per_tile(tm))
    return pl.pallas_call(
        _mix_kernel,
        grid=(t // tm,),
        in_specs=in_specs,
        out_specs=out_specs,
        out_shape=out_shape,
        scratch_shapes=[pltpu.VMEM(wsp.shape, BF16)],
        compiler_params=pltpu.CompilerParams(dimension_semantics=("arbitrary",), vmem_limit_bytes=BIG_VMEM_LIMIT),
        name="mix_route",
    )(*ins)


def _run_pieces(n, body):
    off = jnp.int32(0)
    for bit in reversed(range(ROUTE_TOKENS.bit_length())):
        size = 1 << bit
        take = lax.bitwise_and(n, size) != 0

        @pl.when(take)
        def _(off=off, size=size):
            body(off, size)

        off = off + jnp.where(take, size, 0)


def _token_rows(ref, r, n, rows_per_token=SUBLANES):
    return ref.at[pl.ds(pl.multiple_of(r * rows_per_token, rows_per_token), n * rows_per_token)]


def _pack_bf16_pairs(value):
    half = value.shape[1] // 2
    lo = lax.shift_right_logical(lax.bitcast_convert_type(value[:, :half], jnp.uint32), jnp.uint32(16))
    hi = lax.bitwise_and(lax.bitcast_convert_type(value[:, half:], jnp.uint32), jnp.uint32(0xFFFF0000))
    return lax.bitwise_or(lo, hi)


def _unpack_bf16_pairs(words):
    lo = lax.bitcast_convert_type(lax.shift_left(words, jnp.uint32(16)), F32)
    hi = lax.bitcast_convert_type(lax.bitwise_and(words, jnp.uint32(0xFFFF0000)), F32)
    return jnp.concatenate([lo, hi], axis=1).astype(BF16)


def _dispatch_kernel(len_ref, tpos_ref, dst_ref, padlo_ref, padlen_ref, nused_ref, post_ref, x_ref,
                     xs_hbm, sorted_ref, zeros_ref, sems, zsem):
    step = pl.program_id(0)
    n_steps = pl.num_programs(0)
    tt = x_ref.shape[0]
    n_assign = tt * EXPERT_TOPK
    rpt = PACKED_ROWS
    tile_rows = zeros_ref.shape[0] // rpt
    slot = lax.rem(step, 2)

    def start_runs(tile, buf):
        def per_expert(e, carry):
            k = tile * N_EXPERTS + e

            def piece(off, size):
                pltpu.make_async_copy(_token_rows(sorted_ref.at[buf], tpos_ref[k] + off, size, rpt),
                                      _token_rows(xs_hbm, dst_ref[k] + off, size, rpt), sems.at[buf]).start()

            _run_pieces(len_ref[k], piece)
            return carry

        lax.fori_loop(0, N_EXPERTS, per_expert, 0)

    def wait_runs(buf):
        whole = sorted_ref.at[buf]
        pltpu.make_async_copy(whole, _token_rows(xs_hbm, 0, n_assign, rpt), sems.at[buf]).wait()

    @pl.when(step == 0)
    def _():
        zeros_ref[...] = jnp.zeros_like(zeros_ref)

        def tail_tile(tl, carry):
            cp = pltpu.make_async_copy(zeros_ref, _token_rows(xs_hbm, tl * tile_rows, tile_rows, rpt), zsem)
            cp.start()
            cp.wait()
            return carry

        lax.fori_loop(nused_ref[0], xs_hbm.shape[0] // (tile_rows * rpt), tail_tile, 0)

        def per_expert(e, carry):
            def piece(off, size):
                cp = pltpu.make_async_copy(_token_rows(zeros_ref, 0, size, rpt),
                                           _token_rows(xs_hbm, padlo_ref[e] + off, size, rpt), zsem)
                cp.start()
                cp.wait()

            _run_pieces(padlen_ref[e], piece)
            return carry

        lax.fori_loop(0, N_EXPERTS, per_expert, 0)

    @pl.when(step >= 2)
    def _():
        wait_runs(slot)

    x = x_ref[...]
    pos_by_choice = [post_ref[kk:kk + 1, :] for kk in range(EXPERT_TOPK)]
    chunk = MOBA_BLOCK
    for rc in range(n_assign // chunk):
        r = (lax.broadcasted_iota(jnp.int32, (chunk, 1), 0) + rc * chunk).astype(F32)
        select = jnp.zeros((chunk, tt), F32)
        for pos in pos_by_choice:
            select = jnp.where(r == pos, 1.0, select)
        rows = jnp.dot(select.astype(BF16), x, preferred_element_type=F32)
        words = _pack_bf16_pairs(rows)
        for c in range(rpt):
            sorted_ref[slot, pl.ds(rc * chunk * rpt + c, chunk, stride=rpt), :] = words[:, c * LANES:(c + 1) * LANES]
    start_runs(step, slot)

    @pl.when(step == n_steps - 1)
    def _():
        @pl.when(step >= 1)
        def _():
            wait_runs(1 - slot)

        wait_runs(slot)


def _dispatch(run_len, run_tpos, run_dst, padlo, padlen, n_used, pos_t, xn2, n_rows):
    t, d = xn2.shape
    n_assign = ROUTE_TOKENS * EXPERT_TOPK
    grid_spec = pltpu.PrefetchScalarGridSpec(
        num_scalar_prefetch=6,
        grid=(t // ROUTE_TOKENS,),
        in_specs=[pl.BlockSpec((SUBLANES, ROUTE_TOKENS), lambda c, *_: (c, 0)),
                  pl.BlockSpec((ROUTE_TOKENS, d), lambda c, *_: (c, 0))],
        out_specs=pl.BlockSpec(memory_space=pl.ANY),
        scratch_shapes=[
            pltpu.VMEM((2, n_assign * PACKED_ROWS, LANES), jnp.uint32),
            pltpu.VMEM((EXPERT_ROWS * PACKED_ROWS, LANES), jnp.uint32),
            pltpu.SemaphoreType.DMA((2,)),
            pltpu.SemaphoreType.DMA(()),
        ],
    )
    assert d == 2 * PACKED_ROWS * LANES
    return pl.pallas_call(
        _dispatch_kernel,
        grid_spec=grid_spec,
        out_shape=jax.ShapeDtypeStruct((n_rows * PACKED_ROWS, LANES), jnp.uint32),
        compiler_params=pltpu.CompilerParams(dimension_semantics=("arbitrary",), vmem_limit_bytes=VMEM_LIMIT),
        name="dispatch",
    )(run_len, run_tpos, run_dst, padlo, padlen, n_used, pos_t, xn2)


def _expert_kernel(be_ref, nused_ref, x_ref, wgu_ref, bg_ref, bu_ref, wd_ref, bd_ref, y_ref, wgu_bf, wd_bf):
    t = pl.program_id(0)
    rows = y_ref.shape[0] // SUBLANES
    de = wd_ref.shape[0]
    grp = 2 * LANES
    active = t < nused_ref[0]
    fresh = jnp.logical_and(
        active, jnp.logical_or(t == 0, be_ref[t] != be_ref[jnp.maximum(t - 1, 0)]))

    @pl.when(fresh)
    def _():
        src = lax.broadcasted_iota(jnp.int32, (grp, grp), 0)
        dst = lax.broadcasted_iota(jnp.int32, (grp, grp), 1)
        wanted = jnp.where(dst < LANES, 2 * dst, 2 * (dst - LANES) + 1)
        perm = jnp.where(src == wanted, 1.0, 0.0).astype(BF16)
        for g in range(wgu_ref.shape[1] // grp):
            cols = slice(g * grp, (g + 1) * grp)
            wgu_bf[:, cols] = jnp.dot(wgu_ref[:, cols].astype(BF16), perm,
                                      preferred_element_type=F32).astype(BF16)
        wd_bf[...] = wd_ref[...].astype(BF16)

    @pl.when(active)
    def _():
        words = jnp.concatenate([x_ref[pl.ds(c, rows, stride=PACKED_ROWS), :] for c in range(PACKED_ROWS)], axis=1)
        xb = _unpack_bf16_pairs(words)
        gu = jnp.dot(xb, wgu_bf[...], preferred_element_type=F32)
        hid = []
        for g in range(de // LANES):
            cols = slice(g * LANES, (g + 1) * LANES)
            gate = gu[:, g * grp:g * grp + LANES] + bg_ref[:, cols]
            up = gu[:, g * grp + LANES:(g + 1) * grp] + bu_ref[:, cols]
            gate = jnp.minimum(gate, SWIGLU_LIMIT)
            up = jnp.clip(up, -SWIGLU_LIMIT, SWIGLU_LIMIT)
            hid.append((gate * _sigmoid(SWIGLU_ALPHA * gate) * (up + 1.0)).astype(BF16))
        y = jnp.dot(jnp.concatenate(hid, axis=1), wd_bf[...], preferred_element_type=F32) + bd_ref[...]
        _store_token_rows(y_ref, y)

    @pl.when(jnp.logical_not(active))
    def _():
        y_ref[...] = jnp.zeros_like(y_ref)


def _experts(block_expert, n_used, xs_rows, wgu, bg, bu, wd, bd):
    n_rows = xs_rows.shape[0] // PACKED_ROWS
    _, d, de2 = wgu.shape
    de = wd.shape[1]
    n_tiles = n_rows // EXPERT_ROWS
    tile_rows = EXPERT_ROWS * SUBLANES
    per_expert = lambda t, be, nu: (be[t], 0, 0)
    grid_spec = pltpu.PrefetchScalarGridSpec(
        num_scalar_prefetch=2,
        grid=(n_tiles,),
        in_specs=[
            pl.BlockSpec((EXPERT_ROWS * PACKED_ROWS, LANES),
                         lambda t, be, nu: (jnp.minimum(t, jnp.maximum(nu[0] - 1, 0)), 0)),
            pl.BlockSpec((None, d, de2), per_expert),
            pl.BlockSpec((None, 1, de), per_expert),
            pl.BlockSpec((None, 1, de), per_expert),
            pl.BlockSpec((None, de, d), per_expert),
            pl.BlockSpec((None, 1, d), per_expert),
        ],
        out_specs=pl.BlockSpec((tile_rows, LANES), lambda t, be, nu: (t, 0)),
        scratch_shapes=[pltpu.VMEM((d, de2), BF16), pltpu.VMEM((de, d), BF16)],
    )
    return pl.pallas_call(
        _expert_kernel,
        grid_spec=grid_spec,
        out_shape=jax.ShapeDtypeStruct((n_rows * SUBLANES, LANES), F32),
        compiler_params=pltpu.CompilerParams(
            dimension_semantics=("arbitrary",), vmem_limit_bytes=BIG_VMEM_LIMIT),
        name="experts",
    )(block_expert, n_used, xs_rows, wgu, bg, bu, wd, bd)


def _combine_kernel(len_ref, tpos_ref, src_ref, row_ref, w_ref, h1_ref, fg_ref, ys_hbm,
                    o_ref, buf_ref, h1rows_ref, outrows_ref, sems):
    step = pl.program_id(0)
    n_steps = pl.num_programs(0)
    tt = h1_ref.shape[0]
    slot = lax.rem(step, 2)

    def start_runs(tile, buf):
        def per_expert(e, carry):
            k = tile * N_EXPERTS + e

            def piece(off, size):
                pltpu.make_async_copy(_token_rows(ys_hbm, src_ref[k] + off, size),
                                      _token_rows(buf_ref.at[buf], tpos_ref[k] + off, size), sems.at[buf]).start()

            _run_pieces(len_ref[k], piece)
            return carry

        lax.fori_loop(0, N_EXPERTS, per_expert, 0)

    def wait_runs(buf):
        whole = buf_ref.at[buf]
        pltpu.make_async_copy(_token_rows(ys_hbm, 0, tt * EXPERT_TOPK), whole, sems.at[buf]).wait()

    @pl.when(step == 0)
    def _():
        start_runs(0, 0)

    @pl.when(step + 1 < n_steps)
    def _():
        start_runs(step + 1, 1 - slot)

    wait_runs(slot)

    _store_token_rows(h1rows_ref, h1_ref[...])

    def one_token(tok, carry):
        acc = h1rows_ref[pl.ds(pl.multiple_of(tok * SUBLANES, SUBLANES), SUBLANES), :]
        for kk in range(EXPERT_TOPK):
            a = tok * EXPERT_TOPK + kk
            acc = acc + w_ref[a] * buf_ref[slot, pl.ds(pl.multiple_of(row_ref[a], SUBLANES), SUBLANES), :]
        outrows_ref[pl.ds(pl.multiple_of(tok * SUBLANES, SUBLANES), SUBLANES), :] = acc
        return carry

    lax.fori_loop(0, tt, one_token, 0, unroll=16)
    h = _load_token_rows(outrows_ref, tt)
    o_ref[...] = h * lax.rsqrt(jnp.mean(h * h, axis=-1, keepdims=True) + NORM_EPS) * fg_ref[...]


def _combine(run_len, run_tpos, run_src, buf_row, w_flat, h1, fg, ys_rows):
    t, d = h1.shape
    tt = ROUTE_TOKENS
    n_assign = tt * EXPERT_TOPK
    smem_blk = pl.BlockSpec((n_assign,), lambda c, *_: (c,), memory_space=pltpu.SMEM)
    grid_spec = pltpu.PrefetchScalarGridSpec(
        num_scalar_prefetch=3,
        grid=(t // tt,),
        in_specs=[
            smem_blk, smem_blk,
            pl.BlockSpec((tt, d), lambda c, *_: (c, 0)),
            pl.BlockSpec((1, d), lambda c, *_: (0, 0)),
            pl.BlockSpec(memory_space=pl.ANY),
        ],
        out_specs=pl.BlockSpec((tt, d), lambda c, *_: (c, 0)),
        scratch_shapes=[
            pltpu.VMEM((2, n_assign * SUBLANES, LANES), F32),
            pltpu.VMEM((tt * SUBLANES, LANES), F32),
            pltpu.VMEM((tt * SUBLANES, LANES), F32),
            pltpu.SemaphoreType.DMA((2,)),
        ],
    )
    return pl.pallas_call(
        _combine_kernel,
        grid_spec=grid_spec,
        out_shape=jax.ShapeDtypeStruct((t, d), F32),
        compiler_params=pltpu.CompilerParams(dimension_semantics=("arbitrary",), vmem_limit_bytes=VMEM_LIMIT),
        name="combine",
    )(run_len, run_tpos, run_src, buf_row, w_flat, h1, fg, ys_rows)


def _layer(h, mix_norm_g, w_in, w_attn_out, sgu_ln_g, sgu_ln_b, w_spatial, b_spatial, w_sgu_out,
           w_mix_out, ffn_norm_g, w_router, b_router, w_gate_up, b_gate_up, w_down, b_down, out_g):
    b, s, d = h.shape
    t = b * s
    sgu_width = sgu_ln_g.shape[0]
    x2 = h.reshape(t, d)

    q, qt, kt, v, kmean, u, vvn, ga, gs = _in_proj(
        x2, mix_norm_g.reshape(1, d), w_in.astype(BF16), sgu_ln_g.reshape(1, sgu_width),
        sgu_ln_b.reshape(1, sgu_width), sgu_width)

    nb = s // MOBA_BLOCK
    attn = _moba(q.reshape(b, s, ATTN_WIDTH), qt, kt, v.reshape(b, s, ATTN_WIDTH),
                 kmean.reshape(b, nb, ATTN_WIDTH), _moba_key_table(s)).reshape(t, ATTN_WIDTH)

    gdim = sgu_width // SGU_GROUPS
    bsp_full = jnp.repeat(b_spatial.T, gdim, axis=1)
    wr_pad = jnp.zeros((d, LANES), F32).at[:, :N_EXPERTS].set(w_router)
    br_pad = jnp.zeros((1, LANES), F32).at[0, :N_EXPERTS].set(b_router)
    h1, xn2, ri, rw, tile_cnt, pos_t = _mix(
        attn, u, vvn, ga, gs, x2, w_attn_out.astype(BF16), w_sgu_out.astype(BF16), w_mix_out.astype(BF16),
        w_spatial, bsp_full, ffn_norm_g.reshape(1, d), wr_pad, br_pad)

    n_assign = t * EXPERT_TOPK
    n_tiles = -(-(n_assign + N_EXPERTS * (EXPERT_ROWS - 1)) // EXPERT_ROWS)
    n_rows = n_tiles * EXPERT_ROWS
    run_len = tile_cnt[::SUBLANES, :N_EXPERTS].astype(jnp.int32)
    counts = jnp.sum(run_len, axis=0)
    padded = (counts + EXPERT_ROWS - 1) // EXPERT_ROWS * EXPERT_ROWS
    pad_end = jnp.cumsum(padded)
    starts = pad_end - padded
    run_tpos = jnp.cumsum(run_len, axis=1) - run_len
    run_row = starts[None, :] + jnp.cumsum(run_len, axis=0) - run_len
    tile_start = jnp.arange(n_tiles, dtype=jnp.int32) * EXPERT_ROWS
    block_expert = jnp.minimum(
        jnp.sum((pad_end[None, :] <= tile_start[:, None]).astype(jnp.int32), axis=1), N_EXPERTS - 1)
    n_used = (pad_end[-1:] // EXPERT_ROWS).astype(jnp.int32)
    buf_row = ri[:, EXPERT_TOPK:2 * EXPERT_TOPK].reshape(n_assign) * SUBLANES
    tables = (run_len.reshape(-1), run_tpos.reshape(-1), run_row.reshape(-1))

    xs_rows = _dispatch(*tables, starts + counts, padded - counts, n_used, pos_t, xn2, n_rows)

    de = w_down.shape[1]
    bg = b_gate_up[:, 0::2].reshape(N_EXPERTS, 1, de)
    bu = b_gate_up[:, 1::2].reshape(N_EXPERTS, 1, de)
    ys_rows = _experts(block_expert, n_used, xs_rows, w_gate_up, bg, bu, w_down, b_down.reshape(N_EXPERTS, 1, d))

    out = _combine(*tables, buf_row, rw[:, :EXPERT_TOPK].reshape(n_assign), h1, out_g.reshape(1, d), ys_rows)
    return out.reshape(b, s, d)


def kernel(x, mix_norm_g, w_in, w_attn_out, sgu_ln_g, sgu_ln_b, w_spatial, b_spatial, w_sgu_out, w_mix_out,
           ffn_norm_g, w_router, b_router, w_gate_up, b_gate_up, w_down, b_down, final_norm_g):
    depth = w_in.shape[0]
    assert depth == 1, "the final RMSNorm is fused into the single layer's combine step"
    return _layer(x, mix_norm_g[0], w_in[0], w_attn_out[0], sgu_ln_g[0], sgu_ln_b[0], w_spatial[0],
                  b_spatial[0], w_sgu_out[0], w_mix_out[0], ffn_norm_g[0], w_router[0], b_router[0],
                  w_gate_up[0], b_gate_up[0], w_down[0], b_down[0], final_norm_g)
```

```python
import jax
import jax.numpy as jnp
import numpy as np
from jax import lax
from jax.experimental import pallas as pl
from jax.experimental.pallas import tpu as pltpu

F32 = jnp.float32
BF16 = jnp.bfloat16
NEG_INF = float("-inf")
MASK_VALUE = -1e30


def _bf16_pieces(x, n=3):
    pieces = []
    for _ in range(n):
        p = float(np.asarray(x, np.float32).astype(BF16).astype(np.float32))
        pieces.append(p)
        x = x - p
    return pieces


LOG2E = 1.4426950408889634
LOG2E_PIECES = _bf16_pieces(LOG2E)

N_HEADS = 8
HEAD_DIM = 64
ATTN_WIDTH = N_HEADS * HEAD_DIM
MOBA_BLOCK = 256
MOBA_TOPK = 3
SGU_CHUNK = 128
SGU_GROUPS = 8
N_EXPERTS = 32
EXPERT_TOPK = 4
SWIGLU_LIMIT = 7.0
SWIGLU_ALPHA = 1.702
NORM_EPS = 1e-5

LANES = 128
SUBLANES = 8
PACKED_ROWS = 4
HEADS_PER_LANE_TILE = LANES // HEAD_DIM
MOBA_GROUP = 1
MOBA_BIAS_LANE0 = 64
MOBA_DUMMY_LANE = LANES - 1
PROJ_ROWS = 256
EXPERT_ROWS = 720
ROUTE_TOKENS = 512
VMEM_LIMIT = 48 * 1024 * 1024
BIG_VMEM_LIMIT = 56 * 1024 * 1024


def _sigmoid(x):
    return 1.0 / (1.0 + jnp.exp(-x))


def _store_token_rows(ref, value, base=0):
    n = value.shape[0]
    for c in range(value.shape[1] // LANES):
        ref[pl.ds(base + c, n, stride=SUBLANES), :] = value[:, c * LANES:(c + 1) * LANES]


def _load_token_rows(ref, n, base=0):
    return jnp.concatenate([ref[pl.ds(base + c, n, stride=SUBLANES), :] for c in range(SUBLANES)], axis=1)


def _gelu_exact(x):
    return 0.5 * x * (1.0 + lax.erf(x * (0.5 ** 0.5)))


def _in_proj_kernel(x_ref, g_ref, w_ref, lng_ref, lnb_ref,
                    q_ref, qt_ref, kt_ref, v_ref, kmean_ref, u_ref, vvn_ref, ga_ref, gs_ref):
    x = x_ref[...]
    xn = x * lax.rsqrt(jnp.mean(x * x, axis=-1, keepdims=True) + NORM_EPS) * g_ref[...]
    xb = xn.astype(BF16)

    def proj(lo, hi):
        return jnp.dot(xb, w_ref[:, lo:hi], preferred_element_type=F32)

    a = ATTN_WIDTH
    sw = u_ref.shape[1]
    d = ga_ref.shape[1]
    qf = proj(0, a)
    q_ref[...] = qf
    kf = proj(a, 2 * a)
    for j in range(kf.shape[0] // MOBA_BLOCK):
        qt_ref[j] = qf[j * MOBA_BLOCK:(j + 1) * MOBA_BLOCK].T
        kblk = kf[j * MOBA_BLOCK:(j + 1) * MOBA_BLOCK]
        kt_ref[j] = kblk.T.astype(BF16)
        kmean_ref[j] = jnp.mean(kblk, axis=0, keepdims=True)
    v_ref[...] = proj(2 * a, 3 * a).astype(BF16)
    z0 = 3 * a
    u_ref[...] = _gelu_exact(proj(z0, z0 + sw))
    zv = _gelu_exact(proj(z0 + sw, z0 + 2 * sw))
    mu = jnp.mean(zv, axis=-1, keepdims=True)
    zc = zv - mu
    var = jnp.mean(zc * zc, axis=-1, keepdims=True)
    vvn_ref[...] = zc * lax.rsqrt(var + NORM_EPS) * lng_ref[...] + lnb_ref[...]
    g0 = z0 + 2 * sw
    ga_ref[...] = _sigmoid(proj(g0, g0 + d))
    gs_ref[...] = _sigmoid(proj(g0 + d, g0 + 2 * d))


def _in_proj(x2, g, w_bf, lng, lnb, sgu_width):
    t, d = x2.shape
    tm = PROJ_ROWS
    n_in = w_bf.shape[1]
    row = lambda w: pl.BlockSpec((tm, w), lambda i: (i, 0))
    const = lambda shape: pl.BlockSpec(shape, lambda i: (0,) * len(shape))
    out_shape = (
        jax.ShapeDtypeStruct((t, ATTN_WIDTH), F32),
        jax.ShapeDtypeStruct((t // MOBA_BLOCK, ATTN_WIDTH, MOBA_BLOCK), F32),
        jax.ShapeDtypeStruct((t // MOBA_BLOCK, ATTN_WIDTH, MOBA_BLOCK), BF16),
        jax.ShapeDtypeStruct((t, ATTN_WIDTH), BF16),
        jax.ShapeDtypeStruct((t // MOBA_BLOCK, 1, ATTN_WIDTH), F32),
        jax.ShapeDtypeStruct((t, sgu_width), F32),
        jax.ShapeDtypeStruct((t, sgu_width), F32),
        jax.ShapeDtypeStruct((t, d), F32),
        jax.ShapeDtypeStruct((t, d), F32),
    )
    out_specs = (
        row(ATTN_WIDTH),
        pl.BlockSpec((tm // MOBA_BLOCK, ATTN_WIDTH, MOBA_BLOCK), lambda i: (i, 0, 0)),
        pl.BlockSpec((tm // MOBA_BLOCK, ATTN_WIDTH, MOBA_BLOCK), lambda i: (i, 0, 0)),
        row(ATTN_WIDTH),
        pl.BlockSpec((tm // MOBA_BLOCK, 1, ATTN_WIDTH), lambda i: (i, 0, 0)),
        row(sgu_width), row(sgu_width), row(d), row(d),
    )
    return pl.pallas_call(
        _in_proj_kernel,
        grid=(t // tm,),
        in_specs=[row(d), const((1, d)), const((d, n_in)), const((1, sgu_width)), const((1, sgu_width))],
        out_specs=out_specs,
        out_shape=out_shape,
        compiler_params=pltpu.CompilerParams(dimension_semantics=("arbitrary",), vmem_limit_bytes=VMEM_LIMIT),
        name="in_proj",
    )(x2, g, w_bf, lng, lnb)


def _alibi_slopes():
    slopes = 2.0 ** (-8.0 * np.arange(1, N_HEADS + 1, dtype=np.float64) / N_HEADS)
    assert all(np.log2(s) == np.round(np.log2(s)) for s in slopes), "ALiBi slopes must be powers of two"
    return slopes


def _moba_bias_lane(head_in_tile, part, piece):
    return MOBA_BIAS_LANE0 + (head_in_tile * 2 + part) * len(LOG2E_PIECES) + piece


def _moba_key_table(s):
    nb = s // MOBA_BLOCK
    slopes = _alibi_slopes()
    n_pairs = N_HEADS // HEADS_PER_LANE_TILE
    table = np.zeros((nb + 1, n_pairs, LANES, MOBA_BLOCK), np.float32)
    offs = np.arange(MOBA_BLOCK, dtype=np.float32)
    assert HEADS_PER_LANE_TILE * nb <= MOBA_BIAS_LANE0, "block one-hot rows must not overlap the ALiBi rows"
    for j in range(nb):
        for hh in range(HEADS_PER_LANE_TILE):
            table[j, :, hh * nb + j, :] = 1.0
        for p in range(n_pairs):
            for hh in range(HEADS_PER_LANE_TILE):
                slope = slopes[p * HEADS_PER_LANE_TILE + hh]
                for piece in range(len(LOG2E_PIECES)):
                    table[j, p, _moba_bias_lane(hh, 0, piece), :] = slope * MOBA_BLOCK * j
                    table[j, p, _moba_bias_lane(hh, 1, piece), :] = slope * offs
    table[nb, :, MOBA_DUMMY_LANE, :] = 1.0
    as_bf16 = table.astype(BF16)
    assert np.array_equal(as_bf16.astype(np.float32), table), "bias table must be exact in bf16"
    return jnp.asarray(as_bf16.reshape(nb + 1, n_pairs * LANES, MOBA_BLOCK))


def _moba_kernel(q_ref, qt_ref, qn_ref, qtn_ref, kt_ref, v_ref, km_ref, ct_ref, causal_ref, o_ref, qaug_ref,
                 s_ref, mpart_ref, mrow_ref, acc_ref):
    i = pl.program_id(2)
    nb = kt_ref.shape[0]
    blk = MOBA_BLOCK
    lane = lax.broadcasted_iota(jnp.int32, (1, LANES), 1)
    n_groups = lax.div(i + MOBA_GROUP, MOBA_GROUP)
    heads = range(HEADS_PER_LANE_TILE)
    hmasks = [(lane >= HEAD_DIM * hh) & (lane < HEAD_DIM * (hh + 1)) for hh in heads]
    slot = lax.rem(i, 2)

    def build_query_operand(q, qt, tile, into):
        bid = lax.broadcasted_iota(jnp.int32, (nb, blk), 0)
        bid_f = bid.astype(F32)
        masks_t = []
        for hh in heads:
            km_h = jnp.where(hmasks[hh], km_ref[...], 0.0)
            gate = jnp.dot(km_h, qt, precision=lax.Precision.HIGHEST, preferred_element_type=F32)
            gate = jnp.where(bid < tile, gate, NEG_INF)
            blockmask = jnp.where(bid == tile, 0.0, MASK_VALUE)
            for kk in range(MOBA_TOPK):
                gmax = jnp.max(gate, axis=0, keepdims=True)
                first = jnp.min(jnp.where(gate == gmax, bid_f, float(nb)), axis=0, keepdims=True)
                valid = (tile > kk).astype(F32)
                first = first * valid + (valid - 1.0)
                hit = bid_f == first
                blockmask = jnp.where(hit, 0.0, blockmask)
                gate = jnp.where(hit, NEG_INF, gate)
            masks_t.append(blockmask)
        masks_t.append(jnp.zeros((LANES - len(masks_t) * nb, blk), F32))
        masks = jnp.concatenate(masks_t, axis=0).T

        for hh in heads:
            feats = jnp.where(lane == MOBA_DUMMY_LANE, MASK_VALUE, 0.0)
            for part in range(2):
                for piece, value in enumerate(LOG2E_PIECES):
                    feats = jnp.where(lane == _moba_bias_lane(hh, part, piece), value, feats)
            extra = jnp.where((lane >= hh * nb) & (lane < (hh + 1) * nb), masks, feats)
            qh = jnp.where(hmasks[hh], q, 0.0)
            qaug_ref[into, hh * blk:(hh + 1) * blk, 0:LANES] = (qh * (HEAD_DIM ** -0.5 * LOG2E)).astype(BF16)
            qaug_ref[into, hh * blk:(hh + 1) * blk, LANES:2 * LANES] = extra.astype(BF16)

    @pl.when(i == 0)
    def _():
        build_query_operand(q_ref[...], qt_ref[...], i, slot)

    def scores(jk, jc):
        rhs = jnp.concatenate([kt_ref[jk], ct_ref[jc]], axis=0)
        return jnp.dot(qaug_ref[slot], rhs, preferred_element_type=F32)

    def lane_halves_max(s):
        return jnp.maximum(s[:, 0:LANES], s[:, LANES:2 * LANES])

    n_rows = len(heads) * blk
    variants = [(groups, min(groups * MOBA_GROUP, nb)) for groups in range(1, pl.cdiv(nb, MOBA_GROUP) + 1)]

    for groups, n_blocks in variants:
        @pl.when(n_groups == groups)
        def _(n_blocks=n_blocks):
            part = None
            for j in range(n_blocks):
                s = scores(j, jnp.where(j <= i, j, nb)) + causal_ref[(i == j).astype(jnp.int32)]
                s_ref[j] = s
                part = lane_halves_max(s) if part is None else jnp.maximum(part, lane_halves_max(s))
            mpart_ref[...] = part

    ones = jnp.ones((blk, LANES), BF16)
    rowmax = jnp.max(mpart_ref[...], axis=1, keepdims=True)
    mrow_ref[...] = jnp.broadcast_to(rowmax, (n_rows, 2 * LANES))

    def weighted(j):
        p = jnp.exp2(s_ref[j] - mrow_ref[...]).astype(BF16)
        vaug = jnp.concatenate([v_ref[j * blk:(j + 1) * blk, :], ones], axis=1)
        return jnp.dot(p, vaug, preferred_element_type=F32)

    next_tile = jnp.minimum(i + 1, nb - 1)
    for groups, n_blocks in variants:
        @pl.when(n_groups == groups)
        def _(n_blocks=n_blocks):
            build_query_operand(qn_ref[...], qtn_ref[...], next_tile, 1 - slot)
            tot = weighted(0)
            for j in range(1, n_blocks):
                tot = tot + weighted(j)
            acc_ref[...] = tot

    out = jnp.zeros((blk, LANES), F32)
    for hh in heads:
        acc = acc_ref[hh * blk:(hh + 1) * blk, :]
        out = jnp.where(hmasks[hh], acc[:, 0:LANES] / acc[:, LANES:2 * LANES], out)
    o_ref[...] = out


def _moba(q, qt, kt, v, kmean, key_table):
    b, s, a = q.shape
    nb = s // MOBA_BLOCK
    n_pairs = a // LANES
    rows = HEADS_PER_LANE_TILE * MOBA_BLOCK
    row_in_block = np.arange(rows)[:, None] % MOBA_BLOCK
    causal = np.where(row_in_block >= np.arange(MOBA_BLOCK)[None, :], 0.0, MASK_VALUE).astype(np.float32)
    causal_tiles = jnp.asarray(np.stack([np.zeros_like(causal), causal]))
    nxt = lambda i: jnp.minimum(i + 1, nb - 1)
    return pl.pallas_call(
        _moba_kernel,
        grid=(b, n_pairs, nb),
        in_specs=[
            pl.BlockSpec((None, MOBA_BLOCK, LANES), lambda bi, p, i: (bi, i, p)),
            pl.BlockSpec((None, LANES, MOBA_BLOCK), lambda bi, p, i: (bi * nb + i, p, 0)),
            pl.BlockSpec((None, MOBA_BLOCK, LANES), lambda bi, p, i: (bi, nxt(i), p)),
            pl.BlockSpec((None, LANES, MOBA_BLOCK), lambda bi, p, i: (bi * nb + nxt(i), p, 0)),
            pl.BlockSpec((nb, LANES, MOBA_BLOCK), lambda bi, p, i: (bi, p, 0)),
            pl.BlockSpec((None, s, LANES), lambda bi, p, i: (bi, 0, p)),
            pl.BlockSpec((None, nb, LANES), lambda bi, p, i: (bi, 0, p)),
            pl.BlockSpec((nb + 1, LANES, MOBA_BLOCK), lambda bi, p, i: (0, p, 0)),
            pl.BlockSpec((2, rows, MOBA_BLOCK), lambda bi, p, i: (0, 0, 0)),
        ],
        out_specs=pl.BlockSpec((None, MOBA_BLOCK, LANES), lambda bi, p, i: (bi, i, p)),
        out_shape=jax.ShapeDtypeStruct((b, s, a), F32),
        scratch_shapes=[
            pltpu.VMEM((2, rows, 2 * LANES), BF16),
            pltpu.VMEM((nb + 1, rows, MOBA_BLOCK), F32),
            pltpu.VMEM((rows, LANES), F32),
            pltpu.VMEM((rows, 2 * LANES), F32),
            pltpu.VMEM((rows, 2 * LANES), F32),
        ],
        compiler_params=pltpu.CompilerParams(
            dimension_semantics=("arbitrary", "arbitrary", "arbitrary"), vmem_limit_bytes=VMEM_LIMIT),
        name="moba",
    )(q, qt, q, qt, kt, v, kmean, key_table, causal_tiles)


def _mix_kernel(attn_ref, u_ref, vvn_ref, ga_ref, gs_ref, x_ref,
                wao_ref, wso_ref, wmo_ref, wsp_ref, bsp_ref, fg_ref, wr_ref, br_ref,
                h1_ref, xn2_ref, ri_ref, rw_ref, tcnt_ref, post_ref, wcausal_ref):
    tm = x_ref.shape[0]
    ch = SGU_CHUNK

    @pl.when(pl.program_id(0) == 0)
    def _():
        tril = lax.broadcasted_iota(jnp.int32, (ch, ch), 0) >= lax.broadcasted_iota(jnp.int32, (ch, ch), 1)
        for g in range(SGU_GROUPS):
            wcausal_ref[g] = jnp.where(tril, wsp_ref[g], 0.0).astype(BF16)

    y_attn = jnp.dot(attn_ref[...].astype(BF16), wao_ref[...], preferred_element_type=F32)

    lane = lax.broadcasted_iota(jnp.int32, (1, LANES), 1)
    gdim = vvn_ref.shape[1] // SGU_GROUPS
    groups_per_tile = LANES // gdim
    w_causal = [wcausal_ref[g] for g in range(SGU_GROUPS)]
    rows = []
    for c in range(tm // ch):
        cols = []
        for ct in range(vvn_ref.shape[1] // LANES):
            vp = vvn_ref[c * ch:(c + 1) * ch, ct * LANES:(ct + 1) * LANES]
            acc = jnp.zeros((ch, LANES), F32)
            for gg in range(groups_per_tile):
                gmask = (lane >= gdim * gg) & (lane < gdim * (gg + 1))
                vm = jnp.where(gmask, vp, 0.0).astype(BF16)
                acc = acc + jnp.dot(w_causal[ct * groups_per_tile + gg], vm, preferred_element_type=F32)
            cols.append(acc)
        rows.append(jnp.concatenate(cols, axis=1) + bsp_ref[...])
    mixed = jnp.concatenate(rows, axis=0)
    sgu = u_ref[...] * mixed
    y_sgu = jnp.dot(sgu.astype(BF16), wso_ref[...], preferred_element_type=F32)

    merged = ga_ref[...] * y_attn + gs_ref[...] * y_sgu
    h1 = x_ref[...] + jnp.dot(merged.astype(BF16), wmo_ref[...], preferred_element_type=F32)
    h1_ref[...] = h1
    xn2 = h1 * lax.rsqrt(jnp.mean(h1 * h1, axis=-1, keepdims=True) + NORM_EPS) * fg_ref[...]
    xn2_ref[...] = xn2.astype(BF16)

    def split(a):
        hi = a.astype(BF16)
        return hi, (a - hi.astype(F32)).astype(BF16)

    x_hi, x_lo = split(xn2)
    w_hi, w_lo = split(wr_ref[...])
    hi_both = jnp.dot(x_hi, jnp.concatenate([w_hi, w_lo], axis=1), preferred_element_type=F32)
    logits = (hi_both[:, 0:LANES] + jnp.dot(x_lo, w_hi, preferred_element_type=F32)
              + hi_both[:, LANES:2 * LANES]) + br_ref[...]
    work = jnp.where(lane < N_EXPERTS, logits, NEG_INF)
    lane_f = lane.astype(F32)
    vals, idxs = [], []
    for _ in range(EXPERT_TOPK):
        vmax = jnp.max(work, axis=1, keepdims=True)
        first = jnp.min(jnp.where(work == vmax, lane_f, float(LANES)), axis=1, keepdims=True)
        vals.append(vmax)
        idxs.append(first)
        work = jnp.where(lane_f == first, NEG_INF, work)
    exps = [jnp.exp(v - vals[0]) for v in vals]
    denom = exps[0]
    for e in exps[1:]:
        denom = denom + e
    chosen = jnp.zeros((tm, LANES), F32)
    for first in idxs:
        chosen = jnp.where(lane_f == first, 1.0, chosen)

    strict_lower = (lax.broadcasted_iota(jnp.int32, (tm, tm), 0)
                    > lax.broadcasted_iota(jnp.int32, (tm, tm), 1))
    before = jnp.dot(jnp.where(strict_lower, 1.0, 0.0).astype(BF16), chosen.astype(BF16),
                     preferred_element_type=F32)
    counts = jnp.broadcast_to(jnp.sum(chosen, axis=0, keepdims=True), tcnt_ref.shape)
    lower_expert = (lax.broadcasted_iota(jnp.int32, (LANES, LANES), 0)
                    < lax.broadcasted_iota(jnp.int32, (LANES, LANES), 1))
    expert_start = jnp.dot(counts, jnp.where(lower_expert, 1.0, 0.0), precision=lax.Precision.HIGHEST,
                           preferred_element_type=F32)[0:1, :]
    sorted_pos = before + expert_start
    ri = jnp.zeros((tm, LANES), jnp.int32)
    rw = jnp.zeros((tm, LANES), F32)
    pos_lanes = jnp.zeros((tm, LANES), F32)
    for kk in range(EXPERT_TOPK):
        pos = jnp.sum(jnp.where(lane_f == idxs[kk], sorted_pos, 0.0), axis=1, keepdims=True)
        ri = jnp.where(lane == kk, idxs[kk].astype(jnp.int32), ri)
        ri = jnp.where(lane == EXPERT_TOPK + kk, pos.astype(jnp.int32), ri)
        rw = jnp.where(lane == kk, exps[kk] / denom, rw)
        pos_lanes = jnp.where(lane == kk, pos, pos_lanes)
    ri_ref[...] = ri
    rw_ref[...] = rw
    tcnt_ref[...] = counts
    post_ref[...] = pos_lanes.T[0:SUBLANES, :]


def _mix(attn, u, vvn, ga, gs, x2, wao, wso, wmo, wsp, bsp_full, fg, wr_pad, br_pad):
    t, d = x2.shape
    tm = ROUTE_TOKENS
    row = lambda w: pl.BlockSpec((tm, w), lambda i: (i, 0))
    const = lambda shape: pl.BlockSpec(shape, lambda i: (0,) * len(shape))
    ins = (attn, u, vvn, ga, gs, x2, wao, wso, wmo, wsp, bsp_full, fg, wr_pad, br_pad)
    in_specs = [row(a.shape[1]) for a in ins[:6]] + [const(a.shape) for a in ins[6:]]
    out_shape = (
        jax.ShapeDtypeStruct((t, d), F32),
        jax.ShapeDtypeStruct((t, d), BF16),
        jax.ShapeDtypeStruct((t, LANES), jnp.int32),
        jax.ShapeDtypeStruct((t, LANES), F32),
        jax.ShapeDtypeStruct((t // tm * SUBLANES, LANES), F32),
        jax.ShapeDtypeStruct((t // tm * SUBLANES, tm), F32),
    )
    assert d == SUBLANES * LANES, "a token row is stored as one (8, 128) tile"
    per_tile = lambda w: pl.BlockSpec((SUBLANES, w), lambda i: (i, 0))
    out_specs = (row(d), row(d), row(LANES), row(LANES), per_tile(LANES), per_tile(tm))
    return pl.pallas_call(
        _mix_kernel,
        grid=(t // tm,),
        in_specs=in_specs,
        out_specs=out_specs,
        out_shape=out_shape,
        scratch_shapes=[pltpu.VMEM(wsp.shape, BF16)],
        compiler_params=pltpu.CompilerParams(dimension_semantics=("arbitrary",), vmem_limit_bytes=BIG_VMEM_LIMIT),
        name="mix_route",
    )(*ins)


def _run_pieces(n, body):
    off = jnp.int32(0)
    for bit in reversed(range(ROUTE_TOKENS.bit_length())):
        size = 1 << bit
        take = lax.bitwise_and(n, size) != 0

        @pl.when(take)
        def _(off=off, size=size):
            body(off, size)

        off = off + jnp.where(take, size, 0)


def _token_rows(ref, r, n, rows_per_token=SUBLANES):
    return ref.at[pl.ds(pl.multiple_of(r * rows_per_token, rows_per_token), n * rows_per_token)]


def _pack_bf16_pairs(value):
    half = value.shape[1] // 2
    lo = lax.shift_right_logical(lax.bitcast_convert_type(value[:, :half], jnp.uint32), jnp.uint32(16))
    hi = lax.bitwise_and(lax.bitcast_convert_type(value[:, half:], jnp.uint32), jnp.uint32(0xFFFF0000))
    return lax.bitwise_or(lo, hi)


def _unpack_bf16_pairs(words):
    lo = lax.bitcast_convert_type(lax.shift_left(words, jnp.uint32(16)), F32)
    hi = lax.bitcast_convert_type(lax.bitwise_and(words, jnp.uint32(0xFFFF0000)), F32)
    return jnp.concatenate([lo, hi], axis=1).astype(BF16)


def _dispatch_kernel(len_ref, tpos_ref, dst_ref, padlo_ref, padlen_ref, nused_ref, post_ref, x_ref,
                     xs_hbm, sorted_ref, zeros_ref, sems, zsem):
    step = pl.program_id(0)
    n_steps = pl.num_programs(0)
    tt = x_ref.shape[0]
    n_assign = tt * EXPERT_TOPK
    rpt = PACKED_ROWS
    tile_rows = zeros_ref.shape[0] // rpt
    slot = lax.rem(step, 2)

    def start_runs(tile, buf):
        def per_expert(e, carry):
            k = tile * N_EXPERTS + e

            def piece(off, size):
                pltpu.make_async_copy(_token_rows(sorted_ref.at[buf], tpos_ref[k] + off, size, rpt),
                                      _token_rows(xs_hbm, dst_ref[k] + off, size, rpt), sems.at[buf]).start()

            _run_pieces(len_ref[k], piece)
            return carry

        lax.fori_loop(0, N_EXPERTS, per_expert, 0)

    def wait_runs(buf):
        whole = sorted_ref.at[buf]
        pltpu.make_async_copy(whole, _token_rows(xs_hbm, 0, n_assign, rpt), sems.at[buf]).wait()

    @pl.when(step == 0)
    def _():
        zeros_ref[...] = jnp.zeros_like(zeros_ref)

        def zero_fill(wait):
            def finish(cp):
                if wait:
                    cp.wait()
                else:
                    cp.start()

            def tail_tile(tl, carry):
                finish(pltpu.make_async_copy(zeros_ref, _token_rows(xs_hbm, tl * tile_rows, tile_rows, rpt), zsem))
                return carry

            lax.fori_loop(nused_ref[0], xs_hbm.shape[0] // (tile_rows * rpt), tail_tile, 0)

            def per_expert(e, carry):
                def piece(off, size):
                    finish(pltpu.make_async_copy(_token_rows(zeros_ref, 0, size, rpt),
                                                 _token_rows(xs_hbm, padlo_ref[e] + off, size, rpt), zsem))

                _run_pieces(padlen_ref[e], piece)
                return carry

            lax.fori_loop(0, N_EXPERTS, per_expert, 0)

        zero_fill(wait=False)
        zero_fill(wait=True)

    @pl.when(step >= 2)
    def _():
        wait_runs(slot)

    x = x_ref[...]
    pos_by_choice = [post_ref[kk:kk + 1, :] for kk in range(EXPERT_TOPK)]
    chunk = MOBA_BLOCK
    for rc in range(n_assign // chunk):
        r = (lax.broadcasted_iota(jnp.int32, (chunk, 1), 0) + rc * chunk).astype(F32)
        select = jnp.zeros((chunk, tt), F32)
        for pos in pos_by_choice:
            select = jnp.where(r == pos, 1.0, select)
        rows = jnp.dot(select.astype(BF16), x, preferred_element_type=F32)
        words = _pack_bf16_pairs(rows)
        for c in range(rpt):
            sorted_ref[slot, pl.ds(rc * chunk * rpt + c, chunk, stride=rpt), :] = words[:, c * LANES:(c + 1) * LANES]
    start_runs(step, slot)

    @pl.when(step == n_steps - 1)
    def _():
        @pl.when(step >= 1)
        def _():
            wait_runs(1 - slot)

        wait_runs(slot)


def _dispatch(run_len, run_tpos, run_dst, padlo, padlen, n_used, pos_t, xn2, n_rows):
    t, d = xn2.shape
    n_assign = ROUTE_TOKENS * EXPERT_TOPK
    grid_spec = pltpu.PrefetchScalarGridSpec(
        num_scalar_prefetch=6,
        grid=(t // ROUTE_TOKENS,),
        in_specs=[pl.BlockSpec((SUBLANES, ROUTE_TOKENS), lambda c, *_: (c, 0)),
                  pl.BlockSpec((ROUTE_TOKENS, d), lambda c, *_: (c, 0))],
        out_specs=pl.BlockSpec(memory_space=pl.ANY),
        scratch_shapes=[
            pltpu.VMEM((2, n_assign * PACKED_ROWS, LANES), jnp.uint32),
            pltpu.VMEM((EXPERT_ROWS * PACKED_ROWS, LANES), jnp.uint32),
            pltpu.SemaphoreType.DMA((2,)),
            pltpu.SemaphoreType.DMA(()),
        ],
    )
    assert d == 2 * PACKED_ROWS * LANES
    return pl.pallas_call(
        _dispatch_kernel,
        grid_spec=grid_spec,
        out_shape=jax.ShapeDtypeStruct((n_rows * PACKED_ROWS, LANES), jnp.uint32),
        compiler_params=pltpu.CompilerParams(dimension_semantics=("arbitrary",), vmem_limit_bytes=VMEM_LIMIT),
        name="dispatch",
    )(run_len, run_tpos, run_dst, padlo, padlen, n_used, pos_t, xn2)


def _expert_kernel(be_ref, nused_ref, x_ref, wgu_ref, bg_ref, bu_ref, wd_ref, bd_ref, y_ref, wgu_bf, wd_bf):
    t = pl.program_id(0)
    rows = y_ref.shape[0] // SUBLANES
    de = wd_ref.shape[0]
    grp = 2 * LANES
    active = t < nused_ref[0]
    fresh = jnp.logical_and(
        active, jnp.logical_or(t == 0, be_ref[t] != be_ref[jnp.maximum(t - 1, 0)]))

    @pl.when(fresh)
    def _():
        src = lax.broadcasted_iota(jnp.int32, (grp, grp), 0)
        dst = lax.broadcasted_iota(jnp.int32, (grp, grp), 1)
        wanted = jnp.where(dst < LANES, 2 * dst, 2 * (dst - LANES) + 1)
        perm = jnp.where(src == wanted, 1.0, 0.0).astype(BF16)
        for g in range(wgu_ref.shape[1] // grp):
            cols = slice(g * grp, (g + 1) * grp)
            wgu_bf[:, cols] = jnp.dot(wgu_ref[:, cols].astype(BF16), perm,
                                      preferred_element_type=F32).astype(BF16)
        wd_bf[...] = wd_ref[...].astype(BF16)

    @pl.when(active)
    def _():
        words = jnp.concatenate([x_ref[pl.ds(c, rows, stride=PACKED_ROWS), :] for c in range(PACKED_ROWS)], axis=1)
        xb = _unpack_bf16_pairs(words)
        gu = jnp.dot(xb, wgu_bf[...], preferred_element_type=F32)
        hid = []
        for g in range(de // LANES):
            cols = slice(g * LANES, (g + 1) * LANES)
            gate = gu[:, g * grp:g * grp + LANES] + bg_ref[:, cols]
            up = gu[:, g * grp + LANES:(g + 1) * grp] + bu_ref[:, cols]
            gate = jnp.minimum(gate, SWIGLU_LIMIT)
            up = jnp.clip(up, -SWIGLU_LIMIT, SWIGLU_LIMIT)
            hid.append((gate * _sigmoid(SWIGLU_ALPHA * gate) * (up + 1.0)).astype(BF16))
        y = jnp.dot(jnp.concatenate(hid, axis=1), wd_bf[...], preferred_element_type=F32) + bd_ref[...]
        _store_token_rows(y_ref, y)

    @pl.when(jnp.logical_not(active))
    def _():
        y_ref[...] = jnp.zeros_like(y_ref)


def _experts(block_expert, n_used, xs_rows, wgu, bg, bu, wd, bd):
    n_rows = xs_rows.shape[0] // PACKED_ROWS
    _, d, de2 = wgu.shape
    de = wd.shape[1]
    n_tiles = n_rows // EXPERT_ROWS
    tile_rows = EXPERT_ROWS * SUBLANES
    per_expert = lambda t, be, nu: (be[t], 0, 0)
    grid_spec = pltpu.PrefetchScalarGridSpec(
        num_scalar_prefetch=2,
        grid=(n_tiles,),
        in_specs=[
            pl.BlockSpec((EXPERT_ROWS * PACKED_ROWS, LANES),
                         lambda t, be, nu: (jnp.minimum(t, jnp.maximum(nu[0] - 1, 0)), 0)),
            pl.BlockSpec((None, d, de2), per_expert),
            pl.BlockSpec((None, 1, de), per_expert),
            pl.BlockSpec((None, 1, de), per_expert),
            pl.BlockSpec((None, de, d), per_expert),
            pl.BlockSpec((None, 1, d), per_expert),
        ],
        out_specs=pl.BlockSpec((tile_rows, LANES), lambda t, be, nu: (t, 0)),
        scratch_shapes=[pltpu.VMEM((d, de2), BF16), pltpu.VMEM((de, d), BF16)],
    )
    return pl.pallas_call(
        _expert_kernel,
        grid_spec=grid_spec,
        out_shape=jax.ShapeDtypeStruct((n_rows * SUBLANES, LANES), F32),
        compiler_params=pltpu.CompilerParams(
            dimension_semantics=("arbitrary",), vmem_limit_bytes=BIG_VMEM_LIMIT),
        name="experts",
    )(block_expert, n_used, xs_rows, wgu, bg, bu, wd, bd)


def _combine_kernel(len_ref, tpos_ref, src_ref, row_ref, w_ref, h1_ref, fg_ref, ys_hbm,
                    o_ref, buf_ref, h1rows_ref, outrows_ref, sems):
    step = pl.program_id(0)
    n_steps = pl.num_programs(0)
    tt = h1_ref.shape[0]
    slot = lax.rem(step, 2)

    def start_runs(tile, buf):
        def per_expert(e, carry):
            k = tile * N_EXPERTS + e

            def piece(off, size):
                pltpu.make_async_copy(_token_rows(ys_hbm, src_ref[k] + off, size),
                                      _token_rows(buf_ref.at[buf], tpos_ref[k] + off, size), sems.at[buf]).start()

            _run_pieces(len_ref[k], piece)
            return carry

        lax.fori_loop(0, N_EXPERTS, per_expert, 0)

    def wait_runs(buf):
        whole = buf_ref.at[buf]
        pltpu.make_async_copy(_token_rows(ys_hbm, 0, tt * EXPERT_TOPK), whole, sems.at[buf]).wait()

    @pl.when(step == 0)
    def _():
        start_runs(0, 0)

    @pl.when(step + 1 < n_steps)
    def _():
        start_runs(step + 1, 1 - slot)

    wait_runs(slot)

    _store_token_rows(h1rows_ref, h1_ref[...])

    def one_token(tok, carry):
        acc = h1rows_ref[pl.ds(pl.multiple_of(tok * SUBLANES, SUBLANES), SUBLANES), :]
        for kk in range(EXPERT_TOPK):
            a = tok * EXPERT_TOPK + kk
            acc = acc + w_ref[a] * buf_ref[slot, pl.ds(pl.multiple_of(row_ref[a], SUBLANES), SUBLANES), :]
        outrows_ref[pl.ds(pl.multiple_of(tok * SUBLANES, SUBLANES), SUBLANES), :] = acc
        return carry

    lax.fori_loop(0, tt, one_token, 0, unroll=32)
    h = _load_token_rows(outrows_ref, tt)
    o_ref[...] = h * lax.rsqrt(jnp.mean(h * h, axis=-1, keepdims=True) + NORM_EPS) * fg_ref[...]


def _combine(run_len, run_tpos, run_src, buf_row, w_flat, h1, fg, ys_rows):
    t, d = h1.shape
    tt = ROUTE_TOKENS
    n_assign = tt * EXPERT_TOPK
    smem_blk = pl.BlockSpec((n_assign,), lambda c, *_: (c,), memory_space=pltpu.SMEM)
    grid_spec = pltpu.PrefetchScalarGridSpec(
        num_scalar_prefetch=3,
        grid=(t // tt,),
        in_specs=[
            smem_blk, smem_blk,
            pl.BlockSpec((tt, d), lambda c, *_: (c, 0)),
            pl.BlockSpec((1, d), lambda c, *_: (0, 0)),
            pl.BlockSpec(memory_space=pl.ANY),
        ],
        out_specs=pl.BlockSpec((tt, d), lambda c, *_: (c, 0)),
        scratch_shapes=[
            pltpu.VMEM((2, n_assign * SUBLANES, LANES), F32),
            pltpu.VMEM((tt * SUBLANES, LANES), F32),
            pltpu.VMEM((tt * SUBLANES, LANES), F32),
            pltpu.SemaphoreType.DMA((2,)),
        ],
    )
    return pl.pallas_call(
        _combine_kernel,
        grid_spec=grid_spec,
        out_shape=jax.ShapeDtypeStruct((t, d), F32),
        compiler_params=pltpu.CompilerParams(dimension_semantics=("arbitrary",), vmem_limit_bytes=VMEM_LIMIT),
        name="combine",
    )(run_len, run_tpos, run_src, buf_row, w_flat, h1, fg, ys_rows)


def _layer(h, mix_norm_g, w_in, w_attn_out, sgu_ln_g, sgu_ln_b, w_spatial, b_spatial, w_sgu_out,
           w_mix_out, ffn_norm_g, w_router, b_router, w_gate_up, b_gate_up, w_down, b_down, out_g):
    b, s, d = h.shape
    t = b * s
    sgu_width = sgu_ln_g.shape[0]
    x2 = h.reshape(t, d)

    q, qt, kt, v, kmean, u, vvn, ga, gs = _in_proj(
        x2, mix_norm_g.reshape(1, d), w_in.astype(BF16), sgu_ln_g.reshape(1, sgu_width),
        sgu_ln_b.reshape(1, sgu_width), sgu_width)

    nb = s // MOBA_BLOCK
    attn = _moba(q.reshape(b, s, ATTN_WIDTH), qt, kt, v.reshape(b, s, ATTN_WIDTH),
                 kmean.reshape(b, nb, ATTN_WIDTH), _moba_key_table(s)).reshape(t, ATTN_WIDTH)

    gdim = sgu_width // SGU_GROUPS
    bsp_full = jnp.repeat(b_spatial.T, gdim, axis=1)
    wr_pad = jnp.zeros((d, LANES), F32).at[:, :N_EXPERTS].set(w_router)
    br_pad = jnp.zeros((1, LANES), F32).at[0, :N_EXPERTS].set(b_router)
    h1, xn2, ri, rw, tile_cnt, pos_t = _mix(
        attn, u, vvn, ga, gs, x2, w_attn_out.astype(BF16), w_sgu_out.astype(BF16), w_mix_out.astype(BF16),
        w_spatial, bsp_full, ffn_norm_g.reshape(1, d), wr_pad, br_pad)

    n_assign = t * EXPERT_TOPK
    n_tiles = -(-(n_assign + N_EXPERTS * (EXPERT_ROWS - 1)) // EXPERT_ROWS)
    n_rows = n_tiles * EXPERT_ROWS
    run_len = tile_cnt[::SUBLANES, :N_EXPERTS].astype(jnp.int32)
    counts = jnp.sum(run_len, axis=0)
    padded = (counts + EXPERT_ROWS - 1) // EXPERT_ROWS * EXPERT_ROWS
    pad_end = jnp.cumsum(padded)
    starts = pad_end - padded
    run_tpos = jnp.cumsum(run_len, axis=1) - run_len
    run_row = starts[None, :] + jnp.cumsum(run_len, axis=0) - run_len
    tile_start = jnp.arange(n_tiles, dtype=jnp.int32) * EXPERT_ROWS
    block_expert = jnp.minimum(
        jnp.sum((pad_end[None, :] <= tile_start[:, None]).astype(jnp.int32), axis=1), N_EXPERTS - 1)
    n_used = (pad_end[-1:] // EXPERT_ROWS).astype(jnp.int32)
    buf_row = ri[:, EXPERT_TOPK:2 * EXPERT_TOPK].reshape(n_assign) * SUBLANES
    tables = (run_len.reshape(-1), run_tpos.reshape(-1), run_row.reshape(-1))

    xs_rows = _dispatch(*tables, starts + counts, padded - counts, n_used, pos_t, xn2, n_rows)

    de = w_down.shape[1]
    bg = b_gate_up[:, 0::2].reshape(N_EXPERTS, 1, de)
    bu = b_gate_up[:, 1::2].reshape(N_EXPERTS, 1, de)
    ys_rows = _experts(block_expert, n_used, xs_rows, w_gate_up, bg, bu, w_down, b_down.reshape(N_EXPERTS, 1, d))

    out = _combine(*tables, buf_row, rw[:, :EXPERT_TOPK].reshape(n_assign), h1, out_g.reshape(1, d), ys_rows)
    return out.reshape(b, s, d)


def kernel(x, mix_norm_g, w_in, w_attn_out, sgu_ln_g, sgu_ln_b, w_spatial, b_spatial, w_sgu_out, w_mix_out,
           ffn_norm_g, w_router, b_router, w_gate_up, b_gate_up, w_down, b_down, final_norm_g):
    depth = w_in.shape[0]
    assert depth == 1, "the final RMSNorm is fused into the single layer's combine step"
    return _layer(x, mix_norm_g[0], w_in[0], w_attn_out[0], sgu_ln_g[0], sgu_ln_b[0], w_spatial[0],
                  b_spatial[0], w_sgu_out[0], w_mix_out[0], ffn_norm_g[0], w_router[0], b_router[0],
                  w_gate_up[0], b_gate_up[0], w_down[0], b_down[0], final_norm_g)
```

```python
import jax
import jax.numpy as jnp
import numpy as np
from jax import lax
from jax.experimental import pallas as pl
from jax.experimental.pallas import tpu as pltpu

F32 = jnp.float32
BF16 = jnp.bfloat16
NEG_INF = float("-inf")
MASK_VALUE = -1e30


def _bf16_pieces(x, n=3):
    pieces = []
    for _ in range(n):
        p = float(np.asarray(x, np.float32).astype(BF16).astype(np.float32))
        pieces.append(p)
        x = x - p
    return pieces


LOG2E = 1.4426950408889634
LOG2E_PIECES = _bf16_pieces(LOG2E)

N_HEADS = 8
HEAD_DIM = 64
ATTN_WIDTH = N_HEADS * HEAD_DIM
MOBA_BLOCK = 256
MOBA_TOPK = 3
SGU_CHUNK = 128
SGU_GROUPS = 8
N_EXPERTS = 32
EXPERT_TOPK = 4
SWIGLU_LIMIT = 7.0
SWIGLU_ALPHA = 1.702
NORM_EPS = 1e-5

LANES = 128
SUBLANES = 8
PACKED_ROWS = 4
HEADS_PER_LANE_TILE = LANES // HEAD_DIM
MOBA_GROUP = 1
MOBA_BIAS_LANE0 = 64
MOBA_DUMMY_LANE = LANES - 1
PROJ_ROWS = 256
EXPERT_ROWS = 720
ROUTE_TOKENS = 512
VMEM_LIMIT = 48 * 1024 * 1024
BIG_VMEM_LIMIT = 56 * 1024 * 1024


def _sigmoid(x):
    return 1.0 / (1.0 + jnp.exp(-x))


def _store_token_rows(ref, value, base=0):
    n = value.shape[0]
    for c in range(value.shape[1] // LANES):
        ref[pl.ds(base + c, n, stride=SUBLANES), :] = value[:, c * LANES:(c + 1) * LANES]


def _load_token_rows(ref, n, base=0):
    return jnp.concatenate([ref[pl.ds(base + c, n, stride=SUBLANES), :] for c in range(SUBLANES)], axis=1)


def _gelu_exact(x):
    return 0.5 * x * (1.0 + lax.erf(x * (0.5 ** 0.5)))


def _in_proj_kernel(x_ref, g_ref, w_ref, lng_ref, lnb_ref,
                    q_ref, qt_ref, kt_ref, v_ref, kmean_ref, u_ref, vvn_ref, ga_ref, gs_ref):
    x = x_ref[...]
    xn = x * lax.rsqrt(jnp.mean(x * x, axis=-1, keepdims=True) + NORM_EPS) * g_ref[...]
    xb = xn.astype(BF16)

    def proj(lo, hi):
        return jnp.dot(xb, w_ref[:, lo:hi], preferred_element_type=F32)

    a = ATTN_WIDTH
    sw = u_ref.shape[1]
    d = ga_ref.shape[1]
    qf = proj(0, a)
    q_ref[...] = qf
    kf = proj(a, 2 * a)
    for j in range(kf.shape[0] // MOBA_BLOCK):
        qt_ref[j] = qf[j * MOBA_BLOCK:(j + 1) * MOBA_BLOCK].T
        kblk = kf[j * MOBA_BLOCK:(j + 1) * MOBA_BLOCK]
        kt_ref[j] = kblk.T.astype(BF16)
        kmean_ref[j] = jnp.mean(kblk, axis=0, keepdims=True)
    v_ref[...] = proj(2 * a, 3 * a).astype(BF16)
    z0 = 3 * a
    u_ref[...] = _gelu_exact(proj(z0, z0 + sw))
    zv = _gelu_exact(proj(z0 + sw, z0 + 2 * sw))
    mu = jnp.mean(zv, axis=-1, keepdims=True)
    zc = zv - mu
    var = jnp.mean(zc * zc, axis=-1, keepdims=True)
    vvn_ref[...] = zc * lax.rsqrt(var + NORM_EPS) * lng_ref[...] + lnb_ref[...]
    g0 = z0 + 2 * sw
    ga_ref[...] = _sigmoid(proj(g0, g0 + d))
    gs_ref[...] = _sigmoid(proj(g0 + d, g0 + 2 * d))


def _in_proj(x2, g, w_bf, lng, lnb, sgu_width):
    t, d = x2.shape
    tm = PROJ_ROWS
    n_in = w_bf.shape[1]
    row = lambda w: pl.BlockSpec((tm, w), lambda i: (i, 0))
    const = lambda shape: pl.BlockSpec(shape, lambda i: (0,) * len(shape))
    out_shape = (
        jax.ShapeDtypeStruct((t, ATTN_WIDTH), F32),
        jax.ShapeDtypeStruct((t // MOBA_BLOCK, ATTN_WIDTH, MOBA_BLOCK), F32),
        jax.ShapeDtypeStruct((t // MOBA_BLOCK, ATTN_WIDTH, MOBA_BLOCK), BF16),
        jax.ShapeDtypeStruct((t, ATTN_WIDTH), BF16),
        jax.ShapeDtypeStruct((t // MOBA_BLOCK, 1, ATTN_WIDTH), F32),
        jax.ShapeDtypeStruct((t, sgu_width), F32),
        jax.ShapeDtypeStruct((t, sgu_width), F32),
        jax.ShapeDtypeStruct((t, d), F32),
        jax.ShapeDtypeStruct((t, d), F32),
    )
    out_specs = (
        row(ATTN_WIDTH),
        pl.BlockSpec((tm // MOBA_BLOCK, ATTN_WIDTH, MOBA_BLOCK), lambda i: (i, 0, 0)),
        pl.BlockSpec((tm // MOBA_BLOCK, ATTN_WIDTH, MOBA_BLOCK), lambda i: (i, 0, 0)),
        row(ATTN_WIDTH),
        pl.BlockSpec((tm // MOBA_BLOCK, 1, ATTN_WIDTH), lambda i: (i, 0, 0)),
        row(sgu_width), row(sgu_width), row(d), row(d),
    )
    return pl.pallas_call(
        _in_proj_kernel,
        grid=(t // tm,),
        in_specs=[row(d), const((1, d)), const((d, n_in)), const((1, sgu_width)), const((1, sgu_width))],
        out_specs=out_specs,
        out_shape=out_shape,
        compiler_params=pltpu.CompilerParams(dimension_semantics=("arbitrary",), vmem_limit_bytes=VMEM_LIMIT),
        name="in_proj",
    )(x2, g, w_bf, lng, lnb)


def _alibi_slopes():
    slopes = 2.0 ** (-8.0 * np.arange(1, N_HEADS + 1, dtype=np.float64) / N_HEADS)
    assert all(np.log2(s) == np.round(np.log2(s)) for s in slopes), "ALiBi slopes must be powers of two"
    return slopes


def _moba_bias_lane(head_in_tile, part, piece):
    return MOBA_BIAS_LANE0 + (head_in_tile * 2 + part) * len(LOG2E_PIECES) + piece


def _moba_key_table(s):
    nb = s // MOBA_BLOCK
    slopes = _alibi_slopes()
    n_pairs = N_HEADS // HEADS_PER_LANE_TILE
    table = np.zeros((nb + 1, n_pairs, LANES, MOBA_BLOCK), np.float32)
    offs = np.arange(MOBA_BLOCK, dtype=np.float32)
    assert HEADS_PER_LANE_TILE * nb <= MOBA_BIAS_LANE0, "block one-hot rows must not overlap the ALiBi rows"
    for j in range(nb):
        for hh in range(HEADS_PER_LANE_TILE):
            table[j, :, hh * nb + j, :] = 1.0
        for p in range(n_pairs):
            for hh in range(HEADS_PER_LANE_TILE):
                slope = slopes[p * HEADS_PER_LANE_TILE + hh]
                for piece in range(len(LOG2E_PIECES)):
                    table[j, p, _moba_bias_lane(hh, 0, piece), :] = slope * MOBA_BLOCK * j
                    table[j, p, _moba_bias_lane(hh, 1, piece), :] = slope * offs
    table[nb, :, MOBA_DUMMY_LANE, :] = 1.0
    as_bf16 = table.astype(BF16)
    assert np.array_equal(as_bf16.astype(np.float32), table), "bias table must be exact in bf16"
    return jnp.asarray(as_bf16.reshape(nb + 1, n_pairs * LANES, MOBA_BLOCK))


def _moba_kernel(q_ref, qt_ref, qn_ref, qtn_ref, kt_ref, v_ref, km_ref, ct_ref, causal_ref, o_ref, qaug_ref,
                 s_ref, mpart_ref, mrow_ref, acc_ref):
    i = pl.program_id(2)
    nb = kt_ref.shape[0]
    blk = MOBA_BLOCK
    lane = lax.broadcasted_iota(jnp.int32, (1, LANES), 1)
    n_groups = lax.div(i + MOBA_GROUP, MOBA_GROUP)
    heads = range(HEADS_PER_LANE_TILE)
    hmasks = [(lane >= HEAD_DIM * hh) & (lane < HEAD_DIM * (hh + 1)) for hh in heads]
    slot = lax.rem(i, 2)

    def build_query_operand(q, qt, tile, into):
        bid = lax.broadcasted_iota(jnp.int32, (nb, blk), 0)
        bid_f = bid.astype(F32)
        masks_t = []
        for hh in heads:
            km_h = jnp.where(hmasks[hh], km_ref[...], 0.0)
            gate = jnp.dot(km_h, qt, precision=lax.Precision.HIGHEST, preferred_element_type=F32)
            gate = jnp.where(bid < tile, gate, NEG_INF)
            blockmask = jnp.where(bid == tile, 0.0, MASK_VALUE)
            for kk in range(MOBA_TOPK):
                gmax = jnp.max(gate, axis=0, keepdims=True)
                first = jnp.min(jnp.where(gate == gmax, bid_f, float(nb)), axis=0, keepdims=True)
                valid = (tile > kk).astype(F32)
                first = first * valid + (valid - 1.0)
                hit = bid_f == first
                blockmask = jnp.where(hit, 0.0, blockmask)
                gate = jnp.where(hit, NEG_INF, gate)
            masks_t.append(blockmask)
        masks_t.append(jnp.zeros((LANES - len(masks_t) * nb, blk), F32))
        masks = jnp.concatenate(masks_t, axis=0).T

        for hh in heads:
            feats = jnp.where(lane == MOBA_DUMMY_LANE, MASK_VALUE, 0.0)
            for part in range(2):
                for piece, value in enumerate(LOG2E_PIECES):
                    feats = jnp.where(lane == _moba_bias_lane(hh, part, piece), value, feats)
            extra = jnp.where((lane >= hh * nb) & (lane < (hh + 1) * nb), masks, feats)
            qh = jnp.where(hmasks[hh], q, 0.0)
            qaug_ref[into, hh * blk:(hh + 1) * blk, 0:LANES] = (qh * (HEAD_DIM ** -0.5 * LOG2E)).astype(BF16)
            qaug_ref[into, hh * blk:(hh + 1) * blk, LANES:2 * LANES] = extra.astype(BF16)

    @pl.when(i == 0)
    def _():
        build_query_operand(q_ref[...], qt_ref[...], i, slot)

    def scores(jk, jc):
        rhs = jnp.concatenate([kt_ref[jk], ct_ref[jc]], axis=0)
        return jnp.dot(qaug_ref[slot], rhs, preferred_element_type=F32)

    def lane_halves_max(s):
        return jnp.maximum(s[:, 0:LANES], s[:, LANES:2 * LANES])

    n_rows = len(heads) * blk
    variants = [(groups, min(groups * MOBA_GROUP, nb)) for groups in range(1, pl.cdiv(nb, MOBA_GROUP) + 1)]

    for groups, n_blocks in variants:
        @pl.when(n_groups == groups)
        def _(n_blocks=n_blocks):
            part = None
            for j in range(n_blocks):
                s = scores(j, jnp.where(j <= i, j, nb)) + causal_ref[(i == j).astype(jnp.int32)]
                s_ref[j] = s
                part = lane_halves_max(s) if part is None else jnp.maximum(part, lane_halves_max(s))
            mpart_ref[...] = part

    ones = jnp.ones((blk, LANES), BF16)
    rowmax = jnp.max(mpart_ref[...], axis=1, keepdims=True)
    mrow_ref[...] = jnp.broadcast_to(rowmax, (n_rows, 2 * LANES))

    def weighted(j):
        p = jnp.exp2(s_ref[j] - mrow_ref[...]).astype(BF16)
        vaug = jnp.concatenate([v_ref[j * blk:(j + 1) * blk, :], ones], axis=1)
        return jnp.dot(p, vaug, preferred_element_type=F32)

    next_tile = jnp.minimum(i + 1, nb - 1)
    for groups, n_blocks in variants:
        @pl.when(n_groups == groups)
        def _(n_blocks=n_blocks):
            build_query_operand(qn_ref[...], qtn_ref[...], next_tile, 1 - slot)
            tot = weighted(0)
            for j in range(1, n_blocks):
                tot = tot + weighted(j)
            acc_ref[...] = tot

    out = jnp.zeros((blk, LANES), F32)
    for hh in heads:
        acc = acc_ref[hh * blk:(hh + 1) * blk, :]
        out = jnp.where(hmasks[hh], acc[:, 0:LANES] / acc[:, LANES:2 * LANES], out)
    o_ref[...] = out


def _moba(q, qt, kt, v, kmean, key_table):
    b, s, a = q.shape
    nb = s // MOBA_BLOCK
    n_pairs = a // LANES
    rows = HEADS_PER_LANE_TILE * MOBA_BLOCK
    row_in_block = np.arange(rows)[:, None] % MOBA_BLOCK
    causal = np.where(row_in_block >= np.arange(MOBA_BLOCK)[None, :], 0.0, MASK_VALUE).astype(np.float32)
    causal_tiles = jnp.asarray(np.stack([np.zeros_like(causal), causal]))
    nxt = lambda i: jnp.minimum(i + 1, nb - 1)
    return pl.pallas_call(
        _moba_kernel,
        grid=(b, n_pairs, nb),
        in_specs=[
            pl.BlockSpec((None, MOBA_BLOCK, LANES), lambda bi, p, i: (bi, i, p)),
            pl.BlockSpec((None, LANES, MOBA_BLOCK), lambda bi, p, i: (bi * nb + i, p, 0)),
            pl.BlockSpec((None, MOBA_BLOCK, LANES), lambda bi, p, i: (bi, nxt(i), p)),
            pl.BlockSpec((None, LANES, MOBA_BLOCK), lambda bi, p, i: (bi * nb + nxt(i), p, 0)),
            pl.BlockSpec((nb, LANES, MOBA_BLOCK), lambda bi, p, i: (bi, p, 0)),
            pl.BlockSpec((None, s, LANES), lambda bi, p, i: (bi, 0, p)),
            pl.BlockSpec((None, nb, LANES), lambda bi, p, i: (bi, 0, p)),
            pl.BlockSpec((nb + 1, LANES, MOBA_BLOCK), lambda bi, p, i: (0, p, 0)),
            pl.BlockSpec((2, rows, MOBA_BLOCK), lambda bi, p, i: (0, 0, 0)),
        ],
        out_specs=pl.BlockSpec((None, MOBA_BLOCK, LANES), lambda bi, p, i: (bi, i, p)),
        out_shape=jax.ShapeDtypeStruct((b, s, a), F32),
        scratch_shapes=[
            pltpu.VMEM((2, rows, 2 * LANES), BF16),
            pltpu.VMEM((nb + 1, rows, MOBA_BLOCK), F32),
            pltpu.VMEM((rows, LANES), F32),
            pltpu.VMEM((rows, 2 * LANES), F32),
            pltpu.VMEM((rows, 2 * LANES), F32),
        ],
        compiler_params=pltpu.CompilerParams(
            dimension_semantics=("arbitrary", "arbitrary", "arbitrary"), vmem_limit_bytes=VMEM_LIMIT),
        name="moba",
    )(q, qt, q, qt, kt, v, kmean, key_table, causal_tiles)


def _mix_kernel(attn_ref, u_ref, vvn_ref, ga_ref, gs_ref, x_ref,
                wao_ref, wso_ref, wmo_ref, wsp_ref, bsp_ref, fg_ref, wr_ref, br_ref,
                h1_ref, xn2_ref, ri_ref, rw_ref, tcnt_ref, post_ref, wcausal_ref):
    tm = x_ref.shape[0]
    ch = SGU_CHUNK

    @pl.when(pl.program_id(0) == 0)
    def _():
        tril = lax.broadcasted_iota(jnp.int32, (ch, ch), 0) >= lax.broadcasted_iota(jnp.int32, (ch, ch), 1)
        for g in range(SGU_GROUPS):
            wcausal_ref[g] = jnp.where(tril, wsp_ref[g], 0.0).astype(BF16)

    y_attn = jnp.dot(attn_ref[...].astype(BF16), wao_ref[...], preferred_element_type=F32)

    lane = lax.broadcasted_iota(jnp.int32, (1, LANES), 1)
    gdim = vvn_ref.shape[1] // SGU_GROUPS
    groups_per_tile = LANES // gdim
    w_causal = [wcausal_ref[g] for g in range(SGU_GROUPS)]
    n_chunks = tm // ch
    lane_in_tile = lax.bitwise_and(lax.broadcasted_iota(jnp.int32, (1, n_chunks * LANES), 1), LANES - 1)
    col_blocks = []
    for ct in range(vvn_ref.shape[1] // LANES):
        vp = jnp.concatenate(
            [vvn_ref[c * ch:(c + 1) * ch, ct * LANES:(ct + 1) * LANES] for c in range(n_chunks)], axis=1)
        acc = jnp.zeros((ch, n_chunks * LANES), F32)
        for gg in range(groups_per_tile):
            gmask = (lane_in_tile >= gdim * gg) & (lane_in_tile < gdim * (gg + 1))
            vm = jnp.where(gmask, vp, 0.0).astype(BF16)
            acc = acc + jnp.dot(w_causal[ct * groups_per_tile + gg], vm, preferred_element_type=F32)
        col_blocks.append(acc)
    rows = [jnp.concatenate([blk[:, c * LANES:(c + 1) * LANES] for blk in col_blocks], axis=1) + bsp_ref[...]
            for c in range(n_chunks)]
    mixed = jnp.concatenate(rows, axis=0)
    sgu = u_ref[...] * mixed
    y_sgu = jnp.dot(sgu.astype(BF16), wso_ref[...], preferred_element_type=F32)

    merged = ga_ref[...] * y_attn + gs_ref[...] * y_sgu
    h1 = x_ref[...] + jnp.dot(merged.astype(BF16), wmo_ref[...], preferred_element_type=F32)
    h1_ref[...] = h1
    xn2 = h1 * lax.rsqrt(jnp.mean(h1 * h1, axis=-1, keepdims=True) + NORM_EPS) * fg_ref[...]
    xn2_ref[...] = xn2.astype(BF16)

    def split(a):
        hi = a.astype(BF16)
        return hi, (a - hi.astype(F32)).astype(BF16)

    x_hi, x_lo = split(xn2)
    w_hi, w_lo = split(wr_ref[...])
    hi_both = jnp.dot(x_hi, jnp.concatenate([w_hi, w_lo], axis=1), preferred_element_type=F32)
    logits = (hi_both[:, 0:LANES] + jnp.dot(x_lo, w_hi, preferred_element_type=F32)
              + hi_both[:, LANES:2 * LANES]) + br_ref[...]
    work = jnp.where(lane < N_EXPERTS, logits, NEG_INF)
    lane_f = lane.astype(F32)
    vals, idxs = [], []
    for _ in range(EXPERT_TOPK):
        vmax = jnp.max(work, axis=1, keepdims=True)
        first = jnp.min(jnp.where(work == vmax, lane_f, float(LANES)), axis=1, keepdims=True)
        vals.append(vmax)
        idxs.append(first)
        work = jnp.where(lane_f == first, NEG_INF, work)
    exps = [jnp.exp(v - vals[0]) for v in vals]
    denom = exps[0]
    for e in exps[1:]:
        denom = denom + e
    chosen = jnp.zeros((tm, LANES), F32)
    for first in idxs:
        chosen = jnp.where(lane_f == first, 1.0, chosen)

    strict_lower = (lax.broadcasted_iota(jnp.int32, (tm, tm), 0)
                    > lax.broadcasted_iota(jnp.int32, (tm, tm), 1))
    before = jnp.dot(jnp.where(strict_lower, 1.0, 0.0).astype(BF16), chosen.astype(BF16),
                     preferred_element_type=F32)
    counts = jnp.broadcast_to(jnp.sum(chosen, axis=0, keepdims=True), tcnt_ref.shape)
    lower_expert = (lax.broadcasted_iota(jnp.int32, (LANES, LANES), 0)
                    < lax.broadcasted_iota(jnp.int32, (LANES, LANES), 1))
    expert_start = jnp.dot(counts, jnp.where(lower_expert, 1.0, 0.0), precision=lax.Precision.HIGHEST,
                           preferred_element_type=F32)[0:1, :]
    sorted_pos = before + expert_start
    ri = jnp.zeros((tm, LANES), jnp.int32)
    rw = jnp.zeros((tm, LANES), F32)
    pos_lanes = jnp.zeros((tm, LANES), F32)
    for kk in range(EXPERT_TOPK):
        pos = jnp.sum(jnp.where(lane_f == idxs[kk], sorted_pos, 0.0), axis=1, keepdims=True)
        ri = jnp.where(lane == kk, idxs[kk].astype(jnp.int32), ri)
        ri = jnp.where(lane == EXPERT_TOPK + kk, pos.astype(jnp.int32), ri)
        rw = jnp.where(lane == kk, exps[kk] / denom, rw)
        pos_lanes = jnp.where(lane == kk, pos, pos_lanes)
    ri_ref[...] = ri
    rw_ref[...] = rw
    tcnt_ref[...] = counts
    post_ref[...] = pos_lanes.T[0:SUBLANES, :]


def _mix(attn, u, vvn, ga, gs, x2, wao, wso, wmo, wsp, bsp_full, fg, wr_pad, br_pad):
    t, d = x2.shape
    tm = ROUTE_TOKENS
    row = lambda w: pl.BlockSpec((tm, w), lambda i: (i, 0))
    const = lambda shape: pl.BlockSpec(shape, lambda i: (0,) * len(shape))
    ins = (attn, u, vvn, ga, gs, x2, wao, wso, wmo, wsp, bsp_full, fg, wr_pad, br_pad)
    in_specs = [row(a.shape[1]) for a in ins[:6]] + [const(a.shape) for a in ins[6:]]
    out_shape = (
        jax.ShapeDtypeStruct((t, d), F32),
        jax.ShapeDtypeStruct((t, d), BF16),
        jax.ShapeDtypeStruct((t, LANES), jnp.int32),
        jax.ShapeDtypeStruct((t, LANES), F32),
        jax.ShapeDtypeStruct((t // tm * SUBLANES, LANES), F32),
        jax.ShapeDtypeStruct((t // tm * SUBLANES, tm), F32),
    )
    assert d == SUBLANES * LANES, "a token row is stored as one (8, 128) tile"
    per_tile = lambda w: pl.BlockSpec((SUBLANES, w), lambda i: (i, 0))
    out_specs = (row(d), row(d), row(LANES), row(LANES), per_tile(LANES), per_tile(tm))
    return pl.pallas_call(
        _mix_kernel,
        grid=(t // tm,),
        in_specs=in_specs,
        out_specs=out_specs,
        out_shape=out_shape,
        scratch_shapes=[pltpu.VMEM(wsp.shape, BF16)],
        compiler_params=pltpu.CompilerParams(dimension_semantics=("arbitrary",), vmem_limit_bytes=BIG_VMEM_LIMIT),
        name="mix_route",
    )(*ins)


def _run_pieces(n, body):
    off = jnp.int32(0)
    for bit in reversed(range(ROUTE_TOKENS.bit_length())):
        size = 1 << bit
        take = lax.bitwise_and(n, size) != 0

        @pl.when(take)
        def _(off=off, size=size):
            body(off, size)

        off = off + jnp.where(take, size, 0)


def _token_rows(ref, r, n, rows_per_token=SUBLANES):
    return ref.at[pl.ds(pl.multiple_of(r * rows_per_token, rows_per_token), n * rows_per_token)]


def _pack_bf16_pairs(value):
    half = value.shape[1] // 2
    lo = lax.shift_right_logical(lax.bitcast_convert_type(value[:, :half], jnp.uint32), jnp.uint32(16))
    hi = lax.bitwise_and(lax.bitcast_convert_type(value[:, half:], jnp.uint32), jnp.uint32(0xFFFF0000))
    return lax.bitwise_or(lo, hi)


def _unpack_bf16_pairs(words):
    lo = lax.bitcast_convert_type(lax.shift_left(words, jnp.uint32(16)), F32)
    hi = lax.bitcast_convert_type(lax.bitwise_and(words, jnp.uint32(0xFFFF0000)), F32)
    return jnp.concatenate([lo, hi], axis=1).astype(BF16)


def _dispatch_kernel(len_ref, tpos_ref, dst_ref, padlo_ref, padlen_ref, nused_ref, post_ref, x_ref,
                     xs_hbm, sorted_ref, zeros_ref, sems, zsem):
    step = pl.program_id(0)
    n_steps = pl.num_programs(0)
    tt = x_ref.shape[0]
    n_assign = tt * EXPERT_TOPK
    rpt = PACKED_ROWS
    tile_rows = zeros_ref.shape[0] // rpt
    slot = lax.rem(step, 2)

    def start_runs(tile, buf):
        def per_expert(e, carry):
            k = tile * N_EXPERTS + e

            def piece(off, size):
                pltpu.make_async_copy(_token_rows(sorted_ref.at[buf], tpos_ref[k] + off, size, rpt),
                                      _token_rows(xs_hbm, dst_ref[k] + off, size, rpt), sems.at[buf]).start()

            _run_pieces(len_ref[k], piece)
            return carry

        lax.fori_loop(0, N_EXPERTS, per_expert, 0)

    def wait_runs(buf):
        whole = sorted_ref.at[buf]
        pltpu.make_async_copy(whole, _token_rows(xs_hbm, 0, n_assign, rpt), sems.at[buf]).wait()

    @pl.when(step == 0)
    def _():
        zeros_ref[...] = jnp.zeros_like(zeros_ref)

        def zero_fill(wait):
            def finish(cp):
                if wait:
                    cp.wait()
                else:
                    cp.start()

            def tail_tile(tl, carry):
                finish(pltpu.make_async_copy(zeros_ref, _token_rows(xs_hbm, tl * tile_rows, tile_rows, rpt), zsem))
                return carry

            lax.fori_loop(nused_ref[0], xs_hbm.shape[0] // (tile_rows * rpt), tail_tile, 0)

            def per_expert(e, carry):
                def piece(off, size):
                    finish(pltpu.make_async_copy(_token_rows(zeros_ref, 0, size, rpt),
                                                 _token_rows(xs_hbm, padlo_ref[e] + off, size, rpt), zsem))

                _run_pieces(padlen_ref[e], piece)
                return carry

            lax.fori_loop(0, N_EXPERTS, per_expert, 0)

        zero_fill(wait=False)
        zero_fill(wait=True)

    @pl.when(step >= 2)
    def _():
        wait_runs(slot)

    x = x_ref[...]
    pos_by_choice = [post_ref[kk:kk + 1, :] for kk in range(EXPERT_TOPK)]
    chunk = MOBA_BLOCK
    for rc in range(n_assign // chunk):
        r = (lax.broadcasted_iota(jnp.int32, (chunk, 1), 0) + rc * chunk).astype(F32)
        select = jnp.zeros((chunk, tt), F32)
        for pos in pos_by_choice:
            select = jnp.where(r == pos, 1.0, select)
        rows = jnp.dot(select.astype(BF16), x, preferred_element_type=F32)
        words = _pack_bf16_pairs(rows)
        for c in range(rpt):
            sorted_ref[slot, pl.ds(rc * chunk * rpt + c, chunk, stride=rpt), :] = words[:, c * LANES:(c + 1) * LANES]
    start_runs(step, slot)

    @pl.when(step == n_steps - 1)
    def _():
        @pl.when(step >= 1)
        def _():
            wait_runs(1 - slot)

        wait_runs(slot)


def _dispatch(run_len, run_tpos, run_dst, padlo, padlen, n_used, pos_t, xn2, n_rows):
    t, d = xn2.shape
    n_assign = ROUTE_TOKENS * EXPERT_TOPK
    grid_spec = pltpu.PrefetchScalarGridSpec(
        num_scalar_prefetch=6,
        grid=(t // ROUTE_TOKENS,),
        in_specs=[pl.BlockSpec((SUBLANES, ROUTE_TOKENS), lambda c, *_: (c, 0)),
                  pl.BlockSpec((ROUTE_TOKENS, d), lambda c, *_: (c, 0))],
        out_specs=pl.BlockSpec(memory_space=pl.ANY),
        scratch_shapes=[
            pltpu.VMEM((2, n_assign * PACKED_ROWS, LANES), jnp.uint32),
            pltpu.VMEM((EXPERT_ROWS * PACKED_ROWS, LANES), jnp.uint32),
            pltpu.SemaphoreType.DMA((2,)),
            pltpu.SemaphoreType.DMA(()),
        ],
    )
    assert d == 2 * PACKED_ROWS * LANES
    return pl.pallas_call(
        _dispatch_kernel,
        grid_spec=grid_spec,
        out_shape=jax.ShapeDtypeStruct((n_rows * PACKED_ROWS, LANES), jnp.uint32),
        compiler_params=pltpu.CompilerParams(dimension_semantics=("arbitrary",), vmem_limit_bytes=VMEM_LIMIT),
        name="dispatch",
    )(run_len, run_tpos, run_dst, padlo, padlen, n_used, pos_t, xn2)


def _expert_kernel(be_ref, nused_ref, x_ref, wgu_ref, bg_ref, bu_ref, wd_ref, bd_ref, y_ref, wgu_bf, wd_bf):
    t = pl.program_id(0)
    rows = y_ref.shape[0] // SUBLANES
    de = wd_ref.shape[0]
    grp = 2 * LANES
    active = t < nused_ref[0]
    fresh = jnp.logical_and(
        active, jnp.logical_or(t == 0, be_ref[t] != be_ref[jnp.maximum(t - 1, 0)]))

    @pl.when(fresh)
    def _():
        src = lax.broadcasted_iota(jnp.int32, (grp, grp), 0)
        dst = lax.broadcasted_iota(jnp.int32, (grp, grp), 1)
        wanted = jnp.where(dst < LANES, 2 * dst, 2 * (dst - LANES) + 1)
        perm = jnp.where(src == wanted, 1.0, 0.0).astype(BF16)
        for g in range(wgu_ref.shape[1] // grp):
            cols = slice(g * grp, (g + 1) * grp)
            wgu_bf[:, cols] = jnp.dot(wgu_ref[:, cols].astype(BF16), perm,
                                      preferred_element_type=F32).astype(BF16)
        wd_bf[...] = wd_ref[...].astype(BF16)

    @pl.when(active)
    def _():
        words = jnp.concatenate([x_ref[pl.ds(c, rows, stride=PACKED_ROWS), :] for c in range(PACKED_ROWS)], axis=1)
        xb = _unpack_bf16_pairs(words)
        gu = jnp.dot(xb, wgu_bf[...], preferred_element_type=F32)
        hid = []
        for g in range(de // LANES):
            cols = slice(g * LANES, (g + 1) * LANES)
            gate = gu[:, g * grp:g * grp + LANES] + bg_ref[:, cols]
            up = gu[:, g * grp + LANES:(g + 1) * grp] + bu_ref[:, cols]
            gate = jnp.minimum(gate, SWIGLU_LIMIT)
            up = jnp.clip(up, -SWIGLU_LIMIT, SWIGLU_LIMIT)
            hid.append((gate * _sigmoid(SWIGLU_ALPHA * gate) * (up + 1.0)).astype(BF16))
        y = jnp.dot(jnp.concatenate(hid, axis=1), wd_bf[...], preferred_element_type=F32) + bd_ref[...]
        _store_token_rows(y_ref, y)

    @pl.when(jnp.logical_not(active))
    def _():
        y_ref[...] = jnp.zeros_like(y_ref)


def _experts(block_expert, n_used, xs_rows, wgu, bg, bu, wd, bd):
    n_rows = xs_rows.shape[0] // PACKED_ROWS
    _, d, de2 = wgu.shape
    de = wd.shape[1]
    n_tiles = n_rows // EXPERT_ROWS
    tile_rows = EXPERT_ROWS * SUBLANES
    per_expert = lambda t, be, nu: (be[t], 0, 0)
    grid_spec = pltpu.PrefetchScalarGridSpec(
        num_scalar_prefetch=2,
        grid=(n_tiles,),
        in_specs=[
            pl.BlockSpec((EXPERT_ROWS * PACKED_ROWS, LANES),
                         lambda t, be, nu: (jnp.minimum(t, jnp.maximum(nu[0] - 1, 0)), 0)),
            pl.BlockSpec((None, d, de2), per_expert),
            pl.BlockSpec((None, 1, de), per_expert),
            pl.BlockSpec((None, 1, de), per_expert),
            pl.BlockSpec((None, de, d), per_expert),
            pl.BlockSpec((None, 1, d), per_expert),
        ],
        out_specs=pl.BlockSpec((tile_rows, LANES), lambda t, be, nu: (t, 0)),
        scratch_shapes=[pltpu.VMEM((d, de2), BF16), pltpu.VMEM((de, d), BF16)],
    )
    return pl.pallas_call(
        _expert_kernel,
        grid_spec=grid_spec,
        out_shape=jax.ShapeDtypeStruct((n_rows * SUBLANES, LANES), F32),
        compiler_params=pltpu.CompilerParams(
            dimension_semantics=("arbitrary",), vmem_limit_bytes=BIG_VMEM_LIMIT),
        name="experts",
    )(block_expert, n_used, xs_rows, wgu, bg, bu, wd, bd)


def _combine_kernel(len_ref, tpos_ref, src_ref, row_ref, w_ref, h1_ref, fg_ref, ys_hbm,
                    o_ref, buf_ref, h1rows_ref, outrows_ref, sems):
    step = pl.program_id(0)
    n_steps = pl.num_programs(0)
    tt = h1_ref.shape[0]
    slot = lax.rem(step, 2)

    def start_runs(tile, buf):
        def per_expert(e, carry):
            k = tile * N_EXPERTS + e

            def piece(off, size):
                pltpu.make_async_copy(_token_rows(ys_hbm, src_ref[k] + off, size),
                                      _token_rows(buf_ref.at[buf], tpos_ref[k] + off, size), sems.at[buf]).start()

            _run_pieces(len_ref[k], piece)
            return carry

        lax.fori_loop(0, N_EXPERTS, per_expert, 0)

    def wait_runs(buf):
        whole = buf_ref.at[buf]
        pltpu.make_async_copy(_token_rows(ys_hbm, 0, tt * EXPERT_TOPK), whole, sems.at[buf]).wait()

    @pl.when(step == 0)
    def _():
        start_runs(0, 0)

    @pl.when(step + 1 < n_steps)
    def _():
        start_runs(step + 1, 1 - slot)

    wait_runs(slot)

    _store_token_rows(h1rows_ref, h1_ref[...])

    def one_token(tok, carry):
        acc = h1rows_ref[pl.ds(pl.multiple_of(tok * SUBLANES, SUBLANES), SUBLANES), :]
        for kk in range(EXPERT_TOPK):
            a = tok * EXPERT_TOPK + kk
            acc = acc + w_ref[a] * buf_ref[slot, pl.ds(pl.multiple_of(row_ref[a], SUBLANES), SUBLANES), :]
        outrows_ref[pl.ds(pl.multiple_of(tok * SUBLANES, SUBLANES), SUBLANES), :] = acc
        return carry

    lax.fori_loop(0, tt, one_token, 0, unroll=32)
    h = _load_token_rows(outrows_ref, tt)
    o_ref[...] = h * lax.rsqrt(jnp.mean(h * h, axis=-1, keepdims=True) + NORM_EPS) * fg_ref[...]


def _combine(run_len, run_tpos, run_src, buf_row, w_flat, h1, fg, ys_rows):
    t, d = h1.shape
    tt = ROUTE_TOKENS
    n_assign = tt * EXPERT_TOPK
    smem_blk = pl.BlockSpec((n_assign,), lambda c, *_: (c,), memory_space=pltpu.SMEM)
    grid_spec = pltpu.PrefetchScalarGridSpec(
        num_scalar_prefetch=3,
        grid=(t // tt,),
        in_specs=[
            smem_blk, smem_blk,
            pl.BlockSpec((tt, d), lambda c, *_: (c, 0)),
            pl.BlockSpec((1, d), lambda c, *_: (0, 0)),
            pl.BlockSpec(memory_space=pl.ANY),
        ],
        out_specs=pl.BlockSpec((tt, d), lambda c, *_: (c, 0)),
        scratch_shapes=[
            pltpu.VMEM((2, n_assign * SUBLANES, LANES), F32),
            pltpu.VMEM((tt * SUBLANES, LANES), F32),
            pltpu.VMEM((tt * SUBLANES, LANES), F32),
            pltpu.SemaphoreType.DMA((2,)),
        ],
    )
    return pl.pallas_call(
        _combine_kernel,
        grid_spec=grid_spec,
        out_shape=jax.ShapeDtypeStruct((t, d), F32),
        compiler_params=pltpu.CompilerParams(dimension_semantics=("arbitrary",), vmem_limit_bytes=VMEM_LIMIT),
        name="combine",
    )(run_len, run_tpos, run_src, buf_row, w_flat, h1, fg, ys_rows)


def _layer(h, mix_norm_g, w_in, w_attn_out, sgu_ln_g, sgu_ln_b, w_spatial, b_spatial, w_sgu_out,
           w_mix_out, ffn_norm_g, w_router, b_router, w_gate_up, b_gate_up, w_down, b_down, out_g):
    b, s, d = h.shape
    t = b * s
    sgu_width = sgu_ln_g.shape[0]
    x2 = h.reshape(t, d)

    q, qt, kt, v, kmean, u, vvn, ga, gs = _in_proj(
        x2, mix_norm_g.reshape(1, d), w_in.astype(BF16), sgu_ln_g.reshape(1, sgu_width),
        sgu_ln_b.reshape(1, sgu_width), sgu_width)

    nb = s // MOBA_BLOCK
    attn = _moba(q.reshape(b, s, ATTN_WIDTH), qt, kt, v.reshape(b, s, ATTN_WIDTH),
                 kmean.reshape(b, nb, ATTN_WIDTH), _moba_key_table(s)).reshape(t, ATTN_WIDTH)

    gdim = sgu_width // SGU_GROUPS
    bsp_full = jnp.repeat(b_spatial.T, gdim, axis=1)
    wr_pad = jnp.zeros((d, LANES), F32).at[:, :N_EXPERTS].set(w_router)
    br_pad = jnp.zeros((1, LANES), F32).at[0, :N_EXPERTS].set(b_router)
    h1, xn2, ri, rw, tile_cnt, pos_t = _mix(
        attn, u, vvn, ga, gs, x2, w_attn_out.astype(BF16), w_sgu_out.astype(BF16), w_mix_out.astype(BF16),
        w_spatial, bsp_full, ffn_norm_g.reshape(1, d), wr_pad, br_pad)

    n_assign = t * EXPERT_TOPK
    n_tiles = -(-(n_assign + N_EXPERTS * (EXPERT_ROWS - 1)) // EXPERT_ROWS)
    n_rows = n_tiles * EXPERT_ROWS
    run_len = tile_cnt[::SUBLANES, :N_EXPERTS].astype(jnp.int32)
    counts = jnp.sum(run_len, axis=0)
    padded = (counts + EXPERT_ROWS - 1) // EXPERT_ROWS * EXPERT_ROWS
    pad_end = jnp.cumsum(padded)
    starts = pad_end - padded
    run_tpos = jnp.cumsum(run_len, axis=1) - run_len
    run_row = starts[None, :] + jnp.cumsum(run_len, axis=0) - run_len
    tile_start = jnp.arange(n_tiles, dtype=jnp.int32) * EXPERT_ROWS
    block_expert = jnp.minimum(
        jnp.sum((pad_end[None, :] <= tile_start[:, None]).astype(jnp.int32), axis=1), N_EXPERTS - 1)
    n_used = (pad_end[-1:] // EXPERT_ROWS).astype(jnp.int32)
    buf_row = ri[:, EXPERT_TOPK:2 * EXPERT_TOPK].reshape(n_assign) * SUBLANES
    tables = (run_len.reshape(-1), run_tpos.reshape(-1), run_row.reshape(-1))

    xs_rows = _dispatch(*tables, starts + counts, padded - counts, n_used, pos_t, xn2, n_rows)

    de = w_down.shape[1]
    bg = b_gate_up[:, 0::2].reshape(N_EXPERTS, 1, de)
    bu = b_gate_up[:, 1::2].reshape(N_EXPERTS, 1, de)
    ys_rows = _experts(block_expert, n_used, xs_rows, w_gate_up, bg, bu, w_down, b_down.reshape(N_EXPERTS, 1, d))

    out = _combine(*tables, buf_row, rw[:, :EXPERT_TOPK].reshape(n_assign), h1, out_g.reshape(1, d), ys_rows)
    return out.reshape(b, s, d)


def kernel(x, mix_norm_g, w_in, w_attn_out, sgu_ln_g, sgu_ln_b, w_spatial, b_spatial, w_sgu_out, w_mix_out,
           ffn_norm_g, w_router, b_router, w_gate_up, b_gate_up, w_down, b_down, final_norm_g):
    depth = w_in.shape[0]
    assert depth == 1, "the final RMSNorm is fused into the single layer's combine step"
    return _layer(x, mix_norm_g[0], w_in[0], w_attn_out[0], sgu_ln_g[0], sgu_ln_b[0], w_spatial[0],
                  b_spatial[0], w_sgu_out[0], w_mix_out[0], ffn_norm_g[0], w_router[0], b_router[0],
                  w_gate_up[0], b_gate_up[0], w_down[0], b_down[0], final_norm_g)
```

```python
import jax
import jax.numpy as jnp
import numpy as np
from jax import lax
from jax.experimental import pallas as pl
from jax.experimental.pallas import tpu as pltpu

F32 = jnp.float32
BF16 = jnp.bfloat16
NEG_INF = float("-inf")
MASK_VALUE = -1e30


def _bf16_pieces(x, n=3):
    pieces = []
    for _ in range(n):
        p = float(np.asarray(x, np.float32).astype(BF16).astype(np.float32))
        pieces.append(p)
        x = x - p
    return pieces


LOG2E = 1.4426950408889634
LOG2E_PIECES = _bf16_pieces(LOG2E)

N_HEADS = 8
HEAD_DIM = 64
ATTN_WIDTH = N_HEADS * HEAD_DIM
MOBA_BLOCK = 256
MOBA_TOPK = 3
SGU_CHUNK = 128
SGU_GROUPS = 8
N_EXPERTS = 32
EXPERT_TOPK = 4
SWIGLU_LIMIT = 7.0
SWIGLU_ALPHA = 1.702
NORM_EPS = 1e-5

LANES = 128
SUBLANES = 8
PACKED_ROWS = 4
HEADS_PER_LANE_TILE = LANES // HEAD_DIM
MOBA_GROUP = 1
MOBA_BIAS_LANE0 = 64
MOBA_DUMMY_LANE = LANES - 1
PROJ_ROWS = 256
EXPERT_ROWS = 720
ROUTE_TOKENS = 512
VMEM_LIMIT = 48 * 1024 * 1024
BIG_VMEM_LIMIT = 56 * 1024 * 1024


def _sigmoid(x):
    return 1.0 / (1.0 + jnp.exp(-x))


def _store_token_rows(ref, value, base=0):
    n = value.shape[0]
    for c in range(value.shape[1] // LANES):
        ref[pl.ds(base + c, n, stride=SUBLANES), :] = value[:, c * LANES:(c + 1) * LANES]


def _load_token_rows(ref, n, base=0):
    return jnp.concatenate([ref[pl.ds(base + c, n, stride=SUBLANES), :] for c in range(SUBLANES)], axis=1)


def _gelu_exact(x):
    return 0.5 * x * (1.0 + lax.erf(x * (0.5 ** 0.5)))


def _in_proj_kernel(x_ref, g_ref, w_ref, lng_ref, lnb_ref,
                    q_ref, qt_ref, kt_ref, v_ref, kmean_ref, u_ref, vvn_ref, ga_ref, gs_ref):
    x = x_ref[...]
    xn = x * lax.rsqrt(jnp.mean(x * x, axis=-1, keepdims=True) + NORM_EPS) * g_ref[...]
    xb = xn.astype(BF16)

    def proj(lo, hi):
        return jnp.dot(xb, w_ref[:, lo:hi], preferred_element_type=F32)

    a = ATTN_WIDTH
    sw = u_ref.shape[1]
    d = ga_ref.shape[1]
    qf = proj(0, a)
    q_ref[...] = qf
    kf = proj(a, 2 * a)
    for j in range(kf.shape[0] // MOBA_BLOCK):
        qt_ref[j] = qf[j * MOBA_BLOCK:(j + 1) * MOBA_BLOCK].T
        kblk = kf[j * MOBA_BLOCK:(j + 1) * MOBA_BLOCK]
        kt_ref[j] = kblk.T.astype(BF16)
        kmean_ref[j] = jnp.mean(kblk, axis=0, keepdims=True)
    v_ref[...] = proj(2 * a, 3 * a).astype(BF16)
    z0 = 3 * a
    u_ref[...] = _gelu_exact(proj(z0, z0 + sw))
    zv = _gelu_exact(proj(z0 + sw, z0 + 2 * sw))
    mu = jnp.mean(zv, axis=-1, keepdims=True)
    zc = zv - mu
    var = jnp.mean(zc * zc, axis=-1, keepdims=True)
    vvn_ref[...] = zc * lax.rsqrt(var + NORM_EPS) * lng_ref[...] + lnb_ref[...]
    g0 = z0 + 2 * sw
    ga_ref[...] = _sigmoid(proj(g0, g0 + d))
    gs_ref[...] = _sigmoid(proj(g0 + d, g0 + 2 * d))


def _in_proj(x2, g, w_bf, lng, lnb, sgu_width):
    t, d = x2.shape
    tm = PROJ_ROWS
    n_in = w_bf.shape[1]
    row = lambda w: pl.BlockSpec((tm, w), lambda i: (i, 0))
    const = lambda shape: pl.BlockSpec(shape, lambda i: (0,) * len(shape))
    out_shape = (
        jax.ShapeDtypeStruct((t, ATTN_WIDTH), F32),
        jax.ShapeDtypeStruct((t // MOBA_BLOCK, ATTN_WIDTH, MOBA_BLOCK), F32),
        jax.ShapeDtypeStruct((t // MOBA_BLOCK, ATTN_WIDTH, MOBA_BLOCK), BF16),
        jax.ShapeDtypeStruct((t, ATTN_WIDTH), BF16),
        jax.ShapeDtypeStruct((t // MOBA_BLOCK, 1, ATTN_WIDTH), F32),
        jax.ShapeDtypeStruct((t, sgu_width), F32),
        jax.ShapeDtypeStruct((t, sgu_width), F32),
        jax.ShapeDtypeStruct((t, d), F32),
        jax.ShapeDtypeStruct((t, d), F32),
    )
    out_specs = (
        row(ATTN_WIDTH),
        pl.BlockSpec((tm // MOBA_BLOCK, ATTN_WIDTH, MOBA_BLOCK), lambda i: (i, 0, 0)),
        pl.BlockSpec((tm // MOBA_BLOCK, ATTN_WIDTH, MOBA_BLOCK), lambda i: (i, 0, 0)),
        row(ATTN_WIDTH),
        pl.BlockSpec((tm // MOBA_BLOCK, 1, ATTN_WIDTH), lambda i: (i, 0, 0)),
        row(sgu_width), row(sgu_width), row(d), row(d),
    )
    return pl.pallas_call(
        _in_proj_kernel,
        grid=(t // tm,),
        in_specs=[row(d), const((1, d)), const((d, n_in)), const((1, sgu_width)), const((1, sgu_width))],
        out_specs=out_specs,
        out_shape=out_shape,
        compiler_params=pltpu.CompilerParams(dimension_semantics=("arbitrary",), vmem_limit_bytes=VMEM_LIMIT),
        name="in_proj",
    )(x2, g, w_bf, lng, lnb)


def _alibi_slopes():
    slopes = 2.0 ** (-8.0 * np.arange(1, N_HEADS + 1, dtype=np.float64) / N_HEADS)
    assert all(np.log2(s) == np.round(np.log2(s)) for s in slopes), "ALiBi slopes must be powers of two"
    return slopes


def _moba_bias_lane(head_in_tile, part, piece):
    return MOBA_BIAS_LANE0 + (head_in_tile * 2 + part) * len(LOG2E_PIECES) + piece


def _moba_key_table(s):
    nb = s // MOBA_BLOCK
    slopes = _alibi_slopes()
    n_pairs = N_HEADS // HEADS_PER_LANE_TILE
    table = np.zeros((nb + 1, n_pairs, LANES, MOBA_BLOCK), np.float32)
    offs = np.arange(MOBA_BLOCK, dtype=np.float32)
    assert HEADS_PER_LANE_TILE * nb <= MOBA_BIAS_LANE0, "block one-hot rows must not overlap the ALiBi rows"
    for j in range(nb):
        for hh in range(HEADS_PER_LANE_TILE):
            table[j, :, hh * nb + j, :] = 1.0
        for p in range(n_pairs):
            for hh in range(HEADS_PER_LANE_TILE):
                slope = slopes[p * HEADS_PER_LANE_TILE + hh]
                for piece in range(len(LOG2E_PIECES)):
                    table[j, p, _moba_bias_lane(hh, 0, piece), :] = slope * MOBA_BLOCK * j
                    table[j, p, _moba_bias_lane(hh, 1, piece), :] = slope * offs
    table[nb, :, MOBA_DUMMY_LANE, :] = 1.0
    as_bf16 = table.astype(BF16)
    assert np.array_equal(as_bf16.astype(np.float32), table), "bias table must be exact in bf16"
    return jnp.asarray(as_bf16.reshape(nb + 1, n_pairs * LANES, MOBA_BLOCK))


def _moba_kernel(q_ref, qt_ref, qn_ref, qtn_ref, kt_ref, v_ref, km_ref, ct_ref, causal_ref, o_ref, qaug_ref,
                 s_ref, mrow_ref, acc_ref):
    i = pl.program_id(2)
    nb = kt_ref.shape[0]
    blk = MOBA_BLOCK
    lane = lax.broadcasted_iota(jnp.int32, (1, LANES), 1)
    n_groups = lax.div(i + MOBA_GROUP, MOBA_GROUP)
    heads = range(HEADS_PER_LANE_TILE)
    hmasks = [(lane >= HEAD_DIM * hh) & (lane < HEAD_DIM * (hh + 1)) for hh in heads]
    slot = lax.rem(i, 2)

    def build_query_operand(q, qt, tile, into):
        bid = lax.broadcasted_iota(jnp.int32, (nb, blk), 0)
        bid_f = bid.astype(F32)
        masks_t = []
        for hh in heads:
            km_h = jnp.where(hmasks[hh], km_ref[...], 0.0)
            gate = jnp.dot(km_h, qt, precision=lax.Precision.HIGHEST, preferred_element_type=F32)
            gate = jnp.where(bid < tile, gate, NEG_INF)
            blockmask = jnp.where(bid == tile, 0.0, MASK_VALUE)
            for kk in range(MOBA_TOPK):
                gmax = jnp.max(gate, axis=0, keepdims=True)
                first = jnp.min(jnp.where(gate == gmax, bid_f, float(nb)), axis=0, keepdims=True)
                valid = (tile > kk).astype(F32)
                first = first * valid + (valid - 1.0)
                hit = bid_f == first
                blockmask = jnp.where(hit, 0.0, blockmask)
                gate = jnp.where(hit, NEG_INF, gate)
            masks_t.append(blockmask)
        masks_t.append(jnp.zeros((LANES - len(masks_t) * nb, blk), F32))
        masks = jnp.concatenate(masks_t, axis=0).T

        for hh in heads:
            feats = jnp.where(lane == MOBA_DUMMY_LANE, MASK_VALUE, 0.0)
            for part in range(2):
                for piece, value in enumerate(LOG2E_PIECES):
                    feats = jnp.where(lane == _moba_bias_lane(hh, part, piece), value, feats)
            extra = jnp.where((lane >= hh * nb) & (lane < (hh + 1) * nb), masks, feats)
            qh = jnp.where(hmasks[hh], q, 0.0)
            qaug_ref[into, hh * blk:(hh + 1) * blk, 0:LANES] = (qh * (HEAD_DIM ** -0.5 * LOG2E)).astype(BF16)
            qaug_ref[into, hh * blk:(hh + 1) * blk, LANES:2 * LANES] = extra.astype(BF16)

    @pl.when(i == 0)
    def _():
        build_query_operand(q_ref[...], qt_ref[...], i, slot)

    def scores(jk, jc):
        rhs = jnp.concatenate([kt_ref[jk], ct_ref[jc]], axis=0)
        return jnp.dot(qaug_ref[slot], rhs, preferred_element_type=F32)

    def lane_halves_max(s):
        return jnp.maximum(s[:, 0:LANES], s[:, LANES:2 * LANES])

    n_rows = len(heads) * blk
    variants = [(groups, min(groups * MOBA_GROUP, nb)) for groups in range(1, pl.cdiv(nb, MOBA_GROUP) + 1)]

    for groups, n_blocks in variants:
        @pl.when(n_groups == groups)
        def _(n_blocks=n_blocks):
            part = None
            for j in range(n_blocks):
                s = scores(j, jnp.where(j <= i, j, nb)) + causal_ref[(i == j).astype(jnp.int32)]
                s_ref[j] = s
                part = lane_halves_max(s) if part is None else jnp.maximum(part, lane_halves_max(s))
            rowmax = jnp.max(part, axis=1, keepdims=True)
            mrow_ref[...] = jnp.broadcast_to(rowmax, (n_rows, 2 * LANES))

    ones = jnp.ones((blk, LANES), BF16)

    def weighted(j):
        p = jnp.exp2(s_ref[j] - mrow_ref[...]).astype(BF16)
        vaug = jnp.concatenate([v_ref[j * blk:(j + 1) * blk, :], ones], axis=1)
        return jnp.dot(p, vaug, preferred_element_type=F32)

    next_tile = jnp.minimum(i + 1, nb - 1)
    for groups, n_blocks in variants:
        @pl.when(n_groups == groups)
        def _(n_blocks=n_blocks):
            build_query_operand(qn_ref[...], qtn_ref[...], next_tile, 1 - slot)
            tot = weighted(0)
            for j in range(1, n_blocks):
                tot = tot + weighted(j)
            acc_ref[...] = tot

    out = jnp.zeros((blk, LANES), F32)
    for hh in heads:
        acc = acc_ref[hh * blk:(hh + 1) * blk, :]
        out = jnp.where(hmasks[hh], acc[:, 0:LANES] / acc[:, LANES:2 * LANES], out)
    o_ref[...] = out


def _moba(q, qt, kt, v, kmean, key_table):
    b, s, a = q.shape
    nb = s // MOBA_BLOCK
    n_pairs = a // LANES
    rows = HEADS_PER_LANE_TILE * MOBA_BLOCK
    row_in_block = np.arange(rows)[:, None] % MOBA_BLOCK
    causal = np.where(row_in_block >= np.arange(MOBA_BLOCK)[None, :], 0.0, MASK_VALUE).astype(np.float32)
    causal_tiles = jnp.asarray(np.stack([np.zeros_like(causal), causal]))
    nxt = lambda i: jnp.minimum(i + 1, nb - 1)
    return pl.pallas_call(
        _moba_kernel,
        grid=(b, n_pairs, nb),
        in_specs=[
            pl.BlockSpec((None, MOBA_BLOCK, LANES), lambda bi, p, i: (bi, i, p)),
            pl.BlockSpec((None, LANES, MOBA_BLOCK), lambda bi, p, i: (bi * nb + i, p, 0)),
            pl.BlockSpec((None, MOBA_BLOCK, LANES), lambda bi, p, i: (bi, nxt(i), p)),
            pl.BlockSpec((None, LANES, MOBA_BLOCK), lambda bi, p, i: (bi * nb + nxt(i), p, 0)),
            pl.BlockSpec((nb, LANES, MOBA_BLOCK), lambda bi, p, i: (bi, p, 0)),
            pl.BlockSpec((None, s, LANES), lambda bi, p, i: (bi, 0, p)),
            pl.BlockSpec((None, nb, LANES), lambda bi, p, i: (bi, 0, p)),
            pl.BlockSpec((nb + 1, LANES, MOBA_BLOCK), lambda bi, p, i: (0, p, 0)),
            pl.BlockSpec((2, rows, MOBA_BLOCK), lambda bi, p, i: (0, 0, 0)),
        ],
        out_specs=pl.BlockSpec((None, MOBA_BLOCK, LANES), lambda bi, p, i: (bi, i, p)),
        out_shape=jax.ShapeDtypeStruct((b, s, a), F32),
        scratch_shapes=[
            pltpu.VMEM((2, rows, 2 * LANES), BF16),
            pltpu.VMEM((nb + 1, rows, MOBA_BLOCK), F32),
            pltpu.VMEM((rows, 2 * LANES), F32),
            pltpu.VMEM((rows, 2 * LANES), F32),
        ],
        compiler_params=pltpu.CompilerParams(
            dimension_semantics=("arbitrary", "arbitrary", "arbitrary"), vmem_limit_bytes=VMEM_LIMIT),
        name="moba",
    )(q, qt, q, qt, kt, v, kmean, key_table, causal_tiles)


def _mix_kernel(attn_ref, u_ref, vvn_ref, ga_ref, gs_ref, x_ref,
                wao_ref, wso_ref, wmo_ref, wsp_ref, bsp_ref, fg_ref, wr_ref, br_ref,
                h1_ref, xn2_ref, ri_ref, rw_ref, tcnt_ref, post_ref, wcausal_ref):
    tm = x_ref.shape[0]
    ch = SGU_CHUNK

    @pl.when(pl.program_id(0) == 0)
    def _():
        tril = lax.broadcasted_iota(jnp.int32, (ch, ch), 0) >= lax.broadcasted_iota(jnp.int32, (ch, ch), 1)
        for g in range(SGU_GROUPS):
            wcausal_ref[g] = jnp.where(tril, wsp_ref[g], 0.0).astype(BF16)

    y_attn = jnp.dot(attn_ref[...].astype(BF16), wao_ref[...], preferred_element_type=F32)

    lane = lax.broadcasted_iota(jnp.int32, (1, LANES), 1)
    gdim = vvn_ref.shape[1] // SGU_GROUPS
    groups_per_tile = LANES // gdim
    w_causal = [wcausal_ref[g] for g in range(SGU_GROUPS)]
    n_chunks = tm // ch
    lane_in_tile = lax.bitwise_and(lax.broadcasted_iota(jnp.int32, (1, n_chunks * LANES), 1), LANES - 1)
    col_blocks = []
    for ct in range(vvn_ref.shape[1] // LANES):
        vp = jnp.concatenate(
            [vvn_ref[c * ch:(c + 1) * ch, ct * LANES:(ct + 1) * LANES] for c in range(n_chunks)], axis=1)
        acc = jnp.zeros((ch, n_chunks * LANES), F32)
        for gg in range(groups_per_tile):
            gmask = (lane_in_tile >= gdim * gg) & (lane_in_tile < gdim * (gg + 1))
            vm = jnp.where(gmask, vp, 0.0).astype(BF16)
            acc = acc + jnp.dot(w_causal[ct * groups_per_tile + gg], vm, preferred_element_type=F32)
        col_blocks.append(acc)
    rows = [jnp.concatenate([blk[:, c * LANES:(c + 1) * LANES] for blk in col_blocks], axis=1) + bsp_ref[...]
            for c in range(n_chunks)]
    mixed = jnp.concatenate(rows, axis=0)
    sgu = u_ref[...] * mixed
    y_sgu = jnp.dot(sgu.astype(BF16), wso_ref[...], preferred_element_type=F32)

    merged = ga_ref[...] * y_attn + gs_ref[...] * y_sgu
    h1 = x_ref[...] + jnp.dot(merged.astype(BF16), wmo_ref[...], preferred_element_type=F32)
    h1_ref[...] = h1
    xn2 = h1 * lax.rsqrt(jnp.mean(h1 * h1, axis=-1, keepdims=True) + NORM_EPS) * fg_ref[...]
    xn2_ref[...] = xn2.astype(BF16)

    def split(a):
        hi = a.astype(BF16)
        return hi, (a - hi.astype(F32)).astype(BF16)

    x_hi, x_lo = split(xn2)
    w_hi, w_lo = split(wr_ref[...])
    hi_both = jnp.dot(x_hi, jnp.concatenate([w_hi, w_lo], axis=1), preferred_element_type=F32)
    logits = (hi_both[:, 0:LANES] + jnp.dot(x_lo, w_hi, preferred_element_type=F32)
              + hi_both[:, LANES:2 * LANES]) + br_ref[...]
    work = jnp.where(lane < N_EXPERTS, logits, NEG_INF)
    lane_f = lane.astype(F32)
    vals, idxs = [], []
    for _ in range(EXPERT_TOPK):
        vmax = jnp.max(work, axis=1, keepdims=True)
        first = jnp.min(jnp.where(work == vmax, lane_f, float(LANES)), axis=1, keepdims=True)
        vals.append(vmax)
        idxs.append(first)
        work = jnp.where(lane_f == first, NEG_INF, work)
    exps = [jnp.exp(v - vals[0]) for v in vals]
    denom = exps[0]
    for e in exps[1:]:
        denom = denom + e
    chosen = jnp.zeros((tm, LANES), F32)
    for first in idxs:
        chosen = jnp.where(lane_f == first, 1.0, chosen)

    strict_lower = (lax.broadcasted_iota(jnp.int32, (tm, tm), 0)
                    > lax.broadcasted_iota(jnp.int32, (tm, tm), 1))
    before = jnp.dot(jnp.where(strict_lower, 1.0, 0.0).astype(BF16), chosen.astype(BF16),
                     preferred_element_type=F32)
    counts = jnp.broadcast_to(jnp.sum(chosen, axis=0, keepdims=True), tcnt_ref.shape)
    lower_expert = (lax.broadcasted_iota(jnp.int32, (LANES, LANES), 0)
                    < lax.broadcasted_iota(jnp.int32, (LANES, LANES), 1))
    expert_start = jnp.dot(counts, jnp.where(lower_expert, 1.0, 0.0), precision=lax.Precision.HIGHEST,
                           preferred_element_type=F32)[0:1, :]
    sorted_pos = before + expert_start
    ri = jnp.zeros((tm, LANES), jnp.int32)
    rw = jnp.zeros((tm, LANES), F32)
    pos_lanes = jnp.zeros((tm, LANES), F32)
    for kk in range(EXPERT_TOPK):
        pos = jnp.sum(jnp.where(lane_f == idxs[kk], sorted_pos, 0.0), axis=1, keepdims=True)
        ri = jnp.where(lane == kk, idxs[kk].astype(jnp.int32), ri)
        ri = jnp.where(lane == EXPERT_TOPK + kk, pos.astype(jnp.int32), ri)
        rw = jnp.where(lane == kk, exps[kk] / denom, rw)
        pos_lanes = jnp.where(lane == kk, pos, pos_lanes)
    ri_ref[...] = ri
    rw_ref[...] = rw
    tcnt_ref[...] = counts
    post_ref[...] = pos_lanes.T[0:SUBLANES, :]


def _mix(attn, u, vvn, ga, gs, x2, wao, wso, wmo, wsp, bsp_full, fg, wr_pad, br_pad):
    t, d = x2.shape
    tm = ROUTE_TOKENS
    row = lambda w: pl.BlockSpec((tm, w), lambda i: (i, 0))
    const = lambda shape: pl.BlockSpec(shape, lambda i: (0,) * len(shape))
    ins = (attn, u, vvn, ga, gs, x2, wao, wso, wmo, wsp, bsp_full, fg, wr_pad, br_pad)
    in_specs = [row(a.shape[1]) for a in ins[:6]] + [const(a.shape) for a in ins[6:]]
    out_shape = (
        jax.ShapeDtypeStruct((t, d), F32),
        jax.ShapeDtypeStruct((t, d), BF16),
        jax.ShapeDtypeStruct((t, LANES), jnp.int32),
        jax.ShapeDtypeStruct((t, LANES), F32),
        jax.ShapeDtypeStruct((t // tm * SUBLANES, LANES), F32),
        jax.ShapeDtypeStruct((t // tm * SUBLANES, tm), F32),
    )
    assert d == SUBLANES * LANES, "a token row is stored as one (8, 128) tile"
    per_tile = lambda w: pl.BlockSpec((SUBLANES, w), lambda i: (i, 0))
    out_specs = (row(d), row(d), row(LANES), row(LANES), per_tile(LANES), per_tile(tm))
    return pl.pallas_call(
        _mix_kernel,
        grid=(t // tm,),
        in_specs=in_specs,
        out_specs=out_specs,
        out_shape=out_shape,
        scratch_shapes=[pltpu.VMEM(wsp.shape, BF16)],
        compiler_params=pltpu.CompilerParams(dimension_semantics=("arbitrary",), vmem_limit_bytes=BIG_VMEM_LIMIT),
        name="mix_route",
    )(*ins)


def _run_pieces(n, body):
    off = jnp.int32(0)
    for bit in reversed(range(ROUTE_TOKENS.bit_length())):
        size = 1 << bit
        take = lax.bitwise_and(n, size) != 0

        @pl.when(take)
        def _(off=off, size=size):
            body(off, size)

        off = off + jnp.where(take, size, 0)


def _token_rows(ref, r, n, rows_per_token=SUBLANES):
    return ref.at[pl.ds(pl.multiple_of(r * rows_per_token, rows_per_token), n * rows_per_token)]


def _pack_bf16_pairs(value):
    half = value.shape[1] // 2
    lo = lax.shift_right_logical(lax.bitcast_convert_type(value[:, :half], jnp.uint32), jnp.uint32(16))
    hi = lax.bitwise_and(lax.bitcast_convert_type(value[:, half:], jnp.uint32), jnp.uint32(0xFFFF0000))
    return lax.bitwise_or(lo, hi)


def _unpack_bf16_pairs(words):
    lo = lax.bitcast_convert_type(lax.shift_left(words, jnp.uint32(16)), F32)
    hi = lax.bitcast_convert_type(lax.bitwise_and(words, jnp.uint32(0xFFFF0000)), F32)
    return jnp.concatenate([lo, hi], axis=1).astype(BF16)


def _dispatch_kernel(len_ref, tpos_ref, dst_ref, padlo_ref, padlen_ref, nused_ref, post_ref, x_ref,
                     xs_hbm, sorted_ref, zeros_ref, sems, zsem):
    step = pl.program_id(0)
    n_steps = pl.num_programs(0)
    tt = x_ref.shape[0]
    n_assign = tt * EXPERT_TOPK
    rpt = PACKED_ROWS
    tile_rows = zeros_ref.shape[0] // rpt
    slot = lax.rem(step, 2)

    def start_runs(tile, buf):
        def per_expert(e, carry):
            k = tile * N_EXPERTS + e

            def piece(off, size):
                pltpu.make_async_copy(_token_rows(sorted_ref.at[buf], tpos_ref[k] + off, size, rpt),
                                      _token_rows(xs_hbm, dst_ref[k] + off, size, rpt), sems.at[buf]).start()

            _run_pieces(len_ref[k], piece)
            return carry

        lax.fori_loop(0, N_EXPERTS, per_expert, 0)

    def wait_runs(buf):
        whole = sorted_ref.at[buf]
        pltpu.make_async_copy(whole, _token_rows(xs_hbm, 0, n_assign, rpt), sems.at[buf]).wait()

    @pl.when(step == 0)
    def _():
        zeros_ref[...] = jnp.zeros_like(zeros_ref)

        def zero_fill(wait):
            def finish(cp):
                if wait:
                    cp.wait()
                else:
                    cp.start()

            def tail_tile(tl, carry):
                finish(pltpu.make_async_copy(zeros_ref, _token_rows(xs_hbm, tl * tile_rows, tile_rows, rpt), zsem))
                return carry

            lax.fori_loop(nused_ref[0], xs_hbm.shape[0] // (tile_rows * rpt), tail_tile, 0)

            def per_expert(e, carry):
                def piece(off, size):
                    finish(pltpu.make_async_copy(_token_rows(zeros_ref, 0, size, rpt),
                                                 _token_rows(xs_hbm, padlo_ref[e] + off, size, rpt), zsem))

                _run_pieces(padlen_ref[e], piece)
                return carry

            lax.fori_loop(0, N_EXPERTS, per_expert, 0)

        zero_fill(wait=False)
        zero_fill(wait=True)

    @pl.when(step >= 2)
    def _():
        wait_runs(slot)

    x = x_ref[...]
    pos_by_choice = [post_ref[kk:kk + 1, :] for kk in range(EXPERT_TOPK)]
    chunk = MOBA_BLOCK
    for rc in range(n_assign // chunk):
        r = (lax.broadcasted_iota(jnp.int32, (chunk, 1), 0) + rc * chunk).astype(F32)
        select = jnp.zeros((chunk, tt), F32)
        for pos in pos_by_choice:
            select = jnp.where(r == pos, 1.0, select)
        rows = jnp.dot(select.astype(BF16), x, preferred_element_type=F32)
        words = _pack_bf16_pairs(rows)
        for c in range(rpt):
            sorted_ref[slot, pl.ds(rc * chunk * rpt + c, chunk, stride=rpt), :] = words[:, c * LANES:(c + 1) * LANES]
    start_runs(step, slot)

    @pl.when(step == n_steps - 1)
    def _():
        @pl.when(step >= 1)
        def _():
            wait_runs(1 - slot)

        wait_runs(slot)


def _dispatch(run_len, run_tpos, run_dst, padlo, padlen, n_used, pos_t, xn2, n_rows):
    t, d = xn2.shape
    n_assign = ROUTE_TOKENS * EXPERT_TOPK
    grid_spec = pltpu.PrefetchScalarGridSpec(
        num_scalar_prefetch=6,
        grid=(t // ROUTE_TOKENS,),
        in_specs=[pl.BlockSpec((SUBLANES, ROUTE_TOKENS), lambda c, *_: (c, 0)),
                  pl.BlockSpec((ROUTE_TOKENS, d), lambda c, *_: (c, 0))],
        out_specs=pl.BlockSpec(memory_space=pl.ANY),
        scratch_shapes=[
            pltpu.VMEM((2, n_assign * PACKED_ROWS, LANES), jnp.uint32),
            pltpu.VMEM((EXPERT_ROWS * PACKED_ROWS, LANES), jnp.uint32),
            pltpu.SemaphoreType.DMA((2,)),
            pltpu.SemaphoreType.DMA(()),
        ],
    )
    assert d == 2 * PACKED_ROWS * LANES
    return pl.pallas_call(
        _dispatch_kernel,
        grid_spec=grid_spec,
        out_shape=jax.ShapeDtypeStruct((n_rows * PACKED_ROWS, LANES), jnp.uint32),
        compiler_params=pltpu.CompilerParams(dimension_semantics=("arbitrary",), vmem_limit_bytes=VMEM_LIMIT),
        name="dispatch",
    )(run_len, run_tpos, run_dst, padlo, padlen, n_used, pos_t, xn2)


def _expert_kernel(be_ref, nused_ref, x_ref, wgu_ref, bg_ref, bu_ref, wd_ref, bd_ref, y_ref, wgu_bf, wd_bf):
    t = pl.program_id(0)
    rows = y_ref.shape[0] // SUBLANES
    de = wd_ref.shape[0]
    grp = 2 * LANES
    active = t < nused_ref[0]
    fresh = jnp.logical_and(
        active, jnp.logical_or(t == 0, be_ref[t] != be_ref[jnp.maximum(t - 1, 0)]))

    @pl.when(fresh)
    def _():
        src = lax.broadcasted_iota(jnp.int32, (grp, grp), 0)
        dst = lax.broadcasted_iota(jnp.int32, (grp, grp), 1)
        wanted = jnp.where(dst < LANES, 2 * dst, 2 * (dst - LANES) + 1)
        perm = jnp.where(src == wanted, 1.0, 0.0).astype(BF16)
        for g in range(wgu_ref.shape[1] // grp):
            cols = slice(g * grp, (g + 1) * grp)
            wgu_bf[:, cols] = jnp.dot(wgu_ref[:, cols].astype(BF16), perm,
                                      preferred_element_type=F32).astype(BF16)
        wd_bf[...] = wd_ref[...].astype(BF16)

    @pl.when(active)
    def _():
        words = jnp.concatenate([x_ref[pl.ds(c, rows, stride=PACKED_ROWS), :] for c in range(PACKED_ROWS)], axis=1)
        xb = _unpack_bf16_pairs(words)
        gu = jnp.dot(xb, wgu_bf[...], preferred_element_type=F32)
        hid = []
        for g in range(de // LANES):
            cols = slice(g * LANES, (g + 1) * LANES)
            gate = gu[:, g * grp:g * grp + LANES] + bg_ref[:, cols]
            up = gu[:, g * grp + LANES:(g + 1) * grp] + bu_ref[:, cols]
            gate = jnp.minimum(gate, SWIGLU_LIMIT)
            up = jnp.clip(up, -SWIGLU_LIMIT, SWIGLU_LIMIT)
            hid.append((gate * _sigmoid(SWIGLU_ALPHA * gate) * (up + 1.0)).astype(BF16))
        y = jnp.dot(jnp.concatenate(hid, axis=1), wd_bf[...], preferred_element_type=F32) + bd_ref[...]
        _store_token_rows(y_ref, y)

    @pl.when(jnp.logical_not(active))
    def _():
        y_ref[...] = jnp.zeros_like(y_ref)


def _experts(block_expert, n_used, xs_rows, wgu, bg, bu, wd, bd):
    n_rows = xs_rows.shape[0] // PACKED_ROWS
    _, d, de2 = wgu.shape
    de = wd.shape[1]
    n_tiles = n_rows // EXPERT_ROWS
    tile_rows = EXPERT_ROWS * SUBLANES
    per_expert = lambda t, be, nu: (be[t], 0, 0)
    grid_spec = pltpu.PrefetchScalarGridSpec(
        num_scalar_prefetch=2,
        grid=(n_tiles,),
        in_specs=[
            pl.BlockSpec((EXPERT_ROWS * PACKED_ROWS, LANES),
                         lambda t, be, nu: (jnp.minimum(t, jnp.maximum(nu[0] - 1, 0)), 0)),
            pl.BlockSpec((None, d, de2), per_expert),
            pl.BlockSpec((None, 1, de), per_expert),
            pl.BlockSpec((None, 1, de), per_expert),
            pl.BlockSpec((None, de, d), per_expert),
            pl.BlockSpec((None, 1, d), per_expert),
        ],
        out_specs=pl.BlockSpec((tile_rows, LANES), lambda t, be, nu: (t, 0)),
        scratch_shapes=[pltpu.VMEM((d, de2), BF16), pltpu.VMEM((de, d), BF16)],
    )
    return pl.pallas_call(
        _expert_kernel,
        grid_spec=grid_spec,
        out_shape=jax.ShapeDtypeStruct((n_rows * SUBLANES, LANES), F32),
        compiler_params=pltpu.CompilerParams(
            dimension_semantics=("arbitrary",), vmem_limit_bytes=BIG_VMEM_LIMIT),
        name="experts",
    )(block_expert, n_used, xs_rows, wgu, bg, bu, wd, bd)


def _combine_kernel(len_ref, tpos_ref, src_ref, row_ref, w_ref, h1_ref, fg_ref, ys_hbm,
                    o_ref, buf_ref, h1rows_ref, outrows_ref, sems):
    step = pl.program_id(0)
    n_steps = pl.num_programs(0)
    tt = h1_ref.shape[0]
    slot = lax.rem(step, 2)

    def start_runs(tile, buf):
        def per_expert(e, carry):
            k = tile * N_EXPERTS + e

            def piece(off, size):
                pltpu.make_async_copy(_token_rows(ys_hbm, src_ref[k] + off, size),
                                      _token_rows(buf_ref.at[buf], tpos_ref[k] + off, size), sems.at[buf]).start()

            _run_pieces(len_ref[k], piece)
            return carry

        lax.fori_loop(0, N_EXPERTS, per_expert, 0)

    def wait_runs(buf):
        whole = buf_ref.at[buf]
        pltpu.make_async_copy(_token_rows(ys_hbm, 0, tt * EXPERT_TOPK), whole, sems.at[buf]).wait()

    @pl.when(step == 0)
    def _():
        start_runs(0, 0)

    @pl.when(step + 1 < n_steps)
    def _():
        start_runs(step + 1, 1 - slot)

    wait_runs(slot)

    _store_token_rows(h1rows_ref, h1_ref[...])

    def one_token(tok, carry):
        acc = h1rows_ref[pl.ds(pl.multiple_of(tok * SUBLANES, SUBLANES), SUBLANES), :]
        for kk in range(EXPERT_TOPK):
            a = tok * EXPERT_TOPK + kk
            acc = acc + w_ref[a] * buf_ref[slot, pl.ds(pl.multiple_of(row_ref[a], SUBLANES), SUBLANES), :]
        outrows_ref[pl.ds(pl.multiple_of(tok * SUBLANES, SUBLANES), SUBLANES), :] = acc
        return carry

    lax.fori_loop(0, tt, one_token, 0, unroll=32)
    h = _load_token_rows(outrows_ref, tt)
    o_ref[...] = h * lax.rsqrt(jnp.mean(h * h, axis=-1, keepdims=True) + NORM_EPS) * fg_ref[...]


def _combine(run_len, run_tpos, run_src, buf_row, w_flat, h1, fg, ys_rows):
    t, d = h1.shape
    tt = ROUTE_TOKENS
    n_assign = tt * EXPERT_TOPK
    smem_blk = pl.BlockSpec((n_assign,), lambda c, *_: (c,), memory_space=pltpu.SMEM)
    grid_spec = pltpu.PrefetchScalarGridSpec(
        num_scalar_prefetch=3,
        grid=(t // tt,),
        in_specs=[
            smem_blk, smem_blk,
            pl.BlockSpec((tt, d), lambda c, *_: (c, 0)),
            pl.BlockSpec((1, d), lambda c, *_: (0, 0)),
            pl.BlockSpec(memory_space=pl.ANY),
        ],
        out_specs=pl.BlockSpec((tt, d), lambda c, *_: (c, 0)),
        scratch_shapes=[
            pltpu.VMEM((2, n_assign * SUBLANES, LANES), F32),
            pltpu.VMEM((tt * SUBLANES, LANES), F32),
            pltpu.VMEM((tt * SUBLANES, LANES), F32),
            pltpu.SemaphoreType.DMA((2,)),
        ],
    )
    return pl.pallas_call(
        _combine_kernel,
        grid_spec=grid_spec,
        out_shape=jax.ShapeDtypeStruct((t, d), F32),
        compiler_params=pltpu.CompilerParams(dimension_semantics=("arbitrary",), vmem_limit_bytes=VMEM_LIMIT),
        name="combine",
    )(run_len, run_tpos, run_src, buf_row, w_flat, h1, fg, ys_rows)


def _layer(h, mix_norm_g, w_in, w_attn_out, sgu_ln_g, sgu_ln_b, w_spatial, b_spatial, w_sgu_out,
           w_mix_out, ffn_norm_g, w_router, b_router, w_gate_up, b_gate_up, w_down, b_down, out_g):
    b, s, d = h.shape
    t = b * s
    sgu_width = sgu_ln_g.shape[0]
    x2 = h.reshape(t, d)

    q, qt, kt, v, kmean, u, vvn, ga, gs = _in_proj(
        x2, mix_norm_g.reshape(1, d), w_in.astype(BF16), sgu_ln_g.reshape(1, sgu_width),
        sgu_ln_b.reshape(1, sgu_width), sgu_width)

    nb = s // MOBA_BLOCK
    attn = _moba(q.reshape(b, s, ATTN_WIDTH), qt, kt, v.reshape(b, s, ATTN_WIDTH),
                 kmean.reshape(b, nb, ATTN_WIDTH), _moba_key_table(s)).reshape(t, ATTN_WIDTH)

    gdim = sgu_width // SGU_GROUPS
    bsp_full = jnp.repeat(b_spatial.T, gdim, axis=1)
    wr_pad = jnp.zeros((d, LANES), F32).at[:, :N_EXPERTS].set(w_router)
    br_pad = jnp.zeros((1, LANES), F32).at[0, :N_EXPERTS].set(b_router)
    h1, xn2, ri, rw, tile_cnt, pos_t = _mix(
        attn, u, vvn, ga, gs, x2, w_attn_out.astype(BF16), w_sgu_out.astype(BF16), w_mix_out.astype(BF16),
        w_spatial, bsp_full, ffn_norm_g.reshape(1, d), wr_pad, br_pad)

    n_assign = t * EXPERT_TOPK
    n_tiles = -(-(n_assign + N_EXPERTS * (EXPERT_ROWS - 1)) // EXPERT_ROWS)
    n_rows = n_tiles * EXPERT_ROWS
    run_len = tile_cnt[::SUBLANES, :N_EXPERTS].astype(jnp.int32)
    counts = jnp.sum(run_len, axis=0)
    padded = (counts + EXPERT_ROWS - 1) // EXPERT_ROWS * EXPERT_ROWS
    pad_end = jnp.cumsum(padded)
    starts = pad_end - padded
    run_tpos = jnp.cumsum(run_len, axis=1) - run_len
    run_row = starts[None, :] + jnp.cumsum(run_len, axis=0) - run_len
    tile_start = jnp.arange(n_tiles, dtype=jnp.int32) * EXPERT_ROWS
    block_expert = jnp.minimum(
        jnp.sum((pad_end[None, :] <= tile_start[:, None]).astype(jnp.int32), axis=1), N_EXPERTS - 1)
    n_used = (pad_end[-1:] // EXPERT_ROWS).astype(jnp.int32)
    buf_row = ri[:, EXPERT_TOPK:2 * EXPERT_TOPK].reshape(n_assign) * SUBLANES
    tables = (run_len.reshape(-1), run_tpos.reshape(-1), run_row.reshape(-1))

    xs_rows = _dispatch(*tables, starts + counts, padded - counts, n_used, pos_t, xn2, n_rows)

    de = w_down.shape[1]
    bg = b_gate_up[:, 0::2].reshape(N_EXPERTS, 1, de)
    bu = b_gate_up[:, 1::2].reshape(N_EXPERTS, 1, de)
    ys_rows = _experts(block_expert, n_used, xs_rows, w_gate_up, bg, bu, w_down, b_down.reshape(N_EXPERTS, 1, d))

    out = _combine(*tables, buf_row, rw[:, :EXPERT_TOPK].reshape(n_assign), h1, out_g.reshape(1, d), ys_rows)
    return out.reshape(b, s, d)


def kernel(x, mix_norm_g, w_in, w_attn_out, sgu_ln_g, sgu_ln_b, w_spatial, b_spatial, w_sgu_out, w_mix_out,
           ffn_norm_g, w_router, b_router, w_gate_up, b_gate_up, w_down, b_down, final_norm_g):
    depth = w_in.shape[0]
    assert depth == 1, "the final RMSNorm is fused into the single layer's combine step"
    return _layer(x, mix_norm_g[0], w_in[0], w_attn_out[0], sgu_ln_g[0], sgu_ln_b[0], w_spatial[0],
                  b_spatial[0], w_sgu_out[0], w_mix_out[0], ffn_norm_g[0], w_router[0], b_router[0],
                  w_gate_up[0], b_gate_up[0], w_down[0], b_down[0], final_norm_g)
```
